```python
import math
import jax, jax.numpy as jnp
from jax import lax
import numpy as np

D_MODEL = 1024
BATCH = 8
SEQ = 2048
DEPTH = 1

HEAD_DIM = 64
N_DIFF_HEADS = 4
DIFF_V_DIM = 2 * HEAD_DIM
N_FOX_HEADS = 8
DIFF_WIDTH = N_DIFF_HEADS * DIFF_V_DIM
FOX_WIDTH = N_FOX_HEADS * HEAD_DIM
MIX_WIDTH = DIFF_WIDTH + FOX_WIDTH
ROT_DIM = HEAD_DIM // 4
ROPE_THETA = 500000.0
Q_BLOCK = 128
DQ_W = N_DIFF_HEADS * 2 * HEAD_DIM
DK_W = N_DIFF_HEADS * 2 * HEAD_DIM
DV_W = DIFF_WIDTH
FQ_W = FOX_WIDTH
FK_W = FOX_WIDTH
FV_W = FOX_WIDTH
FF_W = N_FOX_HEADS
IN_WIDTH = DQ_W + DK_W + DV_W + FQ_W + FK_W + FV_W + FF_W
IN_SPLITS = [DQ_W, DQ_W + DK_W, DQ_W + DK_W + DV_W, DQ_W + DK_W + DV_W + FQ_W,
             DQ_W + DK_W + DV_W + FQ_W + FK_W, DQ_W + DK_W + DV_W + FQ_W + FK_W + FV_W]
N_EXPERTS = 32
TOP_K = 4
D_EXPERT = D_MODEL
SWIGLU_ALPHA = 1.702
SWIGLU_LIMIT = 7.0
MOE_BLOCK = 128
PLE_DIM = 256
NORM_EPS = 1e-5

kernel_name = "hymba_style_diff_fox_moe_block"


def rms_norm(x, g):
    xf = x.astype(jnp.float32)
    y = xf * lax.rsqrt(jnp.mean(xf * xf, axis=-1, keepdims=True) + NORM_EPS)
    return (y * g.astype(jnp.float32)).astype(x.dtype)


def partial_rope(x, positions):
    inv_freq = ROPE_THETA ** (-jnp.arange(0, ROT_DIM, 2, dtype=jnp.float32) / ROT_DIM)
    ang = positions.astype(jnp.float32)[:, :, None] * inv_freq
    ang = ang.reshape(ang.shape[:2] + (1,) * (x.ndim - 3) + (ROT_DIM // 2,))
    cos, sin = jnp.cos(ang), jnp.sin(ang)
    xf = x.astype(jnp.float32)
    x1 = xf[..., :ROT_DIM // 2]
    x2 = xf[..., ROT_DIM // 2:ROT_DIM]
    out = jnp.concatenate([x1 * cos - x2 * sin, x2 * cos + x1 * sin, xf[..., ROT_DIM:]], axis=-1)
    return out.astype(x.dtype)


def diff_attention(q, k, v, lam):
    b, s, h, _, dh = q.shape
    nb = s // Q_BLOCK
    q = q.transpose(0, 2, 3, 1, 4)
    k = k.transpose(0, 2, 3, 1, 4)
    v = v.transpose(0, 2, 1, 3)
    qb = jnp.moveaxis(q.reshape(b, h, 2, nb, Q_BLOCK, dh), 3, 0)
    scale = dh ** -0.5
    k_pos = jnp.arange(s)

    def block(args):
        q_blk, qi = args
        sc = jnp.einsum('bhmqd,bhmkd->bhmqk', q_blk, k).astype(jnp.float32) * scale
        mask = k_pos[None, :] <= (qi * Q_BLOCK + jnp.arange(Q_BLOCK))[:, None]
        a = jax.nn.softmax(jnp.where(mask, sc, -jnp.inf), axis=-1)
        a = a[:, :, 0] - lam * a[:, :, 1]
        return jnp.einsum('bhqk,bhkd->bhqd', a.astype(v.dtype), v)

    o = lax.map(block, (qb, jnp.arange(nb)))
    return jnp.moveaxis(o, 0, 2).reshape(b, h, s, -1).transpose(0, 2, 1, 3)


def forgetting_attention(q, k, v, c):
    b, s, h, dh = q.shape
    nb = s // Q_BLOCK
    q = q.transpose(0, 2, 1, 3)
    k = k.transpose(0, 2, 1, 3)
    v = v.transpose(0, 2, 1, 3)
    c = c.transpose(0, 2, 1)
    qb = jnp.moveaxis(q.reshape(b, h, nb, Q_BLOCK, dh), 2, 0)
    cb = jnp.moveaxis(c.reshape(b, h, nb, Q_BLOCK), 2, 0)
    scale = dh ** -0.5
    k_pos = jnp.arange(s)

    def block(args):
        q_blk, c_blk, qi = args
        sc = jnp.einsum('bhqd,bhkd->bhqk', q_blk, k).astype(jnp.float32) * scale
        sc = sc + c_blk[..., :, None] - c[..., None, :]
        mask = k_pos[None, :] <= (qi * Q_BLOCK + jnp.arange(Q_BLOCK))[:, None]
        a = jax.nn.softmax(jnp.where(mask, sc, -jnp.inf), axis=-1)
        return jnp.einsum('bhqk,bhkd->bhqd', a.astype(v.dtype), v)

    o = lax.map(block, (qb, cb, jnp.arange(nb)))
    return jnp.moveaxis(o, 0, 2).reshape(b, h, s, dh).transpose(0, 2, 1, 3)


def hybrid_mixer(xn, positions, w_in, lq1, lk1, lq2, lk2, g_subln, b_forget, w_out, lambda_init):
    b, s, _ = xn.shape
    proj = xn @ w_in
    dq, dk, dv, fq, fk, fv, fz = jnp.split(proj, IN_SPLITS, axis=-1)
    dq = partial_rope(dq.reshape(b, s, N_DIFF_HEADS, 2, HEAD_DIM), positions)
    dk = partial_rope(dk.reshape(b, s, N_DIFF_HEADS, 2, HEAD_DIM), positions)
    dv = dv.reshape(b, s, N_DIFF_HEADS, DIFF_V_DIM)
    f32 = jnp.float32
    lam = (jnp.exp(jnp.sum(lq1.astype(f32) * lk1.astype(f32)))
           - jnp.exp(jnp.sum(lq2.astype(f32) * lk2.astype(f32))) + lambda_init)
    d_out = diff_attention(dq, dk, dv, lam)
    d_out = rms_norm(d_out, g_subln) * (1.0 - lambda_init)
    logf = jax.nn.log_sigmoid(fz.astype(f32) + b_forget.astype(f32))
    c = jnp.cumsum(logf, axis=1)
    f_out = forgetting_attention(fq.reshape(b, s, N_FOX_HEADS, HEAD_DIM),
                                 fk.reshape(b, s, N_FOX_HEADS, HEAD_DIM),
                                 fv.reshape(b, s, N_FOX_HEADS, HEAD_DIM), c)
    mixed = jnp.concatenate([d_out.reshape(b, s, DIFF_WIDTH), f_out.reshape(b, s, FOX_WIDTH)], axis=-1)
    return mixed @ w_out


def moe(xn, w_router, b_router, w1, b1, w2, b2):
    b, s, d = xn.shape
    n = b * s
    xf = xn.reshape(n, d)
    logits = xf.astype(jnp.float32) @ w_router.astype(jnp.float32) + b_router.astype(jnp.float32)
    top_vals, top_idx = lax.top_k(logits, TOP_K)
    gates = jax.nn.softmax(top_vals, axis=-1)
    a = n * TOP_K
    flat_e = top_idx.reshape(a).astype(jnp.int32)
    flat_tok = jnp.arange(a, dtype=jnp.int32) // TOP_K
    flat_w = gates.reshape(a)
    counts = jnp.zeros((N_EXPERTS,), jnp.int32).at[flat_e].add(1)
    padded_counts = (counts + MOE_BLOCK - 1) // MOE_BLOCK * MOE_BLOCK
    padded_ends = jnp.cumsum(padded_counts)
    padded_starts = padded_ends - padded_counts
    starts = jnp.cumsum(counts) - counts
    order = jnp.argsort(flat_e)
    sorted_e = flat_e[order]
    dest = padded_starts[sorted_e] + (jnp.arange(a, dtype=jnp.int32) - starts[sorted_e])
    n_rows = a + N_EXPERTS * MOE_BLOCK
    n_blocks = n_rows // MOE_BLOCK
    row_tok = jnp.zeros((n_rows,), jnp.int32).at[dest].set(flat_tok[order])
    row_w = jnp.zeros((n_rows,), jnp.float32).at[dest].set(flat_w[order])
    block_e = jnp.clip(jnp.searchsorted(padded_ends, jnp.arange(n_blocks) * MOE_BLOCK, side='right'),
                       0, N_EXPERTS - 1).astype(jnp.int32)
    x_rows = xf[row_tok].reshape(n_blocks, MOE_BLOCK, d)

    def expert_block(args):
        xb, e = args
        h = xb @ w1[e] + b1[e]
        glu = jnp.minimum(h[..., ::2], SWIGLU_LIMIT)
        lin = jnp.clip(h[..., 1::2], -SWIGLU_LIMIT, SWIGLU_LIMIT)
        act = glu * jax.nn.sigmoid(SWIGLU_ALPHA * glu) * (lin + 1.0)
        return act @ w2[e] + b2[e]

    y = lax.map(expert_block, (x_rows, block_e)).reshape(n_rows, d)
    out = jnp.zeros((n, d), y.dtype).at[row_tok].add(y * row_w.astype(y.dtype)[:, None])
    return out.reshape(b, s, d)


def setup_inputs(seed: int = 0) -> dict:
    key = jax.random.key(seed)
    ks = jax.random.split(key, 24)
    nrm = jax.random.normal
    L, D, E, I = DEPTH, D_MODEL, N_EXPERTS, D_EXPERT
    return {
        "x": nrm(ks[0], (BATCH, SEQ, D), jnp.float32),
        "p": nrm(ks[1], (L, BATCH, SEQ, PLE_DIM), jnp.float32),
        "positions": (jnp.arange(SEQ, dtype=jnp.int32)[None, :]
                      + jax.random.randint(ks[2], (BATCH, 1), 0, 4096, dtype=jnp.int32)),
        "g_attn": 1.0 + 0.02 * nrm(ks[3], (L, D), jnp.float32),
        "w_in": nrm(ks[4], (L, D, IN_WIDTH), jnp.float32) * D ** -0.5,
        "lambda_q1": 0.1 * nrm(ks[5], (L, HEAD_DIM), jnp.float32),
        "lambda_k1": 0.1 * nrm(ks[6], (L, HEAD_DIM), jnp.float32),
        "lambda_q2": 0.1 * nrm(ks[7], (L, HEAD_DIM), jnp.float32),
        "lambda_k2": 0.1 * nrm(ks[8], (L, HEAD_DIM), jnp.float32),
        "g_subln": 1.0 + 0.02 * nrm(ks[9], (L, DIFF_V_DIM), jnp.float32),
        "b_forget": 2.0 + 0.5 * nrm(ks[10], (L, N_FOX_HEADS), jnp.float32),
        "w_out": nrm(ks[11], (L, MIX_WIDTH, D), jnp.float32) * MIX_WIDTH ** -0.5,
        "g_moe": 1.0 + 0.02 * nrm(ks[12], (L, D), jnp.float32),
        "w_router": nrm(ks[13], (L, D, E), jnp.float32) * D ** -0.5,
        "b_router": 0.01 * nrm(ks[14], (L, E), jnp.float32),
        "w_e1": nrm(ks[15], (L, E, D, 2 * I), jnp.float32) * D ** -0.5,
        "b_e1": 0.01 * nrm(ks[16], (L, E, 2 * I), jnp.float32),
        "w_e2": nrm(ks[17], (L, E, I, D), jnp.float32) * I ** -0.5,
        "b_e2": 0.01 * nrm(ks[18], (L, E, D), jnp.float32),
        "g_ple": 1.0 + 0.02 * nrm(ks[19], (L, D), jnp.float32),
        "w_ple_gate": nrm(ks[20], (L, D, D), jnp.float32) * D ** -0.5,
        "w_ple_proj": nrm(ks[21], (L, PLE_DIM, D), jnp.float32) * PLE_DIM ** -0.5,
        "g_final": 1.0 + 0.02 * nrm(ks[22], (D,), jnp.float32),
    }


def reference(x, p, positions, g_attn, w_in, lambda_q1, lambda_k1, lambda_q2, lambda_k2,
              g_subln, b_forget, w_out, g_moe, w_router, b_router, w_e1, b_e1, w_e2, b_e2,
              g_ple, w_ple_gate, w_ple_proj, g_final):
    h = x
    for i in range(DEPTH):
        lambda_init = 0.8 - 0.6 * math.exp(-0.3 * i)
        h = h + hybrid_mixer(rms_norm(h, g_attn[i]), positions, w_in[i],
                             lambda_q1[i], lambda_k1[i], lambda_q2[i], lambda_k2[i],
                             g_subln[i], b_forget[i], w_out[i], lambda_init)
        h = h + moe(rms_norm(h, g_moe[i]), w_router[i], b_router[i],
                    w_e1[i], b_e1[i], w_e2[i], b_e2[i])
        gate = jax.nn.sigmoid(rms_norm(h, g_ple[i]) @ w_ple_gate[i])
        h = h + gate * (p[i] @ w_ple_proj[i])
    return rms_norm(h, g_final)
```

```python
import functools
import math

import jax
import jax.numpy as jnp
from jax import lax
from jax.experimental import pallas as pl
from jax.experimental.pallas import tpu as pltpu

F32 = jnp.float32
BF16 = jnp.bfloat16
I32 = jnp.int32

LANES = 128
D_MODEL = 1024
HEAD_DIM = 64
N_DIFF_HEADS = 4
DIFF_V_DIM = 2 * HEAD_DIM
N_FOX_HEADS = 8
DIFF_WIDTH = N_DIFF_HEADS * DIFF_V_DIM
FOX_WIDTH = N_FOX_HEADS * HEAD_DIM
ROT_DIM = HEAD_DIM // 4
ROPE_THETA = 500000.0
N_EXPERTS = 32
TOP_K = 4
D_EXPERT = D_MODEL
SWIGLU_ALPHA = 1.702
SWIGLU_LIMIT = 7.0
PLE_DIM = 256
NORM_EPS = 1e-5
MAIN_WIDTH = 3 * DIFF_WIDTH + 3 * FOX_WIDTH

SEQ_TILE = 256
ATTN_TILE = 256
TOKEN_TILE = 256
EXPERT_TILE = 256
VMEM_LIMIT = 48 * 1024 * 1024


def _params(*semantics):
    return pltpu.CompilerParams(dimension_semantics=semantics, vmem_limit_bytes=VMEM_LIMIT)


def _rms(x, g):
    return x * lax.rsqrt(jnp.mean(x * x, axis=-1, keepdims=True) + NORM_EPS) * g


def _rope_table_kernel(pos_ref, cos_ref, sin_ref):
    pos = pos_ref[0].astype(F32)
    d = lax.broadcasted_iota(I32, (1, LANES), 1) % HEAD_DIM
    j = d % (ROT_DIM // 2)
    inv_freq = jnp.power(ROPE_THETA, -(2 * j).astype(F32) / ROT_DIM)
    ang = pos * inv_freq
    c, s = jnp.cos(ang), jnp.sin(ang)
    cos_ref[0] = jnp.where(d < ROT_DIM, c, 1.0)
    sin_ref[0] = jnp.where(d < ROT_DIM // 2, -s, jnp.where(d < ROT_DIM, s, 0.0))


def _rope_tables(positions):
    b, s = positions.shape
    ts = SEQ_TILE
    return pl.pallas_call(
        _rope_table_kernel,
        grid=(b, s // ts),
        in_specs=[pl.BlockSpec((1, ts, 1), lambda i, j: (i, j, 0))],
        out_specs=[pl.BlockSpec((1, ts, LANES), lambda i, j: (i, j, 0))] * 2,
        out_shape=[jax.ShapeDtypeStruct((b, s, LANES), F32)] * 2,
        compiler_params=_params("arbitrary", "arbitrary"),
        name="rope_tables",
    )(positions.reshape(b, s, 1))


def _inproj_kernel(x_ref, g_ref, w_ref, wfz_ref, bfz_ref, cos_ref, sin_ref,
                   dq_ref, dk_ref, dv_ref, fq_ref, fk_ref, fv_ref, c_ref, carry_ref):
    ts = x_ref.shape[1]
    xb = _rms(x_ref[0], g_ref[...]).astype(BF16)
    proj = jnp.dot(xb, w_ref[...], preferred_element_type=F32)

    cosf, sinf = cos_ref[0], sin_ref[0]
    first_half = lax.broadcasted_iota(I32, (1, LANES), 1) % HEAD_DIM < ROT_DIM // 2

    def rope(t):
        outs = []
        for j in range(t.shape[1] // LANES):
            tj = t[:, j * LANES:(j + 1) * LANES]
            partner = jnp.where(first_half,
                                pltpu.roll(tj, LANES - ROT_DIM // 2, 1),
                                pltpu.roll(tj, ROT_DIM // 2, 1))
            outs.append(tj * cosf + partner * sinf)
        return jnp.concatenate(outs, axis=1)

    scale = HEAD_DIM ** -0.5
    w = DIFF_WIDTH
    dq_ref[0] = (rope(proj[:, 0:w]) * scale).astype(BF16)
    dk_ref[0] = rope(proj[:, w:2 * w]).astype(BF16)
    dv_ref[0] = proj[:, 2 * w:3 * w].astype(BF16)
    fq_ref[0] = (proj[:, 3 * w:4 * w] * scale).astype(BF16)
    fk_ref[0] = proj[:, 4 * w:5 * w].astype(BF16)
    fv_ref[0] = proj[:, 5 * w:6 * w].astype(BF16)

    fz = jnp.dot(xb, wfz_ref[...], preferred_element_type=F32) + bfz_ref[...]
    logf = jnp.minimum(fz, 0.0) - jnp.log1p(jnp.exp(-jnp.abs(fz)))
    row = lax.broadcasted_iota(I32, (ts, ts), 0)
    col = lax.broadcasted_iota(I32, (ts, ts), 1)
    tri = jnp.where(col <= row, 1.0, 0.0).astype(BF16)
    hi = logf.astype(BF16)
    r1 = logf - hi.astype(F32)
    mid = r1.astype(BF16)
    lo = (r1 - mid.astype(F32)).astype(BF16)
    cum = (jnp.dot(tri, hi, preferred_element_type=F32)
           + jnp.dot(tri, mid, preferred_element_type=F32)
           + jnp.dot(tri, lo, preferred_element_type=F32))

    @pl.when(pl.program_id(1) == 0)
    def _():
        carry_ref[...] = jnp.zeros_like(carry_ref)

    c = cum + carry_ref[...]
    c_ref[0] = c
    carry_ref[...] = c[ts - 1:ts, :]


def _inproj(x, g, w_main, w_fz, b_fz, cosf, sinf):
    b, s, d = x.shape
    ts = SEQ_TILE
    tile = lambda width: pl.BlockSpec((1, ts, width), lambda i, j: (i, j, 0))
    full = lambda arr: pl.BlockSpec(arr.shape, lambda i, j: (0,) * arr.ndim)
    heads = jax.ShapeDtypeStruct((b, s, DIFF_WIDTH), BF16)
    return pl.pallas_call(
        _inproj_kernel,
        grid=(b, s // ts),
        in_specs=[tile(d), full(g), full(w_main), full(w_fz), full(b_fz), tile(LANES), tile(LANES)],
        out_specs=[tile(DIFF_WIDTH)] * 6 + [tile(FOX_WIDTH)],
        out_shape=[heads] * 6 + [jax.ShapeDtypeStruct((b, s, FOX_WIDTH), F32)],
        scratch_shapes=[pltpu.VMEM((1, FOX_WIDTH), F32)],
        compiler_params=_params("arbitrary", "arbitrary"),
        name="inproj",
    )(x, g, w_main, w_fz, b_fz, cosf, sinf)


def _stack_maps(q):
    lane = lax.broadcasted_iota(I32, q.shape, 1)
    zero = jnp.zeros_like(q)
    return jnp.concatenate([jnp.where(lane < HEAD_DIM, q, zero),
                            jnp.where(lane >= HEAD_DIM, q, zero)], axis=0)


def _causal_softmax_pv(qs, k_ref, v_ref, qi, tile, bias_fn):
    rows = qs.shape[0]

    def step(j, carry, diagonal):
        m, l, acc = carry
        start = pl.multiple_of(j * tile, tile)
        ks = k_ref[0, pl.ds(start, tile), :]
        vs = v_ref[0, pl.ds(start, tile), :]
        s = lax.dot_general(qs, ks, (((1,), (1,)), ((), ())), preferred_element_type=F32)
        if bias_fn is not None:
            s = s + bias_fn(start)
        if diagonal:
            r = lax.broadcasted_iota(I32, s.shape, 0) & (tile - 1)
            c = lax.broadcasted_iota(I32, s.shape, 1)
            s = jnp.where(c <= r, s, -jnp.inf)
        m_new = jnp.maximum(m, jnp.max(s, axis=-1, keepdims=True))
        alpha = jnp.exp(m - m_new)
        p = jnp.exp(s - m_new)
        l = alpha * l + jnp.sum(p, axis=-1, keepdims=True)
        acc = alpha * acc + jnp.dot(p.astype(BF16), vs, preferred_element_type=F32)
        return m_new, l, acc

    init = (jnp.full((rows, 1), -jnp.inf, F32), jnp.zeros((rows, 1), F32),
            jnp.zeros((rows, LANES), F32))
    carry = lax.fori_loop(0, qi, lambda j, c: step(j, c, False), init)
    _, l, acc = step(qi, carry, True)
    return l, acc


def _diff_attn_kernel(lq1_ref, lk1_ref, lq2_ref, lk2_ref, q_ref, k_ref, v_ref, g_ref, o_ref,
                      *, lambda_init):
    tq = q_ref.shape[1]
    l, acc = _causal_softmax_pv(_stack_maps(q_ref[0]), k_ref, v_ref, pl.program_id(2), tq, None)
    lam = (jnp.exp(jnp.sum(lq1_ref[...] * lk1_ref[...], axis=-1, keepdims=True))
           - jnp.exp(jnp.sum(lq2_ref[...] * lk2_ref[...], axis=-1, keepdims=True)) + lambda_init)
    o = acc[:tq] / l[:tq] - lam * (acc[tq:] / l[tq:])
    o_ref[0] = (_rms(o, g_ref[...]) * (1.0 - lambda_init)).astype(BF16)


def _diff_attention(dq, dk, dv, lq1, lk1, lq2, lk2, g_subln, lambda_init):
    b, s, _ = dq.shape
    t = ATTN_TILE
    vec = lambda arr: pl.BlockSpec(arr.shape, lambda i, h, j: (0, 0))
    return pl.pallas_call(
        functools.partial(_diff_attn_kernel, lambda_init=lambda_init),
        grid=(b, N_DIFF_HEADS, s // t),
        in_specs=[vec(lq1), vec(lk1), vec(lq2), vec(lk2),
                  pl.BlockSpec((1, t, LANES), lambda i, h, j: (i, j, h)),
                  pl.BlockSpec((1, s, LANES), lambda i, h, j: (i, 0, h)),
                  pl.BlockSpec((1, s, LANES), lambda i, h, j: (i, 0, h)),
                  vec(g_subln)],
        out_specs=pl.BlockSpec((1, t, LANES), lambda i, h, j: (i, j, h)),
        out_shape=jax.ShapeDtypeStruct((b, s, DIFF_WIDTH), BF16),
        compiler_params=_params("arbitrary", "arbitrary", "arbitrary"),
        name="diff_attention",
    )(lq1, lk1, lq2, lk2, dq, dk, dv, g_subln)


def _fox_attn_kernel(q_ref, k_ref, v_ref, cq_ref, ck_ref, o_ref):
    tq = q_ref.shape[1]
    cq = cq_ref[0]
    bias_q = jnp.concatenate([cq[:, 0:1], cq[:, HEAD_DIM:HEAD_DIM + 1]], axis=0)

    def bias(start):
        ck = ck_ref[0, 0, :, pl.ds(start, tq)]
        ck = jnp.concatenate([jnp.broadcast_to(ck[0:1], (tq, tq)),
                              jnp.broadcast_to(ck[1:2], (tq, tq))], axis=0)
        return bias_q - ck

    l, acc = _causal_softmax_pv(_stack_maps(q_ref[0]), k_ref, v_ref, pl.program_id(2), tq, bias)
    lane = lax.broadcasted_iota(I32, (tq, LANES), 1)
    o_ref[0] = jnp.where(lane < HEAD_DIM, acc[:tq] / l[:tq], acc[tq:] / l[tq:]).astype(BF16)


def _fox_attention(fq, fk, fv, c_cols, c_rows):
    b, s, _ = fq.shape
    t = ATTN_TILE
    return pl.pallas_call(
        _fox_attn_kernel,
        grid=(b, N_FOX_HEADS // 2, s // t),
        in_specs=[pl.BlockSpec((1, t, LANES), lambda i, h, j: (i, j, h)),
                  pl.BlockSpec((1, s, LANES), lambda i, h, j: (i, 0, h)),
                  pl.BlockSpec((1, s, LANES), lambda i, h, j: (i, 0, h)),
                  pl.BlockSpec((1, t, LANES), lambda i, h, j: (i, j, h)),
                  pl.BlockSpec((1, 1, 2, s), lambda i, h, j: (i, h, 0, 0))],
        out_specs=pl.BlockSpec((1, t, LANES), lambda i, h, j: (i, j, h)),
        out_shape=jax.ShapeDtypeStruct((b, s, FOX_WIDTH), BF16),
        compiler_params=_params("arbitrary", "arbitrary", "arbitrary"),
        name="fox_attention",
    )(fq, fk, fv, c_cols, c_rows)


def _router_kernel(d_ref, f_ref, x_ref, wo_ref, g_ref, wr_ref, br_ref,
                   h_ref, xn_ref, gate_ref, idx_ref, rank_ref, count_ref, carry_ref):
    tm = x_ref.shape[0]
    mixed = (jnp.dot(d_ref[...], wo_ref[0:DIFF_WIDTH, :], preferred_element_type=F32)
             + jnp.dot(f_ref[...], wo_ref[DIFF_WIDTH:, :], preferred_element_type=F32))
    h = x_ref[...] + mixed
    h_ref[...] = h
    xn = _rms(h, g_ref[...])
    xn_ref[...] = xn

    xh = xn.astype(BF16)
    xl = (xn - xh.astype(F32)).astype(BF16)
    wr = wr_ref[...]
    wh = wr.astype(BF16)
    wl = (wr - wh.astype(F32)).astype(BF16)
    logits = (jnp.dot(xh, wh, preferred_element_type=F32)
              + jnp.dot(xl, wh, preferred_element_type=F32)
              + jnp.dot(xh, wl, preferred_element_type=F32)) + br_ref[...]
    lane = lax.broadcasted_iota(I32, (tm, LANES), 1)
    work = jnp.where(lane < N_EXPERTS, logits, -jnp.inf)

    vals, picks = [], []
    for _ in range(TOP_K):
        m = jnp.max(work, axis=-1, keepdims=True)
        idx = jnp.min(jnp.where(work == m, lane, LANES), axis=-1, keepdims=True)
        pick = lane == idx
        vals.append(m)
        picks.append(pick)
        work = jnp.where(pick, -jnp.inf, work)
    exps = [jnp.exp(v - vals[0]) for v in vals]
    denom = exps[0] + exps[1] + exps[2] + exps[3]

    chosen = jnp.where(picks[0] | picks[1] | picks[2] | picks[3], 1.0, 0.0)
    row = lax.broadcasted_iota(I32, (tm, tm), 0)
    col = lax.broadcasted_iota(I32, (tm, tm), 1)
    before = jnp.where(col < row, 1.0, 0.0).astype(BF16)

    @pl.when(pl.program_id(0) == 0)
    def _():
        carry_ref[...] = jnp.zeros_like(carry_ref)

    rank = jnp.dot(before, chosen.astype(BF16), preferred_element_type=F32) + carry_ref[...]
    carry_ref[...] = carry_ref[...] + jnp.sum(chosen, axis=0, keepdims=True)
    count_ref[...] = carry_ref[...]

    gate_out = jnp.zeros((tm, LANES), F32)
    idx_out = jnp.zeros((tm, LANES), I32)
    rank_out = jnp.zeros((tm, LANES), F32)
    for k in range(TOP_K):
        here = lane == k
        gate_out = jnp.where(here, exps[k] / denom, gate_out)
        idx_out = jnp.where(here, jnp.min(jnp.where(picks[k], lane, LANES), axis=-1, keepdims=True), idx_out)
        rank_out = jnp.where(here, jnp.sum(jnp.where(picks[k], rank, 0.0), axis=-1, keepdims=True), rank_out)
    gate_ref[...] = gate_out
    idx_ref[...] = idx_out
    rank_ref[...] = rank_out.astype(I32)


def _router(d_out, f_out, x, w_out, g_moe, w_router, b_router):
    n, d = x.shape
    tm = TOKEN_TILE
    tile = lambda width: pl.BlockSpec((tm, width), lambda i: (i, 0))
    full = lambda arr: pl.BlockSpec(arr.shape, lambda i: (0,) * arr.ndim)
    return pl.pallas_call(
        _router_kernel,
        grid=(n // tm,),
        in_specs=[tile(DIFF_WIDTH), tile(FOX_WIDTH), tile(d), full(w_out), full(g_moe),
                  full(w_router), full(b_router)],
        out_specs=[tile(d), tile(d), tile(LANES), tile(LANES), tile(LANES),
                   pl.BlockSpec((1, LANES), lambda i: (0, 0))],
        out_shape=[jax.ShapeDtypeStruct((n, d), F32), jax.ShapeDtypeStruct((n, d), F32),
                   jax.ShapeDtypeStruct((n, LANES), F32), jax.ShapeDtypeStruct((n, LANES), I32),
                   jax.ShapeDtypeStruct((n, LANES), I32), jax.ShapeDtypeStruct((1, LANES), F32)],
        scratch_shapes=[pltpu.VMEM((1, LANES), F32)],
        compiler_params=_params("arbitrary"),
        name="router",
    )(d_out, f_out, x, w_out, g_moe, w_router, b_router)


def _dispatch_kernel(pos_ref, x_ref, init_ref, xs_ref, sem):
    del init_ref
    tm = x_ref.shape[0]

    def issue(r, carry):
        for k in range(TOP_K):
            dst = pos_ref[k * tm + r]
            pltpu.make_async_copy(x_ref.at[pl.ds(r, 1), :], xs_ref.at[pl.ds(dst, 1), :], sem).start()
        return carry

    lax.fori_loop(0, tm, issue, 0)
    for _ in range(TOP_K):
        pltpu.make_async_copy(x_ref, xs_ref.at[pl.ds(0, tm), :], sem).wait()


def _dispatch(pos_flat, xn, n_rows):
    n, d = xn.shape
    tm = TOKEN_TILE
    return pl.pallas_call(
        _dispatch_kernel,
        grid=(n // tm,),
        in_specs=[pl.BlockSpec((TOP_K * tm,), lambda i: (i,), memory_space=pltpu.SMEM),
                  pl.BlockSpec((tm, d), lambda i: (i, 0)),
                  pl.BlockSpec(memory_space=pl.ANY)],
        out_specs=pl.BlockSpec(memory_space=pl.ANY),
        out_shape=jax.ShapeDtypeStruct((n_rows, d), F32),
        scratch_shapes=[pltpu.SemaphoreType.DMA(())],
        input_output_aliases={2: 0},
        compiler_params=_params("arbitrary"),
        name="dispatch",
    )(pos_flat, xn, jnp.zeros((n_rows, d), F32))


def _expert_kernel(tile_expert_ref, n_used_ref, x_ref, w1g_ref, w1l_ref, b1g_ref, b1l_ref,
                   w2_ref, b2_ref, y_ref):
    del tile_expert_ref
    t = pl.program_id(0)

    @pl.when(t < n_used_ref[0])
    def _():
        xb = x_ref[...].astype(BF16)
        hg = jnp.dot(xb, w1g_ref[0], preferred_element_type=F32) + b1g_ref[0]
        hl = jnp.dot(xb, w1l_ref[0], preferred_element_type=F32) + b1l_ref[0]
        glu = jnp.minimum(hg, SWIGLU_LIMIT)
        lin = jnp.clip(hl, -SWIGLU_LIMIT, SWIGLU_LIMIT)
        act = glu * jax.nn.sigmoid(SWIGLU_ALPHA * glu) * (lin + 1.0)
        y_ref[...] = jnp.dot(act.astype(BF16), w2_ref[0], preferred_element_type=F32) + b2_ref[0]

    @pl.when(t >= n_used_ref[0])
    def _():
        y_ref[...] = jnp.zeros_like(y_ref)


def _experts(tile_expert, n_used, xs, w1g, w1l, b1g, b1l, w2, b2):
    n_rows, d = xs.shape
    tm = EXPERT_TILE
    by_expert = lambda arr: pl.BlockSpec((1,) + arr.shape[1:], lambda i, te, nu: (te[i], 0, 0))
    return pl.pallas_call(
        _expert_kernel,
        grid_spec=pltpu.PrefetchScalarGridSpec(
            num_scalar_prefetch=2,
            grid=(n_rows // tm,),
            in_specs=[pl.BlockSpec((tm, d), lambda i, te, nu: (i, 0)),
                      by_expert(w1g), by_expert(w1l), by_expert(b1g), by_expert(b1l),
                      by_expert(w2), by_expert(b2)],
            out_specs=pl.BlockSpec((tm, d), lambda i, te, nu: (i, 0))),
        out_shape=jax.ShapeDtypeStruct((n_rows, d), F32),
        compiler_params=_params("arbitrary"),
        name="experts",
    )(tile_expert, n_used, xs, w1g, w1l, b1g, b1l, w2, b2)


def _combine_kernel(pos_ref, h_ref, gate_ref, p_ref, wg_ref, wp_ref, gple_ref, gfin_ref, y_ref,
                    o_ref, ybuf, sem, *, final_norm):
    tm = h_ref.shape[0]

    def issue(r, carry):
        for k in range(TOP_K):
            src = pos_ref[k * tm + r]
            pltpu.make_async_copy(y_ref.at[pl.ds(src, 1), :], ybuf.at[k, pl.ds(r, 1), :], sem).start()
        return carry

    lax.fori_loop(0, tm, issue, 0)
    for k in range(TOP_K):
        pltpu.make_async_copy(y_ref.at[pl.ds(0, tm), :], ybuf.at[k], sem).wait()

    gates = gate_ref[...]
    moe = gates[:, 0:1] * ybuf[0]
    for k in range(1, TOP_K):
        moe = moe + gates[:, k:k + 1] * ybuf[k]
    h = h_ref[...] + moe
    gate = jax.nn.sigmoid(jnp.dot(_rms(h, gple_ref[...]).astype(BF16), wg_ref[...],
                                  preferred_element_type=F32))
    emb = jnp.dot(p_ref[...].astype(BF16), wp_ref[...], preferred_element_type=F32)
    h = h + gate * emb
    o_ref[...] = _rms(h, gfin_ref[...]) if final_norm else h


def _combine(pos_flat, h, gates, p, w_gate, w_proj, g_ple, g_final, y, final_norm):
    n, d = h.shape
    tm = TOKEN_TILE
    tile = lambda width: pl.BlockSpec((tm, width), lambda i: (i, 0))
    full = lambda arr: pl.BlockSpec(arr.shape, lambda i: (0,) * arr.ndim)
    return pl.pallas_call(
        functools.partial(_combine_kernel, final_norm=final_norm),
        grid=(n // tm,),
        in_specs=[pl.BlockSpec((TOP_K * tm,), lambda i: (i,), memory_space=pltpu.SMEM),
                  tile(d), tile(LANES), tile(p.shape[1]), full(w_gate), full(w_proj),
                  full(g_ple), full(g_final), pl.BlockSpec(memory_space=pl.ANY)],
        out_specs=tile(d),
        out_shape=jax.ShapeDtypeStruct((n, d), F32),
        scratch_shapes=[pltpu.VMEM((TOP_K, tm, d), F32), pltpu.SemaphoreType.DMA(())],
        compiler_params=_params("arbitrary"),
        name="combine",
    )(pos_flat, h, gates, p, w_gate, w_proj, g_ple, g_final, y)


def _layer(h, p, cosf, sinf, g_attn, w_in, lq1, lk1, lq2, lk2, g_subln, b_forget, w_out, g_moe,
           w_router, b_router, w_e1, b_e1, w_e2, b_e2, g_ple, w_ple_gate, w_ple_proj, g_final,
           lambda_init, final_norm):
    b, s, d = h.shape
    n = b * s
    row = lambda v: v.reshape(1, -1)

    w_main = w_in[:, :MAIN_WIDTH].astype(BF16)
    w_fz = jnp.repeat(w_in[:, MAIN_WIDTH:], HEAD_DIM, axis=1).astype(BF16)
    b_fz = row(jnp.repeat(b_forget, HEAD_DIM))
    w1g = w_e1[:, :, 0::2].astype(BF16)
    w1l = w_e1[:, :, 1::2].astype(BF16)
    b1g = b_e1[:, None, 0::2]
    b1l = b_e1[:, None, 1::2]
    w2 = w_e2.astype(BF16)
    b2 = b_e2[:, None, :]
    w_router_p = jnp.pad(w_router, ((0, 0), (0, LANES - N_EXPERTS)))
    b_router_p = row(jnp.pad(b_router, (0, LANES - N_EXPERTS)))

    dq, dk, dv, fq, fk, fv, c = _inproj(h, row(g_attn), w_main, w_fz, b_fz, cosf, sinf)
    d_out = _diff_attention(dq, dk, dv, row(lq1), row(lk1), row(lq2), row(lk2), row(g_subln),
                            lambda_init)
    c_rows = c[:, :, ::HEAD_DIM].transpose(0, 2, 1).reshape(b, N_FOX_HEADS // 2, 2, s)
    f_out = _fox_attention(fq, fk, fv, c, c_rows)

    h1, xn, gates, idx, rank, counts = _router(
        d_out.reshape(n, DIFF_WIDTH), f_out.reshape(n, FOX_WIDTH), h.reshape(n, d),
        w_out.astype(BF16), row(g_moe), w_router_p, b_router_p)

    te = EXPERT_TILE
    n_rows = n * TOP_K + N_EXPERTS * te
    counts = counts[0, :N_EXPERTS].astype(I32)
    padded = (counts + te - 1) // te * te
    ends = jnp.cumsum(padded)
    starts = ends - padded
    pos = starts[idx[:, :TOP_K]] + rank[:, :TOP_K]
    tm = TOKEN_TILE
    pos_flat = pos.reshape(n // tm, tm, TOP_K).transpose(0, 2, 1).reshape(-1)
    tile_expert = jnp.clip(jnp.searchsorted(ends, jnp.arange(n_rows // te, dtype=I32) * te,
                                            side="right"), 0, N_EXPERTS - 1).astype(I32)
    n_used = (ends[-1:] // te).astype(I32)

    xs = _dispatch(pos_flat, xn, n_rows)
    y = _experts(tile_expert, n_used, xs, w1g, w1l, b1g, b1l, w2, b2)
    out = _combine(pos_flat, h1, gates, p.reshape(n, PLE_DIM), w_ple_gate.astype(BF16),
                   w_ple_proj.astype(BF16), row(g_ple), row(g_final), y, final_norm)
    return out.reshape(b, s, d)


def kernel(x, p, positions, g_attn, w_in, lambda_q1, lambda_k1, lambda_q2, lambda_k2, g_subln, b_forget, w_out, g_moe, w_router, b_router, w_e1, b_e1, w_e2, b_e2, g_ple, w_ple_gate, w_ple_proj, g_final):
    depth = g_attn.shape[0]
    cosf, sinf = _rope_tables(positions)
    h = x
    for i in range(depth):
        lambda_init = 0.8 - 0.6 * math.exp(-0.3 * i)
        h = _layer(h, p[i], cosf, sinf, g_attn[i], w_in[i], lambda_q1[i], lambda_k1[i],
                   lambda_q2[i], lambda_k2[i], g_subln[i], b_forget[i], w_out[i], g_moe[i],
                   w_router[i], b_router[i], w_e1[i], b_e1[i], w_e2[i], b_e2[i], g_ple[i],
                   w_ple_gate[i], w_ple_proj[i], g_final, lambda_init, i == depth - 1)
    return h
```

```python
import functools
import math

import jax
import jax.numpy as jnp
from jax import lax
from jax.experimental import pallas as pl
from jax.experimental.pallas import tpu as pltpu

F32 = jnp.float32
BF16 = jnp.bfloat16
I32 = jnp.int32

LANES = 128
D_MODEL = 1024
HEAD_DIM = 64
N_DIFF_HEADS = 4
DIFF_V_DIM = 2 * HEAD_DIM
N_FOX_HEADS = 8
DIFF_WIDTH = N_DIFF_HEADS * DIFF_V_DIM
FOX_WIDTH = N_FOX_HEADS * HEAD_DIM
ROT_DIM = HEAD_DIM // 4
ROPE_THETA = 500000.0
N_EXPERTS = 32
TOP_K = 4
D_EXPERT = D_MODEL
SWIGLU_ALPHA = 1.702
SWIGLU_LIMIT = 7.0
PLE_DIM = 256
NORM_EPS = 1e-5
MAIN_WIDTH = 3 * DIFF_WIDTH + 3 * FOX_WIDTH

SEQ_TILE = 256
ATTN_TILE = 256
TOKEN_TILE = 256
EXPERT_TILE = 256
VMEM_LIMIT = 48 * 1024 * 1024


def _params(*semantics):
    return pltpu.CompilerParams(dimension_semantics=semantics, vmem_limit_bytes=VMEM_LIMIT)


def _rms(x, g):
    return x * lax.rsqrt(jnp.mean(x * x, axis=-1, keepdims=True) + NORM_EPS) * g


def _rope_table_kernel(pos_ref, cos_ref, sin_ref):
    pos = pos_ref[0].astype(F32)
    d = lax.broadcasted_iota(I32, (1, LANES), 1) % HEAD_DIM
    j = d % (ROT_DIM // 2)
    inv_freq = jnp.power(ROPE_THETA, -(2 * j).astype(F32) / ROT_DIM)
    ang = pos * inv_freq
    c, s = jnp.cos(ang), jnp.sin(ang)
    cos_ref[0] = jnp.where(d < ROT_DIM, c, 1.0)
    sin_ref[0] = jnp.where(d < ROT_DIM // 2, -s, jnp.where(d < ROT_DIM, s, 0.0))


def _rope_tables(positions):
    b, s = positions.shape
    ts = SEQ_TILE
    return pl.pallas_call(
        _rope_table_kernel,
        grid=(b, s // ts),
        in_specs=[pl.BlockSpec((1, ts, 1), lambda i, j: (i, j, 0))],
        out_specs=[pl.BlockSpec((1, ts, LANES), lambda i, j: (i, j, 0))] * 2,
        out_shape=[jax.ShapeDtypeStruct((b, s, LANES), F32)] * 2,
        compiler_params=_params("arbitrary", "arbitrary"),
        name="rope_tables",
    )(positions.reshape(b, s, 1))


def _inproj_kernel(x_ref, g_ref, w_ref, wfz_ref, bfz_ref, cos_ref, sin_ref,
                   dq_ref, dk_ref, dv_ref, fq_ref, fk_ref, fv_ref, c_ref, carry_ref):
    ts = x_ref.shape[1]
    xb = _rms(x_ref[0], g_ref[...]).astype(BF16)
    proj = jnp.dot(xb, w_ref[...], preferred_element_type=F32)

    cosf, sinf = cos_ref[0], sin_ref[0]
    first_half = lax.broadcasted_iota(I32, (1, LANES), 1) % HEAD_DIM < ROT_DIM // 2

    def rope(t):
        outs = []
        for j in range(t.shape[1] // LANES):
            tj = t[:, j * LANES:(j + 1) * LANES]
            partner = jnp.where(first_half,
                                pltpu.roll(tj, LANES - ROT_DIM // 2, 1),
                                pltpu.roll(tj, ROT_DIM // 2, 1))
            outs.append(tj * cosf + partner * sinf)
        return jnp.concatenate(outs, axis=1)

    scale = HEAD_DIM ** -0.5
    w = DIFF_WIDTH
    dq_ref[0] = (rope(proj[:, 0:w]) * scale).astype(BF16)
    dk_ref[0] = rope(proj[:, w:2 * w]).astype(BF16)
    dv_ref[0] = proj[:, 2 * w:3 * w].astype(BF16)
    fq_ref[0] = (proj[:, 3 * w:4 * w] * scale).astype(BF16)
    fk_ref[0] = proj[:, 4 * w:5 * w].astype(BF16)
    fv_ref[0] = proj[:, 5 * w:6 * w].astype(BF16)

    fz = jnp.dot(xb, wfz_ref[...], preferred_element_type=F32) + bfz_ref[...]
    logf = jnp.minimum(fz, 0.0) - jnp.log1p(jnp.exp(-jnp.abs(fz)))
    row = lax.broadcasted_iota(I32, (ts, ts), 0)
    col = lax.broadcasted_iota(I32, (ts, ts), 1)
    tri = jnp.where(col <= row, 1.0, 0.0).astype(BF16)
    hi = logf.astype(BF16)
    r1 = logf - hi.astype(F32)
    mid = r1.astype(BF16)
    lo = (r1 - mid.astype(F32)).astype(BF16)
    cum = (jnp.dot(tri, hi, preferred_element_type=F32)
           + jnp.dot(tri, mid, preferred_element_type=F32)
           + jnp.dot(tri, lo, preferred_element_type=F32))

    @pl.when(pl.program_id(1) == 0)
    def _():
        carry_ref[...] = jnp.zeros_like(carry_ref)

    c = cum + carry_ref[...]
    c_ref[0] = c
    carry_ref[...] = c[ts - 1:ts, :]


def _inproj(x, g, w_main, w_fz, b_fz, cosf, sinf):
    b, s, d = x.shape
    ts = SEQ_TILE
    tile = lambda width: pl.BlockSpec((1, ts, width), lambda i, j: (i, j, 0))
    full = lambda arr: pl.BlockSpec(arr.shape, lambda i, j: (0,) * arr.ndim)
    heads = jax.ShapeDtypeStruct((b, s, DIFF_WIDTH), BF16)
    return pl.pallas_call(
        _inproj_kernel,
        grid=(b, s // ts),
        in_specs=[tile(d), full(g), full(w_main), full(w_fz), full(b_fz), tile(LANES), tile(LANES)],
        out_specs=[tile(DIFF_WIDTH)] * 6 + [tile(FOX_WIDTH)],
        out_shape=[heads] * 6 + [jax.ShapeDtypeStruct((b, s, FOX_WIDTH), F32)],
        scratch_shapes=[pltpu.VMEM((1, FOX_WIDTH), F32)],
        compiler_params=_params("arbitrary", "arbitrary"),
        name="inproj",
    )(x, g, w_main, w_fz, b_fz, cosf, sinf)


def _stack_maps(q):
    lane = lax.broadcasted_iota(I32, q.shape, 1)
    zero = jnp.zeros_like(q)
    return jnp.concatenate([jnp.where(lane < HEAD_DIM, q, zero),
                            jnp.where(lane >= HEAD_DIM, q, zero)], axis=0)


def _causal_softmax_pv(qs, k_ref, v_ref, qi, tile, bias_fn):
    rows = qs.shape[0]

    def step(j, carry, diagonal):
        m, l, acc = carry
        start = pl.multiple_of(j * tile, tile)
        ks = k_ref[0, pl.ds(start, tile), :]
        vs = v_ref[0, pl.ds(start, tile), :]
        s = lax.dot_general(qs, ks, (((1,), (1,)), ((), ())), preferred_element_type=F32)
        if bias_fn is not None:
            s = s + bias_fn(start)
        if diagonal:
            r = lax.broadcasted_iota(I32, s.shape, 0) & (tile - 1)
            c = lax.broadcasted_iota(I32, s.shape, 1)
            s = jnp.where(c <= r, s, -jnp.inf)
        m_new = jnp.maximum(m, jnp.max(s, axis=-1, keepdims=True))
        alpha = jnp.exp(m - m_new)
        p = jnp.exp(s - m_new)
        l = alpha * l + jnp.sum(p, axis=-1, keepdims=True)
        acc = alpha * acc + jnp.dot(p.astype(BF16), vs, preferred_element_type=F32)
        return m_new, l, acc

    init = (jnp.full((rows, 1), -jnp.inf, F32), jnp.zeros((rows, 1), F32),
            jnp.zeros((rows, LANES), F32))
    carry = lax.fori_loop(0, qi, lambda j, c: step(j, c, False), init)
    _, l, acc = step(qi, carry, True)
    return l, acc


def _diff_attn_kernel(lq1_ref, lk1_ref, lq2_ref, lk2_ref, q_ref, k_ref, v_ref, g_ref, o_ref,
                      *, lambda_init):
    tq = q_ref.shape[1]
    l, acc = _causal_softmax_pv(_stack_maps(q_ref[0]), k_ref, v_ref, pl.program_id(2), tq, None)
    lam = (jnp.exp(jnp.sum(lq1_ref[...] * lk1_ref[...], axis=-1, keepdims=True))
           - jnp.exp(jnp.sum(lq2_ref[...] * lk2_ref[...], axis=-1, keepdims=True)) + lambda_init)
    o = acc[:tq] / l[:tq] - lam * (acc[tq:] / l[tq:])
    o_ref[0] = (_rms(o, g_ref[...]) * (1.0 - lambda_init)).astype(BF16)


def _diff_attention(dq, dk, dv, lq1, lk1, lq2, lk2, g_subln, lambda_init):
    b, s, _ = dq.shape
    t = ATTN_TILE
    vec = lambda arr: pl.BlockSpec(arr.shape, lambda i, h, j: (0, 0))
    return pl.pallas_call(
        functools.partial(_diff_attn_kernel, lambda_init=lambda_init),
        grid=(b, N_DIFF_HEADS, s // t),
        in_specs=[vec(lq1), vec(lk1), vec(lq2), vec(lk2),
                  pl.BlockSpec((1, t, LANES), lambda i, h, j: (i, j, h)),
                  pl.BlockSpec((1, s, LANES), lambda i, h, j: (i, 0, h)),
                  pl.BlockSpec((1, s, LANES), lambda i, h, j: (i, 0, h)),
                  vec(g_subln)],
        out_specs=pl.BlockSpec((1, t, LANES), lambda i, h, j: (i, j, h)),
        out_shape=jax.ShapeDtypeStruct((b, s, DIFF_WIDTH), BF16),
        compiler_params=_params("arbitrary", "arbitrary", "arbitrary"),
        name="diff_attention",
    )(lq1, lk1, lq2, lk2, dq, dk, dv, g_subln)


def _fox_attn_kernel(q_ref, k_ref, v_ref, cq_ref, ck_ref, o_ref):
    tq = q_ref.shape[1]
    cq = cq_ref[0]
    bias_q = jnp.concatenate([cq[:, 0:1], cq[:, HEAD_DIM:HEAD_DIM + 1]], axis=0)

    def bias(start):
        ck = ck_ref[0, 0, :, pl.ds(start, tq)]
        ck = jnp.concatenate([jnp.broadcast_to(ck[0:1], (tq, tq)),
                              jnp.broadcast_to(ck[1:2], (tq, tq))], axis=0)
        return bias_q - ck

    l, acc = _causal_softmax_pv(_stack_maps(q_ref[0]), k_ref, v_ref, pl.program_id(2), tq, bias)
    lane = lax.broadcasted_iota(I32, (tq, LANES), 1)
    o_ref[0] = jnp.where(lane < HEAD_DIM, acc[:tq] / l[:tq], acc[tq:] / l[tq:]).astype(BF16)


def _fox_attention(fq, fk, fv, c_cols, c_rows):
    b, s, _ = fq.shape
    t = ATTN_TILE
    return pl.pallas_call(
        _fox_attn_kernel,
        grid=(b, N_FOX_HEADS // 2, s // t),
        in_specs=[pl.BlockSpec((1, t, LANES), lambda i, h, j: (i, j, h)),
                  pl.BlockSpec((1, s, LANES), lambda i, h, j: (i, 0, h)),
                  pl.BlockSpec((1, s, LANES), lambda i, h, j: (i, 0, h)),
                  pl.BlockSpec((1, t, LANES), lambda i, h, j: (i, j, h)),
                  pl.BlockSpec((1, 1, 2, s), lambda i, h, j: (i, h, 0, 0))],
        out_specs=pl.BlockSpec((1, t, LANES), lambda i, h, j: (i, j, h)),
        out_shape=jax.ShapeDtypeStruct((b, s, FOX_WIDTH), BF16),
        compiler_params=_params("arbitrary", "arbitrary", "arbitrary"),
        name="fox_attention",
    )(fq, fk, fv, c_cols, c_rows)


def _router_kernel(d_ref, f_ref, x_ref, wo_ref, g_ref, wr_ref, br_ref,
                   h_ref, xn_ref, gate_ref, idx_ref, rank_ref, count_ref, carry_ref):
    tm = x_ref.shape[0]
    mixed = (jnp.dot(d_ref[...], wo_ref[0:DIFF_WIDTH, :], preferred_element_type=F32)
             + jnp.dot(f_ref[...], wo_ref[DIFF_WIDTH:, :], preferred_element_type=F32))
    h = x_ref[...] + mixed
    h_ref[...] = h
    xn = _rms(h, g_ref[...])
    xn_ref[...] = xn

    xh = xn.astype(BF16)
    xl = (xn - xh.astype(F32)).astype(BF16)
    wr = wr_ref[...]
    wh = wr.astype(BF16)
    wl = (wr - wh.astype(F32)).astype(BF16)
    logits = (jnp.dot(xh, wh, preferred_element_type=F32)
              + jnp.dot(xl, wh, preferred_element_type=F32)
              + jnp.dot(xh, wl, preferred_element_type=F32)) + br_ref[...]
    lane = lax.broadcasted_iota(I32, (tm, LANES), 1)
    work = jnp.where(lane < N_EXPERTS, logits, -jnp.inf)

    vals, picks = [], []
    for _ in range(TOP_K):
        m = jnp.max(work, axis=-1, keepdims=True)
        idx = jnp.min(jnp.where(work == m, lane, LANES), axis=-1, keepdims=True)
        pick = lane == idx
        vals.append(m)
        picks.append(pick)
        work = jnp.where(pick, -jnp.inf, work)
    exps = [jnp.exp(v - vals[0]) for v in vals]
    denom = exps[0] + exps[1] + exps[2] + exps[3]

    chosen = jnp.where(picks[0] | picks[1] | picks[2] | picks[3], 1.0, 0.0)
    row = lax.broadcasted_iota(I32, (tm, tm), 0)
    col = lax.broadcasted_iota(I32, (tm, tm), 1)
    before = jnp.where(col < row, 1.0, 0.0).astype(BF16)

    @pl.when(pl.program_id(0) == 0)
    def _():
        carry_ref[...] = jnp.zeros_like(carry_ref)

    rank = jnp.dot(before, chosen.astype(BF16), preferred_element_type=F32) + carry_ref[...]
    carry_ref[...] = carry_ref[...] + jnp.sum(chosen, axis=0, keepdims=True)
    count_ref[...] = carry_ref[...]

    gate_out = jnp.zeros((tm, LANES), F32)
    idx_out = jnp.zeros((tm, LANES), I32)
    rank_out = jnp.zeros((tm, LANES), F32)
    for k in range(TOP_K):
        here = lane == k
        gate_out = jnp.where(here, exps[k] / denom, gate_out)
        idx_out = jnp.where(here, jnp.min(jnp.where(picks[k], lane, LANES), axis=-1, keepdims=True), idx_out)
        rank_out = jnp.where(here, jnp.sum(jnp.where(picks[k], rank, 0.0), axis=-1, keepdims=True), rank_out)
    gate_ref[...] = gate_out
    idx_ref[...] = idx_out
    rank_ref[...] = rank_out.astype(I32)


def _router(d_out, f_out, x, w_out, g_moe, w_router, b_router):
    n, d = x.shape
    tm = TOKEN_TILE
    tile = lambda width: pl.BlockSpec((tm, width), lambda i: (i, 0))
    full = lambda arr: pl.BlockSpec(arr.shape, lambda i: (0,) * arr.ndim)
    return pl.pallas_call(
        _router_kernel,
        grid=(n // tm,),
        in_specs=[tile(DIFF_WIDTH), tile(FOX_WIDTH), tile(d), full(w_out), full(g_moe),
                  full(w_router), full(b_router)],
        out_specs=[tile(d), tile(d), tile(LANES), tile(LANES), tile(LANES),
                   pl.BlockSpec((1, LANES), lambda i: (0, 0))],
        out_shape=[jax.ShapeDtypeStruct((n, d), F32), jax.ShapeDtypeStruct((n, d), F32),
                   jax.ShapeDtypeStruct((n, LANES), F32), jax.ShapeDtypeStruct((n, LANES), I32),
                   jax.ShapeDtypeStruct((n, LANES), I32), jax.ShapeDtypeStruct((1, LANES), F32)],
        scratch_shapes=[pltpu.VMEM((1, LANES), F32)],
        compiler_params=_params("arbitrary"),
        name="router",
    )(d_out, f_out, x, w_out, g_moe, w_router, b_router)


def _dispatch_kernel(pos_ref, x_ref, init_ref, xs_ref, sem):
    del init_ref
    tm = x_ref.shape[0]

    def issue(r, carry):
        for k in range(TOP_K):
            dst = pos_ref[k * tm + r]
            pltpu.make_async_copy(x_ref.at[pl.ds(r, 1), :], xs_ref.at[pl.ds(dst, 1), :], sem).start()
        return carry

    lax.fori_loop(0, tm, issue, 0)
    for _ in range(TOP_K):
        pltpu.make_async_copy(x_ref, xs_ref.at[pl.ds(0, tm), :], sem).wait()


def _dispatch(pos_flat, xn, n_rows):
    n, d = xn.shape
    tm = TOKEN_TILE
    return pl.pallas_call(
        _dispatch_kernel,
        grid=(n // tm,),
        in_specs=[pl.BlockSpec((TOP_K * tm,), lambda i: (i,), memory_space=pltpu.SMEM),
                  pl.BlockSpec((tm, d), lambda i: (i, 0)),
                  pl.BlockSpec(memory_space=pl.ANY)],
        out_specs=pl.BlockSpec(memory_space=pl.ANY),
        out_shape=jax.ShapeDtypeStruct((n_rows, d), F32),
        scratch_shapes=[pltpu.SemaphoreType.DMA(())],
        input_output_aliases={2: 0},
        compiler_params=_params("arbitrary"),
        name="dispatch",
    )(pos_flat, xn, jnp.zeros((n_rows, d), F32))


def _expert_kernel(tile_expert_ref, n_used_ref, x_ref, w1g_ref, w1l_ref, b1g_ref, b1l_ref,
                   w2_ref, b2_ref, y_ref):
    del tile_expert_ref
    t = pl.program_id(0)

    @pl.when(t < n_used_ref[0])
    def _():
        xb = x_ref[...].astype(BF16)
        nt = (((1,), (1,)), ((), ()))
        hg = lax.dot_general(xb, w1g_ref[0], nt, preferred_element_type=F32) + b1g_ref[0]
        hl = lax.dot_general(xb, w1l_ref[0], nt, preferred_element_type=F32) + b1l_ref[0]
        glu = jnp.minimum(hg, SWIGLU_LIMIT)
        lin = jnp.clip(hl, -SWIGLU_LIMIT, SWIGLU_LIMIT)
        act = glu * jax.nn.sigmoid(SWIGLU_ALPHA * glu) * (lin + 1.0)
        y_ref[...] = jnp.dot(act.astype(BF16), w2_ref[0], preferred_element_type=F32) + b2_ref[0]

    @pl.when(t >= n_used_ref[0])
    def _():
        y_ref[...] = jnp.zeros_like(y_ref)


def _experts(tile_expert, n_used, xs, w1g, w1l, b1g, b1l, w2, b2):
    n_rows, d = xs.shape
    tm = EXPERT_TILE
    by_expert = lambda arr: pl.BlockSpec((1,) + arr.shape[1:], lambda i, te, nu: (te[i], 0, 0))
    return pl.pallas_call(
        _expert_kernel,
        grid_spec=pltpu.PrefetchScalarGridSpec(
            num_scalar_prefetch=2,
            grid=(n_rows // tm,),
            in_specs=[pl.BlockSpec((tm, d), lambda i, te, nu: (i, 0)),
                      by_expert(w1g), by_expert(w1l), by_expert(b1g), by_expert(b1l),
                      by_expert(w2), by_expert(b2)],
            out_specs=pl.BlockSpec((tm, d), lambda i, te, nu: (i, 0))),
        out_shape=jax.ShapeDtypeStruct((n_rows, d), F32),
        compiler_params=_params("arbitrary"),
        name="experts",
    )(tile_expert, n_used, xs, w1g, w1l, b1g, b1l, w2, b2)


def _combine_kernel(pos_ref, h_ref, gate_ref, p_ref, wg_ref, wp_ref, gple_ref, gfin_ref, y_ref,
                    o_ref, ybuf, sem, *, final_norm):
    tm = h_ref.shape[0]

    def issue(r, carry):
        for k in range(TOP_K):
            src = pos_ref[k * tm + r]
            pltpu.make_async_copy(y_ref.at[pl.ds(src, 1), :], ybuf.at[k, pl.ds(r, 1), :], sem).start()
        return carry

    lax.fori_loop(0, tm, issue, 0)
    for k in range(TOP_K):
        pltpu.make_async_copy(y_ref.at[pl.ds(0, tm), :], ybuf.at[k], sem).wait()

    gates = gate_ref[...]
    moe = gates[:, 0:1] * ybuf[0]
    for k in range(1, TOP_K):
        moe = moe + gates[:, k:k + 1] * ybuf[k]
    h = h_ref[...] + moe
    gate = jax.nn.sigmoid(jnp.dot(_rms(h, gple_ref[...]).astype(BF16), wg_ref[...],
                                  preferred_element_type=F32))
    emb = jnp.dot(p_ref[...].astype(BF16), wp_ref[...], preferred_element_type=F32)
    h = h + gate * emb
    o_ref[...] = _rms(h, gfin_ref[...]) if final_norm else h


def _combine(pos_flat, h, gates, p, w_gate, w_proj, g_ple, g_final, y, final_norm):
    n, d = h.shape
    tm = TOKEN_TILE
    tile = lambda width: pl.BlockSpec((tm, width), lambda i: (i, 0))
    full = lambda arr: pl.BlockSpec(arr.shape, lambda i: (0,) * arr.ndim)
    return pl.pallas_call(
        functools.partial(_combine_kernel, final_norm=final_norm),
        grid=(n // tm,),
        in_specs=[pl.BlockSpec((TOP_K * tm,), lambda i: (i,), memory_space=pltpu.SMEM),
                  tile(d), tile(LANES), tile(p.shape[1]), full(w_gate), full(w_proj),
                  full(g_ple), full(g_final), pl.BlockSpec(memory_space=pl.ANY)],
        out_specs=tile(d),
        out_shape=jax.ShapeDtypeStruct((n, d), F32),
        scratch_shapes=[pltpu.VMEM((TOP_K, tm, d), F32), pltpu.SemaphoreType.DMA(())],
        compiler_params=_params("arbitrary"),
        name="combine",
    )(pos_flat, h, gates, p, w_gate, w_proj, g_ple, g_final, y)


def _layer(h, p, cosf, sinf, g_attn, w_in, lq1, lk1, lq2, lk2, g_subln, b_forget, w_out, g_moe,
           w_router, b_router, w_e1, b_e1, w_e2, b_e2, g_ple, w_ple_gate, w_ple_proj, g_final,
           lambda_init, final_norm):
    b, s, d = h.shape
    n = b * s
    row = lambda v: v.reshape(1, -1)

    w_main = w_in[:, :MAIN_WIDTH].astype(BF16)
    w_fz = jnp.repeat(w_in[:, MAIN_WIDTH:], HEAD_DIM, axis=1).astype(BF16)
    b_fz = row(jnp.repeat(b_forget, HEAD_DIM))
    w1t = w_e1.transpose(0, 2, 1)
    w1g = w1t[:, 0::2, :].astype(BF16)
    w1l = w1t[:, 1::2, :].astype(BF16)
    b1g = b_e1[:, None, 0::2]
    b1l = b_e1[:, None, 1::2]
    w2 = w_e2.astype(BF16)
    b2 = b_e2[:, None, :]
    w_router_p = jnp.pad(w_router, ((0, 0), (0, LANES - N_EXPERTS)))
    b_router_p = row(jnp.pad(b_router, (0, LANES - N_EXPERTS)))

    dq, dk, dv, fq, fk, fv, c = _inproj(h, row(g_attn), w_main, w_fz, b_fz, cosf, sinf)
    d_out = _diff_attention(dq, dk, dv, row(lq1), row(lk1), row(lq2), row(lk2), row(g_subln),
                            lambda_init)
    c_rows = c[:, :, ::HEAD_DIM].transpose(0, 2, 1).reshape(b, N_FOX_HEADS // 2, 2, s)
    f_out = _fox_attention(fq, fk, fv, c, c_rows)

    h1, xn, gates, idx, rank, counts = _router(
        d_out.reshape(n, DIFF_WIDTH), f_out.reshape(n, FOX_WIDTH), h.reshape(n, d),
        w_out.astype(BF16), row(g_moe), w_router_p, b_router_p)

    te = EXPERT_TILE
    n_rows = n * TOP_K + N_EXPERTS * te
    counts = counts[0, :N_EXPERTS].astype(I32)
    padded = (counts + te - 1) // te * te
    ends = jnp.cumsum(padded)
    starts = ends - padded
    pos = starts[idx[:, :TOP_K]] + rank[:, :TOP_K]
    tm = TOKEN_TILE
    pos_flat = pos.reshape(n // tm, tm, TOP_K).transpose(0, 2, 1).reshape(-1)
    tile_start = jnp.arange(n_rows // te, dtype=I32) * te
    tile_expert = jnp.minimum(jnp.sum((ends[None, :] <= tile_start[:, None]).astype(I32), axis=1),
                              N_EXPERTS - 1)
    n_used = (ends[-1:] // te).astype(I32)

    xs = _dispatch(pos_flat, xn, n_rows)
    y = _experts(tile_expert, n_used, xs, w1g, w1l, b1g, b1l, w2, b2)
    out = _combine(pos_flat, h1, gates, p.reshape(n, PLE_DIM), w_ple_gate.astype(BF16),
                   w_ple_proj.astype(BF16), row(g_ple), row(g_final), y, final_norm)
    return out.reshape(b, s, d)


def kernel(x, p, positions, g_attn, w_in, lambda_q1, lambda_k1, lambda_q2, lambda_k2, g_subln, b_forget, w_out, g_moe, w_router, b_router, w_e1, b_e1, w_e2, b_e2, g_ple, w_ple_gate, w_ple_proj, g_final):
    depth = g_attn.shape[0]
    cosf, sinf = _rope_tables(positions)
    h = x
    for i in range(depth):
        lambda_init = 0.8 - 0.6 * math.exp(-0.3 * i)
        h = _layer(h, p[i], cosf, sinf, g_attn[i], w_in[i], lambda_q1[i], lambda_k1[i],
                   lambda_q2[i], lambda_k2[i], g_subln[i], b_forget[i], w_out[i], g_moe[i],
                   w_router[i], b_router[i], w_e1[i], b_e1[i], w_e2[i], b_e2[i], g_ple[i],
                   w_ple_gate[i], w_ple_proj[i], g_final, lambda_init, i == depth - 1)
    return h
```

```python
import functools
import math

import jax
import jax.numpy as jnp
from jax import lax
from jax.experimental import pallas as pl
from jax.experimental.pallas import tpu as pltpu

F32 = jnp.float32
BF16 = jnp.bfloat16
I32 = jnp.int32

LANES = 128
D_MODEL = 1024
HEAD_DIM = 64
N_DIFF_HEADS = 4
DIFF_V_DIM = 2 * HEAD_DIM
N_FOX_HEADS = 8
DIFF_WIDTH = N_DIFF_HEADS * DIFF_V_DIM
FOX_WIDTH = N_FOX_HEADS * HEAD_DIM
ROT_DIM = HEAD_DIM // 4
ROPE_THETA = 500000.0
N_EXPERTS = 32
TOP_K = 4
D_EXPERT = D_MODEL
SWIGLU_ALPHA = 1.702
SWIGLU_LIMIT = 7.0
PLE_DIM = 256
NORM_EPS = 1e-5
MAIN_WIDTH = 3 * DIFF_WIDTH + 3 * FOX_WIDTH

SEQ_TILE = 256
ATTN_TILE = 256
TOKEN_TILE = 256
EXPERT_TILE = 256
VMEM_LIMIT = 48 * 1024 * 1024


def _params(*semantics):
    return pltpu.CompilerParams(dimension_semantics=semantics, vmem_limit_bytes=VMEM_LIMIT)


def _rms(x, g):
    return x * lax.rsqrt(jnp.mean(x * x, axis=-1, keepdims=True) + NORM_EPS) * g


def _rope_table_kernel(pos_ref, cos_ref, sin_ref):
    pos = pos_ref[0].astype(F32)
    d = lax.broadcasted_iota(I32, (1, LANES), 1) % HEAD_DIM
    j = d % (ROT_DIM // 2)
    inv_freq = jnp.power(ROPE_THETA, -(2 * j).astype(F32) / ROT_DIM)
    ang = pos * inv_freq
    c, s = jnp.cos(ang), jnp.sin(ang)
    cos_ref[0] = jnp.where(d < ROT_DIM, c, 1.0)
    sin_ref[0] = jnp.where(d < ROT_DIM // 2, -s, jnp.where(d < ROT_DIM, s, 0.0))


def _rope_tables(positions):
    b, s = positions.shape
    ts = SEQ_TILE
    return pl.pallas_call(
        _rope_table_kernel,
        grid=(b, s // ts),
        in_specs=[pl.BlockSpec((1, ts, 1), lambda i, j: (i, j, 0))],
        out_specs=[pl.BlockSpec((1, ts, LANES), lambda i, j: (i, j, 0))] * 2,
        out_shape=[jax.ShapeDtypeStruct((b, s, LANES), F32)] * 2,
        compiler_params=_params("arbitrary", "arbitrary"),
        name="rope_tables",
    )(positions.reshape(b, s, 1))


def _inproj_kernel(x_ref, g_ref, w_ref, wfz_ref, bfz_ref, cos_ref, sin_ref,
                   dq_ref, dk_ref, dv_ref, fq_ref, fk_ref, fv_ref, c_ref, carry_ref):
    ts = x_ref.shape[1]
    xb = _rms(x_ref[0], g_ref[...]).astype(BF16)
    proj = jnp.dot(xb, w_ref[...], preferred_element_type=F32)

    cosf, sinf = cos_ref[0], sin_ref[0]
    first_half = lax.broadcasted_iota(I32, (1, LANES), 1) % HEAD_DIM < ROT_DIM // 2

    def rope(t):
        outs = []
        for j in range(t.shape[1] // LANES):
            tj = t[:, j * LANES:(j + 1) * LANES]
            partner = jnp.where(first_half,
                                pltpu.roll(tj, LANES - ROT_DIM // 2, 1),
                                pltpu.roll(tj, ROT_DIM // 2, 1))
            outs.append(tj * cosf + partner * sinf)
        return jnp.concatenate(outs, axis=1)

    scale = HEAD_DIM ** -0.5
    w = DIFF_WIDTH
    dq_ref[0] = (rope(proj[:, 0:w]) * scale).astype(BF16)
    dk_ref[0] = rope(proj[:, w:2 * w]).astype(BF16)
    dv_ref[0] = proj[:, 2 * w:3 * w].astype(BF16)
    fq_ref[0] = (proj[:, 3 * w:4 * w] * scale).astype(BF16)
    fk_ref[0] = proj[:, 4 * w:5 * w].astype(BF16)
    fv_ref[0] = proj[:, 5 * w:6 * w].astype(BF16)

    fz = jnp.dot(xb, wfz_ref[...], preferred_element_type=F32) + bfz_ref[...]
    logf = jnp.minimum(fz, 0.0) - jnp.log1p(jnp.exp(-jnp.abs(fz)))
    row = lax.broadcasted_iota(I32, (ts, ts), 0)
    col = lax.broadcasted_iota(I32, (ts, ts), 1)
    tri = jnp.where(col <= row, 1.0, 0.0).astype(BF16)
    hi = logf.astype(BF16)
    r1 = logf - hi.astype(F32)
    mid = r1.astype(BF16)
    lo = (r1 - mid.astype(F32)).astype(BF16)
    cum = (jnp.dot(tri, hi, preferred_element_type=F32)
           + jnp.dot(tri, mid, preferred_element_type=F32)
           + jnp.dot(tri, lo, preferred_element_type=F32))

    @pl.when(pl.program_id(1) == 0)
    def _():
        carry_ref[...] = jnp.zeros_like(carry_ref)

    c = cum + carry_ref[...]
    c_ref[0] = c
    carry_ref[...] = c[ts - 1:ts, :]


def _inproj(x, g, w_main, w_fz, b_fz, cosf, sinf):
    b, s, d = x.shape
    ts = SEQ_TILE
    tile = lambda width: pl.BlockSpec((1, ts, width), lambda i, j: (i, j, 0))
    full = lambda arr: pl.BlockSpec(arr.shape, lambda i, j: (0,) * arr.ndim)
    heads = jax.ShapeDtypeStruct((b, s, DIFF_WIDTH), BF16)
    return pl.pallas_call(
        _inproj_kernel,
        grid=(b, s // ts),
        in_specs=[tile(d), full(g), full(w_main), full(w_fz), full(b_fz), tile(LANES), tile(LANES)],
        out_specs=[tile(DIFF_WIDTH)] * 6 + [tile(FOX_WIDTH)],
        out_shape=[heads] * 6 + [jax.ShapeDtypeStruct((b, s, FOX_WIDTH), F32)],
        scratch_shapes=[pltpu.VMEM((1, FOX_WIDTH), F32)],
        compiler_params=_params("arbitrary", "arbitrary"),
        name="inproj",
    )(x, g, w_main, w_fz, b_fz, cosf, sinf)


def _stack_maps(q):
    lane = lax.broadcasted_iota(I32, q.shape, 1)
    zero = jnp.zeros_like(q)
    return jnp.concatenate([jnp.where(lane < HEAD_DIM, q, zero),
                            jnp.where(lane >= HEAD_DIM, q, zero)], axis=0)


def _causal_softmax_pv(qs, k_ref, v_ref, qi, tile, bias_fn):
    rows = qs.shape[0]

    def step(j, carry, diagonal):
        m, l, acc = carry
        start = pl.multiple_of(j * tile, tile)
        ks = k_ref[0, pl.ds(start, tile), :]
        vs = v_ref[0, pl.ds(start, tile), :]
        s = lax.dot_general(qs, ks, (((1,), (1,)), ((), ())), preferred_element_type=F32)
        if bias_fn is not None:
            s = s + bias_fn(start)
        if diagonal:
            r = lax.broadcasted_iota(I32, s.shape, 0) & (tile - 1)
            c = lax.broadcasted_iota(I32, s.shape, 1)
            s = jnp.where(c <= r, s, -jnp.inf)
        m_new = jnp.maximum(m, jnp.max(s, axis=-1, keepdims=True))
        alpha = jnp.exp(m - m_new)
        p = jnp.exp(s - m_new)
        l = alpha * l + jnp.sum(p, axis=-1, keepdims=True)
        acc = alpha * acc + jnp.dot(p.astype(BF16), vs, preferred_element_type=F32)
        return m_new, l, acc

    init = (jnp.full((rows, 1), -jnp.inf, F32), jnp.zeros((rows, 1), F32),
            jnp.zeros((rows, LANES), F32))
    carry = lax.fori_loop(0, qi, lambda j, c: step(j, c, False), init)
    _, l, acc = step(qi, carry, True)
    return l, acc


def _diff_attn_kernel(lq1_ref, lk1_ref, lq2_ref, lk2_ref, q_ref, k_ref, v_ref, g_ref, o_ref,
                      *, lambda_init):
    tq = q_ref.shape[1]
    l, acc = _causal_softmax_pv(_stack_maps(q_ref[0]), k_ref, v_ref, pl.program_id(2), tq, None)
    lam = (jnp.exp(jnp.sum(lq1_ref[...] * lk1_ref[...], axis=-1, keepdims=True))
           - jnp.exp(jnp.sum(lq2_ref[...] * lk2_ref[...], axis=-1, keepdims=True)) + lambda_init)
    o = acc[:tq] / l[:tq] - lam * (acc[tq:] / l[tq:])
    o_ref[0] = (_rms(o, g_ref[...]) * (1.0 - lambda_init)).astype(BF16)


def _diff_attention(dq, dk, dv, lq1, lk1, lq2, lk2, g_subln, lambda_init):
    b, s, _ = dq.shape
    t = ATTN_TILE
    vec = lambda arr: pl.BlockSpec(arr.shape, lambda i, h, j: (0, 0))
    return pl.pallas_call(
        functools.partial(_diff_attn_kernel, lambda_init=lambda_init),
        grid=(b, N_DIFF_HEADS, s // t),
        in_specs=[vec(lq1), vec(lk1), vec(lq2), vec(lk2),
                  pl.BlockSpec((1, t, LANES), lambda i, h, j: (i, j, h)),
                  pl.BlockSpec((1, s, LANES), lambda i, h, j: (i, 0, h)),
                  pl.BlockSpec((1, s, LANES), lambda i, h, j: (i, 0, h)),
                  vec(g_subln)],
        out_specs=pl.BlockSpec((1, t, LANES), lambda i, h, j: (i, j, h)),
        out_shape=jax.ShapeDtypeStruct((b, s, DIFF_WIDTH), BF16),
        compiler_params=_params("arbitrary", "arbitrary", "arbitrary"),
        name="diff_attention",
    )(lq1, lk1, lq2, lk2, dq, dk, dv, g_subln)


def _fox_attn_kernel(q_ref, k_ref, v_ref, cq_ref, ck_ref, o_ref):
    tq = q_ref.shape[1]
    cq = cq_ref[0]
    bias_q = jnp.concatenate([cq[:, 0:1], cq[:, HEAD_DIM:HEAD_DIM + 1]], axis=0)

    def bias(start):
        ck = ck_ref[0, 0, :, pl.ds(start, tq)]
        ck = jnp.concatenate([jnp.broadcast_to(ck[0:1], (tq, tq)),
                              jnp.broadcast_to(ck[1:2], (tq, tq))], axis=0)
        return bias_q - ck

    l, acc = _causal_softmax_pv(_stack_maps(q_ref[0]), k_ref, v_ref, pl.program_id(2), tq, bias)
    lane = lax.broadcasted_iota(I32, (tq, LANES), 1)
    o_ref[0] = jnp.where(lane < HEAD_DIM, acc[:tq] / l[:tq], acc[tq:] / l[tq:]).astype(BF16)


def _fox_attention(fq, fk, fv, c_cols, c_rows):
    b, s, _ = fq.shape
    t = ATTN_TILE
    return pl.pallas_call(
        _fox_attn_kernel,
        grid=(b, N_FOX_HEADS // 2, s // t),
        in_specs=[pl.BlockSpec((1, t, LANES), lambda i, h, j: (i, j, h)),
                  pl.BlockSpec((1, s, LANES), lambda i, h, j: (i, 0, h)),
                  pl.BlockSpec((1, s, LANES), lambda i, h, j: (i, 0, h)),
                  pl.BlockSpec((1, t, LANES), lambda i, h, j: (i, j, h)),
                  pl.BlockSpec((1, 1, 2, s), lambda i, h, j: (i, h, 0, 0))],
        out_specs=pl.BlockSpec((1, t, LANES), lambda i, h, j: (i, j, h)),
        out_shape=jax.ShapeDtypeStruct((b, s, FOX_WIDTH), BF16),
        compiler_params=_params("arbitrary", "arbitrary", "arbitrary"),
        name="fox_attention",
    )(fq, fk, fv, c_cols, c_rows)


def _router_kernel(d_ref, f_ref, x_ref, wo_ref, g_ref, wr_ref, br_ref,
                   h_ref, xn_ref, gate_ref, idx_ref, rank_ref, count_ref, carry_ref):
    tm = x_ref.shape[0]
    mixed = (jnp.dot(d_ref[...], wo_ref[0:DIFF_WIDTH, :], preferred_element_type=F32)
             + jnp.dot(f_ref[...], wo_ref[DIFF_WIDTH:, :], preferred_element_type=F32))
    h = x_ref[...] + mixed
    h_ref[...] = h
    xn = _rms(h, g_ref[...])
    xn_ref[...] = xn

    xh = xn.astype(BF16)
    xl = (xn - xh.astype(F32)).astype(BF16)
    wr = wr_ref[...]
    wh = wr.astype(BF16)
    wl = (wr - wh.astype(F32)).astype(BF16)
    logits = (jnp.dot(xh, wh, preferred_element_type=F32)
              + jnp.dot(xl, wh, preferred_element_type=F32)
              + jnp.dot(xh, wl, preferred_element_type=F32)) + br_ref[...]
    lane = lax.broadcasted_iota(I32, (tm, LANES), 1)
    work = jnp.where(lane < N_EXPERTS, logits, -jnp.inf)

    vals, picks = [], []
    for _ in range(TOP_K):
        m = jnp.max(work, axis=-1, keepdims=True)
        idx = jnp.min(jnp.where(work == m, lane, LANES), axis=-1, keepdims=True)
        pick = lane == idx
        vals.append(m)
        picks.append(pick)
        work = jnp.where(pick, -jnp.inf, work)
    exps = [jnp.exp(v - vals[0]) for v in vals]
    denom = exps[0] + exps[1] + exps[2] + exps[3]

    chosen = jnp.where(picks[0] | picks[1] | picks[2] | picks[3], 1.0, 0.0)
    row = lax.broadcasted_iota(I32, (tm, tm), 0)
    col = lax.broadcasted_iota(I32, (tm, tm), 1)
    before = jnp.where(col < row, 1.0, 0.0).astype(BF16)

    @pl.when(pl.program_id(0) == 0)
    def _():
        carry_ref[...] = jnp.zeros_like(carry_ref)

    rank = jnp.dot(before, chosen.astype(BF16), preferred_element_type=F32) + carry_ref[...]
    carry_ref[...] = carry_ref[...] + jnp.sum(chosen, axis=0, keepdims=True)
    count_ref[...] = carry_ref[...]

    gate_out = jnp.zeros((tm, LANES), F32)
    idx_out = jnp.zeros((tm, LANES), I32)
    rank_out = jnp.zeros((tm, LANES), F32)
    for k in range(TOP_K):
        here = lane == k
        gate_out = jnp.where(here, exps[k] / denom, gate_out)
        idx_out = jnp.where(here, jnp.min(jnp.where(picks[k], lane, LANES), axis=-1, keepdims=True), idx_out)
        rank_out = jnp.where(here, jnp.sum(jnp.where(picks[k], rank, 0.0), axis=-1, keepdims=True), rank_out)
    gate_ref[...] = gate_out
    idx_ref[...] = idx_out
    rank_ref[...] = rank_out.astype(I32)


def _router(d_out, f_out, x, w_out, g_moe, w_router, b_router):
    n, d = x.shape
    tm = TOKEN_TILE
    tile = lambda width: pl.BlockSpec((tm, width), lambda i: (i, 0))
    full = lambda arr: pl.BlockSpec(arr.shape, lambda i: (0,) * arr.ndim)
    return pl.pallas_call(
        _router_kernel,
        grid=(n // tm,),
        in_specs=[tile(DIFF_WIDTH), tile(FOX_WIDTH), tile(d), full(w_out), full(g_moe),
                  full(w_router), full(b_router)],
        out_specs=[tile(d), tile(d), tile(LANES), tile(LANES), tile(LANES),
                   pl.BlockSpec((1, LANES), lambda i: (0, 0))],
        out_shape=[jax.ShapeDtypeStruct((n, d), F32), jax.ShapeDtypeStruct((n, d), F32),
                   jax.ShapeDtypeStruct((n, LANES), F32), jax.ShapeDtypeStruct((n, LANES), I32),
                   jax.ShapeDtypeStruct((n, LANES), I32), jax.ShapeDtypeStruct((1, LANES), F32)],
        scratch_shapes=[pltpu.VMEM((1, LANES), F32)],
        compiler_params=_params("arbitrary"),
        name="router",
    )(d_out, f_out, x, w_out, g_moe, w_router, b_router)


def _dispatch_kernel(pos_ref, x_ref, init_ref, xs_ref, sem):
    del init_ref
    tm = x_ref.shape[0]

    def issue(r, carry):
        for k in range(TOP_K):
            dst = pos_ref[k * tm + r]
            pltpu.make_async_copy(x_ref.at[pl.ds(r, 1), :], xs_ref.at[pl.ds(dst, 1), :], sem).start()
        return carry

    lax.fori_loop(0, tm, issue, 0)
    for _ in range(TOP_K):
        pltpu.make_async_copy(x_ref, xs_ref.at[pl.ds(0, tm), :], sem).wait()


def _dispatch(pos_flat, xn, n_rows):
    n, d = xn.shape
    tm = TOKEN_TILE
    return pl.pallas_call(
        _dispatch_kernel,
        grid=(n // tm,),
        in_specs=[pl.BlockSpec((TOP_K * tm,), lambda i: (i,), memory_space=pltpu.SMEM),
                  pl.BlockSpec((tm, d), lambda i: (i, 0)),
                  pl.BlockSpec(memory_space=pl.ANY)],
        out_specs=pl.BlockSpec(memory_space=pl.ANY),
        out_shape=jax.ShapeDtypeStruct((n_rows, d), F32),
        scratch_shapes=[pltpu.SemaphoreType.DMA(())],
        input_output_aliases={2: 0},
        compiler_params=_params("arbitrary"),
        name="dispatch",
    )(pos_flat, xn, jnp.zeros((n_rows, d), F32))


def _expert_kernel(tile_expert_ref, n_used_ref, x_ref, w1_ref, b1_ref, w2_ref, b2_ref, y_ref,
                   w1b_ref):
    t = pl.program_id(0)
    fresh = (t == 0) | (tile_expert_ref[t] != tile_expert_ref[jnp.maximum(t - 1, 0)])
    used = t < n_used_ref[0]

    @pl.when(fresh & used)
    def _():
        rows = LANES
        for r in range(0, w1b_ref.shape[0], rows):
            w1b_ref[r:r + rows, :] = w1_ref[0, r:r + rows, :].astype(BF16)

    @pl.when(used)
    def _():
        xb = x_ref[...].astype(BF16)
        h = jnp.dot(xb, w1b_ref[...], preferred_element_type=F32) + b1_ref[0]
        even = lax.broadcasted_iota(I32, (1, LANES), 1) % 2 == 0
        parts = []
        for q in range(h.shape[1] // (2 * LANES)):
            a = h[:, (2 * q) * LANES:(2 * q + 1) * LANES]
            b = h[:, (2 * q + 1) * LANES:(2 * q + 2) * LANES]
            hg = jnp.where(even, a, pltpu.roll(b, 1, 1))
            hl = jnp.where(even, pltpu.roll(a, LANES - 1, 1), b)
            glu = jnp.minimum(hg, SWIGLU_LIMIT)
            lin = jnp.clip(hl, -SWIGLU_LIMIT, SWIGLU_LIMIT)
            parts.append((glu * jax.nn.sigmoid(SWIGLU_ALPHA * glu) * (lin + 1.0)).astype(BF16))
        act = jnp.concatenate(parts, axis=1)
        y_ref[...] = jnp.dot(act, w2_ref[0], preferred_element_type=F32) + b2_ref[0]

    @pl.when(jnp.logical_not(used))
    def _():
        y_ref[...] = jnp.zeros_like(y_ref)


def _experts(tile_expert, n_used, xs, w1, b1, w2, b2):
    n_rows, d = xs.shape
    tm = EXPERT_TILE
    by_expert = lambda arr: pl.BlockSpec((1,) + arr.shape[1:], lambda i, te, nu: (te[i], 0, 0))
    return pl.pallas_call(
        _expert_kernel,
        grid_spec=pltpu.PrefetchScalarGridSpec(
            num_scalar_prefetch=2,
            grid=(n_rows // tm,),
            in_specs=[pl.BlockSpec((tm, d), lambda i, te, nu: (i, 0)),
                      by_expert(w1), by_expert(b1), by_expert(w2), by_expert(b2)],
            out_specs=pl.BlockSpec((tm, d), lambda i, te, nu: (i, 0)),
            scratch_shapes=[pltpu.VMEM(w1.shape[1:], BF16)]),
        out_shape=jax.ShapeDtypeStruct((n_rows, d), F32),
        compiler_params=_params("arbitrary"),
        name="experts",
    )(tile_expert, n_used, xs, w1, b1, w2, b2)


def _combine_kernel(pos_ref, h_ref, gate_ref, p_ref, wg_ref, wp_ref, gple_ref, gfin_ref, y_ref,
                    o_ref, ybuf, sem, *, final_norm):
    tm = h_ref.shape[0]

    def issue(r, carry):
        for k in range(TOP_K):
            src = pos_ref[k * tm + r]
            pltpu.make_async_copy(y_ref.at[pl.ds(src, 1), :], ybuf.at[k, pl.ds(r, 1), :], sem).start()
        return carry

    lax.fori_loop(0, tm, issue, 0)
    for k in range(TOP_K):
        pltpu.make_async_copy(y_ref.at[pl.ds(0, tm), :], ybuf.at[k], sem).wait()

    gates = gate_ref[...]
    moe = gates[:, 0:1] * ybuf[0]
    for k in range(1, TOP_K):
        moe = moe + gates[:, k:k + 1] * ybuf[k]
    h = h_ref[...] + moe
    gate = jax.nn.sigmoid(jnp.dot(_rms(h, gple_ref[...]).astype(BF16), wg_ref[...],
                                  preferred_element_type=F32))
    emb = jnp.dot(p_ref[...].astype(BF16), wp_ref[...], preferred_element_type=F32)
    h = h + gate * emb
    o_ref[...] = _rms(h, gfin_ref[...]) if final_norm else h


def _combine(pos_flat, h, gates, p, w_gate, w_proj, g_ple, g_final, y, final_norm):
    n, d = h.shape
    tm = TOKEN_TILE
    tile = lambda width: pl.BlockSpec((tm, width), lambda i: (i, 0))
    full = lambda arr: pl.BlockSpec(arr.shape, lambda i: (0,) * arr.ndim)
    return pl.pallas_call(
        functools.partial(_combine_kernel, final_norm=final_norm),
        grid=(n // tm,),
        in_specs=[pl.BlockSpec((TOP_K * tm,), lambda i: (i,), memory_space=pltpu.SMEM),
                  tile(d), tile(LANES), tile(p.shape[1]), full(w_gate), full(w_proj),
                  full(g_ple), full(g_final), pl.BlockSpec(memory_space=pl.ANY)],
        out_specs=tile(d),
        out_shape=jax.ShapeDtypeStruct((n, d), F32),
        scratch_shapes=[pltpu.VMEM((TOP_K, tm, d), F32), pltpu.SemaphoreType.DMA(())],
        compiler_params=_params("arbitrary"),
        name="combine",
    )(pos_flat, h, gates, p, w_gate, w_proj, g_ple, g_final, y)


def _layer(h, p, cosf, sinf, g_attn, w_in, lq1, lk1, lq2, lk2, g_subln, b_forget, w_out, g_moe,
           w_router, b_router, w_e1, b_e1, w_e2, b_e2, g_ple, w_ple_gate, w_ple_proj, g_final,
           lambda_init, final_norm):
    b, s, d = h.shape
    n = b * s
    row = lambda v: v.reshape(1, -1)

    w_main = w_in[:, :MAIN_WIDTH].astype(BF16)
    w_fz = jnp.repeat(w_in[:, MAIN_WIDTH:], HEAD_DIM, axis=1).astype(BF16)
    b_fz = row(jnp.repeat(b_forget, HEAD_DIM))
    half = LANES // 2
    w2 = (w_e2.reshape(N_EXPERTS, D_EXPERT // LANES, 2, half, d).transpose(0, 1, 3, 2, 4)
          .reshape(N_EXPERTS, D_EXPERT, d).astype(BF16))
    b1 = b_e1[:, None, :]
    b2 = b_e2[:, None, :]
    w_router_p = jnp.pad(w_router, ((0, 0), (0, LANES - N_EXPERTS)))
    b_router_p = row(jnp.pad(b_router, (0, LANES - N_EXPERTS)))

    dq, dk, dv, fq, fk, fv, c = _inproj(h, row(g_attn), w_main, w_fz, b_fz, cosf, sinf)
    d_out = _diff_attention(dq, dk, dv, row(lq1), row(lk1), row(lq2), row(lk2), row(g_subln),
                            lambda_init)
    c_rows = c[:, :, ::HEAD_DIM].transpose(0, 2, 1).reshape(b, N_FOX_HEADS // 2, 2, s)
    f_out = _fox_attention(fq, fk, fv, c, c_rows)

    h1, xn, gates, idx, rank, counts = _router(
        d_out.reshape(n, DIFF_WIDTH), f_out.reshape(n, FOX_WIDTH), h.reshape(n, d),
        w_out.astype(BF16), row(g_moe), w_router_p, b_router_p)

    te = EXPERT_TILE
    n_rows = n * TOP_K + N_EXPERTS * te
    counts = counts[0, :N_EXPERTS].astype(I32)
    padded = (counts + te - 1) // te * te
    ends = jnp.cumsum(padded)
    starts = ends - padded
    pos = starts[idx[:, :TOP_K]] + rank[:, :TOP_K]
    tm = TOKEN_TILE
    pos_flat = pos.reshape(n // tm, tm, TOP_K).transpose(0, 2, 1).reshape(-1)
    tile_start = jnp.arange(n_rows // te, dtype=I32) * te
    tile_expert = jnp.minimum(jnp.sum((ends[None, :] <= tile_start[:, None]).astype(I32), axis=1),
                              N_EXPERTS - 1)
    n_used = (ends[-1:] // te).astype(I32)

    xs = _dispatch(pos_flat, xn, n_rows)
    y = _experts(tile_expert, n_used, xs, w_e1, b1, w2, b2)
    out = _combine(pos_flat, h1, gates, p.reshape(n, PLE_DIM), w_ple_gate.astype(BF16),
                   w_ple_proj.astype(BF16), row(g_ple), row(g_final), y, final_norm)
    return out.reshape(b, s, d)


def kernel(x, p, positions, g_attn, w_in, lambda_q1, lambda_k1, lambda_q2, lambda_k2, g_subln, b_forget, w_out, g_moe, w_router, b_router, w_e1, b_e1, w_e2, b_e2, g_ple, w_ple_gate, w_ple_proj, g_final):
    depth = g_attn.shape[0]
    cosf, sinf = _rope_tables(positions)
    h = x
    for i in range(depth):
        lambda_init = 0.8 - 0.6 * math.exp(-0.3 * i)
        h = _layer(h, p[i], cosf, sinf, g_attn[i], w_in[i], lambda_q1[i], lambda_k1[i],
                   lambda_q2[i], lambda_k2[i], g_subln[i], b_forget[i], w_out[i], g_moe[i],
                   w_router[i], b_router[i], w_e1[i], b_e1[i], w_e2[i], b_e2[i], g_ple[i],
                   w_ple_gate[i], w_ple_proj[i], g_final, lambda_init, i == depth - 1)
    return h
```

```python
import functools
import math

import jax
import jax.numpy as jnp
from jax import lax
from jax.experimental import pallas as pl
from jax.experimental.pallas import tpu as pltpu

F32 = jnp.float32
BF16 = jnp.bfloat16
I32 = jnp.int32

LANES = 128
D_MODEL = 1024
HEAD_DIM = 64
N_DIFF_HEADS = 4
DIFF_V_DIM = 2 * HEAD_DIM
N_FOX_HEADS = 8
DIFF_WIDTH = N_DIFF_HEADS * DIFF_V_DIM
FOX_WIDTH = N_FOX_HEADS * HEAD_DIM
ROT_DIM = HEAD_DIM // 4
ROPE_THETA = 500000.0
N_EXPERTS = 32
TOP_K = 4
D_EXPERT = D_MODEL
SWIGLU_ALPHA = 1.702
SWIGLU_LIMIT = 7.0
PLE_DIM = 256
NORM_EPS = 1e-5
MAIN_WIDTH = 3 * DIFF_WIDTH + 3 * FOX_WIDTH

SEQ_TILE = 256
ATTN_TILE = 256
TOKEN_TILE = 256
EXPERT_TILE = 256
VMEM_LIMIT = 48 * 1024 * 1024


def _params(*semantics):
    return pltpu.CompilerParams(dimension_semantics=semantics, vmem_limit_bytes=VMEM_LIMIT)


def _rms(x, g):
    return x * lax.rsqrt(jnp.mean(x * x, axis=-1, keepdims=True) + NORM_EPS) * g


def _rope_table_kernel(pos_ref, cos_ref, sin_ref):
    pos = pos_ref[0].astype(F32)
    d = lax.broadcasted_iota(I32, (1, LANES), 1) % HEAD_DIM
    j = d % (ROT_DIM // 2)
    inv_freq = jnp.power(ROPE_THETA, -(2 * j).astype(F32) / ROT_DIM)
    ang = pos * inv_freq
    c, s = jnp.cos(ang), jnp.sin(ang)
    cos_ref[0] = jnp.where(d < ROT_DIM, c, 1.0)
    sin_ref[0] = jnp.where(d < ROT_DIM // 2, -s, jnp.where(d < ROT_DIM, s, 0.0))


def _rope_tables(positions):
    b, s = positions.shape
    ts = SEQ_TILE
    return pl.pallas_call(
        _rope_table_kernel,
        grid=(b, s // ts),
        in_specs=[pl.BlockSpec((1, ts, 1), lambda i, j: (i, j, 0))],
        out_specs=[pl.BlockSpec((1, ts, LANES), lambda i, j: (i, j, 0))] * 2,
        out_shape=[jax.ShapeDtypeStruct((b, s, LANES), F32)] * 2,
        compiler_params=_params("arbitrary", "arbitrary"),
        name="rope_tables",
    )(positions.reshape(b, s, 1))


def _inproj_kernel(x_ref, g_ref, w_ref, wfz_ref, bfz_ref, cos_ref, sin_ref,
                   dq_ref, dk_ref, dv_ref, fq_ref, fk_ref, fv_ref, c_ref, carry_ref):
    ts = x_ref.shape[1]
    xb = _rms(x_ref[0], g_ref[...]).astype(BF16)
    proj = jnp.dot(xb, w_ref[...], preferred_element_type=F32)

    cosf, sinf = cos_ref[0], sin_ref[0]
    first_half = lax.broadcasted_iota(I32, (1, LANES), 1) % HEAD_DIM < ROT_DIM // 2

    def rope(t):
        outs = []
        for j in range(t.shape[1] // LANES):
            tj = t[:, j * LANES:(j + 1) * LANES]
            partner = jnp.where(first_half,
                                pltpu.roll(tj, LANES - ROT_DIM // 2, 1),
                                pltpu.roll(tj, ROT_DIM // 2, 1))
            outs.append(tj * cosf + partner * sinf)
        return jnp.concatenate(outs, axis=1)

    scale = HEAD_DIM ** -0.5
    w = DIFF_WIDTH
    dq_ref[0] = (rope(proj[:, 0:w]) * scale).astype(BF16)
    dk_ref[0] = rope(proj[:, w:2 * w]).astype(BF16)
    dv_ref[0] = proj[:, 2 * w:3 * w].astype(BF16)
    fq_ref[0] = (proj[:, 3 * w:4 * w] * scale).astype(BF16)
    fk_ref[0] = proj[:, 4 * w:5 * w].astype(BF16)
    fv_ref[0] = proj[:, 5 * w:6 * w].astype(BF16)

    fz = jnp.dot(xb, wfz_ref[...], preferred_element_type=F32) + bfz_ref[...]
    logf = jnp.minimum(fz, 0.0) - jnp.log1p(jnp.exp(-jnp.abs(fz)))
    row = lax.broadcasted_iota(I32, (ts, ts), 0)
    col = lax.broadcasted_iota(I32, (ts, ts), 1)
    tri = jnp.where(col <= row, 1.0, 0.0).astype(BF16)
    hi = logf.astype(BF16)
    r1 = logf - hi.astype(F32)
    mid = r1.astype(BF16)
    lo = (r1 - mid.astype(F32)).astype(BF16)
    cum = (jnp.dot(tri, hi, preferred_element_type=F32)
           + jnp.dot(tri, mid, preferred_element_type=F32)
           + jnp.dot(tri, lo, preferred_element_type=F32))

    @pl.when(pl.program_id(1) == 0)
    def _():
        carry_ref[...] = jnp.zeros_like(carry_ref)

    c = cum + carry_ref[...]
    c_ref[0] = c
    carry_ref[...] = c[ts - 1:ts, :]


def _inproj(x, g, w_main, w_fz, b_fz, cosf, sinf):
    b, s, d = x.shape
    ts = SEQ_TILE
    tile = lambda width: pl.BlockSpec((1, ts, width), lambda i, j: (i, j, 0))
    full = lambda arr: pl.BlockSpec(arr.shape, lambda i, j: (0,) * arr.ndim)
    heads = jax.ShapeDtypeStruct((b, s, DIFF_WIDTH), BF16)
    return pl.pallas_call(
        _inproj_kernel,
        grid=(b, s // ts),
        in_specs=[tile(d), full(g), full(w_main), full(w_fz), full(b_fz), tile(LANES), tile(LANES)],
        out_specs=[tile(DIFF_WIDTH)] * 6 + [tile(FOX_WIDTH)],
        out_shape=[heads] * 6 + [jax.ShapeDtypeStruct((b, s, FOX_WIDTH), F32)],
        scratch_shapes=[pltpu.VMEM((1, FOX_WIDTH), F32)],
        compiler_params=_params("arbitrary", "arbitrary"),
        name="inproj",
    )(x, g, w_main, w_fz, b_fz, cosf, sinf)


ONES_ROWS = 16


def _block_diag_queries(qt):
    row = lax.broadcasted_iota(I32, qt.shape, 0)
    zero = jnp.zeros_like(qt)
    return jnp.concatenate([jnp.where(row < HEAD_DIM, qt, zero),
                            jnp.where(row >= HEAD_DIM, qt, zero)], axis=1)


def _causal_attention(qbd, k_ref, vt_ref, acc_ref, qi, key_bias, query_bias):
    tk = vt_ref.shape[3]
    tq = qbd.shape[1] // 2
    ones = jnp.ones((ONES_ROWS, tk), BF16)
    acc_ref[...] = jnp.zeros_like(acc_ref)

    def step(i, m, diagonal):
        ks = k_ref[0, pl.ds(pl.multiple_of(i * tk, tk), tk), :]
        s = jnp.dot(ks, qbd, preferred_element_type=F32)
        if key_bias is not None:
            s = s - key_bias(i)
        if diagonal:
            r = lax.broadcasted_iota(I32, s.shape, 0)
            c = lax.broadcasted_iota(I32, s.shape, 1) & (tq - 1)
            s = jnp.where(r <= c, s, -jnp.inf)
        top = jnp.max(s, axis=0, keepdims=True)
        if query_bias is not None:
            top = top + query_bias
        m_new = jnp.maximum(m, top)
        alpha = jnp.exp(m - m_new)
        p = jnp.exp(s - (m_new if query_bias is None else m_new - query_bias)).astype(BF16)
        vts = jnp.concatenate([vt_ref[0, i], ones], axis=0)
        acc_ref[...] = alpha * acc_ref[...] + jnp.dot(vts, p, preferred_element_type=F32)
        return m_new

    m = jnp.full((1, 2 * tq), -jnp.inf, F32)
    m = lax.fori_loop(0, qi, lambda i, m: step(i, m, False), m)
    step(qi, m, True)


def _diff_attn_kernel(lq1_ref, lk1_ref, lq2_ref, lk2_ref, qt_ref, k_ref, vt_ref, g_ref, o_ref,
                      acc_ref, *, lambda_init):
    tq = qt_ref.shape[2]
    _causal_attention(_block_diag_queries(qt_ref[0]), k_ref, vt_ref, acc_ref, pl.program_id(2),
                      None, None)
    lam = (jnp.exp(jnp.sum(lq1_ref[...] * lk1_ref[...], axis=-1, keepdims=True))
           - jnp.exp(jnp.sum(lq2_ref[...] * lk2_ref[...], axis=-1, keepdims=True)) + lambda_init)
    dv = DIFF_V_DIM
    o = (acc_ref[0:dv, 0:tq] / acc_ref[dv:dv + 1, 0:tq]
         - lam * (acc_ref[0:dv, tq:] / acc_ref[dv:dv + 1, tq:]))
    ms = jnp.mean(o * o, axis=0, keepdims=True)
    o_ref[0] = (o * lax.rsqrt(ms + NORM_EPS) * g_ref[...] * (1.0 - lambda_init)).astype(BF16)


def _attention_specs(b, s, t):
    return (pl.BlockSpec((1, LANES, t), lambda i, h, j: (i, h, j)),
            pl.BlockSpec((1, s, LANES), lambda i, h, j: (i, 0, h)),
            pl.BlockSpec((1, s // t, LANES, t), lambda i, h, j: (i, 0, h, 0)),
            pl.BlockSpec((1, LANES, t), lambda i, h, j: (i, h, j)))


def _diff_attention(dqt, dk, dvt, lq1, lk1, lq2, lk2, g_col, lambda_init):
    b, s, _ = dk.shape
    t = ATTN_TILE
    vec = lambda arr: pl.BlockSpec(arr.shape, lambda i, h, j: (0, 0))
    q_spec, k_spec, v_spec, o_spec = _attention_specs(b, s, t)
    return pl.pallas_call(
        functools.partial(_diff_attn_kernel, lambda_init=lambda_init),
        grid=(b, N_DIFF_HEADS, s // t),
        in_specs=[vec(lq1), vec(lk1), vec(lq2), vec(lk2), q_spec, k_spec, v_spec, vec(g_col)],
        out_specs=o_spec,
        out_shape=jax.ShapeDtypeStruct((b, DIFF_WIDTH, s), BF16),
        scratch_shapes=[pltpu.VMEM((LANES + ONES_ROWS, 2 * t), F32)],
        compiler_params=_params("arbitrary", "arbitrary", "arbitrary"),
        name="diff_attention",
    )(lq1, lk1, lq2, lk2, dqt, dk, dvt, g_col)


def _fox_attn_kernel(qt_ref, k_ref, vt_ref, cq_ref, ck_ref, o_ref, acc_ref):
    tq = qt_ref.shape[2]
    tk = vt_ref.shape[3]
    cq = jnp.concatenate([cq_ref[0, 0, 0:1, :], cq_ref[0, 0, 1:2, :]], axis=1)
    lane = lax.broadcasted_iota(I32, (1, LANES), 1)

    def key_bias(i):
        ck = ck_ref[0, pl.ds(pl.multiple_of(i * tk, tk), tk), :]
        other = pltpu.roll(ck, HEAD_DIM, 1)
        first = jnp.where(lane < HEAD_DIM, ck, other)
        second = jnp.where(lane >= HEAD_DIM, ck, other)
        return jnp.concatenate([first] * (tq // LANES) + [second] * (tq // LANES), axis=1)

    _causal_attention(_block_diag_queries(qt_ref[0]), k_ref, vt_ref, acc_ref, pl.program_id(2),
                      key_bias, cq)
    h = HEAD_DIM
    o_ref[0] = jnp.concatenate([acc_ref[0:h, 0:tq] / acc_ref[2 * h:2 * h + 1, 0:tq],
                                acc_ref[h:2 * h, tq:] / acc_ref[2 * h:2 * h + 1, tq:]],
                               axis=0).astype(BF16)


def _fox_attention(fqt, fk, fvt, c_rows, c_cols):
    b, s, _ = fk.shape
    t = ATTN_TILE
    q_spec, k_spec, v_spec, o_spec = _attention_specs(b, s, t)
    return pl.pallas_call(
        _fox_attn_kernel,
        grid=(b, N_FOX_HEADS // 2, s // t),
        in_specs=[q_spec, k_spec, v_spec,
                  pl.BlockSpec((1, 1, 2, t), lambda i, h, j: (i, h, 0, j)),
                  pl.BlockSpec((1, s, LANES), lambda i, h, j: (i, 0, h))],
        out_specs=o_spec,
        out_shape=jax.ShapeDtypeStruct((b, FOX_WIDTH, s), BF16),
        scratch_shapes=[pltpu.VMEM((LANES + ONES_ROWS, 2 * t), F32)],
        compiler_params=_params("arbitrary", "arbitrary", "arbitrary"),
        name="fox_attention",
    )(fqt, fk, fvt, c_rows, c_cols)


def _router_kernel(d_ref, f_ref, x_ref, wo_ref, g_ref, wr_ref, br_ref,
                   h_ref, xn_ref, gate_ref, idx_ref, rank_ref, count_ref, carry_ref):
    tm = x_ref.shape[0]
    mixed = (jnp.dot(d_ref[...], wo_ref[0:DIFF_WIDTH, :], preferred_element_type=F32)
             + jnp.dot(f_ref[...], wo_ref[DIFF_WIDTH:, :], preferred_element_type=F32))
    h = x_ref[...] + mixed
    h_ref[...] = h
    xn = _rms(h, g_ref[...])
    xn_ref[...] = xn

    xh = xn.astype(BF16)
    xl = (xn - xh.astype(F32)).astype(BF16)
    wr = wr_ref[...]
    wh = wr.astype(BF16)
    wl = (wr - wh.astype(F32)).astype(BF16)
    logits = (jnp.dot(xh, wh, preferred_element_type=F32)
              + jnp.dot(xl, wh, preferred_element_type=F32)
              + jnp.dot(xh, wl, preferred_element_type=F32)) + br_ref[...]
    lane = lax.broadcasted_iota(I32, (tm, LANES), 1)
    work = jnp.where(lane < N_EXPERTS, logits, -jnp.inf)

    vals, picks = [], []
    for _ in range(TOP_K):
        m = jnp.max(work, axis=-1, keepdims=True)
        idx = jnp.min(jnp.where(work == m, lane, LANES), axis=-1, keepdims=True)
        pick = lane == idx
        vals.append(m)
        picks.append(pick)
        work = jnp.where(pick, -jnp.inf, work)
    exps = [jnp.exp(v - vals[0]) for v in vals]
    denom = exps[0] + exps[1] + exps[2] + exps[3]

    chosen = jnp.where(picks[0] | picks[1] | picks[2] | picks[3], 1.0, 0.0)
    row = lax.broadcasted_iota(I32, (tm, tm), 0)
    col = lax.broadcasted_iota(I32, (tm, tm), 1)
    before = jnp.where(col < row, 1.0, 0.0).astype(BF16)

    @pl.when(pl.program_id(0) == 0)
    def _():
        carry_ref[...] = jnp.zeros_like(carry_ref)

    rank = jnp.dot(before, chosen.astype(BF16), preferred_element_type=F32) + carry_ref[...]
    carry_ref[...] = carry_ref[...] + jnp.sum(chosen, axis=0, keepdims=True)
    count_ref[...] = carry_ref[...]

    gate_out = jnp.zeros((tm, LANES), F32)
    idx_out = jnp.zeros((tm, LANES), I32)
    rank_out = jnp.zeros((tm, LANES), F32)
    for k in range(TOP_K):
        here = lane == k
        gate_out = jnp.where(here, exps[k] / denom, gate_out)
        idx_out = jnp.where(here, jnp.min(jnp.where(picks[k], lane, LANES), axis=-1, keepdims=True), idx_out)
        rank_out = jnp.where(here, jnp.sum(jnp.where(picks[k], rank, 0.0), axis=-1, keepdims=True), rank_out)
    gate_ref[...] = gate_out
    idx_ref[...] = idx_out
    rank_ref[...] = rank_out.astype(I32)


def _router(d_out, f_out, x, w_out, g_moe, w_router, b_router):
    n, d = x.shape
    tm = TOKEN_TILE
    tile = lambda width: pl.BlockSpec((tm, width), lambda i: (i, 0))
    full = lambda arr: pl.BlockSpec(arr.shape, lambda i: (0,) * arr.ndim)
    return pl.pallas_call(
        _router_kernel,
        grid=(n // tm,),
        in_specs=[tile(DIFF_WIDTH), tile(FOX_WIDTH), tile(d), full(w_out), full(g_moe),
                  full(w_router), full(b_router)],
        out_specs=[tile(d), tile(d), tile(LANES), tile(LANES), tile(LANES),
                   pl.BlockSpec((1, LANES), lambda i: (0, 0))],
        out_shape=[jax.ShapeDtypeStruct((n, d), F32), jax.ShapeDtypeStruct((n, d), F32),
                   jax.ShapeDtypeStruct((n, LANES), F32), jax.ShapeDtypeStruct((n, LANES), I32),
                   jax.ShapeDtypeStruct((n, LANES), I32), jax.ShapeDtypeStruct((1, LANES), F32)],
        scratch_shapes=[pltpu.VMEM((1, LANES), F32)],
        compiler_params=_params("arbitrary"),
        name="router",
    )(d_out, f_out, x, w_out, g_moe, w_router, b_router)


def _dispatch_kernel(pos_ref, x_ref, init_ref, xs_ref, sem):
    del init_ref
    tm = x_ref.shape[0]

    def issue(r, carry):
        for k in range(TOP_K):
            dst = pos_ref[k * tm + r]
            pltpu.make_async_copy(x_ref.at[pl.ds(r, 1), :], xs_ref.at[pl.ds(dst, 1), :], sem).start()
        return carry

    lax.fori_loop(0, tm, issue, 0)
    for _ in range(TOP_K):
        pltpu.make_async_copy(x_ref, xs_ref.at[pl.ds(0, tm), :], sem).wait()


def _dispatch(pos_flat, xn, n_rows):
    n, d = xn.shape
    tm = TOKEN_TILE
    return pl.pallas_call(
        _dispatch_kernel,
        grid=(n // tm,),
        in_specs=[pl.BlockSpec((TOP_K * tm,), lambda i: (i,), memory_space=pltpu.SMEM),
                  pl.BlockSpec((tm, d), lambda i: (i, 0)),
                  pl.BlockSpec(memory_space=pl.ANY)],
        out_specs=pl.BlockSpec(memory_space=pl.ANY),
        out_shape=jax.ShapeDtypeStruct((n_rows, d), F32),
        scratch_shapes=[pltpu.SemaphoreType.DMA(())],
        input_output_aliases={2: 0},
        compiler_params=_params("arbitrary"),
        name="dispatch",
    )(pos_flat, xn, jnp.zeros((n_rows, d), F32))


def _expert_kernel(tile_expert_ref, n_used_ref, x_ref, w1_ref, b1_ref, w2_ref, b2_ref, y_ref,
                   w1b_ref):
    t = pl.program_id(0)
    fresh = (t == 0) | (tile_expert_ref[t] != tile_expert_ref[jnp.maximum(t - 1, 0)])
    used = t < n_used_ref[0]

    @pl.when(fresh & used)
    def _():
        rows = LANES
        for r in range(0, w1b_ref.shape[0], rows):
            w1b_ref[r:r + rows, :] = w1_ref[0, r:r + rows, :].astype(BF16)

    @pl.when(used)
    def _():
        xb = x_ref[...].astype(BF16)
        h = jnp.dot(xb, w1b_ref[...], preferred_element_type=F32) + b1_ref[0]
        even = lax.broadcasted_iota(I32, (1, LANES), 1) % 2 == 0
        parts = []
        for q in range(h.shape[1] // (2 * LANES)):
            a = h[:, (2 * q) * LANES:(2 * q + 1) * LANES]
            b = h[:, (2 * q + 1) * LANES:(2 * q + 2) * LANES]
            hg = jnp.where(even, a, pltpu.roll(b, 1, 1))
            hl = jnp.where(even, pltpu.roll(a, LANES - 1, 1), b)
            glu = jnp.minimum(hg, SWIGLU_LIMIT)
            lin = jnp.clip(hl, -SWIGLU_LIMIT, SWIGLU_LIMIT)
            parts.append((glu * jax.nn.sigmoid(SWIGLU_ALPHA * glu) * (lin + 1.0)).astype(BF16))
        act = jnp.concatenate(parts, axis=1)
        y_ref[...] = jnp.dot(act, w2_ref[0], preferred_element_type=F32) + b2_ref[0]

    @pl.when(jnp.logical_not(used))
    def _():
        y_ref[...] = jnp.zeros_like(y_ref)


def _experts(tile_expert, n_used, xs, w1, b1, w2, b2):
    n_rows, d = xs.shape
    tm = EXPERT_TILE
    by_expert = lambda arr: pl.BlockSpec((1,) + arr.shape[1:], lambda i, te, nu: (te[i], 0, 0))
    return pl.pallas_call(
        _expert_kernel,
        grid_spec=pltpu.PrefetchScalarGridSpec(
            num_scalar_prefetch=2,
            grid=(n_rows // tm,),
            in_specs=[pl.BlockSpec((tm, d), lambda i, te, nu: (i, 0)),
                      by_expert(w1), by_expert(b1), by_expert(w2), by_expert(b2)],
            out_specs=pl.BlockSpec((tm, d), lambda i, te, nu: (i, 0)),
            scratch_shapes=[pltpu.VMEM(w1.shape[1:], BF16)]),
        out_shape=jax.ShapeDtypeStruct((n_rows, d), F32),
        compiler_params=_params("arbitrary"),
        name="experts",
    )(tile_expert, n_used, xs, w1, b1, w2, b2)


def _combine_kernel(pos_ref, h_ref, gate_ref, p_ref, wg_ref, wp_ref, gple_ref, gfin_ref, y_ref,
                    o_ref, ybuf, sem, *, final_norm):
    tm = h_ref.shape[0]

    def issue(r, carry):
        for k in range(TOP_K):
            src = pos_ref[k * tm + r]
            pltpu.make_async_copy(y_ref.at[pl.ds(src, 1), :], ybuf.at[k, pl.ds(r, 1), :], sem).start()
        return carry

    lax.fori_loop(0, tm, issue, 0)
    for k in range(TOP_K):
        pltpu.make_async_copy(y_ref.at[pl.ds(0, tm), :], ybuf.at[k], sem).wait()

    gates = gate_ref[...]
    moe = gates[:, 0:1] * ybuf[0]
    for k in range(1, TOP_K):
        moe = moe + gates[:, k:k + 1] * ybuf[k]
    h = h_ref[...] + moe
    gate = jax.nn.sigmoid(jnp.dot(_rms(h, gple_ref[...]).astype(BF16), wg_ref[...],
                                  preferred_element_type=F32))
    emb = jnp.dot(p_ref[...].astype(BF16), wp_ref[...], preferred_element_type=F32)
    h = h + gate * emb
    o_ref[...] = _rms(h, gfin_ref[...]) if final_norm else h


def _combine(pos_flat, h, gates, p, w_gate, w_proj, g_ple, g_final, y, final_norm):
    n, d = h.shape
    tm = TOKEN_TILE
    tile = lambda width: pl.BlockSpec((tm, width), lambda i: (i, 0))
    full = lambda arr: pl.BlockSpec(arr.shape, lambda i: (0,) * arr.ndim)
    return pl.pallas_call(
        functools.partial(_combine_kernel, final_norm=final_norm),
        grid=(n // tm,),
        in_specs=[pl.BlockSpec((TOP_K * tm,), lambda i: (i,), memory_space=pltpu.SMEM),
                  tile(d), tile(LANES), tile(p.shape[1]), full(w_gate), full(w_proj),
                  full(g_ple), full(g_final), pl.BlockSpec(memory_space=pl.ANY)],
        out_specs=tile(d),
        out_shape=jax.ShapeDtypeStruct((n, d), F32),
        scratch_shapes=[pltpu.VMEM((TOP_K, tm, d), F32), pltpu.SemaphoreType.DMA(())],
        compiler_params=_params("arbitrary"),
        name="combine",
    )(pos_flat, h, gates, p, w_gate, w_proj, g_ple, g_final, y)


def _layer(h, p, cosf, sinf, g_attn, w_in, lq1, lk1, lq2, lk2, g_subln, b_forget, w_out, g_moe,
           w_router, b_router, w_e1, b_e1, w_e2, b_e2, g_ple, w_ple_gate, w_ple_proj, g_final,
           lambda_init, final_norm):
    b, s, d = h.shape
    n = b * s
    row = lambda v: v.reshape(1, -1)

    w_main = w_in[:, :MAIN_WIDTH].astype(BF16)
    w_fz = jnp.repeat(w_in[:, MAIN_WIDTH:], HEAD_DIM, axis=1).astype(BF16)
    b_fz = row(jnp.repeat(b_forget, HEAD_DIM))
    half = LANES // 2
    w2 = (w_e2.reshape(N_EXPERTS, D_EXPERT // LANES, 2, half, d).transpose(0, 1, 3, 2, 4)
          .reshape(N_EXPERTS, D_EXPERT, d).astype(BF16))
    b1 = b_e1[:, None, :]
    b2 = b_e2[:, None, :]
    w_router_p = jnp.pad(w_router, ((0, 0), (0, LANES - N_EXPERTS)))
    b_router_p = row(jnp.pad(b_router, (0, LANES - N_EXPERTS)))

    dq, dk, dv, fq, fk, fv, c = _inproj(h, row(g_attn), w_main, w_fz, b_fz, cosf, sinf)
    ta = ATTN_TILE
    seq_last = lambda a: a.transpose(0, 2, 1)
    key_tiles = lambda a: a.reshape(b, s // ta, ta, a.shape[2]).transpose(0, 1, 3, 2)
    d_out = seq_last(_diff_attention(seq_last(dq), dk, key_tiles(dv), row(lq1), row(lk1), row(lq2),
                                     row(lk2), g_subln.reshape(-1, 1), lambda_init))
    c_rows = c[:, :, ::HEAD_DIM].transpose(0, 2, 1).reshape(b, N_FOX_HEADS // 2, 2, s)
    f_out = seq_last(_fox_attention(seq_last(fq), fk, key_tiles(fv), c_rows, c))

    h1, xn, gates, idx, rank, counts = _router(
        d_out.reshape(n, DIFF_WIDTH), f_out.reshape(n, FOX_WIDTH), h.reshape(n, d),
        w_out.astype(BF16), row(g_moe), w_router_p, b_router_p)

    te = EXPERT_TILE
    n_rows = n * TOP_K + N_EXPERTS * te
    counts = counts[0, :N_EXPERTS].astype(I32)
    padded = (counts + te - 1) // te * te
    ends = jnp.cumsum(padded)
    starts = ends - padded
    pos = starts[idx[:, :TOP_K]] + rank[:, :TOP_K]
    tm = TOKEN_TILE
    pos_flat = pos.reshape(n // tm, tm, TOP_K).transpose(0, 2, 1).reshape(-1)
    tile_start = jnp.arange(n_rows // te, dtype=I32) * te
    tile_expert = jnp.minimum(jnp.sum((ends[None, :] <= tile_start[:, None]).astype(I32), axis=1),
                              N_EXPERTS - 1)
    n_used = (ends[-1:] // te).astype(I32)

    xs = _dispatch(pos_flat, xn, n_rows)
    y = _experts(tile_expert, n_used, xs, w_e1, b1, w2, b2)
    out = _combine(pos_flat, h1, gates, p.reshape(n, PLE_DIM), w_ple_gate.astype(BF16),
                   w_ple_proj.astype(BF16), row(g_ple), row(g_final), y, final_norm)
    return out.reshape(b, s, d)


def kernel(x, p, positions, g_attn, w_in, lambda_q1, lambda_k1, lambda_q2, lambda_k2, g_subln, b_forget, w_out, g_moe, w_router, b_router, w_e1, b_e1, w_e2, b_e2, g_ple, w_ple_gate, w_ple_proj, g_final):
    depth = g_attn.shape[0]
    cosf, sinf = _rope_tables(positions)
    h = x
    for i in range(depth):
        lambda_init = 0.8 - 0.6 * math.exp(-0.3 * i)
        h = _layer(h, p[i], cosf, sinf, g_attn[i], w_in[i], lambda_q1[i], lambda_k1[i],
                   lambda_q2[i], lambda_k2[i], g_subln[i], b_forget[i], w_out[i], g_moe[i],
                   w_router[i], b_router[i], w_e1[i], b_e1[i], w_e2[i], b_e2[i], g_ple[i],
                   w_ple_gate[i], w_ple_proj[i], g_final, lambda_init, i == depth - 1)
    return h
```

```python
import functools
import math

import jax
import jax.numpy as jnp
from jax import lax
from jax.experimental import pallas as pl
from jax.experimental.pallas import tpu as pltpu

F32 = jnp.float32
BF16 = jnp.bfloat16
I32 = jnp.int32

LANES = 128
D_MODEL = 1024
HEAD_DIM = 64
N_DIFF_HEADS = 4
DIFF_V_DIM = 2 * HEAD_DIM
N_FOX_HEADS = 8
DIFF_WIDTH = N_DIFF_HEADS * DIFF_V_DIM
FOX_WIDTH = N_FOX_HEADS * HEAD_DIM
ROT_DIM = HEAD_DIM // 4
ROPE_THETA = 500000.0
N_EXPERTS = 32
TOP_K = 4
D_EXPERT = D_MODEL
SWIGLU_ALPHA = 1.702
SWIGLU_LIMIT = 7.0
PLE_DIM = 256
NORM_EPS = 1e-5
MAIN_WIDTH = 3 * DIFF_WIDTH + 3 * FOX_WIDTH

SEQ_TILE = 256
ATTN_TILE = 256
TOKEN_TILE = 256
EXPERT_TILE = 256
VMEM_LIMIT = 48 * 1024 * 1024


def _params(*semantics):
    return pltpu.CompilerParams(dimension_semantics=semantics, vmem_limit_bytes=VMEM_LIMIT)


def _rms(x, g):
    return x * lax.rsqrt(jnp.mean(x * x, axis=-1, keepdims=True) + NORM_EPS) * g


def _rope_table_kernel(pos_ref, cos_ref, sin_ref):
    pos = pos_ref[0].astype(F32)
    d = lax.broadcasted_iota(I32, (1, LANES), 1) % HEAD_DIM
    j = d % (ROT_DIM // 2)
    inv_freq = jnp.power(ROPE_THETA, -(2 * j).astype(F32) / ROT_DIM)
    ang = pos * inv_freq
    c, s = jnp.cos(ang), jnp.sin(ang)
    cos_ref[0] = jnp.where(d < ROT_DIM, c, 1.0)
    sin_ref[0] = jnp.where(d < ROT_DIM // 2, -s, jnp.where(d < ROT_DIM, s, 0.0))


def _rope_tables(positions):
    b, s = positions.shape
    ts = SEQ_TILE
    return pl.pallas_call(
        _rope_table_kernel,
        grid=(b, s // ts),
        in_specs=[pl.BlockSpec((1, ts, 1), lambda i, j: (i, j, 0))],
        out_specs=[pl.BlockSpec((1, ts, LANES), lambda i, j: (i, j, 0))] * 2,
        out_shape=[jax.ShapeDtypeStruct((b, s, LANES), F32)] * 2,
        compiler_params=_params("arbitrary", "arbitrary"),
        name="rope_tables",
    )(positions.reshape(b, s, 1))


def _inproj_kernel(x_ref, g_ref, w_ref, wfz_ref, bfz_ref, cos_ref, sin_ref,
                   dq_ref, dk_ref, dv_ref, fq_ref, fk_ref, fv_ref, c_ref, carry_ref):
    ts = x_ref.shape[1]
    xb = _rms(x_ref[0], g_ref[...]).astype(BF16)
    proj = jnp.dot(xb, w_ref[...], preferred_element_type=F32)

    cosf, sinf = cos_ref[0], sin_ref[0]
    first_half = lax.broadcasted_iota(I32, (1, LANES), 1) % HEAD_DIM < ROT_DIM // 2

    def rope(t):
        outs = []
        for j in range(t.shape[1] // LANES):
            tj = t[:, j * LANES:(j + 1) * LANES]
            partner = jnp.where(first_half,
                                pltpu.roll(tj, LANES - ROT_DIM // 2, 1),
                                pltpu.roll(tj, ROT_DIM // 2, 1))
            outs.append(tj * cosf + partner * sinf)
        return jnp.concatenate(outs, axis=1)

    scale = HEAD_DIM ** -0.5
    w = DIFF_WIDTH
    dq_ref[0] = (rope(proj[:, 0:w]) * scale).astype(BF16)
    dk_ref[0] = rope(proj[:, w:2 * w]).astype(BF16)
    dv_ref[0] = proj[:, 2 * w:3 * w].astype(BF16)
    fq_ref[0] = (proj[:, 3 * w:4 * w] * scale).astype(BF16)
    fk_ref[0] = proj[:, 4 * w:5 * w].astype(BF16)
    fv_ref[0] = proj[:, 5 * w:6 * w].astype(BF16)

    fz = jnp.dot(xb, wfz_ref[...], preferred_element_type=F32) + bfz_ref[...]
    logf = jnp.minimum(fz, 0.0) - jnp.log1p(jnp.exp(-jnp.abs(fz)))
    row = lax.broadcasted_iota(I32, (ts, ts), 0)
    col = lax.broadcasted_iota(I32, (ts, ts), 1)
    tri = jnp.where(col <= row, 1.0, 0.0).astype(BF16)
    hi = logf.astype(BF16)
    r1 = logf - hi.astype(F32)
    mid = r1.astype(BF16)
    lo = (r1 - mid.astype(F32)).astype(BF16)
    cum = (jnp.dot(tri, hi, preferred_element_type=F32)
           + jnp.dot(tri, mid, preferred_element_type=F32)
           + jnp.dot(tri, lo, preferred_element_type=F32))

    @pl.when(pl.program_id(1) == 0)
    def _():
        carry_ref[...] = jnp.zeros_like(carry_ref)

    c = cum + carry_ref[...]
    c_ref[0] = c
    carry_ref[...] = c[ts - 1:ts, :]


def _inproj(x, g, w_main, w_fz, b_fz, cosf, sinf):
    b, s, d = x.shape
    ts = SEQ_TILE
    tile = lambda width: pl.BlockSpec((1, ts, width), lambda i, j: (i, j, 0))
    full = lambda arr: pl.BlockSpec(arr.shape, lambda i, j: (0,) * arr.ndim)
    heads = jax.ShapeDtypeStruct((b, s, DIFF_WIDTH), BF16)
    return pl.pallas_call(
        _inproj_kernel,
        grid=(b, s // ts),
        in_specs=[tile(d), full(g), full(w_main), full(w_fz), full(b_fz), tile(LANES), tile(LANES)],
        out_specs=[tile(DIFF_WIDTH)] * 6 + [tile(FOX_WIDTH)],
        out_shape=[heads] * 6 + [jax.ShapeDtypeStruct((b, s, FOX_WIDTH), F32)],
        scratch_shapes=[pltpu.VMEM((1, FOX_WIDTH), F32)],
        compiler_params=_params("arbitrary", "arbitrary"),
        name="inproj",
    )(x, g, w_main, w_fz, b_fz, cosf, sinf)


ONES_ROWS = 16


def _block_diag_queries(qt):
    row = lax.broadcasted_iota(I32, qt.shape, 0)
    zero = jnp.zeros_like(qt)
    return jnp.concatenate([jnp.where(row < HEAD_DIM, qt, zero),
                            jnp.where(row >= HEAD_DIM, qt, zero)], axis=1)


class _Chain:
    def __init__(self, qbd, k_ref, vt_ref, acc_ref, s_ref, p_ref, stat_ref,
                 key_bias=None, query_bias=None):
        self.qbd, self.k_ref, self.vt_ref = qbd, k_ref, vt_ref
        self.acc_ref, self.s_ref, self.p_ref, self.stat_ref = acc_ref, s_ref, p_ref, stat_ref
        self.key_bias, self.query_bias = key_bias, query_bias


_RUNNING_MAX, _TILE_MAX, _RESCALE = 0, 1, 2


def _causal_attention(chains, qi):
    tk = chains[0].vt_ref.shape[3]
    tq = chains[0].qbd.shape[1] // 2
    ones = jnp.ones((ONES_ROWS, tk), BF16)

    def scores(ch, i, diagonal):
        ks = ch.k_ref[0, pl.ds(pl.multiple_of(i * tk, tk), tk), :]
        s = jnp.dot(ks, ch.qbd, preferred_element_type=F32)
        if ch.key_bias is not None:
            s = s - ch.key_bias(i)
        if diagonal is not False:
            r = lax.broadcasted_iota(I32, s.shape, 0)
            c = lax.broadcasted_iota(I32, s.shape, 1) & (tq - 1)
            visible = r <= c if diagonal is True else jnp.logical_or(r <= c, jnp.logical_not(diagonal))
            s = jnp.where(visible, s, -jnp.inf)
        ch.s_ref[...] = s
        top = jnp.max(s, axis=0, keepdims=True)
        if ch.query_bias is not None:
            top = top + ch.query_bias
        ch.stat_ref[_TILE_MAX:_TILE_MAX + 1, :] = top

    def probs(ch):
        m = ch.stat_ref[_RUNNING_MAX:_RUNNING_MAX + 1, :]
        m_new = jnp.maximum(m, ch.stat_ref[_TILE_MAX:_TILE_MAX + 1, :])
        ch.stat_ref[_RUNNING_MAX:_RUNNING_MAX + 1, :] = m_new
        ch.stat_ref[_RESCALE:_RESCALE + 1, :] = jnp.exp(m - m_new)
        ref = m_new if ch.query_bias is None else m_new - ch.query_bias
        ch.p_ref[...] = jnp.exp(ch.s_ref[...] - ref).astype(BF16)

    def accumulate(ch, i):
        vts = jnp.concatenate([ch.vt_ref[0, i], ones], axis=0)
        ch.acc_ref[...] = (ch.stat_ref[_RESCALE:_RESCALE + 1, :] * ch.acc_ref[...]
                           + jnp.dot(vts, ch.p_ref[...], preferred_element_type=F32))

    def body(k, diagonal_next):
        for ch in chains:
            accumulate(ch, jnp.maximum(k - 1, 0))
        for ch in chains:
            probs(ch)
        for ch in chains:
            scores(ch, k + 1, diagonal_next)

    for ch in chains:
        ch.acc_ref[...] = jnp.zeros_like(ch.acc_ref)
        ch.p_ref[...] = jnp.zeros_like(ch.p_ref)
        ch.stat_ref[...] = jnp.concatenate(
            [jnp.full((1, 2 * tq), -jnp.inf, F32), jnp.ones((ch.stat_ref.shape[0] - 1, 2 * tq), F32)], axis=0)
        scores(ch, 0, qi == 0)

    def loop_body(k, carry):
        body(k, False)
        return carry

    lax.fori_loop(0, qi - 1, loop_body, 0)

    @pl.when(qi >= 1)
    def _():
        body(qi - 1, True)

    for ch in chains:
        accumulate(ch, jnp.maximum(qi - 1, 0))
    for ch in chains:
        probs(ch)
    for ch in chains:
        accumulate(ch, qi)


def _attention_kernel(lq1_ref, lk1_ref, lq2_ref, lk2_ref, g_ref,
                      dqt_ref, dk_ref, dvt_ref, fqt_ref, fk_ref, fvt_ref, cq_ref, ck_ref,
                      do_ref, fo_ref, dacc_ref, ds_ref, dp_ref, dstat_ref,
                      facc_ref, fs_ref, fp_ref, fstat_ref, *, lambda_init):
    tq = dqt_ref.shape[2]
    tk = dvt_ref.shape[3]
    cq = jnp.concatenate([cq_ref[0, 0, 0:1, :], cq_ref[0, 0, 1:2, :]], axis=1)
    lane = lax.broadcasted_iota(I32, (1, LANES), 1)

    def key_bias(i):
        ck = ck_ref[0, pl.ds(pl.multiple_of(i * tk, tk), tk), :]
        other = pltpu.roll(ck, HEAD_DIM, 1)
        first = jnp.where(lane < HEAD_DIM, ck, other)
        second = jnp.where(lane >= HEAD_DIM, ck, other)
        return jnp.concatenate([first] * (tq // LANES) + [second] * (tq // LANES), axis=1)

    _causal_attention(
        [_Chain(_block_diag_queries(dqt_ref[0]), dk_ref, dvt_ref, dacc_ref, ds_ref, dp_ref, dstat_ref),
         _Chain(_block_diag_queries(fqt_ref[0]), fk_ref, fvt_ref, facc_ref, fs_ref, fp_ref, fstat_ref,
                key_bias, cq)],
        pl.program_id(2))

    lam = (jnp.exp(jnp.sum(lq1_ref[...] * lk1_ref[...], axis=-1, keepdims=True))
           - jnp.exp(jnp.sum(lq2_ref[...] * lk2_ref[...], axis=-1, keepdims=True)) + lambda_init)
    dv = DIFF_V_DIM
    o = (dacc_ref[0:dv, 0:tq] / dacc_ref[dv:dv + 1, 0:tq]
         - lam * (dacc_ref[0:dv, tq:] / dacc_ref[dv:dv + 1, tq:]))
    ms = jnp.mean(o * o, axis=0, keepdims=True)
    do_ref[0] = (o * lax.rsqrt(ms + NORM_EPS) * g_ref[...] * (1.0 - lambda_init)).astype(BF16)

    h = HEAD_DIM
    fo_ref[0] = jnp.concatenate([facc_ref[0:h, 0:tq] / facc_ref[2 * h:2 * h + 1, 0:tq],
                                 facc_ref[h:2 * h, tq:] / facc_ref[2 * h:2 * h + 1, tq:]],
                                axis=0).astype(BF16)


def _attention(dqt, dk, dvt, fqt, fk, fvt, c_rows, c_cols, lq1, lk1, lq2, lk2, g_col, lambda_init):
    b, s, _ = dk.shape
    t = ATTN_TILE
    assert N_DIFF_HEADS == N_FOX_HEADS // 2
    vec = lambda arr: pl.BlockSpec(arr.shape, lambda i, h, j: (0, 0))
    q_spec = pl.BlockSpec((1, LANES, t), lambda i, h, j: (i, h, j))
    k_spec = pl.BlockSpec((1, s, LANES), lambda i, h, j: (i, 0, h))
    v_spec = pl.BlockSpec((1, s // t, LANES, t), lambda i, h, j: (i, 0, h, 0))
    out = jax.ShapeDtypeStruct((b, DIFF_WIDTH, s), BF16)
    chain_scratch = [pltpu.VMEM((LANES + ONES_ROWS, 2 * t), F32),
                     pltpu.VMEM((t, 2 * t), F32),
                     pltpu.VMEM((t, 2 * t), BF16),
                     pltpu.VMEM((8, 2 * t), F32)]
    return pl.pallas_call(
        functools.partial(_attention_kernel, lambda_init=lambda_init),
        grid=(b, N_DIFF_HEADS, s // t),
        in_specs=[vec(lq1), vec(lk1), vec(lq2), vec(lk2), vec(g_col),
                  q_spec, k_spec, v_spec, q_spec, k_spec, v_spec,
                  pl.BlockSpec((1, 1, 2, t), lambda i, h, j: (i, h, 0, j)), k_spec],
        out_specs=[q_spec, q_spec],
        out_shape=[out, out],
        scratch_shapes=chain_scratch * 2,
        compiler_params=_params("arbitrary", "arbitrary", "arbitrary"),
        name="attention",
    )(lq1, lk1, lq2, lk2, g_col, dqt, dk, dvt, fqt, fk, fvt, c_rows, c_cols)


def _router_kernel(d_ref, f_ref, x_ref, wo_ref, g_ref, wr_ref, br_ref,
                   h_ref, xn_ref, gate_ref, idx_ref, rank_ref, count_ref, carry_ref):
    tm = x_ref.shape[0]
    mixed = (jnp.dot(d_ref[...], wo_ref[0:DIFF_WIDTH, :], preferred_element_type=F32)
             + jnp.dot(f_ref[...], wo_ref[DIFF_WIDTH:, :], preferred_element_type=F32))
    h = x_ref[...] + mixed
    h_ref[...] = h
    xn = _rms(h, g_ref[...])
    xn_ref[...] = xn

    xh = xn.astype(BF16)
    xl = (xn - xh.astype(F32)).astype(BF16)
    wr = wr_ref[...]
    wh = wr.astype(BF16)
    wl = (wr - wh.astype(F32)).astype(BF16)
    logits = (jnp.dot(xh, wh, preferred_element_type=F32)
              + jnp.dot(xl, wh, preferred_element_type=F32)
              + jnp.dot(xh, wl, preferred_element_type=F32)) + br_ref[...]
    lane = lax.broadcasted_iota(I32, (tm, LANES), 1)
    work = jnp.where(lane < N_EXPERTS, logits, -jnp.inf)

    vals, picks = [], []
    for _ in range(TOP_K):
        m = jnp.max(work, axis=-1, keepdims=True)
        idx = jnp.min(jnp.where(work == m, lane, LANES), axis=-1, keepdims=True)
        pick = lane == idx
        vals.append(m)
        picks.append(pick)
        work = jnp.where(pick, -jnp.inf, work)
    exps = [jnp.exp(v - vals[0]) for v in vals]
    denom = exps[0] + exps[1] + exps[2] + exps[3]

    chosen = jnp.where(picks[0] | picks[1] | picks[2] | picks[3], 1.0, 0.0)
    row = lax.broadcasted_iota(I32, (tm, tm), 0)
    col = lax.broadcasted_iota(I32, (tm, tm), 1)
    before = jnp.where(col < row, 1.0, 0.0).astype(BF16)

    @pl.when(pl.program_id(0) == 0)
    def _():
        carry_ref[...] = jnp.zeros_like(carry_ref)

    rank = jnp.dot(before, chosen.astype(BF16), preferred_element_type=F32) + carry_ref[...]
    carry_ref[...] = carry_ref[...] + jnp.sum(chosen, axis=0, keepdims=True)
    count_ref[...] = carry_ref[...]

    gate_out = jnp.zeros((tm, LANES), F32)
    idx_out = jnp.zeros((tm, LANES), I32)
    rank_out = jnp.zeros((tm, LANES), F32)
    for k in range(TOP_K):
        here = lane == k
        gate_out = jnp.where(here, exps[k] / denom, gate_out)
        idx_out = jnp.where(here, jnp.min(jnp.where(picks[k], lane, LANES), axis=-1, keepdims=True), idx_out)
        rank_out = jnp.where(here, jnp.sum(jnp.where(picks[k], rank, 0.0), axis=-1, keepdims=True), rank_out)
    gate_ref[...] = gate_out
    idx_ref[...] = idx_out
    rank_ref[...] = rank_out.astype(I32)


def _router(d_out, f_out, x, w_out, g_moe, w_router, b_router):
    n, d = x.shape
    tm = TOKEN_TILE
    tile = lambda width: pl.BlockSpec((tm, width), lambda i: (i, 0))
    full = lambda arr: pl.BlockSpec(arr.shape, lambda i: (0,) * arr.ndim)
    return pl.pallas_call(
        _router_kernel,
        grid=(n // tm,),
        in_specs=[tile(DIFF_WIDTH), tile(FOX_WIDTH), tile(d), full(w_out), full(g_moe),
                  full(w_router), full(b_router)],
        out_specs=[tile(d), tile(d), tile(LANES), tile(LANES), tile(LANES),
                   pl.BlockSpec((1, LANES), lambda i: (0, 0))],
        out_shape=[jax.ShapeDtypeStruct((n, d), F32), jax.ShapeDtypeStruct((n, d), F32),
                   jax.ShapeDtypeStruct((n, LANES), F32), jax.ShapeDtypeStruct((n, LANES), I32),
                   jax.ShapeDtypeStruct((n, LANES), I32), jax.ShapeDtypeStruct((1, LANES), F32)],
        scratch_shapes=[pltpu.VMEM((1, LANES), F32)],
        compiler_params=_params("arbitrary"),
        name="router",
    )(d_out, f_out, x, w_out, g_moe, w_router, b_router)


def _dispatch_kernel(pos_ref, x_ref, init_ref, xs_ref, sem):
    del init_ref
    tm = x_ref.shape[0]

    def issue(r, carry):
        for k in range(TOP_K):
            dst = pos_ref[k * tm + r]
            pltpu.make_async_copy(x_ref.at[pl.ds(r, 1), :], xs_ref.at[pl.ds(dst, 1), :], sem).start()
        return carry

    lax.fori_loop(0, tm, issue, 0)
    for _ in range(TOP_K):
        pltpu.make_async_copy(x_ref, xs_ref.at[pl.ds(0, tm), :], sem).wait()


def _dispatch(pos_flat, xn, n_rows):
    n, d = xn.shape
    tm = TOKEN_TILE
    return pl.pallas_call(
        _dispatch_kernel,
        grid=(n // tm,),
        in_specs=[pl.BlockSpec((TOP_K * tm,), lambda i: (i,), memory_space=pltpu.SMEM),
                  pl.BlockSpec((tm, d), lambda i: (i, 0)),
                  pl.BlockSpec(memory_space=pl.ANY)],
        out_specs=pl.BlockSpec(memory_space=pl.ANY),
        out_shape=jax.ShapeDtypeStruct((n_rows, d), F32),
        scratch_shapes=[pltpu.SemaphoreType.DMA(())],
        input_output_aliases={2: 0},
        compiler_params=_params("arbitrary"),
        name="dispatch",
    )(pos_flat, xn, jnp.zeros((n_rows, d), F32))


def _expert_kernel(tile_expert_ref, n_used_ref, x_ref, w1_ref, b1_ref, w2_ref, b2_ref, y_ref,
                   w1b_ref):
    t = pl.program_id(0)
    fresh = (t == 0) | (tile_expert_ref[t] != tile_expert_ref[jnp.maximum(t - 1, 0)])
    used = t < n_used_ref[0]

    @pl.when(fresh & used)
    def _():
        rows = LANES
        for r in range(0, w1b_ref.shape[0], rows):
            w1b_ref[r:r + rows, :] = w1_ref[0, r:r + rows, :].astype(BF16)

    @pl.when(used)
    def _():
        xb = x_ref[...].astype(BF16)
        h = jnp.dot(xb, w1b_ref[...], preferred_element_type=F32) + b1_ref[0]
        even = lax.broadcasted_iota(I32, (1, LANES), 1) % 2 == 0
        parts = []
        for q in range(h.shape[1] // (2 * LANES)):
            a = h[:, (2 * q) * LANES:(2 * q + 1) * LANES]
            b = h[:, (2 * q + 1) * LANES:(2 * q + 2) * LANES]
            hg = jnp.where(even, a, pltpu.roll(b, 1, 1))
            hl = jnp.where(even, pltpu.roll(a, LANES - 1, 1), b)
            glu = jnp.minimum(hg, SWIGLU_LIMIT)
            lin = jnp.clip(hl, -SWIGLU_LIMIT, SWIGLU_LIMIT)
            parts.append((glu * jax.nn.sigmoid(SWIGLU_ALPHA * glu) * (lin + 1.0)).astype(BF16))
        act = jnp.concatenate(parts, axis=1)
        y_ref[...] = jnp.dot(act, w2_ref[0], preferred_element_type=F32) + b2_ref[0]

    @pl.when(jnp.logical_not(used))
    def _():
        y_ref[...] = jnp.zeros_like(y_ref)


def _experts(tile_expert, n_used, xs, w1, b1, w2, b2):
    n_rows, d = xs.shape
    tm = EXPERT_TILE
    by_expert = lambda arr: pl.BlockSpec((1,) + arr.shape[1:], lambda i, te, nu: (te[i], 0, 0))
    return pl.pallas_call(
        _expert_kernel,
        grid_spec=pltpu.PrefetchScalarGridSpec(
            num_scalar_prefetch=2,
            grid=(n_rows // tm,),
            in_specs=[pl.BlockSpec((tm, d), lambda i, te, nu: (i, 0)),
                      by_expert(w1), by_expert(b1), by_expert(w2), by_expert(b2)],
            out_specs=pl.BlockSpec((tm, d), lambda i, te, nu: (i, 0)),
            scratch_shapes=[pltpu.VMEM(w1.shape[1:], BF16)]),
        out_shape=jax.ShapeDtypeStruct((n_rows, d), F32),
        compiler_params=_params("arbitrary"),
        name="experts",
    )(tile_expert, n_used, xs, w1, b1, w2, b2)


def _combine_kernel(pos_ref, h_ref, gate_ref, p_ref, wg_ref, wp_ref, gple_ref, gfin_ref, y_ref,
                    o_ref, ybuf, sem, *, final_norm):
    tm = h_ref.shape[0]

    def issue(r, carry):
        for k in range(TOP_K):
            src = pos_ref[k * tm + r]
            pltpu.make_async_copy(y_ref.at[pl.ds(src, 1), :], ybuf.at[k, pl.ds(r, 1), :], sem).start()
        return carry

    lax.fori_loop(0, tm, issue, 0)
    for k in range(TOP_K):
        pltpu.make_async_copy(y_ref.at[pl.ds(0, tm), :], ybuf.at[k], sem).wait()

    gates = gate_ref[...]
    moe = gates[:, 0:1] * ybuf[0]
    for k in range(1, TOP_K):
        moe = moe + gates[:, k:k + 1] * ybuf[k]
    h = h_ref[...] + moe
    gate = jax.nn.sigmoid(jnp.dot(_rms(h, gple_ref[...]).astype(BF16), wg_ref[...],
                                  preferred_element_type=F32))
    emb = jnp.dot(p_ref[...].astype(BF16), wp_ref[...], preferred_element_type=F32)
    h = h + gate * emb
    o_ref[...] = _rms(h, gfin_ref[...]) if final_norm else h


def _combine(pos_flat, h, gates, p, w_gate, w_proj, g_ple, g_final, y, final_norm):
    n, d = h.shape
    tm = TOKEN_TILE
    tile = lambda width: pl.BlockSpec((tm, width), lambda i: (i, 0))
    full = lambda arr: pl.BlockSpec(arr.shape, lambda i: (0,) * arr.ndim)
    return pl.pallas_call(
        functools.partial(_combine_kernel, final_norm=final_norm),
        grid=(n // tm,),
        in_specs=[pl.BlockSpec((TOP_K * tm,), lambda i: (i,), memory_space=pltpu.SMEM),
                  tile(d), tile(LANES), tile(p.shape[1]), full(w_gate), full(w_proj),
                  full(g_ple), full(g_final), pl.BlockSpec(memory_space=pl.ANY)],
        out_specs=tile(d),
        out_shape=jax.ShapeDtypeStruct((n, d), F32),
        scratch_shapes=[pltpu.VMEM((TOP_K, tm, d), F32), pltpu.SemaphoreType.DMA(())],
        compiler_params=_params("arbitrary"),
        name="combine",
    )(pos_flat, h, gates, p, w_gate, w_proj, g_ple, g_final, y)


def _layer(h, p, cosf, sinf, g_attn, w_in, lq1, lk1, lq2, lk2, g_subln, b_forget, w_out, g_moe,
           w_router, b_router, w_e1, b_e1, w_e2, b_e2, g_ple, w_ple_gate, w_ple_proj, g_final,
           lambda_init, final_norm):
    b, s, d = h.shape
    n = b * s
    row = lambda v: v.reshape(1, -1)

    w_main = w_in[:, :MAIN_WIDTH].astype(BF16)
    w_fz = jnp.repeat(w_in[:, MAIN_WIDTH:], HEAD_DIM, axis=1).astype(BF16)
    b_fz = row(jnp.repeat(b_forget, HEAD_DIM))
    half = LANES // 2
    w2 = (w_e2.reshape(N_EXPERTS, D_EXPERT // LANES, 2, half, d).transpose(0, 1, 3, 2, 4)
          .reshape(N_EXPERTS, D_EXPERT, d).astype(BF16))
    b1 = b_e1[:, None, :]
    b2 = b_e2[:, None, :]
    w_router_p = jnp.pad(w_router, ((0, 0), (0, LANES - N_EXPERTS)))
    b_router_p = row(jnp.pad(b_router, (0, LANES - N_EXPERTS)))

    dq, dk, dv, fq, fk, fv, c = _inproj(h, row(g_attn), w_main, w_fz, b_fz, cosf, sinf)
    ta = ATTN_TILE
    seq_last = lambda a: a.transpose(0, 2, 1)
    key_tiles = lambda a: a.reshape(b, s // ta, ta, a.shape[2]).transpose(0, 1, 3, 2)
    c_rows = c[:, :, ::HEAD_DIM].transpose(0, 2, 1).reshape(b, N_FOX_HEADS // 2, 2, s)
    d_out_t, f_out_t = _attention(seq_last(dq), dk, key_tiles(dv), seq_last(fq), fk, key_tiles(fv),
                                  c_rows, c, row(lq1), row(lk1), row(lq2), row(lk2),
                                  g_subln.reshape(-1, 1), lambda_init)
    d_out, f_out = seq_last(d_out_t), seq_last(f_out_t)

    h1, xn, gates, idx, rank, counts = _router(
        d_out.reshape(n, DIFF_WIDTH), f_out.reshape(n, FOX_WIDTH), h.reshape(n, d),
        w_out.astype(BF16), row(g_moe), w_router_p, b_router_p)

    te = EXPERT_TILE
    n_rows = n * TOP_K + N_EXPERTS * te
    counts = counts[0, :N_EXPERTS].astype(I32)
    padded = (counts + te - 1) // te * te
    ends = jnp.cumsum(padded)
    starts = ends - padded
    pos = starts[idx[:, :TOP_K]] + rank[:, :TOP_K]
    tm = TOKEN_TILE
    pos_flat = pos.reshape(n // tm, tm, TOP_K).transpose(0, 2, 1).reshape(-1)
    tile_start = jnp.arange(n_rows // te, dtype=I32) * te
    tile_expert = jnp.minimum(jnp.sum((ends[None, :] <= tile_start[:, None]).astype(I32), axis=1),
                              N_EXPERTS - 1)
    n_used = (ends[-1:] // te).astype(I32)

    xs = _dispatch(pos_flat, xn, n_rows)
    y = _experts(tile_expert, n_used, xs, w_e1, b1, w2, b2)
    out = _combine(pos_flat, h1, gates, p.reshape(n, PLE_DIM), w_ple_gate.astype(BF16),
                   w_ple_proj.astype(BF16), row(g_ple), row(g_final), y, final_norm)
    return out.reshape(b, s, d)


def kernel(x, p, positions, g_attn, w_in, lambda_q1, lambda_k1, lambda_q2, lambda_k2, g_subln, b_forget, w_out, g_moe, w_router, b_router, w_e1, b_e1, w_e2, b_e2, g_ple, w_ple_gate, w_ple_proj, g_final):
    depth = g_attn.shape[0]
    cosf, sinf = _rope_tables(positions)
    h = x
    for i in range(depth):
        lambda_init = 0.8 - 0.6 * math.exp(-0.3 * i)
        h = _layer(h, p[i], cosf, sinf, g_attn[i], w_in[i], lambda_q1[i], lambda_k1[i],
                   lambda_q2[i], lambda_k2[i], g_subln[i], b_forget[i], w_out[i], g_moe[i],
                   w_router[i], b_router[i], w_e1[i], b_e1[i], w_e2[i], b_e2[i], g_ple[i],
                   w_ple_gate[i], w_ple_proj[i], g_final, lambda_init, i == depth - 1)
    return h
```

```python
import functools
import math

import jax
import jax.numpy as jnp
from jax import lax
from jax.experimental import pallas as pl
from jax.experimental.pallas import tpu as pltpu

F32 = jnp.float32
BF16 = jnp.bfloat16
I32 = jnp.int32

LANES = 128
D_MODEL = 1024
HEAD_DIM = 64
N_DIFF_HEADS = 4
DIFF_V_DIM = 2 * HEAD_DIM
N_FOX_HEADS = 8
DIFF_WIDTH = N_DIFF_HEADS * DIFF_V_DIM
FOX_WIDTH = N_FOX_HEADS * HEAD_DIM
ROT_DIM = HEAD_DIM // 4
ROPE_THETA = 500000.0
N_EXPERTS = 32
TOP_K = 4
D_EXPERT = D_MODEL
SWIGLU_ALPHA = 1.702
SWIGLU_LIMIT = 7.0
PLE_DIM = 256
NORM_EPS = 1e-5
MAIN_WIDTH = 3 * DIFF_WIDTH + 3 * FOX_WIDTH

SEQ_TILE = 256
ATTN_TILE = 256
TOKEN_TILE = 256
EXPERT_TILE = 256
VMEM_LIMIT = 48 * 1024 * 1024
EXPERT_VMEM_LIMIT = 58 * 1024 * 1024


def _params(*semantics):
    return pltpu.CompilerParams(dimension_semantics=semantics, vmem_limit_bytes=VMEM_LIMIT)


def _rms(x, g):
    return x * lax.rsqrt(jnp.mean(x * x, axis=-1, keepdims=True) + NORM_EPS) * g


def _rope_table_kernel(pos_ref, cos_ref, sin_ref):
    pos = pos_ref[0].astype(F32)
    d = lax.broadcasted_iota(I32, (1, LANES), 1) % HEAD_DIM
    j = d % (ROT_DIM // 2)
    inv_freq = jnp.power(ROPE_THETA, -(2 * j).astype(F32) / ROT_DIM)
    ang = pos * inv_freq
    c, s = jnp.cos(ang), jnp.sin(ang)
    cos_ref[0] = jnp.where(d < ROT_DIM, c, 1.0)
    sin_ref[0] = jnp.where(d < ROT_DIM // 2, -s, jnp.where(d < ROT_DIM, s, 0.0))


def _rope_tables(positions):
    b, s = positions.shape
    ts = SEQ_TILE
    return pl.pallas_call(
        _rope_table_kernel,
        grid=(b, s // ts),
        in_specs=[pl.BlockSpec((1, ts, 1), lambda i, j: (i, j, 0))],
        out_specs=[pl.BlockSpec((1, ts, LANES), lambda i, j: (i, j, 0))] * 2,
        out_shape=[jax.ShapeDtypeStruct((b, s, LANES), F32)] * 2,
        compiler_params=_params("arbitrary", "arbitrary"),
        name="rope_tables",
    )(positions.reshape(b, s, 1))


def _inproj_kernel(x_ref, g_ref, w_ref, wfz_ref, bfz_ref, wfzt_ref, bfzt_ref, cos_ref, sin_ref,
                   dq_ref, dk_ref, dv_ref, fq_ref, fk_ref, fv_ref, c_ref, ct_ref,
                   carry_ref, carry_t_ref):
    ts = x_ref.shape[1]
    xb = _rms(x_ref[0], g_ref[...]).astype(BF16)
    proj = jnp.dot(xb, w_ref[...], preferred_element_type=F32)

    cosf, sinf = cos_ref[0], sin_ref[0]
    first_half = lax.broadcasted_iota(I32, (1, LANES), 1) % HEAD_DIM < ROT_DIM // 2

    def rope(t):
        outs = []
        for j in range(t.shape[1] // LANES):
            tj = t[:, j * LANES:(j + 1) * LANES]
            partner = jnp.where(first_half,
                                pltpu.roll(tj, LANES - ROT_DIM // 2, 1),
                                pltpu.roll(tj, ROT_DIM // 2, 1))
            outs.append(tj * cosf + partner * sinf)
        return jnp.concatenate(outs, axis=1)

    scale = HEAD_DIM ** -0.5
    w = DIFF_WIDTH
    dq_ref[0] = (rope(proj[:, 0:w]) * scale).astype(BF16)
    dk_ref[0] = rope(proj[:, w:2 * w]).astype(BF16)
    dv_ref[0] = proj[:, 2 * w:3 * w].astype(BF16)
    fq_ref[0] = (proj[:, 3 * w:4 * w] * scale).astype(BF16)
    fk_ref[0] = proj[:, 4 * w:5 * w].astype(BF16)
    fv_ref[0] = proj[:, 5 * w:6 * w].astype(BF16)

    def log_sigmoid(z):
        return jnp.minimum(z, 0.0) - jnp.log1p(jnp.exp(-jnp.abs(z)))

    def pieces(v):
        hi = v.astype(BF16)
        r1 = v - hi.astype(F32)
        mid = r1.astype(BF16)
        return hi, mid, (r1 - mid.astype(F32)).astype(BF16)

    row = lax.broadcasted_iota(I32, (ts, ts), 0)
    col = lax.broadcasted_iota(I32, (ts, ts), 1)
    upto_row = jnp.where(col <= row, 1.0, 0.0).astype(BF16)
    upto_col = jnp.where(row <= col, 1.0, 0.0).astype(BF16)

    @pl.when(pl.program_id(1) == 0)
    def _():
        carry_ref[...] = jnp.zeros_like(carry_ref)
        carry_t_ref[...] = jnp.zeros_like(carry_t_ref)

    fz = jnp.dot(xb, wfz_ref[...], preferred_element_type=F32) + bfz_ref[...]
    c = carry_ref[...] + sum(jnp.dot(upto_row, part, preferred_element_type=F32)
                             for part in pieces(log_sigmoid(fz)))
    c_ref[0] = c
    carry_ref[...] = c[ts - 1:ts, :]

    fzt = lax.dot_general(wfzt_ref[...], xb, (((1,), (1,)), ((), ())),
                          preferred_element_type=F32) + bfzt_ref[...]
    ct = carry_t_ref[...] + sum(jnp.dot(part, upto_col, preferred_element_type=F32)
                                for part in pieces(log_sigmoid(fzt)))
    ct_ref[0] = ct[0:N_FOX_HEADS, :]
    carry_t_ref[...] = ct[:, ts - 1:ts]


def _inproj(x, g, w_main, w_fz, b_fz, w_fz_t, b_fz_t, cosf, sinf):
    b, s, d = x.shape
    ts = SEQ_TILE
    tile = lambda width: pl.BlockSpec((1, ts, width), lambda i, j: (i, j, 0))
    full = lambda arr: pl.BlockSpec(arr.shape, lambda i, j: (0,) * arr.ndim)
    heads = jax.ShapeDtypeStruct((b, s, DIFF_WIDTH), BF16)
    return pl.pallas_call(
        _inproj_kernel,
        grid=(b, s // ts),
        in_specs=[tile(d), full(g), full(w_main), full(w_fz), full(b_fz), full(w_fz_t), full(b_fz_t),
                  tile(LANES), tile(LANES)],
        out_specs=[tile(DIFF_WIDTH)] * 6 + [tile(FOX_WIDTH),
                                            pl.BlockSpec((1, N_FOX_HEADS, ts), lambda i, j: (i, 0, j))],
        out_shape=[heads] * 6 + [jax.ShapeDtypeStruct((b, s, FOX_WIDTH), F32),
                                 jax.ShapeDtypeStruct((b, N_FOX_HEADS, s), F32)],
        scratch_shapes=[pltpu.VMEM((1, FOX_WIDTH), F32), pltpu.VMEM((w_fz_t.shape[0], 1), F32)],
        compiler_params=_params("arbitrary", "arbitrary"),
        name="inproj",
    )(x, g, w_main, w_fz, b_fz, w_fz_t, b_fz_t, cosf, sinf)


ONES_ROWS = 16


def _block_diag_queries(qt):
    row = lax.broadcasted_iota(I32, qt.shape, 0)
    zero = jnp.zeros_like(qt)
    return jnp.concatenate([jnp.where(row < HEAD_DIM, qt, zero),
                            jnp.where(row >= HEAD_DIM, qt, zero)], axis=1)


class _Chain:
    def __init__(self, qbd, k_ref, vt_ref, acc_ref, s_ref, p_ref, stat_ref,
                 key_bias=None, query_bias=None):
        self.qbd, self.k_ref, self.vt_ref = qbd, k_ref, vt_ref
        self.acc_ref, self.s_ref, self.p_ref, self.stat_ref = acc_ref, s_ref, p_ref, stat_ref
        self.key_bias, self.query_bias = key_bias, query_bias


_RUNNING_MAX, _TILE_MAX, _RESCALE = 0, 1, 2


def _causal_attention(chains, qi):
    tk = chains[0].vt_ref.shape[3]
    tq = chains[0].qbd.shape[1] // 2
    ones = jnp.ones((ONES_ROWS, tk), BF16)

    def scores(ch, i, diagonal):
        ks = ch.k_ref[0, pl.ds(pl.multiple_of(i * tk, tk), tk), :]
        s = jnp.dot(ks, ch.qbd, preferred_element_type=F32)
        if ch.key_bias is not None:
            s = s - ch.key_bias(i)
        if diagonal is not False:
            r = lax.broadcasted_iota(I32, s.shape, 0)
            c = lax.broadcasted_iota(I32, s.shape, 1) & (tq - 1)
            visible = r <= c if diagonal is True else jnp.logical_or(r <= c, jnp.logical_not(diagonal))
            s = jnp.where(visible, s, -jnp.inf)
        ch.s_ref[...] = s
        top = jnp.max(s, axis=0, keepdims=True)
        if ch.query_bias is not None:
            top = top + ch.query_bias
        ch.stat_ref[_TILE_MAX:_TILE_MAX + 1, :] = top

    def probs(ch):
        m = ch.stat_ref[_RUNNING_MAX:_RUNNING_MAX + 1, :]
        m_new = jnp.maximum(m, ch.stat_ref[_TILE_MAX:_TILE_MAX + 1, :])
        ch.stat_ref[_RUNNING_MAX:_RUNNING_MAX + 1, :] = m_new
        ch.stat_ref[_RESCALE:_RESCALE + 1, :] = jnp.exp(m - m_new)
        ref = m_new if ch.query_bias is None else m_new - ch.query_bias
        ch.p_ref[...] = jnp.exp(ch.s_ref[...] - ref).astype(BF16)

    def accumulate(ch, i):
        vts = jnp.concatenate([ch.vt_ref[0, i], ones], axis=0)
        ch.acc_ref[...] = (ch.stat_ref[_RESCALE:_RESCALE + 1, :] * ch.acc_ref[...]
                           + jnp.dot(vts, ch.p_ref[...], preferred_element_type=F32))

    def body(k, diagonal_next):
        for ch in chains:
            accumulate(ch, jnp.maximum(k - 1, 0))
        for ch in chains:
            probs(ch)
        for ch in chains:
            scores(ch, k + 1, diagonal_next)

    for ch in chains:
        ch.acc_ref[...] = jnp.zeros_like(ch.acc_ref)
        ch.p_ref[...] = jnp.zeros_like(ch.p_ref)
        ch.stat_ref[...] = jnp.concatenate(
            [jnp.full((1, 2 * tq), -jnp.inf, F32), jnp.ones((ch.stat_ref.shape[0] - 1, 2 * tq), F32)], axis=0)
        scores(ch, 0, qi == 0)

    def loop_body(k, carry):
        body(k, False)
        return carry

    lax.fori_loop(0, qi - 1, loop_body, 0)

    @pl.when(qi >= 1)
    def _():
        body(qi - 1, True)

    for ch in chains:
        accumulate(ch, jnp.maximum(qi - 1, 0))
    for ch in chains:
        probs(ch)
    for ch in chains:
        accumulate(ch, qi)


def _attention_kernel(lq1_ref, lk1_ref, lq2_ref, lk2_ref, g_ref,
                      dqt_ref, dk_ref, dvt_ref, fqt_ref, fk_ref, fvt_ref, cq_ref, ck_ref,
                      do_ref, fo_ref, dacc_ref, ds_ref, dp_ref, dstat_ref,
                      facc_ref, fs_ref, fp_ref, fstat_ref, *, lambda_init):
    tq = dqt_ref.shape[2]
    tk = dvt_ref.shape[3]
    cq = jnp.concatenate([cq_ref[0, 0, 0:1, :], cq_ref[0, 0, 1:2, :]], axis=1)
    lane = lax.broadcasted_iota(I32, (1, LANES), 1)

    def key_bias(i):
        ck = ck_ref[0, pl.ds(pl.multiple_of(i * tk, tk), tk), :]
        other = pltpu.roll(ck, HEAD_DIM, 1)
        first = jnp.where(lane < HEAD_DIM, ck, other)
        second = jnp.where(lane >= HEAD_DIM, ck, other)
        return jnp.concatenate([first] * (tq // LANES) + [second] * (tq // LANES), axis=1)

    _causal_attention(
        [_Chain(_block_diag_queries(dqt_ref[0]), dk_ref, dvt_ref, dacc_ref, ds_ref, dp_ref, dstat_ref),
         _Chain(_block_diag_queries(fqt_ref[0]), fk_ref, fvt_ref, facc_ref, fs_ref, fp_ref, fstat_ref,
                key_bias, cq)],
        pl.program_id(2))

    lam = (jnp.exp(jnp.sum(lq1_ref[...] * lk1_ref[...], axis=-1, keepdims=True))
           - jnp.exp(jnp.sum(lq2_ref[...] * lk2_ref[...], axis=-1, keepdims=True)) + lambda_init)
    dv = DIFF_V_DIM
    o = (dacc_ref[0:dv, 0:tq] / dacc_ref[dv:dv + 1, 0:tq]
         - lam * (dacc_ref[0:dv, tq:] / dacc_ref[dv:dv + 1, tq:]))
    ms = jnp.mean(o * o, axis=0, keepdims=True)
    do_ref[0] = (o * lax.rsqrt(ms + NORM_EPS) * g_ref[...] * (1.0 - lambda_init)).astype(BF16)

    h = HEAD_DIM
    fo_ref[0] = jnp.concatenate([facc_ref[0:h, 0:tq] / facc_ref[2 * h:2 * h + 1, 0:tq],
                                 facc_ref[h:2 * h, tq:] / facc_ref[2 * h:2 * h + 1, tq:]],
                                axis=0).astype(BF16)


def _attention(dqt, dk, dvt, fqt, fk, fvt, c_rows, c_cols, lq1, lk1, lq2, lk2, g_col, lambda_init):
    b, s, _ = dk.shape
    t = ATTN_TILE
    assert N_DIFF_HEADS == N_FOX_HEADS // 2
    vec = lambda arr: pl.BlockSpec(arr.shape, lambda i, h, j: (0, 0))
    q_spec = pl.BlockSpec((1, LANES, t), lambda i, h, j: (i, h, j))
    k_spec = pl.BlockSpec((1, s, LANES), lambda i, h, j: (i, 0, h))
    v_spec = pl.BlockSpec((1, s // t, LANES, t), lambda i, h, j: (i, 0, h, 0))
    out = jax.ShapeDtypeStruct((b, DIFF_WIDTH, s), BF16)
    chain_scratch = [pltpu.VMEM((LANES + ONES_ROWS, 2 * t), F32),
                     pltpu.VMEM((t, 2 * t), F32),
                     pltpu.VMEM((t, 2 * t), BF16),
                     pltpu.VMEM((8, 2 * t), F32)]
    return pl.pallas_call(
        functools.partial(_attention_kernel, lambda_init=lambda_init),
        grid=(b, N_DIFF_HEADS, s // t),
        in_specs=[vec(lq1), vec(lk1), vec(lq2), vec(lk2), vec(g_col),
                  q_spec, k_spec, v_spec, q_spec, k_spec, v_spec,
                  pl.BlockSpec((1, 1, 2, t), lambda i, h, j: (i, h, 0, j)), k_spec],
        out_specs=[q_spec, q_spec],
        out_shape=[out, out],
        scratch_shapes=chain_scratch * 2,
        compiler_params=_params("arbitrary", "arbitrary", "arbitrary"),
        name="attention",
    )(lq1, lk1, lq2, lk2, g_col, dqt, dk, dvt, fqt, fk, fvt, c_rows, c_cols)


def _router_kernel(d_ref, f_ref, x_ref, wo_ref, g_ref, wr_ref, br_ref,
                   h_ref, xn_ref, gate_ref, idx_ref, rank_ref, count_ref, carry_ref):
    tm = x_ref.shape[0]
    mixed = (jnp.dot(d_ref[...], wo_ref[0:DIFF_WIDTH, :], preferred_element_type=F32)
             + jnp.dot(f_ref[...], wo_ref[DIFF_WIDTH:, :], preferred_element_type=F32))
    h = x_ref[...] + mixed
    h_ref[...] = h
    xn = _rms(h, g_ref[...])
    xn_ref[...] = xn

    xh = xn.astype(BF16)
    xl = (xn - xh.astype(F32)).astype(BF16)
    wr = wr_ref[...]
    wh = wr.astype(BF16)
    wl = (wr - wh.astype(F32)).astype(BF16)
    logits = (jnp.dot(xh, wh, preferred_element_type=F32)
              + jnp.dot(xl, wh, preferred_element_type=F32)
              + jnp.dot(xh, wl, preferred_element_type=F32)) + br_ref[...]
    lane = lax.broadcasted_iota(I32, (tm, LANES), 1)
    work = jnp.where(lane < N_EXPERTS, logits, -jnp.inf)

    vals, picks = [], []
    for _ in range(TOP_K):
        m = jnp.max(work, axis=-1, keepdims=True)
        idx = jnp.min(jnp.where(work == m, lane, LANES), axis=-1, keepdims=True)
        pick = lane == idx
        vals.append(m)
        picks.append(pick)
        work = jnp.where(pick, -jnp.inf, work)
    exps = [jnp.exp(v - vals[0]) for v in vals]
    denom = exps[0] + exps[1] + exps[2] + exps[3]

    chosen = jnp.where(picks[0] | picks[1] | picks[2] | picks[3], 1.0, 0.0)
    row = lax.broadcasted_iota(I32, (tm, tm), 0)
    col = lax.broadcasted_iota(I32, (tm, tm), 1)
    before = jnp.where(col < row, 1.0, 0.0).astype(BF16)

    @pl.when(pl.program_id(0) == 0)
    def _():
        carry_ref[...] = jnp.zeros_like(carry_ref)

    rank = jnp.dot(before, chosen.astype(BF16), preferred_element_type=F32) + carry_ref[...]
    carry_ref[...] = carry_ref[...] + jnp.sum(chosen, axis=0, keepdims=True)
    count_ref[...] = carry_ref[...]

    gate_out = jnp.zeros((tm, LANES), F32)
    idx_out = jnp.zeros((tm, LANES), I32)
    rank_out = jnp.zeros((tm, LANES), F32)
    for k in range(TOP_K):
        here = lane == k
        gate_out = jnp.where(here, exps[k] / denom, gate_out)
        idx_out = jnp.where(here, jnp.min(jnp.where(picks[k], lane, LANES), axis=-1, keepdims=True), idx_out)
        rank_out = jnp.where(here, jnp.sum(jnp.where(picks[k], rank, 0.0), axis=-1, keepdims=True), rank_out)
    gate_ref[...] = gate_out
    idx_ref[...] = idx_out
    rank_ref[...] = rank_out.astype(I32)


def _router(d_out, f_out, x, w_out, g_moe, w_router, b_router):
    n, d = x.shape
    tm = TOKEN_TILE
    tile = lambda width: pl.BlockSpec((tm, width), lambda i: (i, 0))
    full = lambda arr: pl.BlockSpec(arr.shape, lambda i: (0,) * arr.ndim)
    return pl.pallas_call(
        _router_kernel,
        grid=(n // tm,),
        in_specs=[tile(DIFF_WIDTH), tile(FOX_WIDTH), tile(d), full(w_out), full(g_moe),
                  full(w_router), full(b_router)],
        out_specs=[tile(d), tile(d), tile(LANES), tile(LANES), tile(LANES),
                   pl.BlockSpec((1, LANES), lambda i: (0, 0))],
        out_shape=[jax.ShapeDtypeStruct((n, d), F32), jax.ShapeDtypeStruct((n, d), F32),
                   jax.ShapeDtypeStruct((n, LANES), F32), jax.ShapeDtypeStruct((n, LANES), I32),
                   jax.ShapeDtypeStruct((n, LANES), I32), jax.ShapeDtypeStruct((1, LANES), F32)],
        scratch_shapes=[pltpu.VMEM((1, LANES), F32)],
        compiler_params=_params("arbitrary"),
        name="router",
    )(d_out, f_out, x, w_out, g_moe, w_router, b_router)


def _dispatch_kernel(pos_ref, x_ref, init_ref, xs_ref, sem):
    del init_ref
    tm = x_ref.shape[0]

    def issue(r, carry):
        for k in range(TOP_K):
            dst = pos_ref[k * tm + r]
            pltpu.make_async_copy(x_ref.at[pl.ds(r, 1), :], xs_ref.at[pl.ds(dst, 1), :],
                                  sem).start(priority=k % 2)
        return carry

    lax.fori_loop(0, tm, issue, 0)
    for _ in range(TOP_K):
        pltpu.make_async_copy(x_ref, xs_ref.at[pl.ds(0, tm), :], sem).wait()


def _dispatch(pos_flat, xn, n_rows):
    n, d = xn.shape
    tm = TOKEN_TILE
    return pl.pallas_call(
        _dispatch_kernel,
        grid=(n // tm,),
        in_specs=[pl.BlockSpec((TOP_K * tm,), lambda i: (i,), memory_space=pltpu.SMEM),
                  pl.BlockSpec((tm, d), lambda i: (i, 0)),
                  pl.BlockSpec(memory_space=pl.ANY)],
        out_specs=pl.BlockSpec(memory_space=pl.ANY),
        out_shape=jax.ShapeDtypeStruct((n_rows, d), F32),
        scratch_shapes=[pltpu.SemaphoreType.DMA(())],
        input_output_aliases={2: 0},
        compiler_params=_params("arbitrary"),
        name="dispatch",
    )(pos_flat, xn, jnp.zeros((n_rows, d), F32))


def _expert_kernel(tile_expert_ref, n_used_ref, x_ref, w1_ref, b1_ref, w2_ref, b2_ref, y_ref,
                   w1b_ref, w2p_ref, w2b_ref):
    t = pl.program_id(0)
    fresh = (t == 0) | (tile_expert_ref[t] != tile_expert_ref[jnp.maximum(t - 1, 0)])
    used = t < n_used_ref[0]

    @pl.when(fresh & used)
    def _():
        rows = LANES
        half = LANES // 2
        for r in range(0, w1b_ref.shape[0], rows):
            w1b_ref[r:r + rows, :] = w1_ref[0, r:r + rows, :].astype(BF16)
        for c in range(w2p_ref.shape[0]):
            lanes = slice(c * LANES, (c + 1) * LANES)
            for r in range(0, w2b_ref.shape[0], rows):
                for b in range(2):
                    w2p_ref[c, pl.ds(r + b, half, stride=2), :] = (
                        w2_ref[0, r + b * half:r + (b + 1) * half, lanes])
            w2b_ref[:, lanes] = w2p_ref[c].astype(BF16)

    @pl.when(used)
    def _():
        xb = x_ref[...].astype(BF16)
        h = jnp.dot(xb, w1b_ref[...], preferred_element_type=F32) + b1_ref[0]
        even = lax.broadcasted_iota(I32, (1, LANES), 1) % 2 == 0
        parts = []
        for q in range(h.shape[1] // (2 * LANES)):
            a = h[:, (2 * q) * LANES:(2 * q + 1) * LANES]
            b = h[:, (2 * q + 1) * LANES:(2 * q + 2) * LANES]
            hg = jnp.where(even, a, pltpu.roll(b, 1, 1))
            hl = jnp.where(even, pltpu.roll(a, LANES - 1, 1), b)
            glu = jnp.minimum(hg, SWIGLU_LIMIT)
            lin = jnp.clip(hl, -SWIGLU_LIMIT, SWIGLU_LIMIT)
            parts.append((glu * jax.nn.sigmoid(SWIGLU_ALPHA * glu) * (lin + 1.0)).astype(BF16))
        act = jnp.concatenate(parts, axis=1)
        y_ref[...] = jnp.dot(act, w2b_ref[...], preferred_element_type=F32) + b2_ref[0]

    @pl.when(jnp.logical_not(used))
    def _():
        y_ref[...] = jnp.zeros_like(y_ref)


def _experts(tile_expert, n_used, xs, w1, b1, w2, b2):
    n_rows, d = xs.shape
    tm = EXPERT_TILE
    by_expert = lambda arr: pl.BlockSpec((1,) + arr.shape[1:], lambda i, te, nu: (te[i], 0, 0))
    return pl.pallas_call(
        _expert_kernel,
        grid_spec=pltpu.PrefetchScalarGridSpec(
            num_scalar_prefetch=2,
            grid=(n_rows // tm,),
            in_specs=[pl.BlockSpec((tm, d), lambda i, te, nu: (i, 0)),
                      by_expert(w1), by_expert(b1), by_expert(w2), by_expert(b2)],
            out_specs=pl.BlockSpec((tm, d), lambda i, te, nu: (i, 0)),
            scratch_shapes=[pltpu.VMEM(w1.shape[1:], BF16),
                            pltpu.VMEM((w2.shape[2] // LANES, w2.shape[1], LANES), F32),
                            pltpu.VMEM(w2.shape[1:], BF16)]),
        out_shape=jax.ShapeDtypeStruct((n_rows, d), F32),
        compiler_params=pltpu.CompilerParams(dimension_semantics=("arbitrary",),
                                             vmem_limit_bytes=EXPERT_VMEM_LIMIT),
        name="experts",
    )(tile_expert, n_used, xs, w1, b1, w2, b2)


def _combine_kernel(pos_ref, h_ref, gate_ref, p_ref, wg_ref, wp_ref, gple_ref, gfin_ref, y_ref,
                    o_ref, ybuf, sem, *, final_norm):
    tm = h_ref.shape[0]

    def issue(r, carry):
        for k in range(TOP_K):
            src = pos_ref[k * tm + r]
            pltpu.make_async_copy(y_ref.at[pl.ds(src, 1), :], ybuf.at[k, pl.ds(r, 1), :],
                                  sem).start(priority=k % 2)
        return carry

    lax.fori_loop(0, tm, issue, 0)
    for k in range(TOP_K):
        pltpu.make_async_copy(y_ref.at[pl.ds(0, tm), :], ybuf.at[k], sem).wait()

    gates = gate_ref[...]
    moe = gates[:, 0:1] * ybuf[0]
    for k in range(1, TOP_K):
        moe = moe + gates[:, k:k + 1] * ybuf[k]
    h = h_ref[...] + moe
    gate = jax.nn.sigmoid(jnp.dot(_rms(h, gple_ref[...]).astype(BF16), wg_ref[...],
                                  preferred_element_type=F32))
    emb = jnp.dot(p_ref[...].astype(BF16), wp_ref[...], preferred_element_type=F32)
    h = h + gate * emb
    o_ref[...] = _rms(h, gfin_ref[...]) if final_norm else h


def _combine(pos_flat, h, gates, p, w_gate, w_proj, g_ple, g_final, y, final_norm):
    n, d = h.shape
    tm = TOKEN_TILE
    tile = lambda width: pl.BlockSpec((tm, width), lambda i: (i, 0))
    full = lambda arr: pl.BlockSpec(arr.shape, lambda i: (0,) * arr.ndim)
    return pl.pallas_call(
        functools.partial(_combine_kernel, final_norm=final_norm),
        grid=(n // tm,),
        in_specs=[pl.BlockSpec((TOP_K * tm,), lambda i: (i,), memory_space=pltpu.SMEM),
                  tile(d), tile(LANES), tile(p.shape[1]), full(w_gate), full(w_proj),
                  full(g_ple), full(g_final), pl.BlockSpec(memory_space=pl.ANY)],
        out_specs=tile(d),
        out_shape=jax.ShapeDtypeStruct((n, d), F32),
        scratch_shapes=[pltpu.VMEM((TOP_K, tm, d), F32), pltpu.SemaphoreType.DMA(())],
        compiler_params=_params("arbitrary"),
        name="combine",
    )(pos_flat, h, gates, p, w_gate, w_proj, g_ple, g_final, y)


def _layer(h, p, cosf, sinf, g_attn, w_in, lq1, lk1, lq2, lk2, g_subln, b_forget, w_out, g_moe,
           w_router, b_router, w_e1, b_e1, w_e2, b_e2, g_ple, w_ple_gate, w_ple_proj, g_final,
           lambda_init, final_norm):
    b, s, d = h.shape
    n = b * s
    row = lambda v: v.reshape(1, -1)

    w_main = w_in[:, :MAIN_WIDTH].astype(BF16)
    w_fz = jnp.repeat(w_in[:, MAIN_WIDTH:], HEAD_DIM, axis=1).astype(BF16)
    b_fz = row(jnp.repeat(b_forget, HEAD_DIM))
    pad_heads = ((0, 2 * N_FOX_HEADS - N_FOX_HEADS), (0, 0))
    w_fz_t = jnp.pad(w_in[:, MAIN_WIDTH:].T, pad_heads).astype(BF16)
    b_fz_t = jnp.pad(b_forget.reshape(-1, 1), pad_heads)
    b1 = b_e1[:, None, :]
    b2 = b_e2[:, None, :]
    w_router_p = jnp.pad(w_router, ((0, 0), (0, LANES - N_EXPERTS)))
    b_router_p = row(jnp.pad(b_router, (0, LANES - N_EXPERTS)))

    dq, dk, dv, fq, fk, fv, c, c_t = _inproj(h, row(g_attn), w_main, w_fz, b_fz, w_fz_t, b_fz_t,
                                            cosf, sinf)
    ta = ATTN_TILE
    seq_last = lambda a: a.transpose(0, 2, 1)
    key_tiles = lambda a: a.reshape(b, s // ta, ta, a.shape[2]).transpose(0, 1, 3, 2)
    c_rows = c_t.reshape(b, N_FOX_HEADS // 2, 2, s)
    d_out_t, f_out_t = _attention(seq_last(dq), dk, key_tiles(dv), seq_last(fq), fk, key_tiles(fv),
                                  c_rows, c, row(lq1), row(lk1), row(lq2), row(lk2),
                                  g_subln.reshape(-1, 1), lambda_init)
    d_out, f_out = seq_last(d_out_t), seq_last(f_out_t)

    h1, xn, gates, idx, rank, counts = _router(
        d_out.reshape(n, DIFF_WIDTH), f_out.reshape(n, FOX_WIDTH), h.reshape(n, d),
        w_out.astype(BF16), row(g_moe), w_router_p, b_router_p)

    te = EXPERT_TILE
    n_rows = n * TOP_K + N_EXPERTS * te
    counts = counts[0, :N_EXPERTS].astype(I32)
    padded = (counts + te - 1) // te * te
    ends = jnp.cumsum(padded)
    starts = ends - padded
    pos = starts[idx[:, :TOP_K]] + rank[:, :TOP_K]
    tm = TOKEN_TILE
    pos_flat = pos.reshape(n // tm, tm, TOP_K).transpose(0, 2, 1).reshape(-1)
    tile_start = jnp.arange(n_rows // te, dtype=I32) * te
    tile_expert = jnp.minimum(jnp.sum((ends[None, :] <= tile_start[:, None]).astype(I32), axis=1),
                              N_EXPERTS - 1)
    n_used = (ends[-1:] // te).astype(I32)

    xs = _dispatch(pos_flat, xn, n_rows)
    y = _experts(tile_expert, n_used, xs, w_e1, b1, w_e2, b2)
    out = _combine(pos_flat, h1, gates, p.reshape(n, PLE_DIM), w_ple_gate.astype(BF16),
                   w_ple_proj.astype(BF16), row(g_ple), row(g_final), y, final_norm)
    return out.reshape(b, s, d)


def kernel(x, p, positions, g_attn, w_in, lambda_q1, lambda_k1, lambda_q2, lambda_k2, g_subln, b_forget, w_out, g_moe, w_router, b_router, w_e1, b_e1, w_e2, b_e2, g_ple, w_ple_gate, w_ple_proj, g_final):
    depth = g_attn.shape[0]
    cosf, sinf = _rope_tables(positions)
    h = x
    for i in range(depth):
        lambda_init = 0.8 - 0.6 * math.exp(-0.3 * i)
        h = _layer(h, p[i], cosf, sinf, g_attn[i], w_in[i], lambda_q1[i], lambda_k1[i],
                   lambda_q2[i], lambda_k2[i], g_subln[i], b_forget[i], w_out[i], g_moe[i],
                   w_router[i], b_router[i], w_e1[i], b_e1[i], w_e2[i], b_e2[i], g_ple[i],
                   w_ple_gate[i], w_ple_proj[i], g_final, lambda_init, i == depth - 1)
    return h
```

```python
import functools
import math

import jax
import jax.numpy as jnp
from jax import lax
from jax.experimental import pallas as pl
from jax.experimental.pallas import tpu as pltpu

F32 = jnp.float32
BF16 = jnp.bfloat16
I32 = jnp.int32

LANES = 128
D_MODEL = 1024
HEAD_DIM = 64
N_DIFF_HEADS = 4
DIFF_V_DIM = 2 * HEAD_DIM
N_FOX_HEADS = 8
DIFF_WIDTH = N_DIFF_HEADS * DIFF_V_DIM
FOX_WIDTH = N_FOX_HEADS * HEAD_DIM
ROT_DIM = HEAD_DIM // 4
ROPE_THETA = 500000.0
N_EXPERTS = 32
TOP_K = 4
D_EXPERT = D_MODEL
SWIGLU_ALPHA = 1.702
SWIGLU_LIMIT = 7.0
PLE_DIM = 256
NORM_EPS = 1e-5
MAIN_WIDTH = 3 * DIFF_WIDTH + 3 * FOX_WIDTH

SEQ_TILE = 256
ATTN_TILE = 256
TOKEN_TILE = 256
EXPERT_TILE = 256
ROW_GROUP = 8
VMEM_LIMIT = 48 * 1024 * 1024
EXPERT_VMEM_LIMIT = 58 * 1024 * 1024


def _params(*semantics):
    return pltpu.CompilerParams(dimension_semantics=semantics, vmem_limit_bytes=VMEM_LIMIT)


def _rms(x, g):
    return x * lax.rsqrt(jnp.mean(x * x, axis=-1, keepdims=True) + NORM_EPS) * g


def _rope_table_kernel(pos_ref, cos_ref, sin_ref):
    pos = pos_ref[0].astype(F32)
    d = lax.broadcasted_iota(I32, (1, LANES), 1) % HEAD_DIM
    j = d % (ROT_DIM // 2)
    inv_freq = jnp.power(ROPE_THETA, -(2 * j).astype(F32) / ROT_DIM)
    ang = pos * inv_freq
    c, s = jnp.cos(ang), jnp.sin(ang)
    cos_ref[0] = jnp.where(d < ROT_DIM, c, 1.0)
    sin_ref[0] = jnp.where(d < ROT_DIM // 2, -s, jnp.where(d < ROT_DIM, s, 0.0))


def _rope_tables(positions):
    b, s = positions.shape
    ts = SEQ_TILE
    return pl.pallas_call(
        _rope_table_kernel,
        grid=(b, s // ts),
        in_specs=[pl.BlockSpec((1, ts, 1), lambda i, j: (i, j, 0))],
        out_specs=[pl.BlockSpec((1, ts, LANES), lambda i, j: (i, j, 0))] * 2,
        out_shape=[jax.ShapeDtypeStruct((b, s, LANES), F32)] * 2,
        compiler_params=_params("arbitrary", "arbitrary"),
        name="rope_tables",
    )(positions.reshape(b, s, 1))


def _inproj_kernel(x_ref, g_ref, w_ref, wfz_ref, bfz_ref, wfzt_ref, bfzt_ref, cos_ref, sin_ref,
                   dq_ref, dk_ref, dv_ref, fq_ref, fk_ref, fv_ref, c_ref, ct_ref,
                   carry_ref, carry_t_ref):
    ts = x_ref.shape[1]
    xb = _rms(x_ref[0], g_ref[...]).astype(BF16)
    proj = jnp.dot(xb, w_ref[...], preferred_element_type=F32)

    cosf, sinf = cos_ref[0], sin_ref[0]
    first_half = lax.broadcasted_iota(I32, (1, LANES), 1) % HEAD_DIM < ROT_DIM // 2

    def rope(t):
        outs = []
        for j in range(t.shape[1] // LANES):
            tj = t[:, j * LANES:(j + 1) * LANES]
            partner = jnp.where(first_half,
                                pltpu.roll(tj, LANES - ROT_DIM // 2, 1),
                                pltpu.roll(tj, ROT_DIM // 2, 1))
            outs.append(tj * cosf + partner * sinf)
        return jnp.concatenate(outs, axis=1)

    scale = HEAD_DIM ** -0.5
    w = DIFF_WIDTH
    dq_ref[0] = (rope(proj[:, 0:w]) * scale).astype(BF16)
    dk_ref[0] = rope(proj[:, w:2 * w]).astype(BF16)
    dv_ref[0] = proj[:, 2 * w:3 * w].astype(BF16)
    fq_ref[0] = (proj[:, 3 * w:4 * w] * scale).astype(BF16)
    fk_ref[0] = proj[:, 4 * w:5 * w].astype(BF16)
    fv_ref[0] = proj[:, 5 * w:6 * w].astype(BF16)

    def log_sigmoid(z):
        return jnp.minimum(z, 0.0) - jnp.log1p(jnp.exp(-jnp.abs(z)))

    def pieces(v):
        hi = v.astype(BF16)
        r1 = v - hi.astype(F32)
        mid = r1.astype(BF16)
        return hi, mid, (r1 - mid.astype(F32)).astype(BF16)

    row = lax.broadcasted_iota(I32, (ts, ts), 0)
    col = lax.broadcasted_iota(I32, (ts, ts), 1)
    upto_row = jnp.where(col <= row, 1.0, 0.0).astype(BF16)
    upto_col = jnp.where(row <= col, 1.0, 0.0).astype(BF16)

    @pl.when(pl.program_id(1) == 0)
    def _():
        carry_ref[...] = jnp.zeros_like(carry_ref)
        carry_t_ref[...] = jnp.zeros_like(carry_t_ref)

    fz = jnp.dot(xb, wfz_ref[...], preferred_element_type=F32) + bfz_ref[...]
    c = carry_ref[...] + sum(jnp.dot(upto_row, part, preferred_element_type=F32)
                             for part in pieces(log_sigmoid(fz)))
    c_ref[0] = c
    carry_ref[...] = c[ts - 1:ts, :]

    fzt = lax.dot_general(wfzt_ref[...], xb, (((1,), (1,)), ((), ())),
                          preferred_element_type=F32) + bfzt_ref[...]
    ct = carry_t_ref[...] + sum(jnp.dot(part, upto_col, preferred_element_type=F32)
                                for part in pieces(log_sigmoid(fzt)))
    ct_ref[0] = ct[0:N_FOX_HEADS, :]
    carry_t_ref[...] = ct[:, ts - 1:ts]


def _inproj(x, g, w_main, w_fz, b_fz, w_fz_t, b_fz_t, cosf, sinf):
    b, s, d = x.shape
    ts = SEQ_TILE
    tile = lambda width: pl.BlockSpec((1, ts, width), lambda i, j: (i, j, 0))
    full = lambda arr: pl.BlockSpec(arr.shape, lambda i, j: (0,) * arr.ndim)
    heads = jax.ShapeDtypeStruct((b, s, DIFF_WIDTH), BF16)
    return pl.pallas_call(
        _inproj_kernel,
        grid=(b, s // ts),
        in_specs=[tile(d), full(g), full(w_main), full(w_fz), full(b_fz), full(w_fz_t), full(b_fz_t),
                  tile(LANES), tile(LANES)],
        out_specs=[tile(DIFF_WIDTH)] * 6 + [tile(FOX_WIDTH),
                                            pl.BlockSpec((1, N_FOX_HEADS, ts), lambda i, j: (i, 0, j))],
        out_shape=[heads] * 6 + [jax.ShapeDtypeStruct((b, s, FOX_WIDTH), F32),
                                 jax.ShapeDtypeStruct((b, N_FOX_HEADS, s), F32)],
        scratch_shapes=[pltpu.VMEM((1, FOX_WIDTH), F32), pltpu.VMEM((w_fz_t.shape[0], 1), F32)],
        compiler_params=_params("arbitrary", "arbitrary"),
        name="inproj",
    )(x, g, w_main, w_fz, b_fz, w_fz_t, b_fz_t, cosf, sinf)


ONES_ROWS = 16


def _block_diag_queries(qt):
    row = lax.broadcasted_iota(I32, qt.shape, 0)
    zero = jnp.zeros_like(qt)
    return jnp.concatenate([jnp.where(row < HEAD_DIM, qt, zero),
                            jnp.where(row >= HEAD_DIM, qt, zero)], axis=1)


class _Chain:
    def __init__(self, qbd, k_ref, vt_ref, acc_ref, s_ref, p_ref, stat_ref,
                 key_bias=None, query_bias=None):
        self.qbd, self.k_ref, self.vt_ref = qbd, k_ref, vt_ref
        self.acc_ref, self.s_ref, self.p_ref, self.stat_ref = acc_ref, s_ref, p_ref, stat_ref
        self.key_bias, self.query_bias = key_bias, query_bias


_RUNNING_MAX, _TILE_MAX, _RESCALE = 0, 1, 2


def _causal_attention(chains, qi):
    tk = chains[0].vt_ref.shape[3]
    tq = chains[0].qbd.shape[1] // 2
    ones = jnp.ones((ONES_ROWS, tk), BF16)

    def scores(ch, i, diagonal):
        ks = ch.k_ref[0, pl.ds(pl.multiple_of(i * tk, tk), tk), :]
        s = jnp.dot(ks, ch.qbd, preferred_element_type=F32)
        if ch.key_bias is not None:
            s = s - ch.key_bias(i)
        if diagonal is not False:
            r = lax.broadcasted_iota(I32, s.shape, 0)
            c = lax.broadcasted_iota(I32, s.shape, 1) & (tq - 1)
            visible = r <= c if diagonal is True else jnp.logical_or(r <= c, jnp.logical_not(diagonal))
            s = jnp.where(visible, s, -jnp.inf)
        ch.s_ref[...] = s
        top = jnp.max(s, axis=0, keepdims=True)
        if ch.query_bias is not None:
            top = top + ch.query_bias
        ch.stat_ref[_TILE_MAX:_TILE_MAX + 1, :] = top

    def probs(ch):
        m = ch.stat_ref[_RUNNING_MAX:_RUNNING_MAX + 1, :]
        m_new = jnp.maximum(m, ch.stat_ref[_TILE_MAX:_TILE_MAX + 1, :])
        ch.stat_ref[_RUNNING_MAX:_RUNNING_MAX + 1, :] = m_new
        ch.stat_ref[_RESCALE:_RESCALE + 1, :] = jnp.exp(m - m_new)
        ref = m_new if ch.query_bias is None else m_new - ch.query_bias
        ch.p_ref[...] = jnp.exp(ch.s_ref[...] - ref).astype(BF16)

    def accumulate(ch, i):
        vts = jnp.concatenate([ch.vt_ref[0, i], ones], axis=0)
        ch.acc_ref[...] = (ch.stat_ref[_RESCALE:_RESCALE + 1, :] * ch.acc_ref[...]
                           + jnp.dot(vts, ch.p_ref[...], preferred_element_type=F32))

    def body(k, diagonal_next):
        for ch in chains:
            accumulate(ch, jnp.maximum(k - 1, 0))
        for ch in chains:
            probs(ch)
        for ch in chains:
            scores(ch, k + 1, diagonal_next)

    for ch in chains:
        ch.acc_ref[...] = jnp.zeros_like(ch.acc_ref)
        ch.p_ref[...] = jnp.zeros_like(ch.p_ref)
        ch.stat_ref[...] = jnp.concatenate(
            [jnp.full((1, 2 * tq), -jnp.inf, F32), jnp.ones((ch.stat_ref.shape[0] - 1, 2 * tq), F32)], axis=0)
        scores(ch, 0, qi == 0)

    def loop_body(k, carry):
        body(k, False)
        return carry

    lax.fori_loop(0, qi - 1, loop_body, 0)

    @pl.when(qi >= 1)
    def _():
        body(qi - 1, True)

    for ch in chains:
        accumulate(ch, jnp.maximum(qi - 1, 0))
    for ch in chains:
        probs(ch)
    for ch in chains:
        accumulate(ch, qi)


def _attention_kernel(lq1_ref, lk1_ref, lq2_ref, lk2_ref, g_ref,
                      dqt_ref, dk_ref, dvt_ref, fqt_ref, fk_ref, fvt_ref, cq_ref, ck_ref,
                      do_ref, fo_ref, dacc_ref, ds_ref, dp_ref, dstat_ref,
                      facc_ref, fs_ref, fp_ref, fstat_ref, *, lambda_init):
    tq = dqt_ref.shape[2]
    tk = dvt_ref.shape[3]
    cq = jnp.concatenate([cq_ref[0, 0, 0:1, :], cq_ref[0, 0, 1:2, :]], axis=1)
    lane = lax.broadcasted_iota(I32, (1, LANES), 1)

    def key_bias(i):
        ck = ck_ref[0, pl.ds(pl.multiple_of(i * tk, tk), tk), :]
        other = pltpu.roll(ck, HEAD_DIM, 1)
        first = jnp.where(lane < HEAD_DIM, ck, other)
        second = jnp.where(lane >= HEAD_DIM, ck, other)
        return jnp.concatenate([first] * (tq // LANES) + [second] * (tq // LANES), axis=1)

    _causal_attention(
        [_Chain(_block_diag_queries(dqt_ref[0]), dk_ref, dvt_ref, dacc_ref, ds_ref, dp_ref, dstat_ref),
         _Chain(_block_diag_queries(fqt_ref[0]), fk_ref, fvt_ref, facc_ref, fs_ref, fp_ref, fstat_ref,
                key_bias, cq)],
        pl.program_id(2))

    lam = (jnp.exp(jnp.sum(lq1_ref[...] * lk1_ref[...], axis=-1, keepdims=True))
           - jnp.exp(jnp.sum(lq2_ref[...] * lk2_ref[...], axis=-1, keepdims=True)) + lambda_init)
    dv = DIFF_V_DIM
    o = (dacc_ref[0:dv, 0:tq] / dacc_ref[dv:dv + 1, 0:tq]
         - lam * (dacc_ref[0:dv, tq:] / dacc_ref[dv:dv + 1, tq:]))
    ms = jnp.mean(o * o, axis=0, keepdims=True)
    do_ref[0] = (o * lax.rsqrt(ms + NORM_EPS) * g_ref[...] * (1.0 - lambda_init)).astype(BF16)

    h = HEAD_DIM
    fo_ref[0] = jnp.concatenate([facc_ref[0:h, 0:tq] / facc_ref[2 * h:2 * h + 1, 0:tq],
                                 facc_ref[h:2 * h, tq:] / facc_ref[2 * h:2 * h + 1, tq:]],
                                axis=0).astype(BF16)


def _attention(dqt, dk, dvt, fqt, fk, fvt, c_rows, c_cols, lq1, lk1, lq2, lk2, g_col, lambda_init):
    b, s, _ = dk.shape
    t = ATTN_TILE
    assert N_DIFF_HEADS == N_FOX_HEADS // 2
    vec = lambda arr: pl.BlockSpec(arr.shape, lambda i, h, j: (0, 0))
    q_spec = pl.BlockSpec((1, LANES, t), lambda i, h, j: (i, h, j))
    k_spec = pl.BlockSpec((1, s, LANES), lambda i, h, j: (i, 0, h))
    v_spec = pl.BlockSpec((1, s // t, LANES, t), lambda i, h, j: (i, 0, h, 0))
    out = jax.ShapeDtypeStruct((b, DIFF_WIDTH, s), BF16)
    chain_scratch = [pltpu.VMEM((LANES + ONES_ROWS, 2 * t), F32),
                     pltpu.VMEM((t, 2 * t), F32),
                     pltpu.VMEM((t, 2 * t), BF16),
                     pltpu.VMEM((8, 2 * t), F32)]
    return pl.pallas_call(
        functools.partial(_attention_kernel, lambda_init=lambda_init),
        grid=(b, N_DIFF_HEADS, s // t),
        in_specs=[vec(lq1), vec(lk1), vec(lq2), vec(lk2), vec(g_col),
                  q_spec, k_spec, v_spec, q_spec, k_spec, v_spec,
                  pl.BlockSpec((1, 1, 2, t), lambda i, h, j: (i, h, 0, j)), k_spec],
        out_specs=[q_spec, q_spec],
        out_shape=[out, out],
        scratch_shapes=chain_scratch * 2,
        compiler_params=_params("arbitrary", "arbitrary", "arbitrary"),
        name="attention",
    )(lq1, lk1, lq2, lk2, g_col, dqt, dk, dvt, fqt, fk, fvt, c_rows, c_cols)


def _router_kernel(d_ref, f_ref, x_ref, wo_ref, g_ref, wrt_ref, brt_ref,
                   h_ref, xn_ref, gate_ref, idx_ref, rank_ref, count_ref, carry_ref):
    tm = x_ref.shape[0]
    mixed = (jnp.dot(d_ref[...], wo_ref[0:DIFF_WIDTH, :], preferred_element_type=F32)
             + jnp.dot(f_ref[...], wo_ref[DIFF_WIDTH:, :], preferred_element_type=F32))
    h = x_ref[...] + mixed
    h_ref[...] = h
    xn = _rms(h, g_ref[...])
    xn_ref[...] = xn

    def pieces(v, count):
        out = []
        for _ in range(count):
            part = v.astype(BF16)
            out.append(part)
            v = v - part.astype(F32)
        return out

    nt = (((1,), (1,)), ((), ()))
    xh, xl = pieces(xn, 2)
    wh, wl = pieces(wrt_ref[...], 2)
    work = (lax.dot_general(wh, xh, nt, preferred_element_type=F32)
            + lax.dot_general(wh, xl, nt, preferred_element_type=F32)
            + lax.dot_general(wl, xh, nt, preferred_element_type=F32)) + brt_ref[...]
    expert = lax.broadcasted_iota(I32, work.shape, 0)

    vals, ids, picks = [], [], []
    for _ in range(TOP_K):
        m = jnp.max(work, axis=0, keepdims=True)
        idx = jnp.min(jnp.where(work == m, expert, N_EXPERTS), axis=0, keepdims=True)
        pick = expert == idx
        vals.append(m)
        ids.append(idx)
        picks.append(pick)
        work = jnp.where(pick, -jnp.inf, work)
    exps = [jnp.exp(v - vals[0]) for v in vals]
    denom = exps[0] + exps[1] + exps[2] + exps[3]

    chosen = jnp.where(picks[0] | picks[1] | picks[2] | picks[3], 1.0, 0.0)
    row = lax.broadcasted_iota(I32, (tm, tm), 0)
    col = lax.broadcasted_iota(I32, (tm, tm), 1)
    earlier = jnp.where(row < col, 1.0, 0.0).astype(BF16)

    @pl.when(pl.program_id(0) == 0)
    def _():
        carry_ref[...] = jnp.zeros_like(carry_ref)

    rank = jnp.dot(chosen.astype(BF16), earlier, preferred_element_type=F32) + carry_ref[...]
    carry_ref[...] = carry_ref[...] + jnp.sum(chosen, axis=1, keepdims=True)
    count_ref[...] = jnp.broadcast_to(carry_ref[...], count_ref.shape)

    slot = lax.broadcasted_iota(I32, (8, tm), 0)
    gate_t = jnp.zeros((8, tm), F32)
    idx_t = jnp.zeros((8, tm), I32)
    rank_t = jnp.zeros((8, tm), F32)
    for k in range(TOP_K):
        gate_t = jnp.where(slot == k, exps[k] / denom, gate_t)
        idx_t = jnp.where(slot == k, ids[k], idx_t)
        rank_t = jnp.where(slot == k, jnp.sum(jnp.where(picks[k], rank, 0.0), axis=0, keepdims=True), rank_t)
    idx_ref[...] = idx_t
    rank_ref[...] = rank_t.astype(I32)
    gate_rows = jnp.concatenate([gate_t, jnp.zeros((LANES - 8, tm), F32)], axis=0)
    eye = jnp.where(row == col, 1.0, 0.0).astype(BF16)
    gate_ref[...] = sum(lax.dot_general(eye, part, nt, preferred_element_type=F32)
                        for part in pieces(gate_rows, 3))


def _router(d_out, f_out, x, w_out, g_moe, w_router_t, b_router_t):
    n, d = x.shape
    tm = TOKEN_TILE
    tile = lambda width: pl.BlockSpec((tm, width), lambda i: (i, 0))
    rows = pl.BlockSpec((8, tm), lambda i: (0, i))
    full = lambda arr: pl.BlockSpec(arr.shape, lambda i: (0,) * arr.ndim)
    return pl.pallas_call(
        _router_kernel,
        grid=(n // tm,),
        in_specs=[tile(DIFF_WIDTH), tile(FOX_WIDTH), tile(d), full(w_out), full(g_moe),
                  full(w_router_t), full(b_router_t)],
        out_specs=[tile(d), tile(d), tile(LANES), rows, rows,
                   pl.BlockSpec((N_EXPERTS, LANES), lambda i: (0, 0))],
        out_shape=[jax.ShapeDtypeStruct((n, d), F32), jax.ShapeDtypeStruct((n, d), F32),
                   jax.ShapeDtypeStruct((n, LANES), F32), jax.ShapeDtypeStruct((8, n), I32),
                   jax.ShapeDtypeStruct((8, n), I32), jax.ShapeDtypeStruct((N_EXPERTS, LANES), F32)],
        scratch_shapes=[pltpu.VMEM((N_EXPERTS, 1), F32)],
        compiler_params=_params("arbitrary"),
        name="router",
    )(d_out, f_out, x, w_out, g_moe, w_router_t, b_router_t)


def _dispatch_kernel(ends_ref, pos_ref, x_ref, xs_ref, zero_ref, sem):
    tm = x_ref.shape[0]
    te = zero_ref.shape[0]

    @pl.when(pl.program_id(0) == 0)
    def _():
        zero_ref[...] = jnp.zeros_like(zero_ref)
        for e in range(N_EXPERTS):
            end = ends_ref[e]
            start = ends_ref[e - 1] if e else 0

            @pl.when(end > start)
            def _():
                last_tile = pl.multiple_of(end - te, te)
                copy = pltpu.make_async_copy(zero_ref, xs_ref.at[pl.ds(last_tile, te), :], sem)
                copy.start()
                copy.wait()

    def issue(g, carry):
        base = pl.multiple_of(g * ROW_GROUP, ROW_GROUP)
        for j in range(ROW_GROUP):
            for k in range(TOP_K):
                dst = pos_ref[k * tm + base + j]
                pltpu.make_async_copy(x_ref.at[pl.ds(base + j, 1), :], xs_ref.at[pl.ds(dst, 1), :],
                                      sem).start(priority=k % 2)
        return carry

    lax.fori_loop(0, tm // ROW_GROUP, issue, 0)
    for _ in range(TOP_K):
        pltpu.make_async_copy(x_ref, xs_ref.at[pl.ds(0, tm), :], sem).wait()


def _dispatch(ends, pos_flat, xn, n_rows):
    n, d = xn.shape
    tm = TOKEN_TILE
    return pl.pallas_call(
        _dispatch_kernel,
        grid_spec=pltpu.PrefetchScalarGridSpec(
            num_scalar_prefetch=1,
            grid=(n // tm,),
            in_specs=[pl.BlockSpec((TOP_K * tm,), lambda i, ends: (i,), memory_space=pltpu.SMEM),
                      pl.BlockSpec((tm, d), lambda i, ends: (i, 0))],
            out_specs=pl.BlockSpec(memory_space=pl.ANY),
            scratch_shapes=[pltpu.VMEM((EXPERT_TILE, d), F32), pltpu.SemaphoreType.DMA(())]),
        out_shape=jax.ShapeDtypeStruct((n_rows, d), F32),
        compiler_params=_params("arbitrary"),
        name="dispatch",
    )(ends, pos_flat, xn)


def _expert_kernel(tile_expert_ref, n_used_ref, x_ref, w1_ref, b1_ref, w2_ref, b2_ref, y_ref,
                   w1b_ref, w2p_ref, w2b_ref):
    t = pl.program_id(0)
    fresh = (t == 0) | (tile_expert_ref[t] != tile_expert_ref[jnp.maximum(t - 1, 0)])
    used = t < n_used_ref[0]

    @pl.when(fresh & used)
    def _():
        rows = LANES
        half = LANES // 2
        for r in range(0, w1b_ref.shape[0], rows):
            w1b_ref[r:r + rows, :] = w1_ref[0, r:r + rows, :].astype(BF16)
        for c in range(w2p_ref.shape[0]):
            lanes = slice(c * LANES, (c + 1) * LANES)
            for r in range(0, w2b_ref.shape[0], rows):
                for b in range(2):
                    w2p_ref[c, pl.ds(r + b, half, stride=2), :] = (
                        w2_ref[0, r + b * half:r + (b + 1) * half, lanes])
            w2b_ref[:, lanes] = w2p_ref[c].astype(BF16)

    @pl.when(used)
    def _():
        xb = x_ref[...].astype(BF16)
        h = jnp.dot(xb, w1b_ref[...], preferred_element_type=F32) + b1_ref[0]
        even = lax.broadcasted_iota(I32, (1, LANES), 1) % 2 == 0
        parts = []
        for q in range(h.shape[1] // (2 * LANES)):
            a = h[:, (2 * q) * LANES:(2 * q + 1) * LANES]
            b = h[:, (2 * q + 1) * LANES:(2 * q + 2) * LANES]
            hg = jnp.where(even, a, pltpu.roll(b, 1, 1))
            hl = jnp.where(even, pltpu.roll(a, LANES - 1, 1), b)
            glu = jnp.minimum(hg, SWIGLU_LIMIT)
            lin = jnp.clip(hl, -SWIGLU_LIMIT, SWIGLU_LIMIT)
            parts.append((glu * jax.nn.sigmoid(SWIGLU_ALPHA * glu) * (lin + 1.0)).astype(BF16))
        act = jnp.concatenate(parts, axis=1)
        y_ref[...] = jnp.dot(act, w2b_ref[...], preferred_element_type=F32) + b2_ref[0]

    @pl.when(jnp.logical_not(used))
    def _():
        y_ref[...] = jnp.zeros_like(y_ref)


def _experts(tile_expert, n_used, xs, w1, b1, w2, b2):
    n_rows, d = xs.shape
    tm = EXPERT_TILE
    by_expert = lambda arr: pl.BlockSpec((1,) + arr.shape[1:], lambda i, te, nu: (te[i], 0, 0))
    rows = pl.BlockSpec((tm, d), lambda i, te, nu: (i, 0))
    used_rows = pl.BlockSpec((tm, d), lambda i, te, nu: (jnp.minimum(i, nu[0] - 1), 0))
    return pl.pallas_call(
        _expert_kernel,
        grid_spec=pltpu.PrefetchScalarGridSpec(
            num_scalar_prefetch=2,
            grid=(n_rows // tm,),
            in_specs=[used_rows, by_expert(w1), by_expert(b1), by_expert(w2), by_expert(b2)],
            out_specs=rows,
            scratch_shapes=[pltpu.VMEM(w1.shape[1:], BF16),
                            pltpu.VMEM((w2.shape[2] // LANES, w2.shape[1], LANES), F32),
                            pltpu.VMEM(w2.shape[1:], BF16)]),
        out_shape=jax.ShapeDtypeStruct(xs.shape, F32),
        compiler_params=pltpu.CompilerParams(dimension_semantics=("arbitrary",),
                                             vmem_limit_bytes=EXPERT_VMEM_LIMIT),
        name="experts",
    )(tile_expert, n_used, xs, w1, b1, w2, b2)


def _combine_kernel(pos_ref, h_ref, gate_ref, p_ref, wg_ref, wp_ref, gple_ref, gfin_ref, y_ref,
                    o_ref, ybuf, sem, *, final_norm):
    tm = h_ref.shape[0]

    def issue(g, carry):
        base = pl.multiple_of(g * ROW_GROUP, ROW_GROUP)
        for j in range(ROW_GROUP):
            for k in range(TOP_K):
                src = pos_ref[k * tm + base + j]
                pltpu.make_async_copy(y_ref.at[pl.ds(src, 1), :], ybuf.at[k, pl.ds(base + j, 1), :],
                                      sem).start(priority=k % 2)
        return carry

    lax.fori_loop(0, tm // ROW_GROUP, issue, 0)
    for k in range(TOP_K):
        pltpu.make_async_copy(y_ref.at[pl.ds(0, tm), :], ybuf.at[k], sem).wait()

    gates = gate_ref[...]
    moe = gates[:, 0:1] * ybuf[0]
    for k in range(1, TOP_K):
        moe = moe + gates[:, k:k + 1] * ybuf[k]
    h = h_ref[...] + moe
    gate = jax.nn.sigmoid(jnp.dot(_rms(h, gple_ref[...]).astype(BF16), wg_ref[...],
                                  preferred_element_type=F32))
    emb = jnp.dot(p_ref[...].astype(BF16), wp_ref[...], preferred_element_type=F32)
    h = h + gate * emb
    o_ref[...] = _rms(h, gfin_ref[...]) if final_norm else h


def _combine(pos_flat, h, gates, p, w_gate, w_proj, g_ple, g_final, y, final_norm):
    n, d = h.shape
    tm = TOKEN_TILE
    tile = lambda width: pl.BlockSpec((tm, width), lambda i: (i, 0))
    full = lambda arr: pl.BlockSpec(arr.shape, lambda i: (0,) * arr.ndim)
    return pl.pallas_call(
        functools.partial(_combine_kernel, final_norm=final_norm),
        grid=(n // tm,),
        in_specs=[pl.BlockSpec((TOP_K * tm,), lambda i: (i,), memory_space=pltpu.SMEM),
                  tile(d), tile(LANES), tile(p.shape[1]), full(w_gate), full(w_proj),
                  full(g_ple), full(g_final), pl.BlockSpec(memory_space=pl.ANY)],
        out_specs=tile(d),
        out_shape=jax.ShapeDtypeStruct((n, d), F32),
        scratch_shapes=[pltpu.VMEM((TOP_K, tm, d), F32), pltpu.SemaphoreType.DMA(())],
        compiler_params=_params("arbitrary"),
        name="combine",
    )(pos_flat, h, gates, p, w_gate, w_proj, g_ple, g_final, y)


def _layer(h, p, cosf, sinf, g_attn, w_in, lq1, lk1, lq2, lk2, g_subln, b_forget, w_out, g_moe,
           w_router, b_router, w_e1, b_e1, w_e2, b_e2, g_ple, w_ple_gate, w_ple_proj, g_final,
           lambda_init, final_norm):
    b, s, d = h.shape
    n = b * s
    row = lambda v: v.reshape(1, -1)

    w_main = w_in[:, :MAIN_WIDTH].astype(BF16)
    w_fz = jnp.repeat(w_in[:, MAIN_WIDTH:], HEAD_DIM, axis=1).astype(BF16)
    b_fz = row(jnp.repeat(b_forget, HEAD_DIM))
    pad_heads = ((0, 2 * N_FOX_HEADS - N_FOX_HEADS), (0, 0))
    w_fz_t = jnp.pad(w_in[:, MAIN_WIDTH:].T, pad_heads).astype(BF16)
    b_fz_t = jnp.pad(b_forget.reshape(-1, 1), pad_heads)
    b1 = b_e1[:, None, :]
    b2 = b_e2[:, None, :]

    dq, dk, dv, fq, fk, fv, c, c_t = _inproj(h, row(g_attn), w_main, w_fz, b_fz, w_fz_t, b_fz_t,
                                            cosf, sinf)
    ta = ATTN_TILE
    seq_last = lambda a: a.transpose(0, 2, 1)
    key_tiles = lambda a: a.reshape(b, s // ta, ta, a.shape[2]).transpose(0, 1, 3, 2)
    c_rows = c_t.reshape(b, N_FOX_HEADS // 2, 2, s)
    d_out_t, f_out_t = _attention(seq_last(dq), dk, key_tiles(dv), seq_last(fq), fk, key_tiles(fv),
                                  c_rows, c, row(lq1), row(lk1), row(lq2), row(lk2),
                                  g_subln.reshape(-1, 1), lambda_init)
    d_out, f_out = seq_last(d_out_t), seq_last(f_out_t)

    h1, xn, gates, idx, rank, counts = _router(
        d_out.reshape(n, DIFF_WIDTH), f_out.reshape(n, FOX_WIDTH), h.reshape(n, d),
        w_out.astype(BF16), row(g_moe), w_router.T, b_router.reshape(-1, 1))

    te = EXPERT_TILE
    n_rows = n * TOP_K + N_EXPERTS * te
    counts = counts[:, 0].astype(I32)
    padded = (counts + te - 1) // te * te
    ends = jnp.cumsum(padded)
    starts = ends - padded
    experts = jnp.arange(N_EXPERTS, dtype=I32)[:, None, None]
    pos = rank[:TOP_K] + jnp.sum(jnp.where(idx[None, :TOP_K] == experts, starts[:, None, None], 0),
                                 axis=0)
    tm = TOKEN_TILE
    pos_flat = pos.reshape(TOP_K, n // tm, tm).transpose(1, 0, 2).reshape(-1)
    tile_start = jnp.arange(n_rows // te, dtype=I32) * te
    tile_expert = jnp.minimum(jnp.sum((ends[None, :] <= tile_start[:, None]).astype(I32), axis=1),
                              N_EXPERTS - 1)
    n_used = (ends[-1:] // te).astype(I32)

    xs = _dispatch(ends.astype(I32), pos_flat, xn, n_rows)
    y = _experts(tile_expert, n_used, xs, w_e1, b1, w_e2, b2)
    out = _combine(pos_flat, h1, gates, p.reshape(n, PLE_DIM), w_ple_gate.astype(BF16),
                   w_ple_proj.astype(BF16), row(g_ple), row(g_final), y, final_norm)
    return out.reshape(b, s, d)


def kernel(x, p, positions, g_attn, w_in, lambda_q1, lambda_k1, lambda_q2, lambda_k2, g_subln, b_forget, w_out, g_moe, w_router, b_router, w_e1, b_e1, w_e2, b_e2, g_ple, w_ple_gate, w_ple_proj, g_final):
    depth = g_attn.shape[0]
    cosf, sinf = _rope_tables(positions)
    h = x
    for i in range(depth):
        lambda_init = 0.8 - 0.6 * math.exp(-0.3 * i)
        h = _layer(h, p[i], cosf, sinf, g_attn[i], w_in[i], lambda_q1[i], lambda_k1[i],
                   lambda_q2[i], lambda_k2[i], g_subln[i], b_forget[i], w_out[i], g_moe[i],
                   w_router[i], b_router[i], w_e1[i], b_e1[i], w_e2[i], b_e2[i], g_ple[i],
                   w_ple_gate[i], w_ple_proj[i], g_final, lambda_init, i == depth - 1)
    return h
```

```python
import functools
import math

import jax
import jax.numpy as jnp
from jax import lax
from jax.experimental import pallas as pl
from jax.experimental.pallas import tpu as pltpu

F32 = jnp.float32
BF16 = jnp.bfloat16
I32 = jnp.int32

LANES = 128
D_MODEL = 1024
HEAD_DIM = 64
N_DIFF_HEADS = 4
DIFF_V_DIM = 2 * HEAD_DIM
N_FOX_HEADS = 8
DIFF_WIDTH = N_DIFF_HEADS * DIFF_V_DIM
FOX_WIDTH = N_FOX_HEADS * HEAD_DIM
ROT_DIM = HEAD_DIM // 4
ROPE_THETA = 500000.0
N_EXPERTS = 32
TOP_K = 4
D_EXPERT = D_MODEL
SWIGLU_ALPHA = 1.702
SWIGLU_LIMIT = 7.0
PLE_DIM = 256
NORM_EPS = 1e-5
MAIN_WIDTH = 3 * DIFF_WIDTH + 3 * FOX_WIDTH

SEQ_TILE = 256
ATTN_TILE = 256
TOKEN_TILE = 256
EXPERT_TILE = 256
ROW_GROUP = 8
VMEM_LIMIT = 48 * 1024 * 1024
EXPERT_VMEM_LIMIT = 58 * 1024 * 1024


def _params(*semantics):
    return pltpu.CompilerParams(dimension_semantics=semantics, vmem_limit_bytes=VMEM_LIMIT)


def _rms(x, g):
    return x * lax.rsqrt(jnp.mean(x * x, axis=-1, keepdims=True) + NORM_EPS) * g


def _rope_table_kernel(pos_ref, cos_ref, sin_ref):
    pos = pos_ref[0].astype(F32)
    d = lax.broadcasted_iota(I32, (1, LANES), 1) % HEAD_DIM
    j = d % (ROT_DIM // 2)
    inv_freq = jnp.power(ROPE_THETA, -(2 * j).astype(F32) / ROT_DIM)
    ang = pos * inv_freq
    c, s = jnp.cos(ang), jnp.sin(ang)
    cos_ref[0] = jnp.where(d < ROT_DIM, c, 1.0)
    sin_ref[0] = jnp.where(d < ROT_DIM // 2, -s, jnp.where(d < ROT_DIM, s, 0.0))


def _rope_tables(positions):
    b, s = positions.shape
    ts = SEQ_TILE
    return pl.pallas_call(
        _rope_table_kernel,
        grid=(b, s // ts),
        in_specs=[pl.BlockSpec((1, ts, 1), lambda i, j: (i, j, 0))],
        out_specs=[pl.BlockSpec((1, ts, LANES), lambda i, j: (i, j, 0))] * 2,
        out_shape=[jax.ShapeDtypeStruct((b, s, LANES), F32)] * 2,
        compiler_params=_params("arbitrary", "arbitrary"),
        name="rope_tables",
    )(positions.reshape(b, s, 1))


def _inproj_kernel(x_ref, g_ref, w_ref, wfz_ref, bfz_ref, wfzt_ref, bfzt_ref, cos_ref, sin_ref,
                   dq_ref, dk_ref, dv_ref, fq_ref, fk_ref, fv_ref, c_ref, ct_ref,
                   carry_ref, carry_t_ref):
    ts = x_ref.shape[1]
    xb = _rms(x_ref[0], g_ref[...]).astype(BF16)
    proj = jnp.dot(xb, w_ref[...], preferred_element_type=F32)

    cosf, sinf = cos_ref[0], sin_ref[0]
    first_half = lax.broadcasted_iota(I32, (1, LANES), 1) % HEAD_DIM < ROT_DIM // 2

    def rope(t):
        outs = []
        for j in range(t.shape[1] // LANES):
            tj = t[:, j * LANES:(j + 1) * LANES]
            partner = jnp.where(first_half,
                                pltpu.roll(tj, LANES - ROT_DIM // 2, 1),
                                pltpu.roll(tj, ROT_DIM // 2, 1))
            outs.append(tj * cosf + partner * sinf)
        return jnp.concatenate(outs, axis=1)

    scale = HEAD_DIM ** -0.5
    w = DIFF_WIDTH
    dq_ref[0] = (rope(proj[:, 0:w]) * scale).astype(BF16)
    dk_ref[0] = rope(proj[:, w:2 * w]).astype(BF16)
    dv_ref[0] = proj[:, 2 * w:3 * w].astype(BF16)
    fq_ref[0] = (proj[:, 3 * w:4 * w] * scale).astype(BF16)
    fk_ref[0] = proj[:, 4 * w:5 * w].astype(BF16)
    fv_ref[0] = proj[:, 5 * w:6 * w].astype(BF16)

    def log_sigmoid(z):
        return jnp.minimum(z, 0.0) - jnp.log1p(jnp.exp(-jnp.abs(z)))

    def pieces(v):
        hi = v.astype(BF16)
        r1 = v - hi.astype(F32)
        mid = r1.astype(BF16)
        return hi, mid, (r1 - mid.astype(F32)).astype(BF16)

    row = lax.broadcasted_iota(I32, (ts, ts), 0)
    col = lax.broadcasted_iota(I32, (ts, ts), 1)
    upto_row = jnp.where(col <= row, 1.0, 0.0).astype(BF16)
    upto_col = jnp.where(row <= col, 1.0, 0.0).astype(BF16)

    @pl.when(pl.program_id(1) == 0)
    def _():
        carry_ref[...] = jnp.zeros_like(carry_ref)
        carry_t_ref[...] = jnp.zeros_like(carry_t_ref)

    fz = jnp.dot(xb, wfz_ref[...], preferred_element_type=F32) + bfz_ref[...]
    c = carry_ref[...] + sum(jnp.dot(upto_row, part, preferred_element_type=F32)
                             for part in pieces(log_sigmoid(fz)))
    c_ref[0] = c
    carry_ref[...] = c[ts - 1:ts, :]

    fzt = lax.dot_general(wfzt_ref[...], xb, (((1,), (1,)), ((), ())),
                          preferred_element_type=F32) + bfzt_ref[...]
    ct = carry_t_ref[...] + sum(jnp.dot(part, upto_col, preferred_element_type=F32)
                                for part in pieces(log_sigmoid(fzt)))
    ct_ref[0] = ct[0:N_FOX_HEADS, :]
    carry_t_ref[...] = ct[:, ts - 1:ts]


def _inproj(x, g, w_main, w_fz, b_fz, w_fz_t, b_fz_t, cosf, sinf):
    b, s, d = x.shape
    ts = SEQ_TILE
    tile = lambda width: pl.BlockSpec((1, ts, width), lambda i, j: (i, j, 0))
    full = lambda arr: pl.BlockSpec(arr.shape, lambda i, j: (0,) * arr.ndim)
    heads = jax.ShapeDtypeStruct((b, s, DIFF_WIDTH), BF16)
    return pl.pallas_call(
        _inproj_kernel,
        grid=(b, s // ts),
        in_specs=[tile(d), full(g), full(w_main), full(w_fz), full(b_fz), full(w_fz_t), full(b_fz_t),
                  tile(LANES), tile(LANES)],
        out_specs=[tile(DIFF_WIDTH)] * 6 + [tile(FOX_WIDTH),
                                            pl.BlockSpec((1, N_FOX_HEADS, ts), lambda i, j: (i, 0, j))],
        out_shape=[heads] * 6 + [jax.ShapeDtypeStruct((b, s, FOX_WIDTH), F32),
                                 jax.ShapeDtypeStruct((b, N_FOX_HEADS, s), F32)],
        scratch_shapes=[pltpu.VMEM((1, FOX_WIDTH), F32), pltpu.VMEM((w_fz_t.shape[0], 1), F32)],
        compiler_params=_params("arbitrary", "arbitrary"),
        name="inproj",
    )(x, g, w_main, w_fz, b_fz, w_fz_t, b_fz_t, cosf, sinf)


ONES_ROWS = 16


def _block_diag_queries(qt):
    row = lax.broadcasted_iota(I32, qt.shape, 0)
    zero = jnp.zeros_like(qt)
    return jnp.concatenate([jnp.where(row < HEAD_DIM, qt, zero),
                            jnp.where(row >= HEAD_DIM, qt, zero)], axis=1)


class _Chain:
    def __init__(self, qbd, k_ref, vt_ref, acc_ref, s_ref, p_ref, stat_ref,
                 key_bias=None, query_bias=None):
        self.qbd, self.k_ref, self.vt_ref = qbd, k_ref, vt_ref
        self.acc_ref, self.s_ref, self.p_ref, self.stat_ref = acc_ref, s_ref, p_ref, stat_ref
        self.key_bias, self.query_bias = key_bias, query_bias


_RUNNING_MAX, _TILE_MAX, _RESCALE = 0, 1, 2


def _causal_attention(chains, qi):
    tk = chains[0].vt_ref.shape[3]
    tq = chains[0].qbd.shape[1] // 2
    ones = jnp.ones((ONES_ROWS, tk), BF16)

    def scores(ch, i, diagonal):
        ks = ch.k_ref[0, pl.ds(pl.multiple_of(i * tk, tk), tk), :]
        s = jnp.dot(ks, ch.qbd, preferred_element_type=F32)
        if ch.key_bias is not None:
            s = s - ch.key_bias(i)
        if diagonal is not False:
            r = lax.broadcasted_iota(I32, s.shape, 0)
            c = lax.broadcasted_iota(I32, s.shape, 1) & (tq - 1)
            visible = r <= c if diagonal is True else jnp.logical_or(r <= c, jnp.logical_not(diagonal))
            s = jnp.where(visible, s, -jnp.inf)
        ch.s_ref[...] = s
        top = jnp.max(s, axis=0, keepdims=True)
        if ch.query_bias is not None:
            top = top + ch.query_bias
        ch.stat_ref[_TILE_MAX:_TILE_MAX + 1, :] = top

    def probs(ch):
        m = ch.stat_ref[_RUNNING_MAX:_RUNNING_MAX + 1, :]
        m_new = jnp.maximum(m, ch.stat_ref[_TILE_MAX:_TILE_MAX + 1, :])
        ch.stat_ref[_RUNNING_MAX:_RUNNING_MAX + 1, :] = m_new
        ch.stat_ref[_RESCALE:_RESCALE + 1, :] = jnp.exp(m - m_new)
        ref = m_new if ch.query_bias is None else m_new - ch.query_bias
        ch.p_ref[...] = jnp.exp(ch.s_ref[...] - ref).astype(BF16)

    def accumulate(ch, i):
        vts = jnp.concatenate([ch.vt_ref[0, i], ones], axis=0)
        ch.acc_ref[...] = (ch.stat_ref[_RESCALE:_RESCALE + 1, :] * ch.acc_ref[...]
                           + jnp.dot(vts, ch.p_ref[...], preferred_element_type=F32))

    def body(k, diagonal_next):
        for ch in chains:
            accumulate(ch, jnp.maximum(k - 1, 0))
        for ch in chains:
            probs(ch)
        for ch in chains:
            scores(ch, k + 1, diagonal_next)

    for ch in chains:
        ch.acc_ref[...] = jnp.zeros_like(ch.acc_ref)
        ch.p_ref[...] = jnp.zeros_like(ch.p_ref)
        ch.stat_ref[...] = jnp.concatenate(
            [jnp.full((1, 2 * tq), -jnp.inf, F32), jnp.ones((ch.stat_ref.shape[0] - 1, 2 * tq), F32)], axis=0)
        scores(ch, 0, qi == 0)

    def loop_body(k, carry):
        body(k, False)
        return carry

    lax.fori_loop(0, qi - 1, loop_body, 0)

    @pl.when(qi >= 1)
    def _():
        body(qi - 1, True)

    for ch in chains:
        accumulate(ch, jnp.maximum(qi - 1, 0))
    for ch in chains:
        probs(ch)
    for ch in chains:
        accumulate(ch, qi)


def _attention_kernel(lq1_ref, lk1_ref, lq2_ref, lk2_ref, g_ref,
                      dqt_ref, dk_ref, dvt_ref, fqt_ref, fk_ref, fvt_ref, cq_ref, ck_ref,
                      do_ref, fo_ref, dacc_ref, ds_ref, dp_ref, dstat_ref,
                      facc_ref, fs_ref, fp_ref, fstat_ref, *, lambda_init):
    tq = dqt_ref.shape[2]
    tk = dvt_ref.shape[3]
    cq = jnp.concatenate([cq_ref[0, 0, 0:1, :], cq_ref[0, 0, 1:2, :]], axis=1)
    lane = lax.broadcasted_iota(I32, (1, LANES), 1)

    def key_bias(i):
        ck = ck_ref[0, pl.ds(pl.multiple_of(i * tk, tk), tk), :]
        other = pltpu.roll(ck, HEAD_DIM, 1)
        first = jnp.where(lane < HEAD_DIM, ck, other)
        second = jnp.where(lane >= HEAD_DIM, ck, other)
        return jnp.concatenate([first] * (tq // LANES) + [second] * (tq // LANES), axis=1)

    _causal_attention(
        [_Chain(_block_diag_queries(dqt_ref[0]), dk_ref, dvt_ref, dacc_ref, ds_ref, dp_ref, dstat_ref),
         _Chain(_block_diag_queries(fqt_ref[0]), fk_ref, fvt_ref, facc_ref, fs_ref, fp_ref, fstat_ref,
                key_bias, cq)],
        pl.program_id(2))

    lam = (jnp.exp(jnp.sum(lq1_ref[...] * lk1_ref[...], axis=-1, keepdims=True))
           - jnp.exp(jnp.sum(lq2_ref[...] * lk2_ref[...], axis=-1, keepdims=True)) + lambda_init)
    dv = DIFF_V_DIM
    o = (dacc_ref[0:dv, 0:tq] / dacc_ref[dv:dv + 1, 0:tq]
         - lam * (dacc_ref[0:dv, tq:] / dacc_ref[dv:dv + 1, tq:]))
    ms = jnp.mean(o * o, axis=0, keepdims=True)
    do_ref[0] = (o * lax.rsqrt(ms + NORM_EPS) * g_ref[...] * (1.0 - lambda_init)).astype(BF16)

    h = HEAD_DIM
    fo_ref[0] = jnp.concatenate([facc_ref[0:h, 0:tq] / facc_ref[2 * h:2 * h + 1, 0:tq],
                                 facc_ref[h:2 * h, tq:] / facc_ref[2 * h:2 * h + 1, tq:]],
                                axis=0).astype(BF16)


def _attention(dqt, dk, dvt, fqt, fk, fvt, c_rows, c_cols, lq1, lk1, lq2, lk2, g_col, lambda_init):
    b, s, _ = dk.shape
    t = ATTN_TILE
    assert N_DIFF_HEADS == N_FOX_HEADS // 2
    vec = lambda arr: pl.BlockSpec(arr.shape, lambda i, h, j: (0, 0))
    q_spec = pl.BlockSpec((1, LANES, t), lambda i, h, j: (i, h, j))
    k_spec = pl.BlockSpec((1, s, LANES), lambda i, h, j: (i, 0, h))
    v_spec = pl.BlockSpec((1, s // t, LANES, t), lambda i, h, j: (i, 0, h, 0))
    out = jax.ShapeDtypeStruct((b, DIFF_WIDTH, s), BF16)
    chain_scratch = [pltpu.VMEM((LANES + ONES_ROWS, 2 * t), F32),
                     pltpu.VMEM((t, 2 * t), F32),
                     pltpu.VMEM((t, 2 * t), BF16),
                     pltpu.VMEM((8, 2 * t), F32)]
    return pl.pallas_call(
        functools.partial(_attention_kernel, lambda_init=lambda_init),
        grid=(b, N_DIFF_HEADS, s // t),
        in_specs=[vec(lq1), vec(lk1), vec(lq2), vec(lk2), vec(g_col),
                  q_spec, k_spec, v_spec, q_spec, k_spec, v_spec,
                  pl.BlockSpec((1, 1, 2, t), lambda i, h, j: (i, h, 0, j)), k_spec],
        out_specs=[q_spec, q_spec],
        out_shape=[out, out],
        scratch_shapes=chain_scratch * 2,
        compiler_params=_params("arbitrary", "arbitrary", "arbitrary"),
        name="attention",
    )(lq1, lk1, lq2, lk2, g_col, dqt, dk, dvt, fqt, fk, fvt, c_rows, c_cols)


def _router_kernel(d_ref, f_ref, x_ref, wo_ref, g_ref, wrt_ref, brt_ref,
                   h_ref, xn_ref, gate_ref, idx_ref, rank_ref, count_ref, carry_ref):
    tm = x_ref.shape[0]
    mixed = (jnp.dot(d_ref[...], wo_ref[0:DIFF_WIDTH, :], preferred_element_type=F32)
             + jnp.dot(f_ref[...], wo_ref[DIFF_WIDTH:, :], preferred_element_type=F32))
    h = x_ref[...] + mixed
    h_ref[...] = h
    xn = _rms(h, g_ref[...])
    xn_ref[...] = xn

    def pieces(v, count):
        out = []
        for _ in range(count):
            part = v.astype(BF16)
            out.append(part)
            v = v - part.astype(F32)
        return out

    nt = (((1,), (1,)), ((), ()))
    xh, xl = pieces(xn, 2)
    wh, wl = pieces(wrt_ref[...], 2)
    work = (lax.dot_general(wh, xh, nt, preferred_element_type=F32)
            + lax.dot_general(wh, xl, nt, preferred_element_type=F32)
            + lax.dot_general(wl, xh, nt, preferred_element_type=F32)) + brt_ref[...]
    expert = lax.broadcasted_iota(I32, work.shape, 0)

    vals, ids, picks = [], [], []
    for _ in range(TOP_K):
        m = jnp.max(work, axis=0, keepdims=True)
        idx = jnp.min(jnp.where(work == m, expert, N_EXPERTS), axis=0, keepdims=True)
        pick = expert == idx
        vals.append(m)
        ids.append(idx)
        picks.append(pick)
        work = jnp.where(pick, -jnp.inf, work)
    exps = [jnp.exp(v - vals[0]) for v in vals]
    denom = exps[0] + exps[1] + exps[2] + exps[3]

    chosen = jnp.where(picks[0] | picks[1] | picks[2] | picks[3], 1.0, 0.0)
    row = lax.broadcasted_iota(I32, (tm, tm), 0)
    col = lax.broadcasted_iota(I32, (tm, tm), 1)
    earlier = jnp.where(row < col, 1.0, 0.0).astype(BF16)

    @pl.when(pl.program_id(0) == 0)
    def _():
        carry_ref[...] = jnp.zeros_like(carry_ref)

    rank = jnp.dot(chosen.astype(BF16), earlier, preferred_element_type=F32) + carry_ref[...]
    carry_ref[...] = carry_ref[...] + jnp.sum(chosen, axis=1, keepdims=True)
    count_ref[...] = jnp.broadcast_to(carry_ref[...], count_ref.shape)

    slot = lax.broadcasted_iota(I32, (8, tm), 0)
    gate_t = jnp.zeros((8, tm), F32)
    idx_t = jnp.zeros((8, tm), I32)
    rank_t = jnp.zeros((8, tm), F32)
    for k in range(TOP_K):
        gate_t = jnp.where(slot == k, exps[k] / denom, gate_t)
        idx_t = jnp.where(slot == k, ids[k], idx_t)
        rank_t = jnp.where(slot == k, jnp.sum(jnp.where(picks[k], rank, 0.0), axis=0, keepdims=True), rank_t)
    idx_ref[...] = idx_t
    rank_ref[...] = rank_t.astype(I32)
    gate_rows = jnp.concatenate([gate_t, jnp.zeros((LANES - 8, tm), F32)], axis=0)
    eye = jnp.where(row == col, 1.0, 0.0).astype(BF16)
    gate_ref[...] = sum(lax.dot_general(eye, part, nt, preferred_element_type=F32)
                        for part in pieces(gate_rows, 3))


def _router(d_out, f_out, x, w_out, g_moe, w_router_t, b_router_t):
    n, d = x.shape
    tm = TOKEN_TILE
    tile = lambda width: pl.BlockSpec((tm, width), lambda i: (i, 0))
    rows = pl.BlockSpec((8, tm), lambda i: (0, i))
    full = lambda arr: pl.BlockSpec(arr.shape, lambda i: (0,) * arr.ndim)
    return pl.pallas_call(
        _router_kernel,
        grid=(n // tm,),
        in_specs=[tile(DIFF_WIDTH), tile(FOX_WIDTH), tile(d), full(w_out), full(g_moe),
                  full(w_router_t), full(b_router_t)],
        out_specs=[tile(d), tile(d), tile(LANES), rows, rows,
                   pl.BlockSpec((N_EXPERTS, LANES), lambda i: (0, 0))],
        out_shape=[jax.ShapeDtypeStruct((n, d), F32), jax.ShapeDtypeStruct((n, d), F32),
                   jax.ShapeDtypeStruct((n, LANES), F32), jax.ShapeDtypeStruct((8, n), I32),
                   jax.ShapeDtypeStruct((8, n), I32), jax.ShapeDtypeStruct((N_EXPERTS, LANES), F32)],
        scratch_shapes=[pltpu.VMEM((N_EXPERTS, 1), F32)],
        compiler_params=_params("arbitrary"),
        name="router",
    )(d_out, f_out, x, w_out, g_moe, w_router_t, b_router_t)


def _dispatch_kernel(ends_ref, pos_ref, x_ref, xs_ref, zero_ref, sem):
    tm = x_ref.shape[0]
    te = zero_ref.shape[0]

    @pl.when(pl.program_id(0) == 0)
    def _():
        zero_ref[...] = jnp.zeros_like(zero_ref)

        def zero_tile(first_row):
            copy = pltpu.make_async_copy(zero_ref, xs_ref.at[pl.ds(pl.multiple_of(first_row, te), te), :], sem)
            copy.start()
            copy.wait()

        for e in range(N_EXPERTS):
            end = ends_ref[e]
            start = ends_ref[e - 1] if e else 0
            pl.when(end > start)(functools.partial(zero_tile, end - te))

        def unused(t, carry):
            zero_tile(t * te)
            return carry

        lax.fori_loop(ends_ref[N_EXPERTS - 1] // te, xs_ref.shape[0] // te, unused, 0)

    def issue(g, carry):
        base = pl.multiple_of(g * ROW_GROUP, ROW_GROUP)
        for j in range(ROW_GROUP):
            for k in range(TOP_K):
                dst = pos_ref[k * tm + base + j]
                pltpu.make_async_copy(x_ref.at[pl.ds(base + j, 1), :], xs_ref.at[pl.ds(dst, 1), :],
                                      sem).start(priority=k % 2)
        return carry

    lax.fori_loop(0, tm // ROW_GROUP, issue, 0)
    for _ in range(TOP_K):
        pltpu.make_async_copy(x_ref, xs_ref.at[pl.ds(0, tm), :], sem).wait()


def _dispatch(ends, pos_flat, xn, n_rows):
    n, d = xn.shape
    tm = TOKEN_TILE
    return pl.pallas_call(
        _dispatch_kernel,
        grid_spec=pltpu.PrefetchScalarGridSpec(
            num_scalar_prefetch=1,
            grid=(n // tm,),
            in_specs=[pl.BlockSpec((TOP_K * tm,), lambda i, ends: (i,), memory_space=pltpu.SMEM),
                      pl.BlockSpec((tm, d), lambda i, ends: (i, 0))],
            out_specs=pl.BlockSpec(memory_space=pl.ANY),
            scratch_shapes=[pltpu.VMEM((EXPERT_TILE, d), F32), pltpu.SemaphoreType.DMA(())]),
        out_shape=jax.ShapeDtypeStruct((n_rows, d), F32),
        compiler_params=_params("arbitrary"),
        name="dispatch",
    )(ends, pos_flat, xn)


def _expert_kernel(tile_expert_ref, n_used_ref, x_ref, w1_ref, b1_ref, w2_ref, b2_ref, y_ref,
                   w1b_ref, w2p_ref, w2b_ref):
    t = pl.program_id(0)
    fresh = (t == 0) | (tile_expert_ref[t] != tile_expert_ref[jnp.maximum(t - 1, 0)])
    used = t < n_used_ref[0]

    @pl.when(fresh & used)
    def _():
        rows = LANES
        half = LANES // 2
        for r in range(0, w1b_ref.shape[0], rows):
            w1b_ref[r:r + rows, :] = w1_ref[0, r:r + rows, :].astype(BF16)
        for c in range(w2p_ref.shape[0]):
            lanes = slice(c * LANES, (c + 1) * LANES)
            for r in range(0, w2b_ref.shape[0], rows):
                for b in range(2):
                    w2p_ref[c, pl.ds(r + b, half, stride=2), :] = (
                        w2_ref[0, r + b * half:r + (b + 1) * half, lanes])
            w2b_ref[:, lanes] = w2p_ref[c].astype(BF16)

    @pl.when(used)
    def _():
        xb = x_ref[...].astype(BF16)
        h = jnp.dot(xb, w1b_ref[...], preferred_element_type=F32) + b1_ref[0]
        even = lax.broadcasted_iota(I32, (1, LANES), 1) % 2 == 0
        parts = []
        for q in range(h.shape[1] // (2 * LANES)):
            a = h[:, (2 * q) * LANES:(2 * q + 1) * LANES]
            b = h[:, (2 * q + 1) * LANES:(2 * q + 2) * LANES]
            hg = jnp.where(even, a, pltpu.roll(b, 1, 1))
            hl = jnp.where(even, pltpu.roll(a, LANES - 1, 1), b)
            glu = jnp.minimum(hg, SWIGLU_LIMIT)
            lin = jnp.clip(hl, -SWIGLU_LIMIT, SWIGLU_LIMIT)
            parts.append((glu * jax.nn.sigmoid(SWIGLU_ALPHA * glu) * (lin + 1.0)).astype(BF16))
        act = jnp.concatenate(parts, axis=1)
        y_ref[...] = jnp.dot(act, w2b_ref[...], preferred_element_type=F32) + b2_ref[0]

    @pl.when(jnp.logical_not(used))
    def _():
        y_ref[...] = jnp.zeros_like(y_ref)


def _experts(tile_expert, n_used, xs, w1, b1, w2, b2):
    n_rows, d = xs.shape
    tm = EXPERT_TILE
    by_expert = lambda arr: pl.BlockSpec((1,) + arr.shape[1:], lambda i, te, nu: (te[i], 0, 0))
    rows = pl.BlockSpec((tm, d), lambda i, te, nu: (i, 0))
    used_rows = pl.BlockSpec((tm, d), lambda i, te, nu: (jnp.minimum(i, nu[0] - 1), 0))
    return pl.pallas_call(
        _expert_kernel,
        grid_spec=pltpu.PrefetchScalarGridSpec(
            num_scalar_prefetch=2,
            grid=(n_rows // tm,),
            in_specs=[used_rows, by_expert(w1), by_expert(b1), by_expert(w2), by_expert(b2)],
            out_specs=rows,
            scratch_shapes=[pltpu.VMEM(w1.shape[1:], BF16),
                            pltpu.VMEM((w2.shape[2] // LANES, w2.shape[1], LANES), F32),
                            pltpu.VMEM(w2.shape[1:], BF16)]),
        out_shape=jax.ShapeDtypeStruct(xs.shape, F32),
        compiler_params=pltpu.CompilerParams(dimension_semantics=("arbitrary",),
                                             vmem_limit_bytes=EXPERT_VMEM_LIMIT),
        name="experts",
    )(tile_expert, n_used, xs, w1, b1, w2, b2)


def _combine_kernel(pos_ref, h_ref, gate_ref, p_ref, wg_ref, wp_ref, gple_ref, gfin_ref, y_ref,
                    o_ref, ybuf, sem, *, final_norm):
    tm = h_ref.shape[0]

    def issue(g, carry):
        base = pl.multiple_of(g * ROW_GROUP, ROW_GROUP)
        for j in range(ROW_GROUP):
            for k in range(TOP_K):
                src = pos_ref[k * tm + base + j]
                pltpu.make_async_copy(y_ref.at[pl.ds(src, 1), :], ybuf.at[k, pl.ds(base + j, 1), :],
                                      sem).start(priority=k % 2)
        return carry

    lax.fori_loop(0, tm // ROW_GROUP, issue, 0)
    for k in range(TOP_K):
        pltpu.make_async_copy(y_ref.at[pl.ds(0, tm), :], ybuf.at[k], sem).wait()

    gates = gate_ref[...]
    moe = gates[:, 0:1] * ybuf[0]
    for k in range(1, TOP_K):
        moe = moe + gates[:, k:k + 1] * ybuf[k]
    h = h_ref[...] + moe
    gate = jax.nn.sigmoid(jnp.dot(_rms(h, gple_ref[...]).astype(BF16), wg_ref[...],
                                  preferred_element_type=F32))
    emb = jnp.dot(p_ref[...].astype(BF16), wp_ref[...], preferred_element_type=F32)
    h = h + gate * emb
    o_ref[...] = _rms(h, gfin_ref[...]) if final_norm else h


def _combine(pos_flat, h, gates, p, w_gate, w_proj, g_ple, g_final, y, final_norm):
    n, d = h.shape
    tm = TOKEN_TILE
    tile = lambda width: pl.BlockSpec((tm, width), lambda i: (i, 0))
    full = lambda arr: pl.BlockSpec(arr.shape, lambda i: (0,) * arr.ndim)
    return pl.pallas_call(
        functools.partial(_combine_kernel, final_norm=final_norm),
        grid=(n // tm,),
        in_specs=[pl.BlockSpec((TOP_K * tm,), lambda i: (i,), memory_space=pltpu.SMEM),
                  tile(d), tile(LANES), tile(p.shape[1]), full(w_gate), full(w_proj),
                  full(g_ple), full(g_final), pl.BlockSpec(memory_space=pl.ANY)],
        out_specs=tile(d),
        out_shape=jax.ShapeDtypeStruct((n, d), F32),
        scratch_shapes=[pltpu.VMEM((TOP_K, tm, d), F32), pltpu.SemaphoreType.DMA(())],
        compiler_params=_params("arbitrary"),
        name="combine",
    )(pos_flat, h, gates, p, w_gate, w_proj, g_ple, g_final, y)


def _layer(h, p, cosf, sinf, g_attn, w_in, lq1, lk1, lq2, lk2, g_subln, b_forget, w_out, g_moe,
           w_router, b_router, w_e1, b_e1, w_e2, b_e2, g_ple, w_ple_gate, w_ple_proj, g_final,
           lambda_init, final_norm):
    b, s, d = h.shape
    n = b * s
    row = lambda v: v.reshape(1, -1)

    w_main = w_in[:, :MAIN_WIDTH].astype(BF16)
    w_fz = jnp.repeat(w_in[:, MAIN_WIDTH:], HEAD_DIM, axis=1).astype(BF16)
    b_fz = row(jnp.repeat(b_forget, HEAD_DIM))
    pad_heads = ((0, 2 * N_FOX_HEADS - N_FOX_HEADS), (0, 0))
    w_fz_t = jnp.pad(w_in[:, MAIN_WIDTH:].T, pad_heads).astype(BF16)
    b_fz_t = jnp.pad(b_forget.reshape(-1, 1), pad_heads)
    b1 = b_e1[:, None, :]
    b2 = b_e2[:, None, :]

    dq, dk, dv, fq, fk, fv, c, c_t = _inproj(h, row(g_attn), w_main, w_fz, b_fz, w_fz_t, b_fz_t,
                                            cosf, sinf)
    ta = ATTN_TILE
    seq_last = lambda a: a.transpose(0, 2, 1)
    key_tiles = lambda a: a.reshape(b, s // ta, ta, a.shape[2]).transpose(0, 1, 3, 2)
    c_rows = c_t.reshape(b, N_FOX_HEADS // 2, 2, s)
    d_out_t, f_out_t = _attention(seq_last(dq), dk, key_tiles(dv), seq_last(fq), fk, key_tiles(fv),
                                  c_rows, c, row(lq1), row(lk1), row(lq2), row(lk2),
                                  g_subln.reshape(-1, 1), lambda_init)
    d_out, f_out = seq_last(d_out_t), seq_last(f_out_t)

    h1, xn, gates, idx, rank, counts = _router(
        d_out.reshape(n, DIFF_WIDTH), f_out.reshape(n, FOX_WIDTH), h.reshape(n, d),
        w_out.astype(BF16), row(g_moe), w_router.T, b_router.reshape(-1, 1))

    te = EXPERT_TILE
    n_rows = n * TOP_K + N_EXPERTS * te
    counts = counts[:, 0].astype(I32)
    padded = (counts + te - 1) // te * te
    ends = jnp.cumsum(padded)
    starts = ends - padded
    experts = jnp.arange(N_EXPERTS, dtype=I32)[:, None, None]
    pos = rank[:TOP_K] + jnp.sum(jnp.where(idx[None, :TOP_K] == experts, starts[:, None, None], 0),
                                 axis=0)
    tm = TOKEN_TILE
    pos_flat = pos.reshape(TOP_K, n // tm, tm).transpose(1, 0, 2).reshape(-1)
    tile_start = jnp.arange(n_rows // te, dtype=I32) * te
    tile_expert = jnp.minimum(jnp.sum((ends[None, :] <= tile_start[:, None]).astype(I32), axis=1),
                              N_EXPERTS - 1)
    n_used = (ends[-1:] // te).astype(I32)

    xs = _dispatch(ends.astype(I32), pos_flat, xn, n_rows)
    y = _experts(tile_expert, n_used, xs, w_e1, b1, w_e2, b2)
    out = _combine(pos_flat, h1, gates, p.reshape(n, PLE_DIM), w_ple_gate.astype(BF16),
                   w_ple_proj.astype(BF16), row(g_ple), row(g_final), y, final_norm)
    return out.reshape(b, s, d)


def kernel(x, p, positions, g_attn, w_in, lambda_q1, lambda_k1, lambda_q2, lambda_k2, g_subln, b_forget, w_out, g_moe, w_router, b_router, w_e1, b_e1, w_e2, b_e2, g_ple, w_ple_gate, w_ple_proj, g_final):
    depth = g_attn.shape[0]
    cosf, sinf = _rope_tables(positions)
    h = x
    for i in range(depth):
        lambda_init = 0.8 - 0.6 * math.exp(-0.3 * i)
        h = _layer(h, p[i], cosf, sinf, g_attn[i], w_in[i], lambda_q1[i], lambda_k1[i],
                   lambda_q2[i], lambda_k2[i], g_subln[i], b_forget[i], w_out[i], g_moe[i],
                   w_router[i], b_router[i], w_e1[i], b_e1[i], w_e2[i], b_e2[i], g_ple[i],
                   w_ple_gate[i], w_ple_proj[i], g_final, lambda_init, i == depth - 1)
    return h
```

```python
import functools
import math

import jax
import jax.numpy as jnp
from jax import lax
from jax.experimental import pallas as pl
from jax.experimental.pallas import tpu as pltpu

F32 = jnp.float32
BF16 = jnp.bfloat16
I32 = jnp.int32

LANES = 128
D_MODEL = 1024
HEAD_DIM = 64
N_DIFF_HEADS = 4
DIFF_V_DIM = 2 * HEAD_DIM
N_FOX_HEADS = 8
DIFF_WIDTH = N_DIFF_HEADS * DIFF_V_DIM
FOX_WIDTH = N_FOX_HEADS * HEAD_DIM
ROT_DIM = HEAD_DIM // 4
ROPE_THETA = 500000.0
N_EXPERTS = 32
TOP_K = 4
D_EXPERT = D_MODEL
SWIGLU_ALPHA = 1.702
SWIGLU_LIMIT = 7.0
PLE_DIM = 256
NORM_EPS = 1e-5
MAIN_WIDTH = 3 * DIFF_WIDTH + 3 * FOX_WIDTH

SEQ_TILE = 256
ATTN_TILE = 256
TOKEN_TILE = 256
EXPERT_TILE = 256
ROW_GROUP = 8
VMEM_LIMIT = 48 * 1024 * 1024
EXPERT_VMEM_LIMIT = 58 * 1024 * 1024


def _params(*semantics):
    return pltpu.CompilerParams(dimension_semantics=semantics, vmem_limit_bytes=VMEM_LIMIT)


def _rms(x, g):
    return x * lax.rsqrt(jnp.mean(x * x, axis=-1, keepdims=True) + NORM_EPS) * g


def _rope_table_kernel(pos_ref, cos_ref, sin_ref):
    pos = pos_ref[0].astype(F32)
    d = lax.broadcasted_iota(I32, (1, LANES), 1) % HEAD_DIM
    j = d % (ROT_DIM // 2)
    inv_freq = jnp.power(ROPE_THETA, -(2 * j).astype(F32) / ROT_DIM)
    ang = pos * inv_freq
    c, s = jnp.cos(ang), jnp.sin(ang)
    cos_ref[0] = jnp.where(d < ROT_DIM, c, 1.0)
    sin_ref[0] = jnp.where(d < ROT_DIM // 2, -s, jnp.where(d < ROT_DIM, s, 0.0))


def _rope_tables(positions):
    b, s = positions.shape
    ts = SEQ_TILE
    return pl.pallas_call(
        _rope_table_kernel,
        grid=(b, s // ts),
        in_specs=[pl.BlockSpec((1, ts, 1), lambda i, j: (i, j, 0))],
        out_specs=[pl.BlockSpec((1, ts, LANES), lambda i, j: (i, j, 0))] * 2,
        out_shape=[jax.ShapeDtypeStruct((b, s, LANES), F32)] * 2,
        compiler_params=_params("arbitrary", "arbitrary"),
        name="rope_tables",
    )(positions.reshape(b, s, 1))


def _inproj_kernel(x_ref, g_ref, w_ref, wfz_ref, bfz_ref, wfzt_ref, bfzt_ref, cos_ref, sin_ref,
                   dq_ref, dk_ref, dv_ref, fq_ref, fk_ref, fv_ref, c_ref, ct_ref,
                   carry_ref, carry_t_ref):
    ts = x_ref.shape[1]
    xb = _rms(x_ref[0], g_ref[...]).astype(BF16)
    proj = jnp.dot(xb, w_ref[...], preferred_element_type=F32)

    cosf, sinf = cos_ref[0], sin_ref[0]
    first_half = lax.broadcasted_iota(I32, (1, LANES), 1) % HEAD_DIM < ROT_DIM // 2

    def rope(t):
        outs = []
        for j in range(t.shape[1] // LANES):
            tj = t[:, j * LANES:(j + 1) * LANES]
            partner = jnp.where(first_half,
                                pltpu.roll(tj, LANES - ROT_DIM // 2, 1),
                                pltpu.roll(tj, ROT_DIM // 2, 1))
            outs.append(tj * cosf + partner * sinf)
        return jnp.concatenate(outs, axis=1)

    scale = HEAD_DIM ** -0.5
    w = DIFF_WIDTH
    dq_ref[0] = (rope(proj[:, 0:w]) * scale).astype(BF16)
    dk_ref[0] = rope(proj[:, w:2 * w]).astype(BF16)
    dv_ref[0] = proj[:, 2 * w:3 * w].astype(BF16)
    fq_ref[0] = (proj[:, 3 * w:4 * w] * scale).astype(BF16)
    fk_ref[0] = proj[:, 4 * w:5 * w].astype(BF16)
    fv_ref[0] = proj[:, 5 * w:6 * w].astype(BF16)

    def log_sigmoid(z):
        return jnp.minimum(z, 0.0) - jnp.log1p(jnp.exp(-jnp.abs(z)))

    def pieces(v):
        hi = v.astype(BF16)
        r1 = v - hi.astype(F32)
        mid = r1.astype(BF16)
        return hi, mid, (r1 - mid.astype(F32)).astype(BF16)

    row = lax.broadcasted_iota(I32, (ts, ts), 0)
    col = lax.broadcasted_iota(I32, (ts, ts), 1)
    upto_row = jnp.where(col <= row, 1.0, 0.0).astype(BF16)
    upto_col = jnp.where(row <= col, 1.0, 0.0).astype(BF16)

    @pl.when(pl.program_id(1) == 0)
    def _():
        carry_ref[...] = jnp.zeros_like(carry_ref)
        carry_t_ref[...] = jnp.zeros_like(carry_t_ref)

    fz = jnp.dot(xb, wfz_ref[...], preferred_element_type=F32) + bfz_ref[...]
    c = carry_ref[...] + sum(jnp.dot(upto_row, part, preferred_element_type=F32)
                             for part in pieces(log_sigmoid(fz)))
    c_ref[0] = c
    carry_ref[...] = c[ts - 1:ts, :]

    fzt = lax.dot_general(wfzt_ref[...], xb, (((1,), (1,)), ((), ())),
                          preferred_element_type=F32) + bfzt_ref[...]
    ct = carry_t_ref[...] + sum(jnp.dot(part, upto_col, preferred_element_type=F32)
                                for part in pieces(log_sigmoid(fzt)))
    ct_ref[0] = ct[0:N_FOX_HEADS, :]
    carry_t_ref[...] = ct[:, ts - 1:ts]


def _inproj(x, g, w_main, w_fz, b_fz, w_fz_t, b_fz_t, cosf, sinf):
    b, s, d = x.shape
    ts = SEQ_TILE
    tile = lambda width: pl.BlockSpec((1, ts, width), lambda i, j: (i, j, 0))
    full = lambda arr: pl.BlockSpec(arr.shape, lambda i, j: (0,) * arr.ndim)
    heads = jax.ShapeDtypeStruct((b, s, DIFF_WIDTH), BF16)
    return pl.pallas_call(
        _inproj_kernel,
        grid=(b, s // ts),
        in_specs=[tile(d), full(g), full(w_main), full(w_fz), full(b_fz), full(w_fz_t), full(b_fz_t),
                  tile(LANES), tile(LANES)],
        out_specs=[tile(DIFF_WIDTH)] * 6 + [tile(FOX_WIDTH),
                                            pl.BlockSpec((1, N_FOX_HEADS, ts), lambda i, j: (i, 0, j))],
        out_shape=[heads] * 6 + [jax.ShapeDtypeStruct((b, s, FOX_WIDTH), F32),
                                 jax.ShapeDtypeStruct((b, N_FOX_HEADS, s), F32)],
        scratch_shapes=[pltpu.VMEM((1, FOX_WIDTH), F32), pltpu.VMEM((w_fz_t.shape[0], 1), F32)],
        compiler_params=_params("arbitrary", "arbitrary"),
        name="inproj",
    )(x, g, w_main, w_fz, b_fz, w_fz_t, b_fz_t, cosf, sinf)


ONES_ROWS = 16


def _block_diag_queries(qt):
    row = lax.broadcasted_iota(I32, qt.shape, 0)
    zero = jnp.zeros_like(qt)
    return jnp.concatenate([jnp.where(row < HEAD_DIM, qt, zero),
                            jnp.where(row >= HEAD_DIM, qt, zero)], axis=1)


class _Chain:
    def __init__(self, qbd, k_ref, vt_ref, acc_ref, s_ref, p_ref, stat_ref,
                 key_bias=None, query_bias=None):
        self.qbd, self.k_ref, self.vt_ref = qbd, k_ref, vt_ref
        self.acc_ref, self.s_ref, self.p_ref, self.stat_ref = acc_ref, s_ref, p_ref, stat_ref
        self.key_bias, self.query_bias = key_bias, query_bias


_RUNNING_MAX, _TILE_MAX, _RESCALE = 0, 1, 2


def _causal_attention(chains, qi):
    tk = chains[0].vt_ref.shape[3]
    tq = chains[0].qbd.shape[1] // 2
    ones = jnp.ones((ONES_ROWS, tk), BF16)

    def scores(ch, i, diagonal):
        ks = ch.k_ref[0, pl.ds(pl.multiple_of(i * tk, tk), tk), :]
        s = jnp.dot(ks, ch.qbd, preferred_element_type=F32)
        if ch.key_bias is not None:
            s = s - ch.key_bias(i)
        if diagonal is not False:
            r = lax.broadcasted_iota(I32, s.shape, 0)
            c = lax.broadcasted_iota(I32, s.shape, 1) & (tq - 1)
            visible = r <= c if diagonal is True else jnp.logical_or(r <= c, jnp.logical_not(diagonal))
            s = jnp.where(visible, s, -jnp.inf)
        ch.s_ref[...] = s
        top = jnp.max(s, axis=0, keepdims=True)
        if ch.query_bias is not None:
            top = top + ch.query_bias
        ch.stat_ref[_TILE_MAX:_TILE_MAX + 1, :] = top

    def probs(ch):
        m = ch.stat_ref[_RUNNING_MAX:_RUNNING_MAX + 1, :]
        m_new = jnp.maximum(m, ch.stat_ref[_TILE_MAX:_TILE_MAX + 1, :])
        ch.stat_ref[_RUNNING_MAX:_RUNNING_MAX + 1, :] = m_new
        ch.stat_ref[_RESCALE:_RESCALE + 1, :] = jnp.exp(m - m_new)
        ref = m_new if ch.query_bias is None else m_new - ch.query_bias
        ch.p_ref[...] = jnp.exp(ch.s_ref[...] - ref).astype(BF16)

    def accumulate(ch, i):
        vts = jnp.concatenate([ch.vt_ref[0, i], ones], axis=0)
        ch.acc_ref[...] = (ch.stat_ref[_RESCALE:_RESCALE + 1, :] * ch.acc_ref[...]
                           + jnp.dot(vts, ch.p_ref[...], preferred_element_type=F32))

    def body(k, diagonal_next):
        for ch in chains:
            accumulate(ch, jnp.maximum(k - 1, 0))
        for ch in chains:
            probs(ch)
        for ch in chains:
            scores(ch, k + 1, diagonal_next)

    for ch in chains:
        ch.acc_ref[...] = jnp.zeros_like(ch.acc_ref)
        ch.p_ref[...] = jnp.zeros_like(ch.p_ref)
        ch.stat_ref[...] = jnp.concatenate(
            [jnp.full((1, 2 * tq), -jnp.inf, F32), jnp.ones((ch.stat_ref.shape[0] - 1, 2 * tq), F32)], axis=0)
        scores(ch, 0, qi == 0)

    def loop_body(k, carry):
        body(k, False)
        return carry

    lax.fori_loop(0, qi - 1, loop_body, 0)

    @pl.when(qi >= 1)
    def _():
        body(qi - 1, True)

    for ch in chains:
        accumulate(ch, jnp.maximum(qi - 1, 0))
    for ch in chains:
        probs(ch)
    for ch in chains:
        accumulate(ch, qi)


def _attention_kernel(lq1_ref, lk1_ref, lq2_ref, lk2_ref, g_ref,
                      dqt_ref, dk_ref, dvt_ref, fqt_ref, fk_ref, fvt_ref, cq_ref, ck_ref,
                      do_ref, fo_ref, dacc_ref, ds_ref, dp_ref, dstat_ref,
                      facc_ref, fs_ref, fp_ref, fstat_ref, *, lambda_init):
    tq = dqt_ref.shape[2]
    tk = dvt_ref.shape[3]
    cq = jnp.concatenate([cq_ref[0, 0, 0:1, :], cq_ref[0, 0, 1:2, :]], axis=1)
    lane = lax.broadcasted_iota(I32, (1, LANES), 1)

    def key_bias(i):
        ck = ck_ref[0, pl.ds(pl.multiple_of(i * tk, tk), tk), :]
        other = pltpu.roll(ck, HEAD_DIM, 1)
        first = jnp.where(lane < HEAD_DIM, ck, other)
        second = jnp.where(lane >= HEAD_DIM, ck, other)
        return jnp.concatenate([first] * (tq // LANES) + [second] * (tq // LANES), axis=1)

    _causal_attention(
        [_Chain(_block_diag_queries(dqt_ref[0]), dk_ref, dvt_ref, dacc_ref, ds_ref, dp_ref, dstat_ref),
         _Chain(_block_diag_queries(fqt_ref[0]), fk_ref, fvt_ref, facc_ref, fs_ref, fp_ref, fstat_ref,
                key_bias, cq)],
        pl.program_id(2))

    lam = (jnp.exp(jnp.sum(lq1_ref[...] * lk1_ref[...], axis=-1, keepdims=True))
           - jnp.exp(jnp.sum(lq2_ref[...] * lk2_ref[...], axis=-1, keepdims=True)) + lambda_init)
    dv = DIFF_V_DIM
    o = (dacc_ref[0:dv, 0:tq] / dacc_ref[dv:dv + 1, 0:tq]
         - lam * (dacc_ref[0:dv, tq:] / dacc_ref[dv:dv + 1, tq:]))
    ms = jnp.mean(o * o, axis=0, keepdims=True)
    do_ref[0] = (o * lax.rsqrt(ms + NORM_EPS) * g_ref[...] * (1.0 - lambda_init)).astype(BF16)

    h = HEAD_DIM
    fo_ref[0] = jnp.concatenate([facc_ref[0:h, 0:tq] / facc_ref[2 * h:2 * h + 1, 0:tq],
                                 facc_ref[h:2 * h, tq:] / facc_ref[2 * h:2 * h + 1, tq:]],
                                axis=0).astype(BF16)


def _attention(dqt, dk, dvt, fqt, fk, fvt, c_rows, c_cols, lq1, lk1, lq2, lk2, g_col, lambda_init):
    b, s, _ = dk.shape
    t = ATTN_TILE
    assert N_DIFF_HEADS == N_FOX_HEADS // 2
    vec = lambda arr: pl.BlockSpec(arr.shape, lambda i, h, j: (0, 0))
    q_spec = pl.BlockSpec((1, LANES, t), lambda i, h, j: (i, h, j))
    k_spec = pl.BlockSpec((1, s, LANES), lambda i, h, j: (i, 0, h))
    v_spec = pl.BlockSpec((1, s // t, LANES, t), lambda i, h, j: (i, 0, h, 0))
    out = jax.ShapeDtypeStruct((b, DIFF_WIDTH, s), BF16)
    chain_scratch = [pltpu.VMEM((LANES + ONES_ROWS, 2 * t), F32),
                     pltpu.VMEM((t, 2 * t), F32),
                     pltpu.VMEM((t, 2 * t), BF16),
                     pltpu.VMEM((8, 2 * t), F32)]
    return pl.pallas_call(
        functools.partial(_attention_kernel, lambda_init=lambda_init),
        grid=(b, N_DIFF_HEADS, s // t),
        in_specs=[vec(lq1), vec(lk1), vec(lq2), vec(lk2), vec(g_col),
                  q_spec, k_spec, v_spec, q_spec, k_spec, v_spec,
                  pl.BlockSpec((1, 1, 2, t), lambda i, h, j: (i, h, 0, j)), k_spec],
        out_specs=[q_spec, q_spec],
        out_shape=[out, out],
        scratch_shapes=chain_scratch * 2,
        compiler_params=_params("arbitrary", "arbitrary", "arbitrary"),
        name="attention",
    )(lq1, lk1, lq2, lk2, g_col, dqt, dk, dvt, fqt, fk, fvt, c_rows, c_cols)


def _router_kernel(d_ref, f_ref, x_ref, wo_ref, g_ref, wrt_ref, brt_ref,
                   h_ref, xn_ref, gate_ref, idx_ref, rank_ref, count_ref, carry_ref):
    tm = x_ref.shape[0]
    mixed = (jnp.dot(d_ref[...], wo_ref[0:DIFF_WIDTH, :], preferred_element_type=F32)
             + jnp.dot(f_ref[...], wo_ref[DIFF_WIDTH:, :], preferred_element_type=F32))
    h = x_ref[...] + mixed
    h_ref[...] = h
    xn = _rms(h, g_ref[...])
    xn_ref[...] = xn

    def pieces(v, count):
        out = []
        for _ in range(count):
            part = v.astype(BF16)
            out.append(part)
            v = v - part.astype(F32)
        return out

    nt = (((1,), (1,)), ((), ()))
    xh, xl = pieces(xn, 2)
    wh, wl = pieces(wrt_ref[...], 2)
    work = (lax.dot_general(wh, xh, nt, preferred_element_type=F32)
            + lax.dot_general(wh, xl, nt, preferred_element_type=F32)
            + lax.dot_general(wl, xh, nt, preferred_element_type=F32)) + brt_ref[...]
    expert = lax.broadcasted_iota(I32, work.shape, 0)

    vals, ids, picks = [], [], []
    for _ in range(TOP_K):
        m = jnp.max(work, axis=0, keepdims=True)
        idx = jnp.min(jnp.where(work == m, expert, N_EXPERTS), axis=0, keepdims=True)
        pick = expert == idx
        vals.append(m)
        ids.append(idx)
        picks.append(pick)
        work = jnp.where(pick, -jnp.inf, work)
    exps = [jnp.exp(v - vals[0]) for v in vals]
    denom = exps[0] + exps[1] + exps[2] + exps[3]

    chosen = jnp.where(picks[0] | picks[1] | picks[2] | picks[3], 1.0, 0.0)
    row = lax.broadcasted_iota(I32, (tm, tm), 0)
    col = lax.broadcasted_iota(I32, (tm, tm), 1)
    earlier = jnp.where(row < col, 1.0, 0.0).astype(BF16)

    @pl.when(pl.program_id(0) == 0)
    def _():
        carry_ref[...] = jnp.zeros_like(carry_ref)

    rank = jnp.dot(chosen.astype(BF16), earlier, preferred_element_type=F32) + carry_ref[...]
    carry_ref[...] = carry_ref[...] + jnp.sum(chosen, axis=1, keepdims=True)
    count_ref[...] = jnp.broadcast_to(carry_ref[...], count_ref.shape)

    slot = lax.broadcasted_iota(I32, (8, tm), 0)
    gate_t = jnp.zeros((8, tm), F32)
    idx_t = jnp.zeros((8, tm), I32)
    rank_t = jnp.zeros((8, tm), F32)
    for k in range(TOP_K):
        gate_t = jnp.where(slot == k, exps[k] / denom, gate_t)
        idx_t = jnp.where(slot == k, ids[k], idx_t)
        rank_t = jnp.where(slot == k, jnp.sum(jnp.where(picks[k], rank, 0.0), axis=0, keepdims=True), rank_t)
    idx_ref[...] = idx_t
    rank_ref[...] = rank_t.astype(I32)
    gate_rows = jnp.concatenate([gate_t, jnp.zeros((LANES - 8, tm), F32)], axis=0)
    eye = jnp.where(row == col, 1.0, 0.0).astype(BF16)
    gate_ref[...] = sum(lax.dot_general(eye, part, nt, preferred_element_type=F32)
                        for part in pieces(gate_rows, 3))


def _router(d_out, f_out, x, w_out, g_moe, w_router_t, b_router_t):
    n, d = x.shape
    tm = TOKEN_TILE
    tile = lambda width: pl.BlockSpec((tm, width), lambda i: (i, 0))
    rows = pl.BlockSpec((8, tm), lambda i: (0, i))
    full = lambda arr: pl.BlockSpec(arr.shape, lambda i: (0,) * arr.ndim)
    return pl.pallas_call(
        _router_kernel,
        grid=(n // tm,),
        in_specs=[tile(DIFF_WIDTH), tile(FOX_WIDTH), tile(d), full(w_out), full(g_moe),
                  full(w_router_t), full(b_router_t)],
        out_specs=[tile(d), tile(d), tile(LANES), rows, rows,
                   pl.BlockSpec((N_EXPERTS, LANES), lambda i: (0, 0))],
        out_shape=[jax.ShapeDtypeStruct((n, d), F32), jax.ShapeDtypeStruct((n, d), F32),
                   jax.ShapeDtypeStruct((n, LANES), F32), jax.ShapeDtypeStruct((8, n), I32),
                   jax.ShapeDtypeStruct((8, n), I32), jax.ShapeDtypeStruct((N_EXPERTS, LANES), F32)],
        scratch_shapes=[pltpu.VMEM((N_EXPERTS, 1), F32)],
        compiler_params=_params("arbitrary"),
        name="router",
    )(d_out, f_out, x, w_out, g_moe, w_router_t, b_router_t)


def _dispatch_kernel(ends_ref, pos_ref, x_ref, xs_ref, zero_ref, sem):
    tm = x_ref.shape[0]
    te = zero_ref.shape[0]

    @pl.when(pl.program_id(0) == 0)
    def _():
        zero_ref[...] = jnp.zeros_like(zero_ref)

        def zero_tile(first_row):
            copy = pltpu.make_async_copy(zero_ref, xs_ref.at[pl.ds(pl.multiple_of(first_row, te), te), :], sem)
            copy.start()
            copy.wait()

        for e in range(N_EXPERTS):
            end = ends_ref[e]
            start = ends_ref[e - 1] if e else 0
            pl.when(end > start)(functools.partial(zero_tile, end - te))

        def unused(t, carry):
            zero_tile(t * te)
            return carry

        lax.fori_loop(ends_ref[N_EXPERTS - 1] // te, xs_ref.shape[0] // te, unused, 0)

    def issue(g, carry):
        base = pl.multiple_of(g * ROW_GROUP, ROW_GROUP)
        for j in range(ROW_GROUP):
            for k in range(TOP_K):
                dst = pos_ref[k * tm + base + j]
                pltpu.make_async_copy(x_ref.at[pl.ds(base + j, 1), :], xs_ref.at[pl.ds(dst, 1), :],
                                      sem).start(priority=k % 2)
        return carry

    lax.fori_loop(0, tm // ROW_GROUP, issue, 0)
    for _ in range(TOP_K):
        pltpu.make_async_copy(x_ref, xs_ref.at[pl.ds(0, tm), :], sem).wait()


def _dispatch(ends, pos_flat, xn, n_rows):
    n, d = xn.shape
    tm = TOKEN_TILE
    return pl.pallas_call(
        _dispatch_kernel,
        grid_spec=pltpu.PrefetchScalarGridSpec(
            num_scalar_prefetch=1,
            grid=(n // tm,),
            in_specs=[pl.BlockSpec((TOP_K * tm,), lambda i, ends: (i,), memory_space=pltpu.SMEM),
                      pl.BlockSpec((tm, d), lambda i, ends: (i, 0))],
            out_specs=pl.BlockSpec(memory_space=pl.ANY),
            scratch_shapes=[pltpu.VMEM((EXPERT_TILE, d), F32), pltpu.SemaphoreType.DMA(())]),
        out_shape=jax.ShapeDtypeStruct((n_rows, d), F32),
        compiler_params=_params("arbitrary"),
        name="dispatch",
    )(ends, pos_flat, xn)


def _expert_kernel(tile_expert_ref, n_used_ref, next_expert_ref, slot_ref,
                   x_ref, w1_ref, b1_ref, w2_ref, b2_ref, y_ref,
                   w1f_ref, w2f_ref, sem_ref, w1b_ref, w2p_ref, w2b_ref):
    t = pl.program_id(0)
    fresh = (t == 0) | (tile_expert_ref[t] != tile_expert_ref[jnp.maximum(t - 1, 0)])
    used = t < n_used_ref[0]

    def weight_copies(e, s):
        return (pltpu.make_async_copy(w1_ref.at[e], w1f_ref.at[s], sem_ref.at[0, s]),
                pltpu.make_async_copy(w2_ref.at[e], w2f_ref.at[s], sem_ref.at[1, s]))

    @pl.when(t == 0)
    def _():
        for copy in weight_copies(tile_expert_ref[0], slot_ref[0]):
            copy.start()

    @pl.when(fresh & used)
    def _():
        s = slot_ref[t]
        for copy in weight_copies(tile_expert_ref[t], s):
            copy.wait()

        @pl.when(next_expert_ref[t] >= 0)
        def _():
            for copy in weight_copies(next_expert_ref[t], 1 - s):
                copy.start()

        rows = LANES
        half = LANES // 2
        for r in range(0, w1b_ref.shape[0], rows):
            w1b_ref[r:r + rows, :] = w1f_ref[s, r:r + rows, :].astype(BF16)
        for c in range(w2p_ref.shape[0]):
            lanes = slice(c * LANES, (c + 1) * LANES)
            for r in range(0, w2b_ref.shape[0], rows):
                for b in range(2):
                    w2p_ref[c, pl.ds(r + b, half, stride=2), :] = (
                        w2f_ref[s, r + b * half:r + (b + 1) * half, lanes])
            w2b_ref[:, lanes] = w2p_ref[c].astype(BF16)

    @pl.when(used)
    def _():
        xb = x_ref[...].astype(BF16)
        h = jnp.dot(xb, w1b_ref[...], preferred_element_type=F32) + b1_ref[0]
        even = lax.broadcasted_iota(I32, (1, LANES), 1) % 2 == 0
        parts = []
        for q in range(h.shape[1] // (2 * LANES)):
            a = h[:, (2 * q) * LANES:(2 * q + 1) * LANES]
            b = h[:, (2 * q + 1) * LANES:(2 * q + 2) * LANES]
            hg = jnp.where(even, a, pltpu.roll(b, 1, 1))
            hl = jnp.where(even, pltpu.roll(a, LANES - 1, 1), b)
            glu = jnp.minimum(hg, SWIGLU_LIMIT)
            lin = jnp.clip(hl, -SWIGLU_LIMIT, SWIGLU_LIMIT)
            parts.append((glu * jax.nn.sigmoid(SWIGLU_ALPHA * glu) * (lin + 1.0)).astype(BF16))
        act = jnp.concatenate(parts, axis=1)
        y_ref[...] = jnp.dot(act, w2b_ref[...], preferred_element_type=F32) + b2_ref[0]

    @pl.when(jnp.logical_not(used))
    def _():
        y_ref[...] = jnp.zeros_like(y_ref)


def _experts(tile_expert, n_used, next_expert, slot, xs, w1, b1, w2, b2):
    n_rows, d = xs.shape
    tm = EXPERT_TILE
    by_expert = lambda arr: pl.BlockSpec((1,) + arr.shape[1:], lambda i, te, *_: (te[i], 0, 0))
    rows = pl.BlockSpec((tm, d), lambda i, *_: (i, 0))
    used_rows = pl.BlockSpec((tm, d), lambda i, te, nu, *_: (jnp.minimum(i, nu[0] - 1), 0))
    hbm = pl.BlockSpec(memory_space=pl.ANY)
    return pl.pallas_call(
        _expert_kernel,
        grid_spec=pltpu.PrefetchScalarGridSpec(
            num_scalar_prefetch=4,
            grid=(n_rows // tm,),
            in_specs=[used_rows, hbm, by_expert(b1), hbm, by_expert(b2)],
            out_specs=rows,
            scratch_shapes=[pltpu.VMEM((2,) + w1.shape[1:], F32), pltpu.VMEM((2,) + w2.shape[1:], F32),
                            pltpu.SemaphoreType.DMA((2, 2)),
                            pltpu.VMEM(w1.shape[1:], BF16),
                            pltpu.VMEM((w2.shape[2] // LANES, w2.shape[1], LANES), F32),
                            pltpu.VMEM(w2.shape[1:], BF16)]),
        out_shape=jax.ShapeDtypeStruct(xs.shape, F32),
        compiler_params=pltpu.CompilerParams(dimension_semantics=("arbitrary",),
                                             vmem_limit_bytes=EXPERT_VMEM_LIMIT),
        name="experts",
    )(tile_expert, n_used, next_expert, slot, xs, w1, b1, w2, b2)


def _combine_kernel(pos_ref, h_ref, gate_ref, p_ref, wg_ref, wp_ref, gple_ref, gfin_ref, y_ref,
                    o_ref, ybuf, sem, *, final_norm):
    tm = h_ref.shape[0]

    def issue(g, carry):
        base = pl.multiple_of(g * ROW_GROUP, ROW_GROUP)
        for j in range(ROW_GROUP):
            for k in range(TOP_K):
                src = pos_ref[k * tm + base + j]
                pltpu.make_async_copy(y_ref.at[pl.ds(src, 1), :], ybuf.at[k, pl.ds(base + j, 1), :],
                                      sem).start(priority=k % 2)
        return carry

    lax.fori_loop(0, tm // ROW_GROUP, issue, 0)
    for k in range(TOP_K):
        pltpu.make_async_copy(y_ref.at[pl.ds(0, tm), :], ybuf.at[k], sem).wait()

    gates = gate_ref[...]
    moe = gates[:, 0:1] * ybuf[0]
    for k in range(1, TOP_K):
        moe = moe + gates[:, k:k + 1] * ybuf[k]
    h = h_ref[...] + moe
    gate = jax.nn.sigmoid(jnp.dot(_rms(h, gple_ref[...]).astype(BF16), wg_ref[...],
                                  preferred_element_type=F32))
    emb = jnp.dot(p_ref[...].astype(BF16), wp_ref[...], preferred_element_type=F32)
    h = h + gate * emb
    o_ref[...] = _rms(h, gfin_ref[...]) if final_norm else h


def _combine(pos_flat, h, gates, p, w_gate, w_proj, g_ple, g_final, y, final_norm):
    n, d = h.shape
    tm = TOKEN_TILE
    tile = lambda width: pl.BlockSpec((tm, width), lambda i: (i, 0))
    full = lambda arr: pl.BlockSpec(arr.shape, lambda i: (0,) * arr.ndim)
    return pl.pallas_call(
        functools.partial(_combine_kernel, final_norm=final_norm),
        grid=(n // tm,),
        in_specs=[pl.BlockSpec((TOP_K * tm,), lambda i: (i,), memory_space=pltpu.SMEM),
                  tile(d), tile(LANES), tile(p.shape[1]), full(w_gate), full(w_proj),
                  full(g_ple), full(g_final), pl.BlockSpec(memory_space=pl.ANY)],
        out_specs=tile(d),
        out_shape=jax.ShapeDtypeStruct((n, d), F32),
        scratch_shapes=[pltpu.VMEM((TOP_K, tm, d), F32), pltpu.SemaphoreType.DMA(())],
        compiler_params=_params("arbitrary"),
        name="combine",
    )(pos_flat, h, gates, p, w_gate, w_proj, g_ple, g_final, y)


def _layer(h, p, cosf, sinf, g_attn, w_in, lq1, lk1, lq2, lk2, g_subln, b_forget, w_out, g_moe,
           w_router, b_router, w_e1, b_e1, w_e2, b_e2, g_ple, w_ple_gate, w_ple_proj, g_final,
           lambda_init, final_norm):
    b, s, d = h.shape
    n = b * s
    row = lambda v: v.reshape(1, -1)

    w_main = w_in[:, :MAIN_WIDTH].astype(BF16)
    w_fz = jnp.repeat(w_in[:, MAIN_WIDTH:], HEAD_DIM, axis=1).astype(BF16)
    b_fz = row(jnp.repeat(b_forget, HEAD_DIM))
    pad_heads = ((0, 2 * N_FOX_HEADS - N_FOX_HEADS), (0, 0))
    w_fz_t = jnp.pad(w_in[:, MAIN_WIDTH:].T, pad_heads).astype(BF16)
    b_fz_t = jnp.pad(b_forget.reshape(-1, 1), pad_heads)
    b1 = b_e1[:, None, :]
    b2 = b_e2[:, None, :]

    dq, dk, dv, fq, fk, fv, c, c_t = _inproj(h, row(g_attn), w_main, w_fz, b_fz, w_fz_t, b_fz_t,
                                            cosf, sinf)
    ta = ATTN_TILE
    seq_last = lambda a: a.transpose(0, 2, 1)
    key_tiles = lambda a: a.reshape(b, s // ta, ta, a.shape[2]).transpose(0, 1, 3, 2)
    c_rows = c_t.reshape(b, N_FOX_HEADS // 2, 2, s)
    d_out_t, f_out_t = _attention(seq_last(dq), dk, key_tiles(dv), seq_last(fq), fk, key_tiles(fv),
                                  c_rows, c, row(lq1), row(lk1), row(lq2), row(lk2),
                                  g_subln.reshape(-1, 1), lambda_init)
    d_out, f_out = seq_last(d_out_t), seq_last(f_out_t)

    h1, xn, gates, idx, rank, counts = _router(
        d_out.reshape(n, DIFF_WIDTH), f_out.reshape(n, FOX_WIDTH), h.reshape(n, d),
        w_out.astype(BF16), row(g_moe), w_router.T, b_router.reshape(-1, 1))

    te = EXPERT_TILE
    n_rows = n * TOP_K + N_EXPERTS * te
    counts = counts[:, 0].astype(I32)
    padded = (counts + te - 1) // te * te
    ends = jnp.cumsum(padded)
    starts = ends - padded
    experts = jnp.arange(N_EXPERTS, dtype=I32)[:, None, None]
    pos = rank[:TOP_K] + jnp.sum(jnp.where(idx[None, :TOP_K] == experts, starts[:, None, None], 0),
                                 axis=0)
    tm = TOKEN_TILE
    pos_flat = pos.reshape(TOP_K, n // tm, tm).transpose(1, 0, 2).reshape(-1)
    tile_start = jnp.arange(n_rows // te, dtype=I32) * te
    tile_expert = jnp.minimum(jnp.sum((ends[None, :] <= tile_start[:, None]).astype(I32), axis=1),
                              N_EXPERTS - 1)
    n_used = (ends[-1:] // te).astype(I32)
    owns = padded > 0
    later = lax.cummin(jnp.where(owns, jnp.arange(N_EXPERTS, dtype=I32), N_EXPERTS), reverse=True)
    following = jnp.concatenate([later[1:], jnp.full((1,), N_EXPERTS, I32)])
    following = jnp.where(following < N_EXPERTS, following, -1)
    next_expert = following[tile_expert]
    slot = ((jnp.cumsum(owns.astype(I32)) - 1) % 2)[tile_expert]

    xs = _dispatch(ends.astype(I32), pos_flat, xn, n_rows)
    y = _experts(tile_expert, n_used, next_expert.astype(I32), slot.astype(I32), xs, w_e1, b1, w_e2, b2)
    out = _combine(pos_flat, h1, gates, p.reshape(n, PLE_DIM), w_ple_gate.astype(BF16),
                   w_ple_proj.astype(BF16), row(g_ple), row(g_final), y, final_norm)
    return out.reshape(b, s, d)


def kernel(x, p, positions, g_attn, w_in, lambda_q1, lambda_k1, lambda_q2, lambda_k2, g_subln, b_forget, w_out, g_moe, w_router, b_router, w_e1, b_e1, w_e2, b_e2, g_ple, w_ple_gate, w_ple_proj, g_final):
    depth = g_attn.shape[0]
    cosf, sinf = _rope_tables(positions)
    h = x
    for i in range(depth):
        lambda_init = 0.8 - 0.6 * math.exp(-0.3 * i)
        h = _layer(h, p[i], cosf, sinf, g_attn[i], w_in[i], lambda_q1[i], lambda_k1[i],
                   lambda_q2[i], lambda_k2[i], g_subln[i], b_forget[i], w_out[i], g_moe[i],
                   w_router[i], b_router[i], w_e1[i], b_e1[i], w_e2[i], b_e2[i], g_ple[i],
                   w_ple_gate[i], w_ple_proj[i], g_final, lambda_init, i == depth - 1)
    return h
```

```python
import functools
import math

import jax
import jax.numpy as jnp
from jax import lax
from jax.experimental import pallas as pl
from jax.experimental.pallas import tpu as pltpu

F32 = jnp.float32
BF16 = jnp.bfloat16
I32 = jnp.int32

LANES = 128
D_MODEL = 1024
HEAD_DIM = 64
N_DIFF_HEADS = 4
DIFF_V_DIM = 2 * HEAD_DIM
N_FOX_HEADS = 8
DIFF_WIDTH = N_DIFF_HEADS * DIFF_V_DIM
FOX_WIDTH = N_FOX_HEADS * HEAD_DIM
ROT_DIM = HEAD_DIM // 4
ROPE_THETA = 500000.0
N_EXPERTS = 32
TOP_K = 4
D_EXPERT = D_MODEL
SWIGLU_ALPHA = 1.702
SWIGLU_LIMIT = 7.0
PLE_DIM = 256
NORM_EPS = 1e-5
LOG2_E = math.log2(math.e)
MAIN_WIDTH = 3 * DIFF_WIDTH + 3 * FOX_WIDTH

SEQ_TILE = 256
ATTN_TILE = 256
TOKEN_TILE = 256
EXPERT_TILE = 256
ROW_GROUP = 8
VMEM_LIMIT = 48 * 1024 * 1024
EXPERT_VMEM_LIMIT = 58 * 1024 * 1024


def _params(*semantics):
    return pltpu.CompilerParams(dimension_semantics=semantics, vmem_limit_bytes=VMEM_LIMIT)


def _rms(x, g):
    return x * lax.rsqrt(jnp.mean(x * x, axis=-1, keepdims=True) + NORM_EPS) * g


def _inproj_kernel(pos_ref, x_ref, g_ref, w_ref, wfzt_ref, bfzt_ref,
                   dqt_ref, dk_ref, dvt_ref, fqt_ref, fk_ref, fvt_ref, c_ref, ct_ref, carry_t_ref):
    ts = x_ref.shape[1]
    xb = _rms(x_ref[0], g_ref[...]).astype(BF16)
    proj = jnp.dot(xb, w_ref[...], preferred_element_type=F32)
    nt = (((1,), (1,)), ((), ()))
    tn = (((0,), (0,)), ((), ()))

    def pieces(v):
        hi = v.astype(BF16)
        r1 = v - hi.astype(F32)
        mid = r1.astype(BF16)
        return hi, mid, (r1 - mid.astype(F32)).astype(BF16)

    def rows_to_lanes(v, select):
        return sum(lax.dot_general(part, select, tn, preferred_element_type=F32) for part in pieces(v))

    half_rot = ROT_DIM // 2
    freq = lax.broadcasted_iota(I32, (half_rot, 1), 0)
    inv_freq = jnp.power(ROPE_THETA, -(2 * freq).astype(F32) / ROT_DIM)
    ang = inv_freq * pos_ref[0].astype(F32)
    trig = jnp.concatenate([jnp.cos(ang), jnp.sin(ang)], axis=0)
    r = lax.broadcasted_iota(I32, (2 * half_rot, 2 * LANES), 0)
    lane2 = lax.broadcasted_iota(I32, (2 * half_rot, 2 * LANES), 1)
    d = lane2 % HEAD_DIM
    same_freq = jnp.where(d % half_rot == r % half_rot, 1.0, 0.0)
    cos_part = jnp.where((r < half_rot) & (lane2 < LANES) & (d < ROT_DIM), same_freq, 0.0)
    sin_part = jnp.where((r >= half_rot) & (lane2 >= LANES) & (d < ROT_DIM),
                         jnp.where(d < half_rot, -same_freq, same_freq), 0.0)
    tables = rows_to_lanes(trig, (cos_part + sin_part).astype(BF16))
    lane = lax.broadcasted_iota(I32, (1, LANES), 1) % HEAD_DIM
    cosf = jnp.where(lane < ROT_DIM, tables[:, :LANES], 1.0)
    sinf = tables[:, LANES:]
    first_half = lane < half_rot

    def rope(t):
        outs = []
        for j in range(t.shape[1] // LANES):
            tj = t[:, j * LANES:(j + 1) * LANES]
            partner = jnp.where(first_half,
                                pltpu.roll(tj, LANES - ROT_DIM // 2, 1),
                                pltpu.roll(tj, ROT_DIM // 2, 1))
            outs.append(tj * cosf + partner * sinf)
        return jnp.concatenate(outs, axis=1)

    scale = HEAD_DIM ** -0.5 * LOG2_E
    w = DIFF_WIDTH
    dqt_ref[0] = (rope(proj[:, 0:w]) * scale).astype(BF16).T
    dk_ref[0] = rope(proj[:, w:2 * w]).astype(BF16)
    dvt_ref[0, 0] = proj[:, 2 * w:3 * w].astype(BF16).T
    fqt_ref[0] = (proj[:, 3 * w:4 * w] * scale).astype(BF16).T
    fk_ref[0] = proj[:, 4 * w:5 * w].astype(BF16)
    fvt_ref[0, 0] = proj[:, 5 * w:6 * w].astype(BF16).T

    def log_sigmoid(z):
        return jnp.minimum(z, 0.0) - jnp.log1p(jnp.exp(-jnp.abs(z)))

    row = lax.broadcasted_iota(I32, (ts, ts), 0)
    col = lax.broadcasted_iota(I32, (ts, ts), 1)
    upto_col = jnp.where(row <= col, 1.0, 0.0).astype(BF16)

    @pl.when(pl.program_id(1) == 0)
    def _():
        carry_t_ref[...] = jnp.zeros_like(carry_t_ref)

    fzt = lax.dot_general(wfzt_ref[...], xb, nt, preferred_element_type=F32) + bfzt_ref[...]
    ct = carry_t_ref[...] + sum(jnp.dot(part, upto_col, preferred_element_type=F32)
                                for part in pieces(log_sigmoid(fzt)))
    carry_t_ref[...] = ct[:, ts - 1:ts]
    ct = ct * LOG2_E
    ct_ref[0] = ct[0:N_FOX_HEADS, :]
    head = lax.broadcasted_iota(I32, (ct.shape[0], FOX_WIDTH), 0)
    group = lax.broadcasted_iota(I32, (ct.shape[0], FOX_WIDTH), 1) // HEAD_DIM
    c_ref[0] = rows_to_lanes(ct, jnp.where(head == group, 1.0, 0.0).astype(BF16))


def _inproj(positions, x, g, w_main, w_fz_t, b_fz_t):
    b, s, d = x.shape
    ts = SEQ_TILE
    assert ts == ATTN_TILE
    tile = lambda width: pl.BlockSpec((1, ts, width), lambda i, j: (i, j, 0))
    tile_t = pl.BlockSpec((1, DIFF_WIDTH, ts), lambda i, j: (i, 0, j))
    key_tile_t = pl.BlockSpec((1, 1, DIFF_WIDTH, ts), lambda i, j: (i, j, 0, 0))
    full = lambda arr: pl.BlockSpec(arr.shape, lambda i, j: (0,) * arr.ndim)
    rows = jax.ShapeDtypeStruct((b, s, DIFF_WIDTH), BF16)
    rows_t = jax.ShapeDtypeStruct((b, DIFF_WIDTH, s), BF16)
    key_tiles_t = jax.ShapeDtypeStruct((b, s // ts, DIFF_WIDTH, ts), BF16)
    return pl.pallas_call(
        _inproj_kernel,
        grid=(b, s // ts),
        in_specs=[pl.BlockSpec((1, 1, ts), lambda i, j: (i, 0, j)), tile(d), full(g), full(w_main),
                  full(w_fz_t), full(b_fz_t)],
        out_specs=[tile_t, tile(DIFF_WIDTH), key_tile_t, tile_t, tile(FOX_WIDTH), key_tile_t,
                   tile(FOX_WIDTH), pl.BlockSpec((1, N_FOX_HEADS, ts), lambda i, j: (i, 0, j))],
        out_shape=[rows_t, rows, key_tiles_t, rows_t, rows, key_tiles_t,
                   jax.ShapeDtypeStruct((b, s, FOX_WIDTH), F32),
                   jax.ShapeDtypeStruct((b, N_FOX_HEADS, s), F32)],
        scratch_shapes=[pltpu.VMEM((w_fz_t.shape[0], 1), F32)],
        compiler_params=_params("arbitrary", "arbitrary"),
        name="inproj",
    )(positions.reshape(b, 1, s), x, g, w_main, w_fz_t, b_fz_t)


ONES_ROWS = 16


def _block_diag_queries(qt):
    row = lax.broadcasted_iota(I32, qt.shape, 0)
    zero = jnp.zeros_like(qt)
    return jnp.concatenate([jnp.where(row < HEAD_DIM, qt, zero),
                            jnp.where(row >= HEAD_DIM, qt, zero)], axis=1)


class _Chain:
    def __init__(self, qbd, k_ref, vt_ref, acc_ref, s_ref, p_ref, stat_ref,
                 key_bias=None, query_bias=None):
        self.qbd, self.k_ref, self.vt_ref = qbd, k_ref, vt_ref
        self.acc_ref, self.s_ref, self.p_ref, self.stat_ref = acc_ref, s_ref, p_ref, stat_ref
        self.key_bias, self.query_bias = key_bias, query_bias


_RUNNING_MAX, _TILE_MAX, _RESCALE = 0, 1, 2


def _causal_attention(chains, qi):
    tk = chains[0].vt_ref.shape[3]
    tq = chains[0].qbd.shape[1] // 2
    ones = jnp.ones((ONES_ROWS, tk), BF16)

    def scores(ch, i, diagonal):
        ks = ch.k_ref[0, pl.ds(pl.multiple_of(i * tk, tk), tk), :]
        s = jnp.dot(ks, ch.qbd, preferred_element_type=F32)
        if ch.key_bias is not None:
            s = s - ch.key_bias(i)
        if diagonal is not False:
            r = lax.broadcasted_iota(I32, s.shape, 0)
            c = lax.broadcasted_iota(I32, s.shape, 1) & (tq - 1)
            visible = r <= c if diagonal is True else jnp.logical_or(r <= c, jnp.logical_not(diagonal))
            s = jnp.where(visible, s, -jnp.inf)
        ch.s_ref[...] = s
        top = jnp.max(s, axis=0, keepdims=True)
        if ch.query_bias is not None:
            top = top + ch.query_bias
        ch.stat_ref[_TILE_MAX:_TILE_MAX + 1, :] = top

    def probs(ch):
        m = ch.stat_ref[_RUNNING_MAX:_RUNNING_MAX + 1, :]
        m_new = jnp.maximum(m, ch.stat_ref[_TILE_MAX:_TILE_MAX + 1, :])
        ch.stat_ref[_RUNNING_MAX:_RUNNING_MAX + 1, :] = m_new
        ch.stat_ref[_RESCALE:_RESCALE + 1, :] = jnp.exp2(m - m_new)
        ref = m_new if ch.query_bias is None else m_new - ch.query_bias
        ch.p_ref[...] = jnp.exp2(ch.s_ref[...] - ref).astype(BF16)

    def accumulate(ch, i):
        vts = jnp.concatenate([ch.vt_ref[0, i], ones], axis=0)
        ch.acc_ref[...] = (ch.stat_ref[_RESCALE:_RESCALE + 1, :] * ch.acc_ref[...]
                           + jnp.dot(vts, ch.p_ref[...], preferred_element_type=F32))

    def body(k, diagonal_next):
        for ch in chains:
            accumulate(ch, jnp.maximum(k - 1, 0))
        for ch in chains:
            probs(ch)
        for ch in chains:
            scores(ch, k + 1, diagonal_next)

    for ch in chains:
        ch.acc_ref[...] = jnp.zeros_like(ch.acc_ref)
        ch.p_ref[...] = jnp.zeros_like(ch.p_ref)
        ch.stat_ref[...] = jnp.concatenate(
            [jnp.full((1, 2 * tq), -jnp.inf, F32), jnp.ones((ch.stat_ref.shape[0] - 1, 2 * tq), F32)], axis=0)
        scores(ch, 0, qi == 0)

    def loop_body(k, carry):
        body(k, False)
        return carry

    lax.fori_loop(0, qi - 1, loop_body, 0)

    @pl.when(qi >= 1)
    def _():
        body(qi - 1, True)

    for ch in chains:
        accumulate(ch, jnp.maximum(qi - 1, 0))
    for ch in chains:
        probs(ch)
    for ch in chains:
        accumulate(ch, qi)


def _attention_kernel(lq1_ref, lk1_ref, lq2_ref, lk2_ref, g_ref,
                      dqt_ref, dk_ref, dvt_ref, fqt_ref, fk_ref, fvt_ref, cq_ref, ck_ref,
                      do_ref, fo_ref, dacc_ref, ds_ref, dp_ref, dstat_ref,
                      facc_ref, fs_ref, fp_ref, fstat_ref, *, lambda_init):
    tq = dqt_ref.shape[2]
    tk = dvt_ref.shape[3]
    cq = jnp.concatenate([cq_ref[0, 0, 0:1, :], cq_ref[0, 0, 1:2, :]], axis=1)
    lane = lax.broadcasted_iota(I32, (1, LANES), 1)

    def key_bias(i):
        ck = ck_ref[0, pl.ds(pl.multiple_of(i * tk, tk), tk), :]
        other = pltpu.roll(ck, HEAD_DIM, 1)
        first = jnp.where(lane < HEAD_DIM, ck, other)
        second = jnp.where(lane >= HEAD_DIM, ck, other)
        return jnp.concatenate([first] * (tq // LANES) + [second] * (tq // LANES), axis=1)

    _causal_attention(
        [_Chain(_block_diag_queries(dqt_ref[0]), dk_ref, dvt_ref, dacc_ref, ds_ref, dp_ref, dstat_ref),
         _Chain(_block_diag_queries(fqt_ref[0]), fk_ref, fvt_ref, facc_ref, fs_ref, fp_ref, fstat_ref,
                key_bias, cq)],
        pl.program_id(2))

    lam = (jnp.exp(jnp.sum(lq1_ref[...] * lk1_ref[...], axis=-1, keepdims=True))
           - jnp.exp(jnp.sum(lq2_ref[...] * lk2_ref[...], axis=-1, keepdims=True)) + lambda_init)
    dv = DIFF_V_DIM
    o = (dacc_ref[0:dv, 0:tq] / dacc_ref[dv:dv + 1, 0:tq]
         - lam * (dacc_ref[0:dv, tq:] / dacc_ref[dv:dv + 1, tq:]))
    ms = jnp.mean(o * o, axis=0, keepdims=True)
    do_ref[0] = (o * lax.rsqrt(ms + NORM_EPS) * g_ref[...] * (1.0 - lambda_init)).T.astype(BF16)

    h = HEAD_DIM
    fo_ref[0] = jnp.concatenate([facc_ref[0:h, 0:tq] / facc_ref[2 * h:2 * h + 1, 0:tq],
                                 facc_ref[h:2 * h, tq:] / facc_ref[2 * h:2 * h + 1, tq:]],
                                axis=0).T.astype(BF16)


def _attention(dqt, dk, dvt, fqt, fk, fvt, c_rows, c_cols, lq1, lk1, lq2, lk2, g_col, lambda_init):
    b, s, _ = dk.shape
    t = ATTN_TILE
    assert N_DIFF_HEADS == N_FOX_HEADS // 2
    vec = lambda arr: pl.BlockSpec(arr.shape, lambda i, h, j: (0, 0))
    q_spec = pl.BlockSpec((1, LANES, t), lambda i, h, j: (i, h, j))
    k_spec = pl.BlockSpec((1, s, LANES), lambda i, h, j: (i, 0, h))
    v_spec = pl.BlockSpec((1, s // t, LANES, t), lambda i, h, j: (i, 0, h, 0))
    o_spec = pl.BlockSpec((1, t, LANES), lambda i, h, j: (i, j, h))
    out = jax.ShapeDtypeStruct((b, s, DIFF_WIDTH), BF16)
    chain_scratch = [pltpu.VMEM((LANES + ONES_ROWS, 2 * t), F32),
                     pltpu.VMEM((t, 2 * t), F32),
                     pltpu.VMEM((t, 2 * t), BF16),
                     pltpu.VMEM((8, 2 * t), F32)]
    return pl.pallas_call(
        functools.partial(_attention_kernel, lambda_init=lambda_init),
        grid=(b, N_DIFF_HEADS, s // t),
        in_specs=[vec(lq1), vec(lk1), vec(lq2), vec(lk2), vec(g_col),
                  q_spec, k_spec, v_spec, q_spec, k_spec, v_spec,
                  pl.BlockSpec((1, 1, 2, t), lambda i, h, j: (i, h, 0, j)), k_spec],
        out_specs=[o_spec, o_spec],
        out_shape=[out, out],
        scratch_shapes=chain_scratch * 2,
        compiler_params=_params("arbitrary", "arbitrary", "arbitrary"),
        name="attention",
    )(lq1, lk1, lq2, lk2, g_col, dqt, dk, dvt, fqt, fk, fvt, c_rows, c_cols)


def _router_kernel(d_ref, f_ref, x_ref, wo_ref, g_ref, wrt_ref, brt_ref,
                   h_ref, xn_ref, gate_ref, idx_ref, rank_ref, count_ref, carry_ref):
    tm = x_ref.shape[0]
    mixed = (jnp.dot(d_ref[...], wo_ref[0:DIFF_WIDTH, :], preferred_element_type=F32)
             + jnp.dot(f_ref[...], wo_ref[DIFF_WIDTH:, :], preferred_element_type=F32))
    h = x_ref[...] + mixed
    h_ref[...] = h
    xn = _rms(h, g_ref[...])
    xn_ref[...] = xn

    def pieces(v, count):
        out = []
        for _ in range(count):
            part = v.astype(BF16)
            out.append(part)
            v = v - part.astype(F32)
        return out

    nt = (((1,), (1,)), ((), ()))
    xh, xl = pieces(xn, 2)
    wh, wl = pieces(wrt_ref[...], 2)
    work = (lax.dot_general(wh, xh, nt, preferred_element_type=F32)
            + lax.dot_general(wh, xl, nt, preferred_element_type=F32)
            + lax.dot_general(wl, xh, nt, preferred_element_type=F32)) + brt_ref[...]
    expert = lax.broadcasted_iota(I32, work.shape, 0)

    vals, ids, picks = [], [], []
    for _ in range(TOP_K):
        m = jnp.max(work, axis=0, keepdims=True)
        idx = jnp.min(jnp.where(work == m, expert, N_EXPERTS), axis=0, keepdims=True)
        pick = expert == idx
        vals.append(m)
        ids.append(idx)
        picks.append(pick)
        work = jnp.where(pick, -jnp.inf, work)
    exps = [jnp.exp(v - vals[0]) for v in vals]
    denom = exps[0] + exps[1] + exps[2] + exps[3]

    chosen = jnp.where(picks[0] | picks[1] | picks[2] | picks[3], 1.0, 0.0)
    row = lax.broadcasted_iota(I32, (tm, tm), 0)
    col = lax.broadcasted_iota(I32, (tm, tm), 1)
    earlier = jnp.where(row < col, 1.0, 0.0).astype(BF16)

    @pl.when(pl.program_id(0) == 0)
    def _():
        carry_ref[...] = jnp.zeros_like(carry_ref)

    rank = jnp.dot(chosen.astype(BF16), earlier, preferred_element_type=F32) + carry_ref[...]
    carry_ref[...] = carry_ref[...] + jnp.sum(chosen, axis=1, keepdims=True)
    count_ref[...] = jnp.broadcast_to(carry_ref[...], count_ref.shape)

    slot = lax.broadcasted_iota(I32, (8, tm), 0)
    gate_t = jnp.zeros((8, tm), F32)
    idx_t = jnp.zeros((8, tm), I32)
    rank_t = jnp.zeros((8, tm), F32)
    for k in range(TOP_K):
        gate_t = jnp.where(slot == k, exps[k] / denom, gate_t)
        idx_t = jnp.where(slot == k, ids[k], idx_t)
        rank_t = jnp.where(slot == k, jnp.sum(jnp.where(picks[k], rank, 0.0), axis=0, keepdims=True), rank_t)
    idx_ref[...] = idx_t
    rank_ref[...] = rank_t.astype(I32)
    gate_rows = jnp.concatenate([gate_t, jnp.zeros((LANES - 8, tm), F32)], axis=0)
    eye = jnp.where(row == col, 1.0, 0.0).astype(BF16)
    gate_ref[...] = sum(lax.dot_general(eye, part, nt, preferred_element_type=F32)
                        for part in pieces(gate_rows, 3))


def _router(d_out, f_out, x, w_out, g_moe, w_router_t, b_router_t):
    n, d = x.shape
    tm = TOKEN_TILE
    tile = lambda width: pl.BlockSpec((tm, width), lambda i: (i, 0))
    rows = pl.BlockSpec((8, tm), lambda i: (0, i))
    full = lambda arr: pl.BlockSpec(arr.shape, lambda i: (0,) * arr.ndim)
    return pl.pallas_call(
        _router_kernel,
        grid=(n // tm,),
        in_specs=[tile(DIFF_WIDTH), tile(FOX_WIDTH), tile(d), full(w_out), full(g_moe),
                  full(w_router_t), full(b_router_t)],
        out_specs=[tile(d), tile(d), tile(LANES), rows, rows,
                   pl.BlockSpec((N_EXPERTS, LANES), lambda i: (0, 0))],
        out_shape=[jax.ShapeDtypeStruct((n, d), F32), jax.ShapeDtypeStruct((n, d), F32),
                   jax.ShapeDtypeStruct((n, LANES), F32), jax.ShapeDtypeStruct((8, n), I32),
                   jax.ShapeDtypeStruct((8, n), I32), jax.ShapeDtypeStruct((N_EXPERTS, LANES), F32)],
        scratch_shapes=[pltpu.VMEM((N_EXPERTS, 1), F32)],
        compiler_params=_params("arbitrary"),
        name="router",
    )(d_out, f_out, x, w_out, g_moe, w_router_t, b_router_t)


def _dispatch_kernel(ends_ref, pos_ref, x_ref, xs_ref, zero_ref, sem):
    tm = x_ref.shape[0]
    te = zero_ref.shape[0]

    @pl.when(pl.program_id(0) == 0)
    def _():
        zero_ref[...] = jnp.zeros_like(zero_ref)

        def zero_tile(first_row):
            copy = pltpu.make_async_copy(zero_ref, xs_ref.at[pl.ds(pl.multiple_of(first_row, te), te), :], sem)
            copy.start()
            copy.wait()

        for e in range(N_EXPERTS):
            end = ends_ref[e]
            start = ends_ref[e - 1] if e else 0
            pl.when(end > start)(functools.partial(zero_tile, end - te))

        def unused(t, carry):
            zero_tile(t * te)
            return carry

        lax.fori_loop(ends_ref[N_EXPERTS - 1] // te, xs_ref.shape[0] // te, unused, 0)

    def issue(g, carry):
        base = pl.multiple_of(g * ROW_GROUP, ROW_GROUP)
        for j in range(ROW_GROUP):
            for k in range(TOP_K):
                dst = pos_ref[k * tm + base + j]
                pltpu.make_async_copy(x_ref.at[pl.ds(base + j, 1), :], xs_ref.at[pl.ds(dst, 1), :],
                                      sem).start(priority=k % 2)
        return carry

    lax.fori_loop(0, tm // ROW_GROUP, issue, 0)
    for _ in range(TOP_K):
        pltpu.make_async_copy(x_ref, xs_ref.at[pl.ds(0, tm), :], sem).wait()


def _dispatch(ends, pos_flat, xn, n_rows):
    n, d = xn.shape
    tm = TOKEN_TILE
    return pl.pallas_call(
        _dispatch_kernel,
        grid_spec=pltpu.PrefetchScalarGridSpec(
            num_scalar_prefetch=1,
            grid=(n // tm,),
            in_specs=[pl.BlockSpec((TOP_K * tm,), lambda i, ends: (i,), memory_space=pltpu.SMEM),
                      pl.BlockSpec((tm, d), lambda i, ends: (i, 0))],
            out_specs=pl.BlockSpec(memory_space=pl.ANY),
            scratch_shapes=[pltpu.VMEM((EXPERT_TILE, d), F32), pltpu.SemaphoreType.DMA(())]),
        out_shape=jax.ShapeDtypeStruct((n_rows, d), F32),
        compiler_params=_params("arbitrary"),
        name="dispatch",
    )(ends, pos_flat, xn)


def _expert_kernel(tile_expert_ref, n_used_ref, next_expert_ref, slot_ref,
                   x_ref, w1_ref, b1_ref, w2_ref, b2_ref, y_ref,
                   w1f_ref, w2f_ref, sem_ref, w1b_ref, w2p_ref, w2b_ref):
    t = pl.program_id(0)
    fresh = (t == 0) | (tile_expert_ref[t] != tile_expert_ref[jnp.maximum(t - 1, 0)])
    used = t < n_used_ref[0]

    def weight_copies(e, s):
        return (pltpu.make_async_copy(w1_ref.at[e], w1f_ref.at[s], sem_ref.at[0, s]),
                pltpu.make_async_copy(w2_ref.at[e], w2f_ref.at[s], sem_ref.at[1, s]))

    @pl.when(t == 0)
    def _():
        for copy in weight_copies(tile_expert_ref[0], slot_ref[0]):
            copy.start()

    @pl.when(fresh & used)
    def _():
        s = slot_ref[t]
        for copy in weight_copies(tile_expert_ref[t], s):
            copy.wait()

        @pl.when(next_expert_ref[t] >= 0)
        def _():
            for copy in weight_copies(next_expert_ref[t], 1 - s):
                copy.start()

        rows = LANES
        half = LANES // 2
        for r in range(0, w1b_ref.shape[0], rows):
            w1b_ref[r:r + rows, :] = w1f_ref[s, r:r + rows, :].astype(BF16)
        for c in range(w2p_ref.shape[0]):
            lanes = slice(c * LANES, (c + 1) * LANES)
            for r in range(0, w2b_ref.shape[0], rows):
                for b in range(2):
                    w2p_ref[c, pl.ds(r + b, half, stride=2), :] = (
                        w2f_ref[s, r + b * half:r + (b + 1) * half, lanes])
            w2b_ref[:, lanes] = w2p_ref[c].astype(BF16)

    @pl.when(used)
    def _():
        xb = x_ref[...].astype(BF16)
        h = jnp.dot(xb, w1b_ref[...], preferred_element_type=F32) + b1_ref[0]
        even = lax.broadcasted_iota(I32, (1, LANES), 1) % 2 == 0
        parts = []
        for q in range(h.shape[1] // (2 * LANES)):
            a = h[:, (2 * q) * LANES:(2 * q + 1) * LANES]
            b = h[:, (2 * q + 1) * LANES:(2 * q + 2) * LANES]
            hg = jnp.where(even, a, pltpu.roll(b, 1, 1))
            hl = jnp.where(even, pltpu.roll(a, LANES - 1, 1), b)
            glu = jnp.minimum(hg, SWIGLU_LIMIT)
            lin = jnp.clip(hl, -SWIGLU_LIMIT, SWIGLU_LIMIT)
            parts.append((glu * jax.nn.sigmoid(SWIGLU_ALPHA * glu) * (lin + 1.0)).astype(BF16))
        act = jnp.concatenate(parts, axis=1)
        y_ref[...] = jnp.dot(act, w2b_ref[...], preferred_element_type=F32) + b2_ref[0]

    @pl.when(jnp.logical_not(used))
    def _():
        y_ref[...] = jnp.zeros_like(y_ref)


def _experts(tile_expert, n_used, next_expert, slot, xs, w1, b1, w2, b2):
    n_rows, d = xs.shape
    tm = EXPERT_TILE
    by_expert = lambda arr: pl.BlockSpec((1,) + arr.shape[1:], lambda i, te, *_: (te[i], 0, 0))
    rows = pl.BlockSpec((tm, d), lambda i, *_: (i, 0))
    used_rows = pl.BlockSpec((tm, d), lambda i, te, nu, *_: (jnp.minimum(i, nu[0] - 1), 0))
    hbm = pl.BlockSpec(memory_space=pl.ANY)
    return pl.pallas_call(
        _expert_kernel,
        grid_spec=pltpu.PrefetchScalarGridSpec(
            num_scalar_prefetch=4,
            grid=(n_rows // tm,),
            in_specs=[used_rows, hbm, by_expert(b1), hbm, by_expert(b2)],
            out_specs=rows,
            scratch_shapes=[pltpu.VMEM((2,) + w1.shape[1:], F32), pltpu.VMEM((2,) + w2.shape[1:], F32),
                            pltpu.SemaphoreType.DMA((2, 2)),
                            pltpu.VMEM(w1.shape[1:], BF16),
                            pltpu.VMEM((w2.shape[2] // LANES, w2.shape[1], LANES), F32),
                            pltpu.VMEM(w2.shape[1:], BF16)]),
        out_shape=jax.ShapeDtypeStruct(xs.shape, F32),
        compiler_params=pltpu.CompilerParams(dimension_semantics=("arbitrary",),
                                             vmem_limit_bytes=EXPERT_VMEM_LIMIT),
        name="experts",
    )(tile_expert, n_used, next_expert, slot, xs, w1, b1, w2, b2)


def _combine_kernel(pos_ref, h_ref, gate_ref, p_ref, wg_ref, wp_ref, gple_ref, gfin_ref, y_ref,
                    o_ref, ybuf, sem, *, final_norm):
    tm = h_ref.shape[0]

    def issue(g, carry):
        base = pl.multiple_of(g * ROW_GROUP, ROW_GROUP)
        for j in range(ROW_GROUP):
            for k in range(TOP_K):
                src = pos_ref[k * tm + base + j]
                pltpu.make_async_copy(y_ref.at[pl.ds(src, 1), :], ybuf.at[k, pl.ds(base + j, 1), :],
                                      sem).start(priority=k % 2)
        return carry

    lax.fori_loop(0, tm // ROW_GROUP, issue, 0)
    for k in range(TOP_K):
        pltpu.make_async_copy(y_ref.at[pl.ds(0, tm), :], ybuf.at[k], sem).wait()

    gates = gate_ref[...]
    moe = gates[:, 0:1] * ybuf[0]
    for k in range(1, TOP_K):
        moe = moe + gates[:, k:k + 1] * ybuf[k]
    h = h_ref[...] + moe
    gate = jax.nn.sigmoid(jnp.dot(_rms(h, gple_ref[...]).astype(BF16), wg_ref[...],
                                  preferred_element_type=F32))
    emb = jnp.dot(p_ref[...].astype(BF16), wp_ref[...], preferred_element_type=F32)
    h = h + gate * emb
    o_ref[...] = _rms(h, gfin_ref[...]) if final_norm else h


def _combine(pos_flat, h, gates, p, w_gate, w_proj, g_ple, g_final, y, final_norm):
    n, d = h.shape
    tm = TOKEN_TILE
    tile = lambda width: pl.BlockSpec((tm, width), lambda i: (i, 0))
    full = lambda arr: pl.BlockSpec(arr.shape, lambda i: (0,) * arr.ndim)
    return pl.pallas_call(
        functools.partial(_combine_kernel, final_norm=final_norm),
        grid=(n // tm,),
        in_specs=[pl.BlockSpec((TOP_K * tm,), lambda i: (i,), memory_space=pltpu.SMEM),
                  tile(d), tile(LANES), tile(p.shape[1]), full(w_gate), full(w_proj),
                  full(g_ple), full(g_final), pl.BlockSpec(memory_space=pl.ANY)],
        out_specs=tile(d),
        out_shape=jax.ShapeDtypeStruct((n, d), F32),
        scratch_shapes=[pltpu.VMEM((TOP_K, tm, d), F32), pltpu.SemaphoreType.DMA(())],
        compiler_params=_params("arbitrary"),
        name="combine",
    )(pos_flat, h, gates, p, w_gate, w_proj, g_ple, g_final, y)


def _layer(h, p, positions, g_attn, w_in, lq1, lk1, lq2, lk2, g_subln, b_forget, w_out, g_moe,
           w_router, b_router, w_e1, b_e1, w_e2, b_e2, g_ple, w_ple_gate, w_ple_proj, g_final,
           lambda_init, final_norm):
    b, s, d = h.shape
    n = b * s
    row = lambda v: v.reshape(1, -1)

    w_main = w_in[:, :MAIN_WIDTH].astype(BF16)
    pad_heads = ((0, 2 * N_FOX_HEADS - N_FOX_HEADS), (0, 0))
    w_fz_t = jnp.pad(w_in[:, MAIN_WIDTH:].T, pad_heads).astype(BF16)
    b_fz_t = jnp.pad(b_forget.reshape(-1, 1), pad_heads)
    b1 = b_e1[:, None, :]
    b2 = b_e2[:, None, :]

    dq_t, dk, dv_t, fq_t, fk, fv_t, c, c_t = _inproj(positions, h, row(g_attn), w_main, w_fz_t, b_fz_t)
    c_rows = c_t.reshape(b, N_FOX_HEADS // 2, 2, s)
    d_out, f_out = _attention(dq_t, dk, dv_t, fq_t, fk, fv_t, c_rows, c, row(lq1), row(lk1), row(lq2),
                              row(lk2), g_subln.reshape(-1, 1), lambda_init)

    h1, xn, gates, idx, rank, counts = _router(
        d_out.reshape(n, DIFF_WIDTH), f_out.reshape(n, FOX_WIDTH), h.reshape(n, d),
        w_out.astype(BF16), row(g_moe), w_router.T, b_router.reshape(-1, 1))

    te = EXPERT_TILE
    n_rows = n * TOP_K + N_EXPERTS * te
    counts = counts[:, 0].astype(I32)
    padded = (counts + te - 1) // te * te
    ends = jnp.cumsum(padded)
    starts = ends - padded
    experts = jnp.arange(N_EXPERTS, dtype=I32)[:, None, None]
    pos = rank[:TOP_K] + jnp.sum(jnp.where(idx[None, :TOP_K] == experts, starts[:, None, None], 0),
                                 axis=0)
    tm = TOKEN_TILE
    pos_flat = pos.reshape(TOP_K, n // tm, tm).transpose(1, 0, 2).reshape(-1)
    tile_start = jnp.arange(n_rows // te, dtype=I32) * te
    tile_expert = jnp.minimum(jnp.sum((ends[None, :] <= tile_start[:, None]).astype(I32), axis=1),
                              N_EXPERTS - 1)
    n_used = (ends[-1:] // te).astype(I32)
    owns = padded > 0
    later = lax.cummin(jnp.where(owns, jnp.arange(N_EXPERTS, dtype=I32), N_EXPERTS), reverse=True)
    following = jnp.concatenate([later[1:], jnp.full((1,), N_EXPERTS, I32)])
    following = jnp.where(following < N_EXPERTS, following, -1)
    next_expert = following[tile_expert]
    slot = ((jnp.cumsum(owns.astype(I32)) - 1) % 2)[tile_expert]

    xs = _dispatch(ends.astype(I32), pos_flat, xn, n_rows)
    y = _experts(tile_expert, n_used, next_expert.astype(I32), slot.astype(I32), xs, w_e1, b1, w_e2, b2)
    out = _combine(pos_flat, h1, gates, p.reshape(n, PLE_DIM), w_ple_gate.astype(BF16),
                   w_ple_proj.astype(BF16), row(g_ple), row(g_final), y, final_norm)
    return out.reshape(b, s, d)


def kernel(x, p, positions, g_attn, w_in, lambda_q1, lambda_k1, lambda_q2, lambda_k2, g_subln, b_forget, w_out, g_moe, w_router, b_router, w_e1, b_e1, w_e2, b_e2, g_ple, w_ple_gate, w_ple_proj, g_final):
    depth = g_attn.shape[0]
    h = x
    for i in range(depth):
        lambda_init = 0.8 - 0.6 * math.exp(-0.3 * i)
        h = _layer(h, p[i], positions, g_attn[i], w_in[i], lambda_q1[i], lambda_k1[i],
                   lambda_q2[i], lambda_k2[i], g_subln[i], b_forget[i], w_out[i], g_moe[i],
                   w_router[i], b_router[i], w_e1[i], b_e1[i], w_e2[i], b_e2[i], g_ple[i],
                   w_ple_gate[i], w_ple_proj[i], g_final, lambda_init, i == depth - 1)
    return h
```

```python
import functools
import math

import jax
import jax.numpy as jnp
from jax import lax
from jax.experimental import pallas as pl
from jax.experimental.pallas import tpu as pltpu

F32 = jnp.float32
BF16 = jnp.bfloat16
I32 = jnp.int32

LANES = 128
D_MODEL = 1024
HEAD_DIM = 64
N_DIFF_HEADS = 4
DIFF_V_DIM = 2 * HEAD_DIM
N_FOX_HEADS = 8
DIFF_WIDTH = N_DIFF_HEADS * DIFF_V_DIM
FOX_WIDTH = N_FOX_HEADS * HEAD_DIM
ROT_DIM = HEAD_DIM // 4
ROPE_THETA = 500000.0
N_EXPERTS = 32
TOP_K = 4
D_EXPERT = D_MODEL
SWIGLU_ALPHA = 1.702
SWIGLU_LIMIT = 7.0
PLE_DIM = 256
NORM_EPS = 1e-5
LOG2_E = math.log2(math.e)
MAIN_WIDTH = 3 * DIFF_WIDTH + 3 * FOX_WIDTH

SEQ_TILE = 256
ATTN_TILE = 256
ATTN_UNITS = 2
TOKEN_TILE = 256
EXPERT_TILE = 256
ROW_GROUP = 8
VMEM_LIMIT = 48 * 1024 * 1024
EXPERT_VMEM_LIMIT = 58 * 1024 * 1024


def _params(*semantics):
    return pltpu.CompilerParams(dimension_semantics=semantics, vmem_limit_bytes=VMEM_LIMIT)


def _rms(x, g):
    return x * lax.rsqrt(jnp.mean(x * x, axis=-1, keepdims=True) + NORM_EPS) * g


def _inproj_kernel(pos_ref, x_ref, g_ref, w_ref, wfzt_ref, bfzt_ref,
                   dqt_ref, dk_ref, dvt_ref, fqt_ref, fk_ref, fvt_ref, c_ref, ct_ref, carry_t_ref):
    ts = x_ref.shape[1]
    xb = _rms(x_ref[0], g_ref[...]).astype(BF16)
    proj = jnp.dot(xb, w_ref[...], preferred_element_type=F32)
    nt = (((1,), (1,)), ((), ()))
    tn = (((0,), (0,)), ((), ()))

    def pieces(v):
        hi = v.astype(BF16)
        r1 = v - hi.astype(F32)
        mid = r1.astype(BF16)
        return hi, mid, (r1 - mid.astype(F32)).astype(BF16)

    def rows_to_lanes(v, select):
        return sum(lax.dot_general(part, select, tn, preferred_element_type=F32) for part in pieces(v))

    half_rot = ROT_DIM // 2
    freq = lax.broadcasted_iota(I32, (half_rot, 1), 0)
    inv_freq = jnp.power(ROPE_THETA, -(2 * freq).astype(F32) / ROT_DIM)
    ang = inv_freq * pos_ref[0].astype(F32)
    trig = jnp.concatenate([jnp.cos(ang), jnp.sin(ang)], axis=0)
    r = lax.broadcasted_iota(I32, (2 * half_rot, 2 * LANES), 0)
    lane2 = lax.broadcasted_iota(I32, (2 * half_rot, 2 * LANES), 1)
    d = lane2 % HEAD_DIM
    same_freq = jnp.where(d % half_rot == r % half_rot, 1.0, 0.0)
    cos_part = jnp.where((r < half_rot) & (lane2 < LANES) & (d < ROT_DIM), same_freq, 0.0)
    sin_part = jnp.where((r >= half_rot) & (lane2 >= LANES) & (d < ROT_DIM),
                         jnp.where(d < half_rot, -same_freq, same_freq), 0.0)
    tables = rows_to_lanes(trig, (cos_part + sin_part).astype(BF16))
    lane = lax.broadcasted_iota(I32, (1, LANES), 1) % HEAD_DIM
    cosf = jnp.where(lane < ROT_DIM, tables[:, :LANES], 1.0)
    sinf = tables[:, LANES:]
    first_half = lane < half_rot

    def rope(t):
        outs = []
        for j in range(t.shape[1] // LANES):
            tj = t[:, j * LANES:(j + 1) * LANES]
            partner = jnp.where(first_half,
                                pltpu.roll(tj, LANES - ROT_DIM // 2, 1),
                                pltpu.roll(tj, ROT_DIM // 2, 1))
            outs.append(tj * cosf + partner * sinf)
        return jnp.concatenate(outs, axis=1)

    scale = HEAD_DIM ** -0.5 * LOG2_E
    w = DIFF_WIDTH
    dqt_ref[0] = (rope(proj[:, 0:w]) * scale).astype(BF16).T
    dk_ref[0] = rope(proj[:, w:2 * w]).astype(BF16)
    dvt_ref[0, 0] = proj[:, 2 * w:3 * w].astype(BF16).T
    fqt_ref[0] = (proj[:, 3 * w:4 * w] * scale).astype(BF16).T
    fk_ref[0] = proj[:, 4 * w:5 * w].astype(BF16)
    fvt_ref[0, 0] = proj[:, 5 * w:6 * w].astype(BF16).T

    def log_sigmoid(z):
        return jnp.minimum(z, 0.0) - jnp.log1p(jnp.exp(-jnp.abs(z)))

    row = lax.broadcasted_iota(I32, (ts, ts), 0)
    col = lax.broadcasted_iota(I32, (ts, ts), 1)
    upto_col = jnp.where(row <= col, 1.0, 0.0).astype(BF16)

    @pl.when(pl.program_id(1) == 0)
    def _():
        carry_t_ref[...] = jnp.zeros_like(carry_t_ref)

    fzt = lax.dot_general(wfzt_ref[...], xb, nt, preferred_element_type=F32) + bfzt_ref[...]
    ct = carry_t_ref[...] + sum(jnp.dot(part, upto_col, preferred_element_type=F32)
                                for part in pieces(log_sigmoid(fzt)))
    carry_t_ref[...] = ct[:, ts - 1:ts]
    ct = ct * LOG2_E
    ct_ref[0] = ct[0:N_FOX_HEADS, :]
    head = lax.broadcasted_iota(I32, (ct.shape[0], FOX_WIDTH), 0)
    group = lax.broadcasted_iota(I32, (ct.shape[0], FOX_WIDTH), 1) // HEAD_DIM
    c_ref[0] = rows_to_lanes(ct, jnp.where(head == group, 1.0, 0.0).astype(BF16))


def _inproj(positions, x, g, w_main, w_fz_t, b_fz_t):
    b, s, d = x.shape
    ts = SEQ_TILE
    assert ts == ATTN_TILE
    tile = lambda width: pl.BlockSpec((1, ts, width), lambda i, j: (i, j, 0))
    tile_t = pl.BlockSpec((1, DIFF_WIDTH, ts), lambda i, j: (i, 0, j))
    key_tile_t = pl.BlockSpec((1, 1, DIFF_WIDTH, ts), lambda i, j: (i, j, 0, 0))
    full = lambda arr: pl.BlockSpec(arr.shape, lambda i, j: (0,) * arr.ndim)
    rows = jax.ShapeDtypeStruct((b, s, DIFF_WIDTH), BF16)
    rows_t = jax.ShapeDtypeStruct((b, DIFF_WIDTH, s), BF16)
    key_tiles_t = jax.ShapeDtypeStruct((b, s // ts, DIFF_WIDTH, ts), BF16)
    return pl.pallas_call(
        _inproj_kernel,
        grid=(b, s // ts),
        in_specs=[pl.BlockSpec((1, 1, ts), lambda i, j: (i, 0, j)), tile(d), full(g), full(w_main),
                  full(w_fz_t), full(b_fz_t)],
        out_specs=[tile_t, tile(DIFF_WIDTH), key_tile_t, tile_t, tile(FOX_WIDTH), key_tile_t,
                   tile(FOX_WIDTH), pl.BlockSpec((1, N_FOX_HEADS, ts), lambda i, j: (i, 0, j))],
        out_shape=[rows_t, rows, key_tiles_t, rows_t, rows, key_tiles_t,
                   jax.ShapeDtypeStruct((b, s, FOX_WIDTH), F32),
                   jax.ShapeDtypeStruct((b, N_FOX_HEADS, s), F32)],
        scratch_shapes=[pltpu.VMEM((w_fz_t.shape[0], 1), F32)],
        compiler_params=_params("arbitrary", "arbitrary"),
        name="inproj",
    )(positions.reshape(b, 1, s), x, g, w_main, w_fz_t, b_fz_t)


ONES_ROWS = 16


def _block_diag_queries(qt):
    row = lax.broadcasted_iota(I32, qt.shape, 0)
    zero = jnp.zeros_like(qt)
    return jnp.concatenate([jnp.where(row < HEAD_DIM, qt, zero),
                            jnp.where(row >= HEAD_DIM, qt, zero)], axis=1)


class _Chain:
    def __init__(self, qbd, keys, values_t, scratch, key_bias=None, query_bias=None):
        self.qbd, self.keys, self.values_t = qbd, keys, values_t
        self.acc_ref, self.s_ref, self.p_ref, self.stat_ref = scratch
        self.key_bias, self.query_bias = key_bias, query_bias


_RUNNING_MAX, _TILE_MAX, _RESCALE = 0, 1, 2


def _causal_attention(chains, qi):
    tk, tq = chains[0].s_ref.shape[0], chains[0].s_ref.shape[1] // 2
    ones = jnp.ones((ONES_ROWS, tk), BF16)

    def scores(ch, i, diagonal):
        s = jnp.dot(ch.keys(i), ch.qbd, preferred_element_type=F32)
        if ch.key_bias is not None:
            s = s - ch.key_bias(i)
        if diagonal is not False:
            r = lax.broadcasted_iota(I32, s.shape, 0)
            c = lax.broadcasted_iota(I32, s.shape, 1) & (tq - 1)
            visible = r <= c if diagonal is True else jnp.logical_or(r <= c, jnp.logical_not(diagonal))
            s = jnp.where(visible, s, -jnp.inf)
        ch.s_ref[...] = s
        top = jnp.max(s, axis=0, keepdims=True)
        if ch.query_bias is not None:
            top = top + ch.query_bias
        ch.stat_ref[_TILE_MAX:_TILE_MAX + 1, :] = top

    def probs(ch):
        m = ch.stat_ref[_RUNNING_MAX:_RUNNING_MAX + 1, :]
        m_new = jnp.maximum(m, ch.stat_ref[_TILE_MAX:_TILE_MAX + 1, :])
        ch.stat_ref[_RUNNING_MAX:_RUNNING_MAX + 1, :] = m_new
        ch.stat_ref[_RESCALE:_RESCALE + 1, :] = jnp.exp2(m - m_new)
        ref = m_new if ch.query_bias is None else m_new - ch.query_bias
        ch.p_ref[...] = jnp.exp2(ch.s_ref[...] - ref).astype(BF16)

    def accumulate(ch, i):
        vts = jnp.concatenate([ch.values_t(i), ones], axis=0)
        ch.acc_ref[...] = (ch.stat_ref[_RESCALE:_RESCALE + 1, :] * ch.acc_ref[...]
                           + jnp.dot(vts, ch.p_ref[...], preferred_element_type=F32))

    def body(k, diagonal_next):
        for ch in chains:
            accumulate(ch, jnp.maximum(k - 1, 0))
        for ch in chains:
            probs(ch)
        for ch in chains:
            scores(ch, k + 1, diagonal_next)

    for ch in chains:
        ch.acc_ref[...] = jnp.zeros_like(ch.acc_ref)
        ch.p_ref[...] = jnp.zeros_like(ch.p_ref)
        ch.stat_ref[...] = jnp.concatenate(
            [jnp.full((1, 2 * tq), -jnp.inf, F32), jnp.ones((ch.stat_ref.shape[0] - 1, 2 * tq), F32)], axis=0)
        scores(ch, 0, qi == 0)

    def loop_body(k, carry):
        body(k, False)
        return carry

    lax.fori_loop(0, qi - 1, loop_body, 0)

    @pl.when(qi >= 1)
    def _():
        body(qi - 1, True)

    for ch in chains:
        accumulate(ch, jnp.maximum(qi - 1, 0))
    for ch in chains:
        probs(ch)
    for ch in chains:
        accumulate(ch, qi)


def _attention_kernel(lq1_ref, lk1_ref, lq2_ref, lk2_ref, g_ref,
                      dqt_ref, dk_ref, dvt_ref, fqt_ref, fk_ref, fvt_ref, cq_ref, ck_ref,
                      do_ref, fo_ref, *scratch, lambda_init):
    tq = dqt_ref.shape[2]
    tk = dvt_ref.shape[3]
    units = dqt_ref.shape[1] // LANES
    lane = lax.broadcasted_iota(I32, (1, LANES), 1)
    scratch = [scratch[i:i + 4] for i in range(0, len(scratch), 4)]

    def unit_chains(u):
        lanes = slice(u * LANES, (u + 1) * LANES)

        def tile(ref):
            return lambda i: ref[0, pl.ds(pl.multiple_of(i * tk, tk), tk), lanes]

        def tile_t(ref):
            return lambda i: ref[0, i, lanes, :]

        cq = jnp.concatenate([cq_ref[0, u, 0:1, :], cq_ref[0, u, 1:2, :]], axis=1)

        def key_bias(i):
            ck = tile(ck_ref)(i)
            other = pltpu.roll(ck, HEAD_DIM, 1)
            first = jnp.where(lane < HEAD_DIM, ck, other)
            second = jnp.where(lane >= HEAD_DIM, ck, other)
            return jnp.concatenate([first] * (tq // LANES) + [second] * (tq // LANES), axis=1)

        return [_Chain(_block_diag_queries(dqt_ref[0, lanes, :]), tile(dk_ref), tile_t(dvt_ref),
                       scratch[2 * u]),
                _Chain(_block_diag_queries(fqt_ref[0, lanes, :]), tile(fk_ref), tile_t(fvt_ref),
                       scratch[2 * u + 1], key_bias, cq)]

    chains = [ch for u in range(units) for ch in unit_chains(u)]
    _causal_attention(chains, pl.program_id(2))

    lam = (jnp.exp(jnp.sum(lq1_ref[...] * lk1_ref[...], axis=-1, keepdims=True))
           - jnp.exp(jnp.sum(lq2_ref[...] * lk2_ref[...], axis=-1, keepdims=True)) + lambda_init)
    dv, h = DIFF_V_DIM, HEAD_DIM
    for u in range(units):
        lanes = slice(u * LANES, (u + 1) * LANES)
        acc = chains[2 * u].acc_ref
        o = (acc[0:dv, 0:tq] / acc[dv:dv + 1, 0:tq]
             - lam * (acc[0:dv, tq:] / acc[dv:dv + 1, tq:]))
        ms = jnp.mean(o * o, axis=0, keepdims=True)
        do_ref[0, :, lanes] = (o * lax.rsqrt(ms + NORM_EPS) * g_ref[...]
                               * (1.0 - lambda_init)).T.astype(BF16)
        acc = chains[2 * u + 1].acc_ref
        fo_ref[0, :, lanes] = jnp.concatenate([acc[0:h, 0:tq] / acc[2 * h:2 * h + 1, 0:tq],
                                               acc[h:2 * h, tq:] / acc[2 * h:2 * h + 1, tq:]],
                                              axis=0).T.astype(BF16)


def _attention(dqt, dk, dvt, fqt, fk, fvt, c_rows, c_cols, lq1, lk1, lq2, lk2, g_col, lambda_init):
    b, s, _ = dk.shape
    t = ATTN_TILE
    assert N_DIFF_HEADS == N_FOX_HEADS // 2 and N_DIFF_HEADS % ATTN_UNITS == 0
    width = ATTN_UNITS * LANES
    vec = lambda arr: pl.BlockSpec(arr.shape, lambda i, h, j: (0, 0))
    q_spec = pl.BlockSpec((1, width, t), lambda i, h, j: (i, h, j))
    k_spec = pl.BlockSpec((1, s, width), lambda i, h, j: (i, 0, h))
    v_spec = pl.BlockSpec((1, s // t, width, t), lambda i, h, j: (i, 0, h, 0))
    o_spec = pl.BlockSpec((1, t, width), lambda i, h, j: (i, j, h))
    out = jax.ShapeDtypeStruct((b, s, DIFF_WIDTH), BF16)
    chain_scratch = [pltpu.VMEM((LANES + ONES_ROWS, 2 * t), F32),
                     pltpu.VMEM((t, 2 * t), F32),
                     pltpu.VMEM((t, 2 * t), BF16),
                     pltpu.VMEM((8, 2 * t), F32)]
    return pl.pallas_call(
        functools.partial(_attention_kernel, lambda_init=lambda_init),
        grid=(b, N_DIFF_HEADS // ATTN_UNITS, s // t),
        in_specs=[vec(lq1), vec(lk1), vec(lq2), vec(lk2), vec(g_col),
                  q_spec, k_spec, v_spec, q_spec, k_spec, v_spec,
                  pl.BlockSpec((1, ATTN_UNITS, 2, t), lambda i, h, j: (i, h, 0, j)), k_spec],
        out_specs=[o_spec, o_spec],
        out_shape=[out, out],
        scratch_shapes=chain_scratch * (2 * ATTN_UNITS),
        compiler_params=_params("arbitrary", "arbitrary", "arbitrary"),
        name="attention",
    )(lq1, lk1, lq2, lk2, g_col, dqt, dk, dvt, fqt, fk, fvt, c_rows, c_cols)


def _router_kernel(d_ref, f_ref, x_ref, wo_ref, g_ref, wrt_ref, brt_ref,
                   h_ref, xn_ref, gate_ref, idx_ref, rank_ref, count_ref, carry_ref):
    tm = x_ref.shape[0]
    mixed = (jnp.dot(d_ref[...], wo_ref[0:DIFF_WIDTH, :], preferred_element_type=F32)
             + jnp.dot(f_ref[...], wo_ref[DIFF_WIDTH:, :], preferred_element_type=F32))
    h = x_ref[...] + mixed
    h_ref[...] = h
    xn = _rms(h, g_ref[...])
    xn_ref[...] = xn

    def pieces(v, count):
        out = []
        for _ in range(count):
            part = v.astype(BF16)
            out.append(part)
            v = v - part.astype(F32)
        return out

    nt = (((1,), (1,)), ((), ()))
    xh, xl = pieces(xn, 2)
    wh, wl = pieces(wrt_ref[...], 2)
    work = (lax.dot_general(wh, xh, nt, preferred_element_type=F32)
            + lax.dot_general(wh, xl, nt, preferred_element_type=F32)
            + lax.dot_general(wl, xh, nt, preferred_element_type=F32)) + brt_ref[...]
    expert = lax.broadcasted_iota(I32, work.shape, 0)

    vals, ids, picks = [], [], []
    for _ in range(TOP_K):
        m = jnp.max(work, axis=0, keepdims=True)
        idx = jnp.min(jnp.where(work == m, expert, N_EXPERTS), axis=0, keepdims=True)
        pick = expert == idx
        vals.append(m)
        ids.append(idx)
        picks.append(pick)
        work = jnp.where(pick, -jnp.inf, work)
    exps = [jnp.exp(v - vals[0]) for v in vals]
    denom = exps[0] + exps[1] + exps[2] + exps[3]

    chosen = jnp.where(picks[0] | picks[1] | picks[2] | picks[3], 1.0, 0.0)
    row = lax.broadcasted_iota(I32, (tm, tm), 0)
    col = lax.broadcasted_iota(I32, (tm, tm), 1)
    earlier = jnp.where(row < col, 1.0, 0.0).astype(BF16)

    @pl.when(pl.program_id(0) == 0)
    def _():
        carry_ref[...] = jnp.zeros_like(carry_ref)

    rank = jnp.dot(chosen.astype(BF16), earlier, preferred_element_type=F32) + carry_ref[...]
    carry_ref[...] = carry_ref[...] + jnp.sum(chosen, axis=1, keepdims=True)
    count_ref[...] = jnp.broadcast_to(carry_ref[...], count_ref.shape)

    slot = lax.broadcasted_iota(I32, (8, tm), 0)
    gate_t = jnp.zeros((8, tm), F32)
    idx_t = jnp.zeros((8, tm), I32)
    rank_t = jnp.zeros((8, tm), F32)
    for k in range(TOP_K):
        gate_t = jnp.where(slot == k, exps[k] / denom, gate_t)
        idx_t = jnp.where(slot == k, ids[k], idx_t)
        rank_t = jnp.where(slot == k, jnp.sum(jnp.where(picks[k], rank, 0.0), axis=0, keepdims=True), rank_t)
    idx_ref[...] = idx_t
    rank_ref[...] = rank_t.astype(I32)
    gate_rows = jnp.concatenate([gate_t, jnp.zeros((LANES - 8, tm), F32)], axis=0)
    eye = jnp.where(row == col, 1.0, 0.0).astype(BF16)
    gate_ref[...] = sum(lax.dot_general(eye, part, nt, preferred_element_type=F32)
                        for part in pieces(gate_rows, 3))


def _router(d_out, f_out, x, w_out, g_moe, w_router_t, b_router_t):
    n, d = x.shape
    tm = TOKEN_TILE
    tile = lambda width: pl.BlockSpec((tm, width), lambda i: (i, 0))
    rows = pl.BlockSpec((8, tm), lambda i: (0, i))
    full = lambda arr: pl.BlockSpec(arr.shape, lambda i: (0,) * arr.ndim)
    return pl.pallas_call(
        _router_kernel,
        grid=(n // tm,),
        in_specs=[tile(DIFF_WIDTH), tile(FOX_WIDTH), tile(d), full(w_out), full(g_moe),
                  full(w_router_t), full(b_router_t)],
        out_specs=[tile(d), tile(d), tile(LANES), rows, rows,
                   pl.BlockSpec((N_EXPERTS, LANES), lambda i: (0, 0))],
        out_shape=[jax.ShapeDtypeStruct((n, d), F32), jax.ShapeDtypeStruct((n, d), F32),
                   jax.ShapeDtypeStruct((n, LANES), F32), jax.ShapeDtypeStruct((8, n), I32),
                   jax.ShapeDtypeStruct((8, n), I32), jax.ShapeDtypeStruct((N_EXPERTS, LANES), F32)],
        scratch_shapes=[pltpu.VMEM((N_EXPERTS, 1), F32)],
        compiler_params=_params("arbitrary"),
        name="router",
    )(d_out, f_out, x, w_out, g_moe, w_router_t, b_router_t)


def _dispatch_kernel(ends_ref, pos_ref, x_ref, xs_ref, zero_ref, sem):
    tm = x_ref.shape[0]
    te = zero_ref.shape[0]

    @pl.when(pl.program_id(0) == 0)
    def _():
        zero_ref[...] = jnp.zeros_like(zero_ref)

        def zero_tile(first_row):
            copy = pltpu.make_async_copy(zero_ref, xs_ref.at[pl.ds(pl.multiple_of(first_row, te), te), :], sem)
            copy.start()
            copy.wait()

        for e in range(N_EXPERTS):
            end = ends_ref[e]
            start = ends_ref[e - 1] if e else 0
            pl.when(end > start)(functools.partial(zero_tile, end - te))

        def unused(t, carry):
            zero_tile(t * te)
            return carry

        lax.fori_loop(ends_ref[N_EXPERTS - 1] // te, xs_ref.shape[0] // te, unused, 0)

    def issue(g, carry):
        base = pl.multiple_of(g * ROW_GROUP, ROW_GROUP)
        for j in range(ROW_GROUP):
            for k in range(TOP_K):
                dst = pos_ref[k * tm + base + j]
                pltpu.make_async_copy(x_ref.at[pl.ds(base + j, 1), :], xs_ref.at[pl.ds(dst, 1), :],
                                      sem).start(priority=k % 2)
        return carry

    lax.fori_loop(0, tm // ROW_GROUP, issue, 0)
    for _ in range(TOP_K):
        pltpu.make_async_copy(x_ref, xs_ref.at[pl.ds(0, tm), :], sem).wait()


def _dispatch(ends, pos_flat, xn, n_rows):
    n, d = xn.shape
    tm = TOKEN_TILE
    return pl.pallas_call(
        _dispatch_kernel,
        grid_spec=pltpu.PrefetchScalarGridSpec(
            num_scalar_prefetch=1,
            grid=(n // tm,),
            in_specs=[pl.BlockSpec((TOP_K * tm,), lambda i, ends: (i,), memory_space=pltpu.SMEM),
                      pl.BlockSpec((tm, d), lambda i, ends: (i, 0))],
            out_specs=pl.BlockSpec(memory_space=pl.ANY),
            scratch_shapes=[pltpu.VMEM((EXPERT_TILE, d), F32), pltpu.SemaphoreType.DMA(())]),
        out_shape=jax.ShapeDtypeStruct((n_rows, d), F32),
        compiler_params=_params("arbitrary"),
        name="dispatch",
    )(ends, pos_flat, xn)


def _expert_kernel(tile_expert_ref, n_used_ref, next_expert_ref, slot_ref,
                   x_ref, w1_ref, b1_ref, w2_ref, b2_ref, y_ref,
                   w1f_ref, w2f_ref, sem_ref, w1b_ref, w2p_ref, w2b_ref):
    t = pl.program_id(0)
    fresh = (t == 0) | (tile_expert_ref[t] != tile_expert_ref[jnp.maximum(t - 1, 0)])
    used = t < n_used_ref[0]

    def weight_copies(e, s):
        return (pltpu.make_async_copy(w1_ref.at[e], w1f_ref.at[s], sem_ref.at[0, s]),
                pltpu.make_async_copy(w2_ref.at[e], w2f_ref.at[s], sem_ref.at[1, s]))

    @pl.when(t == 0)
    def _():
        for copy in weight_copies(tile_expert_ref[0], slot_ref[0]):
            copy.start()

    @pl.when(fresh & used)
    def _():
        s = slot_ref[t]
        for copy in weight_copies(tile_expert_ref[t], s):
            copy.wait()

        @pl.when(next_expert_ref[t] >= 0)
        def _():
            for copy in weight_copies(next_expert_ref[t], 1 - s):
                copy.start()

        rows = LANES
        half = LANES // 2
        for r in range(0, w1b_ref.shape[0], rows):
            w1b_ref[r:r + rows, :] = w1f_ref[s, r:r + rows, :].astype(BF16)
        for c in range(w2p_ref.shape[0]):
            lanes = slice(c * LANES, (c + 1) * LANES)
            for r in range(0, w2b_ref.shape[0], rows):
                for b in range(2):
                    w2p_ref[c, pl.ds(r + b, half, stride=2), :] = (
                        w2f_ref[s, r + b * half:r + (b + 1) * half, lanes])
            w2b_ref[:, lanes] = w2p_ref[c].astype(BF16)

    @pl.when(used)
    def _():
        xb = x_ref[...].astype(BF16)
        h = jnp.dot(xb, w1b_ref[...], preferred_element_type=F32) + b1_ref[0]
        even = lax.broadcasted_iota(I32, (1, LANES), 1) % 2 == 0
        parts = []
        for q in range(h.shape[1] // (2 * LANES)):
            a = h[:, (2 * q) * LANES:(2 * q + 1) * LANES]
            b = h[:, (2 * q + 1) * LANES:(2 * q + 2) * LANES]
            hg = jnp.where(even, a, pltpu.roll(b, 1, 1))
            hl = jnp.where(even, pltpu.roll(a, LANES - 1, 1), b)
            glu = jnp.minimum(hg, SWIGLU_LIMIT)
            lin = jnp.clip(hl, -SWIGLU_LIMIT, SWIGLU_LIMIT)
            parts.append((glu * jax.nn.sigmoid(SWIGLU_ALPHA * glu) * (lin + 1.0)).astype(BF16))
        act = jnp.concatenate(parts, axis=1)
        y_ref[...] = jnp.dot(act, w2b_ref[...], preferred_element_type=F32) + b2_ref[0]

    @pl.when(jnp.logical_not(used))
    def _():
        y_ref[...] = jnp.zeros_like(y_ref)


def _experts(tile_expert, n_used, next_expert, slot, xs, w1, b1, w2, b2):
    n_rows, d = xs.shape
    tm = EXPERT_TILE
    by_expert = lambda arr: pl.BlockSpec((1,) + arr.shape[1:], lambda i, te, *_: (te[i], 0, 0))
    rows = pl.BlockSpec((tm, d), lambda i, *_: (i, 0))
    used_rows = pl.BlockSpec((tm, d), lambda i, te, nu, *_: (jnp.minimum(i, nu[0] - 1), 0))
    hbm = pl.BlockSpec(memory_space=pl.ANY)
    return pl.pallas_call(
        _expert_kernel,
        grid_spec=pltpu.PrefetchScalarGridSpec(
            num_scalar_prefetch=4,
            grid=(n_rows // tm,),
            in_specs=[used_rows, hbm, by_expert(b1), hbm, by_expert(b2)],
            out_specs=rows,
            scratch_shapes=[pltpu.VMEM((2,) + w1.shape[1:], F32), pltpu.VMEM((2,) + w2.shape[1:], F32),
                            pltpu.SemaphoreType.DMA((2, 2)),
                            pltpu.VMEM(w1.shape[1:], BF16),
                            pltpu.VMEM((w2.shape[2] // LANES, w2.shape[1], LANES), F32),
                            pltpu.VMEM(w2.shape[1:], BF16)]),
        out_shape=jax.ShapeDtypeStruct(xs.shape, F32),
        compiler_params=pltpu.CompilerParams(dimension_semantics=("arbitrary",),
                                             vmem_limit_bytes=EXPERT_VMEM_LIMIT),
        name="experts",
    )(tile_expert, n_used, next_expert, slot, xs, w1, b1, w2, b2)


def _combine_kernel(pos_ref, h_ref, gate_ref, p_ref, wg_ref, wp_ref, gple_ref, gfin_ref, y_ref,
                    o_ref, ybuf, sem, *, final_norm):
    tm = h_ref.shape[0]

    def issue(g, carry):
        base = pl.multiple_of(g * ROW_GROUP, ROW_GROUP)
        for j in range(ROW_GROUP):
            for k in range(TOP_K):
                src = pos_ref[k * tm + base + j]
                pltpu.make_async_copy(y_ref.at[pl.ds(src, 1), :], ybuf.at[k, pl.ds(base + j, 1), :],
                                      sem).start(priority=k % 2)
        return carry

    lax.fori_loop(0, tm // ROW_GROUP, issue, 0)
    for k in range(TOP_K):
        pltpu.make_async_copy(y_ref.at[pl.ds(0, tm), :], ybuf.at[k], sem).wait()

    gates = gate_ref[...]
    moe = gates[:, 0:1] * ybuf[0]
    for k in range(1, TOP_K):
        moe = moe + gates[:, k:k + 1] * ybuf[k]
    h = h_ref[...] + moe
    gate = jax.nn.sigmoid(jnp.dot(_rms(h, gple_ref[...]).astype(BF16), wg_ref[...],
                                  preferred_element_type=F32))
    emb = jnp.dot(p_ref[...].astype(BF16), wp_ref[...], preferred_element_type=F32)
    h = h + gate * emb
    o_ref[...] = _rms(h, gfin_ref[...]) if final_norm else h


def _combine(pos_flat, h, gates, p, w_gate, w_proj, g_ple, g_final, y, final_norm):
    n, d = h.shape
    tm = TOKEN_TILE
    tile = lambda width: pl.BlockSpec((tm, width), lambda i: (i, 0))
    full = lambda arr: pl.BlockSpec(arr.shape, lambda i: (0,) * arr.ndim)
    return pl.pallas_call(
        functools.partial(_combine_kernel, final_norm=final_norm),
        grid=(n // tm,),
        in_specs=[pl.BlockSpec((TOP_K * tm,), lambda i: (i,), memory_space=pltpu.SMEM),
                  tile(d), tile(LANES), tile(p.shape[1]), full(w_gate), full(w_proj),
                  full(g_ple), full(g_final), pl.BlockSpec(memory_space=pl.ANY)],
        out_specs=tile(d),
        out_shape=jax.ShapeDtypeStruct((n, d), F32),
        scratch_shapes=[pltpu.VMEM((TOP_K, tm, d), F32), pltpu.SemaphoreType.DMA(())],
        compiler_params=_params("arbitrary"),
        name="combine",
    )(pos_flat, h, gates, p, w_gate, w_proj, g_ple, g_final, y)


def _layer(h, p, positions, g_attn, w_in, lq1, lk1, lq2, lk2, g_subln, b_forget, w_out, g_moe,
           w_router, b_router, w_e1, b_e1, w_e2, b_e2, g_ple, w_ple_gate, w_ple_proj, g_final,
           lambda_init, final_norm):
    b, s, d = h.shape
    n = b * s
    row = lambda v: v.reshape(1, -1)

    w_main = w_in[:, :MAIN_WIDTH].astype(BF16)
    pad_heads = ((0, 2 * N_FOX_HEADS - N_FOX_HEADS), (0, 0))
    w_fz_t = jnp.pad(w_in[:, MAIN_WIDTH:].T, pad_heads).astype(BF16)
    b_fz_t = jnp.pad(b_forget.reshape(-1, 1), pad_heads)
    b1 = b_e1[:, None, :]
    b2 = b_e2[:, None, :]

    dq_t, dk, dv_t, fq_t, fk, fv_t, c, c_t = _inproj(positions, h, row(g_attn), w_main, w_fz_t, b_fz_t)
    c_rows = c_t.reshape(b, N_FOX_HEADS // 2, 2, s)
    d_out, f_out = _attention(dq_t, dk, dv_t, fq_t, fk, fv_t, c_rows, c, row(lq1), row(lk1), row(lq2),
                              row(lk2), g_subln.reshape(-1, 1), lambda_init)

    h1, xn, gates, idx, rank, counts = _router(
        d_out.reshape(n, DIFF_WIDTH), f_out.reshape(n, FOX_WIDTH), h.reshape(n, d),
        w_out.astype(BF16), row(g_moe), w_router.T, b_router.reshape(-1, 1))

    te = EXPERT_TILE
    n_rows = n * TOP_K + N_EXPERTS * te
    counts = counts[:, 0].astype(I32)
    padded = (counts + te - 1) // te * te
    ends = jnp.cumsum(padded)
    starts = ends - padded
    experts = jnp.arange(N_EXPERTS, dtype=I32)[:, None, None]
    pos = rank[:TOP_K] + jnp.sum(jnp.where(idx[None, :TOP_K] == experts, starts[:, None, None], 0),
                                 axis=0)
    tm = TOKEN_TILE
    pos_flat = pos.reshape(TOP_K, n // tm, tm).transpose(1, 0, 2).reshape(-1)
    tile_start = jnp.arange(n_rows // te, dtype=I32) * te
    tile_expert = jnp.minimum(jnp.sum((ends[None, :] <= tile_start[:, None]).astype(I32), axis=1),
                              N_EXPERTS - 1)
    n_used = (ends[-1:] // te).astype(I32)
    owns = padded > 0
    later = lax.cummin(jnp.where(owns, jnp.arange(N_EXPERTS, dtype=I32), N_EXPERTS), reverse=True)
    following = jnp.concatenate([later[1:], jnp.full((1,), N_EXPERTS, I32)])
    following = jnp.where(following < N_EXPERTS, following, -1)
    next_expert = following[tile_expert]
    slot = ((jnp.cumsum(owns.astype(I32)) - 1) % 2)[tile_expert]

    xs = _dispatch(ends.astype(I32), pos_flat, xn, n_rows)
    y = _experts(tile_expert, n_used, next_expert.astype(I32), slot.astype(I32), xs, w_e1, b1, w_e2, b2)
    out = _combine(pos_flat, h1, gates, p.reshape(n, PLE_DIM), w_ple_gate.astype(BF16),
                   w_ple_proj.astype(BF16), row(g_ple), row(g_final), y, final_norm)
    return out.reshape(b, s, d)


def kernel(x, p, positions, g_attn, w_in, lambda_q1, lambda_k1, lambda_q2, lambda_k2, g_subln, b_forget, w_out, g_moe, w_router, b_router, w_e1, b_e1, w_e2, b_e2, g_ple, w_ple_gate, w_ple_proj, g_final):
    depth = g_attn.shape[0]
    h = x
    for i in range(depth):
        lambda_init = 0.8 - 0.6 * math.exp(-0.3 * i)
        h = _layer(h, p[i], positions, g_attn[i], w_in[i], lambda_q1[i], lambda_k1[i],
                   lambda_q2[i], lambda_k2[i], g_subln[i], b_forget[i], w_out[i], g_moe[i],
                   w_router[i], b_router[i], w_e1[i], b_e1[i], w_e2[i], b_e2[i], g_ple[i],
                   w_ple_gate[i], w_ple_proj[i], g_final, lambda_init, i == depth - 1)
    return h
```

```python
import functools
import math

import jax
import jax.numpy as jnp
from jax import lax
from jax.experimental import pallas as pl
from jax.experimental.pallas import tpu as pltpu
from jax.experimental.pallas import tpu_sc as plsc

F32 = jnp.float32
BF16 = jnp.bfloat16
I32 = jnp.int32

LANES = 128
D_MODEL = 1024
HEAD_DIM = 64
N_DIFF_HEADS = 4
DIFF_V_DIM = 2 * HEAD_DIM
N_FOX_HEADS = 8
DIFF_WIDTH = N_DIFF_HEADS * DIFF_V_DIM
FOX_WIDTH = N_FOX_HEADS * HEAD_DIM
ROT_DIM = HEAD_DIM // 4
ROPE_THETA = 500000.0
N_EXPERTS = 32
TOP_K = 4
D_EXPERT = D_MODEL
SWIGLU_ALPHA = 1.702
SWIGLU_LIMIT = 7.0
PLE_DIM = 256
NORM_EPS = 1e-5
LOG2_E = math.log2(math.e)
MAIN_WIDTH = 3 * DIFF_WIDTH + 3 * FOX_WIDTH

SEQ_TILE = 256
ATTN_TILE = 256
ATTN_UNITS = 2
TOKEN_TILE = 256
EXPERT_TILE = 256
ROW_GROUP = 8
VMEM_LIMIT = 48 * 1024 * 1024
EXPERT_VMEM_LIMIT = 58 * 1024 * 1024


def _params(*semantics):
    return pltpu.CompilerParams(dimension_semantics=semantics, vmem_limit_bytes=VMEM_LIMIT)


def _rms(x, g):
    return x * lax.rsqrt(jnp.mean(x * x, axis=-1, keepdims=True) + NORM_EPS) * g


def _inproj_kernel(pos_ref, x_ref, g_ref, w_ref, wfzt_ref, bfzt_ref,
                   dqt_ref, dk_ref, dvt_ref, fqt_ref, fk_ref, fvt_ref, c_ref, ct_ref, carry_t_ref):
    ts = x_ref.shape[1]
    xb = _rms(x_ref[0], g_ref[...]).astype(BF16)
    proj = jnp.dot(xb, w_ref[...], preferred_element_type=F32)
    nt = (((1,), (1,)), ((), ()))
    tn = (((0,), (0,)), ((), ()))

    def pieces(v):
        hi = v.astype(BF16)
        r1 = v - hi.astype(F32)
        mid = r1.astype(BF16)
        return hi, mid, (r1 - mid.astype(F32)).astype(BF16)

    def rows_to_lanes(v, select):
        return sum(lax.dot_general(part, select, tn, preferred_element_type=F32) for part in pieces(v))

    half_rot = ROT_DIM // 2
    freq = lax.broadcasted_iota(I32, (half_rot, 1), 0)
    inv_freq = jnp.power(ROPE_THETA, -(2 * freq).astype(F32) / ROT_DIM)
    ang = inv_freq * pos_ref[0].astype(F32)
    trig = jnp.concatenate([jnp.cos(ang), jnp.sin(ang)], axis=0)
    r = lax.broadcasted_iota(I32, (2 * half_rot, 2 * LANES), 0)
    lane2 = lax.broadcasted_iota(I32, (2 * half_rot, 2 * LANES), 1)
    d = lane2 % HEAD_DIM
    same_freq = jnp.where(d % half_rot == r % half_rot, 1.0, 0.0)
    cos_part = jnp.where((r < half_rot) & (lane2 < LANES) & (d < ROT_DIM), same_freq, 0.0)
    sin_part = jnp.where((r >= half_rot) & (lane2 >= LANES) & (d < ROT_DIM),
                         jnp.where(d < half_rot, -same_freq, same_freq), 0.0)
    tables = rows_to_lanes(trig, (cos_part + sin_part).astype(BF16))
    lane = lax.broadcasted_iota(I32, (1, LANES), 1) % HEAD_DIM
    cosf = jnp.where(lane < ROT_DIM, tables[:, :LANES], 1.0)
    sinf = tables[:, LANES:]
    first_half = lane < half_rot

    def rope(t):
        outs = []
        for j in range(t.shape[1] // LANES):
            tj = t[:, j * LANES:(j + 1) * LANES]
            partner = jnp.where(first_half,
                                pltpu.roll(tj, LANES - ROT_DIM // 2, 1),
                                pltpu.roll(tj, ROT_DIM // 2, 1))
            outs.append(tj * cosf + partner * sinf)
        return jnp.concatenate(outs, axis=1)

    scale = HEAD_DIM ** -0.5 * LOG2_E
    w = DIFF_WIDTH
    dqt_ref[0] = (rope(proj[:, 0:w]) * scale).astype(BF16).T
    dk_ref[0] = rope(proj[:, w:2 * w]).astype(BF16)
    dvt_ref[0, 0] = proj[:, 2 * w:3 * w].astype(BF16).T
    fqt_ref[0] = (proj[:, 3 * w:4 * w] * scale).astype(BF16).T
    fk_ref[0] = proj[:, 4 * w:5 * w].astype(BF16)
    fvt_ref[0, 0] = proj[:, 5 * w:6 * w].astype(BF16).T

    def log_sigmoid(z):
        return jnp.minimum(z, 0.0) - jnp.log1p(jnp.exp(-jnp.abs(z)))

    row = lax.broadcasted_iota(I32, (ts, ts), 0)
    col = lax.broadcasted_iota(I32, (ts, ts), 1)
    upto_col = jnp.where(row <= col, 1.0, 0.0).astype(BF16)

    @pl.when(pl.program_id(1) == 0)
    def _():
        carry_t_ref[...] = jnp.zeros_like(carry_t_ref)

    fzt = lax.dot_general(wfzt_ref[...], xb, nt, preferred_element_type=F32) + bfzt_ref[...]
    ct = carry_t_ref[...] + sum(jnp.dot(part, upto_col, preferred_element_type=F32)
                                for part in pieces(log_sigmoid(fzt)))
    carry_t_ref[...] = ct[:, ts - 1:ts]
    ct = ct * LOG2_E
    ct_ref[0] = ct[0:N_FOX_HEADS, :]
    head = lax.broadcasted_iota(I32, (ct.shape[0], FOX_WIDTH), 0)
    group = lax.broadcasted_iota(I32, (ct.shape[0], FOX_WIDTH), 1) // HEAD_DIM
    c_ref[0] = rows_to_lanes(ct, jnp.where(head == group, 1.0, 0.0).astype(BF16))


def _inproj(positions, x, g, w_main, w_fz_t, b_fz_t):
    b, s, d = x.shape
    ts = SEQ_TILE
    assert ts == ATTN_TILE
    tile = lambda width: pl.BlockSpec((1, ts, width), lambda i, j: (i, j, 0))
    tile_t = pl.BlockSpec((1, DIFF_WIDTH, ts), lambda i, j: (i, 0, j))
    key_tile_t = pl.BlockSpec((1, 1, DIFF_WIDTH, ts), lambda i, j: (i, j, 0, 0))
    full = lambda arr: pl.BlockSpec(arr.shape, lambda i, j: (0,) * arr.ndim)
    rows = jax.ShapeDtypeStruct((b, s, DIFF_WIDTH), BF16)
    rows_t = jax.ShapeDtypeStruct((b, DIFF_WIDTH, s), BF16)
    key_tiles_t = jax.ShapeDtypeStruct((b, s // ts, DIFF_WIDTH, ts), BF16)
    return pl.pallas_call(
        _inproj_kernel,
        grid=(b, s // ts),
        in_specs=[pl.BlockSpec((1, 1, ts), lambda i, j: (i, 0, j)), tile(d), full(g), full(w_main),
                  full(w_fz_t), full(b_fz_t)],
        out_specs=[tile_t, tile(DIFF_WIDTH), key_tile_t, tile_t, tile(FOX_WIDTH), key_tile_t,
                   tile(FOX_WIDTH), pl.BlockSpec((1, N_FOX_HEADS, ts), lambda i, j: (i, 0, j))],
        out_shape=[rows_t, rows, key_tiles_t, rows_t, rows, key_tiles_t,
                   jax.ShapeDtypeStruct((b, s, FOX_WIDTH), F32),
                   jax.ShapeDtypeStruct((b, N_FOX_HEADS, s), F32)],
        scratch_shapes=[pltpu.VMEM((w_fz_t.shape[0], 1), F32)],
        compiler_params=_params("arbitrary", "arbitrary"),
        name="inproj",
    )(positions.reshape(b, 1, s), x, g, w_main, w_fz_t, b_fz_t)


ONES_ROWS = 16


def _block_diag_queries(qt):
    row = lax.broadcasted_iota(I32, qt.shape, 0)
    zero = jnp.zeros_like(qt)
    return jnp.concatenate([jnp.where(row < HEAD_DIM, qt, zero),
                            jnp.where(row >= HEAD_DIM, qt, zero)], axis=1)


class _Chain:
    def __init__(self, qbd, keys, values_t, scratch, key_bias=None, query_bias=None):
        self.qbd, self.keys, self.values_t = qbd, keys, values_t
        self.acc_ref, self.s_ref, self.p_ref, self.stat_ref = scratch
        self.key_bias, self.query_bias = key_bias, query_bias


_RUNNING_MAX, _TILE_MAX, _RESCALE = 0, 1, 2


def _causal_attention(chains, qi):
    tk, tq = chains[0].s_ref.shape[0], chains[0].s_ref.shape[1] // 2
    ones = jnp.ones((ONES_ROWS, tk), BF16)

    def scores(ch, i, diagonal):
        s = jnp.dot(ch.keys(i), ch.qbd, preferred_element_type=F32)
        if ch.key_bias is not None:
            s = s - ch.key_bias(i)
        if diagonal is not False:
            r = lax.broadcasted_iota(I32, s.shape, 0)
            c = lax.broadcasted_iota(I32, s.shape, 1) & (tq - 1)
            visible = r <= c if diagonal is True else jnp.logical_or(r <= c, jnp.logical_not(diagonal))
            s = jnp.where(visible, s, -jnp.inf)
        ch.s_ref[...] = s
        top = jnp.max(s, axis=0, keepdims=True)
        if ch.query_bias is not None:
            top = top + ch.query_bias
        ch.stat_ref[_TILE_MAX:_TILE_MAX + 1, :] = top

    def probs(ch):
        m = ch.stat_ref[_RUNNING_MAX:_RUNNING_MAX + 1, :]
        m_new = jnp.maximum(m, ch.stat_ref[_TILE_MAX:_TILE_MAX + 1, :])
        ch.stat_ref[_RUNNING_MAX:_RUNNING_MAX + 1, :] = m_new
        ch.stat_ref[_RESCALE:_RESCALE + 1, :] = jnp.exp2(m - m_new)
        ref = m_new if ch.query_bias is None else m_new - ch.query_bias
        ch.p_ref[...] = jnp.exp2(ch.s_ref[...] - ref).astype(BF16)

    def accumulate(ch, i):
        vts = jnp.concatenate([ch.values_t(i), ones], axis=0)
        ch.acc_ref[...] = (ch.stat_ref[_RESCALE:_RESCALE + 1, :] * ch.acc_ref[...]
                           + jnp.dot(vts, ch.p_ref[...], preferred_element_type=F32))

    def body(k, diagonal_next):
        for ch in chains:
            accumulate(ch, jnp.maximum(k - 1, 0))
        for ch in chains:
            probs(ch)
        for ch in chains:
            scores(ch, k + 1, diagonal_next)

    for ch in chains:
        ch.acc_ref[...] = jnp.zeros_like(ch.acc_ref)
        ch.p_ref[...] = jnp.zeros_like(ch.p_ref)
        ch.stat_ref[...] = jnp.concatenate(
            [jnp.full((1, 2 * tq), -jnp.inf, F32), jnp.ones((ch.stat_ref.shape[0] - 1, 2 * tq), F32)], axis=0)
        scores(ch, 0, qi == 0)

    def loop_body(k, carry):
        body(k, False)
        return carry

    lax.fori_loop(0, qi - 1, loop_body, 0)

    @pl.when(qi >= 1)
    def _():
        body(qi - 1, True)

    for ch in chains:
        accumulate(ch, jnp.maximum(qi - 1, 0))
    for ch in chains:
        probs(ch)
    for ch in chains:
        accumulate(ch, qi)


def _attention_kernel(lq1_ref, lk1_ref, lq2_ref, lk2_ref, g_ref,
                      dqt_ref, dk_ref, dvt_ref, fqt_ref, fk_ref, fvt_ref, cq_ref, ck_ref,
                      do_ref, fo_ref, *scratch, lambda_init):
    tq = dqt_ref.shape[2]
    tk = dvt_ref.shape[3]
    units = dqt_ref.shape[1] // LANES
    lane = lax.broadcasted_iota(I32, (1, LANES), 1)
    scratch = [scratch[i:i + 4] for i in range(0, len(scratch), 4)]

    def unit_chains(u):
        lanes = slice(u * LANES, (u + 1) * LANES)

        def tile(ref):
            return lambda i: ref[0, pl.ds(pl.multiple_of(i * tk, tk), tk), lanes]

        def tile_t(ref):
            return lambda i: ref[0, i, lanes, :]

        cq = jnp.concatenate([cq_ref[0, u, 0:1, :], cq_ref[0, u, 1:2, :]], axis=1)

        def key_bias(i):
            ck = tile(ck_ref)(i)
            other = pltpu.roll(ck, HEAD_DIM, 1)
            first = jnp.where(lane < HEAD_DIM, ck, other)
            second = jnp.where(lane >= HEAD_DIM, ck, other)
            return jnp.concatenate([first] * (tq // LANES) + [second] * (tq // LANES), axis=1)

        return [_Chain(_block_diag_queries(dqt_ref[0, lanes, :]), tile(dk_ref), tile_t(dvt_ref),
                       scratch[2 * u]),
                _Chain(_block_diag_queries(fqt_ref[0, lanes, :]), tile(fk_ref), tile_t(fvt_ref),
                       scratch[2 * u + 1], key_bias, cq)]

    chains = [ch for u in range(units) for ch in unit_chains(u)]
    _causal_attention(chains, pl.program_id(2))

    lam = (jnp.exp(jnp.sum(lq1_ref[...] * lk1_ref[...], axis=-1, keepdims=True))
           - jnp.exp(jnp.sum(lq2_ref[...] * lk2_ref[...], axis=-1, keepdims=True)) + lambda_init)
    dv, h = DIFF_V_DIM, HEAD_DIM
    for u in range(units):
        lanes = slice(u * LANES, (u + 1) * LANES)
        acc = chains[2 * u].acc_ref
        o = (acc[0:dv, 0:tq] / acc[dv:dv + 1, 0:tq]
             - lam * (acc[0:dv, tq:] / acc[dv:dv + 1, tq:]))
        ms = jnp.mean(o * o, axis=0, keepdims=True)
        do_ref[0, :, lanes] = (o * lax.rsqrt(ms + NORM_EPS) * g_ref[...]
                               * (1.0 - lambda_init)).T.astype(BF16)
        acc = chains[2 * u + 1].acc_ref
        fo_ref[0, :, lanes] = jnp.concatenate([acc[0:h, 0:tq] / acc[2 * h:2 * h + 1, 0:tq],
                                               acc[h:2 * h, tq:] / acc[2 * h:2 * h + 1, tq:]],
                                              axis=0).T.astype(BF16)


def _attention(dqt, dk, dvt, fqt, fk, fvt, c_rows, c_cols, lq1, lk1, lq2, lk2, g_col, lambda_init):
    b, s, _ = dk.shape
    t = ATTN_TILE
    assert N_DIFF_HEADS == N_FOX_HEADS // 2 and N_DIFF_HEADS % ATTN_UNITS == 0
    width = ATTN_UNITS * LANES
    vec = lambda arr: pl.BlockSpec(arr.shape, lambda i, h, j: (0, 0))
    q_spec = pl.BlockSpec((1, width, t), lambda i, h, j: (i, h, j))
    k_spec = pl.BlockSpec((1, s, width), lambda i, h, j: (i, 0, h))
    v_spec = pl.BlockSpec((1, s // t, width, t), lambda i, h, j: (i, 0, h, 0))
    o_spec = pl.BlockSpec((1, t, width), lambda i, h, j: (i, j, h))
    out = jax.ShapeDtypeStruct((b, s, DIFF_WIDTH), BF16)
    chain_scratch = [pltpu.VMEM((LANES + ONES_ROWS, 2 * t), F32),
                     pltpu.VMEM((t, 2 * t), F32),
                     pltpu.VMEM((t, 2 * t), BF16),
                     pltpu.VMEM((8, 2 * t), F32)]
    return pl.pallas_call(
        functools.partial(_attention_kernel, lambda_init=lambda_init),
        grid=(b, N_DIFF_HEADS // ATTN_UNITS, s // t),
        in_specs=[vec(lq1), vec(lk1), vec(lq2), vec(lk2), vec(g_col),
                  q_spec, k_spec, v_spec, q_spec, k_spec, v_spec,
                  pl.BlockSpec((1, ATTN_UNITS, 2, t), lambda i, h, j: (i, h, 0, j)), k_spec],
        out_specs=[o_spec, o_spec],
        out_shape=[out, out],
        scratch_shapes=chain_scratch * (2 * ATTN_UNITS),
        compiler_params=_params("arbitrary", "arbitrary", "arbitrary"),
        name="attention",
    )(lq1, lk1, lq2, lk2, g_col, dqt, dk, dvt, fqt, fk, fvt, c_rows, c_cols)


def _router_kernel(d_ref, f_ref, x_ref, wo_ref, g_ref, wrt_ref, brt_ref,
                   h_ref, xn_ref, gate_ref, idx_ref, rank_ref, count_ref, carry_ref):
    tm = x_ref.shape[0]
    mixed = (jnp.dot(d_ref[...], wo_ref[0:DIFF_WIDTH, :], preferred_element_type=F32)
             + jnp.dot(f_ref[...], wo_ref[DIFF_WIDTH:, :], preferred_element_type=F32))
    h = x_ref[...] + mixed
    h_ref[...] = h
    xn = _rms(h, g_ref[...])
    xn_ref[...] = xn

    def pieces(v, count):
        out = []
        for _ in range(count):
            part = v.astype(BF16)
            out.append(part)
            v = v - part.astype(F32)
        return out

    nt = (((1,), (1,)), ((), ()))
    xh, xl = pieces(xn, 2)
    wh, wl = pieces(wrt_ref[...], 2)
    work = (lax.dot_general(wh, xh, nt, preferred_element_type=F32)
            + lax.dot_general(wh, xl, nt, preferred_element_type=F32)
            + lax.dot_general(wl, xh, nt, preferred_element_type=F32)) + brt_ref[...]
    expert = lax.broadcasted_iota(I32, work.shape, 0)

    vals, ids, picks = [], [], []
    for _ in range(TOP_K):
        m = jnp.max(work, axis=0, keepdims=True)
        idx = jnp.min(jnp.where(work == m, expert, N_EXPERTS), axis=0, keepdims=True)
        pick = expert == idx
        vals.append(m)
        ids.append(idx)
        picks.append(pick)
        work = jnp.where(pick, -jnp.inf, work)
    exps = [jnp.exp(v - vals[0]) for v in vals]
    denom = exps[0] + exps[1] + exps[2] + exps[3]

    chosen = jnp.where(picks[0] | picks[1] | picks[2] | picks[3], 1.0, 0.0)
    row = lax.broadcasted_iota(I32, (tm, tm), 0)
    col = lax.broadcasted_iota(I32, (tm, tm), 1)
    earlier = jnp.where(row < col, 1.0, 0.0).astype(BF16)

    @pl.when(pl.program_id(0) == 0)
    def _():
        carry_ref[...] = jnp.zeros_like(carry_ref)

    rank = jnp.dot(chosen.astype(BF16), earlier, preferred_element_type=F32) + carry_ref[...]
    carry_ref[...] = carry_ref[...] + jnp.sum(chosen, axis=1, keepdims=True)
    count_ref[...] = jnp.broadcast_to(carry_ref[...], count_ref.shape)

    slot = lax.broadcasted_iota(I32, (8, tm), 0)
    gate_t = jnp.zeros((8, tm), F32)
    idx_t = jnp.zeros((8, tm), I32)
    rank_t = jnp.zeros((8, tm), F32)
    for k in range(TOP_K):
        gate_t = jnp.where(slot == k, exps[k] / denom, gate_t)
        idx_t = jnp.where(slot == k, ids[k], idx_t)
        rank_t = jnp.where(slot == k, jnp.sum(jnp.where(picks[k], rank, 0.0), axis=0, keepdims=True), rank_t)
    idx_ref[...] = idx_t
    rank_ref[...] = rank_t.astype(I32)
    gate_rows = jnp.concatenate([gate_t, jnp.zeros((LANES - 8, tm), F32)], axis=0)
    eye = jnp.where(row == col, 1.0, 0.0).astype(BF16)
    gate_ref[...] = sum(lax.dot_general(eye, part, nt, preferred_element_type=F32)
                        for part in pieces(gate_rows, 3))


def _router(d_out, f_out, x, w_out, g_moe, w_router_t, b_router_t):
    n, d = x.shape
    tm = TOKEN_TILE
    tile = lambda width: pl.BlockSpec((tm, width), lambda i: (i, 0))
    rows = pl.BlockSpec((8, tm), lambda i: (0, i))
    full = lambda arr: pl.BlockSpec(arr.shape, lambda i: (0,) * arr.ndim)
    return pl.pallas_call(
        _router_kernel,
        grid=(n // tm,),
        in_specs=[tile(DIFF_WIDTH), tile(FOX_WIDTH), tile(d), full(w_out), full(g_moe),
                  full(w_router_t), full(b_router_t)],
        out_specs=[tile(d), tile(d), tile(LANES), rows, rows,
                   pl.BlockSpec((N_EXPERTS, LANES), lambda i: (0, 0))],
        out_shape=[jax.ShapeDtypeStruct((n, d), F32), jax.ShapeDtypeStruct((n, d), F32),
                   jax.ShapeDtypeStruct((n, LANES), F32), jax.ShapeDtypeStruct((8, n), I32),
                   jax.ShapeDtypeStruct((8, n), I32), jax.ShapeDtypeStruct((N_EXPERTS, LANES), F32)],
        scratch_shapes=[pltpu.VMEM((N_EXPERTS, 1), F32)],
        compiler_params=_params("arbitrary"),
        name="router",
    )(d_out, f_out, x, w_out, g_moe, w_router_t, b_router_t)


def _dispatch_kernel(ends_ref, pos_ref, x_ref, xs_ref, zero_ref, sem):
    tm = x_ref.shape[0]
    te = zero_ref.shape[0]

    @pl.when(pl.program_id(0) == 0)
    def _():
        zero_ref[...] = jnp.zeros_like(zero_ref)

        def zero_tile(first_row):
            copy = pltpu.make_async_copy(zero_ref, xs_ref.at[pl.ds(pl.multiple_of(first_row, te), te), :], sem)
            copy.start()
            copy.wait()

        for e in range(N_EXPERTS):
            end = ends_ref[e]
            start = ends_ref[e - 1] if e else 0
            pl.when(end > start)(functools.partial(zero_tile, end - te))

        def unused(t, carry):
            zero_tile(t * te)
            return carry

        lax.fori_loop(ends_ref[N_EXPERTS - 1] // te, xs_ref.shape[0] // te, unused, 0)

    def issue(g, carry):
        base = pl.multiple_of(g * ROW_GROUP, ROW_GROUP)
        for j in range(ROW_GROUP):
            for k in range(TOP_K):
                dst = pos_ref[k * tm + base + j]
                pltpu.make_async_copy(x_ref.at[pl.ds(base + j, 1), :], xs_ref.at[pl.ds(dst, 1), :],
                                      sem).start(priority=k % 2)
        return carry

    lax.fori_loop(0, tm // ROW_GROUP, issue, 0)
    for _ in range(TOP_K):
        pltpu.make_async_copy(x_ref, xs_ref.at[pl.ds(0, tm), :], sem).wait()


def _dispatch(ends, pos_flat, xn, n_rows):
    n, d = xn.shape
    tm = TOKEN_TILE
    return pl.pallas_call(
        _dispatch_kernel,
        grid_spec=pltpu.PrefetchScalarGridSpec(
            num_scalar_prefetch=1,
            grid=(n // tm,),
            in_specs=[pl.BlockSpec((TOP_K * tm,), lambda i, ends: (i,), memory_space=pltpu.SMEM),
                      pl.BlockSpec((tm, d), lambda i, ends: (i, 0))],
            out_specs=pl.BlockSpec(memory_space=pl.ANY),
            scratch_shapes=[pltpu.VMEM((EXPERT_TILE, d), F32), pltpu.SemaphoreType.DMA(())]),
        out_shape=jax.ShapeDtypeStruct((n_rows, d), F32),
        compiler_params=_params("arbitrary"),
        name="dispatch",
    )(ends, pos_flat, xn)


SC_WINDOW = 128
SC_COLUMNS = 256


def _dispatch_sparsecore(pos, xn, n_rows):
    n, d = xn.shape
    windows = n // SC_WINDOW
    mesh = plsc.VectorSubcoreMesh(core_axis_name="core", subcore_axis_name="subcore")

    @functools.partial(pl.kernel, out_type=jax.ShapeDtypeStruct((n_rows, d), F32), mesh=mesh,
                       scratch_types=[], name="dispatch_sc")
    def scatter(x_hbm, pos_hbm, xs_hbm):
        for c in range(d // SC_COLUMNS):
            def body(x_vmem, pos_vmem, c=c):
                pltpu.sync_copy(x_vmem, xs_hbm.at[:, pl.ds(c * SC_COLUMNS, SC_COLUMNS)].at[pos_vmem.at[0]])

            pltpu.emit_pipeline(
                body, grid=(TOP_K * windows,),
                in_specs=[pl.BlockSpec((SC_WINDOW, SC_COLUMNS), index_map=lambda i, c=c: (i % windows, c)),
                          pl.BlockSpec((1, SC_WINDOW), index_map=lambda i: (0, i))],
                out_specs=[], core_axis_name=("core", "subcore"),
                dimension_semantics=(pltpu.PARALLEL,))(x_hbm, pos_hbm)

    return scatter(xn, pos.reshape(1, -1))


def _expert_kernel(tile_expert_ref, n_used_ref, next_expert_ref, slot_ref,
                   x_ref, w1_ref, b1_ref, w2_ref, b2_ref, y_ref,
                   w1f_ref, w2f_ref, sem_ref, w1b_ref, w2p_ref, w2b_ref):
    t = pl.program_id(0)
    fresh = (t == 0) | (tile_expert_ref[t] != tile_expert_ref[jnp.maximum(t - 1, 0)])
    used = t < n_used_ref[0]

    def weight_copies(e, s):
        return (pltpu.make_async_copy(w1_ref.at[e], w1f_ref.at[s], sem_ref.at[0, s]),
                pltpu.make_async_copy(w2_ref.at[e], w2f_ref.at[s], sem_ref.at[1, s]))

    @pl.when(t == 0)
    def _():
        for copy in weight_copies(tile_expert_ref[0], slot_ref[0]):
            copy.start()

    @pl.when(fresh & used)
    def _():
        s = slot_ref[t]
        for copy in weight_copies(tile_expert_ref[t], s):
            copy.wait()

        @pl.when(next_expert_ref[t] >= 0)
        def _():
            for copy in weight_copies(next_expert_ref[t], 1 - s):
                copy.start()

        rows = LANES
        half = LANES // 2
        for r in range(0, w1b_ref.shape[0], rows):
            w1b_ref[r:r + rows, :] = w1f_ref[s, r:r + rows, :].astype(BF16)
        for c in range(w2p_ref.shape[0]):
            lanes = slice(c * LANES, (c + 1) * LANES)
            for r in range(0, w2b_ref.shape[0], rows):
                for b in range(2):
                    w2p_ref[c, pl.ds(r + b, half, stride=2), :] = (
                        w2f_ref[s, r + b * half:r + (b + 1) * half, lanes])
            w2b_ref[:, lanes] = w2p_ref[c].astype(BF16)

    @pl.when(used)
    def _():
        xb = x_ref[...].astype(BF16)
        h = jnp.dot(xb, w1b_ref[...], preferred_element_type=F32) + b1_ref[0]
        even = lax.broadcasted_iota(I32, (1, LANES), 1) % 2 == 0
        parts = []
        for q in range(h.shape[1] // (2 * LANES)):
            a = h[:, (2 * q) * LANES:(2 * q + 1) * LANES]
            b = h[:, (2 * q + 1) * LANES:(2 * q + 2) * LANES]
            hg = jnp.where(even, a, pltpu.roll(b, 1, 1))
            hl = jnp.where(even, pltpu.roll(a, LANES - 1, 1), b)
            glu = jnp.minimum(hg, SWIGLU_LIMIT)
            lin = jnp.clip(hl, -SWIGLU_LIMIT, SWIGLU_LIMIT)
            parts.append((glu * jax.nn.sigmoid(SWIGLU_ALPHA * glu) * (lin + 1.0)).astype(BF16))
        act = jnp.concatenate(parts, axis=1)
        y_ref[...] = jnp.dot(act, w2b_ref[...], preferred_element_type=F32) + b2_ref[0]

    @pl.when(jnp.logical_not(used))
    def _():
        y_ref[...] = jnp.zeros_like(y_ref)


def _experts(tile_expert, n_used, next_expert, slot, xs, w1, b1, w2, b2):
    n_rows, d = xs.shape
    tm = EXPERT_TILE
    by_expert = lambda arr: pl.BlockSpec((1,) + arr.shape[1:], lambda i, te, *_: (te[i], 0, 0))
    rows = pl.BlockSpec((tm, d), lambda i, *_: (i, 0))
    used_rows = pl.BlockSpec((tm, d), lambda i, te, nu, *_: (jnp.minimum(i, nu[0] - 1), 0))
    hbm = pl.BlockSpec(memory_space=pl.ANY)
    return pl.pallas_call(
        _expert_kernel,
        grid_spec=pltpu.PrefetchScalarGridSpec(
            num_scalar_prefetch=4,
            grid=(n_rows // tm,),
            in_specs=[used_rows, hbm, by_expert(b1), hbm, by_expert(b2)],
            out_specs=rows,
            scratch_shapes=[pltpu.VMEM((2,) + w1.shape[1:], F32), pltpu.VMEM((2,) + w2.shape[1:], F32),
                            pltpu.SemaphoreType.DMA((2, 2)),
                            pltpu.VMEM(w1.shape[1:], BF16),
                            pltpu.VMEM((w2.shape[2] // LANES, w2.shape[1], LANES), F32),
                            pltpu.VMEM(w2.shape[1:], BF16)]),
        out_shape=jax.ShapeDtypeStruct(xs.shape, F32),
        compiler_params=pltpu.CompilerParams(dimension_semantics=("arbitrary",),
                                             vmem_limit_bytes=EXPERT_VMEM_LIMIT),
        name="experts",
    )(tile_expert, n_used, next_expert, slot, xs, w1, b1, w2, b2)


def _combine_kernel(pos_ref, h_ref, gate_ref, p_ref, wg_ref, wp_ref, gple_ref, gfin_ref, y_ref,
                    o_ref, ybuf, sem, *, final_norm):
    tm = h_ref.shape[0]

    def issue(g, carry):
        base = pl.multiple_of(g * ROW_GROUP, ROW_GROUP)
        for j in range(ROW_GROUP):
            for k in range(TOP_K):
                src = pos_ref[k * tm + base + j]
                pltpu.make_async_copy(y_ref.at[pl.ds(src, 1), :], ybuf.at[k, pl.ds(base + j, 1), :],
                                      sem).start(priority=k % 2)
        return carry

    lax.fori_loop(0, tm // ROW_GROUP, issue, 0)
    for k in range(TOP_K):
        pltpu.make_async_copy(y_ref.at[pl.ds(0, tm), :], ybuf.at[k], sem).wait()

    gates = gate_ref[...]
    moe = gates[:, 0:1] * ybuf[0]
    for k in range(1, TOP_K):
        moe = moe + gates[:, k:k + 1] * ybuf[k]
    h = h_ref[...] + moe
    gate = jax.nn.sigmoid(jnp.dot(_rms(h, gple_ref[...]).astype(BF16), wg_ref[...],
                                  preferred_element_type=F32))
    emb = jnp.dot(p_ref[...].astype(BF16), wp_ref[...], preferred_element_type=F32)
    h = h + gate * emb
    o_ref[...] = _rms(h, gfin_ref[...]) if final_norm else h


def _combine(pos_flat, h, gates, p, w_gate, w_proj, g_ple, g_final, y, final_norm):
    n, d = h.shape
    tm = TOKEN_TILE
    tile = lambda width: pl.BlockSpec((tm, width), lambda i: (i, 0))
    full = lambda arr: pl.BlockSpec(arr.shape, lambda i: (0,) * arr.ndim)
    return pl.pallas_call(
        functools.partial(_combine_kernel, final_norm=final_norm),
        grid=(n // tm,),
        in_specs=[pl.BlockSpec((TOP_K * tm,), lambda i: (i,), memory_space=pltpu.SMEM),
                  tile(d), tile(LANES), tile(p.shape[1]), full(w_gate), full(w_proj),
                  full(g_ple), full(g_final), pl.BlockSpec(memory_space=pl.ANY)],
        out_specs=tile(d),
        out_shape=jax.ShapeDtypeStruct((n, d), F32),
        scratch_shapes=[pltpu.VMEM((TOP_K, tm, d), F32), pltpu.SemaphoreType.DMA(())],
        compiler_params=_params("arbitrary"),
        name="combine",
    )(pos_flat, h, gates, p, w_gate, w_proj, g_ple, g_final, y)


def _layer(h, p, positions, g_attn, w_in, lq1, lk1, lq2, lk2, g_subln, b_forget, w_out, g_moe,
           w_router, b_router, w_e1, b_e1, w_e2, b_e2, g_ple, w_ple_gate, w_ple_proj, g_final,
           lambda_init, final_norm):
    b, s, d = h.shape
    n = b * s
    row = lambda v: v.reshape(1, -1)

    w_main = w_in[:, :MAIN_WIDTH].astype(BF16)
    pad_heads = ((0, 2 * N_FOX_HEADS - N_FOX_HEADS), (0, 0))
    w_fz_t = jnp.pad(w_in[:, MAIN_WIDTH:].T, pad_heads).astype(BF16)
    b_fz_t = jnp.pad(b_forget.reshape(-1, 1), pad_heads)
    b1 = b_e1[:, None, :]
    b2 = b_e2[:, None, :]

    dq_t, dk, dv_t, fq_t, fk, fv_t, c, c_t = _inproj(positions, h, row(g_attn), w_main, w_fz_t, b_fz_t)
    c_rows = c_t.reshape(b, N_FOX_HEADS // 2, 2, s)
    d_out, f_out = _attention(dq_t, dk, dv_t, fq_t, fk, fv_t, c_rows, c, row(lq1), row(lk1), row(lq2),
                              row(lk2), g_subln.reshape(-1, 1), lambda_init)

    h1, xn, gates, idx, rank, counts = _router(
        d_out.reshape(n, DIFF_WIDTH), f_out.reshape(n, FOX_WIDTH), h.reshape(n, d),
        w_out.astype(BF16), row(g_moe), w_router.T, b_router.reshape(-1, 1))

    te = EXPERT_TILE
    n_rows = n * TOP_K + N_EXPERTS * te
    counts = counts[:, 0].astype(I32)
    padded = (counts + te - 1) // te * te
    ends = jnp.cumsum(padded)
    starts = ends - padded
    experts = jnp.arange(N_EXPERTS, dtype=I32)[:, None, None]
    pos = rank[:TOP_K] + jnp.sum(jnp.where(idx[None, :TOP_K] == experts, starts[:, None, None], 0),
                                 axis=0)
    tm = TOKEN_TILE
    pos_flat = pos.reshape(TOP_K, n // tm, tm).transpose(1, 0, 2).reshape(-1)
    tile_start = jnp.arange(n_rows // te, dtype=I32) * te
    tile_expert = jnp.minimum(jnp.sum((ends[None, :] <= tile_start[:, None]).astype(I32), axis=1),
                              N_EXPERTS - 1)
    n_used = (ends[-1:] // te).astype(I32)
    owns = padded > 0
    later = lax.cummin(jnp.where(owns, jnp.arange(N_EXPERTS, dtype=I32), N_EXPERTS), reverse=True)
    following = jnp.concatenate([later[1:], jnp.full((1,), N_EXPERTS, I32)])
    following = jnp.where(following < N_EXPERTS, following, -1)
    next_expert = following[tile_expert]
    slot = ((jnp.cumsum(owns.astype(I32)) - 1) % 2)[tile_expert]

    xs = _dispatch_sparsecore(pos.astype(I32), xn, n_rows)
    y = _experts(tile_expert, n_used, next_expert.astype(I32), slot.astype(I32), xs, w_e1, b1, w_e2, b2)
    out = _combine(pos_flat, h1, gates, p.reshape(n, PLE_DIM), w_ple_gate.astype(BF16),
                   w_ple_proj.astype(BF16), row(g_ple), row(g_final), y, final_norm)
    return out.reshape(b, s, d)


def kernel(x, p, positions, g_attn, w_in, lambda_q1, lambda_k1, lambda_q2, lambda_k2, g_subln, b_forget, w_out, g_moe, w_router, b_router, w_e1, b_e1, w_e2, b_e2, g_ple, w_ple_gate, w_ple_proj, g_final):
    depth = g_attn.shape[0]
    h = x
    for i in range(depth):
        lambda_init = 0.8 - 0.6 * math.exp(-0.3 * i)
        h = _layer(h, p[i], positions, g_attn[i], w_in[i], lambda_q1[i], lambda_k1[i],
                   lambda_q2[i], lambda_k2[i], g_subln[i], b_forget[i], w_out[i], g_moe[i],
                   w_router[i], b_router[i], w_e1[i], b_e1[i], w_e2[i], b_e2[i], g_ple[i],
                   w_ple_gate[i], w_ple_proj[i], g_final, lambda_init, i == depth - 1)
    return h
```

```python
import functools
import math

import jax
import jax.numpy as jnp
from jax import lax
from jax.experimental import pallas as pl
from jax.experimental.pallas import tpu as pltpu
from jax.experimental.pallas import tpu_sc as plsc

F32 = jnp.float32
BF16 = jnp.bfloat16
I32 = jnp.int32

LANES = 128
D_MODEL = 1024
HEAD_DIM = 64
N_DIFF_HEADS = 4
DIFF_V_DIM = 2 * HEAD_DIM
N_FOX_HEADS = 8
DIFF_WIDTH = N_DIFF_HEADS * DIFF_V_DIM
FOX_WIDTH = N_FOX_HEADS * HEAD_DIM
ROT_DIM = HEAD_DIM // 4
ROPE_THETA = 500000.0
N_EXPERTS = 32
TOP_K = 4
D_EXPERT = D_MODEL
SWIGLU_ALPHA = 1.702
SWIGLU_LIMIT = 7.0
PLE_DIM = 256
NORM_EPS = 1e-5
LOG2_E = math.log2(math.e)
MAIN_WIDTH = 3 * DIFF_WIDTH + 3 * FOX_WIDTH

SEQ_TILE = 256
ATTN_TILE = 256
ATTN_UNITS = 2
TOKEN_TILE = 256
EXPERT_TILE = 256
ROW_GROUP = 8
VMEM_LIMIT = 48 * 1024 * 1024
EXPERT_VMEM_LIMIT = 58 * 1024 * 1024


def _params(*semantics):
    return pltpu.CompilerParams(dimension_semantics=semantics, vmem_limit_bytes=VMEM_LIMIT)


def _rms(x, g):
    return x * lax.rsqrt(jnp.mean(x * x, axis=-1, keepdims=True) + NORM_EPS) * g


def _inproj_kernel(pos_ref, x_ref, g_ref, w_ref, wfzt_ref, bfzt_ref,
                   dqt_ref, dk_ref, dvt_ref, fqt_ref, fk_ref, fvt_ref, c_ref, ct_ref, carry_t_ref):
    ts = x_ref.shape[1]
    xb = _rms(x_ref[0], g_ref[...]).astype(BF16)
    proj = jnp.dot(xb, w_ref[...], preferred_element_type=F32)
    nt = (((1,), (1,)), ((), ()))
    tn = (((0,), (0,)), ((), ()))

    def pieces(v):
        hi = v.astype(BF16)
        r1 = v - hi.astype(F32)
        mid = r1.astype(BF16)
        return hi, mid, (r1 - mid.astype(F32)).astype(BF16)

    def rows_to_lanes(v, select):
        return sum(lax.dot_general(part, select, tn, preferred_element_type=F32) for part in pieces(v))

    half_rot = ROT_DIM // 2
    freq = lax.broadcasted_iota(I32, (half_rot, 1), 0)
    inv_freq = jnp.power(ROPE_THETA, -(2 * freq).astype(F32) / ROT_DIM)
    ang = inv_freq * pos_ref[0].astype(F32)
    trig = jnp.concatenate([jnp.cos(ang), jnp.sin(ang)], axis=0)
    r = lax.broadcasted_iota(I32, (2 * half_rot, 2 * LANES), 0)
    lane2 = lax.broadcasted_iota(I32, (2 * half_rot, 2 * LANES), 1)
    d = lane2 % HEAD_DIM
    same_freq = jnp.where(d % half_rot == r % half_rot, 1.0, 0.0)
    cos_part = jnp.where((r < half_rot) & (lane2 < LANES) & (d < ROT_DIM), same_freq, 0.0)
    sin_part = jnp.where((r >= half_rot) & (lane2 >= LANES) & (d < ROT_DIM),
                         jnp.where(d < half_rot, -same_freq, same_freq), 0.0)
    tables = rows_to_lanes(trig, (cos_part + sin_part).astype(BF16))
    lane = lax.broadcasted_iota(I32, (1, LANES), 1) % HEAD_DIM
    cosf = jnp.where(lane < ROT_DIM, tables[:, :LANES], 1.0)
    sinf = tables[:, LANES:]
    first_half = lane < half_rot

    def rope(t):
        outs = []
        for j in range(t.shape[1] // LANES):
            tj = t[:, j * LANES:(j + 1) * LANES]
            partner = jnp.where(first_half,
                                pltpu.roll(tj, LANES - ROT_DIM // 2, 1),
                                pltpu.roll(tj, ROT_DIM // 2, 1))
            outs.append(tj * cosf + partner * sinf)
        return jnp.concatenate(outs, axis=1)

    scale = HEAD_DIM ** -0.5 * LOG2_E
    w = DIFF_WIDTH
    dqt_ref[0] = (rope(proj[:, 0:w]) * scale).astype(BF16).T
    dk_ref[0] = rope(proj[:, w:2 * w]).astype(BF16)
    dvt_ref[0, 0] = proj[:, 2 * w:3 * w].astype(BF16).T
    fqt_ref[0] = (proj[:, 3 * w:4 * w] * scale).astype(BF16).T
    fk_ref[0] = proj[:, 4 * w:5 * w].astype(BF16)
    fvt_ref[0, 0] = proj[:, 5 * w:6 * w].astype(BF16).T

    def log_sigmoid(z):
        return jnp.minimum(z, 0.0) - jnp.log1p(jnp.exp(-jnp.abs(z)))

    row = lax.broadcasted_iota(I32, (ts, ts), 0)
    col = lax.broadcasted_iota(I32, (ts, ts), 1)
    upto_col = jnp.where(row <= col, 1.0, 0.0).astype(BF16)

    @pl.when(pl.program_id(1) == 0)
    def _():
        carry_t_ref[...] = jnp.zeros_like(carry_t_ref)

    fzt = lax.dot_general(wfzt_ref[...], xb, nt, preferred_element_type=F32) + bfzt_ref[...]
    ct = carry_t_ref[...] + sum(jnp.dot(part, upto_col, preferred_element_type=F32)
                                for part in pieces(log_sigmoid(fzt)))
    carry_t_ref[...] = ct[:, ts - 1:ts]
    ct = ct * LOG2_E
    ct_ref[0] = ct[0:N_FOX_HEADS, :]
    head = lax.broadcasted_iota(I32, (ct.shape[0], FOX_WIDTH), 0)
    group = lax.broadcasted_iota(I32, (ct.shape[0], FOX_WIDTH), 1) // HEAD_DIM
    c_ref[0] = rows_to_lanes(ct, jnp.where(head == group, 1.0, 0.0).astype(BF16))


def _inproj(positions, x, g, w_main, w_fz_t, b_fz_t):
    b, s, d = x.shape
    ts = SEQ_TILE
    assert ts == ATTN_TILE
    tile = lambda width: pl.BlockSpec((1, ts, width), lambda i, j: (i, j, 0))
    tile_t = pl.BlockSpec((1, DIFF_WIDTH, ts), lambda i, j: (i, 0, j))
    key_tile_t = pl.BlockSpec((1, 1, DIFF_WIDTH, ts), lambda i, j: (i, j, 0, 0))
    full = lambda arr: pl.BlockSpec(arr.shape, lambda i, j: (0,) * arr.ndim)
    rows = jax.ShapeDtypeStruct((b, s, DIFF_WIDTH), BF16)
    rows_t = jax.ShapeDtypeStruct((b, DIFF_WIDTH, s), BF16)
    key_tiles_t = jax.ShapeDtypeStruct((b, s // ts, DIFF_WIDTH, ts), BF16)
    return pl.pallas_call(
        _inproj_kernel,
        grid=(b, s // ts),
        in_specs=[pl.BlockSpec((1, 1, ts), lambda i, j: (i, 0, j)), tile(d), full(g), full(w_main),
                  full(w_fz_t), full(b_fz_t)],
        out_specs=[tile_t, tile(DIFF_WIDTH), key_tile_t, tile_t, tile(FOX_WIDTH), key_tile_t,
                   tile(FOX_WIDTH), pl.BlockSpec((1, N_FOX_HEADS, ts), lambda i, j: (i, 0, j))],
        out_shape=[rows_t, rows, key_tiles_t, rows_t, rows, key_tiles_t,
                   jax.ShapeDtypeStruct((b, s, FOX_WIDTH), F32),
                   jax.ShapeDtypeStruct((b, N_FOX_HEADS, s), F32)],
        scratch_shapes=[pltpu.VMEM((w_fz_t.shape[0], 1), F32)],
        compiler_params=_params("arbitrary", "arbitrary"),
        name="inproj",
    )(positions.reshape(b, 1, s), x, g, w_main, w_fz_t, b_fz_t)


ONES_ROWS = 16


def _block_diag_queries(qt):
    row = lax.broadcasted_iota(I32, qt.shape, 0)
    zero = jnp.zeros_like(qt)
    return jnp.concatenate([jnp.where(row < HEAD_DIM, qt, zero),
                            jnp.where(row >= HEAD_DIM, qt, zero)], axis=1)


class _Chain:
    def __init__(self, qbd, keys, values_t, scratch, key_bias=None, query_bias=None):
        self.qbd, self.keys, self.values_t = qbd, keys, values_t
        self.acc_ref, self.s_ref, self.p_ref, self.stat_ref = scratch
        self.key_bias, self.query_bias = key_bias, query_bias


_RUNNING_MAX, _TILE_MAX, _RESCALE = 0, 1, 2


def _causal_attention(chains, qi):
    tk, tq = chains[0].s_ref.shape[0], chains[0].s_ref.shape[1] // 2
    ones = jnp.ones((ONES_ROWS, tk), BF16)

    def scores(ch, i, diagonal):
        s = jnp.dot(ch.keys(i), ch.qbd, preferred_element_type=F32)
        if ch.key_bias is not None:
            s = s - ch.key_bias(i)
        if diagonal is not False:
            r = lax.broadcasted_iota(I32, s.shape, 0)
            c = lax.broadcasted_iota(I32, s.shape, 1) & (tq - 1)
            visible = r <= c if diagonal is True else jnp.logical_or(r <= c, jnp.logical_not(diagonal))
            s = jnp.where(visible, s, -jnp.inf)
        ch.s_ref[...] = s
        top = jnp.max(s, axis=0, keepdims=True)
        if ch.query_bias is not None:
            top = top + ch.query_bias
        ch.stat_ref[_TILE_MAX:_TILE_MAX + 1, :] = top

    def probs(ch):
        m = ch.stat_ref[_RUNNING_MAX:_RUNNING_MAX + 1, :]
        m_new = jnp.maximum(m, ch.stat_ref[_TILE_MAX:_TILE_MAX + 1, :])
        ch.stat_ref[_RUNNING_MAX:_RUNNING_MAX + 1, :] = m_new
        ch.stat_ref[_RESCALE:_RESCALE + 1, :] = jnp.exp2(m - m_new)
        ref = m_new if ch.query_bias is None else m_new - ch.query_bias
        ch.p_ref[...] = jnp.exp2(ch.s_ref[...] - ref).astype(BF16)

    def accumulate(ch, i):
        vts = jnp.concatenate([ch.values_t(i), ones], axis=0)
        ch.acc_ref[...] = (ch.stat_ref[_RESCALE:_RESCALE + 1, :] * ch.acc_ref[...]
                           + jnp.dot(vts, ch.p_ref[...], preferred_element_type=F32))

    def body(k, diagonal_next):
        for ch in chains:
            accumulate(ch, jnp.maximum(k - 1, 0))
        for ch in chains:
            probs(ch)
        for ch in chains:
            scores(ch, k + 1, diagonal_next)

    for ch in chains:
        ch.acc_ref[...] = jnp.zeros_like(ch.acc_ref)
        ch.p_ref[...] = jnp.zeros_like(ch.p_ref)
        ch.stat_ref[...] = jnp.concatenate(
            [jnp.full((1, 2 * tq), -jnp.inf, F32), jnp.ones((ch.stat_ref.shape[0] - 1, 2 * tq), F32)], axis=0)
        scores(ch, 0, qi == 0)

    def loop_body(k, carry):
        body(k, False)
        return carry

    lax.fori_loop(0, qi - 1, loop_body, 0)

    @pl.when(qi >= 1)
    def _():
        body(qi - 1, True)

    for ch in chains:
        accumulate(ch, jnp.maximum(qi - 1, 0))
    for ch in chains:
        probs(ch)
    for ch in chains:
        accumulate(ch, qi)


def _attention_kernel(lq1_ref, lk1_ref, lq2_ref, lk2_ref, g_ref,
                      dqt_ref, dk_ref, dvt_ref, fqt_ref, fk_ref, fvt_ref, cq_ref, ck_ref,
                      do_ref, fo_ref, *scratch, lambda_init):
    tq = dqt_ref.shape[2]
    tk = dvt_ref.shape[3]
    units = dqt_ref.shape[1] // LANES
    lane = lax.broadcasted_iota(I32, (1, LANES), 1)
    scratch = [scratch[i:i + 4] for i in range(0, len(scratch), 4)]

    def unit_chains(u):
        lanes = slice(u * LANES, (u + 1) * LANES)

        def tile(ref):
            return lambda i: ref[0, pl.ds(pl.multiple_of(i * tk, tk), tk), lanes]

        def tile_t(ref):
            return lambda i: ref[0, i, lanes, :]

        cq = jnp.concatenate([cq_ref[0, u, 0:1, :], cq_ref[0, u, 1:2, :]], axis=1)

        def key_bias(i):
            ck = tile(ck_ref)(i)
            other = pltpu.roll(ck, HEAD_DIM, 1)
            first = jnp.where(lane < HEAD_DIM, ck, other)
            second = jnp.where(lane >= HEAD_DIM, ck, other)
            return jnp.concatenate([first] * (tq // LANES) + [second] * (tq // LANES), axis=1)

        return [_Chain(_block_diag_queries(dqt_ref[0, lanes, :]), tile(dk_ref), tile_t(dvt_ref),
                       scratch[2 * u]),
                _Chain(_block_diag_queries(fqt_ref[0, lanes, :]), tile(fk_ref), tile_t(fvt_ref),
                       scratch[2 * u + 1], key_bias, cq)]

    chains = [ch for u in range(units) for ch in unit_chains(u)]
    _causal_attention(chains, pl.program_id(2))

    lam = (jnp.exp(jnp.sum(lq1_ref[...] * lk1_ref[...], axis=-1, keepdims=True))
           - jnp.exp(jnp.sum(lq2_ref[...] * lk2_ref[...], axis=-1, keepdims=True)) + lambda_init)
    dv, h = DIFF_V_DIM, HEAD_DIM
    for u in range(units):
        lanes = slice(u * LANES, (u + 1) * LANES)
        acc = chains[2 * u].acc_ref
        o = (acc[0:dv, 0:tq] / acc[dv:dv + 1, 0:tq]
             - lam * (acc[0:dv, tq:] / acc[dv:dv + 1, tq:]))
        ms = jnp.mean(o * o, axis=0, keepdims=True)
        do_ref[0, :, lanes] = (o * lax.rsqrt(ms + NORM_EPS) * g_ref[...]
                               * (1.0 - lambda_init)).T.astype(BF16)
        acc = chains[2 * u + 1].acc_ref
        fo_ref[0, :, lanes] = jnp.concatenate([acc[0:h, 0:tq] / acc[2 * h:2 * h + 1, 0:tq],
                                               acc[h:2 * h, tq:] / acc[2 * h:2 * h + 1, tq:]],
                                              axis=0).T.astype(BF16)


def _attention(dqt, dk, dvt, fqt, fk, fvt, c_rows, c_cols, lq1, lk1, lq2, lk2, g_col, lambda_init):
    b, s, _ = dk.shape
    t = ATTN_TILE
    assert N_DIFF_HEADS == N_FOX_HEADS // 2 and N_DIFF_HEADS % ATTN_UNITS == 0
    width = ATTN_UNITS * LANES
    vec = lambda arr: pl.BlockSpec(arr.shape, lambda i, h, j: (0, 0))
    q_spec = pl.BlockSpec((1, width, t), lambda i, h, j: (i, h, j))
    k_spec = pl.BlockSpec((1, s, width), lambda i, h, j: (i, 0, h))
    v_spec = pl.BlockSpec((1, s // t, width, t), lambda i, h, j: (i, 0, h, 0))
    o_spec = pl.BlockSpec((1, t, width), lambda i, h, j: (i, j, h))
    out = jax.ShapeDtypeStruct((b, s, DIFF_WIDTH), BF16)
    chain_scratch = [pltpu.VMEM((LANES + ONES_ROWS, 2 * t), F32),
                     pltpu.VMEM((t, 2 * t), F32),
                     pltpu.VMEM((t, 2 * t), BF16),
                     pltpu.VMEM((8, 2 * t), F32)]
    return pl.pallas_call(
        functools.partial(_attention_kernel, lambda_init=lambda_init),
        grid=(b, N_DIFF_HEADS // ATTN_UNITS, s // t),
        in_specs=[vec(lq1), vec(lk1), vec(lq2), vec(lk2), vec(g_col),
                  q_spec, k_spec, v_spec, q_spec, k_spec, v_spec,
                  pl.BlockSpec((1, ATTN_UNITS, 2, t), lambda i, h, j: (i, h, 0, j)), k_spec],
        out_specs=[o_spec, o_spec],
        out_shape=[out, out],
        scratch_shapes=chain_scratch * (2 * ATTN_UNITS),
        compiler_params=_params("arbitrary", "arbitrary", "arbitrary"),
        name="attention",
    )(lq1, lk1, lq2, lk2, g_col, dqt, dk, dvt, fqt, fk, fvt, c_rows, c_cols)


def _router_kernel(d_ref, f_ref, x_ref, wo_ref, g_ref, wrt_ref, brt_ref,
                   h_ref, xn_ref, gate_ref, idx_ref, rank_ref, count_ref, carry_ref):
    tm = x_ref.shape[0]
    mixed = (jnp.dot(d_ref[...], wo_ref[0:DIFF_WIDTH, :], preferred_element_type=F32)
             + jnp.dot(f_ref[...], wo_ref[DIFF_WIDTH:, :], preferred_element_type=F32))
    h = x_ref[...] + mixed
    h_ref[...] = h
    xn = _rms(h, g_ref[...])
    xn_ref[...] = xn

    def pieces(v, count):
        out = []
        for _ in range(count):
            part = v.astype(BF16)
            out.append(part)
            v = v - part.astype(F32)
        return out

    nt = (((1,), (1,)), ((), ()))
    xh, xl = pieces(xn, 2)
    wh, wl = pieces(wrt_ref[...], 2)
    work = (lax.dot_general(wh, xh, nt, preferred_element_type=F32)
            + lax.dot_general(wh, xl, nt, preferred_element_type=F32)
            + lax.dot_general(wl, xh, nt, preferred_element_type=F32)) + brt_ref[...]
    expert = lax.broadcasted_iota(I32, work.shape, 0)

    vals, ids, picks = [], [], []
    for _ in range(TOP_K):
        m = jnp.max(work, axis=0, keepdims=True)
        idx = jnp.min(jnp.where(work == m, expert, N_EXPERTS), axis=0, keepdims=True)
        pick = expert == idx
        vals.append(m)
        ids.append(idx)
        picks.append(pick)
        work = jnp.where(pick, -jnp.inf, work)
    exps = [jnp.exp(v - vals[0]) for v in vals]
    denom = exps[0] + exps[1] + exps[2] + exps[3]

    chosen = jnp.where(picks[0] | picks[1] | picks[2] | picks[3], 1.0, 0.0)
    row = lax.broadcasted_iota(I32, (tm, tm), 0)
    col = lax.broadcasted_iota(I32, (tm, tm), 1)
    earlier = jnp.where(row < col, 1.0, 0.0).astype(BF16)

    @pl.when(pl.program_id(0) == 0)
    def _():
        carry_ref[...] = jnp.zeros_like(carry_ref)

    rank = jnp.dot(chosen.astype(BF16), earlier, preferred_element_type=F32) + carry_ref[...]
    carry_ref[...] = carry_ref[...] + jnp.sum(chosen, axis=1, keepdims=True)
    count_ref[...] = jnp.broadcast_to(carry_ref[...], count_ref.shape)

    slot = lax.broadcasted_iota(I32, (8, tm), 0)
    gate_t = jnp.zeros((8, tm), F32)
    idx_t = jnp.zeros((8, tm), I32)
    rank_t = jnp.zeros((8, tm), F32)
    for k in range(TOP_K):
        gate_t = jnp.where(slot == k, exps[k] / denom, gate_t)
        idx_t = jnp.where(slot == k, ids[k], idx_t)
        rank_t = jnp.where(slot == k, jnp.sum(jnp.where(picks[k], rank, 0.0), axis=0, keepdims=True), rank_t)
    idx_ref[...] = idx_t
    rank_ref[...] = rank_t.astype(I32)
    gate_rows = jnp.concatenate([gate_t, jnp.zeros((LANES - 8, tm), F32)], axis=0)
    eye = jnp.where(row == col, 1.0, 0.0).astype(BF16)
    gate_ref[...] = sum(lax.dot_general(eye, part, nt, preferred_element_type=F32)
                        for part in pieces(gate_rows, 3))


def _router(d_out, f_out, x, w_out, g_moe, w_router_t, b_router_t):
    n, d = x.shape
    tm = TOKEN_TILE
    tile = lambda width: pl.BlockSpec((tm, width), lambda i: (i, 0))
    rows = pl.BlockSpec((8, tm), lambda i: (0, i))
    full = lambda arr: pl.BlockSpec(arr.shape, lambda i: (0,) * arr.ndim)
    return pl.pallas_call(
        _router_kernel,
        grid=(n // tm,),
        in_specs=[tile(DIFF_WIDTH), tile(FOX_WIDTH), tile(d), full(w_out), full(g_moe),
                  full(w_router_t), full(b_router_t)],
        out_specs=[tile(d), tile(d), tile(LANES), rows, rows,
                   pl.BlockSpec((N_EXPERTS, LANES), lambda i: (0, 0))],
        out_shape=[jax.ShapeDtypeStruct((n, d), F32), jax.ShapeDtypeStruct((n, d), F32),
                   jax.ShapeDtypeStruct((n, LANES), F32), jax.ShapeDtypeStruct((8, n), I32),
                   jax.ShapeDtypeStruct((8, n), I32), jax.ShapeDtypeStruct((N_EXPERTS, LANES), F32)],
        scratch_shapes=[pltpu.VMEM((N_EXPERTS, 1), F32)],
        compiler_params=_params("arbitrary"),
        name="router",
    )(d_out, f_out, x, w_out, g_moe, w_router_t, b_router_t)


def _dispatch_kernel(ends_ref, pos_ref, x_ref, xs_ref, zero_ref, sem):
    tm = x_ref.shape[0]
    te = zero_ref.shape[0]

    @pl.when(pl.program_id(0) == 0)
    def _():
        zero_ref[...] = jnp.zeros_like(zero_ref)

        def zero_tile(first_row):
            copy = pltpu.make_async_copy(zero_ref, xs_ref.at[pl.ds(pl.multiple_of(first_row, te), te), :], sem)
            copy.start()
            copy.wait()

        for e in range(N_EXPERTS):
            end = ends_ref[e]
            start = ends_ref[e - 1] if e else 0
            pl.when(end > start)(functools.partial(zero_tile, end - te))

        def unused(t, carry):
            zero_tile(t * te)
            return carry

        lax.fori_loop(ends_ref[N_EXPERTS - 1] // te, xs_ref.shape[0] // te, unused, 0)

    def issue(g, carry):
        base = pl.multiple_of(g * ROW_GROUP, ROW_GROUP)
        for j in range(ROW_GROUP):
            for k in range(TOP_K):
                dst = pos_ref[k * tm + base + j]
                pltpu.make_async_copy(x_ref.at[pl.ds(base + j, 1), :], xs_ref.at[pl.ds(dst, 1), :],
                                      sem).start(priority=k % 2)
        return carry

    lax.fori_loop(0, tm // ROW_GROUP, issue, 0)
    for _ in range(TOP_K):
        pltpu.make_async_copy(x_ref, xs_ref.at[pl.ds(0, tm), :], sem).wait()


def _dispatch(ends, pos_flat, xn, n_rows):
    n, d = xn.shape
    tm = TOKEN_TILE
    return pl.pallas_call(
        _dispatch_kernel,
        grid_spec=pltpu.PrefetchScalarGridSpec(
            num_scalar_prefetch=1,
            grid=(n // tm,),
            in_specs=[pl.BlockSpec((TOP_K * tm,), lambda i, ends: (i,), memory_space=pltpu.SMEM),
                      pl.BlockSpec((tm, d), lambda i, ends: (i, 0))],
            out_specs=pl.BlockSpec(memory_space=pl.ANY),
            scratch_shapes=[pltpu.VMEM((EXPERT_TILE, d), F32), pltpu.SemaphoreType.DMA(())]),
        out_shape=jax.ShapeDtypeStruct((n_rows, d), F32),
        compiler_params=_params("arbitrary"),
        name="dispatch",
    )(ends, pos_flat, xn)


SC_WINDOW = 128
SC_COLUMNS = 256


def _dispatch_sparsecore(pos, xn, n_rows):
    n, d = xn.shape
    windows = n // SC_WINDOW
    mesh = plsc.VectorSubcoreMesh(core_axis_name="core", subcore_axis_name="subcore")

    @functools.partial(pl.kernel, out_type=jax.ShapeDtypeStruct((n_rows, d), F32), mesh=mesh,
                       scratch_types=[], name="dispatch_sc")
    def scatter(x_hbm, pos_hbm, xs_hbm):
        for c in range(d // SC_COLUMNS):
            def body(x_vmem, pos_vmem, c=c):
                pltpu.sync_copy(x_vmem, xs_hbm.at[:, pl.ds(c * SC_COLUMNS, SC_COLUMNS)].at[pos_vmem.at[0]])

            pltpu.emit_pipeline(
                body, grid=(TOP_K * windows,),
                in_specs=[pl.BlockSpec((SC_WINDOW, SC_COLUMNS), index_map=lambda i, c=c: (i % windows, c)),
                          pl.BlockSpec((1, SC_WINDOW), index_map=lambda i: (0, i))],
                out_specs=[], core_axis_name=("core", "subcore"),
                dimension_semantics=(pltpu.PARALLEL,))(x_hbm, pos_hbm)

    return scatter(xn, pos.reshape(1, -1))


def _expert_kernel(tile_expert_ref, n_used_ref, next_expert_ref, slot_ref,
                   x_ref, w1_ref, b1_ref, w2_ref, b2_ref, y_ref,
                   w1f_ref, w2f_ref, sem_ref, w1b_ref, w2p_ref, w2b_ref):
    t = pl.program_id(0)
    fresh = (t == 0) | (tile_expert_ref[t] != tile_expert_ref[jnp.maximum(t - 1, 0)])
    used = t < n_used_ref[0]

    def weight_copies(e, s):
        return (pltpu.make_async_copy(w1_ref.at[e], w1f_ref.at[s], sem_ref.at[0, s]),
                pltpu.make_async_copy(w2_ref.at[e], w2f_ref.at[s], sem_ref.at[1, s]))

    @pl.when(t == 0)
    def _():
        for copy in weight_copies(tile_expert_ref[0], slot_ref[0]):
            copy.start()

    @pl.when(fresh & used)
    def _():
        s = slot_ref[t]
        for copy in weight_copies(tile_expert_ref[t], s):
            copy.wait()

        @pl.when(next_expert_ref[t] >= 0)
        def _():
            for copy in weight_copies(next_expert_ref[t], 1 - s):
                copy.start()

        rows = LANES
        half = LANES // 2
        for r in range(0, w1b_ref.shape[0], rows):
            w1b_ref[r:r + rows, :] = w1f_ref[s, r:r + rows, :].astype(BF16)
        for c in range(w2p_ref.shape[0]):
            lanes = slice(c * LANES, (c + 1) * LANES)
            for r in range(0, w2b_ref.shape[0], rows):
                for b in range(2):
                    w2p_ref[c, pl.ds(r + b, half, stride=2), :] = (
                        w2f_ref[s, r + b * half:r + (b + 1) * half, lanes])
            w2b_ref[:, lanes] = w2p_ref[c].astype(BF16)

    @pl.when(used)
    def _():
        xb = x_ref[...].astype(BF16)
        h = jnp.dot(xb, w1b_ref[...], preferred_element_type=F32) + b1_ref[0]
        even = lax.broadcasted_iota(I32, (1, LANES), 1) % 2 == 0
        parts = []
        for q in range(h.shape[1] // (2 * LANES)):
            a = h[:, (2 * q) * LANES:(2 * q + 1) * LANES]
            b = h[:, (2 * q + 1) * LANES:(2 * q + 2) * LANES]
            hg = jnp.where(even, a, pltpu.roll(b, 1, 1))
            hl = jnp.where(even, pltpu.roll(a, LANES - 1, 1), b)
            glu = jnp.minimum(hg, SWIGLU_LIMIT)
            lin = jnp.clip(hl, -SWIGLU_LIMIT, SWIGLU_LIMIT)
            parts.append((glu * jax.nn.sigmoid(SWIGLU_ALPHA * glu) * (lin + 1.0)).astype(BF16))
        act = jnp.concatenate(parts, axis=1)
        y_ref[...] = jnp.dot(act, w2b_ref[...], preferred_element_type=F32) + b2_ref[0]

    @pl.when(jnp.logical_not(used))
    def _():
        y_ref[...] = jnp.zeros_like(y_ref)


def _experts(tile_expert, n_used, next_expert, slot, xs, w1, b1, w2, b2):
    n_rows, d = xs.shape
    tm = EXPERT_TILE
    by_expert = lambda arr: pl.BlockSpec((1,) + arr.shape[1:], lambda i, te, *_: (te[i], 0, 0))
    rows = pl.BlockSpec((tm, d), lambda i, *_: (i, 0))
    used_rows = pl.BlockSpec((tm, d), lambda i, te, nu, *_: (jnp.minimum(i, nu[0] - 1), 0))
    hbm = pl.BlockSpec(memory_space=pl.ANY)
    return pl.pallas_call(
        _expert_kernel,
        grid_spec=pltpu.PrefetchScalarGridSpec(
            num_scalar_prefetch=4,
            grid=(n_rows // tm,),
            in_specs=[used_rows, hbm, by_expert(b1), hbm, by_expert(b2)],
            out_specs=rows,
            scratch_shapes=[pltpu.VMEM((2,) + w1.shape[1:], F32), pltpu.VMEM((2,) + w2.shape[1:], F32),
                            pltpu.SemaphoreType.DMA((2, 2)),
                            pltpu.VMEM(w1.shape[1:], BF16),
                            pltpu.VMEM((w2.shape[2] // LANES, w2.shape[1], LANES), F32),
                            pltpu.VMEM(w2.shape[1:], BF16)]),
        out_shape=jax.ShapeDtypeStruct(xs.shape, F32),
        compiler_params=pltpu.CompilerParams(dimension_semantics=("arbitrary",),
                                             vmem_limit_bytes=EXPERT_VMEM_LIMIT),
        name="experts",
    )(tile_expert, n_used, next_expert, slot, xs, w1, b1, w2, b2)


def _gather_sparsecore(pos, y):
    d = y.shape[1]
    total = pos.size
    mesh = plsc.VectorSubcoreMesh(core_axis_name="core", subcore_axis_name="subcore")

    @functools.partial(pl.kernel, out_type=jax.ShapeDtypeStruct((total, d), F32), mesh=mesh,
                       scratch_types=[], name="gather_sc")
    def gather(y_hbm, pos_hbm, out_hbm):
        for c in range(d // SC_COLUMNS):
            def body(pos_vmem, out_vmem, c=c):
                pltpu.sync_copy(y_hbm.at[:, pl.ds(c * SC_COLUMNS, SC_COLUMNS)].at[pos_vmem.at[0]], out_vmem)

            pltpu.emit_pipeline(
                body, grid=(total // SC_WINDOW,),
                in_specs=[pl.BlockSpec((1, SC_WINDOW), index_map=lambda i: (0, i))],
                out_specs=[pl.BlockSpec((SC_WINDOW, SC_COLUMNS), index_map=lambda i, c=c: (i, c))],
                core_axis_name=("core", "subcore"),
                dimension_semantics=(pltpu.PARALLEL,))(pos_hbm, out_hbm)

    return gather(y, pos.reshape(1, -1))


def _combine_kernel(h_ref, gate_ref, p_ref, wg_ref, wp_ref, gple_ref, gfin_ref, *refs, final_norm):
    y_refs, o_ref = refs[:TOP_K], refs[TOP_K]
    gates = gate_ref[...]
    moe = gates[:, 0:1] * y_refs[0][...]
    for k in range(1, TOP_K):
        moe = moe + gates[:, k:k + 1] * y_refs[k][...]
    h = h_ref[...] + moe
    gate = jax.nn.sigmoid(jnp.dot(_rms(h, gple_ref[...]).astype(BF16), wg_ref[...],
                                  preferred_element_type=F32))
    emb = jnp.dot(p_ref[...].astype(BF16), wp_ref[...], preferred_element_type=F32)
    h = h + gate * emb
    o_ref[...] = _rms(h, gfin_ref[...]) if final_norm else h


def _combine(h, gates, p, w_gate, w_proj, g_ple, g_final, y_tok, final_norm):
    n, d = h.shape
    tm = TOKEN_TILE
    tile = lambda width: pl.BlockSpec((tm, width), lambda i: (i, 0))
    full = lambda arr: pl.BlockSpec(arr.shape, lambda i: (0,) * arr.ndim)
    choice = lambda k: pl.BlockSpec((tm, d), lambda i: (k * (n // tm) + i, 0))
    return pl.pallas_call(
        functools.partial(_combine_kernel, final_norm=final_norm),
        grid=(n // tm,),
        in_specs=[tile(d), tile(LANES), tile(p.shape[1]), full(w_gate), full(w_proj),
                  full(g_ple), full(g_final)] + [choice(k) for k in range(TOP_K)],
        out_specs=tile(d),
        out_shape=jax.ShapeDtypeStruct((n, d), F32),
        compiler_params=_params("arbitrary"),
        name="combine",
    )(h, gates, p, w_gate, w_proj, g_ple, g_final, *([y_tok] * TOP_K))


def _layer(h, p, positions, g_attn, w_in, lq1, lk1, lq2, lk2, g_subln, b_forget, w_out, g_moe,
           w_router, b_router, w_e1, b_e1, w_e2, b_e2, g_ple, w_ple_gate, w_ple_proj, g_final,
           lambda_init, final_norm):
    b, s, d = h.shape
    n = b * s
    row = lambda v: v.reshape(1, -1)

    w_main = w_in[:, :MAIN_WIDTH].astype(BF16)
    pad_heads = ((0, 2 * N_FOX_HEADS - N_FOX_HEADS), (0, 0))
    w_fz_t = jnp.pad(w_in[:, MAIN_WIDTH:].T, pad_heads).astype(BF16)
    b_fz_t = jnp.pad(b_forget.reshape(-1, 1), pad_heads)
    b1 = b_e1[:, None, :]
    b2 = b_e2[:, None, :]

    dq_t, dk, dv_t, fq_t, fk, fv_t, c, c_t = _inproj(positions, h, row(g_attn), w_main, w_fz_t, b_fz_t)
    c_rows = c_t.reshape(b, N_FOX_HEADS // 2, 2, s)
    d_out, f_out = _attention(dq_t, dk, dv_t, fq_t, fk, fv_t, c_rows, c, row(lq1), row(lk1), row(lq2),
                              row(lk2), g_subln.reshape(-1, 1), lambda_init)

    h1, xn, gates, idx, rank, counts = _router(
        d_out.reshape(n, DIFF_WIDTH), f_out.reshape(n, FOX_WIDTH), h.reshape(n, d),
        w_out.astype(BF16), row(g_moe), w_router.T, b_router.reshape(-1, 1))

    te = EXPERT_TILE
    n_rows = n * TOP_K + N_EXPERTS * te
    counts = counts[:, 0].astype(I32)
    padded = (counts + te - 1) // te * te
    ends = jnp.cumsum(padded)
    starts = ends - padded
    experts = jnp.arange(N_EXPERTS, dtype=I32)[:, None, None]
    pos = rank[:TOP_K] + jnp.sum(jnp.where(idx[None, :TOP_K] == experts, starts[:, None, None], 0),
                                 axis=0)
    tm = TOKEN_TILE
    pos_flat = pos.reshape(TOP_K, n // tm, tm).transpose(1, 0, 2).reshape(-1)
    tile_start = jnp.arange(n_rows // te, dtype=I32) * te
    tile_expert = jnp.minimum(jnp.sum((ends[None, :] <= tile_start[:, None]).astype(I32), axis=1),
                              N_EXPERTS - 1)
    n_used = (ends[-1:] // te).astype(I32)
    owns = padded > 0
    later = lax.cummin(jnp.where(owns, jnp.arange(N_EXPERTS, dtype=I32), N_EXPERTS), reverse=True)
    following = jnp.concatenate([later[1:], jnp.full((1,), N_EXPERTS, I32)])
    following = jnp.where(following < N_EXPERTS, following, -1)
    next_expert = following[tile_expert]
    slot = ((jnp.cumsum(owns.astype(I32)) - 1) % 2)[tile_expert]

    xs = _dispatch_sparsecore(pos.astype(I32), xn, n_rows)
    y = _experts(tile_expert, n_used, next_expert.astype(I32), slot.astype(I32), xs, w_e1, b1, w_e2, b2)
    y_tok = _gather_sparsecore(pos.astype(I32), y)
    out = _combine(h1, gates, p.reshape(n, PLE_DIM), w_ple_gate.astype(BF16),
                   w_ple_proj.astype(BF16), row(g_ple), row(g_final), y_tok, final_norm)
    return out.reshape(b, s, d)


def kernel(x, p, positions, g_attn, w_in, lambda_q1, lambda_k1, lambda_q2, lambda_k2, g_subln, b_forget, w_out, g_moe, w_router, b_router, w_e1, b_e1, w_e2, b_e2, g_ple, w_ple_gate, w_ple_proj, g_final):
    depth = g_attn.shape[0]
    h = x
    for i in range(depth):
        lambda_init = 0.8 - 0.6 * math.exp(-0.3 * i)
        h = _layer(h, p[i], positions, g_attn[i], w_in[i], lambda_q1[i], lambda_k1[i],
                   lambda_q2[i], lambda_k2[i], g_subln[i], b_forget[i], w_out[i], g_moe[i],
                   w_router[i], b_router[i], w_e1[i], b_e1[i], w_e2[i], b_e2[i], g_ple[i],
                   w_ple_gate[i], w_ple_proj[i], g_final, lambda_init, i == depth - 1)
    return h
```

```python
import functools
import math

import jax
import jax.numpy as jnp
from jax import lax
from jax.experimental import pallas as pl
from jax.experimental.pallas import tpu as pltpu
from jax.experimental.pallas import tpu_sc as plsc

F32 = jnp.float32
BF16 = jnp.bfloat16
I32 = jnp.int32
U32 = jnp.uint32

LANES = 128
D_MODEL = 1024
HEAD_DIM = 64
N_DIFF_HEADS = 4
DIFF_V_DIM = 2 * HEAD_DIM
N_FOX_HEADS = 8
DIFF_WIDTH = N_DIFF_HEADS * DIFF_V_DIM
FOX_WIDTH = N_FOX_HEADS * HEAD_DIM
ROT_DIM = HEAD_DIM // 4
ROPE_THETA = 500000.0
N_EXPERTS = 32
TOP_K = 4
D_EXPERT = D_MODEL
SWIGLU_ALPHA = 1.702
SWIGLU_LIMIT = 7.0
PLE_DIM = 256
NORM_EPS = 1e-5
LOG2_E = math.log2(math.e)
MAIN_WIDTH = 3 * DIFF_WIDTH + 3 * FOX_WIDTH

SEQ_TILE = 256
ATTN_TILE = 256
ATTN_UNITS = 2
TOKEN_TILE = 256
EXPERT_TILE = 256
VMEM_LIMIT = 48 * 1024 * 1024
EXPERT_VMEM_LIMIT = 58 * 1024 * 1024


def _params(*semantics):
    return pltpu.CompilerParams(dimension_semantics=semantics, vmem_limit_bytes=VMEM_LIMIT)


def _rms(x, g):
    return x * lax.rsqrt(jnp.mean(x * x, axis=-1, keepdims=True) + NORM_EPS) * g


def _pack_rows(v):
    half = v.shape[1] // 2
    bits = pltpu.bitcast(v.astype(BF16).astype(F32), U32)
    return (bits[:, :half] >> 16) | bits[:, half:]


def _unpack_rows(words):
    low = pltpu.bitcast(words << 16, F32)
    high = pltpu.bitcast(words & jnp.uint32(0xFFFF0000), F32)
    return jnp.concatenate([low, high], axis=1)


def _inproj_kernel(pos_ref, x_ref, g_ref, w_ref, wfzt_ref, bfzt_ref,
                   dqt_ref, dk_ref, dvt_ref, fqt_ref, fk_ref, fvt_ref, c_ref, ct_ref, carry_t_ref):
    ts = x_ref.shape[1]
    xb = _rms(x_ref[0], g_ref[...]).astype(BF16)
    proj = jnp.dot(xb, w_ref[...], preferred_element_type=F32)
    nt = (((1,), (1,)), ((), ()))
    tn = (((0,), (0,)), ((), ()))

    def pieces(v):
        hi = v.astype(BF16)
        r1 = v - hi.astype(F32)
        mid = r1.astype(BF16)
        return hi, mid, (r1 - mid.astype(F32)).astype(BF16)

    def rows_to_lanes(v, select):
        return sum(lax.dot_general(part, select, tn, preferred_element_type=F32) for part in pieces(v))

    half_rot = ROT_DIM // 2
    freq = lax.broadcasted_iota(I32, (half_rot, 1), 0)
    inv_freq = jnp.power(ROPE_THETA, -(2 * freq).astype(F32) / ROT_DIM)
    ang = inv_freq * pos_ref[0].astype(F32)
    trig = jnp.concatenate([jnp.cos(ang), jnp.sin(ang)], axis=0)
    r = lax.broadcasted_iota(I32, (2 * half_rot, 2 * LANES), 0)
    lane2 = lax.broadcasted_iota(I32, (2 * half_rot, 2 * LANES), 1)
    d = lane2 % HEAD_DIM
    same_freq = jnp.where(d % half_rot == r % half_rot, 1.0, 0.0)
    cos_part = jnp.where((r < half_rot) & (lane2 < LANES) & (d < ROT_DIM), same_freq, 0.0)
    sin_part = jnp.where((r >= half_rot) & (lane2 >= LANES) & (d < ROT_DIM),
                         jnp.where(d < half_rot, -same_freq, same_freq), 0.0)
    tables = rows_to_lanes(trig, (cos_part + sin_part).astype(BF16))
    lane = lax.broadcasted_iota(I32, (1, LANES), 1) % HEAD_DIM
    cosf = jnp.where(lane < ROT_DIM, tables[:, :LANES], 1.0)
    sinf = tables[:, LANES:]
    first_half = lane < half_rot

    def rope(t):
        outs = []
        for j in range(t.shape[1] // LANES):
            tj = t[:, j * LANES:(j + 1) * LANES]
            partner = jnp.where(first_half,
                                pltpu.roll(tj, LANES - ROT_DIM // 2, 1),
                                pltpu.roll(tj, ROT_DIM // 2, 1))
            outs.append(tj * cosf + partner * sinf)
        return jnp.concatenate(outs, axis=1)

    scale = HEAD_DIM ** -0.5 * LOG2_E
    w = DIFF_WIDTH
    dqt_ref[0] = (rope(proj[:, 0:w]) * scale).astype(BF16).T
    dk_ref[0] = rope(proj[:, w:2 * w]).astype(BF16)
    dvt_ref[0, 0] = proj[:, 2 * w:3 * w].astype(BF16).T
    fqt_ref[0] = (proj[:, 3 * w:4 * w] * scale).astype(BF16).T
    fk_ref[0] = proj[:, 4 * w:5 * w].astype(BF16)
    fvt_ref[0, 0] = proj[:, 5 * w:6 * w].astype(BF16).T

    def log_sigmoid(z):
        return jnp.minimum(z, 0.0) - jnp.log1p(jnp.exp(-jnp.abs(z)))

    row = lax.broadcasted_iota(I32, (ts, ts), 0)
    col = lax.broadcasted_iota(I32, (ts, ts), 1)
    upto_col = jnp.where(row <= col, 1.0, 0.0).astype(BF16)

    @pl.when(pl.program_id(1) == 0)
    def _():
        carry_t_ref[...] = jnp.zeros_like(carry_t_ref)

    fzt = lax.dot_general(wfzt_ref[...], xb, nt, preferred_element_type=F32) + bfzt_ref[...]
    ct = carry_t_ref[...] + sum(jnp.dot(part, upto_col, preferred_element_type=F32)
                                for part in pieces(log_sigmoid(fzt)))
    carry_t_ref[...] = ct[:, ts - 1:ts]
    ct = ct * LOG2_E
    ct_ref[0] = ct[0:N_FOX_HEADS, :]
    head = lax.broadcasted_iota(I32, (ct.shape[0], FOX_WIDTH), 0)
    group = lax.broadcasted_iota(I32, (ct.shape[0], FOX_WIDTH), 1) // HEAD_DIM
    c_ref[0] = rows_to_lanes(ct, jnp.where(head == group, 1.0, 0.0).astype(BF16))


def _inproj(positions, x, g, w_main, w_fz_t, b_fz_t):
    b, s, d = x.shape
    ts = SEQ_TILE
    assert ts == ATTN_TILE
    tile = lambda width: pl.BlockSpec((1, ts, width), lambda i, j: (i, j, 0))
    tile_t = pl.BlockSpec((1, DIFF_WIDTH, ts), lambda i, j: (i, 0, j))
    key_tile_t = pl.BlockSpec((1, 1, DIFF_WIDTH, ts), lambda i, j: (i, j, 0, 0))
    full = lambda arr: pl.BlockSpec(arr.shape, lambda i, j: (0,) * arr.ndim)
    rows = jax.ShapeDtypeStruct((b, s, DIFF_WIDTH), BF16)
    rows_t = jax.ShapeDtypeStruct((b, DIFF_WIDTH, s), BF16)
    key_tiles_t = jax.ShapeDtypeStruct((b, s // ts, DIFF_WIDTH, ts), BF16)
    return pl.pallas_call(
        _inproj_kernel,
        grid=(b, s // ts),
        in_specs=[pl.BlockSpec((1, 1, ts), lambda i, j: (i, 0, j)), tile(d), full(g), full(w_main),
                  full(w_fz_t), full(b_fz_t)],
        out_specs=[tile_t, tile(DIFF_WIDTH), key_tile_t, tile_t, tile(FOX_WIDTH), key_tile_t,
                   tile(FOX_WIDTH), pl.BlockSpec((1, N_FOX_HEADS, ts), lambda i, j: (i, 0, j))],
        out_shape=[rows_t, rows, key_tiles_t, rows_t, rows, key_tiles_t,
                   jax.ShapeDtypeStruct((b, s, FOX_WIDTH), F32),
                   jax.ShapeDtypeStruct((b, N_FOX_HEADS, s), F32)],
        scratch_shapes=[pltpu.VMEM((w_fz_t.shape[0], 1), F32)],
        compiler_params=_params("arbitrary", "arbitrary"),
        name="inproj",
    )(positions.reshape(b, 1, s), x, g, w_main, w_fz_t, b_fz_t)


ONES_ROWS = 16


def _block_diag_queries(qt):
    row = lax.broadcasted_iota(I32, qt.shape, 0)
    zero = jnp.zeros_like(qt)
    return jnp.concatenate([jnp.where(row < HEAD_DIM, qt, zero),
                            jnp.where(row >= HEAD_DIM, qt, zero)], axis=1)


class _Chain:
    def __init__(self, qbd, keys, values_t, scratch, key_bias=None, query_bias=None):
        self.qbd, self.keys, self.values_t = qbd, keys, values_t
        self.acc_ref, self.s_ref, self.p_ref, self.stat_ref = scratch
        self.key_bias, self.query_bias = key_bias, query_bias


_RUNNING_MAX, _TILE_MAX, _RESCALE = 0, 1, 2


def _causal_attention(chains, qi):
    tk, tq = chains[0].s_ref.shape[0], chains[0].s_ref.shape[1] // 2
    ones = jnp.ones((ONES_ROWS, tk), BF16)

    def scores(ch, i, diagonal):
        s = jnp.dot(ch.keys(i), ch.qbd, preferred_element_type=F32)
        if ch.key_bias is not None:
            s = s - ch.key_bias(i)
        if diagonal is not False:
            r = lax.broadcasted_iota(I32, s.shape, 0)
            c = lax.broadcasted_iota(I32, s.shape, 1) & (tq - 1)
            visible = r <= c if diagonal is True else jnp.logical_or(r <= c, jnp.logical_not(diagonal))
            s = jnp.where(visible, s, -jnp.inf)
        ch.s_ref[...] = s
        top = jnp.max(s, axis=0, keepdims=True)
        if ch.query_bias is not None:
            top = top + ch.query_bias
        ch.stat_ref[_TILE_MAX:_TILE_MAX + 1, :] = top

    def probs(ch):
        m = ch.stat_ref[_RUNNING_MAX:_RUNNING_MAX + 1, :]
        m_new = jnp.maximum(m, ch.stat_ref[_TILE_MAX:_TILE_MAX + 1, :])
        ch.stat_ref[_RUNNING_MAX:_RUNNING_MAX + 1, :] = m_new
        ch.stat_ref[_RESCALE:_RESCALE + 1, :] = jnp.exp2(m - m_new)
        ref = m_new if ch.query_bias is None else m_new - ch.query_bias
        ch.p_ref[...] = jnp.exp2(ch.s_ref[...] - ref).astype(BF16)

    def accumulate(ch, i):
        vts = jnp.concatenate([ch.values_t(i), ones], axis=0)
        ch.acc_ref[...] = (ch.stat_ref[_RESCALE:_RESCALE + 1, :] * ch.acc_ref[...]
                           + jnp.dot(vts, ch.p_ref[...], preferred_element_type=F32))

    def body(k, diagonal_next):
        for ch in chains:
            accumulate(ch, jnp.maximum(k - 1, 0))
        for ch in chains:
            probs(ch)
        for ch in chains:
            scores(ch, k + 1, diagonal_next)

    for ch in chains:
        ch.acc_ref[...] = jnp.zeros_like(ch.acc_ref)
        ch.p_ref[...] = jnp.zeros_like(ch.p_ref)
        ch.stat_ref[...] = jnp.concatenate(
            [jnp.full((1, 2 * tq), -jnp.inf, F32), jnp.ones((ch.stat_ref.shape[0] - 1, 2 * tq), F32)], axis=0)
        scores(ch, 0, qi == 0)

    def loop_body(k, carry):
        body(k, False)
        return carry

    lax.fori_loop(0, qi - 1, loop_body, 0)

    @pl.when(qi >= 1)
    def _():
        body(qi - 1, True)

    for ch in chains:
        accumulate(ch, jnp.maximum(qi - 1, 0))
    for ch in chains:
        probs(ch)
    for ch in chains:
        accumulate(ch, qi)


def _attention_kernel(lq1_ref, lk1_ref, lq2_ref, lk2_ref, g_ref,
                      dqt_ref, dk_ref, dvt_ref, fqt_ref, fk_ref, fvt_ref, cq_ref, ck_ref,
                      do_ref, fo_ref, *scratch, lambda_init):
    tq = dqt_ref.shape[2]
    tk = dvt_ref.shape[3]
    units = dqt_ref.shape[1] // LANES
    lane = lax.broadcasted_iota(I32, (1, LANES), 1)
    scratch = [scratch[i:i + 4] for i in range(0, len(scratch), 4)]

    def unit_chains(u):
        lanes = slice(u * LANES, (u + 1) * LANES)

        def tile(ref):
            return lambda i: ref[0, pl.ds(pl.multiple_of(i * tk, tk), tk), lanes]

        def tile_t(ref):
            return lambda i: ref[0, i, lanes, :]

        cq = jnp.concatenate([cq_ref[0, u, 0:1, :], cq_ref[0, u, 1:2, :]], axis=1)

        def key_bias(i):
            ck = tile(ck_ref)(i)
            other = pltpu.roll(ck, HEAD_DIM, 1)
            first = jnp.where(lane < HEAD_DIM, ck, other)
            second = jnp.where(lane >= HEAD_DIM, ck, other)
            return jnp.concatenate([first] * (tq // LANES) + [second] * (tq // LANES), axis=1)

        return [_Chain(_block_diag_queries(dqt_ref[0, lanes, :]), tile(dk_ref), tile_t(dvt_ref),
                       scratch[2 * u]),
                _Chain(_block_diag_queries(fqt_ref[0, lanes, :]), tile(fk_ref), tile_t(fvt_ref),
                       scratch[2 * u + 1], key_bias, cq)]

    chains = [ch for u in range(units) for ch in unit_chains(u)]
    _causal_attention(chains, pl.program_id(2))

    lam = (jnp.exp(jnp.sum(lq1_ref[...] * lk1_ref[...], axis=-1, keepdims=True))
           - jnp.exp(jnp.sum(lq2_ref[...] * lk2_ref[...], axis=-1, keepdims=True)) + lambda_init)
    dv, h = DIFF_V_DIM, HEAD_DIM
    for u in range(units):
        lanes = slice(u * LANES, (u + 1) * LANES)
        acc = chains[2 * u].acc_ref
        o = (acc[0:dv, 0:tq] / acc[dv:dv + 1, 0:tq]
             - lam * (acc[0:dv, tq:] / acc[dv:dv + 1, tq:]))
        ms = jnp.mean(o * o, axis=0, keepdims=True)
        do_ref[0, :, lanes] = (o * lax.rsqrt(ms + NORM_EPS) * g_ref[...]
                               * (1.0 - lambda_init)).T.astype(BF16)
        acc = chains[2 * u + 1].acc_ref
        fo_ref[0, :, lanes] = jnp.concatenate([acc[0:h, 0:tq] / acc[2 * h:2 * h + 1, 0:tq],
                                               acc[h:2 * h, tq:] / acc[2 * h:2 * h + 1, tq:]],
                                              axis=0).T.astype(BF16)


def _attention(dqt, dk, dvt, fqt, fk, fvt, c_rows, c_cols, lq1, lk1, lq2, lk2, g_col, lambda_init):
    b, s, _ = dk.shape
    t = ATTN_TILE
    assert N_DIFF_HEADS == N_FOX_HEADS // 2 and N_DIFF_HEADS % ATTN_UNITS == 0
    width = ATTN_UNITS * LANES
    vec = lambda arr: pl.BlockSpec(arr.shape, lambda i, h, j: (0, 0))
    q_spec = pl.BlockSpec((1, width, t), lambda i, h, j: (i, h, j))
    k_spec = pl.BlockSpec((1, s, width), lambda i, h, j: (i, 0, h))
    v_spec = pl.BlockSpec((1, s // t, width, t), lambda i, h, j: (i, 0, h, 0))
    o_spec = pl.BlockSpec((1, t, width), lambda i, h, j: (i, j, h))
    out = jax.ShapeDtypeStruct((b, s, DIFF_WIDTH), BF16)
    chain_scratch = [pltpu.VMEM((LANES + ONES_ROWS, 2 * t), F32),
                     pltpu.VMEM((t, 2 * t), F32),
                     pltpu.VMEM((t, 2 * t), BF16),
                     pltpu.VMEM((8, 2 * t), F32)]
    return pl.pallas_call(
        functools.partial(_attention_kernel, lambda_init=lambda_init),
        grid=(b, N_DIFF_HEADS // ATTN_UNITS, s // t),
        in_specs=[vec(lq1), vec(lk1), vec(lq2), vec(lk2), vec(g_col),
                  q_spec, k_spec, v_spec, q_spec, k_spec, v_spec,
                  pl.BlockSpec((1, ATTN_UNITS, 2, t), lambda i, h, j: (i, h, 0, j)), k_spec],
        out_specs=[o_spec, o_spec],
        out_shape=[out, out],
        scratch_shapes=chain_scratch * (2 * ATTN_UNITS),
        compiler_params=_params("arbitrary", "arbitrary", "arbitrary"),
        name="attention",
    )(lq1, lk1, lq2, lk2, g_col, dqt, dk, dvt, fqt, fk, fvt, c_rows, c_cols)


def _router_kernel(d_ref, f_ref, x_ref, wo_ref, g_ref, wrt_ref, brt_ref,
                   h_ref, xn_ref, gate_ref, idx_ref, rank_ref, count_ref, carry_ref):
    tm = x_ref.shape[0]
    mixed = (jnp.dot(d_ref[...], wo_ref[0:DIFF_WIDTH, :], preferred_element_type=F32)
             + jnp.dot(f_ref[...], wo_ref[DIFF_WIDTH:, :], preferred_element_type=F32))
    h = x_ref[...] + mixed
    h_ref[...] = h
    xn = _rms(h, g_ref[...])
    xn_ref[...] = _pack_rows(xn)

    def pieces(v, count):
        out = []
        for _ in range(count):
            part = v.astype(BF16)
            out.append(part)
            v = v - part.astype(F32)
        return out

    nt = (((1,), (1,)), ((), ()))
    xh, xl = pieces(xn, 2)
    wh, wl = pieces(wrt_ref[...], 2)
    work = (lax.dot_general(wh, xh, nt, preferred_element_type=F32)
            + lax.dot_general(wh, xl, nt, preferred_element_type=F32)
            + lax.dot_general(wl, xh, nt, preferred_element_type=F32)) + brt_ref[...]
    expert = lax.broadcasted_iota(I32, work.shape, 0)

    vals, ids, picks = [], [], []
    for _ in range(TOP_K):
        m = jnp.max(work, axis=0, keepdims=True)
        idx = jnp.min(jnp.where(work == m, expert, N_EXPERTS), axis=0, keepdims=True)
        pick = expert == idx
        vals.append(m)
        ids.append(idx)
        picks.append(pick)
        work = jnp.where(pick, -jnp.inf, work)
    exps = [jnp.exp(v - vals[0]) for v in vals]
    denom = exps[0] + exps[1] + exps[2] + exps[3]

    chosen = jnp.where(picks[0] | picks[1] | picks[2] | picks[3], 1.0, 0.0)
    row = lax.broadcasted_iota(I32, (tm, tm), 0)
    col = lax.broadcasted_iota(I32, (tm, tm), 1)
    earlier = jnp.where(row < col, 1.0, 0.0).astype(BF16)

    @pl.when(pl.program_id(0) == 0)
    def _():
        carry_ref[...] = jnp.zeros_like(carry_ref)

    rank = jnp.dot(chosen.astype(BF16), earlier, preferred_element_type=F32) + carry_ref[...]
    carry_ref[...] = carry_ref[...] + jnp.sum(chosen, axis=1, keepdims=True)
    count_ref[...] = jnp.broadcast_to(carry_ref[...], count_ref.shape)

    slot = lax.broadcasted_iota(I32, (8, tm), 0)
    gate_t = jnp.zeros((8, tm), F32)
    idx_t = jnp.zeros((8, tm), I32)
    rank_t = jnp.zeros((8, tm), F32)
    for k in range(TOP_K):
        gate_t = jnp.where(slot == k, exps[k] / denom, gate_t)
        idx_t = jnp.where(slot == k, ids[k], idx_t)
        rank_t = jnp.where(slot == k, jnp.sum(jnp.where(picks[k], rank, 0.0), axis=0, keepdims=True), rank_t)
    idx_ref[...] = idx_t
    rank_ref[...] = rank_t.astype(I32)
    gate_rows = jnp.concatenate([gate_t, jnp.zeros((LANES - 8, tm), F32)], axis=0)
    eye = jnp.where(row == col, 1.0, 0.0).astype(BF16)
    gate_ref[...] = sum(lax.dot_general(eye, part, nt, preferred_element_type=F32)
                        for part in pieces(gate_rows, 3))


def _router(d_out, f_out, x, w_out, g_moe, w_router_t, b_router_t):
    n, d = x.shape
    tm = TOKEN_TILE
    tile = lambda width: pl.BlockSpec((tm, width), lambda i: (i, 0))
    rows = pl.BlockSpec((8, tm), lambda i: (0, i))
    full = lambda arr: pl.BlockSpec(arr.shape, lambda i: (0,) * arr.ndim)
    return pl.pallas_call(
        _router_kernel,
        grid=(n // tm,),
        in_specs=[tile(DIFF_WIDTH), tile(FOX_WIDTH), tile(d), full(w_out), full(g_moe),
                  full(w_router_t), full(b_router_t)],
        out_specs=[tile(d), tile(d // 2), tile(LANES), rows, rows,
                   pl.BlockSpec((N_EXPERTS, LANES), lambda i: (0, 0))],
        out_shape=[jax.ShapeDtypeStruct((n, d), F32), jax.ShapeDtypeStruct((n, d // 2), U32),
                   jax.ShapeDtypeStruct((n, LANES), F32), jax.ShapeDtypeStruct((8, n), I32),
                   jax.ShapeDtypeStruct((8, n), I32), jax.ShapeDtypeStruct((N_EXPERTS, LANES), F32)],
        scratch_shapes=[pltpu.VMEM((N_EXPERTS, 1), F32)],
        compiler_params=_params("arbitrary"),
        name="router",
    )(d_out, f_out, x, w_out, g_moe, w_router_t, b_router_t)


SC_WINDOW = 128
SC_COLUMNS = 256


def _dispatch_sparsecore(pos, xn, n_rows):
    n, d = xn.shape
    windows = n // SC_WINDOW
    mesh = plsc.VectorSubcoreMesh(core_axis_name="core", subcore_axis_name="subcore")

    @functools.partial(pl.kernel, out_type=jax.ShapeDtypeStruct((n_rows, d), xn.dtype), mesh=mesh,
                       scratch_types=[], name="dispatch_sc")
    def scatter(x_hbm, pos_hbm, xs_hbm):
        for c in range(d // SC_COLUMNS):
            def body(x_vmem, pos_vmem, c=c):
                pltpu.sync_copy(x_vmem, xs_hbm.at[:, pl.ds(c * SC_COLUMNS, SC_COLUMNS)].at[pos_vmem.at[0]])

            pltpu.emit_pipeline(
                body, grid=(TOP_K * windows,),
                in_specs=[pl.BlockSpec((SC_WINDOW, SC_COLUMNS), index_map=lambda i, c=c: (i % windows, c)),
                          pl.BlockSpec((1, SC_WINDOW), index_map=lambda i: (0, i))],
                out_specs=[], core_axis_name=("core", "subcore"),
                dimension_semantics=(pltpu.PARALLEL,))(x_hbm, pos_hbm)

    return scatter(xn, pos.reshape(1, -1))


def _expert_kernel(tile_expert_ref, n_used_ref, next_expert_ref, slot_ref,
                   x_ref, w1_ref, b1_ref, w2_ref, b2_ref, y_ref,
                   w1f_ref, w2f_ref, sem_ref, w1b_ref, w2p_ref, w2b_ref):
    t = pl.program_id(0)
    fresh = (t == 0) | (tile_expert_ref[t] != tile_expert_ref[jnp.maximum(t - 1, 0)])
    used = t < n_used_ref[0]

    def weight_copies(e, s):
        return (pltpu.make_async_copy(w1_ref.at[e], w1f_ref.at[s], sem_ref.at[0, s]),
                pltpu.make_async_copy(w2_ref.at[e], w2f_ref.at[s], sem_ref.at[1, s]))

    @pl.when(t == 0)
    def _():
        for copy in weight_copies(tile_expert_ref[0], slot_ref[0]):
            copy.start()

    @pl.when(fresh & used)
    def _():
        s = slot_ref[t]
        for copy in weight_copies(tile_expert_ref[t], s):
            copy.wait()

        @pl.when(next_expert_ref[t] >= 0)
        def _():
            for copy in weight_copies(next_expert_ref[t], 1 - s):
                copy.start()

        rows = LANES
        half = LANES // 2
        for r in range(0, w1b_ref.shape[0], rows):
            w1b_ref[r:r + rows, :] = w1f_ref[s, r:r + rows, :].astype(BF16)
        for c in range(w2p_ref.shape[0]):
            lanes = slice(c * LANES, (c + 1) * LANES)
            for r in range(0, w2b_ref.shape[0], rows):
                for b in range(2):
                    w2p_ref[c, pl.ds(r + b, half, stride=2), :] = (
                        w2f_ref[s, r + b * half:r + (b + 1) * half, lanes])
            w2b_ref[:, lanes] = w2p_ref[c].astype(BF16)

    @pl.when(used)
    def _():
        xb = _unpack_rows(x_ref[...]).astype(BF16)
        h = jnp.dot(xb, w1b_ref[...], preferred_element_type=F32) + b1_ref[0]
        even = lax.broadcasted_iota(I32, (1, LANES), 1) % 2 == 0
        parts = []
        for q in range(h.shape[1] // (2 * LANES)):
            a = h[:, (2 * q) * LANES:(2 * q + 1) * LANES]
            b = h[:, (2 * q + 1) * LANES:(2 * q + 2) * LANES]
            hg = jnp.where(even, a, pltpu.roll(b, 1, 1))
            hl = jnp.where(even, pltpu.roll(a, LANES - 1, 1), b)
            glu = jnp.minimum(hg, SWIGLU_LIMIT)
            lin = jnp.clip(hl, -SWIGLU_LIMIT, SWIGLU_LIMIT)
            parts.append((glu * jax.nn.sigmoid(SWIGLU_ALPHA * glu) * (lin + 1.0)).astype(BF16))
        act = jnp.concatenate(parts, axis=1)
        y_ref[...] = _pack_rows(jnp.dot(act, w2b_ref[...], preferred_element_type=F32) + b2_ref[0])

    @pl.when(jnp.logical_not(used))
    def _():
        y_ref[...] = jnp.zeros_like(y_ref)


def _experts(tile_expert, n_used, next_expert, slot, xs, w1, b1, w2, b2):
    n_rows, d = xs.shape
    tm = EXPERT_TILE
    by_expert = lambda arr: pl.BlockSpec((1,) + arr.shape[1:], lambda i, te, *_: (te[i], 0, 0))
    rows = pl.BlockSpec((tm, d), lambda i, *_: (i, 0))
    used_rows = pl.BlockSpec((tm, d), lambda i, te, nu, *_: (jnp.minimum(i, nu[0] - 1), 0))
    hbm = pl.BlockSpec(memory_space=pl.ANY)
    return pl.pallas_call(
        _expert_kernel,
        grid_spec=pltpu.PrefetchScalarGridSpec(
            num_scalar_prefetch=4,
            grid=(n_rows // tm,),
            in_specs=[used_rows, hbm, by_expert(b1), hbm, by_expert(b2)],
            out_specs=rows,
            scratch_shapes=[pltpu.VMEM((2,) + w1.shape[1:], F32), pltpu.VMEM((2,) + w2.shape[1:], F32),
                            pltpu.SemaphoreType.DMA((2, 2)),
                            pltpu.VMEM(w1.shape[1:], BF16),
                            pltpu.VMEM((w2.shape[2] // LANES, w2.shape[1], LANES), F32),
                            pltpu.VMEM(w2.shape[1:], BF16)]),
        out_shape=jax.ShapeDtypeStruct(xs.shape, xs.dtype),
        compiler_params=pltpu.CompilerParams(dimension_semantics=("arbitrary",),
                                             vmem_limit_bytes=EXPERT_VMEM_LIMIT),
        name="experts",
    )(tile_expert, n_used, next_expert, slot, xs, w1, b1, w2, b2)


def _gather_sparsecore(pos, y):
    d = y.shape[1]
    total = pos.size
    mesh = plsc.VectorSubcoreMesh(core_axis_name="core", subcore_axis_name="subcore")

    @functools.partial(pl.kernel, out_type=jax.ShapeDtypeStruct((total, d), y.dtype), mesh=mesh,
                       scratch_types=[], name="gather_sc")
    def gather(y_hbm, pos_hbm, out_hbm):
        for c in range(d // SC_COLUMNS):
            def body(pos_vmem, out_vmem, c=c):
                pltpu.sync_copy(y_hbm.at[:, pl.ds(c * SC_COLUMNS, SC_COLUMNS)].at[pos_vmem.at[0]], out_vmem)

            pltpu.emit_pipeline(
                body, grid=(total // SC_WINDOW,),
                in_specs=[pl.BlockSpec((1, SC_WINDOW), index_map=lambda i: (0, i))],
                out_specs=[pl.BlockSpec((SC_WINDOW, SC_COLUMNS), index_map=lambda i, c=c: (i, c))],
                core_axis_name=("core", "subcore"),
                dimension_semantics=(pltpu.PARALLEL,))(pos_hbm, out_hbm)

    return gather(y, pos.reshape(1, -1))


def _combine_kernel(h_ref, gate_ref, p_ref, wg_ref, wp_ref, gple_ref, gfin_ref, *refs, final_norm):
    y_refs, o_ref = refs[:TOP_K], refs[TOP_K]
    gates = gate_ref[...]
    moe = gates[:, 0:1] * _unpack_rows(y_refs[0][...])
    for k in range(1, TOP_K):
        moe = moe + gates[:, k:k + 1] * _unpack_rows(y_refs[k][...])
    h = h_ref[...] + moe
    gate = jax.nn.sigmoid(jnp.dot(_rms(h, gple_ref[...]).astype(BF16), wg_ref[...],
                                  preferred_element_type=F32))
    emb = jnp.dot(p_ref[...].astype(BF16), wp_ref[...], preferred_element_type=F32)
    h = h + gate * emb
    o_ref[...] = _rms(h, gfin_ref[...]) if final_norm else h


def _combine(h, gates, p, w_gate, w_proj, g_ple, g_final, y_tok, final_norm):
    n, d = h.shape
    tm = TOKEN_TILE
    tile = lambda width: pl.BlockSpec((tm, width), lambda i: (i, 0))
    full = lambda arr: pl.BlockSpec(arr.shape, lambda i: (0,) * arr.ndim)
    choice = lambda k: pl.BlockSpec((tm, y_tok.shape[1]), lambda i: (k * (n // tm) + i, 0))
    return pl.pallas_call(
        functools.partial(_combine_kernel, final_norm=final_norm),
        grid=(n // tm,),
        in_specs=[tile(d), tile(LANES), tile(p.shape[1]), full(w_gate), full(w_proj),
                  full(g_ple), full(g_final)] + [choice(k) for k in range(TOP_K)],
        out_specs=tile(d),
        out_shape=jax.ShapeDtypeStruct((n, d), F32),
        compiler_params=_params("arbitrary"),
        name="combine",
    )(h, gates, p, w_gate, w_proj, g_ple, g_final, *([y_tok] * TOP_K))


def _layer(h, p, positions, g_attn, w_in, lq1, lk1, lq2, lk2, g_subln, b_forget, w_out, g_moe,
           w_router, b_router, w_e1, b_e1, w_e2, b_e2, g_ple, w_ple_gate, w_ple_proj, g_final,
           lambda_init, final_norm):
    b, s, d = h.shape
    n = b * s
    row = lambda v: v.reshape(1, -1)

    w_main = w_in[:, :MAIN_WIDTH].astype(BF16)
    pad_heads = ((0, 2 * N_FOX_HEADS - N_FOX_HEADS), (0, 0))
    w_fz_t = jnp.pad(w_in[:, MAIN_WIDTH:].T, pad_heads).astype(BF16)
    b_fz_t = jnp.pad(b_forget.reshape(-1, 1), pad_heads)
    b1 = b_e1[:, None, :]
    b2 = b_e2[:, None, :]

    dq_t, dk, dv_t, fq_t, fk, fv_t, c, c_t = _inproj(positions, h, row(g_attn), w_main, w_fz_t, b_fz_t)
    c_rows = c_t.reshape(b, N_FOX_HEADS // 2, 2, s)
    d_out, f_out = _attention(dq_t, dk, dv_t, fq_t, fk, fv_t, c_rows, c, row(lq1), row(lk1), row(lq2),
                              row(lk2), g_subln.reshape(-1, 1), lambda_init)

    h1, xn, gates, idx, rank, counts = _router(
        d_out.reshape(n, DIFF_WIDTH), f_out.reshape(n, FOX_WIDTH), h.reshape(n, d),
        w_out.astype(BF16), row(g_moe), w_router.T, b_router.reshape(-1, 1))

    te = EXPERT_TILE
    n_rows = n * TOP_K + N_EXPERTS * te
    counts = counts[:, 0].astype(I32)
    padded = (counts + te - 1) // te * te
    ends = jnp.cumsum(padded)
    starts = ends - padded
    experts = jnp.arange(N_EXPERTS, dtype=I32)[:, None, None]
    pos = rank[:TOP_K] + jnp.sum(jnp.where(idx[None, :TOP_K] == experts, starts[:, None, None], 0),
                                 axis=0)
    tile_start = jnp.arange(n_rows // te, dtype=I32) * te
    tile_expert = jnp.minimum(jnp.sum((ends[None, :] <= tile_start[:, None]).astype(I32), axis=1),
                              N_EXPERTS - 1)
    n_used = (ends[-1:] // te).astype(I32)
    owns = padded > 0
    later = lax.cummin(jnp.where(owns, jnp.arange(N_EXPERTS, dtype=I32), N_EXPERTS), reverse=True)
    following = jnp.concatenate([later[1:], jnp.full((1,), N_EXPERTS, I32)])
    following = jnp.where(following < N_EXPERTS, following, -1)
    next_expert = following[tile_expert]
    slot = ((jnp.cumsum(owns.astype(I32)) - 1) % 2)[tile_expert]

    xs = _dispatch_sparsecore(pos.astype(I32), xn, n_rows)
    y = _experts(tile_expert, n_used, next_expert.astype(I32), slot.astype(I32), xs, w_e1, b1, w_e2, b2)
    y_tok = _gather_sparsecore(pos.astype(I32), y)
    out = _combine(h1, gates, p.reshape(n, PLE_DIM), w_ple_gate.astype(BF16),
                   w_ple_proj.astype(BF16), row(g_ple), row(g_final), y_tok, final_norm)
    return out.reshape(b, s, d)


def kernel(x, p, positions, g_attn, w_in, lambda_q1, lambda_k1, lambda_q2, lambda_k2, g_subln, b_forget, w_out, g_moe, w_router, b_router, w_e1, b_e1, w_e2, b_e2, g_ple, w_ple_gate, w_ple_proj, g_final):
    depth = g_attn.shape[0]
    h = x
    for i in range(depth):
        lambda_init = 0.8 - 0.6 * math.exp(-0.3 * i)
        h = _layer(h, p[i], positions, g_attn[i], w_in[i], lambda_q1[i], lambda_k1[i],
                   lambda_q2[i], lambda_k2[i], g_subln[i], b_forget[i], w_out[i], g_moe[i],
                   w_router[i], b_router[i], w_e1[i], b_e1[i], w_e2[i], b_e2[i], g_ple[i],
                   w_ple_gate[i], w_ple_proj[i], g_final, lambda_init, i == depth - 1)
    return h
```

```python
import functools
import math

import jax
import jax.numpy as jnp
from jax import lax
from jax.experimental import pallas as pl
from jax.experimental.pallas import tpu as pltpu
from jax.experimental.pallas import tpu_sc as plsc

F32 = jnp.float32
BF16 = jnp.bfloat16
I32 = jnp.int32
U32 = jnp.uint32

LANES = 128
D_MODEL = 1024
HEAD_DIM = 64
N_DIFF_HEADS = 4
DIFF_V_DIM = 2 * HEAD_DIM
N_FOX_HEADS = 8
DIFF_WIDTH = N_DIFF_HEADS * DIFF_V_DIM
FOX_WIDTH = N_FOX_HEADS * HEAD_DIM
ROT_DIM = HEAD_DIM // 4
ROPE_THETA = 500000.0
N_EXPERTS = 32
TOP_K = 4
D_EXPERT = D_MODEL
SWIGLU_ALPHA = 1.702
SWIGLU_LIMIT = 7.0
PLE_DIM = 256
NORM_EPS = 1e-5
LOG2_E = math.log2(math.e)
MAIN_WIDTH = 3 * DIFF_WIDTH + 3 * FOX_WIDTH

SEQ_TILE = 256
ATTN_TILE = 256
ATTN_UNITS = 2
TOKEN_TILE = 256
ROUTER_TILE = 256
ROUTER_SUBTILE = 256
EXPERT_TILE = 256
VMEM_LIMIT = 48 * 1024 * 1024
EXPERT_VMEM_LIMIT = 58 * 1024 * 1024


def _params(*semantics):
    return pltpu.CompilerParams(dimension_semantics=semantics, vmem_limit_bytes=VMEM_LIMIT)


def _rms(x, g):
    return x * lax.rsqrt(jnp.mean(x * x, axis=-1, keepdims=True) + NORM_EPS) * g


def _pack_rows(v):
    half = v.shape[1] // 2
    bits = pltpu.bitcast(v.astype(BF16).astype(F32), U32)
    return (bits[:, :half] >> 16) | bits[:, half:]


def _unpack_rows(words):
    low = pltpu.bitcast(words << 16, F32)
    high = pltpu.bitcast(words & jnp.uint32(0xFFFF0000), F32)
    return jnp.concatenate([low, high], axis=1)


def _inproj_kernel(pos_ref, x_ref, g_ref, w_ref, wfzt_ref, bfzt_ref,
                   dqt_ref, dk_ref, dvt_ref, fqt_ref, fk_ref, fvt_ref, c_ref, ct_ref, carry_t_ref):
    ts = x_ref.shape[1]
    xb = _rms(x_ref[0], g_ref[...]).astype(BF16)
    proj = jnp.dot(xb, w_ref[...], preferred_element_type=F32)
    nt = (((1,), (1,)), ((), ()))
    tn = (((0,), (0,)), ((), ()))

    def pieces(v):
        hi = v.astype(BF16)
        r1 = v - hi.astype(F32)
        mid = r1.astype(BF16)
        return hi, mid, (r1 - mid.astype(F32)).astype(BF16)

    def rows_to_lanes(v, select):
        return sum(lax.dot_general(part, select, tn, preferred_element_type=F32) for part in pieces(v))

    half_rot = ROT_DIM // 2
    freq = lax.broadcasted_iota(I32, (half_rot, 1), 0)
    inv_freq = jnp.power(ROPE_THETA, -(2 * freq).astype(F32) / ROT_DIM)
    ang = inv_freq * pos_ref[0].astype(F32)
    trig = jnp.concatenate([jnp.cos(ang), jnp.sin(ang)], axis=0)
    r = lax.broadcasted_iota(I32, (2 * half_rot, 2 * LANES), 0)
    lane2 = lax.broadcasted_iota(I32, (2 * half_rot, 2 * LANES), 1)
    d = lane2 % HEAD_DIM
    same_freq = jnp.where(d % half_rot == r % half_rot, 1.0, 0.0)
    cos_part = jnp.where((r < half_rot) & (lane2 < LANES) & (d < ROT_DIM), same_freq, 0.0)
    sin_part = jnp.where((r >= half_rot) & (lane2 >= LANES) & (d < ROT_DIM),
                         jnp.where(d < half_rot, -same_freq, same_freq), 0.0)
    tables = rows_to_lanes(trig, (cos_part + sin_part).astype(BF16))
    lane = lax.broadcasted_iota(I32, (1, LANES), 1) % HEAD_DIM
    cosf = jnp.where(lane < ROT_DIM, tables[:, :LANES], 1.0)
    sinf = tables[:, LANES:]
    first_half = lane < half_rot

    def rope(t):
        outs = []
        for j in range(t.shape[1] // LANES):
            tj = t[:, j * LANES:(j + 1) * LANES]
            partner = jnp.where(first_half,
                                pltpu.roll(tj, LANES - ROT_DIM // 2, 1),
                                pltpu.roll(tj, ROT_DIM // 2, 1))
            outs.append(tj * cosf + partner * sinf)
        return jnp.concatenate(outs, axis=1)

    scale = HEAD_DIM ** -0.5 * LOG2_E
    w = DIFF_WIDTH
    dqt_ref[0] = (rope(proj[:, 0:w]) * scale).astype(BF16).T
    dk_ref[0] = rope(proj[:, w:2 * w]).astype(BF16)
    dvt_ref[0, 0] = proj[:, 2 * w:3 * w].astype(BF16).T
    fqt_ref[0] = (proj[:, 3 * w:4 * w] * scale).astype(BF16).T
    fk_ref[0] = proj[:, 4 * w:5 * w].astype(BF16)
    fvt_ref[0, 0] = proj[:, 5 * w:6 * w].astype(BF16).T

    def log_sigmoid(z):
        return jnp.minimum(z, 0.0) - jnp.log1p(jnp.exp(-jnp.abs(z)))

    row = lax.broadcasted_iota(I32, (ts, ts), 0)
    col = lax.broadcasted_iota(I32, (ts, ts), 1)
    upto_col = jnp.where(row <= col, 1.0, 0.0).astype(BF16)

    @pl.when(pl.program_id(1) == 0)
    def _():
        carry_t_ref[...] = jnp.zeros_like(carry_t_ref)

    fzt = lax.dot_general(wfzt_ref[...], xb, nt, preferred_element_type=F32) + bfzt_ref[...]
    ct = carry_t_ref[...] + sum(jnp.dot(part, upto_col, preferred_element_type=F32)
                                for part in pieces(log_sigmoid(fzt)))
    carry_t_ref[...] = ct[:, ts - 1:ts]
    ct = ct * LOG2_E
    ct_ref[0] = ct[0:N_FOX_HEADS, :]
    head = lax.broadcasted_iota(I32, (ct.shape[0], FOX_WIDTH), 0)
    group = lax.broadcasted_iota(I32, (ct.shape[0], FOX_WIDTH), 1) // HEAD_DIM
    c_ref[0] = rows_to_lanes(ct, jnp.where(head == group, 1.0, 0.0).astype(BF16))


def _inproj(positions, x, g, w_main, w_fz_t, b_fz_t):
    b, s, d = x.shape
    ts = SEQ_TILE
    assert ts == ATTN_TILE
    tile = lambda width: pl.BlockSpec((1, ts, width), lambda i, j: (i, j, 0))
    tile_t = pl.BlockSpec((1, DIFF_WIDTH, ts), lambda i, j: (i, 0, j))
    key_tile_t = pl.BlockSpec((1, 1, DIFF_WIDTH, ts), lambda i, j: (i, j, 0, 0))
    full = lambda arr: pl.BlockSpec(arr.shape, lambda i, j: (0,) * arr.ndim)
    rows = jax.ShapeDtypeStruct((b, s, DIFF_WIDTH), BF16)
    rows_t = jax.ShapeDtypeStruct((b, DIFF_WIDTH, s), BF16)
    key_tiles_t = jax.ShapeDtypeStruct((b, s // ts, DIFF_WIDTH, ts), BF16)
    return pl.pallas_call(
        _inproj_kernel,
        grid=(b, s // ts),
        in_specs=[pl.BlockSpec((1, 1, ts), lambda i, j: (i, 0, j)), tile(d), full(g), full(w_main),
                  full(w_fz_t), full(b_fz_t)],
        out_specs=[tile_t, tile(DIFF_WIDTH), key_tile_t, tile_t, tile(FOX_WIDTH), key_tile_t,
                   tile(FOX_WIDTH), pl.BlockSpec((1, N_FOX_HEADS, ts), lambda i, j: (i, 0, j))],
        out_shape=[rows_t, rows, key_tiles_t, rows_t, rows, key_tiles_t,
                   jax.ShapeDtypeStruct((b, s, FOX_WIDTH), F32),
                   jax.ShapeDtypeStruct((b, N_FOX_HEADS, s), F32)],
        scratch_shapes=[pltpu.VMEM((w_fz_t.shape[0], 1), F32)],
        compiler_params=_params("arbitrary", "arbitrary"),
        name="inproj",
    )(positions.reshape(b, 1, s), x, g, w_main, w_fz_t, b_fz_t)


ONES_ROWS = 16


def _block_diag_queries(qt):
    row = lax.broadcasted_iota(I32, qt.shape, 0)
    zero = jnp.zeros_like(qt)
    return jnp.concatenate([jnp.where(row < HEAD_DIM, qt, zero),
                            jnp.where(row >= HEAD_DIM, qt, zero)], axis=1)


class _Chain:
    def __init__(self, qbd, keys, values_t, scratch, key_bias=None, query_bias=None):
        self.qbd, self.keys, self.values_t = qbd, keys, values_t
        self.acc_ref, self.s_ref, self.p_ref, self.stat_ref = scratch
        self.key_bias, self.query_bias = key_bias, query_bias


_RUNNING_MAX, _TILE_MAX, _RESCALE = 0, 1, 2


def _causal_attention(chains, qi):
    tk, tq = chains[0].s_ref.shape[0], chains[0].s_ref.shape[1] // 2
    ones = jnp.ones((ONES_ROWS, tk), BF16)

    def scores(ch, i, diagonal):
        s = jnp.dot(ch.keys(i), ch.qbd, preferred_element_type=F32)
        if ch.key_bias is not None:
            s = s - ch.key_bias(i)
        if diagonal is not False:
            r = lax.broadcasted_iota(I32, s.shape, 0)
            c = lax.broadcasted_iota(I32, s.shape, 1) & (tq - 1)
            visible = r <= c if diagonal is True else jnp.logical_or(r <= c, jnp.logical_not(diagonal))
            s = jnp.where(visible, s, -jnp.inf)
        ch.s_ref[...] = s
        top = jnp.max(s, axis=0, keepdims=True)
        if ch.query_bias is not None:
            top = top + ch.query_bias
        ch.stat_ref[_TILE_MAX:_TILE_MAX + 1, :] = top

    def probs(ch):
        m = ch.stat_ref[_RUNNING_MAX:_RUNNING_MAX + 1, :]
        m_new = jnp.maximum(m, ch.stat_ref[_TILE_MAX:_TILE_MAX + 1, :])
        ch.stat_ref[_RUNNING_MAX:_RUNNING_MAX + 1, :] = m_new
        ch.stat_ref[_RESCALE:_RESCALE + 1, :] = jnp.exp2(m - m_new)
        ref = m_new if ch.query_bias is None else m_new - ch.query_bias
        ch.p_ref[...] = jnp.exp2(ch.s_ref[...] - ref).astype(BF16)

    def accumulate(ch, i):
        vts = jnp.concatenate([ch.values_t(i), ones], axis=0)
        ch.acc_ref[...] = (ch.stat_ref[_RESCALE:_RESCALE + 1, :] * ch.acc_ref[...]
                           + jnp.dot(vts, ch.p_ref[...], preferred_element_type=F32))

    def body(k, diagonal_next):
        for ch in chains:
            accumulate(ch, jnp.maximum(k - 1, 0))
        for ch in chains:
            probs(ch)
        for ch in chains:
            scores(ch, k + 1, diagonal_next)

    for ch in chains:
        ch.acc_ref[...] = jnp.zeros_like(ch.acc_ref)
        ch.p_ref[...] = jnp.zeros_like(ch.p_ref)
        ch.stat_ref[...] = jnp.concatenate(
            [jnp.full((1, 2 * tq), -jnp.inf, F32), jnp.ones((ch.stat_ref.shape[0] - 1, 2 * tq), F32)], axis=0)
        scores(ch, 0, qi == 0)

    def loop_body(k, carry):
        body(k, False)
        return carry

    lax.fori_loop(0, qi - 1, loop_body, 0)

    @pl.when(qi >= 1)
    def _():
        body(qi - 1, True)

    for ch in chains:
        accumulate(ch, jnp.maximum(qi - 1, 0))
    for ch in chains:
        probs(ch)
    for ch in chains:
        accumulate(ch, qi)


def _attention_kernel(lq1_ref, lk1_ref, lq2_ref, lk2_ref, g_ref,
                      dqt_ref, dk_ref, dvt_ref, fqt_ref, fk_ref, fvt_ref, cq_ref, ck_ref,
                      do_ref, fo_ref, *scratch, lambda_init):
    tq = dqt_ref.shape[2]
    tk = dvt_ref.shape[3]
    units = dqt_ref.shape[1] // LANES
    lane = lax.broadcasted_iota(I32, (1, LANES), 1)
    scratch = [scratch[i:i + 4] for i in range(0, len(scratch), 4)]

    def unit_chains(u):
        lanes = slice(u * LANES, (u + 1) * LANES)

        def tile(ref):
            return lambda i: ref[0, pl.ds(pl.multiple_of(i * tk, tk), tk), lanes]

        def tile_t(ref):
            return lambda i: ref[0, i, lanes, :]

        cq = jnp.concatenate([cq_ref[0, u, 0:1, :], cq_ref[0, u, 1:2, :]], axis=1)

        def key_bias(i):
            ck = tile(ck_ref)(i)
            other = pltpu.roll(ck, HEAD_DIM, 1)
            first = jnp.where(lane < HEAD_DIM, ck, other)
            second = jnp.where(lane >= HEAD_DIM, ck, other)
            return jnp.concatenate([first] * (tq // LANES) + [second] * (tq // LANES), axis=1)

        return [_Chain(_block_diag_queries(dqt_ref[0, lanes, :]), tile(dk_ref), tile_t(dvt_ref),
                       scratch[2 * u]),
                _Chain(_block_diag_queries(fqt_ref[0, lanes, :]), tile(fk_ref), tile_t(fvt_ref),
                       scratch[2 * u + 1], key_bias, cq)]

    chains = [ch for u in range(units) for ch in unit_chains(u)]
    _causal_attention(chains, pl.program_id(2))

    lam = (jnp.exp(jnp.sum(lq1_ref[...] * lk1_ref[...], axis=-1, keepdims=True))
           - jnp.exp(jnp.sum(lq2_ref[...] * lk2_ref[...], axis=-1, keepdims=True)) + lambda_init)
    dv, h = DIFF_V_DIM, HEAD_DIM
    for u in range(units):
        lanes = slice(u * LANES, (u + 1) * LANES)
        acc = chains[2 * u].acc_ref
        o = (acc[0:dv, 0:tq] / acc[dv:dv + 1, 0:tq]
             - lam * (acc[0:dv, tq:] / acc[dv:dv + 1, tq:]))
        ms = jnp.mean(o * o, axis=0, keepdims=True)
        do_ref[0, :, lanes] = (o * lax.rsqrt(ms + NORM_EPS) * g_ref[...]
                               * (1.0 - lambda_init)).T.astype(BF16)
        acc = chains[2 * u + 1].acc_ref
        fo_ref[0, :, lanes] = jnp.concatenate([acc[0:h, 0:tq] / acc[2 * h:2 * h + 1, 0:tq],
                                               acc[h:2 * h, tq:] / acc[2 * h:2 * h + 1, tq:]],
                                              axis=0).T.astype(BF16)


def _attention(dqt, dk, dvt, fqt, fk, fvt, c_rows, c_cols, lq1, lk1, lq2, lk2, g_col, lambda_init):
    b, s, _ = dk.shape
    t = ATTN_TILE
    assert N_DIFF_HEADS == N_FOX_HEADS // 2 and N_DIFF_HEADS % ATTN_UNITS == 0
    width = ATTN_UNITS * LANES
    vec = lambda arr: pl.BlockSpec(arr.shape, lambda i, h, j: (0, 0))
    q_spec = pl.BlockSpec((1, width, t), lambda i, h, j: (i, h, j))
    k_spec = pl.BlockSpec((1, s, width), lambda i, h, j: (i, 0, h))
    v_spec = pl.BlockSpec((1, s // t, width, t), lambda i, h, j: (i, 0, h, 0))
    o_spec = pl.BlockSpec((1, t, width), lambda i, h, j: (i, j, h))
    out = jax.ShapeDtypeStruct((b, s, DIFF_WIDTH), BF16)
    chain_scratch = [pltpu.VMEM((LANES + ONES_ROWS, 2 * t), F32),
                     pltpu.VMEM((t, 2 * t), F32),
                     pltpu.VMEM((t, 2 * t), BF16),
                     pltpu.VMEM((8, 2 * t), F32)]
    return pl.pallas_call(
        functools.partial(_attention_kernel, lambda_init=lambda_init),
        grid=(b, N_DIFF_HEADS // ATTN_UNITS, s // t),
        in_specs=[vec(lq1), vec(lk1), vec(lq2), vec(lk2), vec(g_col),
                  q_spec, k_spec, v_spec, q_spec, k_spec, v_spec,
                  pl.BlockSpec((1, ATTN_UNITS, 2, t), lambda i, h, j: (i, h, 0, j)), k_spec],
        out_specs=[o_spec, o_spec],
        out_shape=[out, out],
        scratch_shapes=chain_scratch * (2 * ATTN_UNITS),
        compiler_params=_params("arbitrary", "arbitrary", "arbitrary"),
        name="attention",
    )(lq1, lk1, lq2, lk2, g_col, dqt, dk, dvt, fqt, fk, fvt, c_rows, c_cols)


def _router_kernel(d_ref, f_ref, x_ref, wo_ref, g_ref, wrt_ref, brt_ref,
                   h_ref, xn_ref, gate_ref, idx_ref, rank_ref, count_ref, carry_ref):
    tm = ROUTER_SUBTILE
    nt = (((1,), (1,)), ((), ()))

    def pieces(v, count):
        out = []
        for _ in range(count):
            part = v.astype(BF16)
            out.append(part)
            v = v - part.astype(F32)
        return out

    wh, wl = pieces(wrt_ref[...], 2)
    row = lax.broadcasted_iota(I32, (tm, tm), 0)
    col = lax.broadcasted_iota(I32, (tm, tm), 1)
    earlier = jnp.where(row < col, 1.0, 0.0).astype(BF16)
    eye = jnp.where(row == col, 1.0, 0.0).astype(BF16)

    @pl.when(pl.program_id(0) == 0)
    def _():
        carry_ref[...] = jnp.zeros_like(carry_ref)

    for sub in range(x_ref.shape[0] // tm):
        rows = slice(sub * tm, (sub + 1) * tm)
        mixed = (jnp.dot(d_ref[rows, :], wo_ref[0:DIFF_WIDTH, :], preferred_element_type=F32)
                 + jnp.dot(f_ref[rows, :], wo_ref[DIFF_WIDTH:, :], preferred_element_type=F32))
        h = x_ref[rows, :] + mixed
        h_ref[rows, :] = h
        xn = _rms(h, g_ref[...])
        xn_ref[rows, :] = _pack_rows(xn)

        xh, xl = pieces(xn, 2)
        work = (lax.dot_general(wh, xh, nt, preferred_element_type=F32)
                + lax.dot_general(wh, xl, nt, preferred_element_type=F32)
                + lax.dot_general(wl, xh, nt, preferred_element_type=F32)) + brt_ref[...]
        expert = lax.broadcasted_iota(I32, work.shape, 0)

        vals, ids, picks = [], [], []
        for _ in range(TOP_K):
            m = jnp.max(work, axis=0, keepdims=True)
            idx = jnp.min(jnp.where(work == m, expert, N_EXPERTS), axis=0, keepdims=True)
            pick = expert == idx
            vals.append(m)
            ids.append(idx)
            picks.append(pick)
            work = jnp.where(pick, -jnp.inf, work)
        exps = [jnp.exp(v - vals[0]) for v in vals]
        denom = exps[0] + exps[1] + exps[2] + exps[3]

        chosen = jnp.where(picks[0] | picks[1] | picks[2] | picks[3], 1.0, 0.0)
        rank = jnp.dot(chosen.astype(BF16), earlier, preferred_element_type=F32) + carry_ref[...]
        carry_ref[...] = carry_ref[...] + jnp.sum(chosen, axis=1, keepdims=True)

        slot = lax.broadcasted_iota(I32, (8, tm), 0)
        gate_t = jnp.zeros((8, tm), F32)
        idx_t = jnp.zeros((8, tm), I32)
        rank_t = jnp.zeros((8, tm), F32)
        for k in range(TOP_K):
            gate_t = jnp.where(slot == k, exps[k] / denom, gate_t)
            idx_t = jnp.where(slot == k, ids[k], idx_t)
            rank_t = jnp.where(slot == k, jnp.sum(jnp.where(picks[k], rank, 0.0), axis=0, keepdims=True),
                               rank_t)
        idx_ref[:, rows] = idx_t
        rank_ref[:, rows] = rank_t.astype(I32)
        gate_rows = jnp.concatenate([gate_t, jnp.zeros((LANES - 8, tm), F32)], axis=0)
        gate_ref[rows, :] = sum(lax.dot_general(eye, part, nt, preferred_element_type=F32)
                                for part in pieces(gate_rows, 3))

    count_ref[...] = jnp.broadcast_to(carry_ref[...], count_ref.shape)


def _router(d_out, f_out, x, w_out, g_moe, w_router_t, b_router_t):
    n, d = x.shape
    tm = ROUTER_TILE
    tile = lambda width: pl.BlockSpec((tm, width), lambda i: (i, 0))
    rows = pl.BlockSpec((8, tm), lambda i: (0, i))
    full = lambda arr: pl.BlockSpec(arr.shape, lambda i: (0,) * arr.ndim)
    return pl.pallas_call(
        _router_kernel,
        grid=(n // tm,),
        in_specs=[tile(DIFF_WIDTH), tile(FOX_WIDTH), tile(d), full(w_out), full(g_moe),
                  full(w_router_t), full(b_router_t)],
        out_specs=[tile(d), tile(d // 2), tile(LANES), rows, rows,
                   pl.BlockSpec((N_EXPERTS, LANES), lambda i: (0, 0))],
        out_shape=[jax.ShapeDtypeStruct((n, d), F32), jax.ShapeDtypeStruct((n, d // 2), U32),
                   jax.ShapeDtypeStruct((n, LANES), F32), jax.ShapeDtypeStruct((8, n), I32),
                   jax.ShapeDtypeStruct((8, n), I32), jax.ShapeDtypeStruct((N_EXPERTS, LANES), F32)],
        scratch_shapes=[pltpu.VMEM((N_EXPERTS, 1), F32)],
        compiler_params=_params("arbitrary"),
        name="router",
    )(d_out, f_out, x, w_out, g_moe, w_router_t, b_router_t)


def _slot_kernel(starts_ref, idx_ref, rank_ref, pos_ref):
    idx = idx_ref[...]
    pos = rank_ref[...]
    for e in range(N_EXPERTS):
        pos = jnp.where(idx == e, pos + starts_ref[e], pos)
    pos_ref[...] = pos


def _slots(starts, idx, rank):
    rows, n = idx.shape
    width = min(n, 2048)
    block = pl.BlockSpec((rows, width), lambda i, starts: (0, i))
    return pl.pallas_call(
        _slot_kernel,
        grid_spec=pltpu.PrefetchScalarGridSpec(num_scalar_prefetch=1, grid=(n // width,),
                                               in_specs=[block, block], out_specs=block),
        out_shape=jax.ShapeDtypeStruct((rows, n), I32),
        compiler_params=_params("arbitrary"),
        name="slots",
    )(starts, idx, rank)


SC_WINDOW = 128
SC_COLUMNS = 256


def _dispatch_sparsecore(pos, xn, n_rows):
    n, d = xn.shape
    windows = n // SC_WINDOW
    mesh = plsc.VectorSubcoreMesh(core_axis_name="core", subcore_axis_name="subcore")

    @functools.partial(pl.kernel, out_type=jax.ShapeDtypeStruct((n_rows, d), xn.dtype), mesh=mesh,
                       scratch_types=[], name="dispatch_sc")
    def scatter(x_hbm, pos_hbm, xs_hbm):
        for c in range(d // SC_COLUMNS):
            def body(x_vmem, pos_vmem, c=c):
                pltpu.sync_copy(x_vmem, xs_hbm.at[:, pl.ds(c * SC_COLUMNS, SC_COLUMNS)].at[pos_vmem.at[0]])

            pltpu.emit_pipeline(
                body, grid=(TOP_K * windows,),
                in_specs=[pl.BlockSpec((SC_WINDOW, SC_COLUMNS), index_map=lambda i, c=c: (i % windows, c)),
                          pl.BlockSpec((1, SC_WINDOW), index_map=lambda i: (0, i))],
                out_specs=[], core_axis_name=("core", "subcore"),
                dimension_semantics=(pltpu.PARALLEL,))(x_hbm, pos_hbm)

    return scatter(xn, pos.reshape(1, -1))


def _expert_kernel(tile_expert_ref, n_used_ref, next_expert_ref, slot_ref,
                   x_ref, w1_ref, b1_ref, w2_ref, b2_ref, y_ref,
                   w1f_ref, w2f_ref, sem_ref, w1b_ref, w2p_ref, w2b_ref):
    t = pl.program_id(0)
    fresh = (t == 0) | (tile_expert_ref[t] != tile_expert_ref[jnp.maximum(t - 1, 0)])
    used = t < n_used_ref[0]

    def weight_copies(e, s):
        return (pltpu.make_async_copy(w1_ref.at[e], w1f_ref.at[s], sem_ref.at[0, s]),
                pltpu.make_async_copy(w2_ref.at[e], w2f_ref.at[s], sem_ref.at[1, s]))

    @pl.when(t == 0)
    def _():
        for copy in weight_copies(tile_expert_ref[0], slot_ref[0]):
            copy.start()

    @pl.when(fresh & used)
    def _():
        s = slot_ref[t]
        for copy in weight_copies(tile_expert_ref[t], s):
            copy.wait()

        @pl.when(next_expert_ref[t] >= 0)
        def _():
            for copy in weight_copies(next_expert_ref[t], 1 - s):
                copy.start()

        rows = LANES
        half = LANES // 2
        for r in range(0, w1b_ref.shape[0], rows):
            w1b_ref[r:r + rows, :] = w1f_ref[s, r:r + rows, :].astype(BF16)
        for c in range(w2p_ref.shape[0]):
            lanes = slice(c * LANES, (c + 1) * LANES)
            for r in range(0, w2b_ref.shape[0], rows):
                for b in range(2):
                    w2p_ref[c, pl.ds(r + b, half, stride=2), :] = (
                        w2f_ref[s, r + b * half:r + (b + 1) * half, lanes])
            w2b_ref[:, lanes] = w2p_ref[c].astype(BF16)

    @pl.when(used)
    def _():
        xb = _unpack_rows(x_ref[...]).astype(BF16)
        h = jnp.dot(xb, w1b_ref[...], preferred_element_type=F32) + b1_ref[0]
        even = lax.broadcasted_iota(I32, (1, LANES), 1) % 2 == 0
        parts = []
        for q in range(h.shape[1] // (2 * LANES)):
            a = h[:, (2 * q) * LANES:(2 * q + 1) * LANES]
            b = h[:, (2 * q + 1) * LANES:(2 * q + 2) * LANES]
            hg = jnp.where(even, a, pltpu.roll(b, 1, 1))
            hl = jnp.where(even, pltpu.roll(a, LANES - 1, 1), b)
            glu = jnp.minimum(hg, SWIGLU_LIMIT)
            lin = jnp.clip(hl, -SWIGLU_LIMIT, SWIGLU_LIMIT)
            parts.append((glu * jax.nn.sigmoid(SWIGLU_ALPHA * glu) * (lin + 1.0)).astype(BF16))
        act = jnp.concatenate(parts, axis=1)
        y_ref[...] = _pack_rows(jnp.dot(act, w2b_ref[...], preferred_element_type=F32) + b2_ref[0])

    @pl.when(jnp.logical_not(used))
    def _():
        y_ref[...] = jnp.zeros_like(y_ref)


def _experts(tile_expert, n_used, next_expert, slot, xs, w1, b1, w2, b2):
    n_rows, d = xs.shape
    tm = EXPERT_TILE
    by_expert = lambda arr: pl.BlockSpec((1,) + arr.shape[1:], lambda i, te, *_: (te[i], 0, 0))
    rows = pl.BlockSpec((tm, d), lambda i, *_: (i, 0))
    used_rows = pl.BlockSpec((tm, d), lambda i, te, nu, *_: (jnp.minimum(i, nu[0] - 1), 0))
    hbm = pl.BlockSpec(memory_space=pl.ANY)
    return pl.pallas_call(
        _expert_kernel,
        grid_spec=pltpu.PrefetchScalarGridSpec(
            num_scalar_prefetch=4,
            grid=(n_rows // tm,),
            in_specs=[used_rows, hbm, by_expert(b1), hbm, by_expert(b2)],
            out_specs=rows,
            scratch_shapes=[pltpu.VMEM((2,) + w1.shape[1:], F32), pltpu.VMEM((2,) + w2.shape[1:], F32),
                            pltpu.SemaphoreType.DMA((2, 2)),
                            pltpu.VMEM(w1.shape[1:], BF16),
                            pltpu.VMEM((w2.shape[2] // LANES, w2.shape[1], LANES), F32),
                            pltpu.VMEM(w2.shape[1:], BF16)]),
        out_shape=jax.ShapeDtypeStruct(xs.shape, xs.dtype),
        compiler_params=pltpu.CompilerParams(dimension_semantics=("arbitrary",),
                                             vmem_limit_bytes=EXPERT_VMEM_LIMIT),
        name="experts",
    )(tile_expert, n_used, next_expert, slot, xs, w1, b1, w2, b2)


def _gather_sparsecore(pos, y):
    d = y.shape[1]
    total = pos.size
    mesh = plsc.VectorSubcoreMesh(core_axis_name="core", subcore_axis_name="subcore")

    @functools.partial(pl.kernel, out_type=jax.ShapeDtypeStruct((total, d), y.dtype), mesh=mesh,
                       scratch_types=[], name="gather_sc")
    def gather(y_hbm, pos_hbm, out_hbm):
        for c in range(d // SC_COLUMNS):
            def body(pos_vmem, out_vmem, c=c):
                pltpu.sync_copy(y_hbm.at[:, pl.ds(c * SC_COLUMNS, SC_COLUMNS)].at[pos_vmem.at[0]], out_vmem)

            pltpu.emit_pipeline(
                body, grid=(total // SC_WINDOW,),
                in_specs=[pl.BlockSpec((1, SC_WINDOW), index_map=lambda i: (0, i))],
                out_specs=[pl.BlockSpec((SC_WINDOW, SC_COLUMNS), index_map=lambda i, c=c: (i, c))],
                core_axis_name=("core", "subcore"),
                dimension_semantics=(pltpu.PARALLEL,))(pos_hbm, out_hbm)

    return gather(y, pos.reshape(1, -1))


def _combine_kernel(h_ref, gate_ref, p_ref, wg_ref, wp_ref, gple_ref, gfin_ref, *refs, final_norm):
    y_refs, o_ref = refs[:TOP_K], refs[TOP_K]
    gates = gate_ref[...]
    moe = gates[:, 0:1] * _unpack_rows(y_refs[0][...])
    for k in range(1, TOP_K):
        moe = moe + gates[:, k:k + 1] * _unpack_rows(y_refs[k][...])
    h = h_ref[...] + moe
    gate = jax.nn.sigmoid(jnp.dot(_rms(h, gple_ref[...]).astype(BF16), wg_ref[...],
                                  preferred_element_type=F32))
    emb = jnp.dot(p_ref[...].astype(BF16), wp_ref[...], preferred_element_type=F32)
    h = h + gate * emb
    o_ref[...] = _rms(h, gfin_ref[...]) if final_norm else h


def _combine(h, gates, p, w_gate, w_proj, g_ple, g_final, y_tok, final_norm):
    n, d = h.shape
    tm = TOKEN_TILE
    tile = lambda width: pl.BlockSpec((tm, width), lambda i: (i, 0))
    full = lambda arr: pl.BlockSpec(arr.shape, lambda i: (0,) * arr.ndim)
    choice = lambda k: pl.BlockSpec((tm, y_tok.shape[1]), lambda i: (k * (n // tm) + i, 0))
    return pl.pallas_call(
        functools.partial(_combine_kernel, final_norm=final_norm),
        grid=(n // tm,),
        in_specs=[tile(d), tile(LANES), tile(p.shape[1]), full(w_gate), full(w_proj),
                  full(g_ple), full(g_final)] + [choice(k) for k in range(TOP_K)],
        out_specs=tile(d),
        out_shape=jax.ShapeDtypeStruct((n, d), F32),
        compiler_params=_params("arbitrary"),
        name="combine",
    )(h, gates, p, w_gate, w_proj, g_ple, g_final, *([y_tok] * TOP_K))


def _layer(h, p, positions, g_attn, w_in, lq1, lk1, lq2, lk2, g_subln, b_forget, w_out, g_moe,
           w_router, b_router, w_e1, b_e1, w_e2, b_e2, g_ple, w_ple_gate, w_ple_proj, g_final,
           lambda_init, final_norm):
    b, s, d = h.shape
    n = b * s
    row = lambda v: v.reshape(1, -1)

    w_main = w_in[:, :MAIN_WIDTH].astype(BF16)
    pad_heads = ((0, 2 * N_FOX_HEADS - N_FOX_HEADS), (0, 0))
    w_fz_t = jnp.pad(w_in[:, MAIN_WIDTH:].T, pad_heads).astype(BF16)
    b_fz_t = jnp.pad(b_forget.reshape(-1, 1), pad_heads)
    b1 = b_e1[:, None, :]
    b2 = b_e2[:, None, :]

    dq_t, dk, dv_t, fq_t, fk, fv_t, c, c_t = _inproj(positions, h, row(g_attn), w_main, w_fz_t, b_fz_t)
    c_rows = c_t.reshape(b, N_FOX_HEADS // 2, 2, s)
    d_out, f_out = _attention(dq_t, dk, dv_t, fq_t, fk, fv_t, c_rows, c, row(lq1), row(lk1), row(lq2),
                              row(lk2), g_subln.reshape(-1, 1), lambda_init)

    h1, xn, gates, idx, rank, counts = _router(
        d_out.reshape(n, DIFF_WIDTH), f_out.reshape(n, FOX_WIDTH), h.reshape(n, d),
        w_out.astype(BF16), row(g_moe), w_router.T, b_router.reshape(-1, 1))

    te = EXPERT_TILE
    n_rows = n * TOP_K + N_EXPERTS * te
    counts = counts[:, 0].astype(I32)
    padded = (counts + te - 1) // te * te
    ends = jnp.cumsum(padded)
    starts = ends - padded
    pos = _slots(starts.astype(I32), idx, rank)[:TOP_K]
    tile_start = jnp.arange(n_rows // te, dtype=I32) * te
    tile_expert = jnp.minimum(jnp.sum((ends[None, :] <= tile_start[:, None]).astype(I32), axis=1),
                              N_EXPERTS - 1)
    n_used = (ends[-1:] // te).astype(I32)
    owns = padded > 0
    later = lax.cummin(jnp.where(owns, jnp.arange(N_EXPERTS, dtype=I32), N_EXPERTS), reverse=True)
    following = jnp.concatenate([later[1:], jnp.full((1,), N_EXPERTS, I32)])
    following = jnp.where(following < N_EXPERTS, following, -1)
    next_expert = following[tile_expert]
    slot = ((jnp.cumsum(owns.astype(I32)) - 1) % 2)[tile_expert]

    xs = _dispatch_sparsecore(pos.astype(I32), xn, n_rows)
    y = _experts(tile_expert, n_used, next_expert.astype(I32), slot.astype(I32), xs, w_e1, b1, w_e2, b2)
    y_tok = _gather_sparsecore(pos.astype(I32), y)
    out = _combine(h1, gates, p.reshape(n, PLE_DIM), w_ple_gate.astype(BF16),
                   w_ple_proj.astype(BF16), row(g_ple), row(g_final), y_tok, final_norm)
    return out.reshape(b, s, d)


def kernel(x, p, positions, g_attn, w_in, lambda_q1, lambda_k1, lambda_q2, lambda_k2, g_subln, b_forget, w_out, g_moe, w_router, b_router, w_e1, b_e1, w_e2, b_e2, g_ple, w_ple_gate, w_ple_proj, g_final):
    depth = g_attn.shape[0]
    h = x
    for i in range(depth):
        lambda_init = 0.8 - 0.6 * math.exp(-0.3 * i)
        h = _layer(h, p[i], positions, g_attn[i], w_in[i], lambda_q1[i], lambda_k1[i],
                   lambda_q2[i], lambda_k2[i], g_subln[i], b_forget[i], w_out[i], g_moe[i],
                   w_router[i], b_router[i], w_e1[i], b_e1[i], w_e2[i], b_e2[i], g_ple[i],
                   w_ple_gate[i], w_ple_proj[i], g_final, lambda_init, i == depth - 1)
    return h
```

```python
import functools
import math

import jax
import jax.numpy as jnp
from jax import lax
from jax.experimental import pallas as pl
from jax.experimental.pallas import tpu as pltpu
from jax.experimental.pallas import tpu_sc as plsc

F32 = jnp.float32
BF16 = jnp.bfloat16
I32 = jnp.int32
U32 = jnp.uint32

LANES = 128
D_MODEL = 1024
HEAD_DIM = 64
N_DIFF_HEADS = 4
DIFF_V_DIM = 2 * HEAD_DIM
N_FOX_HEADS = 8
DIFF_WIDTH = N_DIFF_HEADS * DIFF_V_DIM
FOX_WIDTH = N_FOX_HEADS * HEAD_DIM
ROT_DIM = HEAD_DIM // 4
ROPE_THETA = 500000.0
N_EXPERTS = 32
TOP_K = 4
D_EXPERT = D_MODEL
SWIGLU_ALPHA = 1.702
SWIGLU_LIMIT = 7.0
PLE_DIM = 256
NORM_EPS = 1e-5
LOG2_E = math.log2(math.e)
MAIN_WIDTH = 3 * DIFF_WIDTH + 3 * FOX_WIDTH

SEQ_TILE = 256
ATTN_TILE = 256
ATTN_UNITS = 2
TOKEN_TILE = 256
ROUTER_TILE = 256
ROUTER_SUBTILE = 256
EXPERT_TILE = 256
VMEM_LIMIT = 48 * 1024 * 1024
EXPERT_VMEM_LIMIT = 58 * 1024 * 1024


def _params(*semantics):
    return pltpu.CompilerParams(dimension_semantics=semantics, vmem_limit_bytes=VMEM_LIMIT)


def _rms(x, g):
    return x * lax.rsqrt(jnp.mean(x * x, axis=-1, keepdims=True) + NORM_EPS) * g


def _pack_rows(v):
    half = v.shape[1] // 2
    bits = pltpu.bitcast(v.astype(BF16).astype(F32), U32)
    return (bits[:, :half] >> 16) | bits[:, half:]


def _unpack_rows(words):
    low = pltpu.bitcast(words << 16, F32)
    high = pltpu.bitcast(words & jnp.uint32(0xFFFF0000), F32)
    return jnp.concatenate([low, high], axis=1)


def _inproj_kernel(pos_ref, x_ref, g_ref, w_ref, wfzt_ref, bfzt_ref,
                   dqt_ref, dk_ref, dvt_ref, fqt_ref, fk_ref, fvt_ref, c_ref, ct_ref, carry_t_ref):
    ts = x_ref.shape[1]
    xb = _rms(x_ref[0], g_ref[...]).astype(BF16)
    proj = jnp.dot(xb, w_ref[...], preferred_element_type=F32)
    nt = (((1,), (1,)), ((), ()))
    tn = (((0,), (0,)), ((), ()))

    def pieces(v):
        hi = v.astype(BF16)
        r1 = v - hi.astype(F32)
        mid = r1.astype(BF16)
        return hi, mid, (r1 - mid.astype(F32)).astype(BF16)

    def rows_to_lanes(v, select):
        return sum(lax.dot_general(part, select, tn, preferred_element_type=F32) for part in pieces(v))

    half_rot = ROT_DIM // 2
    freq = lax.broadcasted_iota(I32, (half_rot, 1), 0)
    inv_freq = jnp.power(ROPE_THETA, -(2 * freq).astype(F32) / ROT_DIM)
    ang = inv_freq * pos_ref[0].astype(F32)
    trig = jnp.concatenate([jnp.cos(ang), jnp.sin(ang)], axis=0)
    r = lax.broadcasted_iota(I32, (2 * half_rot, 2 * LANES), 0)
    lane2 = lax.broadcasted_iota(I32, (2 * half_rot, 2 * LANES), 1)
    d = lane2 % HEAD_DIM
    same_freq = jnp.where(d % half_rot == r % half_rot, 1.0, 0.0)
    cos_part = jnp.where((r < half_rot) & (lane2 < LANES) & (d < ROT_DIM), same_freq, 0.0)
    sin_part = jnp.where((r >= half_rot) & (lane2 >= LANES) & (d < ROT_DIM),
                         jnp.where(d < half_rot, -same_freq, same_freq), 0.0)
    tables = rows_to_lanes(trig, (cos_part + sin_part).astype(BF16))
    lane = lax.broadcasted_iota(I32, (1, LANES), 1) % HEAD_DIM
    cosf = jnp.where(lane < ROT_DIM, tables[:, :LANES], 1.0)
    sinf = tables[:, LANES:]
    first_half = lane < half_rot

    def rope(t):
        outs = []
        for j in range(t.shape[1] // LANES):
            tj = t[:, j * LANES:(j + 1) * LANES]
            partner = jnp.where(first_half,
                                pltpu.roll(tj, LANES - ROT_DIM // 2, 1),
                                pltpu.roll(tj, ROT_DIM // 2, 1))
            outs.append(tj * cosf + partner * sinf)
        return jnp.concatenate(outs, axis=1)

    scale = HEAD_DIM ** -0.5 * LOG2_E
    w = DIFF_WIDTH
    dqt_ref[0] = (rope(proj[:, 0:w]) * scale).astype(BF16).T
    dk_ref[0] = rope(proj[:, w:2 * w]).astype(BF16)
    dvt_ref[0, 0] = proj[:, 2 * w:3 * w].astype(BF16).T
    fqt_ref[0] = (proj[:, 3 * w:4 * w] * scale).astype(BF16).T
    fk_ref[0] = proj[:, 4 * w:5 * w].astype(BF16)
    fvt_ref[0, 0] = proj[:, 5 * w:6 * w].astype(BF16).T

    def log_sigmoid(z):
        return jnp.minimum(z, 0.0) - jnp.log1p(jnp.exp(-jnp.abs(z)))

    row = lax.broadcasted_iota(I32, (ts, ts), 0)
    col = lax.broadcasted_iota(I32, (ts, ts), 1)
    upto_col = jnp.where(row <= col, 1.0, 0.0).astype(BF16)

    @pl.when(pl.program_id(1) == 0)
    def _():
        carry_t_ref[...] = jnp.zeros_like(carry_t_ref)

    fzt = lax.dot_general(wfzt_ref[...], xb, nt, preferred_element_type=F32) + bfzt_ref[...]
    ct = carry_t_ref[...] + sum(jnp.dot(part, upto_col, preferred_element_type=F32)
                                for part in pieces(log_sigmoid(fzt)))
    carry_t_ref[...] = ct[:, ts - 1:ts]
    ct = ct * LOG2_E
    ct_ref[0] = ct[0:N_FOX_HEADS, :]
    head = lax.broadcasted_iota(I32, (ct.shape[0], FOX_WIDTH), 0)
    group = lax.broadcasted_iota(I32, (ct.shape[0], FOX_WIDTH), 1) // HEAD_DIM
    c_ref[0] = rows_to_lanes(ct, jnp.where(head == group, 1.0, 0.0).astype(BF16))


def _inproj(positions, x, g, w_main, w_fz_t, b_fz_t):
    b, s, d = x.shape
    ts = SEQ_TILE
    assert ts == ATTN_TILE
    tile = lambda width: pl.BlockSpec((1, ts, width), lambda i, j: (i, j, 0))
    tile_t = pl.BlockSpec((1, DIFF_WIDTH, ts), lambda i, j: (i, 0, j))
    key_tile_t = pl.BlockSpec((1, 1, DIFF_WIDTH, ts), lambda i, j: (i, j, 0, 0))
    full = lambda arr: pl.BlockSpec(arr.shape, lambda i, j: (0,) * arr.ndim)
    rows = jax.ShapeDtypeStruct((b, s, DIFF_WIDTH), BF16)
    rows_t = jax.ShapeDtypeStruct((b, DIFF_WIDTH, s), BF16)
    key_tiles_t = jax.ShapeDtypeStruct((b, s // ts, DIFF_WIDTH, ts), BF16)
    return pl.pallas_call(
        _inproj_kernel,
        grid=(b, s // ts),
        in_specs=[pl.BlockSpec((1, 1, ts), lambda i, j: (i, 0, j)), tile(d), full(g), full(w_main),
                  full(w_fz_t), full(b_fz_t)],
        out_specs=[tile_t, tile(DIFF_WIDTH), key_tile_t, tile_t, tile(FOX_WIDTH), key_tile_t,
                   tile(FOX_WIDTH), pl.BlockSpec((1, N_FOX_HEADS, ts), lambda i, j: (i, 0, j))],
        out_shape=[rows_t, rows, key_tiles_t, rows_t, rows, key_tiles_t,
                   jax.ShapeDtypeStruct((b, s, FOX_WIDTH), F32),
                   jax.ShapeDtypeStruct((b, N_FOX_HEADS, s), F32)],
        scratch_shapes=[pltpu.VMEM((w_fz_t.shape[0], 1), F32)],
        compiler_params=_params("arbitrary", "arbitrary"),
        name="inproj",
    )(positions.reshape(b, 1, s), x, g, w_main, w_fz_t, b_fz_t)


ONES_ROWS = 16


def _block_diag_queries(qt):
    row = lax.broadcasted_iota(I32, qt.shape, 0)
    zero = jnp.zeros_like(qt)
    return jnp.concatenate([jnp.where(row < HEAD_DIM, qt, zero),
                            jnp.where(row >= HEAD_DIM, qt, zero)], axis=1)


class _Chain:
    def __init__(self, qbd, keys, values_t, scratch, key_bias=None, query_bias=None):
        self.qbd, self.keys, self.values_t = qbd, keys, values_t
        self.acc_ref, self.s_ref, self.p_ref, self.stat_ref = scratch
        self.key_bias, self.query_bias = key_bias, query_bias


_RUNNING_MAX, _TILE_MAX, _RESCALE = 0, 1, 2


def _causal_attention(chains, qi):
    tk, tq = chains[0].s_ref.shape[0], chains[0].s_ref.shape[1] // 2
    ones = jnp.ones((ONES_ROWS, tk), BF16)

    def scores(ch, i, diagonal):
        s = jnp.dot(ch.keys(i), ch.qbd, preferred_element_type=F32)
        if ch.key_bias is not None:
            s = s - ch.key_bias(i)
        if diagonal is not False:
            r = lax.broadcasted_iota(I32, s.shape, 0)
            c = lax.broadcasted_iota(I32, s.shape, 1) & (tq - 1)
            visible = r <= c if diagonal is True else jnp.logical_or(r <= c, jnp.logical_not(diagonal))
            s = jnp.where(visible, s, -jnp.inf)
        ch.s_ref[...] = s
        top = jnp.max(s, axis=0, keepdims=True)
        if ch.query_bias is not None:
            top = top + ch.query_bias
        ch.stat_ref[_TILE_MAX:_TILE_MAX + 1, :] = top

    def probs(ch):
        m = ch.stat_ref[_RUNNING_MAX:_RUNNING_MAX + 1, :]
        m_new = jnp.maximum(m, ch.stat_ref[_TILE_MAX:_TILE_MAX + 1, :])
        ch.stat_ref[_RUNNING_MAX:_RUNNING_MAX + 1, :] = m_new
        ch.stat_ref[_RESCALE:_RESCALE + 1, :] = jnp.exp2(m - m_new)
        ref = m_new if ch.query_bias is None else m_new - ch.query_bias
        ch.p_ref[...] = jnp.exp2(ch.s_ref[...] - ref).astype(BF16)

    def accumulate(ch, i):
        vts = jnp.concatenate([ch.values_t(i), ones], axis=0)
        ch.acc_ref[...] = (ch.stat_ref[_RESCALE:_RESCALE + 1, :] * ch.acc_ref[...]
                           + jnp.dot(vts, ch.p_ref[...], preferred_element_type=F32))

    def body(k, diagonal_next):
        for ch in chains:
            accumulate(ch, jnp.maximum(k - 1, 0))
        for ch in chains:
            probs(ch)
        for ch in chains:
            scores(ch, k + 1, diagonal_next)

    for ch in chains:
        ch.acc_ref[...] = jnp.zeros_like(ch.acc_ref)
        ch.p_ref[...] = jnp.zeros_like(ch.p_ref)
        ch.stat_ref[...] = jnp.concatenate(
            [jnp.full((1, 2 * tq), -jnp.inf, F32), jnp.ones((ch.stat_ref.shape[0] - 1, 2 * tq), F32)], axis=0)
        scores(ch, 0, qi == 0)

    def loop_body(k, carry):
        body(k, False)
        return carry

    lax.fori_loop(0, qi - 1, loop_body, 0)

    @pl.when(qi >= 1)
    def _():
        body(qi - 1, True)

    for ch in chains:
        accumulate(ch, jnp.maximum(qi - 1, 0))
    for ch in chains:
        probs(ch)
    for ch in chains:
        accumulate(ch, qi)


def _attention_kernel(lq1_ref, lk1_ref, lq2_ref, lk2_ref, g_ref,
                      dqt_ref, dk_ref, dvt_ref, fqt_ref, fk_ref, fvt_ref, cq_ref, ck_ref,
                      do_ref, fo_ref, *scratch, lambda_init):
    tq = dqt_ref.shape[2]
    tk = dvt_ref.shape[3]
    units = dqt_ref.shape[1] // LANES
    lane = lax.broadcasted_iota(I32, (1, LANES), 1)
    scratch = [scratch[i:i + 4] for i in range(0, len(scratch), 4)]

    def unit_chains(u):
        lanes = slice(u * LANES, (u + 1) * LANES)

        def tile(ref):
            return lambda i: ref[0, pl.ds(pl.multiple_of(i * tk, tk), tk), lanes]

        def tile_t(ref):
            return lambda i: ref[0, i, lanes, :]

        cq = jnp.concatenate([cq_ref[0, u, 0:1, :], cq_ref[0, u, 1:2, :]], axis=1)

        def key_bias(i):
            ck = tile(ck_ref)(i)
            other = pltpu.roll(ck, HEAD_DIM, 1)
            first = jnp.where(lane < HEAD_DIM, ck, other)
            second = jnp.where(lane >= HEAD_DIM, ck, other)
            return jnp.concatenate([first] * (tq // LANES) + [second] * (tq // LANES), axis=1)

        return [_Chain(_block_diag_queries(dqt_ref[0, lanes, :]), tile(dk_ref), tile_t(dvt_ref),
                       scratch[2 * u]),
                _Chain(_block_diag_queries(fqt_ref[0, lanes, :]), tile(fk_ref), tile_t(fvt_ref),
                       scratch[2 * u + 1], key_bias, cq)]

    chains = [ch for u in range(units) for ch in unit_chains(u)]
    _causal_attention(chains, pl.program_id(2))

    lam = (jnp.exp(jnp.sum(lq1_ref[...] * lk1_ref[...], axis=-1, keepdims=True))
           - jnp.exp(jnp.sum(lq2_ref[...] * lk2_ref[...], axis=-1, keepdims=True)) + lambda_init)
    dv, h = DIFF_V_DIM, HEAD_DIM
    for u in range(units):
        lanes = slice(u * LANES, (u + 1) * LANES)
        acc = chains[2 * u].acc_ref
        o = (acc[0:dv, 0:tq] / acc[dv:dv + 1, 0:tq]
             - lam * (acc[0:dv, tq:] / acc[dv:dv + 1, tq:]))
        ms = jnp.mean(o * o, axis=0, keepdims=True)
        do_ref[0, :, lanes] = (o * lax.rsqrt(ms + NORM_EPS) * g_ref[...]
                               * (1.0 - lambda_init)).astype(BF16).T
        acc = chains[2 * u + 1].acc_ref
        fo_ref[0, :, lanes] = jnp.concatenate([acc[0:h, 0:tq] / acc[2 * h:2 * h + 1, 0:tq],
                                               acc[h:2 * h, tq:] / acc[2 * h:2 * h + 1, tq:]],
                                              axis=0).astype(BF16).T


def _attention(dqt, dk, dvt, fqt, fk, fvt, c_rows, c_cols, lq1, lk1, lq2, lk2, g_col, lambda_init):
    b, s, _ = dk.shape
    t = ATTN_TILE
    assert N_DIFF_HEADS == N_FOX_HEADS // 2 and N_DIFF_HEADS % ATTN_UNITS == 0
    width = ATTN_UNITS * LANES
    vec = lambda arr: pl.BlockSpec(arr.shape, lambda i, h, j: (0, 0))
    q_spec = pl.BlockSpec((1, width, t), lambda i, h, j: (i, h, j))
    k_spec = pl.BlockSpec((1, s, width), lambda i, h, j: (i, 0, h))
    v_spec = pl.BlockSpec((1, s // t, width, t), lambda i, h, j: (i, 0, h, 0))
    o_spec = pl.BlockSpec((1, t, width), lambda i, h, j: (i, j, h))
    out = jax.ShapeDtypeStruct((b, s, DIFF_WIDTH), BF16)
    chain_scratch = [pltpu.VMEM((LANES + ONES_ROWS, 2 * t), F32),
                     pltpu.VMEM((t, 2 * t), F32),
                     pltpu.VMEM((t, 2 * t), BF16),
                     pltpu.VMEM((8, 2 * t), F32)]
    return pl.pallas_call(
        functools.partial(_attention_kernel, lambda_init=lambda_init),
        grid=(b, N_DIFF_HEADS // ATTN_UNITS, s // t),
        in_specs=[vec(lq1), vec(lk1), vec(lq2), vec(lk2), vec(g_col),
                  q_spec, k_spec, v_spec, q_spec, k_spec, v_spec,
                  pl.BlockSpec((1, ATTN_UNITS, 2, t), lambda i, h, j: (i, h, 0, j)), k_spec],
        out_specs=[o_spec, o_spec],
        out_shape=[out, out],
        scratch_shapes=chain_scratch * (2 * ATTN_UNITS),
        compiler_params=_params("arbitrary", "arbitrary", "arbitrary"),
        name="attention",
    )(lq1, lk1, lq2, lk2, g_col, dqt, dk, dvt, fqt, fk, fvt, c_rows, c_cols)


def _router_kernel(d_ref, f_ref, x_ref, wo_ref, g_ref, wrt_ref, brt_ref,
                   h_ref, xn_ref, gate_ref, idx_ref, rank_ref, count_ref, carry_ref):
    tm = ROUTER_SUBTILE
    nt = (((1,), (1,)), ((), ()))

    def pieces(v, count):
        out = []
        for _ in range(count):
            part = v.astype(BF16)
            out.append(part)
            v = v - part.astype(F32)
        return out

    wh, wl = pieces(wrt_ref[...], 2)
    row = lax.broadcasted_iota(I32, (tm, tm), 0)
    col = lax.broadcasted_iota(I32, (tm, tm), 1)
    earlier = jnp.where(row < col, 1.0, 0.0).astype(BF16)
    eye = jnp.where(row == col, 1.0, 0.0).astype(BF16)

    @pl.when(pl.program_id(0) == 0)
    def _():
        carry_ref[...] = jnp.zeros_like(carry_ref)

    for sub in range(x_ref.shape[0] // tm):
        rows = slice(sub * tm, (sub + 1) * tm)
        mixed = (jnp.dot(d_ref[rows, :], wo_ref[0:DIFF_WIDTH, :], preferred_element_type=F32)
                 + jnp.dot(f_ref[rows, :], wo_ref[DIFF_WIDTH:, :], preferred_element_type=F32))
        h = x_ref[rows, :] + mixed
        h_ref[rows, :] = h
        xn = _rms(h, g_ref[...])
        xn_ref[rows, :] = _pack_rows(xn)

        xh, xl = pieces(xn, 2)
        work = (lax.dot_general(wh, xh, nt, preferred_element_type=F32)
                + lax.dot_general(wh, xl, nt, preferred_element_type=F32)
                + lax.dot_general(wl, xh, nt, preferred_element_type=F32)) + brt_ref[...]
        expert = lax.broadcasted_iota(I32, work.shape, 0)

        vals, ids, picks = [], [], []
        for _ in range(TOP_K):
            m = jnp.max(work, axis=0, keepdims=True)
            idx = jnp.min(jnp.where(work == m, expert, N_EXPERTS), axis=0, keepdims=True)
            pick = expert == idx
            vals.append(m)
            ids.append(idx)
            picks.append(pick)
            work = jnp.where(pick, -jnp.inf, work)
        exps = [jnp.exp(v - vals[0]) for v in vals]
        denom = exps[0] + exps[1] + exps[2] + exps[3]

        chosen = jnp.where(picks[0] | picks[1] | picks[2] | picks[3], 1.0, 0.0)
        rank = jnp.dot(chosen.astype(BF16), earlier, preferred_element_type=F32) + carry_ref[...]
        carry_ref[...] = carry_ref[...] + jnp.sum(chosen, axis=1, keepdims=True)

        slot = lax.broadcasted_iota(I32, (8, tm), 0)
        gate_t = jnp.zeros((8, tm), F32)
        idx_t = jnp.zeros((8, tm), I32)
        rank_t = jnp.zeros((8, tm), F32)
        for k in range(TOP_K):
            gate_t = jnp.where(slot == k, exps[k] / denom, gate_t)
            idx_t = jnp.where(slot == k, ids[k], idx_t)
            rank_t = jnp.where(slot == k, jnp.sum(jnp.where(picks[k], rank, 0.0), axis=0, keepdims=True),
                               rank_t)
        idx_ref[:, rows] = idx_t
        rank_ref[:, rows] = rank_t.astype(I32)
        gate_rows = jnp.concatenate([gate_t, jnp.zeros((LANES - 8, tm), F32)], axis=0)
        gate_ref[rows, :] = sum(lax.dot_general(eye, part, nt, preferred_element_type=F32)
                                for part in pieces(gate_rows, 3))

    count_ref[...] = jnp.broadcast_to(carry_ref[...], count_ref.shape)


def _router(d_out, f_out, x, w_out, g_moe, w_router_t, b_router_t):
    n, d = x.shape
    tm = ROUTER_TILE
    tile = lambda width: pl.BlockSpec((tm, width), lambda i: (i, 0))
    rows = pl.BlockSpec((8, tm), lambda i: (0, i))
    full = lambda arr: pl.BlockSpec(arr.shape, lambda i: (0,) * arr.ndim)
    return pl.pallas_call(
        _router_kernel,
        grid=(n // tm,),
        in_specs=[tile(DIFF_WIDTH), tile(FOX_WIDTH), tile(d), full(w_out), full(g_moe),
                  full(w_router_t), full(b_router_t)],
        out_specs=[tile(d), tile(d // 2), tile(LANES), rows, rows,
                   pl.BlockSpec((N_EXPERTS, LANES), lambda i: (0, 0))],
        out_shape=[jax.ShapeDtypeStruct((n, d), F32), jax.ShapeDtypeStruct((n, d // 2), U32),
                   jax.ShapeDtypeStruct((n, LANES), F32), jax.ShapeDtypeStruct((8, n), I32),
                   jax.ShapeDtypeStruct((8, n), I32), jax.ShapeDtypeStruct((N_EXPERTS, LANES), F32)],
        scratch_shapes=[pltpu.VMEM((N_EXPERTS, 1), F32)],
        compiler_params=_params("arbitrary"),
        name="router",
    )(d_out, f_out, x, w_out, g_moe, w_router_t, b_router_t)


def _slot_kernel(starts_ref, idx_ref, rank_ref, pos_ref):
    idx = idx_ref[...]
    pos = rank_ref[...]
    for e in range(N_EXPERTS):
        pos = jnp.where(idx == e, pos + starts_ref[e], pos)
    pos_ref[...] = pos


def _slots(starts, idx, rank):
    rows, n = idx.shape
    width = min(n, 2048)
    block = pl.BlockSpec((rows, width), lambda i, starts: (0, i))
    return pl.pallas_call(
        _slot_kernel,
        grid_spec=pltpu.PrefetchScalarGridSpec(num_scalar_prefetch=1, grid=(n // width,),
                                               in_specs=[block, block], out_specs=block),
        out_shape=jax.ShapeDtypeStruct((rows, n), I32),
        compiler_params=_params("arbitrary"),
        name="slots",
    )(starts, idx, rank)


SC_WINDOW = 128
SC_COLUMNS = 256


def _dispatch_sparsecore(pos, xn, n_rows):
    n, d = xn.shape
    windows = n // SC_WINDOW
    mesh = plsc.VectorSubcoreMesh(core_axis_name="core", subcore_axis_name="subcore")

    @functools.partial(pl.kernel, out_type=jax.ShapeDtypeStruct((n_rows, d), xn.dtype), mesh=mesh,
                       scratch_types=[], name="dispatch_sc")
    def scatter(x_hbm, pos_hbm, xs_hbm):
        for c in range(d // SC_COLUMNS):
            def body(x_vmem, pos_vmem, c=c):
                pltpu.sync_copy(x_vmem, xs_hbm.at[:, pl.ds(c * SC_COLUMNS, SC_COLUMNS)].at[pos_vmem.at[0]])

            pltpu.emit_pipeline(
                body, grid=(TOP_K * windows,),
                in_specs=[pl.BlockSpec((SC_WINDOW, SC_COLUMNS), index_map=lambda i, c=c: (i % windows, c)),
                          pl.BlockSpec((1, SC_WINDOW), index_map=lambda i: (0, i))],
                out_specs=[], core_axis_name=("core", "subcore"),
                dimension_semantics=(pltpu.PARALLEL,))(x_hbm, pos_hbm)

    return scatter(xn, pos.reshape(1, -1))


def _expert_kernel(tile_expert_ref, n_used_ref, next_expert_ref, slot_ref,
                   x_ref, w1_ref, b1_ref, w2_ref, b2_ref, y_ref,
                   w1f_ref, w2f_ref, sem_ref, w1b_ref, w2p_ref, w2b_ref):
    t = pl.program_id(0)
    fresh = (t == 0) | (tile_expert_ref[t] != tile_expert_ref[jnp.maximum(t - 1, 0)])
    used = t < n_used_ref[0]

    def weight_copies(e, s):
        return (pltpu.make_async_copy(w1_ref.at[e], w1f_ref.at[s], sem_ref.at[0, s]),
                pltpu.make_async_copy(w2_ref.at[e], w2f_ref.at[s], sem_ref.at[1, s]))

    @pl.when(t == 0)
    def _():
        for copy in weight_copies(tile_expert_ref[0], slot_ref[0]):
            copy.start()

    @pl.when(fresh & used)
    def _():
        s = slot_ref[t]
        for copy in weight_copies(tile_expert_ref[t], s):
            copy.wait()

        @pl.when(next_expert_ref[t] >= 0)
        def _():
            for copy in weight_copies(next_expert_ref[t], 1 - s):
                copy.start()

        rows = LANES
        half = LANES // 2
        for r in range(0, w1b_ref.shape[0], rows):
            w1b_ref[r:r + rows, :] = w1f_ref[s, r:r + rows, :].astype(BF16)
        for c in range(w2p_ref.shape[0]):
            lanes = slice(c * LANES, (c + 1) * LANES)
            for r in range(0, w2b_ref.shape[0], rows):
                for b in range(2):
                    w2p_ref[c, pl.ds(r + b, half, stride=2), :] = (
                        w2f_ref[s, r + b * half:r + (b + 1) * half, lanes])
            w2b_ref[:, lanes] = w2p_ref[c].astype(BF16)

    @pl.when(used)
    def _():
        xb = _unpack_rows(x_ref[...]).astype(BF16)
        h = jnp.dot(xb, w1b_ref[...], preferred_element_type=F32) + b1_ref[0]
        even = lax.broadcasted_iota(I32, (1, LANES), 1) % 2 == 0
        parts = []
        for q in range(h.shape[1] // (2 * LANES)):
            a = h[:, (2 * q) * LANES:(2 * q + 1) * LANES]
            b = h[:, (2 * q + 1) * LANES:(2 * q + 2) * LANES]
            hg = jnp.where(even, a, pltpu.roll(b, 1, 1))
            hl = jnp.where(even, pltpu.roll(a, LANES - 1, 1), b)
            glu = jnp.minimum(hg, SWIGLU_LIMIT)
            lin = jnp.clip(hl, -SWIGLU_LIMIT, SWIGLU_LIMIT)
            parts.append((glu * jax.nn.sigmoid(SWIGLU_ALPHA * glu) * (lin + 1.0)).astype(BF16))
        act = jnp.concatenate(parts, axis=1)
        y_ref[...] = _pack_rows(jnp.dot(act, w2b_ref[...], preferred_element_type=F32) + b2_ref[0])

    @pl.when(jnp.logical_not(used))
    def _():
        y_ref[...] = jnp.zeros_like(y_ref)


def _experts(tile_expert, n_used, next_expert, slot, xs, w1, b1, w2, b2):
    n_rows, d = xs.shape
    tm = EXPERT_TILE
    by_expert = lambda arr: pl.BlockSpec((1,) + arr.shape[1:], lambda i, te, *_: (te[i], 0, 0))
    rows = pl.BlockSpec((tm, d), lambda i, *_: (i, 0))
    used_rows = pl.BlockSpec((tm, d), lambda i, te, nu, *_: (jnp.minimum(i, nu[0] - 1), 0))
    hbm = pl.BlockSpec(memory_space=pl.ANY)
    return pl.pallas_call(
        _expert_kernel,
        grid_spec=pltpu.PrefetchScalarGridSpec(
            num_scalar_prefetch=4,
            grid=(n_rows // tm,),
            in_specs=[used_rows, hbm, by_expert(b1), hbm, by_expert(b2)],
            out_specs=rows,
            scratch_shapes=[pltpu.VMEM((2,) + w1.shape[1:], F32), pltpu.VMEM((2,) + w2.shape[1:], F32),
                            pltpu.SemaphoreType.DMA((2, 2)),
                            pltpu.VMEM(w1.shape[1:], BF16),
                            pltpu.VMEM((w2.shape[2] // LANES, w2.shape[1], LANES), F32),
                            pltpu.VMEM(w2.shape[1:], BF16)]),
        out_shape=jax.ShapeDtypeStruct(xs.shape, xs.dtype),
        compiler_params=pltpu.CompilerParams(dimension_semantics=("arbitrary",),
                                             vmem_limit_bytes=EXPERT_VMEM_LIMIT),
        name="experts",
    )(tile_expert, n_used, next_expert, slot, xs, w1, b1, w2, b2)


def _gather_sparsecore(pos, y):
    d = y.shape[1]
    total = pos.size
    mesh = plsc.VectorSubcoreMesh(core_axis_name="core", subcore_axis_name="subcore")

    @functools.partial(pl.kernel, out_type=jax.ShapeDtypeStruct((total, d), y.dtype), mesh=mesh,
                       scratch_types=[], name="gather_sc")
    def gather(y_hbm, pos_hbm, out_hbm):
        for c in range(d // SC_COLUMNS):
            def body(pos_vmem, out_vmem, c=c):
                pltpu.sync_copy(y_hbm.at[:, pl.ds(c * SC_COLUMNS, SC_COLUMNS)].at[pos_vmem.at[0]], out_vmem)

            pltpu.emit_pipeline(
                body, grid=(total // SC_WINDOW,),
                in_specs=[pl.BlockSpec((1, SC_WINDOW), index_map=lambda i: (0, i))],
                out_specs=[pl.BlockSpec((SC_WINDOW, SC_COLUMNS), index_map=lambda i, c=c: (i, c))],
                core_axis_name=("core", "subcore"),
                dimension_semantics=(pltpu.PARALLEL,))(pos_hbm, out_hbm)

    return gather(y, pos.reshape(1, -1))


def _combine_kernel(h_ref, gate_ref, p_ref, wg_ref, wp_ref, gple_ref, gfin_ref, *refs, final_norm):
    y_refs, o_ref = refs[:TOP_K], refs[TOP_K]
    gates = gate_ref[...]
    moe = gates[:, 0:1] * _unpack_rows(y_refs[0][...])
    for k in range(1, TOP_K):
        moe = moe + gates[:, k:k + 1] * _unpack_rows(y_refs[k][...])
    h = h_ref[...] + moe
    gate = jax.nn.sigmoid(jnp.dot(_rms(h, gple_ref[...]).astype(BF16), wg_ref[...],
                                  preferred_element_type=F32))
    emb = jnp.dot(p_ref[...].astype(BF16), wp_ref[...], preferred_element_type=F32)
    h = h + gate * emb
    o_ref[...] = _rms(h, gfin_ref[...]) if final_norm else h


def _combine(h, gates, p, w_gate, w_proj, g_ple, g_final, y_tok, final_norm):
    n, d = h.shape
    tm = TOKEN_TILE
    tile = lambda width: pl.BlockSpec((tm, width), lambda i: (i, 0))
    full = lambda arr: pl.BlockSpec(arr.shape, lambda i: (0,) * arr.ndim)
    choice = lambda k: pl.BlockSpec((tm, y_tok.shape[1]), lambda i: (k * (n // tm) + i, 0))
    return pl.pallas_call(
        functools.partial(_combine_kernel, final_norm=final_norm),
        grid=(n // tm,),
        in_specs=[tile(d), tile(LANES), tile(p.shape[1]), full(w_gate), full(w_proj),
                  full(g_ple), full(g_final)] + [choice(k) for k in range(TOP_K)],
        out_specs=tile(d),
        out_shape=jax.ShapeDtypeStruct((n, d), F32),
        compiler_params=_params("arbitrary"),
        name="combine",
    )(h, gates, p, w_gate, w_proj, g_ple, g_final, *([y_tok] * TOP_K))


def _layer(h, p, positions, g_attn, w_in, lq1, lk1, lq2, lk2, g_subln, b_forget, w_out, g_moe,
           w_router, b_router, w_e1, b_e1, w_e2, b_e2, g_ple, w_ple_gate, w_ple_proj, g_final,
           lambda_init, final_norm):
    b, s, d = h.shape
    n = b * s
    row = lambda v: v.reshape(1, -1)

    w_main = w_in[:, :MAIN_WIDTH].astype(BF16)
    pad_heads = ((0, 2 * N_FOX_HEADS - N_FOX_HEADS), (0, 0))
    w_fz_t = jnp.pad(w_in[:, MAIN_WIDTH:].T, pad_heads).astype(BF16)
    b_fz_t = jnp.pad(b_forget.reshape(-1, 1), pad_heads)
    b1 = b_e1[:, None, :]
    b2 = b_e2[:, None, :]

    dq_t, dk, dv_t, fq_t, fk, fv_t, c, c_t = _inproj(positions, h, row(g_attn), w_main, w_fz_t, b_fz_t)
    c_rows = c_t.reshape(b, N_FOX_HEADS // 2, 2, s)
    d_out, f_out = _attention(dq_t, dk, dv_t, fq_t, fk, fv_t, c_rows, c, row(lq1), row(lk1), row(lq2),
                              row(lk2), g_subln.reshape(-1, 1), lambda_init)

    h1, xn, gates, idx, rank, counts = _router(
        d_out.reshape(n, DIFF_WIDTH), f_out.reshape(n, FOX_WIDTH), h.reshape(n, d),
        w_out.astype(BF16), row(g_moe), w_router.T, b_router.reshape(-1, 1))

    te = EXPERT_TILE
    n_rows = n * TOP_K + N_EXPERTS * te
    counts = counts[:, 0].astype(I32)
    padded = (counts + te - 1) // te * te
    ends = jnp.cumsum(padded)
    starts = ends - padded
    pos = _slots(starts.astype(I32), idx, rank)[:TOP_K]
    tile_start = jnp.arange(n_rows // te, dtype=I32) * te
    tile_expert = jnp.minimum(jnp.sum((ends[None, :] <= tile_start[:, None]).astype(I32), axis=1),
                              N_EXPERTS - 1)
    n_used = (ends[-1:] // te).astype(I32)
    owns = padded > 0
    later = lax.cummin(jnp.where(owns, jnp.arange(N_EXPERTS, dtype=I32), N_EXPERTS), reverse=True)
    following = jnp.concatenate([later[1:], jnp.full((1,), N_EXPERTS, I32)])
    following = jnp.where(following < N_EXPERTS, following, -1)
    next_expert = following[tile_expert]
    slot = ((jnp.cumsum(owns.astype(I32)) - 1) % 2)[tile_expert]

    xs = _dispatch_sparsecore(pos.astype(I32), xn, n_rows)
    y = _experts(tile_expert, n_used, next_expert.astype(I32), slot.astype(I32), xs, w_e1, b1, w_e2, b2)
    y_tok = _gather_sparsecore(pos.astype(I32), y)
    out = _combine(h1, gates, p.reshape(n, PLE_DIM), w_ple_gate.astype(BF16),
                   w_ple_proj.astype(BF16), row(g_ple), row(g_final), y_tok, final_norm)
    return out.reshape(b, s, d)


def kernel(x, p, positions, g_attn, w_in, lambda_q1, lambda_k1, lambda_q2, lambda_k2, g_subln, b_forget, w_out, g_moe, w_router, b_router, w_e1, b_e1, w_e2, b_e2, g_ple, w_ple_gate, w_ple_proj, g_final):
    depth = g_attn.shape[0]
    h = x
    for i in range(depth):
        lambda_init = 0.8 - 0.6 * math.exp(-0.3 * i)
        h = _layer(h, p[i], positions, g_attn[i], w_in[i], lambda_q1[i], lambda_k1[i],
                   lambda_q2[i], lambda_k2[i], g_subln[i], b_forget[i], w_out[i], g_moe[i],
                   w_router[i], b_router[i], w_e1[i], b_e1[i], w_e2[i], b_e2[i], g_ple[i],
                   w_ple_gate[i], w_ple_proj[i], g_final, lambda_init, i == depth - 1)
    return h
```

```python
import functools
import math

import jax
import jax.numpy as jnp
from jax import lax
from jax.experimental import pallas as pl
from jax.experimental.pallas import tpu as pltpu
from jax.experimental.pallas import tpu_sc as plsc

F32 = jnp.float32
BF16 = jnp.bfloat16
I32 = jnp.int32
U32 = jnp.uint32

LANES = 128
D_MODEL = 1024
HEAD_DIM = 64
N_DIFF_HEADS = 4
DIFF_V_DIM = 2 * HEAD_DIM
N_FOX_HEADS = 8
DIFF_WIDTH = N_DIFF_HEADS * DIFF_V_DIM
FOX_WIDTH = N_FOX_HEADS * HEAD_DIM
ROT_DIM = HEAD_DIM // 4
ROPE_THETA = 500000.0
N_EXPERTS = 32
TOP_K = 4
D_EXPERT = D_MODEL
SWIGLU_ALPHA = 1.702
SWIGLU_LIMIT = 7.0
PLE_DIM = 256
NORM_EPS = 1e-5
LOG2_E = math.log2(math.e)
MAIN_WIDTH = 3 * DIFF_WIDTH + 3 * FOX_WIDTH

SEQ_TILE = 256
ATTN_TILE = 256
ATTN_UNITS = 2
TOKEN_TILE = 256
COMBINE_PARTS = 2
ROUTER_TILE = 256
ROUTER_SUBTILE = 256
EXPERT_TILE = 256
VMEM_LIMIT = 48 * 1024 * 1024
EXPERT_VMEM_LIMIT = 58 * 1024 * 1024


def _params(*semantics):
    return pltpu.CompilerParams(dimension_semantics=semantics, vmem_limit_bytes=VMEM_LIMIT)


def _rms(x, g):
    return x * lax.rsqrt(jnp.mean(x * x, axis=-1, keepdims=True) + NORM_EPS) * g


def _pack_rows(v):
    half = v.shape[1] // 2
    bits = pltpu.bitcast(v.astype(BF16).astype(F32), U32)
    return (bits[:, :half] >> 16) | bits[:, half:]


def _unpack_rows(words):
    low = pltpu.bitcast(words << 16, F32)
    high = pltpu.bitcast(words & jnp.uint32(0xFFFF0000), F32)
    return jnp.concatenate([low, high], axis=1)


def _inproj_kernel(pos_ref, x_ref, g_ref, w_ref, wfzt_ref, bfzt_ref,
                   dqt_ref, dk_ref, dvt_ref, fqt_ref, fk_ref, fvt_ref, c_ref, ct_ref, carry_t_ref):
    ts = x_ref.shape[1]
    xb = _rms(x_ref[0], g_ref[...]).astype(BF16)
    proj = jnp.dot(xb, w_ref[...], preferred_element_type=F32)
    nt = (((1,), (1,)), ((), ()))
    tn = (((0,), (0,)), ((), ()))

    def pieces(v):
        hi = v.astype(BF16)
        r1 = v - hi.astype(F32)
        mid = r1.astype(BF16)
        return hi, mid, (r1 - mid.astype(F32)).astype(BF16)

    def rows_to_lanes(v, select):
        return sum(lax.dot_general(part, select, tn, preferred_element_type=F32) for part in pieces(v))

    half_rot = ROT_DIM // 2
    freq = lax.broadcasted_iota(I32, (half_rot, 1), 0)
    inv_freq = jnp.power(ROPE_THETA, -(2 * freq).astype(F32) / ROT_DIM)
    ang = inv_freq * pos_ref[0].astype(F32)
    trig = jnp.concatenate([jnp.cos(ang), jnp.sin(ang)], axis=0)
    r = lax.broadcasted_iota(I32, (2 * half_rot, 2 * LANES), 0)
    lane2 = lax.broadcasted_iota(I32, (2 * half_rot, 2 * LANES), 1)
    d = lane2 % HEAD_DIM
    same_freq = jnp.where(d % half_rot == r % half_rot, 1.0, 0.0)
    cos_part = jnp.where((r < half_rot) & (lane2 < LANES) & (d < ROT_DIM), same_freq, 0.0)
    sin_part = jnp.where((r >= half_rot) & (lane2 >= LANES) & (d < ROT_DIM),
                         jnp.where(d < half_rot, -same_freq, same_freq), 0.0)
    tables = rows_to_lanes(trig, (cos_part + sin_part).astype(BF16))
    lane = lax.broadcasted_iota(I32, (1, LANES), 1) % HEAD_DIM
    cosf = jnp.where(lane < ROT_DIM, tables[:, :LANES], 1.0)
    sinf = tables[:, LANES:]
    first_half = lane < half_rot

    def rope(t):
        outs = []
        for j in range(t.shape[1] // LANES):
            tj = t[:, j * LANES:(j + 1) * LANES]
            partner = jnp.where(first_half,
                                pltpu.roll(tj, LANES - ROT_DIM // 2, 1),
                                pltpu.roll(tj, ROT_DIM // 2, 1))
            outs.append(tj * cosf + partner * sinf)
        return jnp.concatenate(outs, axis=1)

    scale = HEAD_DIM ** -0.5 * LOG2_E
    w = DIFF_WIDTH
    dqt_ref[0] = (rope(proj[:, 0:w]) * scale).astype(BF16).T
    dk_ref[0] = rope(proj[:, w:2 * w]).astype(BF16)
    dvt_ref[0, 0] = proj[:, 2 * w:3 * w].astype(BF16).T
    fqt_ref[0] = (proj[:, 3 * w:4 * w] * scale).astype(BF16).T
    fk_ref[0] = proj[:, 4 * w:5 * w].astype(BF16)
    fvt_ref[0, 0] = proj[:, 5 * w:6 * w].astype(BF16).T

    def log_sigmoid(z):
        return jnp.minimum(z, 0.0) - jnp.log1p(jnp.exp(-jnp.abs(z)))

    row = lax.broadcasted_iota(I32, (ts, ts), 0)
    col = lax.broadcasted_iota(I32, (ts, ts), 1)
    upto_col = jnp.where(row <= col, 1.0, 0.0).astype(BF16)

    @pl.when(pl.program_id(1) == 0)
    def _():
        carry_t_ref[...] = jnp.zeros_like(carry_t_ref)

    fzt = lax.dot_general(wfzt_ref[...], xb, nt, preferred_element_type=F32) + bfzt_ref[...]
    ct = carry_t_ref[...] + sum(jnp.dot(part, upto_col, preferred_element_type=F32)
                                for part in pieces(log_sigmoid(fzt)))
    carry_t_ref[...] = ct[:, ts - 1:ts]
    ct = ct * LOG2_E
    ct_ref[0] = ct[0:N_FOX_HEADS, :]
    head = lax.broadcasted_iota(I32, (ct.shape[0], FOX_WIDTH), 0)
    group = lax.broadcasted_iota(I32, (ct.shape[0], FOX_WIDTH), 1) // HEAD_DIM
    c_ref[0] = rows_to_lanes(ct, jnp.where(head == group, 1.0, 0.0).astype(BF16))


def _inproj(positions, x, g, w_main, w_fz_t, b_fz_t):
    b, s, d = x.shape
    ts = SEQ_TILE
    assert ts == ATTN_TILE
    tile = lambda width: pl.BlockSpec((1, ts, width), lambda i, j: (i, j, 0))
    tile_t = pl.BlockSpec((1, DIFF_WIDTH, ts), lambda i, j: (i, 0, j))
    key_tile_t = pl.BlockSpec((1, 1, DIFF_WIDTH, ts), lambda i, j: (i, j, 0, 0))
    full = lambda arr: pl.BlockSpec(arr.shape, lambda i, j: (0,) * arr.ndim)
    rows = jax.ShapeDtypeStruct((b, s, DIFF_WIDTH), BF16)
    rows_t = jax.ShapeDtypeStruct((b, DIFF_WIDTH, s), BF16)
    key_tiles_t = jax.ShapeDtypeStruct((b, s // ts, DIFF_WIDTH, ts), BF16)
    return pl.pallas_call(
        _inproj_kernel,
        grid=(b, s // ts),
        in_specs=[pl.BlockSpec((1, 1, ts), lambda i, j: (i, 0, j)), tile(d), full(g), full(w_main),
                  full(w_fz_t), full(b_fz_t)],
        out_specs=[tile_t, tile(DIFF_WIDTH), key_tile_t, tile_t, tile(FOX_WIDTH), key_tile_t,
                   tile(FOX_WIDTH), pl.BlockSpec((1, N_FOX_HEADS, ts), lambda i, j: (i, 0, j))],
        out_shape=[rows_t, rows, key_tiles_t, rows_t, rows, key_tiles_t,
                   jax.ShapeDtypeStruct((b, s, FOX_WIDTH), F32),
                   jax.ShapeDtypeStruct((b, N_FOX_HEADS, s), F32)],
        scratch_shapes=[pltpu.VMEM((w_fz_t.shape[0], 1), F32)],
        compiler_params=_params("arbitrary", "arbitrary"),
        name="inproj",
    )(positions.reshape(b, 1, s), x, g, w_main, w_fz_t, b_fz_t)


ONES_ROWS = 16


def _block_diag_queries(qt):
    row = lax.broadcasted_iota(I32, qt.shape, 0)
    zero = jnp.zeros_like(qt)
    return jnp.concatenate([jnp.where(row < HEAD_DIM, qt, zero),
                            jnp.where(row >= HEAD_DIM, qt, zero)], axis=1)


class _Chain:
    def __init__(self, qbd, keys, values_t, scratch, key_bias=None, query_bias=None):
        self.qbd, self.keys, self.values_t = qbd, keys, values_t
        self.acc_ref, self.s_ref, self.p_ref, self.stat_ref = scratch
        self.key_bias, self.query_bias = key_bias, query_bias


_RUNNING_MAX, _TILE_MAX, _RESCALE = 0, 1, 2


def _causal_attention(chains, qi):
    tk, tq = chains[0].s_ref.shape[0], chains[0].s_ref.shape[1] // 2
    ones = jnp.ones((ONES_ROWS, tk), BF16)

    def scores(ch, i, diagonal):
        s = jnp.dot(ch.keys(i), ch.qbd, preferred_element_type=F32)
        if ch.key_bias is not None:
            s = s - ch.key_bias(i)
        if diagonal is not False:
            r = lax.broadcasted_iota(I32, s.shape, 0)
            c = lax.broadcasted_iota(I32, s.shape, 1) & (tq - 1)
            visible = r <= c if diagonal is True else jnp.logical_or(r <= c, jnp.logical_not(diagonal))
            s = jnp.where(visible, s, -jnp.inf)
        ch.s_ref[...] = s
        top = jnp.max(s, axis=0, keepdims=True)
        if ch.query_bias is not None:
            top = top + ch.query_bias
        ch.stat_ref[_TILE_MAX:_TILE_MAX + 1, :] = top

    def probs(ch):
        m = ch.stat_ref[_RUNNING_MAX:_RUNNING_MAX + 1, :]
        m_new = jnp.maximum(m, ch.stat_ref[_TILE_MAX:_TILE_MAX + 1, :])
        ch.stat_ref[_RUNNING_MAX:_RUNNING_MAX + 1, :] = m_new
        ch.stat_ref[_RESCALE:_RESCALE + 1, :] = jnp.exp2(m - m_new)
        ref = m_new if ch.query_bias is None else m_new - ch.query_bias
        ch.p_ref[...] = jnp.exp2(ch.s_ref[...] - ref).astype(BF16)

    def accumulate(ch, i):
        vts = jnp.concatenate([ch.values_t(i), ones], axis=0)
        ch.acc_ref[...] = (ch.stat_ref[_RESCALE:_RESCALE + 1, :] * ch.acc_ref[...]
                           + jnp.dot(vts, ch.p_ref[...], preferred_element_type=F32))

    def body(k, diagonal_next):
        for ch in chains:
            accumulate(ch, jnp.maximum(k - 1, 0))
        for ch in chains:
            probs(ch)
        for ch in chains:
            scores(ch, k + 1, diagonal_next)

    for ch in chains:
        ch.acc_ref[...] = jnp.zeros_like(ch.acc_ref)
        ch.p_ref[...] = jnp.zeros_like(ch.p_ref)
        ch.stat_ref[...] = jnp.concatenate(
            [jnp.full((1, 2 * tq), -jnp.inf, F32), jnp.ones((ch.stat_ref.shape[0] - 1, 2 * tq), F32)], axis=0)
        scores(ch, 0, qi == 0)

    def loop_body(k, carry):
        body(k, False)
        return carry

    lax.fori_loop(0, qi - 1, loop_body, 0)

    @pl.when(qi >= 1)
    def _():
        body(qi - 1, True)

    for ch in chains:
        accumulate(ch, jnp.maximum(qi - 1, 0))
    for ch in chains:
        probs(ch)
    for ch in chains:
        accumulate(ch, qi)


def _attention_kernel(lq1_ref, lk1_ref, lq2_ref, lk2_ref, g_ref,
                      dqt_ref, dk_ref, dvt_ref, fqt_ref, fk_ref, fvt_ref, cq_ref, ck_ref,
                      do_ref, fo_ref, *scratch, lambda_init):
    tq = dqt_ref.shape[2]
    tk = dvt_ref.shape[3]
    units = dqt_ref.shape[1] // LANES
    lane = lax.broadcasted_iota(I32, (1, LANES), 1)
    scratch = [scratch[i:i + 4] for i in range(0, len(scratch), 4)]

    def unit_chains(u):
        lanes = slice(u * LANES, (u + 1) * LANES)

        def tile(ref):
            return lambda i: ref[0, pl.ds(pl.multiple_of(i * tk, tk), tk), lanes]

        def tile_t(ref):
            return lambda i: ref[0, i, lanes, :]

        cq = jnp.concatenate([cq_ref[0, u, 0:1, :], cq_ref[0, u, 1:2, :]], axis=1)

        def key_bias(i):
            ck = tile(ck_ref)(i)
            other = pltpu.roll(ck, HEAD_DIM, 1)
            first = jnp.where(lane < HEAD_DIM, ck, other)
            second = jnp.where(lane >= HEAD_DIM, ck, other)
            return jnp.concatenate([first] * (tq // LANES) + [second] * (tq // LANES), axis=1)

        return [_Chain(_block_diag_queries(dqt_ref[0, lanes, :]), tile(dk_ref), tile_t(dvt_ref),
                       scratch[2 * u]),
                _Chain(_block_diag_queries(fqt_ref[0, lanes, :]), tile(fk_ref), tile_t(fvt_ref),
                       scratch[2 * u + 1], key_bias, cq)]

    chains = [ch for u in range(units) for ch in unit_chains(u)]
    _causal_attention(chains, pl.program_id(2))

    lam = (jnp.exp(jnp.sum(lq1_ref[...] * lk1_ref[...], axis=-1, keepdims=True))
           - jnp.exp(jnp.sum(lq2_ref[...] * lk2_ref[...], axis=-1, keepdims=True)) + lambda_init)
    dv, h = DIFF_V_DIM, HEAD_DIM
    for u in range(units):
        lanes = slice(u * LANES, (u + 1) * LANES)
        acc = chains[2 * u].acc_ref
        o = (acc[0:dv, 0:tq] / acc[dv:dv + 1, 0:tq]
             - lam * (acc[0:dv, tq:] / acc[dv:dv + 1, tq:]))
        ms = jnp.mean(o * o, axis=0, keepdims=True)
        do_ref[0, :, lanes] = (o * lax.rsqrt(ms + NORM_EPS) * g_ref[...]
                               * (1.0 - lambda_init)).astype(BF16).T
        acc = chains[2 * u + 1].acc_ref
        fo_ref[0, :, lanes] = jnp.concatenate([acc[0:h, 0:tq] / acc[2 * h:2 * h + 1, 0:tq],
                                               acc[h:2 * h, tq:] / acc[2 * h:2 * h + 1, tq:]],
                                              axis=0).astype(BF16).T


def _attention(dqt, dk, dvt, fqt, fk, fvt, c_rows, c_cols, lq1, lk1, lq2, lk2, g_col, lambda_init):
    b, s, _ = dk.shape
    t = ATTN_TILE
    assert N_DIFF_HEADS == N_FOX_HEADS // 2 and N_DIFF_HEADS % ATTN_UNITS == 0
    width = ATTN_UNITS * LANES
    vec = lambda arr: pl.BlockSpec(arr.shape, lambda i, h, j: (0, 0))
    q_spec = pl.BlockSpec((1, width, t), lambda i, h, j: (i, h, j))
    k_spec = pl.BlockSpec((1, s, width), lambda i, h, j: (i, 0, h))
    v_spec = pl.BlockSpec((1, s // t, width, t), lambda i, h, j: (i, 0, h, 0))
    o_spec = pl.BlockSpec((1, t, width), lambda i, h, j: (i, j, h))
    out = jax.ShapeDtypeStruct((b, s, DIFF_WIDTH), BF16)
    chain_scratch = [pltpu.VMEM((LANES + ONES_ROWS, 2 * t), F32),
                     pltpu.VMEM((t, 2 * t), F32),
                     pltpu.VMEM((t, 2 * t), BF16),
                     pltpu.VMEM((8, 2 * t), F32)]
    return pl.pallas_call(
        functools.partial(_attention_kernel, lambda_init=lambda_init),
        grid=(b, N_DIFF_HEADS // ATTN_UNITS, s // t),
        in_specs=[vec(lq1), vec(lk1), vec(lq2), vec(lk2), vec(g_col),
                  q_spec, k_spec, v_spec, q_spec, k_spec, v_spec,
                  pl.BlockSpec((1, ATTN_UNITS, 2, t), lambda i, h, j: (i, h, 0, j)), k_spec],
        out_specs=[o_spec, o_spec],
        out_shape=[out, out],
        scratch_shapes=chain_scratch * (2 * ATTN_UNITS),
        compiler_params=_params("arbitrary", "arbitrary", "arbitrary"),
        name="attention",
    )(lq1, lk1, lq2, lk2, g_col, dqt, dk, dvt, fqt, fk, fvt, c_rows, c_cols)


def _router_kernel(d_ref, f_ref, x_ref, wo_ref, g_ref, wrt_ref, brt_ref,
                   h_ref, xn_ref, gate_ref, idx_ref, rank_ref, count_ref, carry_ref):
    tm = ROUTER_SUBTILE
    nt = (((1,), (1,)), ((), ()))

    def pieces(v, count):
        out = []
        for _ in range(count):
            part = v.astype(BF16)
            out.append(part)
            v = v - part.astype(F32)
        return out

    wh, wl = pieces(wrt_ref[...], 2)
    row = lax.broadcasted_iota(I32, (tm, tm), 0)
    col = lax.broadcasted_iota(I32, (tm, tm), 1)
    earlier = jnp.where(row < col, 1.0, 0.0).astype(BF16)
    eye = jnp.where(row == col, 1.0, 0.0).astype(BF16)

    @pl.when(pl.program_id(0) == 0)
    def _():
        carry_ref[...] = jnp.zeros_like(carry_ref)

    for sub in range(x_ref.shape[0] // tm):
        rows = slice(sub * tm, (sub + 1) * tm)
        mixed = (jnp.dot(d_ref[rows, :], wo_ref[0:DIFF_WIDTH, :], preferred_element_type=F32)
                 + jnp.dot(f_ref[rows, :], wo_ref[DIFF_WIDTH:, :], preferred_element_type=F32))
        h = x_ref[rows, :] + mixed
        h_ref[rows, :] = h
        xn = _rms(h, g_ref[...])
        xn_ref[rows, :] = _pack_rows(xn)

        xh, xl = pieces(xn, 2)
        work = (lax.dot_general(wh, xh, nt, preferred_element_type=F32)
                + lax.dot_general(wh, xl, nt, preferred_element_type=F32)
                + lax.dot_general(wl, xh, nt, preferred_element_type=F32)) + brt_ref[...]
        expert = lax.broadcasted_iota(I32, work.shape, 0)

        vals, ids, picks = [], [], []
        for _ in range(TOP_K):
            m = jnp.max(work, axis=0, keepdims=True)
            idx = jnp.min(jnp.where(work == m, expert, N_EXPERTS), axis=0, keepdims=True)
            pick = expert == idx
            vals.append(m)
            ids.append(idx)
            picks.append(pick)
            work = jnp.where(pick, -jnp.inf, work)
        exps = [jnp.exp(v - vals[0]) for v in vals]
        denom = exps[0] + exps[1] + exps[2] + exps[3]

        chosen = jnp.where(picks[0] | picks[1] | picks[2] | picks[3], 1.0, 0.0)
        rank = jnp.dot(chosen.astype(BF16), earlier, preferred_element_type=F32) + carry_ref[...]
        carry_ref[...] = carry_ref[...] + jnp.sum(chosen, axis=1, keepdims=True)

        slot = lax.broadcasted_iota(I32, (8, tm), 0)
        gate_t = jnp.zeros((8, tm), F32)
        idx_t = jnp.zeros((8, tm), I32)
        rank_t = jnp.zeros((8, tm), F32)
        for k in range(TOP_K):
            gate_t = jnp.where(slot == k, exps[k] / denom, gate_t)
            idx_t = jnp.where(slot == k, ids[k], idx_t)
            rank_t = jnp.where(slot == k, jnp.sum(jnp.where(picks[k], rank, 0.0), axis=0, keepdims=True),
                               rank_t)
        idx_ref[:, rows] = idx_t
        rank_ref[:, rows] = rank_t.astype(I32)
        gate_rows = jnp.concatenate([gate_t, jnp.zeros((LANES - 8, tm), F32)], axis=0)
        gate_ref[rows, :] = sum(lax.dot_general(eye, part, nt, preferred_element_type=F32)
                                for part in pieces(gate_rows, 3))

    count_ref[...] = jnp.broadcast_to(carry_ref[...], count_ref.shape)


def _router(d_out, f_out, x, w_out, g_moe, w_router_t, b_router_t):
    n, d = x.shape
    tm = ROUTER_TILE
    tile = lambda width: pl.BlockSpec((tm, width), lambda i: (i, 0))
    rows = pl.BlockSpec((8, tm), lambda i: (0, i))
    full = lambda arr: pl.BlockSpec(arr.shape, lambda i: (0,) * arr.ndim)
    return pl.pallas_call(
        _router_kernel,
        grid=(n // tm,),
        in_specs=[tile(DIFF_WIDTH), tile(FOX_WIDTH), tile(d), full(w_out), full(g_moe),
                  full(w_router_t), full(b_router_t)],
        out_specs=[tile(d), tile(d // 2), tile(LANES), rows, rows,
                   pl.BlockSpec((N_EXPERTS, LANES), lambda i: (0, 0))],
        out_shape=[jax.ShapeDtypeStruct((n, d), F32), jax.ShapeDtypeStruct((n, d // 2), U32),
                   jax.ShapeDtypeStruct((n, LANES), F32), jax.ShapeDtypeStruct((8, n), I32),
                   jax.ShapeDtypeStruct((8, n), I32), jax.ShapeDtypeStruct((N_EXPERTS, LANES), F32)],
        scratch_shapes=[pltpu.VMEM((N_EXPERTS, 1), F32)],
        compiler_params=_params("arbitrary"),
        name="router",
    )(d_out, f_out, x, w_out, g_moe, w_router_t, b_router_t)


def _slot_kernel(starts_ref, idx_ref, rank_ref, pos_ref):
    idx = idx_ref[...]
    pos = rank_ref[...]
    for e in range(N_EXPERTS):
        pos = jnp.where(idx == e, pos + starts_ref[e], pos)
    pos_ref[...] = pos


def _slots(starts, idx, rank):
    rows, n = idx.shape
    width = min(n, 2048)
    block = pl.BlockSpec((rows, width), lambda i, starts: (0, i))
    return pl.pallas_call(
        _slot_kernel,
        grid_spec=pltpu.PrefetchScalarGridSpec(num_scalar_prefetch=1, grid=(n // width,),
                                               in_specs=[block, block], out_specs=block),
        out_shape=jax.ShapeDtypeStruct((rows, n), I32),
        compiler_params=_params("arbitrary"),
        name="slots",
    )(starts, idx, rank)


SC_WINDOW = 128
SC_COLUMNS = 256


def _dispatch_sparsecore(pos, xn, n_rows):
    n, d = xn.shape
    windows = n // SC_WINDOW
    mesh = plsc.VectorSubcoreMesh(core_axis_name="core", subcore_axis_name="subcore")

    @functools.partial(pl.kernel, out_type=jax.ShapeDtypeStruct((n_rows, d), xn.dtype), mesh=mesh,
                       scratch_types=[], name="dispatch_sc")
    def scatter(x_hbm, pos_hbm, xs_hbm):
        for c in range(d // SC_COLUMNS):
            def body(x_vmem, pos_vmem, c=c):
                pltpu.sync_copy(x_vmem, xs_hbm.at[:, pl.ds(c * SC_COLUMNS, SC_COLUMNS)].at[pos_vmem.at[0]])

            pltpu.emit_pipeline(
                body, grid=(TOP_K * windows,),
                in_specs=[pl.BlockSpec((SC_WINDOW, SC_COLUMNS), index_map=lambda i, c=c: (i % windows, c)),
                          pl.BlockSpec((1, SC_WINDOW), index_map=lambda i: (0, i))],
                out_specs=[], core_axis_name=("core", "subcore"),
                dimension_semantics=(pltpu.PARALLEL,))(x_hbm, pos_hbm)

    return scatter(xn, pos.reshape(1, -1))


def _expert_kernel(tile_expert_ref, n_used_ref, next_expert_ref, slot_ref,
                   x_ref, w1_ref, b1_ref, w2_ref, b2_ref, y_ref,
                   w1f_ref, w2f_ref, sem_ref, w1b_ref, w2p_ref, w2b_ref):
    t = pl.program_id(0)
    fresh = (t == 0) | (tile_expert_ref[t] != tile_expert_ref[jnp.maximum(t - 1, 0)])
    used = t < n_used_ref[0]

    def weight_copies(e, s):
        return (pltpu.make_async_copy(w1_ref.at[e], w1f_ref.at[s], sem_ref.at[0, s]),
                pltpu.make_async_copy(w2_ref.at[e], w2f_ref.at[s], sem_ref.at[1, s]))

    @pl.when(t == 0)
    def _():
        for copy in weight_copies(tile_expert_ref[0], slot_ref[0]):
            copy.start()

    @pl.when(fresh & used)
    def _():
        s = slot_ref[t]
        for copy in weight_copies(tile_expert_ref[t], s):
            copy.wait()

        @pl.when(next_expert_ref[t] >= 0)
        def _():
            for copy in weight_copies(next_expert_ref[t], 1 - s):
                copy.start()

        rows = LANES
        half = LANES // 2
        for r in range(0, w1b_ref.shape[0], rows):
            w1b_ref[r:r + rows, :] = w1f_ref[s, r:r + rows, :].astype(BF16)
        for c in range(w2p_ref.shape[0]):
            lanes = slice(c * LANES, (c + 1) * LANES)
            for r in range(0, w2b_ref.shape[0], rows):
                for b in range(2):
                    w2p_ref[c, pl.ds(r + b, half, stride=2), :] = (
                        w2f_ref[s, r + b * half:r + (b + 1) * half, lanes])
            w2b_ref[:, lanes] = w2p_ref[c].astype(BF16)

    @pl.when(used)
    def _():
        xb = _unpack_rows(x_ref[...]).astype(BF16)
        h = jnp.dot(xb, w1b_ref[...], preferred_element_type=F32) + b1_ref[0]
        even = lax.broadcasted_iota(I32, (1, LANES), 1) % 2 == 0
        parts = []
        for q in range(h.shape[1] // (2 * LANES)):
            a = h[:, (2 * q) * LANES:(2 * q + 1) * LANES]
            b = h[:, (2 * q + 1) * LANES:(2 * q + 2) * LANES]
            hg = jnp.where(even, a, pltpu.roll(b, 1, 1))
            hl = jnp.where(even, pltpu.roll(a, LANES - 1, 1), b)
            glu = jnp.minimum(hg, SWIGLU_LIMIT)
            lin = jnp.clip(hl, -SWIGLU_LIMIT, SWIGLU_LIMIT)
            parts.append((glu * jax.nn.sigmoid(SWIGLU_ALPHA * glu) * (lin + 1.0)).astype(BF16))
        act = jnp.concatenate(parts, axis=1)
        y_ref[...] = _pack_rows(jnp.dot(act, w2b_ref[...], preferred_element_type=F32) + b2_ref[0])

    @pl.when(jnp.logical_not(used))
    def _():
        y_ref[...] = jnp.zeros_like(y_ref)


def _experts(tile_expert, n_used, next_expert, slot, xs, w1, b1, w2, b2):
    n_rows, d = xs.shape
    tm = EXPERT_TILE
    by_expert = lambda arr: pl.BlockSpec((1,) + arr.shape[1:], lambda i, te, *_: (te[i], 0, 0))
    rows = pl.BlockSpec((tm, d), lambda i, *_: (i, 0))
    used_rows = pl.BlockSpec((tm, d), lambda i, te, nu, *_: (jnp.minimum(i, nu[0] - 1), 0))
    hbm = pl.BlockSpec(memory_space=pl.ANY)
    return pl.pallas_call(
        _expert_kernel,
        grid_spec=pltpu.PrefetchScalarGridSpec(
            num_scalar_prefetch=4,
            grid=(n_rows // tm,),
            in_specs=[used_rows, hbm, by_expert(b1), hbm, by_expert(b2)],
            out_specs=rows,
            scratch_shapes=[pltpu.VMEM((2,) + w1.shape[1:], F32), pltpu.VMEM((2,) + w2.shape[1:], F32),
                            pltpu.SemaphoreType.DMA((2, 2)),
                            pltpu.VMEM(w1.shape[1:], BF16),
                            pltpu.VMEM((w2.shape[2] // LANES, w2.shape[1], LANES), F32),
                            pltpu.VMEM(w2.shape[1:], BF16)]),
        out_shape=jax.ShapeDtypeStruct(xs.shape, xs.dtype),
        compiler_params=pltpu.CompilerParams(dimension_semantics=("arbitrary",),
                                             vmem_limit_bytes=EXPERT_VMEM_LIMIT),
        name="experts",
    )(tile_expert, n_used, next_expert, slot, xs, w1, b1, w2, b2)


def _gather_sparsecore(pos, y):
    d = y.shape[1]
    total = pos.size
    mesh = plsc.VectorSubcoreMesh(core_axis_name="core", subcore_axis_name="subcore")

    @functools.partial(pl.kernel, out_type=jax.ShapeDtypeStruct((total, d), y.dtype), mesh=mesh,
                       scratch_types=[], name="gather_sc")
    def gather(y_hbm, pos_hbm, out_hbm):
        for c in range(d // SC_COLUMNS):
            def body(pos_vmem, out_vmem, c=c):
                pltpu.sync_copy(y_hbm.at[:, pl.ds(c * SC_COLUMNS, SC_COLUMNS)].at[pos_vmem.at[0]], out_vmem)

            pltpu.emit_pipeline(
                body, grid=(total // SC_WINDOW,),
                in_specs=[pl.BlockSpec((1, SC_WINDOW), index_map=lambda i: (0, i))],
                out_specs=[pl.BlockSpec((SC_WINDOW, SC_COLUMNS), index_map=lambda i, c=c: (i, c))],
                core_axis_name=("core", "subcore"),
                dimension_semantics=(pltpu.PARALLEL,))(pos_hbm, out_hbm)

    return gather(y, pos.reshape(1, -1))


def _combine_kernel(h_ref, gate_ref, p_ref, wg_ref, wp_ref, gple_ref, gfin_ref, *refs, final_norm):
    y_refs, o_ref = refs[:TOP_K], refs[-1]
    gates = gate_ref[...]
    moe = gates[:, 0:1] * _unpack_rows(y_refs[0][...])
    for k in range(1, TOP_K):
        moe = moe + gates[:, k:k + 1] * _unpack_rows(y_refs[k][...])
    h = h_ref[...] + moe
    gate = jax.nn.sigmoid(jnp.dot(_rms(h, gple_ref[...]).astype(BF16), wg_ref[...],
                                  preferred_element_type=F32))
    emb = jnp.dot(p_ref[...].astype(BF16), wp_ref[...], preferred_element_type=F32)
    h = h + gate * emb
    o_ref[...] = _rms(h, gfin_ref[...]) if final_norm else h


def _combine(h, gates, p, w_gate, w_proj, g_ple, g_final, y_part, part, parts, earlier, final_norm):
    n, d = h.shape
    tm = TOKEN_TILE
    tiles = n // tm // parts
    first = part * tiles
    tile = lambda width: pl.BlockSpec((tm, width), lambda i: (first + i, 0))
    full = lambda arr: pl.BlockSpec(arr.shape, lambda i: (0,) * arr.ndim)
    choice = lambda k: pl.BlockSpec((tm, y_part.shape[1]), lambda i: (k * tiles + i, 0))
    carried = [] if earlier is None else [earlier]
    return pl.pallas_call(
        functools.partial(_combine_kernel, final_norm=final_norm),
        grid=(tiles,),
        in_specs=[tile(d), tile(LANES), tile(p.shape[1]), full(w_gate), full(w_proj),
                  full(g_ple), full(g_final)] + [choice(k) for k in range(TOP_K)]
                 + [pl.BlockSpec(memory_space=pl.ANY)] * len(carried),
        out_specs=tile(d),
        out_shape=jax.ShapeDtypeStruct((n, d), F32),
        input_output_aliases={7 + TOP_K: 0} if carried else {},
        compiler_params=_params("arbitrary"),
        name=f"combine_{part}",
    )(h, gates, p, w_gate, w_proj, g_ple, g_final, *([y_part] * TOP_K), *carried)


def _layer(h, p, positions, g_attn, w_in, lq1, lk1, lq2, lk2, g_subln, b_forget, w_out, g_moe,
           w_router, b_router, w_e1, b_e1, w_e2, b_e2, g_ple, w_ple_gate, w_ple_proj, g_final,
           lambda_init, final_norm):
    b, s, d = h.shape
    n = b * s
    row = lambda v: v.reshape(1, -1)

    w_main = w_in[:, :MAIN_WIDTH].astype(BF16)
    pad_heads = ((0, 2 * N_FOX_HEADS - N_FOX_HEADS), (0, 0))
    w_fz_t = jnp.pad(w_in[:, MAIN_WIDTH:].T, pad_heads).astype(BF16)
    b_fz_t = jnp.pad(b_forget.reshape(-1, 1), pad_heads)
    b1 = b_e1[:, None, :]
    b2 = b_e2[:, None, :]

    dq_t, dk, dv_t, fq_t, fk, fv_t, c, c_t = _inproj(positions, h, row(g_attn), w_main, w_fz_t, b_fz_t)
    c_rows = c_t.reshape(b, N_FOX_HEADS // 2, 2, s)
    d_out, f_out = _attention(dq_t, dk, dv_t, fq_t, fk, fv_t, c_rows, c, row(lq1), row(lk1), row(lq2),
                              row(lk2), g_subln.reshape(-1, 1), lambda_init)

    h1, xn, gates, idx, rank, counts = _router(
        d_out.reshape(n, DIFF_WIDTH), f_out.reshape(n, FOX_WIDTH), h.reshape(n, d),
        w_out.astype(BF16), row(g_moe), w_router.T, b_router.reshape(-1, 1))

    te = EXPERT_TILE
    n_rows = n * TOP_K + N_EXPERTS * te
    counts = counts[:, 0].astype(I32)
    padded = (counts + te - 1) // te * te
    ends = jnp.cumsum(padded)
    starts = ends - padded
    pos = _slots(starts.astype(I32), idx, rank)[:TOP_K]
    tile_start = jnp.arange(n_rows // te, dtype=I32) * te
    tile_expert = jnp.minimum(jnp.sum((ends[None, :] <= tile_start[:, None]).astype(I32), axis=1),
                              N_EXPERTS - 1)
    n_used = (ends[-1:] // te).astype(I32)
    owns = padded > 0
    later = lax.cummin(jnp.where(owns, jnp.arange(N_EXPERTS, dtype=I32), N_EXPERTS), reverse=True)
    following = jnp.concatenate([later[1:], jnp.full((1,), N_EXPERTS, I32)])
    following = jnp.where(following < N_EXPERTS, following, -1)
    next_expert = following[tile_expert]
    slot = ((jnp.cumsum(owns.astype(I32)) - 1) % 2)[tile_expert]

    xs = _dispatch_sparsecore(pos.astype(I32), xn, n_rows)
    y = _experts(tile_expert, n_used, next_expert.astype(I32), slot.astype(I32), xs, w_e1, b1, w_e2, b2)
    out = None
    for part in range(COMBINE_PARTS):
        tokens = slice(part * n // COMBINE_PARTS, (part + 1) * n // COMBINE_PARTS)
        y_part = _gather_sparsecore(pos[:, tokens].astype(I32), y)
        out = _combine(h1, gates, p.reshape(n, PLE_DIM), w_ple_gate.astype(BF16), w_ple_proj.astype(BF16),
                       row(g_ple), row(g_final), y_part, part, COMBINE_PARTS, out, final_norm)
    return out.reshape(b, s, d)


def kernel(x, p, positions, g_attn, w_in, lambda_q1, lambda_k1, lambda_q2, lambda_k2, g_subln, b_forget, w_out, g_moe, w_router, b_router, w_e1, b_e1, w_e2, b_e2, g_ple, w_ple_gate, w_ple_proj, g_final):
    depth = g_attn.shape[0]
    h = x
    for i in range(depth):
        lambda_init = 0.8 - 0.6 * math.exp(-0.3 * i)
        h = _layer(h, p[i], positions, g_attn[i], w_in[i], lambda_q1[i], lambda_k1[i],
                   lambda_q2[i], lambda_k2[i], g_subln[i], b_forget[i], w_out[i], g_moe[i],
                   w_router[i], b_router[i], w_e1[i], b_e1[i], w_e2[i], b_e2[i], g_ple[i],
                   w_ple_gate[i], w_ple_proj[i], g_final, lambda_init, i == depth - 1)
    return h
```

```python
import functools
import math

import jax
import jax.numpy as jnp
from jax import lax
from jax.experimental import pallas as pl
from jax.experimental.pallas import tpu as pltpu
from jax.experimental.pallas import tpu_sc as plsc

F32 = jnp.float32
BF16 = jnp.bfloat16
I32 = jnp.int32
U32 = jnp.uint32

LANES = 128
D_MODEL = 1024
HEAD_DIM = 64
N_DIFF_HEADS = 4
DIFF_V_DIM = 2 * HEAD_DIM
N_FOX_HEADS = 8
DIFF_WIDTH = N_DIFF_HEADS * DIFF_V_DIM
FOX_WIDTH = N_FOX_HEADS * HEAD_DIM
ROT_DIM = HEAD_DIM // 4
ROPE_THETA = 500000.0
N_EXPERTS = 32
TOP_K = 4
D_EXPERT = D_MODEL
SWIGLU_ALPHA = 1.702
SWIGLU_LIMIT = 7.0
PLE_DIM = 256
NORM_EPS = 1e-5
LOG2_E = math.log2(math.e)
MAIN_WIDTH = 3 * DIFF_WIDTH + 3 * FOX_WIDTH

SEQ_TILE = 256
ATTN_TILE = 256
ATTN_UNITS = 2
TOKEN_TILE = 256
ROUTER_TILE = 1024
ROUTER_SUBTILE = 256
EXPERT_TILE = 256
VMEM_LIMIT = 48 * 1024 * 1024
EXPERT_VMEM_LIMIT = 58 * 1024 * 1024


def _params(*semantics):
    return pltpu.CompilerParams(dimension_semantics=semantics, vmem_limit_bytes=VMEM_LIMIT)


def _rms(x, g):
    return x * lax.rsqrt(jnp.mean(x * x, axis=-1, keepdims=True) + NORM_EPS) * g


def _pack_rows(v):
    half = v.shape[1] // 2
    bits = pltpu.bitcast(v.astype(BF16).astype(F32), U32)
    return (bits[:, :half] >> 16) | bits[:, half:]


def _unpack_rows(words):
    low = pltpu.bitcast(words << 16, F32)
    high = pltpu.bitcast(words & jnp.uint32(0xFFFF0000), F32)
    return jnp.concatenate([low, high], axis=1)


def _inproj_kernel(pos_ref, x_ref, g_ref, w_ref, wfzt_ref, bfzt_ref,
                   dqt_ref, dk_ref, dvt_ref, fqt_ref, fk_ref, fvt_ref, c_ref, ct_ref, carry_t_ref):
    ts = x_ref.shape[1]
    xb = _rms(x_ref[0], g_ref[...]).astype(BF16)
    proj = jnp.dot(xb, w_ref[...], preferred_element_type=F32)
    nt = (((1,), (1,)), ((), ()))
    tn = (((0,), (0,)), ((), ()))

    def pieces(v):
        hi = v.astype(BF16)
        r1 = v - hi.astype(F32)
        mid = r1.astype(BF16)
        return hi, mid, (r1 - mid.astype(F32)).astype(BF16)

    def rows_to_lanes(v, select):
        return sum(lax.dot_general(part, select, tn, preferred_element_type=F32) for part in pieces(v))

    half_rot = ROT_DIM // 2
    freq = lax.broadcasted_iota(I32, (half_rot, 1), 0)
    inv_freq = jnp.power(ROPE_THETA, -(2 * freq).astype(F32) / ROT_DIM)
    ang = inv_freq * pos_ref[0].astype(F32)
    trig = jnp.concatenate([jnp.cos(ang), jnp.sin(ang)], axis=0)
    r = lax.broadcasted_iota(I32, (2 * half_rot, 2 * LANES), 0)
    lane2 = lax.broadcasted_iota(I32, (2 * half_rot, 2 * LANES), 1)
    d = lane2 % HEAD_DIM
    same_freq = jnp.where(d % half_rot == r % half_rot, 1.0, 0.0)
    cos_part = jnp.where((r < half_rot) & (lane2 < LANES) & (d < ROT_DIM), same_freq, 0.0)
    sin_part = jnp.where((r >= half_rot) & (lane2 >= LANES) & (d < ROT_DIM),
                         jnp.where(d < half_rot, -same_freq, same_freq), 0.0)
    tables = rows_to_lanes(trig, (cos_part + sin_part).astype(BF16))
    lane = lax.broadcasted_iota(I32, (1, LANES), 1) % HEAD_DIM
    cosf = jnp.where(lane < ROT_DIM, tables[:, :LANES], 1.0)
    sinf = tables[:, LANES:]
    first_half = lane < half_rot

    def rope(t):
        outs = []
        for j in range(t.shape[1] // LANES):
            tj = t[:, j * LANES:(j + 1) * LANES]
            partner = jnp.where(first_half,
                                pltpu.roll(tj, LANES - ROT_DIM // 2, 1),
                                pltpu.roll(tj, ROT_DIM // 2, 1))
            outs.append(tj * cosf + partner * sinf)
        return jnp.concatenate(outs, axis=1)

    scale = HEAD_DIM ** -0.5 * LOG2_E
    w = DIFF_WIDTH
    dqt_ref[0] = (rope(proj[:, 0:w]) * scale).astype(BF16).T
    dk_ref[0] = rope(proj[:, w:2 * w]).astype(BF16)
    dvt_ref[0, 0] = proj[:, 2 * w:3 * w].astype(BF16).T
    fqt_ref[0] = (proj[:, 3 * w:4 * w] * scale).astype(BF16).T
    fk_ref[0] = proj[:, 4 * w:5 * w].astype(BF16)
    fvt_ref[0, 0] = proj[:, 5 * w:6 * w].astype(BF16).T

    def log_sigmoid(z):
        return jnp.minimum(z, 0.0) - jnp.log1p(jnp.exp(-jnp.abs(z)))

    row = lax.broadcasted_iota(I32, (ts, ts), 0)
    col = lax.broadcasted_iota(I32, (ts, ts), 1)
    upto_col = jnp.where(row <= col, 1.0, 0.0).astype(BF16)

    @pl.when(pl.program_id(1) == 0)
    def _():
        carry_t_ref[...] = jnp.zeros_like(carry_t_ref)

    fzt = lax.dot_general(wfzt_ref[...], xb, nt, preferred_element_type=F32) + bfzt_ref[...]
    ct = carry_t_ref[...] + sum(jnp.dot(part, upto_col, preferred_element_type=F32)
                                for part in pieces(log_sigmoid(fzt)))
    carry_t_ref[...] = ct[:, ts - 1:ts]
    ct = ct * LOG2_E
    ct_ref[0] = ct[0:N_FOX_HEADS, :]
    head = lax.broadcasted_iota(I32, (ct.shape[0], FOX_WIDTH), 0)
    group = lax.broadcasted_iota(I32, (ct.shape[0], FOX_WIDTH), 1) // HEAD_DIM
    c_ref[0] = rows_to_lanes(ct, jnp.where(head == group, 1.0, 0.0).astype(BF16))


def _inproj(positions, x, g, w_main, w_fz_t, b_fz_t):
    b, s, d = x.shape
    ts = SEQ_TILE
    assert ts == ATTN_TILE
    tile = lambda width: pl.BlockSpec((1, ts, width), lambda i, j: (i, j, 0))
    tile_t = pl.BlockSpec((1, DIFF_WIDTH, ts), lambda i, j: (i, 0, j))
    key_tile_t = pl.BlockSpec((1, 1, DIFF_WIDTH, ts), lambda i, j: (i, j, 0, 0))
    full = lambda arr: pl.BlockSpec(arr.shape, lambda i, j: (0,) * arr.ndim)
    rows = jax.ShapeDtypeStruct((b, s, DIFF_WIDTH), BF16)
    rows_t = jax.ShapeDtypeStruct((b, DIFF_WIDTH, s), BF16)
    key_tiles_t = jax.ShapeDtypeStruct((b, s // ts, DIFF_WIDTH, ts), BF16)
    return pl.pallas_call(
        _inproj_kernel,
        grid=(b, s // ts),
        in_specs=[pl.BlockSpec((1, 1, ts), lambda i, j: (i, 0, j)), tile(d), full(g), full(w_main),
                  full(w_fz_t), full(b_fz_t)],
        out_specs=[tile_t, tile(DIFF_WIDTH), key_tile_t, tile_t, tile(FOX_WIDTH), key_tile_t,
                   tile(FOX_WIDTH), pl.BlockSpec((1, N_FOX_HEADS, ts), lambda i, j: (i, 0, j))],
        out_shape=[rows_t, rows, key_tiles_t, rows_t, rows, key_tiles_t,
                   jax.ShapeDtypeStruct((b, s, FOX_WIDTH), F32),
                   jax.ShapeDtypeStruct((b, N_FOX_HEADS, s), F32)],
        scratch_shapes=[pltpu.VMEM((w_fz_t.shape[0], 1), F32)],
        compiler_params=_params("arbitrary", "arbitrary"),
        name="inproj",
    )(positions.reshape(b, 1, s), x, g, w_main, w_fz_t, b_fz_t)


ONES_ROWS = 16


def _block_diag_queries(qt):
    row = lax.broadcasted_iota(I32, qt.shape, 0)
    zero = jnp.zeros_like(qt)
    return jnp.concatenate([jnp.where(row < HEAD_DIM, qt, zero),
                            jnp.where(row >= HEAD_DIM, qt, zero)], axis=1)


class _Chain:
    def __init__(self, qbd, keys, values_t, scratch, key_bias=None, query_bias=None):
        self.qbd, self.keys, self.values_t = qbd, keys, values_t
        self.acc_ref, self.s_ref, self.p_ref, self.stat_ref = scratch
        self.key_bias, self.query_bias = key_bias, query_bias


_RUNNING_MAX, _TILE_MAX, _RESCALE = 0, 1, 2


def _causal_attention(chains, qi):
    tk, tq = chains[0].s_ref.shape[0], chains[0].s_ref.shape[1] // 2
    ones = jnp.ones((ONES_ROWS, tk), BF16)

    def scores(ch, i, diagonal):
        s = jnp.dot(ch.keys(i), ch.qbd, preferred_element_type=F32)
        if ch.key_bias is not None:
            s = s - ch.key_bias(i)
        if diagonal is not False:
            r = lax.broadcasted_iota(I32, s.shape, 0)
            c = lax.broadcasted_iota(I32, s.shape, 1) & (tq - 1)
            visible = r <= c if diagonal is True else jnp.logical_or(r <= c, jnp.logical_not(diagonal))
            s = jnp.where(visible, s, -jnp.inf)
        ch.s_ref[...] = s
        top = jnp.max(s, axis=0, keepdims=True)
        if ch.query_bias is not None:
            top = top + ch.query_bias
        ch.stat_ref[_TILE_MAX:_TILE_MAX + 1, :] = top

    def probs(ch):
        m = ch.stat_ref[_RUNNING_MAX:_RUNNING_MAX + 1, :]
        m_new = jnp.maximum(m, ch.stat_ref[_TILE_MAX:_TILE_MAX + 1, :])
        ch.stat_ref[_RUNNING_MAX:_RUNNING_MAX + 1, :] = m_new
        ch.stat_ref[_RESCALE:_RESCALE + 1, :] = jnp.exp2(m - m_new)
        ref = m_new if ch.query_bias is None else m_new - ch.query_bias
        ch.p_ref[...] = jnp.exp2(ch.s_ref[...] - ref).astype(BF16)

    def accumulate(ch, i):
        vts = jnp.concatenate([ch.values_t(i), ones], axis=0)
        ch.acc_ref[...] = (ch.stat_ref[_RESCALE:_RESCALE + 1, :] * ch.acc_ref[...]
                           + jnp.dot(vts, ch.p_ref[...], preferred_element_type=F32))

    def body(k, diagonal_next):
        for ch in chains:
            accumulate(ch, jnp.maximum(k - 1, 0))
        for ch in chains:
            probs(ch)
        for ch in chains:
            scores(ch, k + 1, diagonal_next)

    for ch in chains:
        ch.acc_ref[...] = jnp.zeros_like(ch.acc_ref)
        ch.p_ref[...] = jnp.zeros_like(ch.p_ref)
        ch.stat_ref[...] = jnp.concatenate(
            [jnp.full((1, 2 * tq), -jnp.inf, F32), jnp.ones((ch.stat_ref.shape[0] - 1, 2 * tq), F32)], axis=0)
        scores(ch, 0, qi == 0)

    def loop_body(k, carry):
        body(k, False)
        return carry

    lax.fori_loop(0, qi - 1, loop_body, 0)

    @pl.when(qi >= 1)
    def _():
        body(qi - 1, True)

    for ch in chains:
        accumulate(ch, jnp.maximum(qi - 1, 0))
    for ch in chains:
        probs(ch)
    for ch in chains:
        accumulate(ch, qi)


def _attention_kernel(lq1_ref, lk1_ref, lq2_ref, lk2_ref, g_ref,
                      dqt_ref, dk_ref, dvt_ref, fqt_ref, fk_ref, fvt_ref, cq_ref, ck_ref,
                      do_ref, fo_ref, *scratch, lambda_init):
    tq = dqt_ref.shape[2]
    tk = dvt_ref.shape[3]
    units = dqt_ref.shape[1] // LANES
    lane = lax.broadcasted_iota(I32, (1, LANES), 1)
    scratch = [scratch[i:i + 4] for i in range(0, len(scratch), 4)]

    def unit_chains(u):
        lanes = slice(u * LANES, (u + 1) * LANES)

        def tile(ref):
            return lambda i: ref[0, pl.ds(pl.multiple_of(i * tk, tk), tk), lanes]

        def tile_t(ref):
            return lambda i: ref[0, i, lanes, :]

        cq = jnp.concatenate([cq_ref[0, u, 0:1, :], cq_ref[0, u, 1:2, :]], axis=1)

        def key_bias(i):
            ck = tile(ck_ref)(i)
            other = pltpu.roll(ck, HEAD_DIM, 1)
            first = jnp.where(lane < HEAD_DIM, ck, other)
            second = jnp.where(lane >= HEAD_DIM, ck, other)
            return jnp.concatenate([first] * (tq // LANES) + [second] * (tq // LANES), axis=1)

        return [_Chain(_block_diag_queries(dqt_ref[0, lanes, :]), tile(dk_ref), tile_t(dvt_ref),
                       scratch[2 * u]),
                _Chain(_block_diag_queries(fqt_ref[0, lanes, :]), tile(fk_ref), tile_t(fvt_ref),
                       scratch[2 * u + 1], key_bias, cq)]

    chains = [ch for u in range(units) for ch in unit_chains(u)]
    _causal_attention(chains, pl.program_id(2))

    lam = (jnp.exp(jnp.sum(lq1_ref[...] * lk1_ref[...], axis=-1, keepdims=True))
           - jnp.exp(jnp.sum(lq2_ref[...] * lk2_ref[...], axis=-1, keepdims=True)) + lambda_init)
    dv, h = DIFF_V_DIM, HEAD_DIM
    for u in range(units):
        lanes = slice(u * LANES, (u + 1) * LANES)
        acc = chains[2 * u].acc_ref
        o = (acc[0:dv, 0:tq] / acc[dv:dv + 1, 0:tq]
             - lam * (acc[0:dv, tq:] / acc[dv:dv + 1, tq:]))
        ms = jnp.mean(o * o, axis=0, keepdims=True)
        do_ref[0, :, lanes] = (o * lax.rsqrt(ms + NORM_EPS) * g_ref[...]
                               * (1.0 - lambda_init)).astype(BF16).T
        acc = chains[2 * u + 1].acc_ref
        fo_ref[0, :, lanes] = jnp.concatenate([acc[0:h, 0:tq] / acc[2 * h:2 * h + 1, 0:tq],
                                               acc[h:2 * h, tq:] / acc[2 * h:2 * h + 1, tq:]],
                                              axis=0).astype(BF16).T


def _attention(dqt, dk, dvt, fqt, fk, fvt, c_rows, c_cols, lq1, lk1, lq2, lk2, g_col, lambda_init):
    b, s, _ = dk.shape
    t = ATTN_TILE
    assert N_DIFF_HEADS == N_FOX_HEADS // 2 and N_DIFF_HEADS % ATTN_UNITS == 0
    width = ATTN_UNITS * LANES
    vec = lambda arr: pl.BlockSpec(arr.shape, lambda i, h, j: (0, 0))
    q_spec = pl.BlockSpec((1, width, t), lambda i, h, j: (i, h, j))
    k_spec = pl.BlockSpec((1, s, width), lambda i, h, j: (i, 0, h))
    v_spec = pl.BlockSpec((1, s // t, width, t), lambda i, h, j: (i, 0, h, 0))
    o_spec = pl.BlockSpec((1, t, width), lambda i, h, j: (i, j, h))
    out = jax.ShapeDtypeStruct((b, s, DIFF_WIDTH), BF16)
    chain_scratch = [pltpu.VMEM((LANES + ONES_ROWS, 2 * t), F32),
                     pltpu.VMEM((t, 2 * t), F32),
                     pltpu.VMEM((t, 2 * t), BF16),
                     pltpu.VMEM((8, 2 * t), F32)]
    return pl.pallas_call(
        functools.partial(_attention_kernel, lambda_init=lambda_init),
        grid=(b, N_DIFF_HEADS // ATTN_UNITS, s // t),
        in_specs=[vec(lq1), vec(lk1), vec(lq2), vec(lk2), vec(g_col),
                  q_spec, k_spec, v_spec, q_spec, k_spec, v_spec,
                  pl.BlockSpec((1, ATTN_UNITS, 2, t), lambda i, h, j: (i, h, 0, j)), k_spec],
        out_specs=[o_spec, o_spec],
        out_shape=[out, out],
        scratch_shapes=chain_scratch * (2 * ATTN_UNITS),
        compiler_params=_params("arbitrary", "arbitrary", "arbitrary"),
        name="attention",
    )(lq1, lk1, lq2, lk2, g_col, dqt, dk, dvt, fqt, fk, fvt, c_rows, c_cols)


def _router_kernel(d_ref, f_ref, x_ref, wo_ref, g_ref, wrt_ref, brt_ref,
                   h_ref, xn_ref, gate_ref, idx_ref, rank_ref, count_ref, carry_ref):
    tm = ROUTER_SUBTILE
    nt = (((1,), (1,)), ((), ()))

    def pieces(v, count):
        out = []
        for _ in range(count):
            part = v.astype(BF16)
            out.append(part)
            v = v - part.astype(F32)
        return out

    wh, wl = pieces(wrt_ref[...], 2)
    row = lax.broadcasted_iota(I32, (tm, tm), 0)
    col = lax.broadcasted_iota(I32, (tm, tm), 1)
    earlier = jnp.where(row < col, 1.0, 0.0).astype(BF16)
    eye = jnp.where(row == col, 1.0, 0.0).astype(BF16)

    @pl.when(pl.program_id(0) == 0)
    def _():
        carry_ref[...] = jnp.zeros_like(carry_ref)

    subs = [slice(i * tm, (i + 1) * tm) for i in range(x_ref.shape[0] // tm)]

    def logits(rows):
        mixed = (jnp.dot(d_ref[rows, :], wo_ref[0:DIFF_WIDTH, :], preferred_element_type=F32)
                 + jnp.dot(f_ref[rows, :], wo_ref[DIFF_WIDTH:, :], preferred_element_type=F32))
        h = x_ref[rows, :] + mixed
        h_ref[rows, :] = h
        xn = _rms(h, g_ref[...])
        xn_ref[rows, :] = _pack_rows(xn)
        xh, xl = pieces(xn, 2)
        return (lax.dot_general(wh, xh, nt, preferred_element_type=F32)
                + lax.dot_general(wh, xl, nt, preferred_element_type=F32)
                + lax.dot_general(wl, xh, nt, preferred_element_type=F32)) + brt_ref[...]

    def top_k(work):
        expert = lax.broadcasted_iota(I32, work.shape, 0)
        vals, ids, picks = [], [], []
        for _ in range(TOP_K):
            m = jnp.max(work, axis=0, keepdims=True)
            idx = jnp.min(jnp.where(work == m, expert, N_EXPERTS), axis=0, keepdims=True)
            pick = expert == idx
            vals.append(m)
            ids.append(idx)
            picks.append(pick)
            work = jnp.where(pick, -jnp.inf, work)
        exps = [jnp.exp(v - vals[0]) for v in vals]
        denom = exps[0] + exps[1] + exps[2] + exps[3]
        return ids, picks, [e / denom for e in exps]

    def ranks(picks):
        chosen = jnp.where(picks[0] | picks[1] | picks[2] | picks[3], 1.0, 0.0)
        rank = jnp.dot(chosen.astype(BF16), earlier, preferred_element_type=F32) + carry_ref[...]
        carry_ref[...] = carry_ref[...] + jnp.sum(chosen, axis=1, keepdims=True)
        return rank

    def emit(rows, ids, picks, gates, rank):
        slot = lax.broadcasted_iota(I32, (8, tm), 0)
        gate_t = jnp.zeros((8, tm), F32)
        idx_t = jnp.zeros((8, tm), I32)
        rank_t = jnp.zeros((8, tm), F32)
        for k in range(TOP_K):
            gate_t = jnp.where(slot == k, gates[k], gate_t)
            idx_t = jnp.where(slot == k, ids[k], idx_t)
            rank_t = jnp.where(slot == k, jnp.sum(jnp.where(picks[k], rank, 0.0), axis=0, keepdims=True),
                               rank_t)
        idx_ref[:, rows] = idx_t
        rank_ref[:, rows] = rank_t.astype(I32)
        gate_rows = jnp.concatenate([gate_t, jnp.zeros((LANES - 8, tm), F32)], axis=0)
        gate_ref[rows, :] = sum(lax.dot_general(eye, part, nt, preferred_element_type=F32)
                                for part in pieces(gate_rows, 3))

    work = [logits(rows) for rows in subs]
    chosen = [top_k(w) for w in work]
    rank = [ranks(picks) for _, picks, _ in chosen]
    for rows, (ids, picks, gates), r in zip(subs, chosen, rank):
        emit(rows, ids, picks, gates, r)
    count_ref[...] = jnp.broadcast_to(carry_ref[...], count_ref.shape)


def _router(d_out, f_out, x, w_out, g_moe, w_router_t, b_router_t):
    n, d = x.shape
    tm = ROUTER_TILE
    tile = lambda width: pl.BlockSpec((tm, width), lambda i: (i, 0))
    rows = pl.BlockSpec((8, tm), lambda i: (0, i))
    full = lambda arr: pl.BlockSpec(arr.shape, lambda i: (0,) * arr.ndim)
    return pl.pallas_call(
        _router_kernel,
        grid=(n // tm,),
        in_specs=[tile(DIFF_WIDTH), tile(FOX_WIDTH), tile(d), full(w_out), full(g_moe),
                  full(w_router_t), full(b_router_t)],
        out_specs=[tile(d), tile(d // 2), tile(LANES), rows, rows,
                   pl.BlockSpec((N_EXPERTS, LANES), lambda i: (0, 0))],
        out_shape=[jax.ShapeDtypeStruct((n, d), F32), jax.ShapeDtypeStruct((n, d // 2), U32),
                   jax.ShapeDtypeStruct((n, LANES), F32), jax.ShapeDtypeStruct((8, n), I32),
                   jax.ShapeDtypeStruct((8, n), I32), jax.ShapeDtypeStruct((N_EXPERTS, LANES), F32)],
        scratch_shapes=[pltpu.VMEM((N_EXPERTS, 1), F32)],
        compiler_params=_params("arbitrary"),
        name="router",
    )(d_out, f_out, x, w_out, g_moe, w_router_t, b_router_t)


def _slot_kernel(starts_ref, idx_ref, rank_ref, pos_ref):
    idx = idx_ref[...]
    pos = rank_ref[...]
    for e in range(N_EXPERTS):
        pos = jnp.where(idx == e, pos + starts_ref[e], pos)
    pos_ref[...] = pos


def _slots(starts, idx, rank):
    rows, n = idx.shape
    width = min(n, 2048)
    block = pl.BlockSpec((rows, width), lambda i, starts: (0, i))
    return pl.pallas_call(
        _slot_kernel,
        grid_spec=pltpu.PrefetchScalarGridSpec(num_scalar_prefetch=1, grid=(n // width,),
                                               in_specs=[block, block], out_specs=block),
        out_shape=jax.ShapeDtypeStruct((rows, n), I32),
        compiler_params=_params("arbitrary"),
        name="slots",
    )(starts, idx, rank)


SC_WINDOW = 128
SC_COLUMNS = 256


def _dispatch_sparsecore(pos, xn, n_rows):
    n, d = xn.shape
    windows = n // SC_WINDOW
    mesh = plsc.VectorSubcoreMesh(core_axis_name="core", subcore_axis_name="subcore")

    @functools.partial(pl.kernel, out_type=jax.ShapeDtypeStruct((n_rows, d), xn.dtype), mesh=mesh,
                       scratch_types=[], name="dispatch_sc")
    def scatter(x_hbm, pos_hbm, xs_hbm):
        for c in range(d // SC_COLUMNS):
            def body(x_vmem, pos_vmem, c=c):
                pltpu.sync_copy(x_vmem, xs_hbm.at[:, pl.ds(c * SC_COLUMNS, SC_COLUMNS)].at[pos_vmem.at[0]])

            pltpu.emit_pipeline(
                body, grid=(TOP_K * windows,),
                in_specs=[pl.BlockSpec((SC_WINDOW, SC_COLUMNS), index_map=lambda i, c=c: (i % windows, c)),
                          pl.BlockSpec((1, SC_WINDOW), index_map=lambda i: (0, i))],
                out_specs=[], core_axis_name=("core", "subcore"),
                dimension_semantics=(pltpu.PARALLEL,))(x_hbm, pos_hbm)

    return scatter(xn, pos.reshape(1, -1))


def _expert_kernel(tile_expert_ref, n_used_ref, next_expert_ref, slot_ref,
                   x_ref, w1_ref, b1_ref, w2_ref, b2_ref, y_ref,
                   w1f_ref, w2f_ref, sem_ref, w1b_ref, w2p_ref, w2b_ref):
    t = pl.program_id(0)
    fresh = (t == 0) | (tile_expert_ref[t] != tile_expert_ref[jnp.maximum(t - 1, 0)])
    used = t < n_used_ref[0]

    def weight_copies(e, s):
        return (pltpu.make_async_copy(w1_ref.at[e], w1f_ref.at[s], sem_ref.at[0, s]),
                pltpu.make_async_copy(w2_ref.at[e], w2f_ref.at[s], sem_ref.at[1, s]))

    @pl.when(t == 0)
    def _():
        for copy in weight_copies(tile_expert_ref[0], slot_ref[0]):
            copy.start()

    @pl.when(fresh & used)
    def _():
        s = slot_ref[t]
        for copy in weight_copies(tile_expert_ref[t], s):
            copy.wait()

        @pl.when(next_expert_ref[t] >= 0)
        def _():
            for copy in weight_copies(next_expert_ref[t], 1 - s):
                copy.start()

        rows = LANES
        half = LANES // 2
        for r in range(0, w1b_ref.shape[0], rows):
            w1b_ref[r:r + rows, :] = w1f_ref[s, r:r + rows, :].astype(BF16)
        for c in range(w2p_ref.shape[0]):
            lanes = slice(c * LANES, (c + 1) * LANES)
            for r in range(0, w2b_ref.shape[0], rows):
                for b in range(2):
                    w2p_ref[c, pl.ds(r + b, half, stride=2), :] = (
                        w2f_ref[s, r + b * half:r + (b + 1) * half, lanes])
            w2b_ref[:, lanes] = w2p_ref[c].astype(BF16)

    @pl.when(used)
    def _():
        xb = _unpack_rows(x_ref[...]).astype(BF16)
        h = jnp.dot(xb, w1b_ref[...], preferred_element_type=F32) + b1_ref[0]
        even = lax.broadcasted_iota(I32, (1, LANES), 1) % 2 == 0
        parts = []
        for q in range(h.shape[1] // (2 * LANES)):
            a = h[:, (2 * q) * LANES:(2 * q + 1) * LANES]
            b = h[:, (2 * q + 1) * LANES:(2 * q + 2) * LANES]
            hg = jnp.where(even, a, pltpu.roll(b, 1, 1))
            hl = jnp.where(even, pltpu.roll(a, LANES - 1, 1), b)
            glu = jnp.minimum(hg, SWIGLU_LIMIT)
            lin = jnp.clip(hl, -SWIGLU_LIMIT, SWIGLU_LIMIT)
            parts.append((glu * jax.nn.sigmoid(SWIGLU_ALPHA * glu) * (lin + 1.0)).astype(BF16))
        act = jnp.concatenate(parts, axis=1)
        y_ref[...] = _pack_rows(jnp.dot(act, w2b_ref[...], preferred_element_type=F32) + b2_ref[0])

    @pl.when(jnp.logical_not(used))
    def _():
        y_ref[...] = jnp.zeros_like(y_ref)


def _experts(tile_expert, n_used, next_expert, slot, xs, w1, b1, w2, b2):
    n_rows, d = xs.shape
    tm = EXPERT_TILE
    by_expert = lambda arr: pl.BlockSpec((1,) + arr.shape[1:], lambda i, te, *_: (te[i], 0, 0))
    rows = pl.BlockSpec((tm, d), lambda i, *_: (i, 0))
    used_rows = pl.BlockSpec((tm, d), lambda i, te, nu, *_: (jnp.minimum(i, nu[0] - 1), 0))
    hbm = pl.BlockSpec(memory_space=pl.ANY)
    return pl.pallas_call(
        _expert_kernel,
        grid_spec=pltpu.PrefetchScalarGridSpec(
            num_scalar_prefetch=4,
            grid=(n_rows // tm,),
            in_specs=[used_rows, hbm, by_expert(b1), hbm, by_expert(b2)],
            out_specs=rows,
            scratch_shapes=[pltpu.VMEM((2,) + w1.shape[1:], F32), pltpu.VMEM((2,) + w2.shape[1:], F32),
                            pltpu.SemaphoreType.DMA((2, 2)),
                            pltpu.VMEM(w1.shape[1:], BF16),
                            pltpu.VMEM((w2.shape[2] // LANES, w2.shape[1], LANES), F32),
                            pltpu.VMEM(w2.shape[1:], BF16)]),
        out_shape=jax.ShapeDtypeStruct(xs.shape, xs.dtype),
        compiler_params=pltpu.CompilerParams(dimension_semantics=("arbitrary",),
                                             vmem_limit_bytes=EXPERT_VMEM_LIMIT),
        name="experts",
    )(tile_expert, n_used, next_expert, slot, xs, w1, b1, w2, b2)


def _gather_sparsecore(pos, y):
    d = y.shape[1]
    total = pos.size
    mesh = plsc.VectorSubcoreMesh(core_axis_name="core", subcore_axis_name="subcore")

    @functools.partial(pl.kernel, out_type=jax.ShapeDtypeStruct((total, d), y.dtype), mesh=mesh,
                       scratch_types=[], name="gather_sc")
    def gather(y_hbm, pos_hbm, out_hbm):
        for c in range(d // SC_COLUMNS):
            def body(pos_vmem, out_vmem, c=c):
                pltpu.sync_copy(y_hbm.at[:, pl.ds(c * SC_COLUMNS, SC_COLUMNS)].at[pos_vmem.at[0]], out_vmem)

            pltpu.emit_pipeline(
                body, grid=(total // SC_WINDOW,),
                in_specs=[pl.BlockSpec((1, SC_WINDOW), index_map=lambda i: (0, i))],
                out_specs=[pl.BlockSpec((SC_WINDOW, SC_COLUMNS), index_map=lambda i, c=c: (i, c))],
                core_axis_name=("core", "subcore"),
                dimension_semantics=(pltpu.PARALLEL,))(pos_hbm, out_hbm)

    return gather(y, pos.reshape(1, -1))


def _combine_kernel(h_ref, gate_ref, p_ref, wg_ref, wp_ref, gple_ref, gfin_ref, *refs, final_norm):
    y_refs, o_ref = refs[:TOP_K], refs[TOP_K]
    gates = gate_ref[...]
    moe = gates[:, 0:1] * _unpack_rows(y_refs[0][...])
    for k in range(1, TOP_K):
        moe = moe + gates[:, k:k + 1] * _unpack_rows(y_refs[k][...])
    h = h_ref[...] + moe
    gate = jax.nn.sigmoid(jnp.dot(_rms(h, gple_ref[...]).astype(BF16), wg_ref[...],
                                  preferred_element_type=F32))
    emb = jnp.dot(p_ref[...].astype(BF16), wp_ref[...], preferred_element_type=F32)
    h = h + gate * emb
    o_ref[...] = _rms(h, gfin_ref[...]) if final_norm else h


def _combine(h, gates, p, w_gate, w_proj, g_ple, g_final, y_tok, final_norm):
    n, d = h.shape
    tm = TOKEN_TILE
    tile = lambda width: pl.BlockSpec((tm, width), lambda i: (i, 0))
    full = lambda arr: pl.BlockSpec(arr.shape, lambda i: (0,) * arr.ndim)
    choice = lambda k: pl.BlockSpec((tm, y_tok.shape[1]), lambda i: (k * (n // tm) + i, 0))
    return pl.pallas_call(
        functools.partial(_combine_kernel, final_norm=final_norm),
        grid=(n // tm,),
        in_specs=[tile(d), tile(LANES), tile(p.shape[1]), full(w_gate), full(w_proj),
                  full(g_ple), full(g_final)] + [choice(k) for k in range(TOP_K)],
        out_specs=tile(d),
        out_shape=jax.ShapeDtypeStruct((n, d), F32),
        compiler_params=_params("arbitrary"),
        name="combine",
    )(h, gates, p, w_gate, w_proj, g_ple, g_final, *([y_tok] * TOP_K))


def _layer(h, p, positions, g_attn, w_in, lq1, lk1, lq2, lk2, g_subln, b_forget, w_out, g_moe,
           w_router, b_router, w_e1, b_e1, w_e2, b_e2, g_ple, w_ple_gate, w_ple_proj, g_final,
           lambda_init, final_norm):
    b, s, d = h.shape
    n = b * s
    row = lambda v: v.reshape(1, -1)

    w_main = w_in[:, :MAIN_WIDTH].astype(BF16)
    pad_heads = ((0, 2 * N_FOX_HEADS - N_FOX_HEADS), (0, 0))
    w_fz_t = jnp.pad(w_in[:, MAIN_WIDTH:].T, pad_heads).astype(BF16)
    b_fz_t = jnp.pad(b_forget.reshape(-1, 1), pad_heads)
    b1 = b_e1[:, None, :]
    b2 = b_e2[:, None, :]

    dq_t, dk, dv_t, fq_t, fk, fv_t, c, c_t = _inproj(positions, h, row(g_attn), w_main, w_fz_t, b_fz_t)
    c_rows = c_t.reshape(b, N_FOX_HEADS // 2, 2, s)
    d_out, f_out = _attention(dq_t, dk, dv_t, fq_t, fk, fv_t, c_rows, c, row(lq1), row(lk1), row(lq2),
                              row(lk2), g_subln.reshape(-1, 1), lambda_init)

    h1, xn, gates, idx, rank, counts = _router(
        d_out.reshape(n, DIFF_WIDTH), f_out.reshape(n, FOX_WIDTH), h.reshape(n, d),
        w_out.astype(BF16), row(g_moe), w_router.T, b_router.reshape(-1, 1))

    te = EXPERT_TILE
    n_rows = n * TOP_K + N_EXPERTS * te
    counts = counts[:, 0].astype(I32)
    padded = (counts + te - 1) // te * te
    ends = jnp.cumsum(padded)
    starts = ends - padded
    pos = _slots(starts.astype(I32), idx, rank)[:TOP_K]
    tile_start = jnp.arange(n_rows // te, dtype=I32) * te
    tile_expert = jnp.minimum(jnp.sum((ends[None, :] <= tile_start[:, None]).astype(I32), axis=1),
                              N_EXPERTS - 1)
    n_used = (ends[-1:] // te).astype(I32)
    owns = padded > 0
    later = lax.cummin(jnp.where(owns, jnp.arange(N_EXPERTS, dtype=I32), N_EXPERTS), reverse=True)
    following = jnp.concatenate([later[1:], jnp.full((1,), N_EXPERTS, I32)])
    following = jnp.where(following < N_EXPERTS, following, -1)
    next_expert = following[tile_expert]
    slot = ((jnp.cumsum(owns.astype(I32)) - 1) % 2)[tile_expert]

    xs = _dispatch_sparsecore(pos.astype(I32), xn, n_rows)
    y = _experts(tile_expert, n_used, next_expert.astype(I32), slot.astype(I32), xs, w_e1, b1, w_e2, b2)
    y_tok = _gather_sparsecore(pos.astype(I32), y)
    out = _combine(h1, gates, p.reshape(n, PLE_DIM), w_ple_gate.astype(BF16),
                   w_ple_proj.astype(BF16), row(g_ple), row(g_final), y_tok, final_norm)
    return out.reshape(b, s, d)


def kernel(x, p, positions, g_attn, w_in, lambda_q1, lambda_k1, lambda_q2, lambda_k2, g_subln, b_forget, w_out, g_moe, w_router, b_router, w_e1, b_e1, w_e2, b_e2, g_ple, w_ple_gate, w_ple_proj, g_final):
    depth = g_attn.shape[0]
    h = x
    for i in range(depth):
        lambda_init = 0.8 - 0.6 * math.exp(-0.3 * i)
        h = _layer(h, p[i], positions, g_attn[i], w_in[i], lambda_q1[i], lambda_k1[i],
                   lambda_q2[i], lambda_k2[i], g_subln[i], b_forget[i], w_out[i], g_moe[i],
                   w_router[i], b_router[i], w_e1[i], b_e1[i], w_e2[i], b_e2[i], g_ple[i],
                   w_ple_gate[i], w_ple_proj[i], g_final, lambda_init, i == depth - 1)
    return h
```

```python
import functools
import math

import jax
import jax.numpy as jnp
from jax import lax
from jax.experimental import pallas as pl
from jax.experimental.pallas import tpu as pltpu
from jax.experimental.pallas import tpu_sc as plsc

F32 = jnp.float32
BF16 = jnp.bfloat16
I32 = jnp.int32
U32 = jnp.uint32

LANES = 128
D_MODEL = 1024
HEAD_DIM = 64
N_DIFF_HEADS = 4
DIFF_V_DIM = 2 * HEAD_DIM
N_FOX_HEADS = 8
DIFF_WIDTH = N_DIFF_HEADS * DIFF_V_DIM
FOX_WIDTH = N_FOX_HEADS * HEAD_DIM
ROT_DIM = HEAD_DIM // 4
ROPE_THETA = 500000.0
N_EXPERTS = 32
TOP_K = 4
D_EXPERT = D_MODEL
SWIGLU_ALPHA = 1.702
SWIGLU_LIMIT = 7.0
PLE_DIM = 256
NORM_EPS = 1e-5
LOG2_E = math.log2(math.e)
MAIN_WIDTH = 3 * DIFF_WIDTH + 3 * FOX_WIDTH

SEQ_TILE = 1024
SEQ_SUBTILE = 256
ATTN_TILE = 256
ATTN_UNITS = 2
TOKEN_TILE = 1024
COMBINE_SUBTILE = 256
ROUTER_TILE = 1024
ROUTER_SUBTILE = 256
EXPERT_TILE = 256
VMEM_LIMIT = 48 * 1024 * 1024
EXPERT_VMEM_LIMIT = 58 * 1024 * 1024


def _params(*semantics):
    return pltpu.CompilerParams(dimension_semantics=semantics, vmem_limit_bytes=VMEM_LIMIT)


def _rms(x, g):
    return x * lax.rsqrt(jnp.mean(x * x, axis=-1, keepdims=True) + NORM_EPS) * g


def _pack_rows(v):
    half = v.shape[1] // 2
    bits = pltpu.bitcast(v.astype(BF16).astype(F32), U32)
    return (bits[:, :half] >> 16) | bits[:, half:]


def _unpack_rows(words):
    low = pltpu.bitcast(words << 16, F32)
    high = pltpu.bitcast(words & jnp.uint32(0xFFFF0000), F32)
    return jnp.concatenate([low, high], axis=1)


def _inproj_kernel(pos_ref, x_ref, g_ref, w_ref, wfzt_ref, bfzt_ref,
                   dqt_ref, dk_ref, dvt_ref, fqt_ref, fk_ref, fvt_ref, c_ref, ct_ref, carry_t_ref):
    ts = SEQ_SUBTILE
    subs = [slice(i * ts, (i + 1) * ts) for i in range(x_ref.shape[1] // ts)]
    xbs = [_rms(x_ref[0, rows, :], g_ref[...]).astype(BF16) for rows in subs]
    projs = [jnp.dot(xb, w_ref[...], preferred_element_type=F32) for xb in xbs]
    nt = (((1,), (1,)), ((), ()))
    tn = (((0,), (0,)), ((), ()))

    def pieces(v):
        hi = v.astype(BF16)
        r1 = v - hi.astype(F32)
        mid = r1.astype(BF16)
        return hi, mid, (r1 - mid.astype(F32)).astype(BF16)

    def rows_to_lanes(v, select):
        return sum(lax.dot_general(part, select, tn, preferred_element_type=F32) for part in pieces(v))

    half_rot = ROT_DIM // 2
    freq = lax.broadcasted_iota(I32, (half_rot, 1), 0)
    inv_freq = jnp.power(ROPE_THETA, -(2 * freq).astype(F32) / ROT_DIM)
    r = lax.broadcasted_iota(I32, (2 * half_rot, 2 * LANES), 0)
    lane2 = lax.broadcasted_iota(I32, (2 * half_rot, 2 * LANES), 1)
    d = lane2 % HEAD_DIM
    same_freq = jnp.where(d % half_rot == r % half_rot, 1.0, 0.0)
    cos_part = jnp.where((r < half_rot) & (lane2 < LANES) & (d < ROT_DIM), same_freq, 0.0)
    sin_part = jnp.where((r >= half_rot) & (lane2 >= LANES) & (d < ROT_DIM),
                         jnp.where(d < half_rot, -same_freq, same_freq), 0.0)
    expand = (cos_part + sin_part).astype(BF16)
    lane = lax.broadcasted_iota(I32, (1, LANES), 1) % HEAD_DIM
    first_half = lane < half_rot

    def rope_tables(rows):
        ang = inv_freq * pos_ref[0, :, rows].astype(F32)
        trig = jnp.concatenate([jnp.cos(ang), jnp.sin(ang)], axis=0)
        tables = rows_to_lanes(trig, expand)
        return jnp.where(lane < ROT_DIM, tables[:, :LANES], 1.0), tables[:, LANES:]

    def rope(t, cosf, sinf):
        outs = []
        for j in range(t.shape[1] // LANES):
            tj = t[:, j * LANES:(j + 1) * LANES]
            partner = jnp.where(first_half,
                                pltpu.roll(tj, LANES - ROT_DIM // 2, 1),
                                pltpu.roll(tj, ROT_DIM // 2, 1))
            outs.append(tj * cosf + partner * sinf)
        return jnp.concatenate(outs, axis=1)

    scale = HEAD_DIM ** -0.5 * LOG2_E
    w = DIFF_WIDTH
    for i, (rows, proj) in enumerate(zip(subs, projs)):
        cosf, sinf = rope_tables(rows)
        dqt_ref[0, :, rows] = (rope(proj[:, 0:w], cosf, sinf) * scale).astype(BF16).T
        dk_ref[0, rows, :] = rope(proj[:, w:2 * w], cosf, sinf).astype(BF16)
        dvt_ref[0, i] = proj[:, 2 * w:3 * w].astype(BF16).T
        fqt_ref[0, :, rows] = (proj[:, 3 * w:4 * w] * scale).astype(BF16).T
        fk_ref[0, rows, :] = proj[:, 4 * w:5 * w].astype(BF16)
        fvt_ref[0, i] = proj[:, 5 * w:6 * w].astype(BF16).T

    def log_sigmoid(z):
        return jnp.minimum(z, 0.0) - jnp.log1p(jnp.exp(-jnp.abs(z)))

    row = lax.broadcasted_iota(I32, (ts, ts), 0)
    col = lax.broadcasted_iota(I32, (ts, ts), 1)
    upto_col = jnp.where(row <= col, 1.0, 0.0).astype(BF16)
    heads = wfzt_ref.shape[0]
    head = lax.broadcasted_iota(I32, (heads, FOX_WIDTH), 0)
    group = lax.broadcasted_iota(I32, (heads, FOX_WIDTH), 1) // HEAD_DIM
    head_lanes = jnp.where(head == group, 1.0, 0.0).astype(BF16)

    @pl.when(pl.program_id(1) == 0)
    def _():
        carry_t_ref[...] = jnp.zeros_like(carry_t_ref)

    for rows, xb in zip(subs, xbs):
        fzt = lax.dot_general(wfzt_ref[...], xb, nt, preferred_element_type=F32) + bfzt_ref[...]
        ct = carry_t_ref[...] + sum(jnp.dot(part, upto_col, preferred_element_type=F32)
                                    for part in pieces(log_sigmoid(fzt)))
        carry_t_ref[...] = ct[:, ts - 1:ts]
        ct = ct * LOG2_E
        ct_ref[0, :, rows] = ct[0:N_FOX_HEADS, :]
        c_ref[0, rows, :] = rows_to_lanes(ct, head_lanes)


def _inproj(positions, x, g, w_main, w_fz_t, b_fz_t):
    b, s, d = x.shape
    ts = SEQ_TILE
    sub = SEQ_SUBTILE
    assert sub == ATTN_TILE
    tile = lambda width: pl.BlockSpec((1, ts, width), lambda i, j: (i, j, 0))
    tile_t = pl.BlockSpec((1, DIFF_WIDTH, ts), lambda i, j: (i, 0, j))
    key_tile_t = pl.BlockSpec((1, ts // sub, DIFF_WIDTH, sub), lambda i, j: (i, j, 0, 0))
    full = lambda arr: pl.BlockSpec(arr.shape, lambda i, j: (0,) * arr.ndim)
    rows = jax.ShapeDtypeStruct((b, s, DIFF_WIDTH), BF16)
    rows_t = jax.ShapeDtypeStruct((b, DIFF_WIDTH, s), BF16)
    key_tiles_t = jax.ShapeDtypeStruct((b, s // sub, DIFF_WIDTH, sub), BF16)
    return pl.pallas_call(
        _inproj_kernel,
        grid=(b, s // ts),
        in_specs=[pl.BlockSpec((1, 1, ts), lambda i, j: (i, 0, j)), tile(d), full(g), full(w_main),
                  full(w_fz_t), full(b_fz_t)],
        out_specs=[tile_t, tile(DIFF_WIDTH), key_tile_t, tile_t, tile(FOX_WIDTH), key_tile_t,
                   tile(FOX_WIDTH), pl.BlockSpec((1, N_FOX_HEADS, ts), lambda i, j: (i, 0, j))],
        out_shape=[rows_t, rows, key_tiles_t, rows_t, rows, key_tiles_t,
                   jax.ShapeDtypeStruct((b, s, FOX_WIDTH), F32),
                   jax.ShapeDtypeStruct((b, N_FOX_HEADS, s), F32)],
        scratch_shapes=[pltpu.VMEM((w_fz_t.shape[0], 1), F32)],
        compiler_params=_params("arbitrary", "arbitrary"),
        name="inproj",
    )(positions.reshape(b, 1, s), x, g, w_main, w_fz_t, b_fz_t)


ONES_ROWS = 16


def _block_diag_queries(qt):
    row = lax.broadcasted_iota(I32, qt.shape, 0)
    zero = jnp.zeros_like(qt)
    return jnp.concatenate([jnp.where(row < HEAD_DIM, qt, zero),
                            jnp.where(row >= HEAD_DIM, qt, zero)], axis=1)


class _Chain:
    def __init__(self, qbd, keys, values_t, scratch, key_bias=None, query_bias=None):
        self.qbd, self.keys, self.values_t = qbd, keys, values_t
        self.acc_ref, self.s_ref, self.p_ref, self.stat_ref = scratch
        self.key_bias, self.query_bias = key_bias, query_bias


_RUNNING_MAX, _TILE_MAX, _RESCALE = 0, 1, 2


def _causal_attention(chains, qi):
    tk, tq = chains[0].s_ref.shape[0], chains[0].s_ref.shape[1] // 2
    ones = jnp.ones((ONES_ROWS, tk), BF16)

    def scores(ch, i, diagonal):
        s = jnp.dot(ch.keys(i), ch.qbd, preferred_element_type=F32)
        if ch.key_bias is not None:
            s = s - ch.key_bias(i)
        if diagonal is not False:
            r = lax.broadcasted_iota(I32, s.shape, 0)
            c = lax.broadcasted_iota(I32, s.shape, 1) & (tq - 1)
            visible = r <= c if diagonal is True else jnp.logical_or(r <= c, jnp.logical_not(diagonal))
            s = jnp.where(visible, s, -jnp.inf)
        ch.s_ref[...] = s
        top = jnp.max(s, axis=0, keepdims=True)
        if ch.query_bias is not None:
            top = top + ch.query_bias
        ch.stat_ref[_TILE_MAX:_TILE_MAX + 1, :] = top

    def probs(ch):
        m = ch.stat_ref[_RUNNING_MAX:_RUNNING_MAX + 1, :]
        m_new = jnp.maximum(m, ch.stat_ref[_TILE_MAX:_TILE_MAX + 1, :])
        ch.stat_ref[_RUNNING_MAX:_RUNNING_MAX + 1, :] = m_new
        ch.stat_ref[_RESCALE:_RESCALE + 1, :] = jnp.exp2(m - m_new)
        ref = m_new if ch.query_bias is None else m_new - ch.query_bias
        ch.p_ref[...] = jnp.exp2(ch.s_ref[...] - ref).astype(BF16)

    def accumulate(ch, i):
        vts = jnp.concatenate([ch.values_t(i), ones], axis=0)
        ch.acc_ref[...] = (ch.stat_ref[_RESCALE:_RESCALE + 1, :] * ch.acc_ref[...]
                           + jnp.dot(vts, ch.p_ref[...], preferred_element_type=F32))

    def body(k, diagonal_next):
        for ch in chains:
            accumulate(ch, jnp.maximum(k - 1, 0))
        for ch in chains:
            probs(ch)
        for ch in chains:
            scores(ch, k + 1, diagonal_next)

    for ch in chains:
        ch.acc_ref[...] = jnp.zeros_like(ch.acc_ref)
        ch.p_ref[...] = jnp.zeros_like(ch.p_ref)
        ch.stat_ref[...] = jnp.concatenate(
            [jnp.full((1, 2 * tq), -jnp.inf, F32), jnp.ones((ch.stat_ref.shape[0] - 1, 2 * tq), F32)], axis=0)
        scores(ch, 0, qi == 0)

    def loop_body(k, carry):
        body(k, False)
        return carry

    lax.fori_loop(0, qi - 1, loop_body, 0)

    @pl.when(qi >= 1)
    def _():
        body(qi - 1, True)

    for ch in chains:
        accumulate(ch, jnp.maximum(qi - 1, 0))
    for ch in chains:
        probs(ch)
    for ch in chains:
        accumulate(ch, qi)


def _attention_kernel(lq1_ref, lk1_ref, lq2_ref, lk2_ref, g_ref,
                      dqt_ref, dk_ref, dvt_ref, fqt_ref, fk_ref, fvt_ref, cq_ref, ck_ref,
                      do_ref, fo_ref, *scratch, lambda_init):
    tq = dqt_ref.shape[2]
    tk = dvt_ref.shape[3]
    units = dqt_ref.shape[1] // LANES
    lane = lax.broadcasted_iota(I32, (1, LANES), 1)
    scratch = [scratch[i:i + 4] for i in range(0, len(scratch), 4)]

    def unit_chains(u):
        lanes = slice(u * LANES, (u + 1) * LANES)

        def tile(ref):
            return lambda i: ref[0, pl.ds(pl.multiple_of(i * tk, tk), tk), lanes]

        def tile_t(ref):
            return lambda i: ref[0, i, lanes, :]

        cq = jnp.concatenate([cq_ref[0, u, 0:1, :], cq_ref[0, u, 1:2, :]], axis=1)

        def key_bias(i):
            ck = tile(ck_ref)(i)
            other = pltpu.roll(ck, HEAD_DIM, 1)
            first = jnp.where(lane < HEAD_DIM, ck, other)
            second = jnp.where(lane >= HEAD_DIM, ck, other)
            return jnp.concatenate([first] * (tq // LANES) + [second] * (tq // LANES), axis=1)

        return [_Chain(_block_diag_queries(dqt_ref[0, lanes, :]), tile(dk_ref), tile_t(dvt_ref),
                       scratch[2 * u]),
                _Chain(_block_diag_queries(fqt_ref[0, lanes, :]), tile(fk_ref), tile_t(fvt_ref),
                       scratch[2 * u + 1], key_bias, cq)]

    chains = [ch for u in range(units) for ch in unit_chains(u)]
    _causal_attention(chains, pl.program_id(2))

    lam = (jnp.exp(jnp.sum(lq1_ref[...] * lk1_ref[...], axis=-1, keepdims=True))
           - jnp.exp(jnp.sum(lq2_ref[...] * lk2_ref[...], axis=-1, keepdims=True)) + lambda_init)
    dv, h = DIFF_V_DIM, HEAD_DIM
    for u in range(units):
        lanes = slice(u * LANES, (u + 1) * LANES)
        acc = chains[2 * u].acc_ref
        o = (acc[0:dv, 0:tq] / acc[dv:dv + 1, 0:tq]
             - lam * (acc[0:dv, tq:] / acc[dv:dv + 1, tq:]))
        ms = jnp.mean(o * o, axis=0, keepdims=True)
        do_ref[0, :, lanes] = (o * lax.rsqrt(ms + NORM_EPS) * g_ref[...]
                               * (1.0 - lambda_init)).astype(BF16).T
        acc = chains[2 * u + 1].acc_ref
        fo_ref[0, :, lanes] = jnp.concatenate([acc[0:h, 0:tq] / acc[2 * h:2 * h + 1, 0:tq],
                                               acc[h:2 * h, tq:] / acc[2 * h:2 * h + 1, tq:]],
                                              axis=0).astype(BF16).T


def _attention(dqt, dk, dvt, fqt, fk, fvt, c_rows, c_cols, lq1, lk1, lq2, lk2, g_col, lambda_init):
    b, s, _ = dk.shape
    t = ATTN_TILE
    assert N_DIFF_HEADS == N_FOX_HEADS // 2 and N_DIFF_HEADS % ATTN_UNITS == 0
    width = ATTN_UNITS * LANES
    vec = lambda arr: pl.BlockSpec(arr.shape, lambda i, h, j: (0, 0))
    q_spec = pl.BlockSpec((1, width, t), lambda i, h, j: (i, h, j))
    k_spec = pl.BlockSpec((1, s, width), lambda i, h, j: (i, 0, h))
    v_spec = pl.BlockSpec((1, s // t, width, t), lambda i, h, j: (i, 0, h, 0))
    o_spec = pl.BlockSpec((1, t, width), lambda i, h, j: (i, j, h))
    out = jax.ShapeDtypeStruct((b, s, DIFF_WIDTH), BF16)
    chain_scratch = [pltpu.VMEM((LANES + ONES_ROWS, 2 * t), F32),
                     pltpu.VMEM((t, 2 * t), F32),
                     pltpu.VMEM((t, 2 * t), BF16),
                     pltpu.VMEM((8, 2 * t), F32)]
    return pl.pallas_call(
        functools.partial(_attention_kernel, lambda_init=lambda_init),
        grid=(b, N_DIFF_HEADS // ATTN_UNITS, s // t),
        in_specs=[vec(lq1), vec(lk1), vec(lq2), vec(lk2), vec(g_col),
                  q_spec, k_spec, v_spec, q_spec, k_spec, v_spec,
                  pl.BlockSpec((1, ATTN_UNITS, 2, t), lambda i, h, j: (i, h, 0, j)), k_spec],
        out_specs=[o_spec, o_spec],
        out_shape=[out, out],
        scratch_shapes=chain_scratch * (2 * ATTN_UNITS),
        compiler_params=_params("arbitrary", "arbitrary", "arbitrary"),
        name="attention",
    )(lq1, lk1, lq2, lk2, g_col, dqt, dk, dvt, fqt, fk, fvt, c_rows, c_cols)


def _router_kernel(d_ref, f_ref, x_ref, wo_ref, g_ref, wrt_ref, brt_ref,
                   h_ref, xn_ref, gate_ref, idx_ref, rank_ref, count_ref, carry_ref):
    tm = ROUTER_SUBTILE
    nt = (((1,), (1,)), ((), ()))

    def pieces(v, count):
        out = []
        for _ in range(count):
            part = v.astype(BF16)
            out.append(part)
            v = v - part.astype(F32)
        return out

    wh, wl = pieces(wrt_ref[...], 2)
    row = lax.broadcasted_iota(I32, (tm, tm), 0)
    col = lax.broadcasted_iota(I32, (tm, tm), 1)
    earlier = jnp.where(row < col, 1.0, 0.0).astype(BF16)
    eye = jnp.where(row == col, 1.0, 0.0).astype(BF16)

    @pl.when(pl.program_id(0) == 0)
    def _():
        carry_ref[...] = jnp.zeros_like(carry_ref)

    subs = [slice(i * tm, (i + 1) * tm) for i in range(x_ref.shape[0] // tm)]

    def logits(rows):
        mixed = (jnp.dot(d_ref[rows, :], wo_ref[0:DIFF_WIDTH, :], preferred_element_type=F32)
                 + jnp.dot(f_ref[rows, :], wo_ref[DIFF_WIDTH:, :], preferred_element_type=F32))
        h = x_ref[rows, :] + mixed
        h_ref[rows, :] = h
        xn = _rms(h, g_ref[...])
        xn_ref[rows, :] = _pack_rows(xn)
        xh, xl = pieces(xn, 2)
        return (lax.dot_general(wh, xh, nt, preferred_element_type=F32)
                + lax.dot_general(wh, xl, nt, preferred_element_type=F32)
                + lax.dot_general(wl, xh, nt, preferred_element_type=F32)) + brt_ref[...]

    def top_k(work):
        expert = lax.broadcasted_iota(I32, work.shape, 0)
        vals, ids, picks = [], [], []
        for _ in range(TOP_K):
            m = jnp.max(work, axis=0, keepdims=True)
            idx = jnp.min(jnp.where(work == m, expert, N_EXPERTS), axis=0, keepdims=True)
            pick = expert == idx
            vals.append(m)
            ids.append(idx)
            picks.append(pick)
            work = jnp.where(pick, -jnp.inf, work)
        exps = [jnp.exp(v - vals[0]) for v in vals]
        denom = exps[0] + exps[1] + exps[2] + exps[3]
        return ids, picks, [e / denom for e in exps]

    def ranks(picks):
        chosen = jnp.where(picks[0] | picks[1] | picks[2] | picks[3], 1.0, 0.0)
        rank = jnp.dot(chosen.astype(BF16), earlier, preferred_element_type=F32) + carry_ref[...]
        carry_ref[...] = carry_ref[...] + jnp.sum(chosen, axis=1, keepdims=True)
        return rank

    def emit(rows, ids, picks, gates, rank):
        slot = lax.broadcasted_iota(I32, (8, tm), 0)
        gate_t = jnp.zeros((8, tm), F32)
        idx_t = jnp.zeros((8, tm), I32)
        rank_t = jnp.zeros((8, tm), F32)
        for k in range(TOP_K):
            gate_t = jnp.where(slot == k, gates[k], gate_t)
            idx_t = jnp.where(slot == k, ids[k], idx_t)
            rank_t = jnp.where(slot == k, jnp.sum(jnp.where(picks[k], rank, 0.0), axis=0, keepdims=True),
                               rank_t)
        idx_ref[:, rows] = idx_t
        rank_ref[:, rows] = rank_t.astype(I32)
        gate_rows = jnp.concatenate([gate_t, jnp.zeros((LANES - 8, tm), F32)], axis=0)
        gate_ref[rows, :] = sum(lax.dot_general(eye, part, nt, preferred_element_type=F32)
                                for part in pieces(gate_rows, 3))

    work = [logits(rows) for rows in subs]
    chosen = [top_k(w) for w in work]
    rank = [ranks(picks) for _, picks, _ in chosen]
    for rows, (ids, picks, gates), r in zip(subs, chosen, rank):
        emit(rows, ids, picks, gates, r)
    count_ref[...] = jnp.broadcast_to(carry_ref[...], count_ref.shape)


def _router(d_out, f_out, x, w_out, g_moe, w_router_t, b_router_t):
    n, d = x.shape
    tm = ROUTER_TILE
    tile = lambda width: pl.BlockSpec((tm, width), lambda i: (i, 0))
    rows = pl.BlockSpec((8, tm), lambda i: (0, i))
    full = lambda arr: pl.BlockSpec(arr.shape, lambda i: (0,) * arr.ndim)
    return pl.pallas_call(
        _router_kernel,
        grid=(n // tm,),
        in_specs=[tile(DIFF_WIDTH), tile(FOX_WIDTH), tile(d), full(w_out), full(g_moe),
                  full(w_router_t), full(b_router_t)],
        out_specs=[tile(d), tile(d // 2), tile(LANES), rows, rows,
                   pl.BlockSpec((N_EXPERTS, LANES), lambda i: (0, 0))],
        out_shape=[jax.ShapeDtypeStruct((n, d), F32), jax.ShapeDtypeStruct((n, d // 2), U32),
                   jax.ShapeDtypeStruct((n, LANES), F32), jax.ShapeDtypeStruct((8, n), I32),
                   jax.ShapeDtypeStruct((8, n), I32), jax.ShapeDtypeStruct((N_EXPERTS, LANES), F32)],
        scratch_shapes=[pltpu.VMEM((N_EXPERTS, 1), F32)],
        compiler_params=_params("arbitrary"),
        name="router",
    )(d_out, f_out, x, w_out, g_moe, w_router_t, b_router_t)


def _slot_kernel(starts_ref, idx_ref, rank_ref, pos_ref):
    idx = idx_ref[...]
    pos = rank_ref[...]
    for e in range(N_EXPERTS):
        pos = jnp.where(idx == e, pos + starts_ref[e], pos)
    pos_ref[...] = pos


def _slots(starts, idx, rank):
    rows, n = idx.shape
    width = min(n, 2048)
    block = pl.BlockSpec((rows, width), lambda i, starts: (0, i))
    return pl.pallas_call(
        _slot_kernel,
        grid_spec=pltpu.PrefetchScalarGridSpec(num_scalar_prefetch=1, grid=(n // width,),
                                               in_specs=[block, block], out_specs=block),
        out_shape=jax.ShapeDtypeStruct((rows, n), I32),
        compiler_params=_params("arbitrary"),
        name="slots",
    )(starts, idx, rank)


SC_WINDOW = 128
SC_COLUMNS = 256


def _dispatch_sparsecore(pos, xn, n_rows):
    n, d = xn.shape
    windows = n // SC_WINDOW
    mesh = plsc.VectorSubcoreMesh(core_axis_name="core", subcore_axis_name="subcore")

    @functools.partial(pl.kernel, out_type=jax.ShapeDtypeStruct((n_rows, d), xn.dtype), mesh=mesh,
                       scratch_types=[], name="dispatch_sc")
    def scatter(x_hbm, pos_hbm, xs_hbm):
        for c in range(d // SC_COLUMNS):
            def body(x_vmem, pos_vmem, c=c):
                pltpu.sync_copy(x_vmem, xs_hbm.at[:, pl.ds(c * SC_COLUMNS, SC_COLUMNS)].at[pos_vmem.at[0]])

            pltpu.emit_pipeline(
                body, grid=(TOP_K * windows,),
                in_specs=[pl.BlockSpec((SC_WINDOW, SC_COLUMNS), index_map=lambda i, c=c: (i % windows, c)),
                          pl.BlockSpec((1, SC_WINDOW), index_map=lambda i: (0, i))],
                out_specs=[], core_axis_name=("core", "subcore"),
                dimension_semantics=(pltpu.PARALLEL,))(x_hbm, pos_hbm)

    return scatter(xn, pos.reshape(1, -1))


def _expert_kernel(tile_expert_ref, n_used_ref, next_expert_ref, slot_ref,
                   x_ref, w1_ref, b1_ref, w2_ref, b2_ref, y_ref,
                   w1f_ref, w2f_ref, sem_ref, w1b_ref, w2p_ref, w2b_ref):
    t = pl.program_id(0)
    fresh = (t == 0) | (tile_expert_ref[t] != tile_expert_ref[jnp.maximum(t - 1, 0)])
    used = t < n_used_ref[0]

    def weight_copies(e, s):
        return (pltpu.make_async_copy(w1_ref.at[e], w1f_ref.at[s], sem_ref.at[0, s]),
                pltpu.make_async_copy(w2_ref.at[e], w2f_ref.at[s], sem_ref.at[1, s]))

    @pl.when(t == 0)
    def _():
        for copy in weight_copies(tile_expert_ref[0], slot_ref[0]):
            copy.start()

    @pl.when(fresh & used)
    def _():
        s = slot_ref[t]
        for copy in weight_copies(tile_expert_ref[t], s):
            copy.wait()

        @pl.when(next_expert_ref[t] >= 0)
        def _():
            for copy in weight_copies(next_expert_ref[t], 1 - s):
                copy.start()

        rows = LANES
        half = LANES // 2
        for r in range(0, w1b_ref.shape[0], rows):
            w1b_ref[r:r + rows, :] = w1f_ref[s, r:r + rows, :].astype(BF16)
        for c in range(w2p_ref.shape[0]):
            lanes = slice(c * LANES, (c + 1) * LANES)
            for r in range(0, w2b_ref.shape[0], rows):
                for b in range(2):
                    w2p_ref[c, pl.ds(r + b, half, stride=2), :] = (
                        w2f_ref[s, r + b * half:r + (b + 1) * half, lanes])
            w2b_ref[:, lanes] = w2p_ref[c].astype(BF16)

    @pl.when(used)
    def _():
        xb = _unpack_rows(x_ref[...]).astype(BF16)
        h = jnp.dot(xb, w1b_ref[...], preferred_element_type=F32) + b1_ref[0]
        even = lax.broadcasted_iota(I32, (1, LANES), 1) % 2 == 0
        parts = []
        for q in range(h.shape[1] // (2 * LANES)):
            a = h[:, (2 * q) * LANES:(2 * q + 1) * LANES]
            b = h[:, (2 * q + 1) * LANES:(2 * q + 2) * LANES]
            hg = jnp.where(even, a, pltpu.roll(b, 1, 1))
            hl = jnp.where(even, pltpu.roll(a, LANES - 1, 1), b)
            glu = jnp.minimum(hg, SWIGLU_LIMIT)
            lin = jnp.clip(hl, -SWIGLU_LIMIT, SWIGLU_LIMIT)
            parts.append((glu * jax.nn.sigmoid(SWIGLU_ALPHA * glu) * (lin + 1.0)).astype(BF16))
        act = jnp.concatenate(parts, axis=1)
        y_ref[...] = _pack_rows(jnp.dot(act, w2b_ref[...], preferred_element_type=F32) + b2_ref[0])

    @pl.when(jnp.logical_not(used))
    def _():
        y_ref[...] = jnp.zeros_like(y_ref)


def _experts(tile_expert, n_used, next_expert, slot, xs, w1, b1, w2, b2):
    n_rows, d = xs.shape
    tm = EXPERT_TILE
    by_expert = lambda arr: pl.BlockSpec((1,) + arr.shape[1:], lambda i, te, *_: (te[i], 0, 0))
    rows = pl.BlockSpec((tm, d), lambda i, *_: (i, 0))
    used_rows = pl.BlockSpec((tm, d), lambda i, te, nu, *_: (jnp.minimum(i, nu[0] - 1), 0))
    hbm = pl.BlockSpec(memory_space=pl.ANY)
    return pl.pallas_call(
        _expert_kernel,
        grid_spec=pltpu.PrefetchScalarGridSpec(
            num_scalar_prefetch=4,
            grid=(n_rows // tm,),
            in_specs=[used_rows, hbm, by_expert(b1), hbm, by_expert(b2)],
            out_specs=rows,
            scratch_shapes=[pltpu.VMEM((2,) + w1.shape[1:], F32), pltpu.VMEM((2,) + w2.shape[1:], F32),
                            pltpu.SemaphoreType.DMA((2, 2)),
                            pltpu.VMEM(w1.shape[1:], BF16),
                            pltpu.VMEM((w2.shape[2] // LANES, w2.shape[1], LANES), F32),
                            pltpu.VMEM(w2.shape[1:], BF16)]),
        out_shape=jax.ShapeDtypeStruct(xs.shape, xs.dtype),
        compiler_params=pltpu.CompilerParams(dimension_semantics=("arbitrary",),
                                             vmem_limit_bytes=EXPERT_VMEM_LIMIT),
        name="experts",
    )(tile_expert, n_used, next_expert, slot, xs, w1, b1, w2, b2)


def _gather_sparsecore(pos, y):
    d = y.shape[1]
    total = pos.size
    mesh = plsc.VectorSubcoreMesh(core_axis_name="core", subcore_axis_name="subcore")

    @functools.partial(pl.kernel, out_type=jax.ShapeDtypeStruct((total, d), y.dtype), mesh=mesh,
                       scratch_types=[], name="gather_sc")
    def gather(y_hbm, pos_hbm, out_hbm):
        for c in range(d // SC_COLUMNS):
            def body(pos_vmem, out_vmem, c=c):
                pltpu.sync_copy(y_hbm.at[:, pl.ds(c * SC_COLUMNS, SC_COLUMNS)].at[pos_vmem.at[0]], out_vmem)

            pltpu.emit_pipeline(
                body, grid=(total // SC_WINDOW,),
                in_specs=[pl.BlockSpec((1, SC_WINDOW), index_map=lambda i: (0, i))],
                out_specs=[pl.BlockSpec((SC_WINDOW, SC_COLUMNS), index_map=lambda i, c=c: (i, c))],
                core_axis_name=("core", "subcore"),
                dimension_semantics=(pltpu.PARALLEL,))(pos_hbm, out_hbm)

    return gather(y, pos.reshape(1, -1))


def _combine_kernel(h_ref, gate_ref, p_ref, wg_ref, wp_ref, gple_ref, gfin_ref, *refs, final_norm):
    y_refs, o_ref = refs[:TOP_K], refs[TOP_K]
    ts = COMBINE_SUBTILE
    subs = [slice(i * ts, (i + 1) * ts) for i in range(h_ref.shape[0] // ts)]

    def mix(rows):
        gates = gate_ref[rows, :]
        moe = gates[:, 0:1] * _unpack_rows(y_refs[0][rows, :])
        for k in range(1, TOP_K):
            moe = moe + gates[:, k:k + 1] * _unpack_rows(y_refs[k][rows, :])
        h = h_ref[rows, :] + moe
        logit = jnp.dot(_rms(h, gple_ref[...]).astype(BF16), wg_ref[...], preferred_element_type=F32)
        emb = jnp.dot(p_ref[rows, :].astype(BF16), wp_ref[...], preferred_element_type=F32)
        return h, logit, emb

    mixed = [mix(rows) for rows in subs]
    for rows, (h, logit, emb) in zip(subs, mixed):
        h = h + jax.nn.sigmoid(logit) * emb
        o_ref[rows, :] = _rms(h, gfin_ref[...]) if final_norm else h


def _combine(h, gates, p, w_gate, w_proj, g_ple, g_final, y_tok, final_norm):
    n, d = h.shape
    tm = TOKEN_TILE
    tile = lambda width: pl.BlockSpec((tm, width), lambda i: (i, 0))
    full = lambda arr: pl.BlockSpec(arr.shape, lambda i: (0,) * arr.ndim)
    choice = lambda k: pl.BlockSpec((tm, y_tok.shape[1]), lambda i: (k * (n // tm) + i, 0))
    return pl.pallas_call(
        functools.partial(_combine_kernel, final_norm=final_norm),
        grid=(n // tm,),
        in_specs=[tile(d), tile(LANES), tile(p.shape[1]), full(w_gate), full(w_proj),
                  full(g_ple), full(g_final)] + [choice(k) for k in range(TOP_K)],
        out_specs=tile(d),
        out_shape=jax.ShapeDtypeStruct((n, d), F32),
        compiler_params=_params("arbitrary"),
        name="combine",
    )(h, gates, p, w_gate, w_proj, g_ple, g_final, *([y_tok] * TOP_K))


def _layer(h, p, positions, g_attn, w_in, lq1, lk1, lq2, lk2, g_subln, b_forget, w_out, g_moe,
           w_router, b_router, w_e1, b_e1, w_e2, b_e2, g_ple, w_ple_gate, w_ple_proj, g_final,
           lambda_init, final_norm):
    b, s, d = h.shape
    n = b * s
    row = lambda v: v.reshape(1, -1)

    w_main = w_in[:, :MAIN_WIDTH].astype(BF16)
    pad_heads = ((0, 2 * N_FOX_HEADS - N_FOX_HEADS), (0, 0))
    w_fz_t = jnp.pad(w_in[:, MAIN_WIDTH:].T, pad_heads).astype(BF16)
    b_fz_t = jnp.pad(b_forget.reshape(-1, 1), pad_heads)
    b1 = b_e1[:, None, :]
    b2 = b_e2[:, None, :]

    dq_t, dk, dv_t, fq_t, fk, fv_t, c, c_t = _inproj(positions, h, row(g_attn), w_main, w_fz_t, b_fz_t)
    c_rows = c_t.reshape(b, N_FOX_HEADS // 2, 2, s)
    d_out, f_out = _attention(dq_t, dk, dv_t, fq_t, fk, fv_t, c_rows, c, row(lq1), row(lk1), row(lq2),
                              row(lk2), g_subln.reshape(-1, 1), lambda_init)

    h1, xn, gates, idx, rank, counts = _router(
        d_out.reshape(n, DIFF_WIDTH), f_out.reshape(n, FOX_WIDTH), h.reshape(n, d),
        w_out.astype(BF16), row(g_moe), w_router.T, b_router.reshape(-1, 1))

    te = EXPERT_TILE
    n_rows = n * TOP_K + N_EXPERTS * te
    counts = counts[:, 0].astype(I32)
    padded = (counts + te - 1) // te * te
    ends = jnp.cumsum(padded)
    starts = ends - padded
    pos = _slots(starts.astype(I32), idx, rank)[:TOP_K]
    tile_start = jnp.arange(n_rows // te, dtype=I32) * te
    tile_expert = jnp.minimum(jnp.sum((ends[None, :] <= tile_start[:, None]).astype(I32), axis=1),
                              N_EXPERTS - 1)
    n_used = (ends[-1:] // te).astype(I32)
    owns = padded > 0
    later = lax.cummin(jnp.where(owns, jnp.arange(N_EXPERTS, dtype=I32), N_EXPERTS), reverse=True)
    following = jnp.concatenate([later[1:], jnp.full((1,), N_EXPERTS, I32)])
    following = jnp.where(following < N_EXPERTS, following, -1)
    next_expert = following[tile_expert]
    slot = ((jnp.cumsum(owns.astype(I32)) - 1) % 2)[tile_expert]

    xs = _dispatch_sparsecore(pos.astype(I32), xn, n_rows)
    y = _experts(tile_expert, n_used, next_expert.astype(I32), slot.astype(I32), xs, w_e1, b1, w_e2, b2)
    y_tok = _gather_sparsecore(pos.astype(I32), y)
    out = _combine(h1, gates, p.reshape(n, PLE_DIM), w_ple_gate.astype(BF16),
                   w_ple_proj.astype(BF16), row(g_ple), row(g_final), y_tok, final_norm)
    return out.reshape(b, s, d)


def kernel(x, p, positions, g_attn, w_in, lambda_q1, lambda_k1, lambda_q2, lambda_k2, g_subln, b_forget, w_out, g_moe, w_router, b_router, w_e1, b_e1, w_e2, b_e2, g_ple, w_ple_gate, w_ple_proj, g_final):
    depth = g_attn.shape[0]
    h = x
    for i in range(depth):
        lambda_init = 0.8 - 0.6 * math.exp(-0.3 * i)
        h = _layer(h, p[i], positions, g_attn[i], w_in[i], lambda_q1[i], lambda_k1[i],
                   lambda_q2[i], lambda_k2[i], g_subln[i], b_forget[i], w_out[i], g_moe[i],
                   w_router[i], b_router[i], w_e1[i], b_e1[i], w_e2[i], b_e2[i], g_ple[i],
                   w_ple_gate[i], w_ple_proj[i], g_final, lambda_init, i == depth - 1)
    return h
```

```python
import functools
import math

import jax
import jax.numpy as jnp
from jax import lax
from jax.experimental import pallas as pl
from jax.experimental.pallas import tpu as pltpu
from jax.experimental.pallas import tpu_sc as plsc

F32 = jnp.float32
BF16 = jnp.bfloat16
I32 = jnp.int32
U32 = jnp.uint32

LANES = 128
D_MODEL = 1024
HEAD_DIM = 64
N_DIFF_HEADS = 4
DIFF_V_DIM = 2 * HEAD_DIM
N_FOX_HEADS = 8
DIFF_WIDTH = N_DIFF_HEADS * DIFF_V_DIM
FOX_WIDTH = N_FOX_HEADS * HEAD_DIM
ROT_DIM = HEAD_DIM // 4
ROPE_THETA = 500000.0
N_EXPERTS = 32
TOP_K = 4
D_EXPERT = D_MODEL
SWIGLU_ALPHA = 1.702
SWIGLU_LIMIT = 7.0
PLE_DIM = 256
NORM_EPS = 1e-5
LOG2_E = math.log2(math.e)
MAIN_WIDTH = 3 * DIFF_WIDTH + 3 * FOX_WIDTH

SEQ_TILE = 1024
SEQ_SUBTILE = 256
ATTN_TILE = 256
ATTN_UNITS = 2
TOKEN_TILE = 1024
COMBINE_SUBTILE = 256
ROUTER_TILE = 1024
ROUTER_SUBTILE = 256
EXPERT_TILE = 256
VMEM_LIMIT = 48 * 1024 * 1024
EXPERT_VMEM_LIMIT = 58 * 1024 * 1024


def _params(*semantics):
    return pltpu.CompilerParams(dimension_semantics=semantics, vmem_limit_bytes=VMEM_LIMIT)


def _rms(x, g):
    return x * lax.rsqrt(jnp.mean(x * x, axis=-1, keepdims=True) + NORM_EPS) * g


def _pack_rows(v):
    half = v.shape[1] // 2
    bits = pltpu.bitcast(v.astype(BF16).astype(F32), U32)
    return (bits[:, :half] >> 16) | bits[:, half:]


def _unpack_rows(words):
    low = pltpu.bitcast(words << 16, F32)
    high = pltpu.bitcast(words & jnp.uint32(0xFFFF0000), F32)
    return jnp.concatenate([low, high], axis=1)


def _inproj_kernel(pos_ref, x_ref, g_ref, w_ref, wfzt_ref, bfzt_ref,
                   dqt_ref, dk_ref, dvt_ref, fqt_ref, fk_ref, fvt_ref, c_ref, ct_ref, carry_t_ref):
    ts = SEQ_SUBTILE
    subs = [slice(i * ts, (i + 1) * ts) for i in range(x_ref.shape[1] // ts)]
    xbs = [_rms(x_ref[0, rows, :], g_ref[...]).astype(BF16) for rows in subs]
    projs = [jnp.dot(xb, w_ref[...], preferred_element_type=F32) for xb in xbs]
    nt = (((1,), (1,)), ((), ()))
    tn = (((0,), (0,)), ((), ()))

    def pieces(v):
        hi = v.astype(BF16)
        r1 = v - hi.astype(F32)
        mid = r1.astype(BF16)
        return hi, mid, (r1 - mid.astype(F32)).astype(BF16)

    def rows_to_lanes(v, select):
        return sum(lax.dot_general(part, select, tn, preferred_element_type=F32) for part in pieces(v))

    half_rot = ROT_DIM // 2
    freq = lax.broadcasted_iota(I32, (half_rot, 1), 0)
    inv_freq = jnp.power(ROPE_THETA, -(2 * freq).astype(F32) / ROT_DIM)
    r = lax.broadcasted_iota(I32, (2 * half_rot, 2 * LANES), 0)
    lane2 = lax.broadcasted_iota(I32, (2 * half_rot, 2 * LANES), 1)
    d = lane2 % HEAD_DIM
    same_freq = jnp.where(d % half_rot == r % half_rot, 1.0, 0.0)
    cos_part = jnp.where((r < half_rot) & (lane2 < LANES) & (d < ROT_DIM), same_freq, 0.0)
    sin_part = jnp.where((r >= half_rot) & (lane2 >= LANES) & (d < ROT_DIM),
                         jnp.where(d < half_rot, -same_freq, same_freq), 0.0)
    expand = (cos_part + sin_part).astype(BF16)
    lane = lax.broadcasted_iota(I32, (1, LANES), 1) % HEAD_DIM
    first_half = lane < half_rot

    def rope_tables(rows):
        ang = inv_freq * pos_ref[0, :, rows].astype(F32)
        trig = jnp.concatenate([jnp.cos(ang), jnp.sin(ang)], axis=0)
        tables = rows_to_lanes(trig, expand)
        return jnp.where(lane < ROT_DIM, tables[:, :LANES], 1.0), tables[:, LANES:]

    def rope(t, cosf, sinf):
        outs = []
        for j in range(t.shape[1] // LANES):
            tj = t[:, j * LANES:(j + 1) * LANES]
            partner = jnp.where(first_half,
                                pltpu.roll(tj, LANES - ROT_DIM // 2, 1),
                                pltpu.roll(tj, ROT_DIM // 2, 1))
            outs.append(tj * cosf + partner * sinf)
        return jnp.concatenate(outs, axis=1)

    scale = HEAD_DIM ** -0.5 * LOG2_E
    w = DIFF_WIDTH
    for i, (rows, proj) in enumerate(zip(subs, projs)):
        cosf, sinf = rope_tables(rows)
        dqt_ref[0, :, rows] = (rope(proj[:, 0:w], cosf, sinf) * scale).astype(BF16).T
        dk_ref[0, rows, :] = rope(proj[:, w:2 * w], cosf, sinf).astype(BF16)
        dvt_ref[0, i] = proj[:, 2 * w:3 * w].astype(BF16).T
        fqt_ref[0, :, rows] = (proj[:, 3 * w:4 * w] * scale).astype(BF16).T
        fk_ref[0, rows, :] = proj[:, 4 * w:5 * w].astype(BF16)
        fvt_ref[0, i] = proj[:, 5 * w:6 * w].astype(BF16).T

    def log_sigmoid(z):
        return jnp.minimum(z, 0.0) - jnp.log1p(jnp.exp(-jnp.abs(z)))

    row = lax.broadcasted_iota(I32, (ts, ts), 0)
    col = lax.broadcasted_iota(I32, (ts, ts), 1)
    upto_col = jnp.where(row <= col, 1.0, 0.0).astype(BF16)
    heads = wfzt_ref.shape[0]
    head = lax.broadcasted_iota(I32, (heads, FOX_WIDTH), 0)
    group = lax.broadcasted_iota(I32, (heads, FOX_WIDTH), 1) // HEAD_DIM
    head_lanes = jnp.where(head == group, 1.0, 0.0).astype(BF16)

    @pl.when(pl.program_id(1) == 0)
    def _():
        carry_t_ref[...] = jnp.zeros_like(carry_t_ref)

    for rows, xb in zip(subs, xbs):
        fzt = lax.dot_general(wfzt_ref[...], xb, nt, preferred_element_type=F32) + bfzt_ref[...]
        ct = carry_t_ref[...] + sum(jnp.dot(part, upto_col, preferred_element_type=F32)
                                    for part in pieces(log_sigmoid(fzt)))
        carry_t_ref[...] = ct[:, ts - 1:ts]
        ct = ct * LOG2_E
        ct_ref[0, :, rows] = ct[0:N_FOX_HEADS, :]
        c_ref[0, rows, :] = rows_to_lanes(ct, head_lanes)


def _inproj(positions, x, g, w_main, w_fz_t, b_fz_t):
    b, s, d = x.shape
    ts = SEQ_TILE
    sub = SEQ_SUBTILE
    assert sub == ATTN_TILE
    tile = lambda width: pl.BlockSpec((1, ts, width), lambda i, j: (i, j, 0))
    tile_t = pl.BlockSpec((1, DIFF_WIDTH, ts), lambda i, j: (i, 0, j))
    key_tile_t = pl.BlockSpec((1, ts // sub, DIFF_WIDTH, sub), lambda i, j: (i, j, 0, 0))
    full = lambda arr: pl.BlockSpec(arr.shape, lambda i, j: (0,) * arr.ndim)
    rows = jax.ShapeDtypeStruct((b, s, DIFF_WIDTH), BF16)
    rows_t = jax.ShapeDtypeStruct((b, DIFF_WIDTH, s), BF16)
    key_tiles_t = jax.ShapeDtypeStruct((b, s // sub, DIFF_WIDTH, sub), BF16)
    return pl.pallas_call(
        _inproj_kernel,
        grid=(b, s // ts),
        in_specs=[pl.BlockSpec((1, 1, ts), lambda i, j: (i, 0, j)), tile(d), full(g), full(w_main),
                  full(w_fz_t), full(b_fz_t)],
        out_specs=[tile_t, tile(DIFF_WIDTH), key_tile_t, tile_t, tile(FOX_WIDTH), key_tile_t,
                   tile(FOX_WIDTH), pl.BlockSpec((1, N_FOX_HEADS, ts), lambda i, j: (i, 0, j))],
        out_shape=[rows_t, rows, key_tiles_t, rows_t, rows, key_tiles_t,
                   jax.ShapeDtypeStruct((b, s, FOX_WIDTH), F32),
                   jax.ShapeDtypeStruct((b, N_FOX_HEADS, s), F32)],
        scratch_shapes=[pltpu.VMEM((w_fz_t.shape[0], 1), F32)],
        compiler_params=_params("arbitrary", "arbitrary"),
        name="inproj",
    )(positions.reshape(b, 1, s), x, g, w_main, w_fz_t, b_fz_t)


ONES_ROWS = 16


def _block_diag_queries(qt):
    row = lax.broadcasted_iota(I32, qt.shape, 0)
    zero = jnp.zeros_like(qt)
    return jnp.concatenate([jnp.where(row < HEAD_DIM, qt, zero),
                            jnp.where(row >= HEAD_DIM, qt, zero)], axis=1)


class _Chain:
    def __init__(self, qbd, keys, values_t, scratch, key_bias=None, query_bias=None):
        self.qbd, self.keys, self.values_t = qbd, keys, values_t
        self.acc_ref, self.s_ref, self.p_ref, self.stat_ref = scratch
        self.key_bias, self.query_bias = key_bias, query_bias


_RUNNING_MAX, _TILE_MAX, _RESCALE = 0, 1, 2


def _causal_attention(chains, qi):
    tk, tq = chains[0].s_ref.shape[0], chains[0].s_ref.shape[1] // 2
    ones = jnp.ones((ONES_ROWS, tk), BF16)

    def scores(ch, i, diagonal):
        s = jnp.dot(ch.keys(i), ch.qbd, preferred_element_type=F32)
        if ch.key_bias is not None:
            s = s - ch.key_bias(i)
        if diagonal is not False:
            r = lax.broadcasted_iota(I32, s.shape, 0)
            c = lax.broadcasted_iota(I32, s.shape, 1) & (tq - 1)
            visible = r <= c if diagonal is True else jnp.logical_or(r <= c, jnp.logical_not(diagonal))
            s = jnp.where(visible, s, -jnp.inf)
        ch.s_ref[...] = s
        top = jnp.max(s, axis=0, keepdims=True)
        if ch.query_bias is not None:
            top = top + ch.query_bias
        ch.stat_ref[_TILE_MAX:_TILE_MAX + 1, :] = top

    def probs(ch):
        m = ch.stat_ref[_RUNNING_MAX:_RUNNING_MAX + 1, :]
        m_new = jnp.maximum(m, ch.stat_ref[_TILE_MAX:_TILE_MAX + 1, :])
        ch.stat_ref[_RUNNING_MAX:_RUNNING_MAX + 1, :] = m_new
        ch.stat_ref[_RESCALE:_RESCALE + 1, :] = jnp.exp2(m - m_new)
        ref = m_new if ch.query_bias is None else m_new - ch.query_bias
        ch.p_ref[...] = jnp.exp2(ch.s_ref[...] - ref).astype(BF16)

    def accumulate(ch, i):
        vts = jnp.concatenate([ch.values_t(i), ones], axis=0)
        ch.acc_ref[...] = (ch.stat_ref[_RESCALE:_RESCALE + 1, :] * ch.acc_ref[...]
                           + jnp.dot(vts, ch.p_ref[...], preferred_element_type=F32))

    def body(k, diagonal_next):
        for ch in chains:
            accumulate(ch, jnp.maximum(k - 1, 0))
        for ch in chains:
            probs(ch)
        for ch in chains:
            scores(ch, k + 1, diagonal_next)

    for ch in chains:
        ch.acc_ref[...] = jnp.zeros_like(ch.acc_ref)
        ch.p_ref[...] = jnp.zeros_like(ch.p_ref)
        ch.stat_ref[...] = jnp.concatenate(
            [jnp.full((1, 2 * tq), -jnp.inf, F32), jnp.ones((ch.stat_ref.shape[0] - 1, 2 * tq), F32)], axis=0)
        scores(ch, 0, qi == 0)

    def loop_body(k, carry):
        body(k, False)
        return carry

    lax.fori_loop(0, qi - 1, loop_body, 0)

    @pl.when(qi >= 1)
    def _():
        body(qi - 1, True)

    for ch in chains:
        accumulate(ch, jnp.maximum(qi - 1, 0))
    for ch in chains:
        probs(ch)
    for ch in chains:
        accumulate(ch, qi)


def _attention_kernel(lq1_ref, lk1_ref, lq2_ref, lk2_ref, g_ref,
                      dqt_ref, dk_ref, dvt_ref, fqt_ref, fk_ref, fvt_ref, cq_ref, ck_ref,
                      do_ref, fo_ref, *scratch, lambda_init):
    tq = dqt_ref.shape[2]
    tk = dvt_ref.shape[3]
    units = dqt_ref.shape[1] // LANES
    lane = lax.broadcasted_iota(I32, (1, LANES), 1)
    scratch = [scratch[i:i + 4] for i in range(0, len(scratch), 4)]

    def unit_chains(u):
        lanes = slice(u * LANES, (u + 1) * LANES)

        def tile(ref):
            return lambda i: ref[0, pl.ds(pl.multiple_of(i * tk, tk), tk), lanes]

        def tile_t(ref):
            return lambda i: ref[0, i, lanes, :]

        cq = jnp.concatenate([cq_ref[0, u, 0:1, :], cq_ref[0, u, 1:2, :]], axis=1)

        def key_bias(i):
            ck = tile(ck_ref)(i)
            other = pltpu.roll(ck, HEAD_DIM, 1)
            first = jnp.where(lane < HEAD_DIM, ck, other)
            second = jnp.where(lane >= HEAD_DIM, ck, other)
            return jnp.concatenate([first] * (tq // LANES) + [second] * (tq // LANES), axis=1)

        return [_Chain(_block_diag_queries(dqt_ref[0, lanes, :]), tile(dk_ref), tile_t(dvt_ref),
                       scratch[2 * u]),
                _Chain(_block_diag_queries(fqt_ref[0, lanes, :]), tile(fk_ref), tile_t(fvt_ref),
                       scratch[2 * u + 1], key_bias, cq)]

    chains = [ch for u in range(units) for ch in unit_chains(u)]
    _causal_attention(chains, pl.program_id(2))

    lam = (jnp.exp(jnp.sum(lq1_ref[...] * lk1_ref[...], axis=-1, keepdims=True))
           - jnp.exp(jnp.sum(lq2_ref[...] * lk2_ref[...], axis=-1, keepdims=True)) + lambda_init)
    dv, h = DIFF_V_DIM, HEAD_DIM
    for u in range(units):
        lanes = slice(u * LANES, (u + 1) * LANES)
        acc = chains[2 * u].acc_ref
        o = (acc[0:dv, 0:tq] / acc[dv:dv + 1, 0:tq]
             - lam * (acc[0:dv, tq:] / acc[dv:dv + 1, tq:]))
        ms = jnp.mean(o * o, axis=0, keepdims=True)
        do_ref[0, :, lanes] = (o * lax.rsqrt(ms + NORM_EPS) * g_ref[...]
                               * (1.0 - lambda_init)).astype(BF16).T
        acc = chains[2 * u + 1].acc_ref
        fo_ref[0, :, lanes] = jnp.concatenate([acc[0:h, 0:tq] / acc[2 * h:2 * h + 1, 0:tq],
                                               acc[h:2 * h, tq:] / acc[2 * h:2 * h + 1, tq:]],
                                              axis=0).astype(BF16).T


def _attention(dqt, dk, dvt, fqt, fk, fvt, c_rows, c_cols, lq1, lk1, lq2, lk2, g_col, lambda_init):
    b, s, _ = dk.shape
    t = ATTN_TILE
    assert N_DIFF_HEADS == N_FOX_HEADS // 2 and N_DIFF_HEADS % ATTN_UNITS == 0
    width = ATTN_UNITS * LANES
    vec = lambda arr: pl.BlockSpec(arr.shape, lambda i, h, j: (0, 0))
    q_spec = pl.BlockSpec((1, width, t), lambda i, h, j: (i, h, j))
    k_spec = pl.BlockSpec((1, s, width), lambda i, h, j: (i, 0, h))
    v_spec = pl.BlockSpec((1, s // t, width, t), lambda i, h, j: (i, 0, h, 0))
    o_spec = pl.BlockSpec((1, t, width), lambda i, h, j: (i, j, h))
    out = jax.ShapeDtypeStruct((b, s, DIFF_WIDTH), BF16)
    chain_scratch = [pltpu.VMEM((LANES + ONES_ROWS, 2 * t), F32),
                     pltpu.VMEM((t, 2 * t), F32),
                     pltpu.VMEM((t, 2 * t), BF16),
                     pltpu.VMEM((8, 2 * t), F32)]
    return pl.pallas_call(
        functools.partial(_attention_kernel, lambda_init=lambda_init),
        grid=(b, N_DIFF_HEADS // ATTN_UNITS, s // t),
        in_specs=[vec(lq1), vec(lk1), vec(lq2), vec(lk2), vec(g_col),
                  q_spec, k_spec, v_spec, q_spec, k_spec, v_spec,
                  pl.BlockSpec((1, ATTN_UNITS, 2, t), lambda i, h, j: (i, h, 0, j)), k_spec],
        out_specs=[o_spec, o_spec],
        out_shape=[out, out],
        scratch_shapes=chain_scratch * (2 * ATTN_UNITS),
        compiler_params=_params("arbitrary", "arbitrary", "arbitrary"),
        name="attention",
    )(lq1, lk1, lq2, lk2, g_col, dqt, dk, dvt, fqt, fk, fvt, c_rows, c_cols)


def _router_kernel(d_ref, f_ref, x_ref, wo_ref, g_ref, wrt_ref, brt_ref,
                   h_ref, xn_ref, gate_ref, idx_ref, rank_ref, count_ref, carry_ref):
    tm = ROUTER_SUBTILE
    nt = (((1,), (1,)), ((), ()))

    def pieces(v, count):
        out = []
        for _ in range(count):
            part = v.astype(BF16)
            out.append(part)
            v = v - part.astype(F32)
        return out

    wh, wl = pieces(wrt_ref[...], 2)
    row = lax.broadcasted_iota(I32, (tm, tm), 0)
    col = lax.broadcasted_iota(I32, (tm, tm), 1)
    earlier = jnp.where(row < col, 1.0, 0.0).astype(BF16)
    eye = jnp.where(row == col, 1.0, 0.0).astype(BF16)

    @pl.when(pl.program_id(0) == 0)
    def _():
        carry_ref[...] = jnp.zeros_like(carry_ref)

    subs = [slice(i * tm, (i + 1) * tm) for i in range(x_ref.shape[0] // tm)]

    def logits(rows):
        mixed = (jnp.dot(d_ref[rows, :], wo_ref[0:DIFF_WIDTH, :], preferred_element_type=F32)
                 + jnp.dot(f_ref[rows, :], wo_ref[DIFF_WIDTH:, :], preferred_element_type=F32))
        h = x_ref[rows, :] + mixed
        h_ref[rows, :] = h
        xn = _rms(h, g_ref[...])
        xn_ref[rows, :] = _pack_rows(xn)
        xh, xl = pieces(xn, 2)
        return (lax.dot_general(wh, xh, nt, preferred_element_type=F32)
                + lax.dot_general(wh, xl, nt, preferred_element_type=F32)
                + lax.dot_general(wl, xh, nt, preferred_element_type=F32)) + brt_ref[...]

    def top_k(work):
        expert = lax.broadcasted_iota(I32, work.shape, 0)
        vals, ids, picks = [], [], []
        for _ in range(TOP_K):
            m = jnp.max(work, axis=0, keepdims=True)
            idx = jnp.min(jnp.where(work == m, expert, N_EXPERTS), axis=0, keepdims=True)
            pick = expert == idx
            vals.append(m)
            ids.append(idx)
            picks.append(pick)
            work = jnp.where(pick, -jnp.inf, work)
        exps = [jnp.exp(v - vals[0]) for v in vals]
        denom = exps[0] + exps[1] + exps[2] + exps[3]
        return ids, picks, [e / denom for e in exps]

    def ranks(picks):
        chosen = jnp.where(picks[0] | picks[1] | picks[2] | picks[3], 1.0, 0.0)
        rank = jnp.dot(chosen.astype(BF16), earlier, preferred_element_type=F32) + carry_ref[...]
        carry_ref[...] = carry_ref[...] + jnp.sum(chosen, axis=1, keepdims=True)
        return rank

    def emit(rows, ids, picks, gates, rank):
        slot = lax.broadcasted_iota(I32, (8, tm), 0)
        gate_t = jnp.zeros((8, tm), F32)
        idx_t = jnp.zeros((8, tm), I32)
        rank_t = jnp.zeros((8, tm), F32)
        for k in range(TOP_K):
            gate_t = jnp.where(slot == k, gates[k], gate_t)
            idx_t = jnp.where(slot == k, ids[k], idx_t)
            rank_t = jnp.where(slot == k, jnp.sum(jnp.where(picks[k], rank, 0.0), axis=0, keepdims=True),
                               rank_t)
        idx_ref[:, rows] = idx_t
        rank_ref[:, rows] = rank_t.astype(I32)
        gate_rows = jnp.concatenate([gate_t, jnp.zeros((LANES - 8, tm), F32)], axis=0)
        gate_ref[rows, :] = sum(lax.dot_general(eye, part, nt, preferred_element_type=F32)
                                for part in pieces(gate_rows, 3))

    work = [logits(rows) for rows in subs]
    chosen = [top_k(w) for w in work]
    rank = [ranks(picks) for _, picks, _ in chosen]
    for rows, (ids, picks, gates), r in zip(subs, chosen, rank):
        emit(rows, ids, picks, gates, r)
    count_ref[...] = jnp.broadcast_to(carry_ref[...], count_ref.shape)


def _router(d_out, f_out, x, w_out, g_moe, w_router_t, b_router_t):
    n, d = x.shape
    tm = ROUTER_TILE
    tile = lambda width: pl.BlockSpec((tm, width), lambda i: (i, 0))
    rows = pl.BlockSpec((8, tm), lambda i: (0, i))
    full = lambda arr: pl.BlockSpec(arr.shape, lambda i: (0,) * arr.ndim)
    return pl.pallas_call(
        _router_kernel,
        grid=(n // tm,),
        in_specs=[tile(DIFF_WIDTH), tile(FOX_WIDTH), tile(d), full(w_out), full(g_moe),
                  full(w_router_t), full(b_router_t)],
        out_specs=[tile(d), tile(d // 2), tile(LANES), rows, rows,
                   pl.BlockSpec((N_EXPERTS, LANES), lambda i: (0, 0))],
        out_shape=[jax.ShapeDtypeStruct((n, d), F32), jax.ShapeDtypeStruct((n, d // 2), U32),
                   jax.ShapeDtypeStruct((n, LANES), F32), jax.ShapeDtypeStruct((8, n), I32),
                   jax.ShapeDtypeStruct((8, n), I32), jax.ShapeDtypeStruct((N_EXPERTS, LANES), F32)],
        scratch_shapes=[pltpu.VMEM((N_EXPERTS, 1), F32)],
        compiler_params=_params("arbitrary"),
        name="router",
    )(d_out, f_out, x, w_out, g_moe, w_router_t, b_router_t)


def _slot_kernel(starts_ref, idx_ref, rank_ref, pos_ref):
    idx = idx_ref[...]
    pos = rank_ref[...]
    for e in range(N_EXPERTS):
        pos = jnp.where(idx == e, pos + starts_ref[e], pos)
    pos_ref[...] = pos


def _slots(starts, idx, rank):
    rows, n = idx.shape
    width = min(n, 2048)
    block = pl.BlockSpec((rows, width), lambda i, starts: (0, i))
    return pl.pallas_call(
        _slot_kernel,
        grid_spec=pltpu.PrefetchScalarGridSpec(num_scalar_prefetch=1, grid=(n // width,),
                                               in_specs=[block, block], out_specs=block),
        out_shape=jax.ShapeDtypeStruct((rows, n), I32),
        compiler_params=_params("arbitrary"),
        name="slots",
    )(starts, idx, rank)


SC_WINDOW = 128
SC_COLUMNS = 256
SC_IN_FLIGHT = 3


def _dispatch_sparsecore(pos, xn, n_rows):
    n, d = xn.shape
    windows = n // SC_WINDOW
    mesh = plsc.VectorSubcoreMesh(core_axis_name="core", subcore_axis_name="subcore")

    @functools.partial(pl.kernel, out_type=jax.ShapeDtypeStruct((n_rows, d), xn.dtype), mesh=mesh,
                       scratch_types=[], name="dispatch_sc")
    def scatter(x_hbm, pos_hbm, xs_hbm):
        for c in range(d // SC_COLUMNS):
            def body(x_vmem, pos_vmem, c=c):
                pltpu.sync_copy(x_vmem, xs_hbm.at[:, pl.ds(c * SC_COLUMNS, SC_COLUMNS)].at[pos_vmem.at[0]])

            pltpu.emit_pipeline(
                body, grid=(TOP_K * windows,),
                in_specs=[pl.BlockSpec((SC_WINDOW, SC_COLUMNS), index_map=lambda i, c=c: (i % windows, c)),
                          pl.BlockSpec((1, SC_WINDOW), index_map=lambda i: (0, i))],
                out_specs=[], core_axis_name=("core", "subcore"),
                dimension_semantics=(pltpu.PARALLEL,))(x_hbm, pos_hbm)

    return scatter(xn, pos.reshape(1, -1))


def _expert_kernel(tile_expert_ref, n_used_ref, next_expert_ref, slot_ref,
                   x_ref, w1_ref, b1_ref, w2_ref, b2_ref, y_ref,
                   w1f_ref, w2f_ref, sem_ref, w1b_ref, w2p_ref, w2b_ref):
    t = pl.program_id(0)
    fresh = (t == 0) | (tile_expert_ref[t] != tile_expert_ref[jnp.maximum(t - 1, 0)])
    used = t < n_used_ref[0]

    def weight_copies(e, s):
        return (pltpu.make_async_copy(w1_ref.at[e], w1f_ref.at[s], sem_ref.at[0, s]),
                pltpu.make_async_copy(w2_ref.at[e], w2f_ref.at[s], sem_ref.at[1, s]))

    @pl.when(t == 0)
    def _():
        for copy in weight_copies(tile_expert_ref[0], slot_ref[0]):
            copy.start()

    @pl.when(fresh & used)
    def _():
        s = slot_ref[t]
        for copy in weight_copies(tile_expert_ref[t], s):
            copy.wait()

        @pl.when(next_expert_ref[t] >= 0)
        def _():
            for copy in weight_copies(next_expert_ref[t], 1 - s):
                copy.start()

        rows = LANES
        half = LANES // 2
        for r in range(0, w1b_ref.shape[0], rows):
            w1b_ref[r:r + rows, :] = w1f_ref[s, r:r + rows, :].astype(BF16)
        for c in range(w2p_ref.shape[0]):
            lanes = slice(c * LANES, (c + 1) * LANES)
            for r in range(0, w2b_ref.shape[0], rows):
                for b in range(2):
                    w2p_ref[c, pl.ds(r + b, half, stride=2), :] = (
                        w2f_ref[s, r + b * half:r + (b + 1) * half, lanes])
            w2b_ref[:, lanes] = w2p_ref[c].astype(BF16)

    @pl.when(used)
    def _():
        xb = _unpack_rows(x_ref[...]).astype(BF16)
        h = jnp.dot(xb, w1b_ref[...], preferred_element_type=F32) + b1_ref[0]
        even = lax.broadcasted_iota(I32, (1, LANES), 1) % 2 == 0
        parts = []
        for q in range(h.shape[1] // (2 * LANES)):
            a = h[:, (2 * q) * LANES:(2 * q + 1) * LANES]
            b = h[:, (2 * q + 1) * LANES:(2 * q + 2) * LANES]
            hg = jnp.where(even, a, pltpu.roll(b, 1, 1))
            hl = jnp.where(even, pltpu.roll(a, LANES - 1, 1), b)
            glu = jnp.minimum(hg, SWIGLU_LIMIT)
            lin = jnp.clip(hl, -SWIGLU_LIMIT, SWIGLU_LIMIT)
            parts.append((glu * jax.nn.sigmoid(SWIGLU_ALPHA * glu) * (lin + 1.0)).astype(BF16))
        act = jnp.concatenate(parts, axis=1)
        y_ref[...] = _pack_rows(jnp.dot(act, w2b_ref[...], preferred_element_type=F32) + b2_ref[0])

    @pl.when(jnp.logical_not(used))
    def _():
        y_ref[...] = jnp.zeros_like(y_ref)


def _experts(tile_expert, n_used, next_expert, slot, xs, w1, b1, w2, b2):
    n_rows, d = xs.shape
    tm = EXPERT_TILE
    by_expert = lambda arr: pl.BlockSpec((1,) + arr.shape[1:], lambda i, te, *_: (te[i], 0, 0))
    rows = pl.BlockSpec((tm, d), lambda i, *_: (i, 0))
    used_rows = pl.BlockSpec((tm, d), lambda i, te, nu, *_: (jnp.minimum(i, nu[0] - 1), 0))
    hbm = pl.BlockSpec(memory_space=pl.ANY)
    return pl.pallas_call(
        _expert_kernel,
        grid_spec=pltpu.PrefetchScalarGridSpec(
            num_scalar_prefetch=4,
            grid=(n_rows // tm,),
            in_specs=[used_rows, hbm, by_expert(b1), hbm, by_expert(b2)],
            out_specs=rows,
            scratch_shapes=[pltpu.VMEM((2,) + w1.shape[1:], F32), pltpu.VMEM((2,) + w2.shape[1:], F32),
                            pltpu.SemaphoreType.DMA((2, 2)),
                            pltpu.VMEM(w1.shape[1:], BF16),
                            pltpu.VMEM((w2.shape[2] // LANES, w2.shape[1], LANES), F32),
                            pltpu.VMEM(w2.shape[1:], BF16)]),
        out_shape=jax.ShapeDtypeStruct(xs.shape, xs.dtype),
        compiler_params=pltpu.CompilerParams(dimension_semantics=("arbitrary",),
                                             vmem_limit_bytes=EXPERT_VMEM_LIMIT),
        name="experts",
    )(tile_expert, n_used, next_expert, slot, xs, w1, b1, w2, b2)


def _gather_sparsecore(pos, y):
    d = y.shape[1]
    total = pos.size
    info = plsc.get_sparse_core_info()
    cores, workers = info.num_cores, info.num_cores * info.num_subcores
    per_worker = total // SC_WINDOW // workers
    assert per_worker * workers * SC_WINDOW == total
    mesh = plsc.VectorSubcoreMesh(core_axis_name="core", subcore_axis_name="subcore")

    @functools.partial(pl.kernel, out_type=jax.ShapeDtypeStruct((total, d), y.dtype), mesh=mesh,
                       scratch_types=[pltpu.VMEM((SC_IN_FLIGHT, SC_WINDOW), I32),
                                      pltpu.VMEM((SC_IN_FLIGHT, SC_WINDOW, SC_COLUMNS), y.dtype),
                                      pltpu.SemaphoreType.DMA((SC_IN_FLIGHT,))],
                       name="gather_sc")
    def gather(y_hbm, pos_hbm, out_hbm, idx_vmem, rows_vmem, sems):
        first = (lax.axis_index("subcore") * cores + lax.axis_index("core")) * per_worker
        items = [(j, c) for c in range(d // SC_COLUMNS) for j in range(per_worker)]
        started = []

        def finish(n):
            j, c, copy = started[n]
            copy.wait()
            pltpu.sync_copy(rows_vmem.at[n % SC_IN_FLIGHT],
                            out_hbm.at[pl.ds((first + j) * SC_WINDOW, SC_WINDOW),
                                       pl.ds(c * SC_COLUMNS, SC_COLUMNS)])

        for n, (j, c) in enumerate(items):
            slot = n % SC_IN_FLIGHT
            if n >= SC_IN_FLIGHT:
                finish(n - SC_IN_FLIGHT)
            pltpu.sync_copy(pos_hbm.at[0, pl.ds((first + j) * SC_WINDOW, SC_WINDOW)], idx_vmem.at[slot])
            copy = pltpu.async_copy(y_hbm.at[:, pl.ds(c * SC_COLUMNS, SC_COLUMNS)].at[idx_vmem.at[slot]],
                                    rows_vmem.at[slot], sems.at[slot])
            started.append((j, c, copy))
        for n in range(max(len(items) - SC_IN_FLIGHT, 0), len(items)):
            finish(n)

    return gather(y, pos.reshape(1, -1))


def _combine_kernel(h_ref, gate_ref, p_ref, wg_ref, wp_ref, gple_ref, gfin_ref, *refs, final_norm):
    y_refs, o_ref = refs[:TOP_K], refs[TOP_K]
    ts = COMBINE_SUBTILE
    subs = [slice(i * ts, (i + 1) * ts) for i in range(h_ref.shape[0] // ts)]

    def mix(rows):
        gates = gate_ref[rows, :]
        moe = gates[:, 0:1] * _unpack_rows(y_refs[0][rows, :])
        for k in range(1, TOP_K):
            moe = moe + gates[:, k:k + 1] * _unpack_rows(y_refs[k][rows, :])
        h = h_ref[rows, :] + moe
        logit = jnp.dot(_rms(h, gple_ref[...]).astype(BF16), wg_ref[...], preferred_element_type=F32)
        emb = jnp.dot(p_ref[rows, :].astype(BF16), wp_ref[...], preferred_element_type=F32)
        return h, logit, emb

    mixed = [mix(rows) for rows in subs]
    for rows, (h, logit, emb) in zip(subs, mixed):
        h = h + jax.nn.sigmoid(logit) * emb
        o_ref[rows, :] = _rms(h, gfin_ref[...]) if final_norm else h


def _combine(h, gates, p, w_gate, w_proj, g_ple, g_final, y_tok, final_norm):
    n, d = h.shape
    tm = TOKEN_TILE
    tile = lambda width: pl.BlockSpec((tm, width), lambda i: (i, 0))
    full = lambda arr: pl.BlockSpec(arr.shape, lambda i: (0,) * arr.ndim)
    choice = lambda k: pl.BlockSpec((tm, y_tok.shape[1]), lambda i: (k * (n // tm) + i, 0))
    return pl.pallas_call(
        functools.partial(_combine_kernel, final_norm=final_norm),
        grid=(n // tm,),
        in_specs=[tile(d), tile(LANES), tile(p.shape[1]), full(w_gate), full(w_proj),
                  full(g_ple), full(g_final)] + [choice(k) for k in range(TOP_K)],
        out_specs=tile(d),
        out_shape=jax.ShapeDtypeStruct((n, d), F32),
        compiler_params=_params("arbitrary"),
        name="combine",
    )(h, gates, p, w_gate, w_proj, g_ple, g_final, *([y_tok] * TOP_K))


def _layer(h, p, positions, g_attn, w_in, lq1, lk1, lq2, lk2, g_subln, b_forget, w_out, g_moe,
           w_router, b_router, w_e1, b_e1, w_e2, b_e2, g_ple, w_ple_gate, w_ple_proj, g_final,
           lambda_init, final_norm):
    b, s, d = h.shape
    n = b * s
    row = lambda v: v.reshape(1, -1)

    w_main = w_in[:, :MAIN_WIDTH].astype(BF16)
    pad_heads = ((0, 2 * N_FOX_HEADS - N_FOX_HEADS), (0, 0))
    w_fz_t = jnp.pad(w_in[:, MAIN_WIDTH:].T, pad_heads).astype(BF16)
    b_fz_t = jnp.pad(b_forget.reshape(-1, 1), pad_heads)
    b1 = b_e1[:, None, :]
    b2 = b_e2[:, None, :]

    dq_t, dk, dv_t, fq_t, fk, fv_t, c, c_t = _inproj(positions, h, row(g_attn), w_main, w_fz_t, b_fz_t)
    c_rows = c_t.reshape(b, N_FOX_HEADS // 2, 2, s)
    d_out, f_out = _attention(dq_t, dk, dv_t, fq_t, fk, fv_t, c_rows, c, row(lq1), row(lk1), row(lq2),
                              row(lk2), g_subln.reshape(-1, 1), lambda_init)

    h1, xn, gates, idx, rank, counts = _router(
        d_out.reshape(n, DIFF_WIDTH), f_out.reshape(n, FOX_WIDTH), h.reshape(n, d),
        w_out.astype(BF16), row(g_moe), w_router.T, b_router.reshape(-1, 1))

    te = EXPERT_TILE
    n_rows = n * TOP_K + N_EXPERTS * te
    counts = counts[:, 0].astype(I32)
    padded = (counts + te - 1) // te * te
    ends = jnp.cumsum(padded)
    starts = ends - padded
    pos = _slots(starts.astype(I32), idx, rank)[:TOP_K]
    tile_start = jnp.arange(n_rows // te, dtype=I32) * te
    tile_expert = jnp.minimum(jnp.sum((ends[None, :] <= tile_start[:, None]).astype(I32), axis=1),
                              N_EXPERTS - 1)
    n_used = (ends[-1:] // te).astype(I32)
    owns = padded > 0
    later = lax.cummin(jnp.where(owns, jnp.arange(N_EXPERTS, dtype=I32), N_EXPERTS), reverse=True)
    following = jnp.concatenate([later[1:], jnp.full((1,), N_EXPERTS, I32)])
    following = jnp.where(following < N_EXPERTS, following, -1)
    next_expert = following[tile_expert]
    slot = ((jnp.cumsum(owns.astype(I32)) - 1) % 2)[tile_expert]

    xs = _dispatch_sparsecore(pos.astype(I32), xn, n_rows)
    y = _experts(tile_expert, n_used, next_expert.astype(I32), slot.astype(I32), xs, w_e1, b1, w_e2, b2)
    y_tok = _gather_sparsecore(pos.astype(I32), y)
    out = _combine(h1, gates, p.reshape(n, PLE_DIM), w_ple_gate.astype(BF16),
                   w_ple_proj.astype(BF16), row(g_ple), row(g_final), y_tok, final_norm)
    return out.reshape(b, s, d)


def kernel(x, p, positions, g_attn, w_in, lambda_q1, lambda_k1, lambda_q2, lambda_k2, g_subln, b_forget, w_out, g_moe, w_router, b_router, w_e1, b_e1, w_e2, b_e2, g_ple, w_ple_gate, w_ple_proj, g_final):
    depth = g_attn.shape[0]
    h = x
    for i in range(depth):
        lambda_init = 0.8 - 0.6 * math.exp(-0.3 * i)
        h = _layer(h, p[i], positions, g_attn[i], w_in[i], lambda_q1[i], lambda_k1[i],
                   lambda_q2[i], lambda_k2[i], g_subln[i], b_forget[i], w_out[i], g_moe[i],
                   w_router[i], b_router[i], w_e1[i], b_e1[i], w_e2[i], b_e2[i], g_ple[i],
                   w_ple_gate[i], w_ple_proj[i], g_final, lambda_init, i == depth - 1)
    return h
```

```python
import functools
import math

import jax
import jax.numpy as jnp
from jax import lax
from jax.experimental import pallas as pl
from jax.experimental.pallas import tpu as pltpu
from jax.experimental.pallas import tpu_sc as plsc

F32 = jnp.float32
BF16 = jnp.bfloat16
I32 = jnp.int32
U32 = jnp.uint32

LANES = 128
D_MODEL = 1024
HEAD_DIM = 64
N_DIFF_HEADS = 4
DIFF_V_DIM = 2 * HEAD_DIM
N_FOX_HEADS = 8
DIFF_WIDTH = N_DIFF_HEADS * DIFF_V_DIM
FOX_WIDTH = N_FOX_HEADS * HEAD_DIM
ROT_DIM = HEAD_DIM // 4
ROPE_THETA = 500000.0
N_EXPERTS = 32
TOP_K = 4
D_EXPERT = D_MODEL
SWIGLU_ALPHA = 1.702
SWIGLU_LIMIT = 7.0
PLE_DIM = 256
NORM_EPS = 1e-5
LOG2_E = math.log2(math.e)
MAIN_WIDTH = 3 * DIFF_WIDTH + 3 * FOX_WIDTH

SEQ_TILE = 1024
SEQ_SUBTILE = 256
ATTN_TILE = 256
ATTN_UNITS = 2
TOKEN_TILE = 1024
COMBINE_SUBTILE = 256
ROUTER_TILE = 1024
ROUTER_SUBTILE = 256
EXPERT_TILE = 256
VMEM_LIMIT = 48 * 1024 * 1024
EXPERT_VMEM_LIMIT = 58 * 1024 * 1024


def _params(*semantics):
    return pltpu.CompilerParams(dimension_semantics=semantics, vmem_limit_bytes=VMEM_LIMIT)


def _rms(x, g):
    return x * lax.rsqrt(jnp.mean(x * x, axis=-1, keepdims=True) + NORM_EPS) * g


def _pack_rows(v):
    half = v.shape[1] // 2
    bits = pltpu.bitcast(v.astype(BF16).astype(F32), U32)
    return (bits[:, :half] >> 16) | bits[:, half:]


def _unpack_rows(words):
    low = pltpu.bitcast(words << 16, F32)
    high = pltpu.bitcast(words & jnp.uint32(0xFFFF0000), F32)
    return jnp.concatenate([low, high], axis=1)


def _inproj_kernel(pos_ref, x_ref, g_ref, w_ref, wfzt_ref, bfzt_ref,
                   dqt_ref, dk_ref, dvt_ref, fqt_ref, fk_ref, fvt_ref, c_ref, ct_ref, carry_t_ref):
    ts = SEQ_SUBTILE
    subs = [slice(i * ts, (i + 1) * ts) for i in range(x_ref.shape[1] // ts)]
    xbs = [_rms(x_ref[0, rows, :], g_ref[...]).astype(BF16) for rows in subs]
    projs = [jnp.dot(xb, w_ref[...], preferred_element_type=F32) for xb in xbs]
    nt = (((1,), (1,)), ((), ()))
    tn = (((0,), (0,)), ((), ()))

    def pieces(v):
        hi = v.astype(BF16)
        r1 = v - hi.astype(F32)
        mid = r1.astype(BF16)
        return hi, mid, (r1 - mid.astype(F32)).astype(BF16)

    def rows_to_lanes(v, select):
        return sum(lax.dot_general(part, select, tn, preferred_element_type=F32) for part in pieces(v))

    half_rot = ROT_DIM // 2
    freq = lax.broadcasted_iota(I32, (half_rot, 1), 0)
    inv_freq = jnp.power(ROPE_THETA, -(2 * freq).astype(F32) / ROT_DIM)
    r = lax.broadcasted_iota(I32, (2 * half_rot, 2 * LANES), 0)
    lane2 = lax.broadcasted_iota(I32, (2 * half_rot, 2 * LANES), 1)
    d = lane2 % HEAD_DIM
    same_freq = jnp.where(d % half_rot == r % half_rot, 1.0, 0.0)
    cos_part = jnp.where((r < half_rot) & (lane2 < LANES) & (d < ROT_DIM), same_freq, 0.0)
    sin_part = jnp.where((r >= half_rot) & (lane2 >= LANES) & (d < ROT_DIM),
                         jnp.where(d < half_rot, -same_freq, same_freq), 0.0)
    expand = (cos_part + sin_part).astype(BF16)
    lane = lax.broadcasted_iota(I32, (1, LANES), 1) % HEAD_DIM
    first_half = lane < half_rot

    def rope_tables(rows):
        ang = inv_freq * pos_ref[0, :, rows].astype(F32)
        trig = jnp.concatenate([jnp.cos(ang), jnp.sin(ang)], axis=0)
        tables = rows_to_lanes(trig, expand)
        return jnp.where(lane < ROT_DIM, tables[:, :LANES], 1.0), tables[:, LANES:]

    def rope(t, cosf, sinf):
        outs = []
        for j in range(t.shape[1] // LANES):
            tj = t[:, j * LANES:(j + 1) * LANES]
            partner = jnp.where(first_half,
                                pltpu.roll(tj, LANES - ROT_DIM // 2, 1),
                                pltpu.roll(tj, ROT_DIM // 2, 1))
            outs.append(tj * cosf + partner * sinf)
        return jnp.concatenate(outs, axis=1)

    scale = HEAD_DIM ** -0.5 * LOG2_E
    w = DIFF_WIDTH
    for i, (rows, proj) in enumerate(zip(subs, projs)):
        cosf, sinf = rope_tables(rows)
        dqt_ref[0, :, rows] = (rope(proj[:, 0:w], cosf, sinf) * scale).astype(BF16).T
        dk_ref[0, rows, :] = rope(proj[:, w:2 * w], cosf, sinf).astype(BF16)
        dvt_ref[0, i] = proj[:, 2 * w:3 * w].astype(BF16).T
        fqt_ref[0, :, rows] = (proj[:, 3 * w:4 * w] * scale).astype(BF16).T
        fk_ref[0, rows, :] = proj[:, 4 * w:5 * w].astype(BF16)
        fvt_ref[0, i] = proj[:, 5 * w:6 * w].astype(BF16).T

    def log_sigmoid(z):
        return jnp.minimum(z, 0.0) - jnp.log1p(jnp.exp(-jnp.abs(z)))

    row = lax.broadcasted_iota(I32, (ts, ts), 0)
    col = lax.broadcasted_iota(I32, (ts, ts), 1)
    upto_col = jnp.where(row <= col, 1.0, 0.0).astype(BF16)
    heads = wfzt_ref.shape[0]
    head = lax.broadcasted_iota(I32, (heads, FOX_WIDTH), 0)
    group = lax.broadcasted_iota(I32, (heads, FOX_WIDTH), 1) // HEAD_DIM
    head_lanes = jnp.where(head == group, 1.0, 0.0).astype(BF16)

    @pl.when(pl.program_id(1) == 0)
    def _():
        carry_t_ref[...] = jnp.zeros_like(carry_t_ref)

    for rows, xb in zip(subs, xbs):
        fzt = lax.dot_general(wfzt_ref[...], xb, nt, preferred_element_type=F32) + bfzt_ref[...]
        ct = carry_t_ref[...] + sum(jnp.dot(part, upto_col, preferred_element_type=F32)
                                    for part in pieces(log_sigmoid(fzt)))
        carry_t_ref[...] = ct[:, ts - 1:ts]
        ct = ct * LOG2_E
        ct_ref[0, :, rows] = ct[0:N_FOX_HEADS, :]
        c_ref[0, rows, :] = rows_to_lanes(ct, head_lanes)


def _inproj(positions, x, g, w_main, w_fz_t, b_fz_t):
    b, s, d = x.shape
    ts = SEQ_TILE
    sub = SEQ_SUBTILE
    assert sub == ATTN_TILE
    tile = lambda width: pl.BlockSpec((1, ts, width), lambda i, j: (i, j, 0))
    tile_t = pl.BlockSpec((1, DIFF_WIDTH, ts), lambda i, j: (i, 0, j))
    key_tile_t = pl.BlockSpec((1, ts // sub, DIFF_WIDTH, sub), lambda i, j: (i, j, 0, 0))
    full = lambda arr: pl.BlockSpec(arr.shape, lambda i, j: (0,) * arr.ndim)
    rows = jax.ShapeDtypeStruct((b, s, DIFF_WIDTH), BF16)
    rows_t = jax.ShapeDtypeStruct((b, DIFF_WIDTH, s), BF16)
    key_tiles_t = jax.ShapeDtypeStruct((b, s // sub, DIFF_WIDTH, sub), BF16)
    return pl.pallas_call(
        _inproj_kernel,
        grid=(b, s // ts),
        in_specs=[pl.BlockSpec((1, 1, ts), lambda i, j: (i, 0, j)), tile(d), full(g), full(w_main),
                  full(w_fz_t), full(b_fz_t)],
        out_specs=[tile_t, tile(DIFF_WIDTH), key_tile_t, tile_t, tile(FOX_WIDTH), key_tile_t,
                   tile(FOX_WIDTH), pl.BlockSpec((1, N_FOX_HEADS, ts), lambda i, j: (i, 0, j))],
        out_shape=[rows_t, rows, key_tiles_t, rows_t, rows, key_tiles_t,
                   jax.ShapeDtypeStruct((b, s, FOX_WIDTH), F32),
                   jax.ShapeDtypeStruct((b, N_FOX_HEADS, s), F32)],
        scratch_shapes=[pltpu.VMEM((w_fz_t.shape[0], 1), F32)],
        compiler_params=_params("arbitrary", "arbitrary"),
        name="inproj",
    )(positions.reshape(b, 1, s), x, g, w_main, w_fz_t, b_fz_t)


ONES_ROWS = 16


def _block_diag_queries(qt):
    row = lax.broadcasted_iota(I32, qt.shape, 0)
    zero = jnp.zeros_like(qt)
    return jnp.concatenate([jnp.where(row < HEAD_DIM, qt, zero),
                            jnp.where(row >= HEAD_DIM, qt, zero)], axis=1)


class _Chain:
    def __init__(self, qbd, keys, values_t, scratch, key_bias=None, query_bias=None):
        self.qbd, self.keys, self.values_t = qbd, keys, values_t
        self.acc_ref, self.s_ref, self.p_ref, self.stat_ref = scratch
        self.key_bias, self.query_bias = key_bias, query_bias


_RUNNING_MAX, _TILE_MAX, _RESCALE = 0, 1, 2


def _causal_attention(chains, qi):
    tk, tq = chains[0].s_ref.shape[0], chains[0].s_ref.shape[1] // 2
    ones = jnp.ones((ONES_ROWS, tk), BF16)

    def scores(ch, i, diagonal):
        s = jnp.dot(ch.keys(i), ch.qbd, preferred_element_type=F32)
        if ch.key_bias is not None:
            s = s - ch.key_bias(i)
        if diagonal is not False:
            r = lax.broadcasted_iota(I32, s.shape, 0)
            c = lax.broadcasted_iota(I32, s.shape, 1) & (tq - 1)
            visible = r <= c if diagonal is True else jnp.logical_or(r <= c, jnp.logical_not(diagonal))
            s = jnp.where(visible, s, -jnp.inf)
        ch.s_ref[...] = s
        top = jnp.max(s, axis=0, keepdims=True)
        if ch.query_bias is not None:
            top = top + ch.query_bias
        ch.stat_ref[_TILE_MAX:_TILE_MAX + 1, :] = top

    def probs(ch):
        m = ch.stat_ref[_RUNNING_MAX:_RUNNING_MAX + 1, :]
        m_new = jnp.maximum(m, ch.stat_ref[_TILE_MAX:_TILE_MAX + 1, :])
        ch.stat_ref[_RUNNING_MAX:_RUNNING_MAX + 1, :] = m_new
        ch.stat_ref[_RESCALE:_RESCALE + 1, :] = jnp.exp2(m - m_new)
        ref = m_new if ch.query_bias is None else m_new - ch.query_bias
        ch.p_ref[...] = jnp.exp2(ch.s_ref[...] - ref).astype(BF16)

    def accumulate(ch, i):
        vts = jnp.concatenate([ch.values_t(i), ones], axis=0)
        ch.acc_ref[...] = (ch.stat_ref[_RESCALE:_RESCALE + 1, :] * ch.acc_ref[...]
                           + jnp.dot(vts, ch.p_ref[...], preferred_element_type=F32))

    def body(k, diagonal_next):
        for ch in chains:
            accumulate(ch, jnp.maximum(k - 1, 0))
        for ch in chains:
            probs(ch)
        for ch in chains:
            scores(ch, k + 1, diagonal_next)

    for ch in chains:
        ch.acc_ref[...] = jnp.zeros_like(ch.acc_ref)
        ch.p_ref[...] = jnp.zeros_like(ch.p_ref)
        ch.stat_ref[...] = jnp.concatenate(
            [jnp.full((1, 2 * tq), -jnp.inf, F32), jnp.ones((ch.stat_ref.shape[0] - 1, 2 * tq), F32)], axis=0)
        scores(ch, 0, qi == 0)

    def loop_body(k, carry):
        body(k, False)
        return carry

    lax.fori_loop(0, qi - 1, loop_body, 0)

    @pl.when(qi >= 1)
    def _():
        body(qi - 1, True)

    for ch in chains:
        accumulate(ch, jnp.maximum(qi - 1, 0))
    for ch in chains:
        probs(ch)
    for ch in chains:
        accumulate(ch, qi)


def _attention_kernel(lq1_ref, lk1_ref, lq2_ref, lk2_ref, g_ref,
                      dqt_ref, dk_ref, dvt_ref, fqt_ref, fk_ref, fvt_ref, cq_ref, ck_ref,
                      do_ref, fo_ref, *scratch, lambda_init):
    tq = dqt_ref.shape[2]
    tk = dvt_ref.shape[3]
    units = dqt_ref.shape[1] // LANES
    lane = lax.broadcasted_iota(I32, (1, LANES), 1)
    scratch = [scratch[i:i + 4] for i in range(0, len(scratch), 4)]

    def unit_chains(u):
        lanes = slice(u * LANES, (u + 1) * LANES)

        def tile(ref):
            return lambda i: ref[0, pl.ds(pl.multiple_of(i * tk, tk), tk), lanes]

        def tile_t(ref):
            return lambda i: ref[0, i, lanes, :]

        cq = jnp.concatenate([cq_ref[0, u, 0:1, :], cq_ref[0, u, 1:2, :]], axis=1)

        def key_bias(i):
            ck = tile(ck_ref)(i)
            other = pltpu.roll(ck, HEAD_DIM, 1)
            first = jnp.where(lane < HEAD_DIM, ck, other)
            second = jnp.where(lane >= HEAD_DIM, ck, other)
            return jnp.concatenate([first] * (tq // LANES) + [second] * (tq // LANES), axis=1)

        return [_Chain(_block_diag_queries(dqt_ref[0, lanes, :]), tile(dk_ref), tile_t(dvt_ref),
                       scratch[2 * u]),
                _Chain(_block_diag_queries(fqt_ref[0, lanes, :]), tile(fk_ref), tile_t(fvt_ref),
                       scratch[2 * u + 1], key_bias, cq)]

    chains = [ch for u in range(units) for ch in unit_chains(u)]
    _causal_attention(chains, pl.program_id(2))

    lam = (jnp.exp(jnp.sum(lq1_ref[...] * lk1_ref[...], axis=-1, keepdims=True))
           - jnp.exp(jnp.sum(lq2_ref[...] * lk2_ref[...], axis=-1, keepdims=True)) + lambda_init)
    dv, h = DIFF_V_DIM, HEAD_DIM
    for u in range(units):
        lanes = slice(u * LANES, (u + 1) * LANES)
        acc = chains[2 * u].acc_ref
        o = (acc[0:dv, 0:tq] / acc[dv:dv + 1, 0:tq]
             - lam * (acc[0:dv, tq:] / acc[dv:dv + 1, tq:]))
        ms = jnp.mean(o * o, axis=0, keepdims=True)
        do_ref[0, :, lanes] = (o * lax.rsqrt(ms + NORM_EPS) * g_ref[...]
                               * (1.0 - lambda_init)).astype(BF16).T
        acc = chains[2 * u + 1].acc_ref
        fo_ref[0, :, lanes] = jnp.concatenate([acc[0:h, 0:tq] / acc[2 * h:2 * h + 1, 0:tq],
                                               acc[h:2 * h, tq:] / acc[2 * h:2 * h + 1, tq:]],
                                              axis=0).astype(BF16).T


def _attention(dqt, dk, dvt, fqt, fk, fvt, c_rows, c_cols, lq1, lk1, lq2, lk2, g_col, lambda_init):
    b, s, _ = dk.shape
    t = ATTN_TILE
    assert N_DIFF_HEADS == N_FOX_HEADS // 2 and N_DIFF_HEADS % ATTN_UNITS == 0
    width = ATTN_UNITS * LANES
    vec = lambda arr: pl.BlockSpec(arr.shape, lambda i, h, j: (0, 0))
    q_spec = pl.BlockSpec((1, width, t), lambda i, h, j: (i, h, j))
    k_spec = pl.BlockSpec((1, s, width), lambda i, h, j: (i, 0, h))
    v_spec = pl.BlockSpec((1, s // t, width, t), lambda i, h, j: (i, 0, h, 0))
    o_spec = pl.BlockSpec((1, t, width), lambda i, h, j: (i, j, h))
    out = jax.ShapeDtypeStruct((b, s, DIFF_WIDTH), BF16)
    chain_scratch = [pltpu.VMEM((LANES + ONES_ROWS, 2 * t), F32),
                     pltpu.VMEM((t, 2 * t), F32),
                     pltpu.VMEM((t, 2 * t), BF16),
                     pltpu.VMEM((8, 2 * t), F32)]
    return pl.pallas_call(
        functools.partial(_attention_kernel, lambda_init=lambda_init),
        grid=(b, N_DIFF_HEADS // ATTN_UNITS, s // t),
        in_specs=[vec(lq1), vec(lk1), vec(lq2), vec(lk2), vec(g_col),
                  q_spec, k_spec, v_spec, q_spec, k_spec, v_spec,
                  pl.BlockSpec((1, ATTN_UNITS, 2, t), lambda i, h, j: (i, h, 0, j)), k_spec],
        out_specs=[o_spec, o_spec],
        out_shape=[out, out],
        scratch_shapes=chain_scratch * (2 * ATTN_UNITS),
        compiler_params=_params("arbitrary", "arbitrary", "arbitrary"),
        name="attention",
    )(lq1, lk1, lq2, lk2, g_col, dqt, dk, dvt, fqt, fk, fvt, c_rows, c_cols)


def _router_kernel(d_ref, f_ref, x_ref, wo_ref, g_ref, wrt_ref, brt_ref,
                   h_ref, xn_ref, gate_ref, idx_ref, rank_ref, count_ref, carry_ref):
    tm = ROUTER_SUBTILE
    nt = (((1,), (1,)), ((), ()))

    def pieces(v, count):
        out = []
        for _ in range(count):
            part = v.astype(BF16)
            out.append(part)
            v = v - part.astype(F32)
        return out

    wh, wl = pieces(wrt_ref[...], 2)
    row = lax.broadcasted_iota(I32, (tm, tm), 0)
    col = lax.broadcasted_iota(I32, (tm, tm), 1)
    earlier = jnp.where(row < col, 1.0, 0.0).astype(BF16)
    eye = jnp.where(row == col, 1.0, 0.0).astype(BF16)

    @pl.when(pl.program_id(0) == 0)
    def _():
        carry_ref[...] = jnp.zeros_like(carry_ref)

    subs = [slice(i * tm, (i + 1) * tm) for i in range(x_ref.shape[0] // tm)]

    def logits(rows):
        mixed = (jnp.dot(d_ref[rows, :], wo_ref[0:DIFF_WIDTH, :], preferred_element_type=F32)
                 + jnp.dot(f_ref[rows, :], wo_ref[DIFF_WIDTH:, :], preferred_element_type=F32))
        h = x_ref[rows, :] + mixed
        h_ref[rows, :] = h
        xn = _rms(h, g_ref[...])
        xn_ref[rows, :] = _pack_rows(xn)
        xh, xl = pieces(xn, 2)
        return (lax.dot_general(wh, xh, nt, preferred_element_type=F32)
                + lax.dot_general(wh, xl, nt, preferred_element_type=F32)
                + lax.dot_general(wl, xh, nt, preferred_element_type=F32)) + brt_ref[...]

    def top_k(work):
        expert = lax.broadcasted_iota(I32, work.shape, 0)
        vals, ids, picks = [], [], []
        for _ in range(TOP_K):
            m = jnp.max(work, axis=0, keepdims=True)
            idx = jnp.min(jnp.where(work == m, expert, N_EXPERTS), axis=0, keepdims=True)
            pick = expert == idx
            vals.append(m)
            ids.append(idx)
            picks.append(pick)
            work = jnp.where(pick, -jnp.inf, work)
        exps = [jnp.exp(v - vals[0]) for v in vals]
        denom = exps[0] + exps[1] + exps[2] + exps[3]
        return ids, picks, [e / denom for e in exps]

    def ranks(picks):
        chosen = jnp.where(picks[0] | picks[1] | picks[2] | picks[3], 1.0, 0.0)
        rank = jnp.dot(chosen.astype(BF16), earlier, preferred_element_type=F32) + carry_ref[...]
        carry_ref[...] = carry_ref[...] + jnp.sum(chosen, axis=1, keepdims=True)
        return rank

    def emit(rows, ids, picks, gates, rank):
        slot = lax.broadcasted_iota(I32, (8, tm), 0)
        gate_t = jnp.zeros((8, tm), F32)
        idx_t = jnp.zeros((8, tm), I32)
        rank_t = jnp.zeros((8, tm), F32)
        for k in range(TOP_K):
            gate_t = jnp.where(slot == k, gates[k], gate_t)
            idx_t = jnp.where(slot == k, ids[k], idx_t)
            rank_t = jnp.where(slot == k, jnp.sum(jnp.where(picks[k], rank, 0.0), axis=0, keepdims=True),
                               rank_t)
        idx_ref[:, rows] = idx_t
        rank_ref[:, rows] = rank_t.astype(I32)
        gate_rows = jnp.concatenate([gate_t, jnp.zeros((LANES - 8, tm), F32)], axis=0)
        gate_ref[rows, :] = sum(lax.dot_general(eye, part, nt, preferred_element_type=F32)
                                for part in pieces(gate_rows, 3))

    work = [logits(rows) for rows in subs]
    chosen = [top_k(w) for w in work]
    rank = [ranks(picks) for _, picks, _ in chosen]
    for rows, (ids, picks, gates), r in zip(subs, chosen, rank):
        emit(rows, ids, picks, gates, r)
    count_ref[...] = jnp.broadcast_to(carry_ref[...], count_ref.shape)


def _router(d_out, f_out, x, w_out, g_moe, w_router_t, b_router_t):
    n, d = x.shape
    tm = ROUTER_TILE
    tile = lambda width: pl.BlockSpec((tm, width), lambda i: (i, 0))
    rows = pl.BlockSpec((8, tm), lambda i: (0, i))
    full = lambda arr: pl.BlockSpec(arr.shape, lambda i: (0,) * arr.ndim)
    return pl.pallas_call(
        _router_kernel,
        grid=(n // tm,),
        in_specs=[tile(DIFF_WIDTH), tile(FOX_WIDTH), tile(d), full(w_out), full(g_moe),
                  full(w_router_t), full(b_router_t)],
        out_specs=[tile(d), tile(d // 2), tile(LANES), rows, rows,
                   pl.BlockSpec((N_EXPERTS, LANES), lambda i: (0, 0))],
        out_shape=[jax.ShapeDtypeStruct((n, d), F32), jax.ShapeDtypeStruct((n, d // 2), U32),
                   jax.ShapeDtypeStruct((n, LANES), F32), jax.ShapeDtypeStruct((8, n), I32),
                   jax.ShapeDtypeStruct((8, n), I32), jax.ShapeDtypeStruct((N_EXPERTS, LANES), F32)],
        scratch_shapes=[pltpu.VMEM((N_EXPERTS, 1), F32)],
        compiler_params=_params("arbitrary"),
        name="router",
    )(d_out, f_out, x, w_out, g_moe, w_router_t, b_router_t)


def _slot_kernel(starts_ref, idx_ref, rank_ref, pos_ref):
    idx = idx_ref[...]
    pos = rank_ref[...]
    for e in range(N_EXPERTS):
        pos = jnp.where(idx == e, pos + starts_ref[e], pos)
    pos_ref[...] = pos


def _slots(starts, idx, rank):
    rows, n = idx.shape
    width = min(n, 2048)
    block = pl.BlockSpec((rows, width), lambda i, starts: (0, i))
    return pl.pallas_call(
        _slot_kernel,
        grid_spec=pltpu.PrefetchScalarGridSpec(num_scalar_prefetch=1, grid=(n // width,),
                                               in_specs=[block, block], out_specs=block),
        out_shape=jax.ShapeDtypeStruct((rows, n), I32),
        compiler_params=_params("arbitrary"),
        name="slots",
    )(starts, idx, rank)


SC_WINDOW = 128
SC_COLUMNS = 256


def _dispatch_sparsecore(pos, xn, n_rows):
    n, d = xn.shape
    windows = n // SC_WINDOW
    mesh = plsc.VectorSubcoreMesh(core_axis_name="core", subcore_axis_name="subcore")

    @functools.partial(pl.kernel, out_type=jax.ShapeDtypeStruct((n_rows, d), xn.dtype), mesh=mesh,
                       scratch_types=[], name="dispatch_sc")
    def scatter(x_hbm, pos_hbm, xs_hbm):
        for c in range(d // SC_COLUMNS):
            def body(x_vmem, pos_vmem, c=c):
                for k in range(TOP_K):
                    pltpu.sync_copy(x_vmem, xs_hbm.at[:, pl.ds(c * SC_COLUMNS, SC_COLUMNS)].at[pos_vmem.at[k]])

            pltpu.emit_pipeline(
                body, grid=(windows,),
                in_specs=[pl.BlockSpec((SC_WINDOW, SC_COLUMNS), index_map=lambda i, c=c: (i, c)),
                          pl.BlockSpec((TOP_K, SC_WINDOW), index_map=lambda i: (0, i))],
                out_specs=[], core_axis_name=("core", "subcore"),
                dimension_semantics=(pltpu.PARALLEL,))(x_hbm, pos_hbm)

    return scatter(xn, pos)


def _expert_kernel(tile_expert_ref, n_used_ref, next_expert_ref, slot_ref,
                   x_ref, w1_ref, b1_ref, w2_ref, b2_ref, y_ref,
                   w1f_ref, w2f_ref, sem_ref, w1b_ref, w2p_ref, w2b_ref):
    t = pl.program_id(0)
    fresh = (t == 0) | (tile_expert_ref[t] != tile_expert_ref[jnp.maximum(t - 1, 0)])
    used = t < n_used_ref[0]

    def weight_copies(e, s):
        return (pltpu.make_async_copy(w1_ref.at[e], w1f_ref.at[s], sem_ref.at[0, s]),
                pltpu.make_async_copy(w2_ref.at[e], w2f_ref.at[s], sem_ref.at[1, s]))

    @pl.when(t == 0)
    def _():
        for copy in weight_copies(tile_expert_ref[0], slot_ref[0]):
            copy.start()

    @pl.when(fresh & used)
    def _():
        s = slot_ref[t]
        for copy in weight_copies(tile_expert_ref[t], s):
            copy.wait()

        @pl.when(next_expert_ref[t] >= 0)
        def _():
            for copy in weight_copies(next_expert_ref[t], 1 - s):
                copy.start()

        rows = LANES
        half = LANES // 2
        for r in range(0, w1b_ref.shape[0], rows):
            w1b_ref[r:r + rows, :] = w1f_ref[s, r:r + rows, :].astype(BF16)
        for c in range(w2p_ref.shape[0]):
            lanes = slice(c * LANES, (c + 1) * LANES)
            for r in range(0, w2b_ref.shape[0], rows):
                for b in range(2):
                    w2p_ref[c, pl.ds(r + b, half, stride=2), :] = (
                        w2f_ref[s, r + b * half:r + (b + 1) * half, lanes])
            w2b_ref[:, lanes] = w2p_ref[c].astype(BF16)

    @pl.when(used)
    def _():
        xb = _unpack_rows(x_ref[...]).astype(BF16)
        h = jnp.dot(xb, w1b_ref[...], preferred_element_type=F32) + b1_ref[0]
        even = lax.broadcasted_iota(I32, (1, LANES), 1) % 2 == 0
        parts = []
        for q in range(h.shape[1] // (2 * LANES)):
            a = h[:, (2 * q) * LANES:(2 * q + 1) * LANES]
            b = h[:, (2 * q + 1) * LANES:(2 * q + 2) * LANES]
            hg = jnp.where(even, a, pltpu.roll(b, 1, 1))
            hl = jnp.where(even, pltpu.roll(a, LANES - 1, 1), b)
            glu = jnp.minimum(hg, SWIGLU_LIMIT)
            lin = jnp.clip(hl, -SWIGLU_LIMIT, SWIGLU_LIMIT)
            parts.append((glu * jax.nn.sigmoid(SWIGLU_ALPHA * glu) * (lin + 1.0)).astype(BF16))
        act = jnp.concatenate(parts, axis=1)
        y_ref[...] = _pack_rows(jnp.dot(act, w2b_ref[...], preferred_element_type=F32) + b2_ref[0])

    @pl.when(jnp.logical_not(used))
    def _():
        y_ref[...] = jnp.zeros_like(y_ref)


def _experts(tile_expert, n_used, next_expert, slot, xs, w1, b1, w2, b2):
    n_rows, d = xs.shape
    tm = EXPERT_TILE
    by_expert = lambda arr: pl.BlockSpec((1,) + arr.shape[1:], lambda i, te, *_: (te[i], 0, 0))
    rows = pl.BlockSpec((tm, d), lambda i, *_: (i, 0))
    used_rows = pl.BlockSpec((tm, d), lambda i, te, nu, *_: (jnp.minimum(i, nu[0] - 1), 0))
    hbm = pl.BlockSpec(memory_space=pl.ANY)
    return pl.pallas_call(
        _expert_kernel,
        grid_spec=pltpu.PrefetchScalarGridSpec(
            num_scalar_prefetch=4,
            grid=(n_rows // tm,),
            in_specs=[used_rows, hbm, by_expert(b1), hbm, by_expert(b2)],
            out_specs=rows,
            scratch_shapes=[pltpu.VMEM((2,) + w1.shape[1:], F32), pltpu.VMEM((2,) + w2.shape[1:], F32),
                            pltpu.SemaphoreType.DMA((2, 2)),
                            pltpu.VMEM(w1.shape[1:], BF16),
                            pltpu.VMEM((w2.shape[2] // LANES, w2.shape[1], LANES), F32),
                            pltpu.VMEM(w2.shape[1:], BF16)]),
        out_shape=jax.ShapeDtypeStruct(xs.shape, xs.dtype),
        compiler_params=pltpu.CompilerParams(dimension_semantics=("arbitrary",),
                                             vmem_limit_bytes=EXPERT_VMEM_LIMIT),
        name="experts",
    )(tile_expert, n_used, next_expert, slot, xs, w1, b1, w2, b2)


def _gather_sparsecore(pos, y):
    d = y.shape[1]
    total = pos.size
    mesh = plsc.VectorSubcoreMesh(core_axis_name="core", subcore_axis_name="subcore")

    @functools.partial(pl.kernel, out_type=jax.ShapeDtypeStruct((total, d), y.dtype), mesh=mesh,
                       scratch_types=[], name="gather_sc")
    def gather(y_hbm, pos_hbm, out_hbm):
        for c in range(d // SC_COLUMNS):
            def body(pos_vmem, out_vmem, c=c):
                pltpu.sync_copy(y_hbm.at[:, pl.ds(c * SC_COLUMNS, SC_COLUMNS)].at[pos_vmem.at[0]], out_vmem)

            pltpu.emit_pipeline(
                body, grid=(total // SC_WINDOW,),
                in_specs=[pl.BlockSpec((1, SC_WINDOW), index_map=lambda i: (0, i))],
                out_specs=[pl.BlockSpec((SC_WINDOW, SC_COLUMNS), index_map=lambda i, c=c: (i, c))],
                core_axis_name=("core", "subcore"),
                dimension_semantics=(pltpu.PARALLEL,))(pos_hbm, out_hbm)

    return gather(y, pos.reshape(1, -1))


def _combine_kernel(h_ref, gate_ref, p_ref, wg_ref, wp_ref, gple_ref, gfin_ref, *refs, final_norm):
    y_refs, o_ref = refs[:TOP_K], refs[TOP_K]
    ts = COMBINE_SUBTILE
    subs = [slice(i * ts, (i + 1) * ts) for i in range(h_ref.shape[0] // ts)]

    def mix(rows):
        gates = gate_ref[rows, :]
        moe = gates[:, 0:1] * _unpack_rows(y_refs[0][rows, :])
        for k in range(1, TOP_K):
            moe = moe + gates[:, k:k + 1] * _unpack_rows(y_refs[k][rows, :])
        h = h_ref[rows, :] + moe
        logit = jnp.dot(_rms(h, gple_ref[...]).astype(BF16), wg_ref[...], preferred_element_type=F32)
        emb = jnp.dot(p_ref[rows, :].astype(BF16), wp_ref[...], preferred_element_type=F32)
        return h, logit, emb

    mixed = [mix(rows) for rows in subs]
    for rows, (h, logit, emb) in zip(subs, mixed):
        h = h + jax.nn.sigmoid(logit) * emb
        o_ref[rows, :] = _rms(h, gfin_ref[...]) if final_norm else h


def _combine(h, gates, p, w_gate, w_proj, g_ple, g_final, y_tok, final_norm):
    n, d = h.shape
    tm = TOKEN_TILE
    tile = lambda width: pl.BlockSpec((tm, width), lambda i: (i, 0))
    full = lambda arr: pl.BlockSpec(arr.shape, lambda i: (0,) * arr.ndim)
    choice = lambda k: pl.BlockSpec((tm, y_tok.shape[1]), lambda i: (k * (n // tm) + i, 0))
    return pl.pallas_call(
        functools.partial(_combine_kernel, final_norm=final_norm),
        grid=(n // tm,),
        in_specs=[tile(d), tile(LANES), tile(p.shape[1]), full(w_gate), full(w_proj),
                  full(g_ple), full(g_final)] + [choice(k) for k in range(TOP_K)],
        out_specs=tile(d),
        out_shape=jax.ShapeDtypeStruct((n, d), F32),
        compiler_params=_params("arbitrary"),
        name="combine",
    )(h, gates, p, w_gate, w_proj, g_ple, g_final, *([y_tok] * TOP_K))


def _layer(h, p, positions, g_attn, w_in, lq1, lk1, lq2, lk2, g_subln, b_forget, w_out, g_moe,
           w_router, b_router, w_e1, b_e1, w_e2, b_e2, g_ple, w_ple_gate, w_ple_proj, g_final,
           lambda_init, final_norm):
    b, s, d = h.shape
    n = b * s
    row = lambda v: v.reshape(1, -1)

    w_main = w_in[:, :MAIN_WIDTH].astype(BF16)
    pad_heads = ((0, 2 * N_FOX_HEADS - N_FOX_HEADS), (0, 0))
    w_fz_t = jnp.pad(w_in[:, MAIN_WIDTH:].T, pad_heads).astype(BF16)
    b_fz_t = jnp.pad(b_forget.reshape(-1, 1), pad_heads)
    b1 = b_e1[:, None, :]
    b2 = b_e2[:, None, :]

    dq_t, dk, dv_t, fq_t, fk, fv_t, c, c_t = _inproj(positions, h, row(g_attn), w_main, w_fz_t, b_fz_t)
    c_rows = c_t.reshape(b, N_FOX_HEADS // 2, 2, s)
    d_out, f_out = _attention(dq_t, dk, dv_t, fq_t, fk, fv_t, c_rows, c, row(lq1), row(lk1), row(lq2),
                              row(lk2), g_subln.reshape(-1, 1), lambda_init)

    h1, xn, gates, idx, rank, counts = _router(
        d_out.reshape(n, DIFF_WIDTH), f_out.reshape(n, FOX_WIDTH), h.reshape(n, d),
        w_out.astype(BF16), row(g_moe), w_router.T, b_router.reshape(-1, 1))

    te = EXPERT_TILE
    n_rows = n * TOP_K + N_EXPERTS * te
    counts = counts[:, 0].astype(I32)
    padded = (counts + te - 1) // te * te
    ends = jnp.cumsum(padded)
    starts = ends - padded
    pos = _slots(starts.astype(I32), idx, rank)[:TOP_K]
    tile_start = jnp.arange(n_rows // te, dtype=I32) * te
    tile_expert = jnp.minimum(jnp.sum((ends[None, :] <= tile_start[:, None]).astype(I32), axis=1),
                              N_EXPERTS - 1)
    n_used = (ends[-1:] // te).astype(I32)
    owns = padded > 0
    later = lax.cummin(jnp.where(owns, jnp.arange(N_EXPERTS, dtype=I32), N_EXPERTS), reverse=True)
    following = jnp.concatenate([later[1:], jnp.full((1,), N_EXPERTS, I32)])
    following = jnp.where(following < N_EXPERTS, following, -1)
    next_expert = following[tile_expert]
    slot = ((jnp.cumsum(owns.astype(I32)) - 1) % 2)[tile_expert]

    xs = _dispatch_sparsecore(pos.astype(I32), xn, n_rows)
    y = _experts(tile_expert, n_used, next_expert.astype(I32), slot.astype(I32), xs, w_e1, b1, w_e2, b2)
    y_tok = _gather_sparsecore(pos.astype(I32), y)
    out = _combine(h1, gates, p.reshape(n, PLE_DIM), w_ple_gate.astype(BF16),
                   w_ple_proj.astype(BF16), row(g_ple), row(g_final), y_tok, final_norm)
    return out.reshape(b, s, d)


def kernel(x, p, positions, g_attn, w_in, lambda_q1, lambda_k1, lambda_q2, lambda_k2, g_subln, b_forget, w_out, g_moe, w_router, b_router, w_e1, b_e1, w_e2, b_e2, g_ple, w_ple_gate, w_ple_proj, g_final):
    depth = g_attn.shape[0]
    h = x
    for i in range(depth):
        lambda_init = 0.8 - 0.6 * math.exp(-0.3 * i)
        h = _layer(h, p[i], positions, g_attn[i], w_in[i], lambda_q1[i], lambda_k1[i],
                   lambda_q2[i], lambda_k2[i], g_subln[i], b_forget[i], w_out[i], g_moe[i],
                   w_router[i], b_router[i], w_e1[i], b_e1[i], w_e2[i], b_e2[i], g_ple[i],
                   w_ple_gate[i], w_ple_proj[i], g_final, lambda_init, i == depth - 1)
    return h
```

```python
import functools
import math

import jax
import jax.numpy as jnp
from jax import lax
from jax.experimental import pallas as pl
from jax.experimental.pallas import tpu as pltpu
from jax.experimental.pallas import tpu_sc as plsc

F32 = jnp.float32
BF16 = jnp.bfloat16
I32 = jnp.int32
U32 = jnp.uint32

LANES = 128
D_MODEL = 1024
HEAD_DIM = 64
N_DIFF_HEADS = 4
DIFF_V_DIM = 2 * HEAD_DIM
N_FOX_HEADS = 8
DIFF_WIDTH = N_DIFF_HEADS * DIFF_V_DIM
FOX_WIDTH = N_FOX_HEADS * HEAD_DIM
ROT_DIM = HEAD_DIM // 4
ROPE_THETA = 500000.0
N_EXPERTS = 32
TOP_K = 4
D_EXPERT = D_MODEL
SWIGLU_ALPHA = 1.702
SWIGLU_LIMIT = 7.0
PLE_DIM = 256
NORM_EPS = 1e-5
LOG2_E = math.log2(math.e)
MAIN_WIDTH = 3 * DIFF_WIDTH + 3 * FOX_WIDTH

SEQ_TILE = 1024
SEQ_SUBTILE = 256
ATTN_TILE = 256
ATTN_UNITS = 2
TOKEN_TILE = 1024
COMBINE_SUBTILE = 256
ROUTER_TILE = 1024
ROUTER_SUBTILE = 256
EXPERT_TILE = 256
VMEM_LIMIT = 48 * 1024 * 1024
EXPERT_VMEM_LIMIT = 58 * 1024 * 1024


def _params(*semantics):
    return pltpu.CompilerParams(dimension_semantics=semantics, vmem_limit_bytes=VMEM_LIMIT)


def _rms(x, g):
    return x * lax.rsqrt(jnp.mean(x * x, axis=-1, keepdims=True) + NORM_EPS) * g


def _pack_rows(v):
    half = v.shape[1] // 2
    bits = pltpu.bitcast(v.astype(BF16).astype(F32), U32)
    return (bits[:, :half] >> 16) | bits[:, half:]


def _unpack_rows(words):
    low = pltpu.bitcast(words << 16, F32)
    high = pltpu.bitcast(words & jnp.uint32(0xFFFF0000), F32)
    return jnp.concatenate([low, high], axis=1)


def _inproj_kernel(pos_ref, x_ref, g_ref, w_ref, wfzt_ref, bfzt_ref,
                   dqt_ref, dk_ref, dvt_ref, fqt_ref, fk_ref, fvt_ref, c_ref, ct_ref, carry_t_ref):
    ts = SEQ_SUBTILE
    subs = [slice(i * ts, (i + 1) * ts) for i in range(x_ref.shape[1] // ts)]
    xbs = [_rms(x_ref[0, rows, :], g_ref[...]).astype(BF16) for rows in subs]
    projs = [jnp.dot(xb, w_ref[...], preferred_element_type=F32) for xb in xbs]
    nt = (((1,), (1,)), ((), ()))
    tn = (((0,), (0,)), ((), ()))

    def pieces(v):
        hi = v.astype(BF16)
        r1 = v - hi.astype(F32)
        mid = r1.astype(BF16)
        return hi, mid, (r1 - mid.astype(F32)).astype(BF16)

    def rows_to_lanes(v, select):
        return sum(lax.dot_general(part, select, tn, preferred_element_type=F32) for part in pieces(v))

    half_rot = ROT_DIM // 2
    freq = lax.broadcasted_iota(I32, (half_rot, 1), 0)
    inv_freq = jnp.power(ROPE_THETA, -(2 * freq).astype(F32) / ROT_DIM)
    r = lax.broadcasted_iota(I32, (2 * half_rot, 2 * LANES), 0)
    lane2 = lax.broadcasted_iota(I32, (2 * half_rot, 2 * LANES), 1)
    d = lane2 % HEAD_DIM
    same_freq = jnp.where(d % half_rot == r % half_rot, 1.0, 0.0)
    cos_part = jnp.where((r < half_rot) & (lane2 < LANES) & (d < ROT_DIM), same_freq, 0.0)
    sin_part = jnp.where((r >= half_rot) & (lane2 >= LANES) & (d < ROT_DIM),
                         jnp.where(d < half_rot, -same_freq, same_freq), 0.0)
    expand = (cos_part + sin_part).astype(BF16)
    lane = lax.broadcasted_iota(I32, (1, LANES), 1) % HEAD_DIM
    first_half = lane < half_rot

    def rope_tables(rows):
        ang = inv_freq * pos_ref[0, :, rows].astype(F32)
        trig = jnp.concatenate([jnp.cos(ang), jnp.sin(ang)], axis=0)
        tables = rows_to_lanes(trig, expand)
        return jnp.where(lane < ROT_DIM, tables[:, :LANES], 1.0), tables[:, LANES:]

    def rope(t, cosf, sinf):
        outs = []
        for j in range(t.shape[1] // LANES):
            tj = t[:, j * LANES:(j + 1) * LANES]
            partner = jnp.where(first_half,
                                pltpu.roll(tj, LANES - ROT_DIM // 2, 1),
                                pltpu.roll(tj, ROT_DIM // 2, 1))
            outs.append(tj * cosf + partner * sinf)
        return jnp.concatenate(outs, axis=1)

    scale = HEAD_DIM ** -0.5 * LOG2_E
    w = DIFF_WIDTH
    for i, (rows, proj) in enumerate(zip(subs, projs)):
        cosf, sinf = rope_tables(rows)
        dqt_ref[0, :, rows] = (rope(proj[:, 0:w], cosf, sinf) * scale).astype(BF16).T
        dk_ref[0, rows, :] = rope(proj[:, w:2 * w], cosf, sinf).astype(BF16)
        dvt_ref[0, i] = proj[:, 2 * w:3 * w].astype(BF16).T
        fqt_ref[0, :, rows] = (proj[:, 3 * w:4 * w] * scale).astype(BF16).T
        fk_ref[0, rows, :] = proj[:, 4 * w:5 * w].astype(BF16)
        fvt_ref[0, i] = proj[:, 5 * w:6 * w].astype(BF16).T

    def log_sigmoid(z):
        return jnp.minimum(z, 0.0) - jnp.log1p(jnp.exp(-jnp.abs(z)))

    row = lax.broadcasted_iota(I32, (ts, ts), 0)
    col = lax.broadcasted_iota(I32, (ts, ts), 1)
    upto_col = jnp.where(row <= col, 1.0, 0.0).astype(BF16)
    heads = wfzt_ref.shape[0]
    head = lax.broadcasted_iota(I32, (heads, FOX_WIDTH), 0)
    group = lax.broadcasted_iota(I32, (heads, FOX_WIDTH), 1) // HEAD_DIM
    head_lanes = jnp.where(head == group, 1.0, 0.0).astype(BF16)

    @pl.when(pl.program_id(1) == 0)
    def _():
        carry_t_ref[...] = jnp.zeros_like(carry_t_ref)

    for rows, xb in zip(subs, xbs):
        fzt = lax.dot_general(wfzt_ref[...], xb, nt, preferred_element_type=F32) + bfzt_ref[...]
        ct = carry_t_ref[...] + sum(jnp.dot(part, upto_col, preferred_element_type=F32)
                                    for part in pieces(log_sigmoid(fzt)))
        carry_t_ref[...] = ct[:, ts - 1:ts]
        ct = ct * LOG2_E
        ct_ref[0, :, rows] = ct[0:N_FOX_HEADS, :]
        c_ref[0, rows, :] = rows_to_lanes(ct, head_lanes)


def _inproj(positions, x, g, w_main, w_fz_t, b_fz_t):
    b, s, d = x.shape
    ts = SEQ_TILE
    sub = SEQ_SUBTILE
    assert sub == ATTN_TILE
    tile = lambda width: pl.BlockSpec((1, ts, width), lambda i, j: (i, j, 0))
    tile_t = pl.BlockSpec((1, DIFF_WIDTH, ts), lambda i, j: (i, 0, j))
    key_tile_t = pl.BlockSpec((1, ts // sub, DIFF_WIDTH, sub), lambda i, j: (i, j, 0, 0))
    full = lambda arr: pl.BlockSpec(arr.shape, lambda i, j: (0,) * arr.ndim)
    rows = jax.ShapeDtypeStruct((b, s, DIFF_WIDTH), BF16)
    rows_t = jax.ShapeDtypeStruct((b, DIFF_WIDTH, s), BF16)
    key_tiles_t = jax.ShapeDtypeStruct((b, s // sub, DIFF_WIDTH, sub), BF16)
    return pl.pallas_call(
        _inproj_kernel,
        grid=(b, s // ts),
        in_specs=[pl.BlockSpec((1, 1, ts), lambda i, j: (i, 0, j)), tile(d), full(g), full(w_main),
                  full(w_fz_t), full(b_fz_t)],
        out_specs=[tile_t, tile(DIFF_WIDTH), key_tile_t, tile_t, tile(FOX_WIDTH), key_tile_t,
                   tile(FOX_WIDTH), pl.BlockSpec((1, N_FOX_HEADS, ts), lambda i, j: (i, 0, j))],
        out_shape=[rows_t, rows, key_tiles_t, rows_t, rows, key_tiles_t,
                   jax.ShapeDtypeStruct((b, s, FOX_WIDTH), F32),
                   jax.ShapeDtypeStruct((b, N_FOX_HEADS, s), F32)],
        scratch_shapes=[pltpu.VMEM((w_fz_t.shape[0], 1), F32)],
        compiler_params=_params("arbitrary", "arbitrary"),
        name="inproj",
    )(positions.reshape(b, 1, s), x, g, w_main, w_fz_t, b_fz_t)


ONES_ROWS = 16


def _block_diag_queries(qt):
    row = lax.broadcasted_iota(I32, qt.shape, 0)
    zero = jnp.zeros_like(qt)
    return jnp.concatenate([jnp.where(row < HEAD_DIM, qt, zero),
                            jnp.where(row >= HEAD_DIM, qt, zero)], axis=1)


class _Chain:
    def __init__(self, qbd, keys, values_t, scratch, key_bias=None, query_bias=None):
        self.qbd, self.keys, self.values_t = qbd, keys, values_t
        self.acc_ref, self.s_ref, self.p_ref, self.stat_ref = scratch
        self.key_bias, self.query_bias = key_bias, query_bias


_RUNNING_MAX, _TILE_MAX, _RESCALE = 0, 1, 2


def _causal_attention(chains, qi):
    tk, tq = chains[0].s_ref.shape[0], chains[0].s_ref.shape[1] // 2
    ones = jnp.ones((ONES_ROWS, tk), BF16)

    def scores(ch, i, diagonal):
        s = jnp.dot(ch.keys(i), ch.qbd, preferred_element_type=F32)
        if ch.key_bias is not None:
            s = s - ch.key_bias(i)
        if diagonal is not False:
            r = lax.broadcasted_iota(I32, s.shape, 0)
            c = lax.broadcasted_iota(I32, s.shape, 1) & (tq - 1)
            visible = r <= c if diagonal is True else jnp.logical_or(r <= c, jnp.logical_not(diagonal))
            s = jnp.where(visible, s, -jnp.inf)
        ch.s_ref[...] = s
        top = jnp.max(s, axis=0, keepdims=True)
        if ch.query_bias is not None:
            top = top + ch.query_bias
        ch.stat_ref[_TILE_MAX:_TILE_MAX + 1, :] = top

    def probs(ch):
        m = ch.stat_ref[_RUNNING_MAX:_RUNNING_MAX + 1, :]
        m_new = jnp.maximum(m, ch.stat_ref[_TILE_MAX:_TILE_MAX + 1, :])
        ch.stat_ref[_RUNNING_MAX:_RUNNING_MAX + 1, :] = m_new
        ch.stat_ref[_RESCALE:_RESCALE + 1, :] = jnp.exp2(m - m_new)
        ref = m_new if ch.query_bias is None else m_new - ch.query_bias
        ch.p_ref[...] = jnp.exp2(ch.s_ref[...] - ref).astype(BF16)

    def accumulate(ch, i):
        vts = jnp.concatenate([ch.values_t(i), ones], axis=0)
        ch.acc_ref[...] = (ch.stat_ref[_RESCALE:_RESCALE + 1, :] * ch.acc_ref[...]
                           + jnp.dot(vts, ch.p_ref[...], preferred_element_type=F32))

    def body(k, diagonal_next):
        for ch in chains:
            accumulate(ch, jnp.maximum(k - 1, 0))
        for ch in chains:
            probs(ch)
        for ch in chains:
            scores(ch, k + 1, diagonal_next)

    for ch in chains:
        ch.acc_ref[...] = jnp.zeros_like(ch.acc_ref)
        ch.p_ref[...] = jnp.zeros_like(ch.p_ref)
        ch.stat_ref[...] = jnp.concatenate(
            [jnp.full((1, 2 * tq), -jnp.inf, F32), jnp.ones((ch.stat_ref.shape[0] - 1, 2 * tq), F32)], axis=0)
        scores(ch, 0, qi == 0)

    def loop_body(k, carry):
        body(k, False)
        return carry

    lax.fori_loop(0, qi - 1, loop_body, 0)

    @pl.when(qi >= 1)
    def _():
        body(qi - 1, True)

    for ch in chains:
        accumulate(ch, jnp.maximum(qi - 1, 0))
    for ch in chains:
        probs(ch)
    for ch in chains:
        accumulate(ch, qi)


def _attention_kernel(lq1_ref, lk1_ref, lq2_ref, lk2_ref, g_ref,
                      dqt_ref, dk_ref, dvt_ref, fqt_ref, fk_ref, fvt_ref, cq_ref, ck_ref,
                      do_ref, fo_ref, *scratch, lambda_init):
    tq = dqt_ref.shape[2]
    tk = dvt_ref.shape[3]
    units = dqt_ref.shape[1] // LANES
    lane = lax.broadcasted_iota(I32, (1, LANES), 1)
    scratch = [scratch[i:i + 4] for i in range(0, len(scratch), 4)]

    def unit_chains(u):
        lanes = slice(u * LANES, (u + 1) * LANES)

        def tile(ref):
            return lambda i: ref[0, pl.ds(pl.multiple_of(i * tk, tk), tk), lanes]

        def tile_t(ref):
            return lambda i: ref[0, i, lanes, :]

        cq = jnp.concatenate([cq_ref[0, u, 0:1, :], cq_ref[0, u, 1:2, :]], axis=1)

        def key_bias(i):
            ck = tile(ck_ref)(i)
            other = pltpu.roll(ck, HEAD_DIM, 1)
            first = jnp.where(lane < HEAD_DIM, ck, other)
            second = jnp.where(lane >= HEAD_DIM, ck, other)
            return jnp.concatenate([first] * (tq // LANES) + [second] * (tq // LANES), axis=1)

        return [_Chain(_block_diag_queries(dqt_ref[0, lanes, :]), tile(dk_ref), tile_t(dvt_ref),
                       scratch[2 * u]),
                _Chain(_block_diag_queries(fqt_ref[0, lanes, :]), tile(fk_ref), tile_t(fvt_ref),
                       scratch[2 * u + 1], key_bias, cq)]

    chains = [ch for u in range(units) for ch in unit_chains(u)]
    _causal_attention(chains, pl.program_id(2))

    lam = (jnp.exp(jnp.sum(lq1_ref[...] * lk1_ref[...], axis=-1, keepdims=True))
           - jnp.exp(jnp.sum(lq2_ref[...] * lk2_ref[...], axis=-1, keepdims=True)) + lambda_init)
    dv, h = DIFF_V_DIM, HEAD_DIM
    for u in range(units):
        lanes = slice(u * LANES, (u + 1) * LANES)
        acc = chains[2 * u].acc_ref
        o = (acc[0:dv, 0:tq] / acc[dv:dv + 1, 0:tq]
             - lam * (acc[0:dv, tq:] / acc[dv:dv + 1, tq:]))
        ms = jnp.mean(o * o, axis=0, keepdims=True)
        do_ref[0, :, lanes] = (o * lax.rsqrt(ms + NORM_EPS) * g_ref[...]
                               * (1.0 - lambda_init)).astype(BF16).T
        acc = chains[2 * u + 1].acc_ref
        fo_ref[0, :, lanes] = jnp.concatenate([acc[0:h, 0:tq] / acc[2 * h:2 * h + 1, 0:tq],
                                               acc[h:2 * h, tq:] / acc[2 * h:2 * h + 1, tq:]],
                                              axis=0).astype(BF16).T


def _attention(dqt, dk, dvt, fqt, fk, fvt, c_rows, c_cols, lq1, lk1, lq2, lk2, g_col, lambda_init):
    b, s, _ = dk.shape
    t = ATTN_TILE
    assert N_DIFF_HEADS == N_FOX_HEADS // 2 and N_DIFF_HEADS % ATTN_UNITS == 0
    width = ATTN_UNITS * LANES
    vec = lambda arr: pl.BlockSpec(arr.shape, lambda i, h, j: (0, 0))
    q_spec = pl.BlockSpec((1, width, t), lambda i, h, j: (i, h, j))
    k_spec = pl.BlockSpec((1, s, width), lambda i, h, j: (i, 0, h))
    v_spec = pl.BlockSpec((1, s // t, width, t), lambda i, h, j: (i, 0, h, 0))
    o_spec = pl.BlockSpec((1, t, width), lambda i, h, j: (i, j, h))
    out = jax.ShapeDtypeStruct((b, s, DIFF_WIDTH), BF16)
    chain_scratch = [pltpu.VMEM((LANES + ONES_ROWS, 2 * t), F32),
                     pltpu.VMEM((t, 2 * t), F32),
                     pltpu.VMEM((t, 2 * t), BF16),
                     pltpu.VMEM((8, 2 * t), F32)]
    return pl.pallas_call(
        functools.partial(_attention_kernel, lambda_init=lambda_init),
        grid=(b, N_DIFF_HEADS // ATTN_UNITS, s // t),
        in_specs=[vec(lq1), vec(lk1), vec(lq2), vec(lk2), vec(g_col),
                  q_spec, k_spec, v_spec, q_spec, k_spec, v_spec,
                  pl.BlockSpec((1, ATTN_UNITS, 2, t), lambda i, h, j: (i, h, 0, j)), k_spec],
        out_specs=[o_spec, o_spec],
        out_shape=[out, out],
        scratch_shapes=chain_scratch * (2 * ATTN_UNITS),
        compiler_params=_params("arbitrary", "arbitrary", "arbitrary"),
        name="attention",
    )(lq1, lk1, lq2, lk2, g_col, dqt, dk, dvt, fqt, fk, fvt, c_rows, c_cols)


def _router_kernel(d_ref, f_ref, x_ref, wo_ref, g_ref, wrt_ref, brt_ref,
                   h_ref, xn_ref, gate_ref, idx_ref, rank_ref, count_ref, carry_ref):
    tm = ROUTER_SUBTILE
    nt = (((1,), (1,)), ((), ()))

    def pieces(v, count):
        out = []
        for _ in range(count):
            part = v.astype(BF16)
            out.append(part)
            v = v - part.astype(F32)
        return out

    wh, wl = pieces(wrt_ref[...], 2)
    row = lax.broadcasted_iota(I32, (tm, tm), 0)
    col = lax.broadcasted_iota(I32, (tm, tm), 1)
    earlier = jnp.where(row < col, 1.0, 0.0).astype(BF16)
    eye = jnp.where(row == col, 1.0, 0.0).astype(BF16)

    @pl.when(pl.program_id(0) == 0)
    def _():
        carry_ref[...] = jnp.zeros_like(carry_ref)

    subs = [slice(i * tm, (i + 1) * tm) for i in range(x_ref.shape[0] // tm)]

    def logits(rows):
        mixed = (jnp.dot(d_ref[rows, :], wo_ref[0:DIFF_WIDTH, :], preferred_element_type=F32)
                 + jnp.dot(f_ref[rows, :], wo_ref[DIFF_WIDTH:, :], preferred_element_type=F32))
        h = x_ref[rows, :] + mixed
        h_ref[rows, :] = h
        xn = _rms(h, g_ref[...])
        xn_ref[rows, :] = _pack_rows(xn)
        xh, xl = pieces(xn, 2)
        return (lax.dot_general(wh, xh, nt, preferred_element_type=F32)
                + lax.dot_general(wh, xl, nt, preferred_element_type=F32)
                + lax.dot_general(wl, xh, nt, preferred_element_type=F32)) + brt_ref[...]

    def top_k(work):
        expert = lax.broadcasted_iota(I32, work.shape, 0)
        vals, ids, picks = [], [], []
        for _ in range(TOP_K):
            m = jnp.max(work, axis=0, keepdims=True)
            idx = jnp.min(jnp.where(work == m, expert, N_EXPERTS), axis=0, keepdims=True)
            pick = expert == idx
            vals.append(m)
            ids.append(idx)
            picks.append(pick)
            work = jnp.where(pick, -jnp.inf, work)
        exps = [jnp.exp(v - vals[0]) for v in vals]
        denom = exps[0] + exps[1] + exps[2] + exps[3]
        return ids, picks, [e / denom for e in exps]

    def ranks(picks):
        chosen = jnp.where(picks[0] | picks[1] | picks[2] | picks[3], 1.0, 0.0)
        rank = jnp.dot(chosen.astype(BF16), earlier, preferred_element_type=F32) + carry_ref[...]
        carry_ref[...] = carry_ref[...] + jnp.sum(chosen, axis=1, keepdims=True)
        return rank

    def emit(rows, ids, picks, gates, rank):
        slot = lax.broadcasted_iota(I32, (8, tm), 0)
        gate_t = jnp.zeros((8, tm), F32)
        idx_t = jnp.zeros((8, tm), I32)
        rank_t = jnp.zeros((8, tm), F32)
        for k in range(TOP_K):
            gate_t = jnp.where(slot == k, gates[k], gate_t)
            idx_t = jnp.where(slot == k, ids[k], idx_t)
            rank_t = jnp.where(slot == k, jnp.sum(jnp.where(picks[k], rank, 0.0), axis=0, keepdims=True),
                               rank_t)
        idx_ref[:, rows] = idx_t
        rank_ref[:, rows] = rank_t.astype(I32)
        gate_rows = jnp.concatenate([gate_t, jnp.zeros((LANES - 8, tm), F32)], axis=0)
        gate_ref[rows, :] = sum(lax.dot_general(eye, part, nt, preferred_element_type=F32)
                                for part in pieces(gate_rows, 3))

    work = [logits(rows) for rows in subs]
    chosen = [top_k(w) for w in work]
    rank = [ranks(picks) for _, picks, _ in chosen]
    for rows, (ids, picks, gates), r in zip(subs, chosen, rank):
        emit(rows, ids, picks, gates, r)
    count_ref[...] = jnp.broadcast_to(carry_ref[...], count_ref.shape)


def _router(d_out, f_out, x, w_out, g_moe, w_router_t, b_router_t):
    n, d = x.shape
    tm = ROUTER_TILE
    tile = lambda width: pl.BlockSpec((tm, width), lambda i: (i, 0))
    rows = pl.BlockSpec((8, tm), lambda i: (0, i))
    full = lambda arr: pl.BlockSpec(arr.shape, lambda i: (0,) * arr.ndim)
    return pl.pallas_call(
        _router_kernel,
        grid=(n // tm,),
        in_specs=[tile(DIFF_WIDTH), tile(FOX_WIDTH), tile(d), full(w_out), full(g_moe),
                  full(w_router_t), full(b_router_t)],
        out_specs=[tile(d), tile(d // 2), tile(LANES), rows, rows,
                   pl.BlockSpec((N_EXPERTS, LANES), lambda i: (0, 0))],
        out_shape=[jax.ShapeDtypeStruct((n, d), F32), jax.ShapeDtypeStruct((n, d // 2), U32),
                   jax.ShapeDtypeStruct((n, LANES), F32), jax.ShapeDtypeStruct((8, n), I32),
                   jax.ShapeDtypeStruct((8, n), I32), jax.ShapeDtypeStruct((N_EXPERTS, LANES), F32)],
        scratch_shapes=[pltpu.VMEM((N_EXPERTS, 1), F32)],
        compiler_params=_params("arbitrary"),
        name="router",
    )(d_out, f_out, x, w_out, g_moe, w_router_t, b_router_t)


def _slot_kernel(starts_ref, idx_ref, rank_ref, pos_ref):
    idx = idx_ref[...]
    pos = rank_ref[...]
    for e in range(N_EXPERTS):
        pos = jnp.where(idx == e, pos + starts_ref[e], pos)
    pos_ref[...] = pos


def _slots(starts, idx, rank):
    rows, n = idx.shape
    width = min(n, 2048)
    block = pl.BlockSpec((rows, width), lambda i, starts: (0, i))
    return pl.pallas_call(
        _slot_kernel,
        grid_spec=pltpu.PrefetchScalarGridSpec(num_scalar_prefetch=1, grid=(n // width,),
                                               in_specs=[block, block], out_specs=block),
        out_shape=jax.ShapeDtypeStruct((rows, n), I32),
        compiler_params=_params("arbitrary"),
        name="slots",
    )(starts, idx, rank)


SC_WINDOW = 128
SC_COLUMNS = 256


def _dispatch_sparsecore(pos, xn, n_rows):
    n, d = xn.shape
    windows = n // SC_WINDOW
    mesh = plsc.VectorSubcoreMesh(core_axis_name="core", subcore_axis_name="subcore")

    @functools.partial(pl.kernel, out_type=jax.ShapeDtypeStruct((n_rows, d), xn.dtype), mesh=mesh,
                       scratch_types=[], name="dispatch_sc")
    def scatter(x_hbm, pos_hbm, xs_hbm):
        for c in range(d // SC_COLUMNS):
            def body(x_vmem, pos_vmem, c=c):
                for k in range(TOP_K):
                    pltpu.sync_copy(x_vmem, xs_hbm.at[:, pl.ds(c * SC_COLUMNS, SC_COLUMNS)].at[pos_vmem.at[k]])

            pltpu.emit_pipeline(
                body, grid=(windows,),
                in_specs=[pl.BlockSpec((SC_WINDOW, SC_COLUMNS), index_map=lambda i, c=c: (i, c)),
                          pl.BlockSpec((TOP_K, SC_WINDOW), index_map=lambda i: (0, i))],
                out_specs=[], core_axis_name=("core", "subcore"),
                dimension_semantics=(pltpu.PARALLEL,))(x_hbm, pos_hbm)

    return scatter(xn, pos)


def _expert_kernel(tile_expert_ref, n_used_ref, next_expert_ref, slot_ref,
                   x_ref, w1_ref, b1_ref, w2_ref, b2_ref, y_ref,
                   w1f_ref, w2f_ref, sem_ref, w1b_ref, w2p_ref, w2b_ref):
    t = pl.program_id(0)
    fresh = (t == 0) | (tile_expert_ref[t] != tile_expert_ref[jnp.maximum(t - 1, 0)])
    used = t < n_used_ref[0]

    def weight_copies(e, s):
        return (pltpu.make_async_copy(w1_ref.at[e], w1f_ref.at[s], sem_ref.at[0, s]),
                pltpu.make_async_copy(w2_ref.at[e], w2f_ref.at[s], sem_ref.at[1, s]))

    @pl.when(t == 0)
    def _():
        for copy in weight_copies(tile_expert_ref[0], slot_ref[0]):
            copy.start()

    @pl.when(fresh & used)
    def _():
        s = slot_ref[t]
        for copy in weight_copies(tile_expert_ref[t], s):
            copy.wait()

        @pl.when(next_expert_ref[t] >= 0)
        def _():
            for copy in weight_copies(next_expert_ref[t], 1 - s):
                copy.start()

        rows = LANES
        half = LANES // 2
        for r in range(0, w1b_ref.shape[0], rows):
            w1b_ref[r:r + rows, :] = w1f_ref[s, r:r + rows, :].astype(BF16)
        for c in range(w2p_ref.shape[0]):
            lanes = slice(c * LANES, (c + 1) * LANES)
            for r in range(0, w2b_ref.shape[0], rows):
                for b in range(2):
                    w2p_ref[c, pl.ds(r + b, half, stride=2), :] = (
                        w2f_ref[s, r + b * half:r + (b + 1) * half, lanes])
            w2b_ref[:, lanes] = w2p_ref[c].astype(BF16)

    @pl.when(used)
    def _():
        xb = _unpack_rows(x_ref[...]).astype(BF16)
        h = jnp.dot(xb, w1b_ref[...], preferred_element_type=F32) + b1_ref[0]
        even = lax.broadcasted_iota(I32, (1, LANES), 1) % 2 == 0
        parts = []
        for q in range(h.shape[1] // (2 * LANES)):
            a = h[:, (2 * q) * LANES:(2 * q + 1) * LANES]
            b = h[:, (2 * q + 1) * LANES:(2 * q + 2) * LANES]
            hg = jnp.where(even, a, pltpu.roll(b, 1, 1))
            hl = jnp.where(even, pltpu.roll(a, LANES - 1, 1), b)
            glu = jnp.minimum(hg, SWIGLU_LIMIT)
            lin = jnp.clip(hl, -SWIGLU_LIMIT, SWIGLU_LIMIT)
            parts.append((glu * jax.nn.sigmoid(SWIGLU_ALPHA * glu) * (lin + 1.0)).astype(BF16))
        act = jnp.concatenate(parts, axis=1)
        y_ref[...] = _pack_rows(jnp.dot(act, w2b_ref[...], preferred_element_type=F32) + b2_ref[0])

    @pl.when(jnp.logical_not(used))
    def _():
        y_ref[...] = jnp.zeros_like(y_ref)


def _experts(tile_expert, n_used, next_expert, slot, xs, w1, b1, w2, b2):
    n_rows, d = xs.shape
    tm = EXPERT_TILE
    by_expert = lambda arr: pl.BlockSpec((1,) + arr.shape[1:], lambda i, te, *_: (te[i], 0, 0))
    rows = pl.BlockSpec((tm, d), lambda i, *_: (i, 0))
    used_rows = pl.BlockSpec((tm, d), lambda i, te, nu, *_: (jnp.minimum(i, nu[0] - 1), 0))
    hbm = pl.BlockSpec(memory_space=pl.ANY)
    return pl.pallas_call(
        _expert_kernel,
        grid_spec=pltpu.PrefetchScalarGridSpec(
            num_scalar_prefetch=4,
            grid=(n_rows // tm,),
            in_specs=[used_rows, hbm, by_expert(b1), hbm, by_expert(b2)],
            out_specs=rows,
            scratch_shapes=[pltpu.VMEM((2,) + w1.shape[1:], F32), pltpu.VMEM((2,) + w2.shape[1:], F32),
                            pltpu.SemaphoreType.DMA((2, 2)),
                            pltpu.VMEM(w1.shape[1:], BF16),
                            pltpu.VMEM((w2.shape[2] // LANES, w2.shape[1], LANES), F32),
                            pltpu.VMEM(w2.shape[1:], BF16)]),
        out_shape=jax.ShapeDtypeStruct(xs.shape, xs.dtype),
        compiler_params=pltpu.CompilerParams(dimension_semantics=("arbitrary",),
                                             vmem_limit_bytes=EXPERT_VMEM_LIMIT),
        name="experts",
    )(tile_expert, n_used, next_expert, slot, xs, w1, b1, w2, b2)


def _gather_sparsecore(pos, y):
    d = y.shape[1]
    total = pos.size
    mesh = plsc.VectorSubcoreMesh(core_axis_name="core", subcore_axis_name="subcore")

    @functools.partial(pl.kernel, out_type=jax.ShapeDtypeStruct((total, d), y.dtype), mesh=mesh,
                       scratch_types=[], name="gather_sc")
    def gather(y_hbm, pos_hbm, out_hbm):
        for c in range(d // SC_COLUMNS):
            def body(pos_vmem, out_vmem, c=c):
                pltpu.sync_copy(y_hbm.at[:, pl.ds(c * SC_COLUMNS, SC_COLUMNS)].at[pos_vmem.at[0]], out_vmem)

            pltpu.emit_pipeline(
                body, grid=(total // SC_WINDOW,),
                in_specs=[pl.BlockSpec((1, SC_WINDOW), index_map=lambda i: (0, i))],
                out_specs=[pl.BlockSpec((SC_WINDOW, SC_COLUMNS), index_map=lambda i, c=c: (i, c))],
                core_axis_name=("core", "subcore"),
                dimension_semantics=(pltpu.PARALLEL,))(pos_hbm, out_hbm)

    return gather(y, pos.reshape(1, -1))


def _combine_kernel(h_ref, gate_ref, p_ref, wg_ref, wp_ref, gple_ref, gfin_ref, *refs, final_norm):
    y_refs, o_ref = refs[:TOP_K], refs[TOP_K]
    ts = COMBINE_SUBTILE
    subs = [slice(i * ts, (i + 1) * ts) for i in range(h_ref.shape[0] // ts)]

    def mix(rows):
        gates = gate_ref[rows, :]
        moe = gates[:, 0:1] * _unpack_rows(y_refs[0][rows, :])
        for k in range(1, TOP_K):
            moe = moe + gates[:, k:k + 1] * _unpack_rows(y_refs[k][rows, :])
        h = h_ref[rows, :] + moe
        logit = jnp.dot(_rms(h, gple_ref[...]).astype(BF16), wg_ref[...], preferred_element_type=F32)
        emb = jnp.dot(p_ref[rows, :].astype(BF16), wp_ref[...], preferred_element_type=F32)
        return h, logit, emb

    mixed = [mix(rows) for rows in subs]
    for rows, (h, logit, emb) in zip(subs, mixed):
        h = h + jax.nn.sigmoid(logit) * emb
        o_ref[rows, :] = _rms(h, gfin_ref[...]) if final_norm else h


def _combine(h, gates, p, w_gate, w_proj, g_ple, g_final, y_tok, final_norm):
    n, d = h.shape
    tm = TOKEN_TILE
    tile = lambda width: pl.BlockSpec((tm, width), lambda i: (i, 0))
    full = lambda arr: pl.BlockSpec(arr.shape, lambda i: (0,) * arr.ndim)
    choice = lambda k: pl.BlockSpec((tm, y_tok.shape[1]), lambda i: (k * (n // tm) + i, 0))
    return pl.pallas_call(
        functools.partial(_combine_kernel, final_norm=final_norm),
        grid=(n // tm,),
        in_specs=[tile(d), tile(LANES), tile(p.shape[1]), full(w_gate), full(w_proj),
                  full(g_ple), full(g_final)] + [choice(k) for k in range(TOP_K)],
        out_specs=tile(d),
        out_shape=jax.ShapeDtypeStruct((n, d), F32),
        compiler_params=_params("arbitrary"),
        name="combine",
    )(h, gates, p, w_gate, w_proj, g_ple, g_final, *([y_tok] * TOP_K))


def _layer(h, p, positions, g_attn, w_in, lq1, lk1, lq2, lk2, g_subln, b_forget, w_out, g_moe,
           w_router, b_router, w_e1, b_e1, w_e2, b_e2, g_ple, w_ple_gate, w_ple_proj, g_final,
           lambda_init, final_norm):
    b, s, d = h.shape
    n = b * s
    row = lambda v: v.reshape(1, -1)

    w_main = w_in[:, :MAIN_WIDTH].astype(BF16)
    pad_heads = ((0, 2 * N_FOX_HEADS - N_FOX_HEADS), (0, 0))
    w_fz_t = jnp.pad(w_in[:, MAIN_WIDTH:].T, pad_heads).astype(BF16)
    b_fz_t = jnp.pad(b_forget.reshape(-1, 1), pad_heads)
    b1 = b_e1[:, None, :]
    b2 = b_e2[:, None, :]

    dq_t, dk, dv_t, fq_t, fk, fv_t, c, c_t = _inproj(positions, h, row(g_attn), w_main, w_fz_t, b_fz_t)
    c_rows = c_t.reshape(b, N_FOX_HEADS // 2, 2, s)
    d_out, f_out = _attention(dq_t, dk, dv_t, fq_t, fk, fv_t, c_rows, c, row(lq1), row(lk1), row(lq2),
                              row(lk2), g_subln.reshape(-1, 1), lambda_init)

    h1, xn, gates, idx, rank, counts = _router(
        d_out.reshape(n, DIFF_WIDTH), f_out.reshape(n, FOX_WIDTH), h.reshape(n, d),
        w_out.astype(BF16), row(g_moe), w_router.T, b_router.reshape(-1, 1))

    te = EXPERT_TILE
    n_rows = n * TOP_K + N_EXPERTS * te
    counts = counts[:, 0].astype(I32)
    padded = (counts + te - 1) // te * te
    ends = jnp.cumsum(padded)
    starts = ends - padded
    pos = _slots(starts.astype(I32), idx, rank)[:TOP_K]
    tile_start = jnp.arange(n_rows // te, dtype=I32) * te
    tile_expert = jnp.minimum(jnp.sum((ends[None, :] <= tile_start[:, None]).astype(I32), axis=1),
                              N_EXPERTS - 1)
    n_used = (ends[-1:] // te).astype(I32)
    owns = padded > 0
    later = lax.cummin(jnp.where(owns, jnp.arange(N_EXPERTS, dtype=I32), N_EXPERTS), reverse=True)
    following = jnp.concatenate([later[1:], jnp.full((1,), N_EXPERTS, I32)])
    following = jnp.where(following < N_EXPERTS, following, -1)
    owner = tile_expert[:, None] == jnp.arange(N_EXPERTS, dtype=I32)[None, :]
    next_expert = jnp.sum(jnp.where(owner, following[None, :], 0), axis=1)
    slot = jnp.sum(jnp.where(owner, ((jnp.cumsum(owns.astype(I32)) - 1) % 2)[None, :], 0), axis=1)

    xs = _dispatch_sparsecore(pos.astype(I32), xn, n_rows)
    y = _experts(tile_expert, n_used, next_expert.astype(I32), slot.astype(I32), xs, w_e1, b1, w_e2, b2)
    y_tok = _gather_sparsecore(pos.astype(I32), y)
    out = _combine(h1, gates, p.reshape(n, PLE_DIM), w_ple_gate.astype(BF16),
                   w_ple_proj.astype(BF16), row(g_ple), row(g_final), y_tok, final_norm)
    return out.reshape(b, s, d)


def kernel(x, p, positions, g_attn, w_in, lambda_q1, lambda_k1, lambda_q2, lambda_k2, g_subln, b_forget, w_out, g_moe, w_router, b_router, w_e1, b_e1, w_e2, b_e2, g_ple, w_ple_gate, w_ple_proj, g_final):
    depth = g_attn.shape[0]
    h = x
    for i in range(depth):
        lambda_init = 0.8 - 0.6 * math.exp(-0.3 * i)
        h = _layer(h, p[i], positions, g_attn[i], w_in[i], lambda_q1[i], lambda_k1[i],
                   lambda_q2[i], lambda_k2[i], g_subln[i], b_forget[i], w_out[i], g_moe[i],
                   w_router[i], b_router[i], w_e1[i], b_e1[i], w_e2[i], b_e2[i], g_ple[i],
                   w_ple_gate[i], w_ple_proj[i], g_final, lambda_init, i == depth - 1)
    return h
```

```python
import functools
import math

import jax
import jax.numpy as jnp
from jax import lax
from jax.experimental import pallas as pl
from jax.experimental.pallas import tpu as pltpu
from jax.experimental.pallas import tpu_sc as plsc

F32 = jnp.float32
BF16 = jnp.bfloat16
I32 = jnp.int32
U32 = jnp.uint32

LANES = 128
D_MODEL = 1024
HEAD_DIM = 64
N_DIFF_HEADS = 4
DIFF_V_DIM = 2 * HEAD_DIM
N_FOX_HEADS = 8
DIFF_WIDTH = N_DIFF_HEADS * DIFF_V_DIM
FOX_WIDTH = N_FOX_HEADS * HEAD_DIM
ROT_DIM = HEAD_DIM // 4
ROPE_THETA = 500000.0
N_EXPERTS = 32
TOP_K = 4
D_EXPERT = D_MODEL
SWIGLU_ALPHA = 1.702
SWIGLU_LIMIT = 7.0
PLE_DIM = 256
NORM_EPS = 1e-5
LOG2_E = math.log2(math.e)
MAIN_WIDTH = 3 * DIFF_WIDTH + 3 * FOX_WIDTH

SEQ_TILE = 1024
SEQ_SUBTILE = 256
ATTN_TILE = 256
ATTN_UNITS = 2
TOKEN_TILE = 1024
COMBINE_SUBTILE = 256
ROUTER_TILE = 1024
ROUTER_SUBTILE = 256
EXPERT_TILE = 256
VMEM_LIMIT = 48 * 1024 * 1024
EXPERT_VMEM_LIMIT = 58 * 1024 * 1024


def _params(*semantics):
    return pltpu.CompilerParams(dimension_semantics=semantics, vmem_limit_bytes=VMEM_LIMIT)


def _rms(x, g):
    return x * lax.rsqrt(jnp.mean(x * x, axis=-1, keepdims=True) + NORM_EPS) * g


def _pack_rows(v):
    half = v.shape[1] // 2
    bits = pltpu.bitcast(v.astype(BF16).astype(F32), U32)
    return (bits[:, :half] >> 16) | bits[:, half:]


def _unpack_rows(words):
    low = pltpu.bitcast(words << 16, F32)
    high = pltpu.bitcast(words & jnp.uint32(0xFFFF0000), F32)
    return jnp.concatenate([low, high], axis=1)


def _inproj_kernel(pos_ref, x_ref, g_ref, w_ref, wfzt_ref, bfzt_ref,
                   dqt_ref, dk_ref, dvt_ref, fqt_ref, fk_ref, fvt_ref, c_ref, ct_ref, carry_t_ref):
    ts = SEQ_SUBTILE
    subs = [slice(i * ts, (i + 1) * ts) for i in range(x_ref.shape[1] // ts)]
    xbs = [_rms(x_ref[0, rows, :], g_ref[...]).astype(BF16) for rows in subs]
    projs = [jnp.dot(xb, w_ref[...], preferred_element_type=F32) for xb in xbs]
    nt = (((1,), (1,)), ((), ()))
    tn = (((0,), (0,)), ((), ()))

    def pieces(v):
        hi = v.astype(BF16)
        r1 = v - hi.astype(F32)
        mid = r1.astype(BF16)
        return hi, mid, (r1 - mid.astype(F32)).astype(BF16)

    def rows_to_lanes(v, select):
        return sum(lax.dot_general(part, select, tn, preferred_element_type=F32) for part in pieces(v))

    half_rot = ROT_DIM // 2
    freq = lax.broadcasted_iota(I32, (half_rot, 1), 0)
    inv_freq = jnp.power(ROPE_THETA, -(2 * freq).astype(F32) / ROT_DIM)
    r = lax.broadcasted_iota(I32, (2 * half_rot, 2 * LANES), 0)
    lane2 = lax.broadcasted_iota(I32, (2 * half_rot, 2 * LANES), 1)
    d = lane2 % HEAD_DIM
    same_freq = jnp.where(d % half_rot == r % half_rot, 1.0, 0.0)
    cos_part = jnp.where((r < half_rot) & (lane2 < LANES) & (d < ROT_DIM), same_freq, 0.0)
    sin_part = jnp.where((r >= half_rot) & (lane2 >= LANES) & (d < ROT_DIM),
                         jnp.where(d < half_rot, -same_freq, same_freq), 0.0)
    expand = (cos_part + sin_part).astype(BF16)
    lane = lax.broadcasted_iota(I32, (1, LANES), 1) % HEAD_DIM
    first_half = lane < half_rot

    def rope_tables(rows):
        ang = inv_freq * pos_ref[0, :, rows].astype(F32)
        trig = jnp.concatenate([jnp.cos(ang), jnp.sin(ang)], axis=0)
        tables = rows_to_lanes(trig, expand)
        return jnp.where(lane < ROT_DIM, tables[:, :LANES], 1.0), tables[:, LANES:]

    def rope(t, cosf, sinf):
        outs = []
        for j in range(t.shape[1] // LANES):
            tj = t[:, j * LANES:(j + 1) * LANES]
            partner = jnp.where(first_half,
                                pltpu.roll(tj, LANES - ROT_DIM // 2, 1),
                                pltpu.roll(tj, ROT_DIM // 2, 1))
            outs.append(tj * cosf + partner * sinf)
        return jnp.concatenate(outs, axis=1)

    scale = HEAD_DIM ** -0.5 * LOG2_E
    w = DIFF_WIDTH
    for i, (rows, proj) in enumerate(zip(subs, projs)):
        cosf, sinf = rope_tables(rows)
        dqt_ref[0, :, rows] = (rope(proj[:, 0:w], cosf, sinf) * scale).astype(BF16).T
        dk_ref[0, rows, :] = rope(proj[:, w:2 * w], cosf, sinf).astype(BF16)
        dvt_ref[0, i] = proj[:, 2 * w:3 * w].astype(BF16).T
        fqt_ref[0, :, rows] = (proj[:, 3 * w:4 * w] * scale).astype(BF16).T
        fk_ref[0, rows, :] = proj[:, 4 * w:5 * w].astype(BF16)
        fvt_ref[0, i] = proj[:, 5 * w:6 * w].astype(BF16).T

    def log_sigmoid(z):
        return jnp.minimum(z, 0.0) - jnp.log1p(jnp.exp(-jnp.abs(z)))

    row = lax.broadcasted_iota(I32, (ts, ts), 0)
    col = lax.broadcasted_iota(I32, (ts, ts), 1)
    upto_col = jnp.where(row <= col, 1.0, 0.0).astype(BF16)
    heads = wfzt_ref.shape[0]
    head = lax.broadcasted_iota(I32, (heads, FOX_WIDTH), 0)
    group = lax.broadcasted_iota(I32, (heads, FOX_WIDTH), 1) // HEAD_DIM
    head_lanes = jnp.where(head == group, 1.0, 0.0).astype(BF16)

    @pl.when(pl.program_id(1) == 0)
    def _():
        carry_t_ref[...] = jnp.zeros_like(carry_t_ref)

    for rows, xb in zip(subs, xbs):
        fzt = lax.dot_general(wfzt_ref[...], xb, nt, preferred_element_type=F32) + bfzt_ref[...]
        ct = carry_t_ref[...] + sum(jnp.dot(part, upto_col, preferred_element_type=F32)
                                    for part in pieces(log_sigmoid(fzt)))
        carry_t_ref[...] = ct[:, ts - 1:ts]
        ct = ct * LOG2_E
        ct_ref[0, :, rows] = ct[0:N_FOX_HEADS, :]
        c_ref[0, rows, :] = rows_to_lanes(ct, head_lanes)


def _inproj(positions, x, g, w_main, w_fz_t, b_fz_t):
    b, s, d = x.shape
    ts = SEQ_TILE
    sub = SEQ_SUBTILE
    assert sub == ATTN_TILE
    tile = lambda width: pl.BlockSpec((1, ts, width), lambda i, j: (i, j, 0))
    tile_t = pl.BlockSpec((1, DIFF_WIDTH, ts), lambda i, j: (i, 0, j))
    key_tile_t = pl.BlockSpec((1, ts // sub, DIFF_WIDTH, sub), lambda i, j: (i, j, 0, 0))
    full = lambda arr: pl.BlockSpec(arr.shape, lambda i, j: (0,) * arr.ndim)
    rows = jax.ShapeDtypeStruct((b, s, DIFF_WIDTH), BF16)
    rows_t = jax.ShapeDtypeStruct((b, DIFF_WIDTH, s), BF16)
    key_tiles_t = jax.ShapeDtypeStruct((b, s // sub, DIFF_WIDTH, sub), BF16)
    return pl.pallas_call(
        _inproj_kernel,
        grid=(b, s // ts),
        in_specs=[pl.BlockSpec((1, 1, ts), lambda i, j: (i, 0, j)), tile(d), full(g), full(w_main),
                  full(w_fz_t), full(b_fz_t)],
        out_specs=[tile_t, tile(DIFF_WIDTH), key_tile_t, tile_t, tile(FOX_WIDTH), key_tile_t,
                   tile(FOX_WIDTH), pl.BlockSpec((1, N_FOX_HEADS, ts), lambda i, j: (i, 0, j))],
        out_shape=[rows_t, rows, key_tiles_t, rows_t, rows, key_tiles_t,
                   jax.ShapeDtypeStruct((b, s, FOX_WIDTH), F32),
                   jax.ShapeDtypeStruct((b, N_FOX_HEADS, s), F32)],
        scratch_shapes=[pltpu.VMEM((w_fz_t.shape[0], 1), F32)],
        compiler_params=_params("arbitrary", "arbitrary"),
        name="inproj",
    )(positions.reshape(b, 1, s), x, g, w_main, w_fz_t, b_fz_t)


ONES_ROWS = 16


def _block_diag_queries(qt):
    row = lax.broadcasted_iota(I32, qt.shape, 0)
    zero = jnp.zeros_like(qt)
    return jnp.concatenate([jnp.where(row < HEAD_DIM, qt, zero),
                            jnp.where(row >= HEAD_DIM, qt, zero)], axis=1)


class _Chain:
    def __init__(self, qbd, keys, values_t, scratch, key_bias=None, query_bias=None):
        self.qbd, self.keys, self.values_t = qbd, keys, values_t
        self.acc_ref, self.s_ref, self.p_ref, self.stat_ref = scratch
        self.key_bias, self.query_bias = key_bias, query_bias


_RUNNING_MAX, _TILE_MAX, _RESCALE = 0, 1, 2


def _causal_attention(chains, qi):
    tk, tq = chains[0].s_ref.shape[0], chains[0].s_ref.shape[1] // 2
    ones = jnp.ones((ONES_ROWS, tk), BF16)

    def scores(ch, i, diagonal):
        s = jnp.dot(ch.keys(i), ch.qbd, preferred_element_type=F32)
        if ch.key_bias is not None:
            s = s - ch.key_bias(i)
        if diagonal is not False:
            r = lax.broadcasted_iota(I32, s.shape, 0)
            c = lax.broadcasted_iota(I32, s.shape, 1) & (tq - 1)
            visible = r <= c if diagonal is True else jnp.logical_or(r <= c, jnp.logical_not(diagonal))
            s = jnp.where(visible, s, -jnp.inf)
        ch.s_ref[...] = s
        top = jnp.max(s, axis=0, keepdims=True)
        if ch.query_bias is not None:
            top = top + ch.query_bias
        ch.stat_ref[_TILE_MAX:_TILE_MAX + 1, :] = top

    def probs(ch):
        m = ch.stat_ref[_RUNNING_MAX:_RUNNING_MAX + 1, :]
        m_new = jnp.maximum(m, ch.stat_ref[_TILE_MAX:_TILE_MAX + 1, :])
        ch.stat_ref[_RUNNING_MAX:_RUNNING_MAX + 1, :] = m_new
        ch.stat_ref[_RESCALE:_RESCALE + 1, :] = jnp.exp2(m - m_new)
        ref = m_new if ch.query_bias is None else m_new - ch.query_bias
        ch.p_ref[...] = jnp.exp2(ch.s_ref[...] - ref).astype(BF16)

    def accumulate(ch, i):
        vts = jnp.concatenate([ch.values_t(i), ones], axis=0)
        ch.acc_ref[...] = (ch.stat_ref[_RESCALE:_RESCALE + 1, :] * ch.acc_ref[...]
                           + jnp.dot(vts, ch.p_ref[...], preferred_element_type=F32))

    def body(k, diagonal_next):
        for ch in chains:
            accumulate(ch, jnp.maximum(k - 1, 0))
        for ch in chains:
            probs(ch)
        for ch in chains:
            scores(ch, k + 1, diagonal_next)

    for ch in chains:
        ch.acc_ref[...] = jnp.zeros_like(ch.acc_ref)
        ch.p_ref[...] = jnp.zeros_like(ch.p_ref)
        ch.stat_ref[...] = jnp.concatenate(
            [jnp.full((1, 2 * tq), -jnp.inf, F32), jnp.ones((ch.stat_ref.shape[0] - 1, 2 * tq), F32)], axis=0)
        scores(ch, 0, qi == 0)

    def loop_body(k, carry):
        body(k, False)
        return carry

    lax.fori_loop(0, qi - 1, loop_body, 0)

    @pl.when(qi >= 1)
    def _():
        body(qi - 1, True)

    for ch in chains:
        accumulate(ch, jnp.maximum(qi - 1, 0))
    for ch in chains:
        probs(ch)
    for ch in chains:
        accumulate(ch, qi)


def _attention_kernel(lq1_ref, lk1_ref, lq2_ref, lk2_ref, g_ref,
                      dqt_ref, dk_ref, dvt_ref, fqt_ref, fk_ref, fvt_ref, cq_ref, ck_ref,
                      do_ref, fo_ref, *scratch, lambda_init):
    tq = dqt_ref.shape[2]
    tk = dvt_ref.shape[3]
    units = dqt_ref.shape[1] // LANES
    lane = lax.broadcasted_iota(I32, (1, LANES), 1)
    scratch = [scratch[i:i + 4] for i in range(0, len(scratch), 4)]

    def unit_chains(u):
        lanes = slice(u * LANES, (u + 1) * LANES)

        def tile(ref):
            return lambda i: ref[0, pl.ds(pl.multiple_of(i * tk, tk), tk), lanes]

        def tile_t(ref):
            return lambda i: ref[0, i, lanes, :]

        cq = jnp.concatenate([cq_ref[0, u, 0:1, :], cq_ref[0, u, 1:2, :]], axis=1)

        def key_bias(i):
            ck = tile(ck_ref)(i)
            other = pltpu.roll(ck, HEAD_DIM, 1)
            first = jnp.where(lane < HEAD_DIM, ck, other)
            second = jnp.where(lane >= HEAD_DIM, ck, other)
            return jnp.concatenate([first] * (tq // LANES) + [second] * (tq // LANES), axis=1)

        return [_Chain(_block_diag_queries(dqt_ref[0, lanes, :]), tile(dk_ref), tile_t(dvt_ref),
                       scratch[2 * u]),
                _Chain(_block_diag_queries(fqt_ref[0, lanes, :]), tile(fk_ref), tile_t(fvt_ref),
                       scratch[2 * u + 1], key_bias, cq)]

    chains = [ch for u in range(units) for ch in unit_chains(u)]
    _causal_attention(chains, pl.program_id(2))

    lam = (jnp.exp(jnp.sum(lq1_ref[...] * lk1_ref[...], axis=-1, keepdims=True))
           - jnp.exp(jnp.sum(lq2_ref[...] * lk2_ref[...], axis=-1, keepdims=True)) + lambda_init)
    dv, h = DIFF_V_DIM, HEAD_DIM
    for u in range(units):
        lanes = slice(u * LANES, (u + 1) * LANES)
        acc = chains[2 * u].acc_ref
        o = (acc[0:dv, 0:tq] / acc[dv:dv + 1, 0:tq]
             - lam * (acc[0:dv, tq:] / acc[dv:dv + 1, tq:]))
        ms = jnp.mean(o * o, axis=0, keepdims=True)
        do_ref[0, :, lanes] = (o * lax.rsqrt(ms + NORM_EPS) * g_ref[...]
                               * (1.0 - lambda_init)).astype(BF16).T
        acc = chains[2 * u + 1].acc_ref
        fo_ref[0, :, lanes] = jnp.concatenate([acc[0:h, 0:tq] / acc[2 * h:2 * h + 1, 0:tq],
                                               acc[h:2 * h, tq:] / acc[2 * h:2 * h + 1, tq:]],
                                              axis=0).astype(BF16).T


def _attention(dqt, dk, dvt, fqt, fk, fvt, c_rows, c_cols, lq1, lk1, lq2, lk2, g_col, lambda_init):
    b, s, _ = dk.shape
    t = ATTN_TILE
    assert N_DIFF_HEADS == N_FOX_HEADS // 2 and N_DIFF_HEADS % ATTN_UNITS == 0
    width = ATTN_UNITS * LANES
    vec = lambda arr: pl.BlockSpec(arr.shape, lambda i, h, j: (0, 0))
    q_spec = pl.BlockSpec((1, width, t), lambda i, h, j: (i, h, j))
    k_spec = pl.BlockSpec((1, s, width), lambda i, h, j: (i, 0, h))
    v_spec = pl.BlockSpec((1, s // t, width, t), lambda i, h, j: (i, 0, h, 0))
    o_spec = pl.BlockSpec((1, t, width), lambda i, h, j: (i, j, h))
    out = jax.ShapeDtypeStruct((b, s, DIFF_WIDTH), BF16)
    chain_scratch = [pltpu.VMEM((LANES + ONES_ROWS, 2 * t), F32),
                     pltpu.VMEM((t, 2 * t), F32),
                     pltpu.VMEM((t, 2 * t), BF16),
                     pltpu.VMEM((8, 2 * t), F32)]
    return pl.pallas_call(
        functools.partial(_attention_kernel, lambda_init=lambda_init),
        grid=(b, N_DIFF_HEADS // ATTN_UNITS, s // t),
        in_specs=[vec(lq1), vec(lk1), vec(lq2), vec(lk2), vec(g_col),
                  q_spec, k_spec, v_spec, q_spec, k_spec, v_spec,
                  pl.BlockSpec((1, ATTN_UNITS, 2, t), lambda i, h, j: (i, h, 0, j)), k_spec],
        out_specs=[o_spec, o_spec],
        out_shape=[out, out],
        scratch_shapes=chain_scratch * (2 * ATTN_UNITS),
        compiler_params=_params("arbitrary", "arbitrary", "arbitrary"),
        name="attention",
    )(lq1, lk1, lq2, lk2, g_col, dqt, dk, dvt, fqt, fk, fvt, c_rows, c_cols)


def _router_kernel(d_ref, f_ref, x_ref, wo_ref, g_ref, wrt_ref, brt_ref,
                   h_ref, xn_ref, gate_ref, idx_ref, rank_ref, count_ref, carry_ref):
    tm = ROUTER_SUBTILE
    nt = (((1,), (1,)), ((), ()))

    def pieces(v, count):
        out = []
        for _ in range(count):
            part = v.astype(BF16)
            out.append(part)
            v = v - part.astype(F32)
        return out

    wh, wl = pieces(wrt_ref[...], 2)
    row = lax.broadcasted_iota(I32, (tm, tm), 0)
    col = lax.broadcasted_iota(I32, (tm, tm), 1)
    earlier = jnp.where(row < col, 1.0, 0.0).astype(BF16)
    eye = jnp.where(row == col, 1.0, 0.0).astype(BF16)

    @pl.when(pl.program_id(0) == 0)
    def _():
        carry_ref[...] = jnp.zeros_like(carry_ref)

    subs = [slice(i * tm, (i + 1) * tm) for i in range(x_ref.shape[0] // tm)]

    def logits(rows):
        mixed = (jnp.dot(d_ref[rows, :], wo_ref[0:DIFF_WIDTH, :], preferred_element_type=F32)
                 + jnp.dot(f_ref[rows, :], wo_ref[DIFF_WIDTH:, :], preferred_element_type=F32))
        h = x_ref[rows, :] + mixed
        h_ref[rows, :] = h
        xn = _rms(h, g_ref[...])
        xn_ref[rows, :] = _pack_rows(xn)
        xh, xl = pieces(xn, 2)
        return (lax.dot_general(wh, xh, nt, preferred_element_type=F32)
                + lax.dot_general(wh, xl, nt, preferred_element_type=F32)
                + lax.dot_general(wl, xh, nt, preferred_element_type=F32)) + brt_ref[...]

    def top_k(work):
        expert = lax.broadcasted_iota(I32, work.shape, 0)
        vals, ids, picks = [], [], []
        for _ in range(TOP_K):
            m = jnp.max(work, axis=0, keepdims=True)
            idx = jnp.min(jnp.where(work == m, expert, N_EXPERTS), axis=0, keepdims=True)
            pick = expert == idx
            vals.append(m)
            ids.append(idx)
            picks.append(pick)
            work = jnp.where(pick, -jnp.inf, work)
        exps = [jnp.exp(v - vals[0]) for v in vals]
        denom = exps[0] + exps[1] + exps[2] + exps[3]
        return ids, picks, [e / denom for e in exps]

    def ranks(picks):
        chosen = jnp.where(picks[0] | picks[1] | picks[2] | picks[3], 1.0, 0.0)
        rank = jnp.dot(chosen.astype(BF16), earlier, preferred_element_type=F32) + carry_ref[...]
        carry_ref[...] = carry_ref[...] + jnp.sum(chosen, axis=1, keepdims=True)
        return rank

    def emit(rows, ids, picks, gates, rank):
        slot = lax.broadcasted_iota(I32, (8, tm), 0)
        gate_t = jnp.zeros((8, tm), F32)
        idx_t = jnp.zeros((8, tm), I32)
        rank_t = jnp.zeros((8, tm), F32)
        for k in range(TOP_K):
            gate_t = jnp.where(slot == k, gates[k], gate_t)
            idx_t = jnp.where(slot == k, ids[k], idx_t)
            rank_t = jnp.where(slot == k, jnp.sum(jnp.where(picks[k], rank, 0.0), axis=0, keepdims=True),
                               rank_t)
        idx_ref[:, rows] = idx_t
        rank_ref[:, rows] = rank_t.astype(I32)
        gate_rows = jnp.concatenate([gate_t, jnp.zeros((LANES - 8, tm), F32)], axis=0)
        gate_ref[rows, :] = sum(lax.dot_general(eye, part, nt, preferred_element_type=F32)
                                for part in pieces(gate_rows, 3))

    work = [logits(rows) for rows in subs]
    chosen = [top_k(w) for w in work]
    rank = [ranks(picks) for _, picks, _ in chosen]
    for rows, (ids, picks, gates), r in zip(subs, chosen, rank):
        emit(rows, ids, picks, gates, r)
    count_ref[...] = jnp.broadcast_to(carry_ref[...], count_ref.shape)


def _router(d_out, f_out, x, w_out, g_moe, w_router_t, b_router_t):
    n, d = x.shape
    tm = ROUTER_TILE
    tile = lambda width: pl.BlockSpec((tm, width), lambda i: (i, 0))
    rows = pl.BlockSpec((8, tm), lambda i: (0, i))
    full = lambda arr: pl.BlockSpec(arr.shape, lambda i: (0,) * arr.ndim)
    return pl.pallas_call(
        _router_kernel,
        grid=(n // tm,),
        in_specs=[tile(DIFF_WIDTH), tile(FOX_WIDTH), tile(d), full(w_out), full(g_moe),
                  full(w_router_t), full(b_router_t)],
        out_specs=[tile(d), tile(d // 2), tile(LANES), rows, rows,
                   pl.BlockSpec((N_EXPERTS, LANES), lambda i: (0, 0))],
        out_shape=[jax.ShapeDtypeStruct((n, d), F32), jax.ShapeDtypeStruct((n, d // 2), U32),
                   jax.ShapeDtypeStruct((n, LANES), F32), jax.ShapeDtypeStruct((8, n), I32),
                   jax.ShapeDtypeStruct((8, n), I32), jax.ShapeDtypeStruct((N_EXPERTS, LANES), F32)],
        scratch_shapes=[pltpu.VMEM((N_EXPERTS, 1), F32)],
        compiler_params=_params("arbitrary"),
        name="router",
    )(d_out, f_out, x, w_out, g_moe, w_router_t, b_router_t)


def _slot_kernel(starts_ref, idx_ref, rank_ref, pos_ref):
    idx = idx_ref[...]
    pos = rank_ref[...]
    for e in range(N_EXPERTS):
        pos = jnp.where(idx == e, pos + starts_ref[e], pos)
    pos_ref[...] = pos


def _slots(starts, idx, rank):
    rows, n = idx.shape
    width = min(n, 2048)
    block = pl.BlockSpec((rows, width), lambda i, starts: (0, i))
    return pl.pallas_call(
        _slot_kernel,
        grid_spec=pltpu.PrefetchScalarGridSpec(num_scalar_prefetch=1, grid=(n // width,),
                                               in_specs=[block, block], out_specs=block),
        out_shape=jax.ShapeDtypeStruct((rows, n), I32),
        compiler_params=_params("arbitrary"),
        name="slots",
    )(starts, idx, rank)


SC_WINDOW = 128
SC_COLUMNS = 256


def _dispatch_sparsecore(pos, pad_rows, xn, n_rows):
    n, d = xn.shape
    mesh = plsc.VectorSubcoreMesh(core_axis_name="core", subcore_axis_name="subcore")
    spread = dict(core_axis_name=("core", "subcore"), dimension_semantics=(pltpu.PARALLEL,))

    @functools.partial(pl.kernel, out_type=jax.ShapeDtypeStruct((n_rows, d), xn.dtype), mesh=mesh,
                       scratch_types=[], name="dispatch_sc")
    def scatter(x_hbm, pos_hbm, zero_hbm, pad_hbm, xs_hbm):
        for c in range(d // SC_COLUMNS):
            columns = xs_hbm.at[:, pl.ds(c * SC_COLUMNS, SC_COLUMNS)]

            def body(x_vmem, pos_vmem, columns=columns):
                for k in range(TOP_K):
                    pltpu.sync_copy(x_vmem, columns.at[pos_vmem.at[k]])

            pltpu.emit_pipeline(
                body, grid=(n // SC_WINDOW,),
                in_specs=[pl.BlockSpec((SC_WINDOW, SC_COLUMNS), index_map=lambda i, c=c: (i, c)),
                          pl.BlockSpec((TOP_K, SC_WINDOW), index_map=lambda i: (0, i))],
                out_specs=[], **spread)(x_hbm, pos_hbm)

            def pad_body(zero_vmem, pad_vmem, columns=columns):
                pltpu.sync_copy(zero_vmem, columns.at[pad_vmem.at[0]])

            pltpu.emit_pipeline(
                pad_body, grid=(pad_rows.size // SC_WINDOW,),
                in_specs=[pl.BlockSpec((SC_WINDOW, SC_COLUMNS), index_map=lambda i: (0, 0)),
                          pl.BlockSpec((1, SC_WINDOW), index_map=lambda i: (0, i))],
                out_specs=[], **spread)(zero_hbm, pad_hbm)

    return scatter(xn, pos, jnp.zeros((SC_WINDOW, SC_COLUMNS), xn.dtype), pad_rows.reshape(1, -1))


def _expert_kernel(tile_expert_ref, n_used_ref, next_expert_ref, slot_ref,
                   x_ref, w1_ref, b1_ref, w2_ref, b2_ref, y_ref,
                   w1f_ref, w2f_ref, sem_ref, w1b_ref, w2p_ref, w2b_ref):
    t = pl.program_id(0)
    fresh = (t == 0) | (tile_expert_ref[t] != tile_expert_ref[jnp.maximum(t - 1, 0)])
    used = t < n_used_ref[0]

    def weight_copies(e, s):
        return (pltpu.make_async_copy(w1_ref.at[e], w1f_ref.at[s], sem_ref.at[0, s]),
                pltpu.make_async_copy(w2_ref.at[e], w2f_ref.at[s], sem_ref.at[1, s]))

    @pl.when(t == 0)
    def _():
        for copy in weight_copies(tile_expert_ref[0], slot_ref[0]):
            copy.start()

    @pl.when(fresh & used)
    def _():
        s = slot_ref[t]
        for copy in weight_copies(tile_expert_ref[t], s):
            copy.wait()

        @pl.when(next_expert_ref[t] >= 0)
        def _():
            for copy in weight_copies(next_expert_ref[t], 1 - s):
                copy.start()

        rows = LANES
        half = LANES // 2
        for r in range(0, w1b_ref.shape[0], rows):
            w1b_ref[r:r + rows, :] = w1f_ref[s, r:r + rows, :].astype(BF16)
        for c in range(w2p_ref.shape[0]):
            lanes = slice(c * LANES, (c + 1) * LANES)
            for r in range(0, w2b_ref.shape[0], rows):
                for b in range(2):
                    w2p_ref[c, pl.ds(r + b, half, stride=2), :] = (
                        w2f_ref[s, r + b * half:r + (b + 1) * half, lanes])
            w2b_ref[:, lanes] = w2p_ref[c].astype(BF16)

    @pl.when(used)
    def _():
        xb = _unpack_rows(x_ref[...]).astype(BF16)
        h = jnp.dot(xb, w1b_ref[...], preferred_element_type=F32) + b1_ref[0]
        even = lax.broadcasted_iota(I32, (1, LANES), 1) % 2 == 0
        parts = []
        for q in range(h.shape[1] // (2 * LANES)):
            a = h[:, (2 * q) * LANES:(2 * q + 1) * LANES]
            b = h[:, (2 * q + 1) * LANES:(2 * q + 2) * LANES]
            hg = jnp.where(even, a, pltpu.roll(b, 1, 1))
            hl = jnp.where(even, pltpu.roll(a, LANES - 1, 1), b)
            glu = jnp.minimum(hg, SWIGLU_LIMIT)
            lin = jnp.clip(hl, -SWIGLU_LIMIT, SWIGLU_LIMIT)
            parts.append((glu * jax.nn.sigmoid(SWIGLU_ALPHA * glu) * (lin + 1.0)).astype(BF16))
        act = jnp.concatenate(parts, axis=1)
        y_ref[...] = _pack_rows(jnp.dot(act, w2b_ref[...], preferred_element_type=F32) + b2_ref[0])

    @pl.when(jnp.logical_not(used))
    def _():
        y_ref[...] = jnp.zeros_like(y_ref)


def _experts(tile_expert, n_used, next_expert, slot, xs, w1, b1, w2, b2):
    n_rows, d = xs.shape
    tm = EXPERT_TILE
    by_expert = lambda arr: pl.BlockSpec((1,) + arr.shape[1:], lambda i, te, *_: (te[i], 0, 0))
    rows = pl.BlockSpec((tm, d), lambda i, *_: (i, 0))
    used_rows = pl.BlockSpec((tm, d), lambda i, te, nu, *_: (jnp.minimum(i, nu[0] - 1), 0))
    hbm = pl.BlockSpec(memory_space=pl.ANY)
    return pl.pallas_call(
        _expert_kernel,
        grid_spec=pltpu.PrefetchScalarGridSpec(
            num_scalar_prefetch=4,
            grid=(n_rows // tm,),
            in_specs=[used_rows, hbm, by_expert(b1), hbm, by_expert(b2)],
            out_specs=rows,
            scratch_shapes=[pltpu.VMEM((2,) + w1.shape[1:], F32), pltpu.VMEM((2,) + w2.shape[1:], F32),
                            pltpu.SemaphoreType.DMA((2, 2)),
                            pltpu.VMEM(w1.shape[1:], BF16),
                            pltpu.VMEM((w2.shape[2] // LANES, w2.shape[1], LANES), F32),
                            pltpu.VMEM(w2.shape[1:], BF16)]),
        out_shape=jax.ShapeDtypeStruct(xs.shape, xs.dtype),
        compiler_params=pltpu.CompilerParams(dimension_semantics=("arbitrary",),
                                             vmem_limit_bytes=EXPERT_VMEM_LIMIT),
        name="experts",
    )(tile_expert, n_used, next_expert, slot, xs, w1, b1, w2, b2)


def _gather_sparsecore(pos, y):
    d = y.shape[1]
    total = pos.size
    mesh = plsc.VectorSubcoreMesh(core_axis_name="core", subcore_axis_name="subcore")

    @functools.partial(pl.kernel, out_type=jax.ShapeDtypeStruct((total, d), y.dtype), mesh=mesh,
                       scratch_types=[], name="gather_sc")
    def gather(y_hbm, pos_hbm, out_hbm):
        for c in range(d // SC_COLUMNS):
            def body(pos_vmem, out_vmem, c=c):
                pltpu.sync_copy(y_hbm.at[:, pl.ds(c * SC_COLUMNS, SC_COLUMNS)].at[pos_vmem.at[0]], out_vmem)

            pltpu.emit_pipeline(
                body, grid=(total // SC_WINDOW,),
                in_specs=[pl.BlockSpec((1, SC_WINDOW), index_map=lambda i: (0, i))],
                out_specs=[pl.BlockSpec((SC_WINDOW, SC_COLUMNS), index_map=lambda i, c=c: (i, c))],
                core_axis_name=("core", "subcore"),
                dimension_semantics=(pltpu.PARALLEL,))(pos_hbm, out_hbm)

    return gather(y, pos.reshape(1, -1))


def _combine_kernel(h_ref, gate_ref, p_ref, wg_ref, wp_ref, gple_ref, gfin_ref, *refs, final_norm):
    y_refs, o_ref = refs[:TOP_K], refs[TOP_K]
    ts = COMBINE_SUBTILE
    subs = [slice(i * ts, (i + 1) * ts) for i in range(h_ref.shape[0] // ts)]

    def mix(rows):
        gates = gate_ref[rows, :]
        moe = gates[:, 0:1] * _unpack_rows(y_refs[0][rows, :])
        for k in range(1, TOP_K):
            moe = moe + gates[:, k:k + 1] * _unpack_rows(y_refs[k][rows, :])
        h = h_ref[rows, :] + moe
        logit = jnp.dot(_rms(h, gple_ref[...]).astype(BF16), wg_ref[...], preferred_element_type=F32)
        emb = jnp.dot(p_ref[rows, :].astype(BF16), wp_ref[...], preferred_element_type=F32)
        return h, logit, emb

    mixed = [mix(rows) for rows in subs]
    for rows, (h, logit, emb) in zip(subs, mixed):
        h = h + jax.nn.sigmoid(logit) * emb
        o_ref[rows, :] = _rms(h, gfin_ref[...]) if final_norm else h


def _combine(h, gates, p, w_gate, w_proj, g_ple, g_final, y_tok, final_norm):
    n, d = h.shape
    tm = TOKEN_TILE
    tile = lambda width: pl.BlockSpec((tm, width), lambda i: (i, 0))
    full = lambda arr: pl.BlockSpec(arr.shape, lambda i: (0,) * arr.ndim)
    choice = lambda k: pl.BlockSpec((tm, y_tok.shape[1]), lambda i: (k * (n // tm) + i, 0))
    return pl.pallas_call(
        functools.partial(_combine_kernel, final_norm=final_norm),
        grid=(n // tm,),
        in_specs=[tile(d), tile(LANES), tile(p.shape[1]), full(w_gate), full(w_proj),
                  full(g_ple), full(g_final)] + [choice(k) for k in range(TOP_K)],
        out_specs=tile(d),
        out_shape=jax.ShapeDtypeStruct((n, d), F32),
        compiler_params=_params("arbitrary"),
        name="combine",
    )(h, gates, p, w_gate, w_proj, g_ple, g_final, *([y_tok] * TOP_K))


def _layer(h, p, positions, g_attn, w_in, lq1, lk1, lq2, lk2, g_subln, b_forget, w_out, g_moe,
           w_router, b_router, w_e1, b_e1, w_e2, b_e2, g_ple, w_ple_gate, w_ple_proj, g_final,
           lambda_init, final_norm):
    b, s, d = h.shape
    n = b * s
    row = lambda v: v.reshape(1, -1)

    w_main = w_in[:, :MAIN_WIDTH].astype(BF16)
    pad_heads = ((0, 2 * N_FOX_HEADS - N_FOX_HEADS), (0, 0))
    w_fz_t = jnp.pad(w_in[:, MAIN_WIDTH:].T, pad_heads).astype(BF16)
    b_fz_t = jnp.pad(b_forget.reshape(-1, 1), pad_heads)
    b1 = b_e1[:, None, :]
    b2 = b_e2[:, None, :]

    dq_t, dk, dv_t, fq_t, fk, fv_t, c, c_t = _inproj(positions, h, row(g_attn), w_main, w_fz_t, b_fz_t)
    c_rows = c_t.reshape(b, N_FOX_HEADS // 2, 2, s)
    d_out, f_out = _attention(dq_t, dk, dv_t, fq_t, fk, fv_t, c_rows, c, row(lq1), row(lk1), row(lq2),
                              row(lk2), g_subln.reshape(-1, 1), lambda_init)

    h1, xn, gates, idx, rank, counts = _router(
        d_out.reshape(n, DIFF_WIDTH), f_out.reshape(n, FOX_WIDTH), h.reshape(n, d),
        w_out.astype(BF16), row(g_moe), w_router.T, b_router.reshape(-1, 1))

    te = EXPERT_TILE
    n_rows = n * TOP_K + N_EXPERTS * te
    counts = counts[:, 0].astype(I32)
    padded = (counts + te - 1) // te * te
    ends = jnp.cumsum(padded)
    starts = ends - padded
    pos = _slots(starts.astype(I32), idx, rank)[:TOP_K]
    tile_start = jnp.arange(n_rows // te, dtype=I32) * te
    tile_expert = jnp.minimum(jnp.sum((ends[None, :] <= tile_start[:, None]).astype(I32), axis=1),
                              N_EXPERTS - 1)
    n_used = (ends[-1:] // te).astype(I32)
    owns = padded > 0
    later = lax.cummin(jnp.where(owns, jnp.arange(N_EXPERTS, dtype=I32), N_EXPERTS), reverse=True)
    following = jnp.concatenate([later[1:], jnp.full((1,), N_EXPERTS, I32)])
    following = jnp.where(following < N_EXPERTS, following, -1)
    owner = tile_expert[:, None] == jnp.arange(N_EXPERTS, dtype=I32)[None, :]
    next_expert = jnp.sum(jnp.where(owner, following[None, :], 0), axis=1)
    slot = jnp.sum(jnp.where(owner, ((jnp.cumsum(owns.astype(I32)) - 1) % 2)[None, :], 0), axis=1)

    pad_count = padded - counts
    pad_before = jnp.cumsum(pad_count) - pad_count
    slot_id = jnp.arange(N_EXPERTS * te, dtype=I32)
    in_or_before = slot_id[:, None] >= pad_before[None, :]
    expert_of = jnp.sum(in_or_before.astype(I32), axis=1) - 1
    mine = expert_of[:, None] == jnp.arange(N_EXPERTS, dtype=I32)[None, :]
    pad_rows = jnp.sum(jnp.where(mine, (starts + counts - pad_before)[None, :], 0), axis=1) + slot_id
    pad_rows = jnp.where(slot_id < jnp.sum(pad_count), pad_rows, n_rows - 1)
    xs = _dispatch_sparsecore(pos.astype(I32), pad_rows.astype(I32), xn, n_rows)
    y = _experts(tile_expert, n_used, next_expert.astype(I32), slot.astype(I32), xs, w_e1, b1, w_e2, b2)
    y_tok = _gather_sparsecore(pos.astype(I32), y)
    out = _combine(h1, gates, p.reshape(n, PLE_DIM), w_ple_gate.astype(BF16),
                   w_ple_proj.astype(BF16), row(g_ple), row(g_final), y_tok, final_norm)
    return out.reshape(b, s, d)


def kernel(x, p, positions, g_attn, w_in, lambda_q1, lambda_k1, lambda_q2, lambda_k2, g_subln, b_forget, w_out, g_moe, w_router, b_router, w_e1, b_e1, w_e2, b_e2, g_ple, w_ple_gate, w_ple_proj, g_final):
    depth = g_attn.shape[0]
    h = x
    for i in range(depth):
        lambda_init = 0.8 - 0.6 * math.exp(-0.3 * i)
        h = _layer(h, p[i], positions, g_attn[i], w_in[i], lambda_q1[i], lambda_k1[i],
                   lambda_q2[i], lambda_k2[i], g_subln[i], b_forget[i], w_out[i], g_moe[i],
                   w_router[i], b_router[i], w_e1[i], b_e1[i], w_e2[i], b_e2[i], g_ple[i],
                   w_ple_gate[i], w_ple_proj[i], g_final, lambda_init, i == depth - 1)
    return h
```

```python
import functools
import math

import jax
import jax.numpy as jnp
from jax import lax
from jax.experimental import pallas as pl
from jax.experimental.pallas import tpu as pltpu
from jax.experimental.pallas import tpu_sc as plsc

F32 = jnp.float32
BF16 = jnp.bfloat16
I32 = jnp.int32
U32 = jnp.uint32

LANES = 128
D_MODEL = 1024
HEAD_DIM = 64
N_DIFF_HEADS = 4
DIFF_V_DIM = 2 * HEAD_DIM
N_FOX_HEADS = 8
DIFF_WIDTH = N_DIFF_HEADS * DIFF_V_DIM
FOX_WIDTH = N_FOX_HEADS * HEAD_DIM
ROT_DIM = HEAD_DIM // 4
ROPE_THETA = 500000.0
N_EXPERTS = 32
TOP_K = 4
D_EXPERT = D_MODEL
SWIGLU_ALPHA = 1.702
SWIGLU_LIMIT = 7.0
PLE_DIM = 256
NORM_EPS = 1e-5
LOG2_E = math.log2(math.e)
MAIN_WIDTH = 3 * DIFF_WIDTH + 3 * FOX_WIDTH

SEQ_TILE = 1024
SEQ_SUBTILE = 256
ATTN_TILE = 256
ATTN_UNITS = 2
TOKEN_TILE = 1024
COMBINE_SUBTILE = 256
ROUTER_TILE = 1024
ROUTER_SUBTILE = 256
EXPERT_TILE = 256
VMEM_LIMIT = 48 * 1024 * 1024
EXPERT_VMEM_LIMIT = 58 * 1024 * 1024


def _params(*semantics):
    return pltpu.CompilerParams(dimension_semantics=semantics, vmem_limit_bytes=VMEM_LIMIT)


def _rms(x, g):
    return x * lax.rsqrt(jnp.mean(x * x, axis=-1, keepdims=True) + NORM_EPS) * g


def _pack_rows(v):
    half = v.shape[1] // 2
    bits = pltpu.bitcast(v.astype(BF16).astype(F32), U32)
    return (bits[:, :half] >> 16) | bits[:, half:]


def _unpack_rows(words):
    low = pltpu.bitcast(words << 16, F32)
    high = pltpu.bitcast(words & jnp.uint32(0xFFFF0000), F32)
    return jnp.concatenate([low, high], axis=1)


def _inproj_kernel(pos_ref, x_ref, g_ref, w_ref, wfzt_ref, bfzt_ref,
                   dqt_ref, dk_ref, dvt_ref, fqt_ref, fk_ref, fvt_ref, c_ref, ct_ref, carry_t_ref):
    ts = SEQ_SUBTILE
    subs = [slice(i * ts, (i + 1) * ts) for i in range(x_ref.shape[1] // ts)]
    xbs = [_rms(x_ref[0, rows, :], g_ref[...]).astype(BF16) for rows in subs]
    projs = [jnp.dot(xb, w_ref[...], preferred_element_type=F32) for xb in xbs]
    nt = (((1,), (1,)), ((), ()))
    tn = (((0,), (0,)), ((), ()))

    def pieces(v):
        hi = v.astype(BF16)
        r1 = v - hi.astype(F32)
        mid = r1.astype(BF16)
        return hi, mid, (r1 - mid.astype(F32)).astype(BF16)

    def rows_to_lanes(v, select):
        return sum(lax.dot_general(part, select, tn, preferred_element_type=F32) for part in pieces(v))

    half_rot = ROT_DIM // 2
    freq = lax.broadcasted_iota(I32, (half_rot, 1), 0)
    inv_freq = jnp.power(ROPE_THETA, -(2 * freq).astype(F32) / ROT_DIM)
    r = lax.broadcasted_iota(I32, (2 * half_rot, 2 * LANES), 0)
    lane2 = lax.broadcasted_iota(I32, (2 * half_rot, 2 * LANES), 1)
    d = lane2 % HEAD_DIM
    same_freq = jnp.where(d % half_rot == r % half_rot, 1.0, 0.0)
    cos_part = jnp.where((r < half_rot) & (lane2 < LANES) & (d < ROT_DIM), same_freq, 0.0)
    sin_part = jnp.where((r >= half_rot) & (lane2 >= LANES) & (d < ROT_DIM),
                         jnp.where(d < half_rot, -same_freq, same_freq), 0.0)
    expand = (cos_part + sin_part).astype(BF16)
    lane = lax.broadcasted_iota(I32, (1, LANES), 1) % HEAD_DIM
    first_half = lane < half_rot

    def rope_tables(rows):
        ang = inv_freq * pos_ref[0, :, rows].astype(F32)
        trig = jnp.concatenate([jnp.cos(ang), jnp.sin(ang)], axis=0)
        tables = rows_to_lanes(trig, expand)
        return jnp.where(lane < ROT_DIM, tables[:, :LANES], 1.0), tables[:, LANES:]

    def rope(t, cosf, sinf):
        outs = []
        for j in range(t.shape[1] // LANES):
            tj = t[:, j * LANES:(j + 1) * LANES]
            partner = jnp.where(first_half,
                                pltpu.roll(tj, LANES - ROT_DIM // 2, 1),
                                pltpu.roll(tj, ROT_DIM // 2, 1))
            outs.append(tj * cosf + partner * sinf)
        return jnp.concatenate(outs, axis=1)

    scale = HEAD_DIM ** -0.5 * LOG2_E
    w = DIFF_WIDTH
    for i, (rows, proj) in enumerate(zip(subs, projs)):
        cosf, sinf = rope_tables(rows)
        dqt_ref[0, :, rows] = (rope(proj[:, 0:w], cosf, sinf) * scale).astype(BF16).T
        dk_ref[0, rows, :] = rope(proj[:, w:2 * w], cosf, sinf).astype(BF16)
        dvt_ref[0, i] = proj[:, 2 * w:3 * w].astype(BF16).T
        fqt_ref[0, :, rows] = (proj[:, 3 * w:4 * w] * scale).astype(BF16).T
        fk_ref[0, rows, :] = proj[:, 4 * w:5 * w].astype(BF16)
        fvt_ref[0, i] = proj[:, 5 * w:6 * w].astype(BF16).T

    def log_sigmoid(z):
        return jnp.minimum(z, 0.0) - jnp.log1p(jnp.exp(-jnp.abs(z)))

    row = lax.broadcasted_iota(I32, (ts, ts), 0)
    col = lax.broadcasted_iota(I32, (ts, ts), 1)
    upto_col = jnp.where(row <= col, 1.0, 0.0).astype(BF16)
    heads = wfzt_ref.shape[0]
    head = lax.broadcasted_iota(I32, (heads, FOX_WIDTH), 0)
    group = lax.broadcasted_iota(I32, (heads, FOX_WIDTH), 1) // HEAD_DIM
    head_lanes = jnp.where(head == group, 1.0, 0.0).astype(BF16)

    @pl.when(pl.program_id(1) == 0)
    def _():
        carry_t_ref[...] = jnp.zeros_like(carry_t_ref)

    for rows, xb in zip(subs, xbs):
        fzt = lax.dot_general(wfzt_ref[...], xb, nt, preferred_element_type=F32) + bfzt_ref[...]
        ct = carry_t_ref[...] + sum(jnp.dot(part, upto_col, preferred_element_type=F32)
                                    for part in pieces(log_sigmoid(fzt)))
        carry_t_ref[...] = ct[:, ts - 1:ts]
        ct = ct * LOG2_E
        ct_ref[0, :, rows] = ct[0:N_FOX_HEADS, :]
        c_ref[0, rows, :] = rows_to_lanes(ct, head_lanes)


def _inproj(positions, x, g, w_main, w_fz_t, b_fz_t):
    b, s, d = x.shape
    ts = SEQ_TILE
    sub = SEQ_SUBTILE
    assert sub == ATTN_TILE
    tile = lambda width: pl.BlockSpec((1, ts, width), lambda i, j: (i, j, 0))
    tile_t = pl.BlockSpec((1, DIFF_WIDTH, ts), lambda i, j: (i, 0, j))
    key_tile_t = pl.BlockSpec((1, ts // sub, DIFF_WIDTH, sub), lambda i, j: (i, j, 0, 0))
    full = lambda arr: pl.BlockSpec(arr.shape, lambda i, j: (0,) * arr.ndim)
    rows = jax.ShapeDtypeStruct((b, s, DIFF_WIDTH), BF16)
    rows_t = jax.ShapeDtypeStruct((b, DIFF_WIDTH, s), BF16)
    key_tiles_t = jax.ShapeDtypeStruct((b, s // sub, DIFF_WIDTH, sub), BF16)
    return pl.pallas_call(
        _inproj_kernel,
        grid=(b, s // ts),
        in_specs=[pl.BlockSpec((1, 1, ts), lambda i, j: (i, 0, j)), tile(d), full(g), full(w_main),
                  full(w_fz_t), full(b_fz_t)],
        out_specs=[tile_t, tile(DIFF_WIDTH), key_tile_t, tile_t, tile(FOX_WIDTH), key_tile_t,
                   tile(FOX_WIDTH), pl.BlockSpec((1, N_FOX_HEADS, ts), lambda i, j: (i, 0, j))],
        out_shape=[rows_t, rows, key_tiles_t, rows_t, rows, key_tiles_t,
                   jax.ShapeDtypeStruct((b, s, FOX_WIDTH), F32),
                   jax.ShapeDtypeStruct((b, N_FOX_HEADS, s), F32)],
        scratch_shapes=[pltpu.VMEM((w_fz_t.shape[0], 1), F32)],
        compiler_params=_params("arbitrary", "arbitrary"),
        name="inproj",
    )(positions.reshape(b, 1, s), x, g, w_main, w_fz_t, b_fz_t)


ONES_ROWS = 16


def _block_diag_queries(qt):
    row = lax.broadcasted_iota(I32, qt.shape, 0)
    zero = jnp.zeros_like(qt)
    return jnp.concatenate([jnp.where(row < HEAD_DIM, qt, zero),
                            jnp.where(row >= HEAD_DIM, qt, zero)], axis=1)


class _Chain:
    def __init__(self, qbd, keys, values_t, scratch, key_bias=None, query_bias=None):
        self.qbd, self.keys, self.values_t = qbd, keys, values_t
        self.acc_ref, self.s_ref, self.p_ref, self.stat_ref = scratch
        self.key_bias, self.query_bias = key_bias, query_bias


_RUNNING_MAX, _TILE_MAX, _RESCALE = 0, 1, 2


def _causal_attention(chains, qi):
    tk, tq = chains[0].s_ref.shape[0], chains[0].s_ref.shape[1] // 2
    ones = jnp.ones((ONES_ROWS, tk), BF16)

    def scores(ch, i, diagonal):
        s = jnp.dot(ch.keys(i), ch.qbd, preferred_element_type=F32)
        if ch.key_bias is not None:
            s = s - ch.key_bias(i)
        if diagonal is not False:
            r = lax.broadcasted_iota(I32, s.shape, 0)
            c = lax.broadcasted_iota(I32, s.shape, 1) & (tq - 1)
            visible = r <= c if diagonal is True else jnp.logical_or(r <= c, jnp.logical_not(diagonal))
            s = jnp.where(visible, s, -jnp.inf)
        ch.s_ref[...] = s
        top = jnp.max(s, axis=0, keepdims=True)
        if ch.query_bias is not None:
            top = top + ch.query_bias
        ch.stat_ref[_TILE_MAX:_TILE_MAX + 1, :] = top

    def probs(ch):
        m = ch.stat_ref[_RUNNING_MAX:_RUNNING_MAX + 1, :]
        m_new = jnp.maximum(m, ch.stat_ref[_TILE_MAX:_TILE_MAX + 1, :])
        ch.stat_ref[_RUNNING_MAX:_RUNNING_MAX + 1, :] = m_new
        ch.stat_ref[_RESCALE:_RESCALE + 1, :] = jnp.exp2(m - m_new)
        ref = m_new if ch.query_bias is None else m_new - ch.query_bias
        ch.p_ref[...] = jnp.exp2(ch.s_ref[...] - ref).astype(BF16)

    def accumulate(ch, i):
        vts = jnp.concatenate([ch.values_t(i), ones], axis=0)
        ch.acc_ref[...] = (ch.stat_ref[_RESCALE:_RESCALE + 1, :] * ch.acc_ref[...]
                           + jnp.dot(vts, ch.p_ref[...], preferred_element_type=F32))

    def body(k, diagonal_next):
        for ch in chains:
            accumulate(ch, jnp.maximum(k - 1, 0))
        for ch in chains:
            probs(ch)
        for ch in chains:
            scores(ch, k + 1, diagonal_next)

    for ch in chains:
        ch.acc_ref[...] = jnp.zeros_like(ch.acc_ref)
        ch.p_ref[...] = jnp.zeros_like(ch.p_ref)
        ch.stat_ref[...] = jnp.concatenate(
            [jnp.full((1, 2 * tq), -jnp.inf, F32), jnp.ones((ch.stat_ref.shape[0] - 1, 2 * tq), F32)], axis=0)
        scores(ch, 0, qi == 0)

    def loop_body(k, carry):
        body(k, False)
        return carry

    lax.fori_loop(0, qi - 1, loop_body, 0)

    @pl.when(qi >= 1)
    def _():
        body(qi - 1, True)

    for ch in chains:
        accumulate(ch, jnp.maximum(qi - 1, 0))
    for ch in chains:
        probs(ch)
    for ch in chains:
        accumulate(ch, qi)


def _attention_kernel(lq1_ref, lk1_ref, lq2_ref, lk2_ref, g_ref,
                      dqt_ref, dk_ref, dvt_ref, fqt_ref, fk_ref, fvt_ref, cq_ref, ck_ref,
                      do_ref, fo_ref, *scratch, lambda_init):
    tq = dqt_ref.shape[2]
    tk = dvt_ref.shape[3]
    units = dqt_ref.shape[1] // LANES
    lane = lax.broadcasted_iota(I32, (1, LANES), 1)
    scratch = [scratch[i:i + 4] for i in range(0, len(scratch), 4)]

    def unit_chains(u):
        lanes = slice(u * LANES, (u + 1) * LANES)

        def tile(ref):
            return lambda i: ref[0, pl.ds(pl.multiple_of(i * tk, tk), tk), lanes]

        def tile_t(ref):
            return lambda i: ref[0, i, lanes, :]

        cq = jnp.concatenate([cq_ref[0, u, 0:1, :], cq_ref[0, u, 1:2, :]], axis=1)

        def key_bias(i):
            ck = tile(ck_ref)(i)
            other = pltpu.roll(ck, HEAD_DIM, 1)
            first = jnp.where(lane < HEAD_DIM, ck, other)
            second = jnp.where(lane >= HEAD_DIM, ck, other)
            return jnp.concatenate([first] * (tq // LANES) + [second] * (tq // LANES), axis=1)

        return [_Chain(_block_diag_queries(dqt_ref[0, lanes, :]), tile(dk_ref), tile_t(dvt_ref),
                       scratch[2 * u]),
                _Chain(_block_diag_queries(fqt_ref[0, lanes, :]), tile(fk_ref), tile_t(fvt_ref),
                       scratch[2 * u + 1], key_bias, cq)]

    chains = [ch for u in range(units) for ch in unit_chains(u)]
    _causal_attention(chains, pl.program_id(2))

    lam = (jnp.exp(jnp.sum(lq1_ref[...] * lk1_ref[...], axis=-1, keepdims=True))
           - jnp.exp(jnp.sum(lq2_ref[...] * lk2_ref[...], axis=-1, keepdims=True)) + lambda_init)
    dv, h = DIFF_V_DIM, HEAD_DIM
    for u in range(units):
        lanes = slice(u * LANES, (u + 1) * LANES)
        acc = chains[2 * u].acc_ref
        o = (acc[0:dv, 0:tq] / acc[dv:dv + 1, 0:tq]
             - lam * (acc[0:dv, tq:] / acc[dv:dv + 1, tq:]))
        ms = jnp.mean(o * o, axis=0, keepdims=True)
        do_ref[0, :, lanes] = (o * lax.rsqrt(ms + NORM_EPS) * g_ref[...]
                               * (1.0 - lambda_init)).astype(BF16).T
        acc = chains[2 * u + 1].acc_ref
        fo_ref[0, :, lanes] = jnp.concatenate([acc[0:h, 0:tq] / acc[2 * h:2 * h + 1, 0:tq],
                                               acc[h:2 * h, tq:] / acc[2 * h:2 * h + 1, tq:]],
                                              axis=0).astype(BF16).T


def _attention(dqt, dk, dvt, fqt, fk, fvt, c_rows, c_cols, lq1, lk1, lq2, lk2, g_col, lambda_init):
    b, s, _ = dk.shape
    t = ATTN_TILE
    assert N_DIFF_HEADS == N_FOX_HEADS // 2 and N_DIFF_HEADS % ATTN_UNITS == 0
    width = ATTN_UNITS * LANES
    vec = lambda arr: pl.BlockSpec(arr.shape, lambda i, h, j: (0, 0))
    q_spec = pl.BlockSpec((1, width, t), lambda i, h, j: (i, h, j))
    k_spec = pl.BlockSpec((1, s, width), lambda i, h, j: (i, 0, h))
    v_spec = pl.BlockSpec((1, s // t, width, t), lambda i, h, j: (i, 0, h, 0))
    o_spec = pl.BlockSpec((1, t, width), lambda i, h, j: (i, j, h))
    out = jax.ShapeDtypeStruct((b, s, DIFF_WIDTH), BF16)
    chain_scratch = [pltpu.VMEM((LANES + ONES_ROWS, 2 * t), F32),
                     pltpu.VMEM((t, 2 * t), F32),
                     pltpu.VMEM((t, 2 * t), BF16),
                     pltpu.VMEM((8, 2 * t), F32)]
    return pl.pallas_call(
        functools.partial(_attention_kernel, lambda_init=lambda_init),
        grid=(b, N_DIFF_HEADS // ATTN_UNITS, s // t),
        in_specs=[vec(lq1), vec(lk1), vec(lq2), vec(lk2), vec(g_col),
                  q_spec, k_spec, v_spec, q_spec, k_spec, v_spec,
                  pl.BlockSpec((1, ATTN_UNITS, 2, t), lambda i, h, j: (i, h, 0, j)), k_spec],
        out_specs=[o_spec, o_spec],
        out_shape=[out, out],
        scratch_shapes=chain_scratch * (2 * ATTN_UNITS),
        compiler_params=_params("arbitrary", "arbitrary", "arbitrary"),
        name="attention",
    )(lq1, lk1, lq2, lk2, g_col, dqt, dk, dvt, fqt, fk, fvt, c_rows, c_cols)


def _router_kernel(d_ref, f_ref, x_ref, wo_ref, g_ref, wrt_ref, brt_ref,
                   h_ref, xn_ref, gate_ref, idx_ref, rank_ref, count_ref, carry_ref):
    tm = ROUTER_SUBTILE
    nt = (((1,), (1,)), ((), ()))

    def pieces(v, count):
        out = []
        for _ in range(count):
            part = v.astype(BF16)
            out.append(part)
            v = v - part.astype(F32)
        return out

    wh, wl = pieces(wrt_ref[...], 2)
    row = lax.broadcasted_iota(I32, (tm, tm), 0)
    col = lax.broadcasted_iota(I32, (tm, tm), 1)
    earlier = jnp.where(row < col, 1.0, 0.0).astype(BF16)
    eye = jnp.where(row == col, 1.0, 0.0).astype(BF16)

    @pl.when(pl.program_id(0) == 0)
    def _():
        carry_ref[...] = jnp.zeros_like(carry_ref)

    subs = [slice(i * tm, (i + 1) * tm) for i in range(x_ref.shape[0] // tm)]

    def logits(rows):
        mixed = (jnp.dot(d_ref[rows, :], wo_ref[0:DIFF_WIDTH, :], preferred_element_type=F32)
                 + jnp.dot(f_ref[rows, :], wo_ref[DIFF_WIDTH:, :], preferred_element_type=F32))
        h = x_ref[rows, :] + mixed
        h_ref[rows, :] = h
        xn = _rms(h, g_ref[...])
        xn_ref[rows, :] = _pack_rows(xn)
        xh, xl = pieces(xn, 2)
        return (lax.dot_general(wh, xh, nt, preferred_element_type=F32)
                + lax.dot_general(wh, xl, nt, preferred_element_type=F32)
                + lax.dot_general(wl, xh, nt, preferred_element_type=F32)) + brt_ref[...]

    def top_k(work):
        expert = lax.broadcasted_iota(I32, work.shape, 0)
        vals, ids, picks = [], [], []
        for _ in range(TOP_K):
            m = jnp.max(work, axis=0, keepdims=True)
            idx = jnp.min(jnp.where(work == m, expert, N_EXPERTS), axis=0, keepdims=True)
            pick = expert == idx
            vals.append(m)
            ids.append(idx)
            picks.append(pick)
            work = jnp.where(pick, -jnp.inf, work)
        exps = [jnp.exp(v - vals[0]) for v in vals]
        denom = exps[0] + exps[1] + exps[2] + exps[3]
        return ids, picks, [e / denom for e in exps]

    def ranks(picks):
        chosen = jnp.where(picks[0] | picks[1] | picks[2] | picks[3], 1.0, 0.0)
        rank = jnp.dot(chosen.astype(BF16), earlier, preferred_element_type=F32) + carry_ref[...]
        carry_ref[...] = carry_ref[...] + jnp.sum(chosen, axis=1, keepdims=True)
        return rank

    def emit(rows, ids, picks, gates, rank):
        slot = lax.broadcasted_iota(I32, (8, tm), 0)
        gate_t = jnp.zeros((8, tm), F32)
        idx_t = jnp.zeros((8, tm), I32)
        rank_t = jnp.zeros((8, tm), F32)
        for k in range(TOP_K):
            gate_t = jnp.where(slot == k, gates[k], gate_t)
            idx_t = jnp.where(slot == k, ids[k], idx_t)
            rank_t = jnp.where(slot == k, jnp.sum(jnp.where(picks[k], rank, 0.0), axis=0, keepdims=True),
                               rank_t)
        idx_ref[:, rows] = idx_t
        rank_ref[:, rows] = rank_t.astype(I32)
        gate_rows = jnp.concatenate([gate_t, jnp.zeros((LANES - 8, tm), F32)], axis=0)
        gate_ref[rows, :] = sum(lax.dot_general(eye, part, nt, preferred_element_type=F32)
                                for part in pieces(gate_rows, 3))

    work = [logits(rows) for rows in subs]
    chosen = [top_k(w) for w in work]
    rank = [ranks(picks) for _, picks, _ in chosen]
    for rows, (ids, picks, gates), r in zip(subs, chosen, rank):
        emit(rows, ids, picks, gates, r)
    count_ref[...] = jnp.broadcast_to(carry_ref[...], count_ref.shape)


def _router(d_out, f_out, x, w_out, g_moe, w_router_t, b_router_t):
    n, d = x.shape
    tm = ROUTER_TILE
    tile = lambda width: pl.BlockSpec((tm, width), lambda i: (i, 0))
    rows = pl.BlockSpec((8, tm), lambda i: (0, i))
    full = lambda arr: pl.BlockSpec(arr.shape, lambda i: (0,) * arr.ndim)
    return pl.pallas_call(
        _router_kernel,
        grid=(n // tm,),
        in_specs=[tile(DIFF_WIDTH), tile(FOX_WIDTH), tile(d), full(w_out), full(g_moe),
                  full(w_router_t), full(b_router_t)],
        out_specs=[tile(d), tile(d // 2), tile(LANES), rows, rows,
                   pl.BlockSpec((N_EXPERTS, LANES), lambda i: (0, 0))],
        out_shape=[jax.ShapeDtypeStruct((n, d), F32), jax.ShapeDtypeStruct((n, d // 2), U32),
                   jax.ShapeDtypeStruct((n, LANES), F32), jax.ShapeDtypeStruct((8, n), I32),
                   jax.ShapeDtypeStruct((8, n), I32), jax.ShapeDtypeStruct((N_EXPERTS, LANES), F32)],
        scratch_shapes=[pltpu.VMEM((N_EXPERTS, 1), F32)],
        compiler_params=_params("arbitrary"),
        name="router",
    )(d_out, f_out, x, w_out, g_moe, w_router_t, b_router_t)


def _slot_kernel(starts_ref, idx_ref, rank_ref, pos_ref):
    idx = idx_ref[...]
    pos = rank_ref[...]
    for e in range(N_EXPERTS):
        pos = jnp.where(idx == e, pos + starts_ref[e], pos)
    pos_ref[...] = pos


def _slots(starts, idx, rank):
    rows, n = idx.shape
    width = min(n, 2048)
    block = pl.BlockSpec((rows, width), lambda i, starts: (0, i))
    return pl.pallas_call(
        _slot_kernel,
        grid_spec=pltpu.PrefetchScalarGridSpec(num_scalar_prefetch=1, grid=(n // width,),
                                               in_specs=[block, block], out_specs=block),
        out_shape=jax.ShapeDtypeStruct((rows, n), I32),
        compiler_params=_params("arbitrary"),
        name="slots",
    )(starts, idx, rank)


SC_WINDOW = 128
SC_COLUMNS = 256


def _dispatch_sparsecore(pos, pad_rows, xn, n_rows):
    n, d = xn.shape
    mesh = plsc.VectorSubcoreMesh(core_axis_name="core", subcore_axis_name="subcore")
    spread = dict(core_axis_name=("core", "subcore"), dimension_semantics=(pltpu.PARALLEL,))

    @functools.partial(pl.kernel, out_type=jax.ShapeDtypeStruct((n_rows, d), xn.dtype), mesh=mesh,
                       scratch_types=[], name="dispatch_sc")
    def scatter(x_hbm, pos_hbm, zero_hbm, pad_hbm, xs_hbm):
        for c in range(d // SC_COLUMNS):
            columns = xs_hbm.at[:, pl.ds(c * SC_COLUMNS, SC_COLUMNS)]

            def body(x_vmem, pos_vmem, columns=columns):
                for k in range(TOP_K):
                    pltpu.sync_copy(x_vmem, columns.at[pos_vmem.at[k]])

            pltpu.emit_pipeline(
                body, grid=(n // SC_WINDOW,),
                in_specs=[pl.BlockSpec((SC_WINDOW, SC_COLUMNS), index_map=lambda i, c=c: (i, c)),
                          pl.BlockSpec((TOP_K, SC_WINDOW), index_map=lambda i: (0, i))],
                out_specs=[], **spread)(x_hbm, pos_hbm)

            def pad_body(zero_vmem, pad_vmem, columns=columns):
                pltpu.sync_copy(zero_vmem, columns.at[pad_vmem.at[0]])

            pltpu.emit_pipeline(
                pad_body, grid=(pad_rows.size // SC_WINDOW,),
                in_specs=[pl.BlockSpec((SC_WINDOW, SC_COLUMNS), index_map=lambda i: (0, 0)),
                          pl.BlockSpec((1, SC_WINDOW), index_map=lambda i: (0, i))],
                out_specs=[], **spread)(zero_hbm, pad_hbm)

    return scatter(xn, pos, jnp.zeros((SC_WINDOW, SC_COLUMNS), xn.dtype), pad_rows.reshape(1, -1))


def _expert_kernel(tile_expert_ref, n_used_ref, next_expert_ref, slot_ref,
                   x_ref, w1_ref, b1_ref, w2_ref, b2_ref, y_ref,
                   w1f_ref, w2f_ref, sem_ref, w1b_ref, w2p_ref, w2b_ref):
    t = pl.program_id(0)
    fresh = (t == 0) | (tile_expert_ref[t] != tile_expert_ref[jnp.maximum(t - 1, 0)])
    used = t < n_used_ref[0]

    def weight_copies(e, s):
        return (pltpu.make_async_copy(w1_ref.at[e], w1f_ref.at[s], sem_ref.at[0, s]),
                pltpu.make_async_copy(w2_ref.at[e], w2f_ref.at[s], sem_ref.at[1, s]))

    @pl.when(t == 0)
    def _():
        for copy in weight_copies(tile_expert_ref[0], slot_ref[0]):
            copy.start()

    @pl.when(fresh & used)
    def _():
        s = slot_ref[t]
        for copy in weight_copies(tile_expert_ref[t], s):
            copy.wait()

        @pl.when(next_expert_ref[t] >= 0)
        def _():
            for copy in weight_copies(next_expert_ref[t], 1 - s):
                copy.start()

        rows = LANES
        half = LANES // 2
        for r in range(0, w1b_ref.shape[0], rows):
            w1b_ref[r:r + rows, :] = w1f_ref[s, r:r + rows, :].astype(BF16)
        for c in range(w2p_ref.shape[0]):
            lanes = slice(c * LANES, (c + 1) * LANES)
            for r in range(0, w2b_ref.shape[0], rows):
                for b in range(2):
                    w2p_ref[c, pl.ds(r + b, half, stride=2), :] = (
                        w2f_ref[s, r + b * half:r + (b + 1) * half, lanes])
            w2b_ref[:, lanes] = w2p_ref[c].astype(BF16)

    @pl.when(used)
    def _():
        xb = _unpack_rows(x_ref[...]).astype(BF16)
        h = jnp.dot(xb, w1b_ref[...], preferred_element_type=F32) + b1_ref[0]
        even = lax.broadcasted_iota(I32, (1, LANES), 1) % 2 == 0
        parts = []
        for q in range(h.shape[1] // (2 * LANES)):
            a = h[:, (2 * q) * LANES:(2 * q + 1) * LANES]
            b = h[:, (2 * q + 1) * LANES:(2 * q + 2) * LANES]
            hg = jnp.where(even, a, pltpu.roll(b, 1, 1))
            hl = jnp.where(even, pltpu.roll(a, LANES - 1, 1), b)
            glu = jnp.minimum(hg, SWIGLU_LIMIT)
            lin = jnp.clip(hl, -SWIGLU_LIMIT, SWIGLU_LIMIT)
            parts.append((glu * jax.nn.sigmoid(SWIGLU_ALPHA * glu) * (lin + 1.0)).astype(BF16))
        act = jnp.concatenate(parts, axis=1)
        y_ref[...] = _pack_rows(jnp.dot(act, w2b_ref[...], preferred_element_type=F32) + b2_ref[0])

    @pl.when(jnp.logical_not(used))
    def _():
        y_ref[...] = jnp.zeros_like(y_ref)


def _experts(tile_expert, n_used, next_expert, slot, xs, w1, b1, w2, b2):
    n_rows, d = xs.shape
    tm = EXPERT_TILE
    by_expert = lambda arr: pl.BlockSpec((1,) + arr.shape[1:], lambda i, te, *_: (te[i], 0, 0))
    rows = pl.BlockSpec((tm, d), lambda i, *_: (i, 0))
    used_rows = pl.BlockSpec((tm, d), lambda i, te, nu, *_: (jnp.minimum(i, nu[0] - 1), 0))
    hbm = pl.BlockSpec(memory_space=pl.ANY)
    return pl.pallas_call(
        _expert_kernel,
        grid_spec=pltpu.PrefetchScalarGridSpec(
            num_scalar_prefetch=4,
            grid=(n_rows // tm,),
            in_specs=[used_rows, hbm, by_expert(b1), hbm, by_expert(b2)],
            out_specs=rows,
            scratch_shapes=[pltpu.VMEM((2,) + w1.shape[1:], F32), pltpu.VMEM((2,) + w2.shape[1:], F32),
                            pltpu.SemaphoreType.DMA((2, 2)),
                            pltpu.VMEM(w1.shape[1:], BF16),
                            pltpu.VMEM((w2.shape[2] // LANES, w2.shape[1], LANES), F32),
                            pltpu.VMEM(w2.shape[1:], BF16)]),
        out_shape=jax.ShapeDtypeStruct(xs.shape, xs.dtype),
        compiler_params=pltpu.CompilerParams(dimension_semantics=("arbitrary",),
                                             vmem_limit_bytes=EXPERT_VMEM_LIMIT),
        name="experts",
    )(tile_expert, n_used, next_expert, slot, xs, w1, b1, w2, b2)


def _gather_sparsecore(pos, y):
    d = y.shape[1]
    total = pos.size
    mesh = plsc.VectorSubcoreMesh(core_axis_name="core", subcore_axis_name="subcore")

    @functools.partial(pl.kernel, out_type=jax.ShapeDtypeStruct((total, d), y.dtype), mesh=mesh,
                       scratch_types=[], name="gather_sc")
    def gather(y_hbm, pos_hbm, out_hbm):
        for c in range(d // SC_COLUMNS):
            def body(pos_vmem, out_vmem, c=c):
                pltpu.sync_copy(y_hbm.at[:, pl.ds(c * SC_COLUMNS, SC_COLUMNS)].at[pos_vmem.at[0]], out_vmem)

            pltpu.emit_pipeline(
                body, grid=(total // SC_WINDOW,),
                in_specs=[pl.BlockSpec((1, SC_WINDOW), index_map=lambda i: (0, i))],
                out_specs=[pl.BlockSpec((SC_WINDOW, SC_COLUMNS), index_map=lambda i, c=c: (i, c))],
                core_axis_name=("core", "subcore"),
                dimension_semantics=(pltpu.PARALLEL,))(pos_hbm, out_hbm)

    return gather(y, pos.reshape(1, -1))


def _combine_kernel(h_ref, gate_ref, p_ref, wg_ref, wp_ref, gple_ref, gfin_ref, *refs, final_norm):
    y_refs, o_ref = refs[:TOP_K], refs[TOP_K]
    ts = COMBINE_SUBTILE
    subs = [slice(i * ts, (i + 1) * ts) for i in range(h_ref.shape[0] // ts)]

    def mix(rows):
        gates = gate_ref[rows, :]
        moe = gates[:, 0:1] * _unpack_rows(y_refs[0][rows, :])
        for k in range(1, TOP_K):
            moe = moe + gates[:, k:k + 1] * _unpack_rows(y_refs[k][rows, :])
        h = h_ref[rows, :] + moe
        logit = jnp.dot(_rms(h, gple_ref[...]).astype(BF16), wg_ref[...], preferred_element_type=F32)
        emb = jnp.dot(p_ref[rows, :].astype(BF16), wp_ref[...], preferred_element_type=F32)
        return h, logit, emb

    mixed = [mix(rows) for rows in subs]
    for rows, (h, logit, emb) in zip(subs, mixed):
        h = h + jax.nn.sigmoid(logit) * emb
        o_ref[rows, :] = _rms(h, gfin_ref[...]) if final_norm else h


def _combine(h, gates, p, w_gate, w_proj, g_ple, g_final, y_tok, final_norm):
    n, d = h.shape
    tm = TOKEN_TILE
    tile = lambda width: pl.BlockSpec((tm, width), lambda i: (i, 0))
    full = lambda arr: pl.BlockSpec(arr.shape, lambda i: (0,) * arr.ndim)
    choice = lambda k: pl.BlockSpec((tm, y_tok.shape[1]), lambda i: (k * (n // tm) + i, 0))
    return pl.pallas_call(
        functools.partial(_combine_kernel, final_norm=final_norm),
        grid=(n // tm,),
        in_specs=[tile(d), tile(LANES), tile(p.shape[1]), full(w_gate), full(w_proj),
                  full(g_ple), full(g_final)] + [choice(k) for k in range(TOP_K)],
        out_specs=tile(d),
        out_shape=jax.ShapeDtypeStruct((n, d), F32),
        compiler_params=_params("arbitrary"),
        name="combine",
    )(h, gates, p, w_gate, w_proj, g_ple, g_final, *([y_tok] * TOP_K))


def _layer(h, p, positions, g_attn, w_in, lq1, lk1, lq2, lk2, g_subln, b_forget, w_out, g_moe,
           w_router, b_router, w_e1, b_e1, w_e2, b_e2, g_ple, w_ple_gate, w_ple_proj, g_final,
           lambda_init, final_norm):
    b, s, d = h.shape
    n = b * s
    row = lambda v: v.reshape(1, -1)

    w_main = w_in[:, :MAIN_WIDTH].astype(BF16)
    pad_heads = ((0, 2 * N_FOX_HEADS - N_FOX_HEADS), (0, 0))
    w_fz_t = jnp.pad(w_in[:, MAIN_WIDTH:].T, pad_heads).astype(BF16)
    b_fz_t = jnp.pad(b_forget.reshape(-1, 1), pad_heads)
    b1 = b_e1[:, None, :]
    b2 = b_e2[:, None, :]

    dq_t, dk, dv_t, fq_t, fk, fv_t, c, c_t = _inproj(positions, h, row(g_attn), w_main, w_fz_t, b_fz_t)
    c_rows = c_t.reshape(b, N_FOX_HEADS // 2, 2, s)
    d_out, f_out = _attention(dq_t, dk, dv_t, fq_t, fk, fv_t, c_rows, c, row(lq1), row(lk1), row(lq2),
                              row(lk2), g_subln.reshape(-1, 1), lambda_init)

    h1, xn, gates, idx, rank, counts = _router(
        d_out.reshape(n, DIFF_WIDTH), f_out.reshape(n, FOX_WIDTH), h.reshape(n, d),
        w_out.astype(BF16), row(g_moe), w_router.T, b_router.reshape(-1, 1))

    te = EXPERT_TILE
    n_rows = n * TOP_K + N_EXPERTS * te
    counts = counts[:, 0].astype(I32)
    padded = (counts + te - 1) // te * te
    ends = jnp.cumsum(padded)
    starts = ends - padded
    pos = _slots(starts.astype(I32), idx, rank)[:TOP_K]
    tile_start = jnp.arange(n_rows // te, dtype=I32) * te
    tile_expert = jnp.minimum(jnp.sum((ends[None, :] <= tile_start[:, None]).astype(I32), axis=1),
                              N_EXPERTS - 1)
    n_used = (ends[-1:] // te).astype(I32)
    owns = padded > 0
    later = lax.cummin(jnp.where(owns, jnp.arange(N_EXPERTS, dtype=I32), N_EXPERTS), reverse=True)
    following = jnp.concatenate([later[1:], jnp.full((1,), N_EXPERTS, I32)])
    following = jnp.where(following < N_EXPERTS, following, -1)
    owner = tile_expert[:, None] == jnp.arange(N_EXPERTS, dtype=I32)[None, :]
    next_expert = jnp.sum(jnp.where(owner, following[None, :], 0), axis=1)
    slot = jnp.sum(jnp.where(owner, ((jnp.cumsum(owns.astype(I32)) - 1) % 2)[None, :], 0), axis=1)

    pad_count = padded - counts
    pad_total = jnp.sum(pad_count)
    pad_before = jnp.cumsum(pad_count) - pad_count
    entry = jnp.arange(N_EXPERTS * te, dtype=I32)
    which = entry % jnp.maximum(pad_total, 1)
    in_or_before = which[:, None] >= pad_before[None, :]
    expert_of = jnp.sum(in_or_before.astype(I32), axis=1) - 1
    mine = expert_of[:, None] == jnp.arange(N_EXPERTS, dtype=I32)[None, :]
    pad_rows = jnp.sum(jnp.where(mine, (starts + counts - pad_before)[None, :], 0), axis=1) + which
    pad_rows = jnp.where(pad_total > 0, pad_rows, n_rows - 1 - entry % N_EXPERTS)
    xs = _dispatch_sparsecore(pos.astype(I32), pad_rows.astype(I32), xn, n_rows)
    y = _experts(tile_expert, n_used, next_expert.astype(I32), slot.astype(I32), xs, w_e1, b1, w_e2, b2)
    y_tok = _gather_sparsecore(pos.astype(I32), y)
    out = _combine(h1, gates, p.reshape(n, PLE_DIM), w_ple_gate.astype(BF16),
                   w_ple_proj.astype(BF16), row(g_ple), row(g_final), y_tok, final_norm)
    return out.reshape(b, s, d)


def kernel(x, p, positions, g_attn, w_in, lambda_q1, lambda_k1, lambda_q2, lambda_k2, g_subln, b_forget, w_out, g_moe, w_router, b_router, w_e1, b_e1, w_e2, b_e2, g_ple, w_ple_gate, w_ple_proj, g_final):
    depth = g_attn.shape[0]
    h = x
    for i in range(depth):
        lambda_init = 0.8 - 0.6 * math.exp(-0.3 * i)
        h = _layer(h, p[i], positions, g_attn[i], w_in[i], lambda_q1[i], lambda_k1[i],
                   lambda_q2[i], lambda_k2[i], g_subln[i], b_forget[i], w_out[i], g_moe[i],
                   w_router[i], b_router[i], w_e1[i], b_e1[i], w_e2[i], b_e2[i], g_ple[i],
                   w_ple_gate[i], w_ple_proj[i], g_final, lambda_init, i == depth - 1)
    return h
```

```python
import functools
import math

import jax
import jax.numpy as jnp
from jax import lax
from jax.experimental import pallas as pl
from jax.experimental.pallas import tpu as pltpu
from jax.experimental.pallas import tpu_sc as plsc

F32 = jnp.float32
BF16 = jnp.bfloat16
I32 = jnp.int32
U32 = jnp.uint32

LANES = 128
BF16_TILE_ROWS = 16
D_MODEL = 1024
HEAD_DIM = 64
N_DIFF_HEADS = 4
DIFF_V_DIM = 2 * HEAD_DIM
N_FOX_HEADS = 8
DIFF_WIDTH = N_DIFF_HEADS * DIFF_V_DIM
FOX_WIDTH = N_FOX_HEADS * HEAD_DIM
ROT_DIM = HEAD_DIM // 4
ROPE_THETA = 500000.0
N_EXPERTS = 32
TOP_K = 4
D_EXPERT = D_MODEL
SWIGLU_ALPHA = 1.702
SWIGLU_LIMIT = 7.0
PLE_DIM = 256
NORM_EPS = 1e-5
LOG2_E = math.log2(math.e)
MAIN_WIDTH = 3 * DIFF_WIDTH + 3 * FOX_WIDTH

SEQ_TILE = 1024
SEQ_SUBTILE = 256
ATTN_TILE = 256
ATTN_UNITS = 2
TOKEN_TILE = 1024
COMBINE_SUBTILE = 256
ROUTER_TILE = 1024
ROUTER_SUBTILE = 256
EXPERT_TILE = 256
VMEM_LIMIT = 48 * 1024 * 1024
EXPERT_VMEM_LIMIT = 58 * 1024 * 1024


def _params(*semantics):
    return pltpu.CompilerParams(dimension_semantics=semantics, vmem_limit_bytes=VMEM_LIMIT)


def _rms(x, g):
    return x * lax.rsqrt(jnp.mean(x * x, axis=-1, keepdims=True) + NORM_EPS) * g


def _pack_rows(v):
    half = v.shape[1] // 2
    bits = pltpu.bitcast(v.astype(BF16).astype(F32), U32)
    return (bits[:, :half] >> 16) | bits[:, half:]


def _unpack_rows(words):
    low = pltpu.bitcast(words << 16, F32)
    high = pltpu.bitcast(words & jnp.uint32(0xFFFF0000), F32)
    return jnp.concatenate([low, high], axis=1)


def _inproj_kernel(pos_ref, x_ref, g_ref, w_ref, wfzt_ref, bfzt_ref,
                   dqt_ref, dk_ref, dvt_ref, fqt_ref, fk_ref, fvt_ref, c_ref, ct_ref, carry_t_ref):
    ts = SEQ_SUBTILE
    subs = [slice(i * ts, (i + 1) * ts) for i in range(x_ref.shape[1] // ts)]
    xbs = [_rms(x_ref[0, rows, :], g_ref[...]).astype(BF16) for rows in subs]
    projs = [jnp.dot(xb, w_ref[...], preferred_element_type=F32) for xb in xbs]
    nt = (((1,), (1,)), ((), ()))
    tn = (((0,), (0,)), ((), ()))

    def pieces(v):
        hi = v.astype(BF16)
        r1 = v - hi.astype(F32)
        mid = r1.astype(BF16)
        return hi, mid, (r1 - mid.astype(F32)).astype(BF16)

    def rows_to_lanes(v, select):
        return sum(lax.dot_general(part, select, tn, preferred_element_type=F32) for part in pieces(v))

    half_rot = ROT_DIM // 2
    freq = lax.broadcasted_iota(I32, (half_rot, 1), 0)
    inv_freq = jnp.power(ROPE_THETA, -(2 * freq).astype(F32) / ROT_DIM)
    r = lax.broadcasted_iota(I32, (2 * half_rot, 2 * LANES), 0)
    lane2 = lax.broadcasted_iota(I32, (2 * half_rot, 2 * LANES), 1)
    d = lane2 % HEAD_DIM
    same_freq = jnp.where(d % half_rot == r % half_rot, 1.0, 0.0)
    cos_part = jnp.where((r < half_rot) & (lane2 < LANES) & (d < ROT_DIM), same_freq, 0.0)
    sin_part = jnp.where((r >= half_rot) & (lane2 >= LANES) & (d < ROT_DIM),
                         jnp.where(d < half_rot, -same_freq, same_freq), 0.0)
    expand = (cos_part + sin_part).astype(BF16)
    lane = lax.broadcasted_iota(I32, (1, LANES), 1) % HEAD_DIM
    first_half = lane < half_rot

    def rope_tables(rows):
        ang = inv_freq * pos_ref[0, :, rows].astype(F32)
        trig = jnp.concatenate([jnp.cos(ang), jnp.sin(ang)], axis=0)
        tables = rows_to_lanes(trig, expand)
        return jnp.where(lane < ROT_DIM, tables[:, :LANES], 1.0), tables[:, LANES:]

    def rope(t, cosf, sinf):
        outs = []
        for j in range(t.shape[1] // LANES):
            tj = t[:, j * LANES:(j + 1) * LANES]
            partner = jnp.where(first_half,
                                pltpu.roll(tj, LANES - ROT_DIM // 2, 1),
                                pltpu.roll(tj, ROT_DIM // 2, 1))
            outs.append(tj * cosf + partner * sinf)
        return jnp.concatenate(outs, axis=1)

    scale = HEAD_DIM ** -0.5 * LOG2_E
    w = DIFF_WIDTH
    for i, (rows, proj) in enumerate(zip(subs, projs)):
        cosf, sinf = rope_tables(rows)
        dqt_ref[0, :, rows] = (rope(proj[:, 0:w], cosf, sinf) * scale).astype(BF16).T
        dk_ref[0, rows, :] = rope(proj[:, w:2 * w], cosf, sinf).astype(BF16)
        dvt_ref[0, i] = proj[:, 2 * w:3 * w].astype(BF16).T
        fqt_ref[0, :, rows] = (proj[:, 3 * w:4 * w] * scale).astype(BF16).T
        fk_ref[0, rows, :] = proj[:, 4 * w:5 * w].astype(BF16)
        fvt_ref[0, i] = proj[:, 5 * w:6 * w].astype(BF16).T

    def log_sigmoid(z):
        return jnp.minimum(z, 0.0) - jnp.log1p(jnp.exp(-jnp.abs(z)))

    row = lax.broadcasted_iota(I32, (ts, ts), 0)
    col = lax.broadcasted_iota(I32, (ts, ts), 1)
    upto_col = jnp.where(row <= col, 1.0, 0.0).astype(BF16)
    heads = wfzt_ref.shape[0]
    head = lax.broadcasted_iota(I32, (heads, FOX_WIDTH), 0)
    group = lax.broadcasted_iota(I32, (heads, FOX_WIDTH), 1) // HEAD_DIM
    head_lanes = jnp.where(head == group, 1.0, 0.0).astype(BF16)

    @pl.when(pl.program_id(1) == 0)
    def _():
        carry_t_ref[...] = jnp.zeros_like(carry_t_ref)

    for rows, xb in zip(subs, xbs):
        fzt = lax.dot_general(wfzt_ref[...], xb, nt, preferred_element_type=F32) + bfzt_ref[...]
        ct = carry_t_ref[...] + sum(jnp.dot(part, upto_col, preferred_element_type=F32)
                                    for part in pieces(log_sigmoid(fzt)))
        carry_t_ref[...] = ct[:, ts - 1:ts]
        ct = ct * LOG2_E
        ct_ref[0, :, rows] = ct[0:N_FOX_HEADS, :]
        c_ref[0, rows, :] = rows_to_lanes(ct, head_lanes)


def _inproj(positions, x, g, w_main, w_fz_t, b_fz_t):
    b, s, d = x.shape
    ts = SEQ_TILE
    sub = SEQ_SUBTILE
    assert sub == ATTN_TILE
    tile = lambda width: pl.BlockSpec((1, ts, width), lambda i, j: (i, j, 0))
    tile_t = pl.BlockSpec((1, DIFF_WIDTH, ts), lambda i, j: (i, 0, j))
    key_tile_t = pl.BlockSpec((1, ts // sub, DIFF_WIDTH, sub), lambda i, j: (i, j, 0, 0))
    full = lambda arr: pl.BlockSpec(arr.shape, lambda i, j: (0,) * arr.ndim)
    rows = jax.ShapeDtypeStruct((b, s, DIFF_WIDTH), BF16)
    rows_t = jax.ShapeDtypeStruct((b, DIFF_WIDTH, s), BF16)
    key_tiles_t = jax.ShapeDtypeStruct((b, s // sub, DIFF_WIDTH, sub), BF16)
    return pl.pallas_call(
        _inproj_kernel,
        grid=(b, s // ts),
        in_specs=[pl.BlockSpec((1, 1, ts), lambda i, j: (i, 0, j)), tile(d), full(g), full(w_main),
                  full(w_fz_t), full(b_fz_t)],
        out_specs=[tile_t, tile(DIFF_WIDTH), key_tile_t, tile_t, tile(FOX_WIDTH), key_tile_t,
                   tile(FOX_WIDTH), pl.BlockSpec((1, N_FOX_HEADS, ts), lambda i, j: (i, 0, j))],
        out_shape=[rows_t, rows, key_tiles_t, rows_t, rows, key_tiles_t,
                   jax.ShapeDtypeStruct((b, s, FOX_WIDTH), F32),
                   jax.ShapeDtypeStruct((b, N_FOX_HEADS, s), F32)],
        scratch_shapes=[pltpu.VMEM((w_fz_t.shape[0], 1), F32)],
        compiler_params=_params("arbitrary", "arbitrary"),
        name="inproj",
    )(positions.reshape(b, 1, s), x, g, w_main, w_fz_t, b_fz_t)


ONES_ROWS = 16


def _block_diag_queries(qt):
    row = lax.broadcasted_iota(I32, qt.shape, 0)
    zero = jnp.zeros_like(qt)
    return jnp.concatenate([jnp.where(row < HEAD_DIM, qt, zero),
                            jnp.where(row >= HEAD_DIM, qt, zero)], axis=1)


class _Chain:
    def __init__(self, qbd, keys, values_t, scratch, key_bias=None, query_bias=None):
        self.qbd, self.keys, self.values_t = qbd, keys, values_t
        self.acc_ref, self.s_ref, self.p_ref, self.stat_ref = scratch
        self.key_bias, self.query_bias = key_bias, query_bias


_RUNNING_MAX, _TILE_MAX, _RESCALE = 0, 1, 2


def _causal_attention(chains, qi):
    tk, tq = chains[0].s_ref.shape[0], chains[0].s_ref.shape[1] // 2
    ones = jnp.ones((ONES_ROWS, tk), BF16)

    def scores(ch, i, diagonal):
        s = jnp.dot(ch.keys(i), ch.qbd, preferred_element_type=F32)
        if ch.key_bias is not None:
            s = s - ch.key_bias(i)
        if diagonal is not False:
            r = lax.broadcasted_iota(I32, s.shape, 0)
            c = lax.broadcasted_iota(I32, s.shape, 1) & (tq - 1)
            visible = r <= c if diagonal is True else jnp.logical_or(r <= c, jnp.logical_not(diagonal))
            s = jnp.where(visible, s, -jnp.inf)
        ch.s_ref[...] = s
        top = jnp.max(s, axis=0, keepdims=True)
        if ch.query_bias is not None:
            top = top + ch.query_bias
        ch.stat_ref[_TILE_MAX:_TILE_MAX + 1, :] = top

    def probs(ch):
        m = ch.stat_ref[_RUNNING_MAX:_RUNNING_MAX + 1, :]
        m_new = jnp.maximum(m, ch.stat_ref[_TILE_MAX:_TILE_MAX + 1, :])
        ch.stat_ref[_RUNNING_MAX:_RUNNING_MAX + 1, :] = m_new
        ch.stat_ref[_RESCALE:_RESCALE + 1, :] = jnp.exp2(m - m_new)
        ref = m_new if ch.query_bias is None else m_new - ch.query_bias
        ch.p_ref[...] = jnp.exp2(ch.s_ref[...] - ref).astype(BF16)

    def accumulate(ch, i):
        vts = jnp.concatenate([ch.values_t(i), ones], axis=0)
        ch.acc_ref[...] = (ch.stat_ref[_RESCALE:_RESCALE + 1, :] * ch.acc_ref[...]
                           + jnp.dot(vts, ch.p_ref[...], preferred_element_type=F32))

    def body(k, diagonal_next):
        for ch in chains:
            accumulate(ch, jnp.maximum(k - 1, 0))
        for ch in chains:
            probs(ch)
        for ch in chains:
            scores(ch, k + 1, diagonal_next)

    for ch in chains:
        ch.acc_ref[...] = jnp.zeros_like(ch.acc_ref)
        ch.p_ref[...] = jnp.zeros_like(ch.p_ref)
        ch.stat_ref[...] = jnp.concatenate(
            [jnp.full((1, 2 * tq), -jnp.inf, F32), jnp.ones((ch.stat_ref.shape[0] - 1, 2 * tq), F32)], axis=0)
        scores(ch, 0, qi == 0)

    def loop_body(k, carry):
        body(k, False)
        return carry

    lax.fori_loop(0, qi - 1, loop_body, 0)

    @pl.when(qi >= 1)
    def _():
        body(qi - 1, True)

    for ch in chains:
        accumulate(ch, jnp.maximum(qi - 1, 0))
    for ch in chains:
        probs(ch)
    for ch in chains:
        accumulate(ch, qi)


def _attention_kernel(lq1_ref, lk1_ref, lq2_ref, lk2_ref, g_ref,
                      dqt_ref, dk_ref, dvt_ref, fqt_ref, fk_ref, fvt_ref, cq_ref, ck_ref,
                      do_ref, fo_ref, *scratch, lambda_init):
    tq = dqt_ref.shape[2]
    tk = dvt_ref.shape[3]
    units = dqt_ref.shape[1] // LANES
    lane = lax.broadcasted_iota(I32, (1, LANES), 1)
    scratch = [scratch[i:i + 4] for i in range(0, len(scratch), 4)]

    def unit_chains(u):
        lanes = slice(u * LANES, (u + 1) * LANES)

        def tile(ref):
            return lambda i: ref[0, pl.ds(pl.multiple_of(i * tk, tk), tk), lanes]

        def tile_t(ref):
            return lambda i: ref[0, i, lanes, :]

        cq = jnp.concatenate([cq_ref[0, u, 0:1, :], cq_ref[0, u, 1:2, :]], axis=1)

        def key_bias(i):
            ck = tile(ck_ref)(i)
            other = pltpu.roll(ck, HEAD_DIM, 1)
            first = jnp.where(lane < HEAD_DIM, ck, other)
            second = jnp.where(lane >= HEAD_DIM, ck, other)
            return jnp.concatenate([first] * (tq // LANES) + [second] * (tq // LANES), axis=1)

        return [_Chain(_block_diag_queries(dqt_ref[0, lanes, :]), tile(dk_ref), tile_t(dvt_ref),
                       scratch[2 * u]),
                _Chain(_block_diag_queries(fqt_ref[0, lanes, :]), tile(fk_ref), tile_t(fvt_ref),
                       scratch[2 * u + 1], key_bias, cq)]

    chains = [ch for u in range(units) for ch in unit_chains(u)]
    _causal_attention(chains, pl.program_id(2))

    lam = (jnp.exp(jnp.sum(lq1_ref[...] * lk1_ref[...], axis=-1, keepdims=True))
           - jnp.exp(jnp.sum(lq2_ref[...] * lk2_ref[...], axis=-1, keepdims=True)) + lambda_init)
    dv, h = DIFF_V_DIM, HEAD_DIM
    for u in range(units):
        lanes = slice(u * LANES, (u + 1) * LANES)
        acc = chains[2 * u].acc_ref
        o = (acc[0:dv, 0:tq] / acc[dv:dv + 1, 0:tq]
             - lam * (acc[0:dv, tq:] / acc[dv:dv + 1, tq:]))
        ms = jnp.mean(o * o, axis=0, keepdims=True)
        do_ref[0, :, lanes] = (o * lax.rsqrt(ms + NORM_EPS) * g_ref[...]
                               * (1.0 - lambda_init)).astype(BF16).T
        acc = chains[2 * u + 1].acc_ref
        fo_ref[0, :, lanes] = jnp.concatenate([acc[0:h, 0:tq] / acc[2 * h:2 * h + 1, 0:tq],
                                               acc[h:2 * h, tq:] / acc[2 * h:2 * h + 1, tq:]],
                                              axis=0).astype(BF16).T


def _attention(dqt, dk, dvt, fqt, fk, fvt, c_rows, c_cols, lq1, lk1, lq2, lk2, g_col, lambda_init):
    b, s, _ = dk.shape
    t = ATTN_TILE
    assert N_DIFF_HEADS == N_FOX_HEADS // 2 and N_DIFF_HEADS % ATTN_UNITS == 0
    width = ATTN_UNITS * LANES
    vec = lambda arr: pl.BlockSpec(arr.shape, lambda i, h, j: (0, 0))
    q_spec = pl.BlockSpec((1, width, t), lambda i, h, j: (i, h, j))
    k_spec = pl.BlockSpec((1, s, width), lambda i, h, j: (i, 0, h))
    v_spec = pl.BlockSpec((1, s // t, width, t), lambda i, h, j: (i, 0, h, 0))
    o_spec = pl.BlockSpec((1, t, width), lambda i, h, j: (i, j, h))
    out = jax.ShapeDtypeStruct((b, s, DIFF_WIDTH), BF16)
    chain_scratch = [pltpu.VMEM((LANES + ONES_ROWS, 2 * t), F32),
                     pltpu.VMEM((t, 2 * t), F32),
                     pltpu.VMEM((t, 2 * t), BF16),
                     pltpu.VMEM((8, 2 * t), F32)]
    return pl.pallas_call(
        functools.partial(_attention_kernel, lambda_init=lambda_init),
        grid=(b, N_DIFF_HEADS // ATTN_UNITS, s // t),
        in_specs=[vec(lq1), vec(lk1), vec(lq2), vec(lk2), vec(g_col),
                  q_spec, k_spec, v_spec, q_spec, k_spec, v_spec,
                  pl.BlockSpec((1, ATTN_UNITS, 2, t), lambda i, h, j: (i, h, 0, j)), k_spec],
        out_specs=[o_spec, o_spec],
        out_shape=[out, out],
        scratch_shapes=chain_scratch * (2 * ATTN_UNITS),
        compiler_params=_params("arbitrary", "arbitrary", "arbitrary"),
        name="attention",
    )(lq1, lk1, lq2, lk2, g_col, dqt, dk, dvt, fqt, fk, fvt, c_rows, c_cols)


def _router_kernel(d_ref, f_ref, x_ref, wo_ref, g_ref, wrt_ref, brt_ref,
                   h_ref, xn_ref, gate_ref, idx_ref, rank_ref, count_ref, carry_ref):
    tm = ROUTER_SUBTILE
    nt = (((1,), (1,)), ((), ()))

    def pieces(v, count):
        out = []
        for _ in range(count):
            part = v.astype(BF16)
            out.append(part)
            v = v - part.astype(F32)
        return out

    wh = wrt_ref[...].astype(BF16)
    row = lax.broadcasted_iota(I32, (tm, tm), 0)
    col = lax.broadcasted_iota(I32, (tm, tm), 1)
    earlier = jnp.where(row < col, 1.0, 0.0).astype(BF16)
    eye = jnp.where(row == col, 1.0, 0.0).astype(BF16)

    @pl.when(pl.program_id(0) == 0)
    def _():
        carry_ref[...] = jnp.zeros_like(carry_ref)

    subs = [slice(i * tm, (i + 1) * tm) for i in range(x_ref.shape[0] // tm)]

    def logits(rows):
        mixed = (jnp.dot(d_ref[rows, :], wo_ref[0:DIFF_WIDTH, :], preferred_element_type=F32)
                 + jnp.dot(f_ref[rows, :], wo_ref[DIFF_WIDTH:, :], preferred_element_type=F32))
        h = x_ref[rows, :] + mixed
        h_ref[rows, :] = h
        xn = _rms(h, g_ref[...])
        xn_ref[rows, :] = _pack_rows(xn)
        return lax.dot_general(wh, xn.astype(BF16), nt, preferred_element_type=F32) + brt_ref[...]

    def top_k(work):
        expert = lax.broadcasted_iota(I32, work.shape, 0)
        vals, ids, picks = [], [], []
        for _ in range(TOP_K):
            m = jnp.max(work, axis=0, keepdims=True)
            idx = jnp.min(jnp.where(work == m, expert, N_EXPERTS), axis=0, keepdims=True)
            pick = expert == idx
            vals.append(m)
            ids.append(idx)
            picks.append(pick)
            work = jnp.where(pick, -jnp.inf, work)
        exps = [jnp.exp(v - vals[0]) for v in vals]
        denom = exps[0] + exps[1] + exps[2] + exps[3]
        return ids, picks, [e / denom for e in exps]

    def ranks(picks):
        chosen = jnp.where(picks[0] | picks[1] | picks[2] | picks[3], 1.0, 0.0)
        rank = jnp.dot(chosen.astype(BF16), earlier, preferred_element_type=F32) + carry_ref[...]
        carry_ref[...] = carry_ref[...] + jnp.sum(chosen, axis=1, keepdims=True)
        return rank

    def emit(rows, ids, picks, gates, rank):
        slot = lax.broadcasted_iota(I32, (8, tm), 0)
        gate_t = jnp.zeros((8, tm), F32)
        idx_t = jnp.zeros((8, tm), I32)
        rank_t = jnp.zeros((8, tm), F32)
        for k in range(TOP_K):
            gate_t = jnp.where(slot == k, gates[k], gate_t)
            idx_t = jnp.where(slot == k, ids[k], idx_t)
            rank_t = jnp.where(slot == k, jnp.sum(jnp.where(picks[k], rank, 0.0), axis=0, keepdims=True),
                               rank_t)
        idx_ref[:, rows] = idx_t
        rank_ref[:, rows] = rank_t.astype(I32)
        gate_rows = jnp.concatenate([gate_t, jnp.zeros((LANES - 8, tm), F32)], axis=0)
        gate_ref[rows, :] = sum(lax.dot_general(eye, part, nt, preferred_element_type=F32)
                                for part in pieces(gate_rows, 3))

    work = [logits(rows) for rows in subs]
    chosen = [top_k(w) for w in work]
    rank = [ranks(picks) for _, picks, _ in chosen]
    for rows, (ids, picks, gates), r in zip(subs, chosen, rank):
        emit(rows, ids, picks, gates, r)
    count_ref[...] = jnp.broadcast_to(carry_ref[...], count_ref.shape)


def _router(d_out, f_out, x, w_out, g_moe, w_router_t, b_router_t):
    n, d = x.shape
    tm = ROUTER_TILE
    tile = lambda width: pl.BlockSpec((tm, width), lambda i: (i, 0))
    rows = pl.BlockSpec((8, tm), lambda i: (0, i))
    full = lambda arr: pl.BlockSpec(arr.shape, lambda i: (0,) * arr.ndim)
    return pl.pallas_call(
        _router_kernel,
        grid=(n // tm,),
        in_specs=[tile(DIFF_WIDTH), tile(FOX_WIDTH), tile(d), full(w_out), full(g_moe),
                  full(w_router_t), full(b_router_t)],
        out_specs=[tile(d), tile(d // 2), tile(LANES), rows, rows,
                   pl.BlockSpec((N_EXPERTS, LANES), lambda i: (0, 0))],
        out_shape=[jax.ShapeDtypeStruct((n, d), F32), jax.ShapeDtypeStruct((n, d // 2), U32),
                   jax.ShapeDtypeStruct((n, LANES), F32), jax.ShapeDtypeStruct((8, n), I32),
                   jax.ShapeDtypeStruct((8, n), I32), jax.ShapeDtypeStruct((N_EXPERTS, LANES), F32)],
        scratch_shapes=[pltpu.VMEM((N_EXPERTS, 1), F32)],
        compiler_params=_params("arbitrary"),
        name="router",
    )(d_out, f_out, x, w_out, g_moe, w_router_t, b_router_t)


def _slot_kernel(starts_ref, idx_ref, rank_ref, pos_ref):
    idx = idx_ref[...]
    pos = rank_ref[...]
    for e in range(N_EXPERTS):
        pos = jnp.where(idx == e, pos + starts_ref[e], pos)
    pos_ref[...] = pos


def _slots(starts, idx, rank):
    rows, n = idx.shape
    width = min(n, 2048)
    block = pl.BlockSpec((rows, width), lambda i, starts: (0, i))
    return pl.pallas_call(
        _slot_kernel,
        grid_spec=pltpu.PrefetchScalarGridSpec(num_scalar_prefetch=1, grid=(n // width,),
                                               in_specs=[block, block], out_specs=block),
        out_shape=jax.ShapeDtypeStruct((rows, n), I32),
        compiler_params=_params("arbitrary"),
        name="slots",
    )(starts, idx, rank)


SC_WINDOW = 128
SC_COLUMNS = 256


def _dispatch_sparsecore(pos, pad_rows, xn, n_rows):
    n, d = xn.shape
    mesh = plsc.VectorSubcoreMesh(core_axis_name="core", subcore_axis_name="subcore")
    spread = dict(core_axis_name=("core", "subcore"), dimension_semantics=(pltpu.PARALLEL,))

    @functools.partial(pl.kernel, out_type=jax.ShapeDtypeStruct((n_rows, d), xn.dtype), mesh=mesh,
                       scratch_types=[], name="dispatch_sc")
    def scatter(x_hbm, pos_hbm, zero_hbm, pad_hbm, xs_hbm):
        for c in range(d // SC_COLUMNS):
            columns = xs_hbm.at[:, pl.ds(c * SC_COLUMNS, SC_COLUMNS)]

            def body(x_vmem, pos_vmem, columns=columns):
                for k in range(TOP_K):
                    pltpu.sync_copy(x_vmem, columns.at[pos_vmem.at[k]])

            pltpu.emit_pipeline(
                body, grid=(n // SC_WINDOW,),
                in_specs=[pl.BlockSpec((SC_WINDOW, SC_COLUMNS), index_map=lambda i, c=c: (i, c)),
                          pl.BlockSpec((TOP_K, SC_WINDOW), index_map=lambda i: (0, i))],
                out_specs=[], **spread)(x_hbm, pos_hbm)

            def pad_body(zero_vmem, pad_vmem, columns=columns):
                pltpu.sync_copy(zero_vmem, columns.at[pad_vmem.at[0]])

            pltpu.emit_pipeline(
                pad_body, grid=(pad_rows.size // SC_WINDOW,),
                in_specs=[pl.BlockSpec((SC_WINDOW, SC_COLUMNS), index_map=lambda i: (0, 0)),
                          pl.BlockSpec((1, SC_WINDOW), index_map=lambda i: (0, i))],
                out_specs=[], **spread)(zero_hbm, pad_hbm)

    return scatter(xn, pos, jnp.zeros((SC_WINDOW, SC_COLUMNS), xn.dtype), pad_rows.reshape(1, -1))


def _expert_kernel(tile_expert_ref, n_used_ref, next_expert_ref, slot_ref,
                   x_ref, w1_ref, b1_ref, w2_ref, b2_ref, y_ref,
                   w1f_ref, w2f_ref, sem_ref, w1b_ref, w2p_ref, w2b_ref):
    t = pl.program_id(0)
    fresh = (t == 0) | (tile_expert_ref[t] != tile_expert_ref[jnp.maximum(t - 1, 0)])
    used = t < n_used_ref[0]

    def weight_copies(e, s):
        return (pltpu.make_async_copy(w1_ref.at[e], w1f_ref.at[s], sem_ref.at[0, s]),
                pltpu.make_async_copy(w2_ref.at[e], w2f_ref.at[s], sem_ref.at[1, s]))

    @pl.when(t == 0)
    def _():
        for copy in weight_copies(tile_expert_ref[0], slot_ref[0]):
            copy.start()

    @pl.when(fresh & used)
    def _():
        s = slot_ref[t]
        for copy in weight_copies(tile_expert_ref[t], s):
            copy.wait()

        @pl.when(next_expert_ref[t] >= 0)
        def _():
            for copy in weight_copies(next_expert_ref[t], 1 - s):
                copy.start()

        rows = LANES
        half = LANES // 2
        for r in range(0, w1b_ref.shape[0], rows):
            w1b_ref[r:r + rows, :] = w1f_ref[s, r:r + rows, :].astype(BF16)
        for c in range(w2p_ref.shape[0]):
            lanes = slice(c * LANES, (c + 1) * LANES)
            for r in range(0, w2b_ref.shape[0], rows):
                for b in range(2):
                    w2p_ref[c, pl.ds(r + b, half, stride=2), :] = (
                        w2f_ref[s, r + b * half:r + (b + 1) * half, lanes])
            w2b_ref[:, lanes] = w2p_ref[c].astype(BF16)

    @pl.when(used)
    def _():
        xb = _unpack_rows(x_ref[...]).astype(BF16)
        h = jnp.dot(xb, w1b_ref[...], preferred_element_type=F32) + b1_ref[0]
        even = lax.broadcasted_iota(I32, (1, LANES), 1) % 2 == 0
        parts = []
        for q in range(h.shape[1] // (2 * LANES)):
            a = h[:, (2 * q) * LANES:(2 * q + 1) * LANES]
            b = h[:, (2 * q + 1) * LANES:(2 * q + 2) * LANES]
            hg = jnp.where(even, a, pltpu.roll(b, 1, 1))
            hl = jnp.where(even, pltpu.roll(a, LANES - 1, 1), b)
            glu = jnp.minimum(hg, SWIGLU_LIMIT)
            lin = jnp.clip(hl, -SWIGLU_LIMIT, SWIGLU_LIMIT)
            parts.append((glu * jax.nn.sigmoid(SWIGLU_ALPHA * glu) * (lin + 1.0)).astype(BF16))
        act = jnp.concatenate(parts, axis=1)
        y_ref[...] = _pack_rows(jnp.dot(act, w2b_ref[...], preferred_element_type=F32) + b2_ref[0])

    @pl.when(jnp.logical_not(used))
    def _():
        y_ref[...] = jnp.zeros_like(y_ref)


def _experts(tile_expert, n_used, next_expert, slot, xs, w1, b1, w2, b2):
    n_rows, d = xs.shape
    tm = EXPERT_TILE
    by_expert = lambda arr: pl.BlockSpec((1,) + arr.shape[1:], lambda i, te, *_: (te[i], 0, 0))
    rows = pl.BlockSpec((tm, d), lambda i, *_: (i, 0))
    used_rows = pl.BlockSpec((tm, d), lambda i, te, nu, *_: (jnp.minimum(i, nu[0] - 1), 0))
    hbm = pl.BlockSpec(memory_space=pl.ANY)
    return pl.pallas_call(
        _expert_kernel,
        grid_spec=pltpu.PrefetchScalarGridSpec(
            num_scalar_prefetch=4,
            grid=(n_rows // tm,),
            in_specs=[used_rows, hbm, by_expert(b1), hbm, by_expert(b2)],
            out_specs=rows,
            scratch_shapes=[pltpu.VMEM((2,) + w1.shape[1:], F32), pltpu.VMEM((2,) + w2.shape[1:], F32),
                            pltpu.SemaphoreType.DMA((2, 2)),
                            pltpu.VMEM(w1.shape[1:], BF16),
                            pltpu.VMEM((w2.shape[2] // LANES, w2.shape[1], LANES), F32),
                            pltpu.VMEM(w2.shape[1:], BF16)]),
        out_shape=jax.ShapeDtypeStruct(xs.shape, xs.dtype),
        compiler_params=pltpu.CompilerParams(dimension_semantics=("arbitrary",),
                                             vmem_limit_bytes=EXPERT_VMEM_LIMIT),
        name="experts",
    )(tile_expert, n_used, next_expert, slot, xs, w1, b1, w2, b2)


def _gather_sparsecore(pos, y):
    d = y.shape[1]
    total = pos.size
    mesh = plsc.VectorSubcoreMesh(core_axis_name="core", subcore_axis_name="subcore")

    @functools.partial(pl.kernel, out_type=jax.ShapeDtypeStruct((total, d), y.dtype), mesh=mesh,
                       scratch_types=[], name="gather_sc")
    def gather(y_hbm, pos_hbm, out_hbm):
        for c in range(d // SC_COLUMNS):
            def body(pos_vmem, out_vmem, c=c):
                pltpu.sync_copy(y_hbm.at[:, pl.ds(c * SC_COLUMNS, SC_COLUMNS)].at[pos_vmem.at[0]], out_vmem)

            pltpu.emit_pipeline(
                body, grid=(total // SC_WINDOW,),
                in_specs=[pl.BlockSpec((1, SC_WINDOW), index_map=lambda i: (0, i))],
                out_specs=[pl.BlockSpec((SC_WINDOW, SC_COLUMNS), index_map=lambda i, c=c: (i, c))],
                core_axis_name=("core", "subcore"),
                dimension_semantics=(pltpu.PARALLEL,))(pos_hbm, out_hbm)

    return gather(y, pos.reshape(1, -1))


def _combine_kernel(h_ref, gate_ref, p_ref, wg_ref, wp_ref, gple_ref, gfin_ref, *refs, final_norm):
    y_refs, o_ref = refs[:TOP_K], refs[TOP_K]
    ts = COMBINE_SUBTILE
    subs = [slice(i * ts, (i + 1) * ts) for i in range(h_ref.shape[0] // ts)]

    def mix(rows):
        gates = gate_ref[rows, :]
        moe = gates[:, 0:1] * _unpack_rows(y_refs[0][rows, :])
        for k in range(1, TOP_K):
            moe = moe + gates[:, k:k + 1] * _unpack_rows(y_refs[k][rows, :])
        h = h_ref[rows, :] + moe
        logit = jnp.dot(_rms(h, gple_ref[...]).astype(BF16), wg_ref[...], preferred_element_type=F32)
        emb = jnp.dot(p_ref[rows, :].astype(BF16), wp_ref[...], preferred_element_type=F32)
        return h, logit, emb

    mixed = [mix(rows) for rows in subs]
    for rows, (h, logit, emb) in zip(subs, mixed):
        h = h + jax.nn.sigmoid(logit) * emb
        o_ref[rows, :] = _rms(h, gfin_ref[...]) if final_norm else h


def _combine(h, gates, p, w_gate, w_proj, g_ple, g_final, y_tok, final_norm):
    n, d = h.shape
    tm = TOKEN_TILE
    tile = lambda width: pl.BlockSpec((tm, width), lambda i: (i, 0))
    full = lambda arr: pl.BlockSpec(arr.shape, lambda i: (0,) * arr.ndim)
    choice = lambda k: pl.BlockSpec((tm, y_tok.shape[1]), lambda i: (k * (n // tm) + i, 0))
    return pl.pallas_call(
        functools.partial(_combine_kernel, final_norm=final_norm),
        grid=(n // tm,),
        in_specs=[tile(d), tile(LANES), tile(p.shape[1]), full(w_gate), full(w_proj),
                  full(g_ple), full(g_final)] + [choice(k) for k in range(TOP_K)],
        out_specs=tile(d),
        out_shape=jax.ShapeDtypeStruct((n, d), F32),
        compiler_params=_params("arbitrary"),
        name="combine",
    )(h, gates, p, w_gate, w_proj, g_ple, g_final, *([y_tok] * TOP_K))


def _layer(h, p, positions, g_attn, w_in, lq1, lk1, lq2, lk2, g_subln, b_forget, w_out, g_moe,
           w_router, b_router, w_e1, b_e1, w_e2, b_e2, g_ple, w_ple_gate, w_ple_proj, g_final,
           lambda_init, final_norm):
    b, s, d = h.shape
    n = b * s
    row = lambda v: v.reshape(1, -1)

    w_main = w_in[:, :MAIN_WIDTH].astype(BF16)
    pad_heads = ((0, BF16_TILE_ROWS - N_FOX_HEADS), (0, 0))
    w_fz_t = jnp.pad(w_in[:, MAIN_WIDTH:].T, pad_heads).astype(BF16)
    b_fz_t = jnp.pad(b_forget.reshape(-1, 1), pad_heads)
    b1 = b_e1[:, None, :]
    b2 = b_e2[:, None, :]

    dq_t, dk, dv_t, fq_t, fk, fv_t, c, c_t = _inproj(positions, h, row(g_attn), w_main, w_fz_t, b_fz_t)
    c_rows = c_t.reshape(b, N_FOX_HEADS // 2, 2, s)
    d_out, f_out = _attention(dq_t, dk, dv_t, fq_t, fk, fv_t, c_rows, c, row(lq1), row(lk1), row(lq2),
                              row(lk2), g_subln.reshape(-1, 1), lambda_init)

    h1, xn, gates, idx, rank, counts = _router(
        d_out.reshape(n, DIFF_WIDTH), f_out.reshape(n, FOX_WIDTH), h.reshape(n, d),
        w_out.astype(BF16), row(g_moe), w_router.T, b_router.reshape(-1, 1))

    te = EXPERT_TILE
    n_rows = n * TOP_K + N_EXPERTS * te
    counts = counts[:, 0].astype(I32)
    padded = (counts + te - 1) // te * te
    ends = jnp.cumsum(padded)
    starts = ends - padded
    pos = _slots(starts.astype(I32), idx, rank)[:TOP_K]
    tile_start = jnp.arange(n_rows // te, dtype=I32) * te
    tile_expert = jnp.minimum(jnp.sum((ends[None, :] <= tile_start[:, None]).astype(I32), axis=1),
                              N_EXPERTS - 1)
    n_used = (ends[-1:] // te).astype(I32)
    owns = padded > 0
    later = lax.cummin(jnp.where(owns, jnp.arange(N_EXPERTS, dtype=I32), N_EXPERTS), reverse=True)
    following = jnp.concatenate([later[1:], jnp.full((1,), N_EXPERTS, I32)])
    following = jnp.where(following < N_EXPERTS, following, -1)
    owner = tile_expert[:, None] == jnp.arange(N_EXPERTS, dtype=I32)[None, :]
    next_expert = jnp.sum(jnp.where(owner, following[None, :], 0), axis=1)
    slot = jnp.sum(jnp.where(owner, ((jnp.cumsum(owns.astype(I32)) - 1) % 2)[None, :], 0), axis=1)

    pad_count = padded - counts
    pad_total = jnp.sum(pad_count)
    pad_before = jnp.cumsum(pad_count) - pad_count
    entry = jnp.arange(N_EXPERTS * te, dtype=I32)
    which = entry % jnp.maximum(pad_total, 1)
    in_or_before = which[:, None] >= pad_before[None, :]
    expert_of = jnp.sum(in_or_before.astype(I32), axis=1) - 1
    mine = expert_of[:, None] == jnp.arange(N_EXPERTS, dtype=I32)[None, :]
    pad_rows = jnp.sum(jnp.where(mine, (starts + counts - pad_before)[None, :], 0), axis=1) + which
    pad_rows = jnp.where(pad_total > 0, pad_rows, n_rows - 1 - entry % N_EXPERTS)
    xs = _dispatch_sparsecore(pos.astype(I32), pad_rows.astype(I32), xn, n_rows)
    y = _experts(tile_expert, n_used, next_expert.astype(I32), slot.astype(I32), xs, w_e1, b1, w_e2, b2)
    y_tok = _gather_sparsecore(pos.astype(I32), y)
    out = _combine(h1, gates, p.reshape(n, PLE_DIM), w_ple_gate.astype(BF16),
                   w_ple_proj.astype(BF16), row(g_ple), row(g_final), y_tok, final_norm)
    return out.reshape(b, s, d)


def kernel(x, p, positions, g_attn, w_in, lambda_q1, lambda_k1, lambda_q2, lambda_k2, g_subln, b_forget, w_out, g_moe, w_router, b_router, w_e1, b_e1, w_e2, b_e2, g_ple, w_ple_gate, w_ple_proj, g_final):
    depth = g_attn.shape[0]
    h = x
    for i in range(depth):
        lambda_init = 0.8 - 0.6 * math.exp(-0.3 * i)
        h = _layer(h, p[i], positions, g_attn[i], w_in[i], lambda_q1[i], lambda_k1[i],
                   lambda_q2[i], lambda_k2[i], g_subln[i], b_forget[i], w_out[i], g_moe[i],
                   w_router[i], b_router[i], w_e1[i], b_e1[i], w_e2[i], b_e2[i], g_ple[i],
                   w_ple_gate[i], w_ple_proj[i], g_final, lambda_init, i == depth - 1)
    return h
```

```python
import functools
import math

import jax
import jax.numpy as jnp
from jax import lax
from jax.experimental import pallas as pl
from jax.experimental.pallas import tpu as pltpu
from jax.experimental.pallas import tpu_sc as plsc

F32 = jnp.float32
BF16 = jnp.bfloat16
I32 = jnp.int32
U32 = jnp.uint32

LANES = 128
BF16_TILE_ROWS = 16
D_MODEL = 1024
HEAD_DIM = 64
N_DIFF_HEADS = 4
DIFF_V_DIM = 2 * HEAD_DIM
N_FOX_HEADS = 8
DIFF_WIDTH = N_DIFF_HEADS * DIFF_V_DIM
FOX_WIDTH = N_FOX_HEADS * HEAD_DIM
ROT_DIM = HEAD_DIM // 4
ROPE_THETA = 500000.0
N_EXPERTS = 32
TOP_K = 4
D_EXPERT = D_MODEL
SWIGLU_ALPHA = 1.702
SWIGLU_LIMIT = 7.0
PLE_DIM = 256
NORM_EPS = 1e-5
LOG2_E = math.log2(math.e)
MAIN_WIDTH = 3 * DIFF_WIDTH + 3 * FOX_WIDTH

SEQ_TILE = 1024
SEQ_SUBTILE = 256
ATTN_TILE = 256
ATTN_UNITS = 2
TOKEN_TILE = 1024
COMBINE_SUBTILE = 256
ROUTER_TILE = 1024
ROUTER_SUBTILE = 256
EXPERT_TILE = 256
VMEM_LIMIT = 48 * 1024 * 1024
EXPERT_VMEM_LIMIT = 58 * 1024 * 1024


def _params(*semantics):
    return pltpu.CompilerParams(dimension_semantics=semantics, vmem_limit_bytes=VMEM_LIMIT)


def _rms(x, g):
    return x * lax.rsqrt(jnp.mean(x * x, axis=-1, keepdims=True) + NORM_EPS) * g


def _pack_rows(v):
    half = v.shape[1] // 2
    bits = pltpu.bitcast(v.astype(BF16).astype(F32), U32)
    return (bits[:, :half] >> 16) | bits[:, half:]


def _unpack_rows(words):
    low = pltpu.bitcast(words << 16, F32)
    high = pltpu.bitcast(words & jnp.uint32(0xFFFF0000), F32)
    return jnp.concatenate([low, high], axis=1)


def _inproj_kernel(pos_ref, x_ref, g_ref, w_ref, wfzt_ref, bfzt_ref,
                   dqt_ref, dk_ref, dvt_ref, fqt_ref, fk_ref, fvt_ref, c_ref, ct_ref, carry_t_ref):
    @pl.when(pl.program_id(1) == 0)
    def _():
        carry_t_ref[...] = jnp.zeros_like(carry_t_ref)

    ts = SEQ_SUBTILE
    subs = [slice(i * ts, (i + 1) * ts) for i in range(x_ref.shape[1] // ts)]
    xbs = [_rms(x_ref[0, rows, :], g_ref[...]).astype(BF16) for rows in subs]
    nt = (((1,), (1,)), ((), ()))
    tn = (((0,), (0,)), ((), ()))

    def pieces(v):
        hi = v.astype(BF16)
        r1 = v - hi.astype(F32)
        mid = r1.astype(BF16)
        return hi, mid, (r1 - mid.astype(F32)).astype(BF16)

    def rows_to_lanes(v, select):
        return sum(lax.dot_general(part, select, tn, preferred_element_type=F32) for part in pieces(v))

    half_rot = ROT_DIM // 2
    freq = lax.broadcasted_iota(I32, (half_rot, 1), 0)
    inv_freq = jnp.power(ROPE_THETA, -(2 * freq).astype(F32) / ROT_DIM)
    r = lax.broadcasted_iota(I32, (2 * half_rot, 2 * LANES), 0)
    lane2 = lax.broadcasted_iota(I32, (2 * half_rot, 2 * LANES), 1)
    d = lane2 % HEAD_DIM
    same_freq = jnp.where(d % half_rot == r % half_rot, 1.0, 0.0)
    cos_part = jnp.where((r < half_rot) & (lane2 < LANES) & (d < ROT_DIM), same_freq, 0.0)
    sin_part = jnp.where((r >= half_rot) & (lane2 >= LANES) & (d < ROT_DIM),
                         jnp.where(d < half_rot, -same_freq, same_freq), 0.0)
    expand = (cos_part + sin_part).astype(BF16)
    lane = lax.broadcasted_iota(I32, (1, LANES), 1) % HEAD_DIM
    first_half = lane < half_rot

    def rope_tables(rows):
        ang = inv_freq * pos_ref[0, :, rows].astype(F32)
        trig = jnp.concatenate([jnp.cos(ang), jnp.sin(ang)], axis=0)
        tables = rows_to_lanes(trig, expand)
        return jnp.where(lane < ROT_DIM, tables[:, :LANES], 1.0), tables[:, LANES:]

    def rope(t, cosf, sinf):
        outs = []
        for j in range(t.shape[1] // LANES):
            tj = t[:, j * LANES:(j + 1) * LANES]
            partner = jnp.where(first_half,
                                pltpu.roll(tj, LANES - ROT_DIM // 2, 1),
                                pltpu.roll(tj, ROT_DIM // 2, 1))
            outs.append(tj * cosf + partner * sinf)
        return jnp.concatenate(outs, axis=1)

    tables = [rope_tables(rows) for rows in subs]

    def log_sigmoid(z):
        return jnp.minimum(z, 0.0) - jnp.log1p(jnp.exp(-jnp.abs(z)))

    row = lax.broadcasted_iota(I32, (ts, ts), 0)
    col = lax.broadcasted_iota(I32, (ts, ts), 1)
    upto_col = jnp.where(row <= col, 1.0, 0.0).astype(BF16)
    heads = wfzt_ref.shape[0]
    head = lax.broadcasted_iota(I32, (heads, FOX_WIDTH), 0)
    group = lax.broadcasted_iota(I32, (heads, FOX_WIDTH), 1) // HEAD_DIM
    head_lanes = jnp.where(head == group, 1.0, 0.0).astype(BF16)

    for rows, xb in zip(subs, xbs):
        fzt = lax.dot_general(wfzt_ref[...], xb, nt, preferred_element_type=F32) + bfzt_ref[...]
        ct = carry_t_ref[...] + sum(jnp.dot(part, upto_col, preferred_element_type=F32)
                                    for part in pieces(log_sigmoid(fzt)))
        carry_t_ref[...] = ct[:, ts - 1:ts]
        ct = ct * LOG2_E
        ct_ref[0, :, rows] = ct[0:N_FOX_HEADS, :]
        c_ref[0, rows, :] = rows_to_lanes(ct, head_lanes)

    scale = HEAD_DIM ** -0.5 * LOG2_E
    w = DIFF_WIDTH
    projs = [jnp.dot(xb, w_ref[...], preferred_element_type=F32) for xb in xbs]
    for i, (rows, proj, (cosf, sinf)) in enumerate(zip(subs, projs, tables)):
        dqt_ref[0, :, rows] = (rope(proj[:, 0:w], cosf, sinf) * scale).astype(BF16).T
        dk_ref[0, rows, :] = rope(proj[:, w:2 * w], cosf, sinf).astype(BF16)
        dvt_ref[0, i] = proj[:, 2 * w:3 * w].astype(BF16).T
        fqt_ref[0, :, rows] = (proj[:, 3 * w:4 * w] * scale).astype(BF16).T
        fk_ref[0, rows, :] = proj[:, 4 * w:5 * w].astype(BF16)
        fvt_ref[0, i] = proj[:, 5 * w:6 * w].astype(BF16).T


def _inproj(positions, x, g, w_main, w_fz_t, b_fz_t):
    b, s, d = x.shape
    ts = SEQ_TILE
    sub = SEQ_SUBTILE
    assert sub == ATTN_TILE
    tile = lambda width: pl.BlockSpec((1, ts, width), lambda i, j: (i, j, 0))
    tile_t = pl.BlockSpec((1, DIFF_WIDTH, ts), lambda i, j: (i, 0, j))
    key_tile_t = pl.BlockSpec((1, ts // sub, DIFF_WIDTH, sub), lambda i, j: (i, j, 0, 0))
    full = lambda arr: pl.BlockSpec(arr.shape, lambda i, j: (0,) * arr.ndim)
    rows = jax.ShapeDtypeStruct((b, s, DIFF_WIDTH), BF16)
    rows_t = jax.ShapeDtypeStruct((b, DIFF_WIDTH, s), BF16)
    key_tiles_t = jax.ShapeDtypeStruct((b, s // sub, DIFF_WIDTH, sub), BF16)
    return pl.pallas_call(
        _inproj_kernel,
        grid=(b, s // ts),
        in_specs=[pl.BlockSpec((1, 1, ts), lambda i, j: (i, 0, j)), tile(d), full(g), full(w_main),
                  full(w_fz_t), full(b_fz_t)],
        out_specs=[tile_t, tile(DIFF_WIDTH), key_tile_t, tile_t, tile(FOX_WIDTH), key_tile_t,
                   tile(FOX_WIDTH), pl.BlockSpec((1, N_FOX_HEADS, ts), lambda i, j: (i, 0, j))],
        out_shape=[rows_t, rows, key_tiles_t, rows_t, rows, key_tiles_t,
                   jax.ShapeDtypeStruct((b, s, FOX_WIDTH), F32),
                   jax.ShapeDtypeStruct((b, N_FOX_HEADS, s), F32)],
        scratch_shapes=[pltpu.VMEM((w_fz_t.shape[0], 1), F32)],
        compiler_params=_params("arbitrary", "arbitrary"),
        name="inproj",
    )(positions.reshape(b, 1, s), x, g, w_main, w_fz_t, b_fz_t)


ONES_ROWS = 16


def _block_diag_queries(qt):
    row = lax.broadcasted_iota(I32, qt.shape, 0)
    zero = jnp.zeros_like(qt)
    return jnp.concatenate([jnp.where(row < HEAD_DIM, qt, zero),
                            jnp.where(row >= HEAD_DIM, qt, zero)], axis=1)


class _Chain:
    def __init__(self, qbd, keys, values_t, scratch, key_bias=None, query_bias=None):
        self.qbd, self.keys, self.values_t = qbd, keys, values_t
        self.acc_ref, self.s_ref, self.p_ref, self.stat_ref = scratch
        self.key_bias, self.query_bias = key_bias, query_bias


_RUNNING_MAX, _TILE_MAX, _RESCALE = 0, 1, 2


def _causal_attention(chains, qi):
    tk, tq = chains[0].s_ref.shape[0], chains[0].s_ref.shape[1] // 2
    ones = jnp.ones((ONES_ROWS, tk), BF16)

    def scores(ch, i, diagonal):
        s = jnp.dot(ch.keys(i), ch.qbd, preferred_element_type=F32)
        if ch.key_bias is not None:
            s = s - ch.key_bias(i)
        if diagonal is not False:
            r = lax.broadcasted_iota(I32, s.shape, 0)
            c = lax.broadcasted_iota(I32, s.shape, 1) & (tq - 1)
            visible = r <= c if diagonal is True else jnp.logical_or(r <= c, jnp.logical_not(diagonal))
            s = jnp.where(visible, s, -jnp.inf)
        ch.s_ref[...] = s
        top = jnp.max(s, axis=0, keepdims=True)
        if ch.query_bias is not None:
            top = top + ch.query_bias
        ch.stat_ref[_TILE_MAX:_TILE_MAX + 1, :] = top

    def probs(ch):
        m = ch.stat_ref[_RUNNING_MAX:_RUNNING_MAX + 1, :]
        m_new = jnp.maximum(m, ch.stat_ref[_TILE_MAX:_TILE_MAX + 1, :])
        ch.stat_ref[_RUNNING_MAX:_RUNNING_MAX + 1, :] = m_new
        ch.stat_ref[_RESCALE:_RESCALE + 1, :] = jnp.exp2(m - m_new)
        ref = m_new if ch.query_bias is None else m_new - ch.query_bias
        ch.p_ref[...] = jnp.exp2(ch.s_ref[...] - ref).astype(BF16)

    def accumulate(ch, i):
        vts = jnp.concatenate([ch.values_t(i), ones], axis=0)
        ch.acc_ref[...] = (ch.stat_ref[_RESCALE:_RESCALE + 1, :] * ch.acc_ref[...]
                           + jnp.dot(vts, ch.p_ref[...], preferred_element_type=F32))

    def body(k, diagonal_next):
        for ch in chains:
            accumulate(ch, jnp.maximum(k - 1, 0))
        for ch in chains:
            probs(ch)
        for ch in chains:
            scores(ch, k + 1, diagonal_next)

    for ch in chains:
        ch.acc_ref[...] = jnp.zeros_like(ch.acc_ref)
        ch.p_ref[...] = jnp.zeros_like(ch.p_ref)
        ch.stat_ref[...] = jnp.concatenate(
            [jnp.full((1, 2 * tq), -jnp.inf, F32), jnp.ones((ch.stat_ref.shape[0] - 1, 2 * tq), F32)], axis=0)
        scores(ch, 0, qi == 0)

    def loop_body(k, carry):
        body(k, False)
        return carry

    lax.fori_loop(0, qi - 1, loop_body, 0)

    @pl.when(qi >= 1)
    def _():
        body(qi - 1, True)

    for ch in chains:
        accumulate(ch, jnp.maximum(qi - 1, 0))
    for ch in chains:
        probs(ch)
    for ch in chains:
        accumulate(ch, qi)


def _attention_kernel(lq1_ref, lk1_ref, lq2_ref, lk2_ref, g_ref,
                      dqt_ref, dk_ref, dvt_ref, fqt_ref, fk_ref, fvt_ref, cq_ref, ck_ref,
                      do_ref, fo_ref, *scratch, lambda_init):
    tq = dqt_ref.shape[2]
    tk = dvt_ref.shape[3]
    units = dqt_ref.shape[1] // LANES
    lane = lax.broadcasted_iota(I32, (1, LANES), 1)
    scratch = [scratch[i:i + 4] for i in range(0, len(scratch), 4)]

    def unit_chains(u):
        lanes = slice(u * LANES, (u + 1) * LANES)

        def tile(ref):
            return lambda i: ref[0, pl.ds(pl.multiple_of(i * tk, tk), tk), lanes]

        def tile_t(ref):
            return lambda i: ref[0, i, lanes, :]

        cq = jnp.concatenate([cq_ref[0, u, 0:1, :], cq_ref[0, u, 1:2, :]], axis=1)

        def key_bias(i):
            ck = tile(ck_ref)(i)
            other = pltpu.roll(ck, HEAD_DIM, 1)
            first = jnp.where(lane < HEAD_DIM, ck, other)
            second = jnp.where(lane >= HEAD_DIM, ck, other)
            return jnp.concatenate([first] * (tq // LANES) + [second] * (tq // LANES), axis=1)

        return [_Chain(_block_diag_queries(dqt_ref[0, lanes, :]), tile(dk_ref), tile_t(dvt_ref),
                       scratch[2 * u]),
                _Chain(_block_diag_queries(fqt_ref[0, lanes, :]), tile(fk_ref), tile_t(fvt_ref),
                       scratch[2 * u + 1], key_bias, cq)]

    chains = [ch for u in range(units) for ch in unit_chains(u)]
    _causal_attention(chains, pl.program_id(2))

    lam = (jnp.exp(jnp.sum(lq1_ref[...] * lk1_ref[...], axis=-1, keepdims=True))
           - jnp.exp(jnp.sum(lq2_ref[...] * lk2_ref[...], axis=-1, keepdims=True)) + lambda_init)
    dv, h = DIFF_V_DIM, HEAD_DIM
    for u in range(units):
        lanes = slice(u * LANES, (u + 1) * LANES)
        acc = chains[2 * u].acc_ref
        o = (acc[0:dv, 0:tq] / acc[dv:dv + 1, 0:tq]
             - lam * (acc[0:dv, tq:] / acc[dv:dv + 1, tq:]))
        ms = jnp.mean(o * o, axis=0, keepdims=True)
        do_ref[0, :, lanes] = (o * lax.rsqrt(ms + NORM_EPS) * g_ref[...]
                               * (1.0 - lambda_init)).astype(BF16).T
        acc = chains[2 * u + 1].acc_ref
        fo_ref[0, :, lanes] = jnp.concatenate([acc[0:h, 0:tq] / acc[2 * h:2 * h + 1, 0:tq],
                                               acc[h:2 * h, tq:] / acc[2 * h:2 * h + 1, tq:]],
                                              axis=0).astype(BF16).T


def _attention(dqt, dk, dvt, fqt, fk, fvt, c_rows, c_cols, lq1, lk1, lq2, lk2, g_col, lambda_init):
    b, s, _ = dk.shape
    t = ATTN_TILE
    assert N_DIFF_HEADS == N_FOX_HEADS // 2 and N_DIFF_HEADS % ATTN_UNITS == 0
    width = ATTN_UNITS * LANES
    vec = lambda arr: pl.BlockSpec(arr.shape, lambda i, h, j: (0, 0))
    q_spec = pl.BlockSpec((1, width, t), lambda i, h, j: (i, h, j))
    k_spec = pl.BlockSpec((1, s, width), lambda i, h, j: (i, 0, h))
    v_spec = pl.BlockSpec((1, s // t, width, t), lambda i, h, j: (i, 0, h, 0))
    o_spec = pl.BlockSpec((1, t, width), lambda i, h, j: (i, j, h))
    out = jax.ShapeDtypeStruct((b, s, DIFF_WIDTH), BF16)
    chain_scratch = [pltpu.VMEM((LANES + ONES_ROWS, 2 * t), F32),
                     pltpu.VMEM((t, 2 * t), F32),
                     pltpu.VMEM((t, 2 * t), BF16),
                     pltpu.VMEM((8, 2 * t), F32)]
    return pl.pallas_call(
        functools.partial(_attention_kernel, lambda_init=lambda_init),
        grid=(b, N_DIFF_HEADS // ATTN_UNITS, s // t),
        in_specs=[vec(lq1), vec(lk1), vec(lq2), vec(lk2), vec(g_col),
                  q_spec, k_spec, v_spec, q_spec, k_spec, v_spec,
                  pl.BlockSpec((1, ATTN_UNITS, 2, t), lambda i, h, j: (i, h, 0, j)), k_spec],
        out_specs=[o_spec, o_spec],
        out_shape=[out, out],
        scratch_shapes=chain_scratch * (2 * ATTN_UNITS),
        compiler_params=_params("arbitrary", "arbitrary", "arbitrary"),
        name="attention",
    )(lq1, lk1, lq2, lk2, g_col, dqt, dk, dvt, fqt, fk, fvt, c_rows, c_cols)


def _router_kernel(d_ref, f_ref, x_ref, wo_ref, g_ref, wrt_ref, brt_ref,
                   h_ref, xn_ref, gate_ref, idx_ref, rank_ref, count_ref, carry_ref):
    tm = ROUTER_SUBTILE
    nt = (((1,), (1,)), ((), ()))

    def pieces(v, count):
        out = []
        for _ in range(count):
            part = v.astype(BF16)
            out.append(part)
            v = v - part.astype(F32)
        return out

    wh = wrt_ref[...].astype(BF16)
    row = lax.broadcasted_iota(I32, (tm, tm), 0)
    col = lax.broadcasted_iota(I32, (tm, tm), 1)
    earlier = jnp.where(row < col, 1.0, 0.0).astype(BF16)
    eye = jnp.where(row == col, 1.0, 0.0).astype(BF16)

    @pl.when(pl.program_id(0) == 0)
    def _():
        carry_ref[...] = jnp.zeros_like(carry_ref)

    subs = [slice(i * tm, (i + 1) * tm) for i in range(x_ref.shape[0] // tm)]

    def logits(rows):
        mixed = (jnp.dot(d_ref[rows, :], wo_ref[0:DIFF_WIDTH, :], preferred_element_type=F32)
                 + jnp.dot(f_ref[rows, :], wo_ref[DIFF_WIDTH:, :], preferred_element_type=F32))
        h = x_ref[rows, :] + mixed
        h_ref[rows, :] = h
        xn = _rms(h, g_ref[...])
        xn_ref[rows, :] = _pack_rows(xn)
        return lax.dot_general(wh, xn.astype(BF16), nt, preferred_element_type=F32) + brt_ref[...]

    def top_k(work):
        expert = lax.broadcasted_iota(I32, work.shape, 0)
        vals, ids, picks = [], [], []
        for _ in range(TOP_K):
            m = jnp.max(work, axis=0, keepdims=True)
            idx = jnp.min(jnp.where(work == m, expert, N_EXPERTS), axis=0, keepdims=True)
            pick = expert == idx
            vals.append(m)
            ids.append(idx)
            picks.append(pick)
            work = jnp.where(pick, -jnp.inf, work)
        exps = [jnp.exp(v - vals[0]) for v in vals]
        denom = exps[0] + exps[1] + exps[2] + exps[3]
        return ids, picks, [e / denom for e in exps]

    def ranks(picks):
        chosen = jnp.where(picks[0] | picks[1] | picks[2] | picks[3], 1.0, 0.0)
        rank = jnp.dot(chosen.astype(BF16), earlier, preferred_element_type=F32) + carry_ref[...]
        carry_ref[...] = carry_ref[...] + jnp.sum(chosen, axis=1, keepdims=True)
        return rank

    def emit(rows, ids, picks, gates, rank):
        slot = lax.broadcasted_iota(I32, (8, tm), 0)
        gate_t = jnp.zeros((8, tm), F32)
        idx_t = jnp.zeros((8, tm), I32)
        rank_t = jnp.zeros((8, tm), F32)
        for k in range(TOP_K):
            gate_t = jnp.where(slot == k, gates[k], gate_t)
            idx_t = jnp.where(slot == k, ids[k], idx_t)
            rank_t = jnp.where(slot == k, jnp.sum(jnp.where(picks[k], rank, 0.0), axis=0, keepdims=True),
                               rank_t)
        idx_ref[:, rows] = idx_t
        rank_ref[:, rows] = rank_t.astype(I32)
        gate_rows = jnp.concatenate([gate_t, jnp.zeros((LANES - 8, tm), F32)], axis=0)
        gate_ref[rows, :] = sum(lax.dot_general(eye, part, nt, preferred_element_type=F32)
                                for part in pieces(gate_rows, 3))

    work = [logits(rows) for rows in subs]
    chosen = [top_k(w) for w in work]
    rank = [ranks(picks) for _, picks, _ in chosen]
    for rows, (ids, picks, gates), r in zip(subs, chosen, rank):
        emit(rows, ids, picks, gates, r)
    count_ref[...] = jnp.broadcast_to(carry_ref[...], count_ref.shape)


def _router(d_out, f_out, x, w_out, g_moe, w_router_t, b_router_t):
    n, d = x.shape
    tm = ROUTER_TILE
    tile = lambda width: pl.BlockSpec((tm, width), lambda i: (i, 0))
    rows = pl.BlockSpec((8, tm), lambda i: (0, i))
    full = lambda arr: pl.BlockSpec(arr.shape, lambda i: (0,) * arr.ndim)
    return pl.pallas_call(
        _router_kernel,
        grid=(n // tm,),
        in_specs=[tile(DIFF_WIDTH), tile(FOX_WIDTH), tile(d), full(w_out), full(g_moe),
                  full(w_router_t), full(b_router_t)],
        out_specs=[tile(d), tile(d // 2), tile(LANES), rows, rows,
                   pl.BlockSpec((N_EXPERTS, LANES), lambda i: (0, 0))],
        out_shape=[jax.ShapeDtypeStruct((n, d), F32), jax.ShapeDtypeStruct((n, d // 2), U32),
                   jax.ShapeDtypeStruct((n, LANES), F32), jax.ShapeDtypeStruct((8, n), I32),
                   jax.ShapeDtypeStruct((8, n), I32), jax.ShapeDtypeStruct((N_EXPERTS, LANES), F32)],
        scratch_shapes=[pltpu.VMEM((N_EXPERTS, 1), F32)],
        compiler_params=_params("arbitrary"),
        name="router",
    )(d_out, f_out, x, w_out, g_moe, w_router_t, b_router_t)


def _slot_kernel(starts_ref, idx_ref, rank_ref, pos_ref):
    idx = idx_ref[...]
    pos = rank_ref[...]
    for e in range(N_EXPERTS):
        pos = jnp.where(idx == e, pos + starts_ref[e], pos)
    pos_ref[...] = pos


def _slots(starts, idx, rank):
    rows, n = idx.shape
    width = min(n, 2048)
    block = pl.BlockSpec((rows, width), lambda i, starts: (0, i))
    return pl.pallas_call(
        _slot_kernel,
        grid_spec=pltpu.PrefetchScalarGridSpec(num_scalar_prefetch=1, grid=(n // width,),
                                               in_specs=[block, block], out_specs=block),
        out_shape=jax.ShapeDtypeStruct((rows, n), I32),
        compiler_params=_params("arbitrary"),
        name="slots",
    )(starts, idx, rank)


SC_WINDOW = 128
SC_COLUMNS = 256


def _dispatch_sparsecore(pos, pad_rows, xn, n_rows):
    n, d = xn.shape
    mesh = plsc.VectorSubcoreMesh(core_axis_name="core", subcore_axis_name="subcore")
    spread = dict(core_axis_name=("core", "subcore"), dimension_semantics=(pltpu.PARALLEL,))

    @functools.partial(pl.kernel, out_type=jax.ShapeDtypeStruct((n_rows, d), xn.dtype), mesh=mesh,
                       scratch_types=[], name="dispatch_sc")
    def scatter(x_hbm, pos_hbm, zero_hbm, pad_hbm, xs_hbm):
        for c in range(d // SC_COLUMNS):
            columns = xs_hbm.at[:, pl.ds(c * SC_COLUMNS, SC_COLUMNS)]

            def body(x_vmem, pos_vmem, columns=columns):
                for k in range(TOP_K):
                    pltpu.sync_copy(x_vmem, columns.at[pos_vmem.at[k]])

            pltpu.emit_pipeline(
                body, grid=(n // SC_WINDOW,),
                in_specs=[pl.BlockSpec((SC_WINDOW, SC_COLUMNS), index_map=lambda i, c=c: (i, c)),
                          pl.BlockSpec((TOP_K, SC_WINDOW), index_map=lambda i: (0, i))],
                out_specs=[], **spread)(x_hbm, pos_hbm)

            def pad_body(zero_vmem, pad_vmem, columns=columns):
                pltpu.sync_copy(zero_vmem, columns.at[pad_vmem.at[0]])

            pltpu.emit_pipeline(
                pad_body, grid=(pad_rows.size // SC_WINDOW,),
                in_specs=[pl.BlockSpec((SC_WINDOW, SC_COLUMNS), index_map=lambda i: (0, 0)),
                          pl.BlockSpec((1, SC_WINDOW), index_map=lambda i: (0, i))],
                out_specs=[], **spread)(zero_hbm, pad_hbm)

    return scatter(xn, pos, jnp.zeros((SC_WINDOW, SC_COLUMNS), xn.dtype), pad_rows.reshape(1, -1))


def _expert_kernel(tile_expert_ref, n_used_ref, next_expert_ref, slot_ref,
                   x_ref, w1_ref, b1_ref, w2_ref, b2_ref, y_ref,
                   w1f_ref, w2f_ref, sem_ref, w1b_ref, w2p_ref, w2b_ref):
    t = pl.program_id(0)
    fresh = (t == 0) | (tile_expert_ref[t] != tile_expert_ref[jnp.maximum(t - 1, 0)])
    used = t < n_used_ref[0]

    def weight_copies(e, s):
        return (pltpu.make_async_copy(w1_ref.at[e], w1f_ref.at[s], sem_ref.at[0, s]),
                pltpu.make_async_copy(w2_ref.at[e], w2f_ref.at[s], sem_ref.at[1, s]))

    @pl.when(t == 0)
    def _():
        for copy in weight_copies(tile_expert_ref[0], slot_ref[0]):
            copy.start()

    @pl.when(fresh & used)
    def _():
        s = slot_ref[t]
        for copy in weight_copies(tile_expert_ref[t], s):
            copy.wait()

        @pl.when(next_expert_ref[t] >= 0)
        def _():
            for copy in weight_copies(next_expert_ref[t], 1 - s):
                copy.start()

        rows = LANES
        half = LANES // 2
        for r in range(0, w1b_ref.shape[0], rows):
            w1b_ref[r:r + rows, :] = w1f_ref[s, r:r + rows, :].astype(BF16)
        for c in range(w2p_ref.shape[0]):
            lanes = slice(c * LANES, (c + 1) * LANES)
            for r in range(0, w2b_ref.shape[0], rows):
                for b in range(2):
                    w2p_ref[c, pl.ds(r + b, half, stride=2), :] = (
                        w2f_ref[s, r + b * half:r + (b + 1) * half, lanes])
            w2b_ref[:, lanes] = w2p_ref[c].astype(BF16)

    @pl.when(used)
    def _():
        xb = _unpack_rows(x_ref[...]).astype(BF16)
        h = jnp.dot(xb, w1b_ref[...], preferred_element_type=F32) + b1_ref[0]
        even = lax.broadcasted_iota(I32, (1, LANES), 1) % 2 == 0
        parts = []
        for q in range(h.shape[1] // (2 * LANES)):
            a = h[:, (2 * q) * LANES:(2 * q + 1) * LANES]
            b = h[:, (2 * q + 1) * LANES:(2 * q + 2) * LANES]
            hg = jnp.where(even, a, pltpu.roll(b, 1, 1))
            hl = jnp.where(even, pltpu.roll(a, LANES - 1, 1), b)
            glu = jnp.minimum(hg, SWIGLU_LIMIT)
            lin = jnp.clip(hl, -SWIGLU_LIMIT, SWIGLU_LIMIT)
            parts.append((glu * jax.nn.sigmoid(SWIGLU_ALPHA * glu) * (lin + 1.0)).astype(BF16))
        act = jnp.concatenate(parts, axis=1)
        y_ref[...] = _pack_rows(jnp.dot(act, w2b_ref[...], preferred_element_type=F32) + b2_ref[0])

    @pl.when(jnp.logical_not(used))
    def _():
        y_ref[...] = jnp.zeros_like(y_ref)


def _experts(tile_expert, n_used, next_expert, slot, xs, w1, b1, w2, b2):
    n_rows, d = xs.shape
    tm = EXPERT_TILE
    by_expert = lambda arr: pl.BlockSpec((1,) + arr.shape[1:], lambda i, te, *_: (te[i], 0, 0))
    rows = pl.BlockSpec((tm, d), lambda i, *_: (i, 0))
    used_rows = pl.BlockSpec((tm, d), lambda i, te, nu, *_: (jnp.minimum(i, nu[0] - 1), 0))
    hbm = pl.BlockSpec(memory_space=pl.ANY)
    return pl.pallas_call(
        _expert_kernel,
        grid_spec=pltpu.PrefetchScalarGridSpec(
            num_scalar_prefetch=4,
            grid=(n_rows // tm,),
            in_specs=[used_rows, hbm, by_expert(b1), hbm, by_expert(b2)],
            out_specs=rows,
            scratch_shapes=[pltpu.VMEM((2,) + w1.shape[1:], F32), pltpu.VMEM((2,) + w2.shape[1:], F32),
                            pltpu.SemaphoreType.DMA((2, 2)),
                            pltpu.VMEM(w1.shape[1:], BF16),
                            pltpu.VMEM((w2.shape[2] // LANES, w2.shape[1], LANES), F32),
                            pltpu.VMEM(w2.shape[1:], BF16)]),
        out_shape=jax.ShapeDtypeStruct(xs.shape, xs.dtype),
        compiler_params=pltpu.CompilerParams(dimension_semantics=("arbitrary",),
                                             vmem_limit_bytes=EXPERT_VMEM_LIMIT),
        name="experts",
    )(tile_expert, n_used, next_expert, slot, xs, w1, b1, w2, b2)


def _gather_sparsecore(pos, y):
    d = y.shape[1]
    total = pos.size
    mesh = plsc.VectorSubcoreMesh(core_axis_name="core", subcore_axis_name="subcore")

    @functools.partial(pl.kernel, out_type=jax.ShapeDtypeStruct((total, d), y.dtype), mesh=mesh,
                       scratch_types=[], name="gather_sc")
    def gather(y_hbm, pos_hbm, out_hbm):
        for c in range(d // SC_COLUMNS):
            def body(pos_vmem, out_vmem, c=c):
                pltpu.sync_copy(y_hbm.at[:, pl.ds(c * SC_COLUMNS, SC_COLUMNS)].at[pos_vmem.at[0]], out_vmem)

            pltpu.emit_pipeline(
                body, grid=(total // SC_WINDOW,),
                in_specs=[pl.BlockSpec((1, SC_WINDOW), index_map=lambda i: (0, i))],
                out_specs=[pl.BlockSpec((SC_WINDOW, SC_COLUMNS), index_map=lambda i, c=c: (i, c))],
                core_axis_name=("core", "subcore"),
                dimension_semantics=(pltpu.PARALLEL,))(pos_hbm, out_hbm)

    return gather(y, pos.reshape(1, -1))


def _combine_kernel(h_ref, gate_ref, p_ref, wg_ref, wp_ref, gple_ref, gfin_ref, *refs, final_norm):
    y_refs, o_ref = refs[:TOP_K], refs[TOP_K]
    ts = COMBINE_SUBTILE
    subs = [slice(i * ts, (i + 1) * ts) for i in range(h_ref.shape[0] // ts)]

    def mix(rows):
        gates = gate_ref[rows, :]
        moe = gates[:, 0:1] * _unpack_rows(y_refs[0][rows, :])
        for k in range(1, TOP_K):
            moe = moe + gates[:, k:k + 1] * _unpack_rows(y_refs[k][rows, :])
        h = h_ref[rows, :] + moe
        return h, jnp.dot(_rms(h, gple_ref[...]).astype(BF16), wg_ref[...], preferred_element_type=F32)

    embs = [jnp.dot(p_ref[rows, :].astype(BF16), wp_ref[...], preferred_element_type=F32) for rows in subs]
    mixed = [mix(rows) for rows in subs]
    for rows, (h, logit), emb in zip(subs, mixed, embs):
        h = h + jax.nn.sigmoid(logit) * emb
        o_ref[rows, :] = _rms(h, gfin_ref[...]) if final_norm else h


def _combine(h, gates, p, w_gate, w_proj, g_ple, g_final, y_tok, final_norm):
    n, d = h.shape
    tm = TOKEN_TILE
    tile = lambda width: pl.BlockSpec((tm, width), lambda i: (i, 0))
    full = lambda arr: pl.BlockSpec(arr.shape, lambda i: (0,) * arr.ndim)
    choice = lambda k: pl.BlockSpec((tm, y_tok.shape[1]), lambda i: (k * (n // tm) + i, 0))
    return pl.pallas_call(
        functools.partial(_combine_kernel, final_norm=final_norm),
        grid=(n // tm,),
        in_specs=[tile(d), tile(LANES), tile(p.shape[1]), full(w_gate), full(w_proj),
                  full(g_ple), full(g_final)] + [choice(k) for k in range(TOP_K)],
        out_specs=tile(d),
        out_shape=jax.ShapeDtypeStruct((n, d), F32),
        compiler_params=_params("arbitrary"),
        name="combine",
    )(h, gates, p, w_gate, w_proj, g_ple, g_final, *([y_tok] * TOP_K))


def _layer(h, p, positions, g_attn, w_in, lq1, lk1, lq2, lk2, g_subln, b_forget, w_out, g_moe,
           w_router, b_router, w_e1, b_e1, w_e2, b_e2, g_ple, w_ple_gate, w_ple_proj, g_final,
           lambda_init, final_norm):
    b, s, d = h.shape
    n = b * s
    row = lambda v: v.reshape(1, -1)

    w_main = w_in[:, :MAIN_WIDTH].astype(BF16)
    pad_heads = ((0, BF16_TILE_ROWS - N_FOX_HEADS), (0, 0))
    w_fz_t = jnp.pad(w_in[:, MAIN_WIDTH:].T, pad_heads).astype(BF16)
    b_fz_t = jnp.pad(b_forget.reshape(-1, 1), pad_heads)
    b1 = b_e1[:, None, :]
    b2 = b_e2[:, None, :]

    dq_t, dk, dv_t, fq_t, fk, fv_t, c, c_t = _inproj(positions, h, row(g_attn), w_main, w_fz_t, b_fz_t)
    c_rows = c_t.reshape(b, N_FOX_HEADS // 2, 2, s)
    d_out, f_out = _attention(dq_t, dk, dv_t, fq_t, fk, fv_t, c_rows, c, row(lq1), row(lk1), row(lq2),
                              row(lk2), g_subln.reshape(-1, 1), lambda_init)

    h1, xn, gates, idx, rank, counts = _router(
        d_out.reshape(n, DIFF_WIDTH), f_out.reshape(n, FOX_WIDTH), h.reshape(n, d),
        w_out.astype(BF16), row(g_moe), w_router.T, b_router.reshape(-1, 1))

    te = EXPERT_TILE
    n_rows = n * TOP_K + N_EXPERTS * te
    counts = counts[:, 0].astype(I32)
    padded = (counts + te - 1) // te * te
    ends = jnp.cumsum(padded)
    starts = ends - padded
    pos = _slots(starts.astype(I32), idx, rank)[:TOP_K]
    tile_start = jnp.arange(n_rows // te, dtype=I32) * te
    tile_expert = jnp.minimum(jnp.sum((ends[None, :] <= tile_start[:, None]).astype(I32), axis=1),
                              N_EXPERTS - 1)
    n_used = (ends[-1:] // te).astype(I32)
    owns = padded > 0
    later = lax.cummin(jnp.where(owns, jnp.arange(N_EXPERTS, dtype=I32), N_EXPERTS), reverse=True)
    following = jnp.concatenate([later[1:], jnp.full((1,), N_EXPERTS, I32)])
    following = jnp.where(following < N_EXPERTS, following, -1)
    owner = tile_expert[:, None] == jnp.arange(N_EXPERTS, dtype=I32)[None, :]
    next_expert = jnp.sum(jnp.where(owner, following[None, :], 0), axis=1)
    slot = jnp.sum(jnp.where(owner, ((jnp.cumsum(owns.astype(I32)) - 1) % 2)[None, :], 0), axis=1)

    pad_count = padded - counts
    pad_total = jnp.sum(pad_count)
    pad_before = jnp.cumsum(pad_count) - pad_count
    entry = jnp.arange(N_EXPERTS * te, dtype=I32)
    which = entry % jnp.maximum(pad_total, 1)
    in_or_before = which[:, None] >= pad_before[None, :]
    expert_of = jnp.sum(in_or_before.astype(I32), axis=1) - 1
    mine = expert_of[:, None] == jnp.arange(N_EXPERTS, dtype=I32)[None, :]
    pad_rows = jnp.sum(jnp.where(mine, (starts + counts - pad_before)[None, :], 0), axis=1) + which
    pad_rows = jnp.where(pad_total > 0, pad_rows, n_rows - 1 - entry % N_EXPERTS)
    xs = _dispatch_sparsecore(pos.astype(I32), pad_rows.astype(I32), xn, n_rows)
    y = _experts(tile_expert, n_used, next_expert.astype(I32), slot.astype(I32), xs, w_e1, b1, w_e2, b2)
    y_tok = _gather_sparsecore(pos.astype(I32), y)
    out = _combine(h1, gates, p.reshape(n, PLE_DIM), w_ple_gate.astype(BF16),
                   w_ple_proj.astype(BF16), row(g_ple), row(g_final), y_tok, final_norm)
    return out.reshape(b, s, d)


def kernel(x, p, positions, g_attn, w_in, lambda_q1, lambda_k1, lambda_q2, lambda_k2, g_subln, b_forget, w_out, g_moe, w_router, b_router, w_e1, b_e1, w_e2, b_e2, g_ple, w_ple_gate, w_ple_proj, g_final):
    depth = g_attn.shape[0]
    h = x
    for i in range(depth):
        lambda_init = 0.8 - 0.6 * math.exp(-0.3 * i)
        h = _layer(h, p[i], positions, g_attn[i], w_in[i], lambda_q1[i], lambda_k1[i],
                   lambda_q2[i], lambda_k2[i], g_subln[i], b_forget[i], w_out[i], g_moe[i],
                   w_router[i], b_router[i], w_e1[i], b_e1[i], w_e2[i], b_e2[i], g_ple[i],
                   w_ple_gate[i], w_ple_proj[i], g_final, lambda_init, i == depth - 1)
    return h
```

```python
import functools
import math

import jax
import jax.numpy as jnp
from jax import lax
from jax.experimental import pallas as pl
from jax.experimental.pallas import tpu as pltpu
from jax.experimental.pallas import tpu_sc as plsc

F32 = jnp.float32
BF16 = jnp.bfloat16
I32 = jnp.int32
U32 = jnp.uint32

LANES = 128
BF16_TILE_ROWS = 16
D_MODEL = 1024
HEAD_DIM = 64
N_DIFF_HEADS = 4
DIFF_V_DIM = 2 * HEAD_DIM
N_FOX_HEADS = 8
DIFF_WIDTH = N_DIFF_HEADS * DIFF_V_DIM
FOX_WIDTH = N_FOX_HEADS * HEAD_DIM
ROT_DIM = HEAD_DIM // 4
ROPE_THETA = 500000.0
N_EXPERTS = 32
TOP_K = 4
D_EXPERT = D_MODEL
SWIGLU_ALPHA = 1.702
SWIGLU_LIMIT = 7.0
PLE_DIM = 256
NORM_EPS = 1e-5
LOG2_E = math.log2(math.e)
MAIN_WIDTH = 3 * DIFF_WIDTH + 3 * FOX_WIDTH

SEQ_TILE = 1024
SEQ_SUBTILE = 256
ATTN_TILE = 256
ATTN_UNITS = 2
TOKEN_TILE = 1024
COMBINE_SUBTILE = 256
ROUTER_TILE = 1024
ROUTER_SUBTILE = 256
EXPERT_TILE = 256
VMEM_LIMIT = 48 * 1024 * 1024
EXPERT_VMEM_LIMIT = 58 * 1024 * 1024


def _params(*semantics):
    return pltpu.CompilerParams(dimension_semantics=semantics, vmem_limit_bytes=VMEM_LIMIT)


def _rms(x, g):
    return x * lax.rsqrt(jnp.mean(x * x, axis=-1, keepdims=True) + NORM_EPS) * g


def _pack_rows(v):
    half = v.shape[1] // 2
    bits = pltpu.bitcast(v.astype(BF16).astype(F32), U32)
    return (bits[:, :half] >> 16) | bits[:, half:]


def _unpack_rows(words):
    low = pltpu.bitcast(words << 16, F32)
    high = pltpu.bitcast(words & jnp.uint32(0xFFFF0000), F32)
    return jnp.concatenate([low, high], axis=1)


def _inproj_kernel(pos_ref, x_ref, g_ref, w_ref, wfzt_ref, bfzt_ref,
                   dqt_ref, dk_ref, dvt_ref, fqt_ref, fk_ref, fvt_ref, c_ref, ct_ref, carry_t_ref):
    @pl.when(pl.program_id(1) == 0)
    def _():
        carry_t_ref[...] = jnp.zeros_like(carry_t_ref)

    ts = SEQ_SUBTILE
    subs = [slice(i * ts, (i + 1) * ts) for i in range(x_ref.shape[1] // ts)]
    xbs = [_rms(x_ref[0, rows, :], g_ref[...]).astype(BF16) for rows in subs]
    nt = (((1,), (1,)), ((), ()))
    tn = (((0,), (0,)), ((), ()))

    def pieces(v):
        hi = v.astype(BF16)
        r1 = v - hi.astype(F32)
        mid = r1.astype(BF16)
        return hi, mid, (r1 - mid.astype(F32)).astype(BF16)

    def rows_to_lanes(v, select):
        return sum(lax.dot_general(part, select, tn, preferred_element_type=F32) for part in pieces(v))

    half_rot = ROT_DIM // 2
    freq = lax.broadcasted_iota(I32, (half_rot, 1), 0)
    inv_freq = jnp.power(ROPE_THETA, -(2 * freq).astype(F32) / ROT_DIM)
    r = lax.broadcasted_iota(I32, (2 * half_rot, 2 * LANES), 0)
    lane2 = lax.broadcasted_iota(I32, (2 * half_rot, 2 * LANES), 1)
    d = lane2 % HEAD_DIM
    same_freq = jnp.where(d % half_rot == r % half_rot, 1.0, 0.0)
    cos_part = jnp.where((r < half_rot) & (lane2 < LANES) & (d < ROT_DIM), same_freq, 0.0)
    sin_part = jnp.where((r >= half_rot) & (lane2 >= LANES) & (d < ROT_DIM),
                         jnp.where(d < half_rot, -same_freq, same_freq), 0.0)
    expand = (cos_part + sin_part).astype(BF16)
    lane = lax.broadcasted_iota(I32, (1, LANES), 1) % HEAD_DIM
    first_half = lane < half_rot

    def rope_tables(rows):
        ang = inv_freq * pos_ref[0, :, rows].astype(F32)
        trig = jnp.concatenate([jnp.cos(ang), jnp.sin(ang)], axis=0)
        tables = rows_to_lanes(trig, expand)
        return jnp.where(lane < ROT_DIM, tables[:, :LANES], 1.0), tables[:, LANES:]

    def rope(t, cosf, sinf):
        outs = []
        for j in range(t.shape[1] // LANES):
            tj = t[:, j * LANES:(j + 1) * LANES]
            partner = jnp.where(first_half,
                                pltpu.roll(tj, LANES - ROT_DIM // 2, 1),
                                pltpu.roll(tj, ROT_DIM // 2, 1))
            outs.append(tj * cosf + partner * sinf)
        return jnp.concatenate(outs, axis=1)

    tables = [rope_tables(rows) for rows in subs]

    def log_sigmoid(z):
        return jnp.minimum(z, 0.0) - jnp.log1p(jnp.exp(-jnp.abs(z)))

    row = lax.broadcasted_iota(I32, (ts, ts), 0)
    col = lax.broadcasted_iota(I32, (ts, ts), 1)
    upto_col = jnp.where(row <= col, 1.0, 0.0).astype(BF16)
    heads = wfzt_ref.shape[0]
    head = lax.broadcasted_iota(I32, (heads, FOX_WIDTH), 0)
    group = lax.broadcasted_iota(I32, (heads, FOX_WIDTH), 1) // HEAD_DIM
    head_lanes = jnp.where(head == group, 1.0, 0.0).astype(BF16)

    for rows, xb in zip(subs, xbs):
        fzt = lax.dot_general(wfzt_ref[...], xb, nt, preferred_element_type=F32) + bfzt_ref[...]
        ct = carry_t_ref[...] + sum(jnp.dot(part, upto_col, preferred_element_type=F32)
                                    for part in pieces(log_sigmoid(fzt)))
        carry_t_ref[...] = ct[:, ts - 1:ts]
        ct = ct * LOG2_E
        ct_ref[0, :, rows] = ct[0:N_FOX_HEADS, :]
        c_ref[0, rows, :] = rows_to_lanes(ct, head_lanes)

    scale = HEAD_DIM ** -0.5 * LOG2_E
    w = DIFF_WIDTH
    projs = [jnp.dot(xb, w_ref[...], preferred_element_type=F32) for xb in xbs]
    for i, (rows, proj, (cosf, sinf)) in enumerate(zip(subs, projs, tables)):
        dqt_ref[0, :, rows] = (rope(proj[:, 0:w], cosf, sinf) * scale).astype(BF16).T
        dk_ref[0, rows, :] = rope(proj[:, w:2 * w], cosf, sinf).astype(BF16)
        dvt_ref[0, i] = proj[:, 2 * w:3 * w].astype(BF16).T
        fqt_ref[0, :, rows] = (proj[:, 3 * w:4 * w] * scale).astype(BF16).T
        fk_ref[0, rows, :] = proj[:, 4 * w:5 * w].astype(BF16)
        fvt_ref[0, i] = proj[:, 5 * w:6 * w].astype(BF16).T


def _inproj(positions, x, g, w_main, w_fz_t, b_fz_t):
    b, s, d = x.shape
    ts = SEQ_TILE
    sub = SEQ_SUBTILE
    assert sub == ATTN_TILE
    tile = lambda width: pl.BlockSpec((1, ts, width), lambda i, j: (i, j, 0))
    tile_t = pl.BlockSpec((1, DIFF_WIDTH, ts), lambda i, j: (i, 0, j))
    key_tile_t = pl.BlockSpec((1, ts // sub, DIFF_WIDTH, sub), lambda i, j: (i, j, 0, 0))
    full = lambda arr: pl.BlockSpec(arr.shape, lambda i, j: (0,) * arr.ndim)
    rows = jax.ShapeDtypeStruct((b, s, DIFF_WIDTH), BF16)
    rows_t = jax.ShapeDtypeStruct((b, DIFF_WIDTH, s), BF16)
    key_tiles_t = jax.ShapeDtypeStruct((b, s // sub, DIFF_WIDTH, sub), BF16)
    return pl.pallas_call(
        _inproj_kernel,
        grid=(b, s // ts),
        in_specs=[pl.BlockSpec((1, 1, ts), lambda i, j: (i, 0, j)), tile(d), full(g), full(w_main),
                  full(w_fz_t), full(b_fz_t)],
        out_specs=[tile_t, tile(DIFF_WIDTH), key_tile_t, tile_t, tile(FOX_WIDTH), key_tile_t,
                   tile(FOX_WIDTH), pl.BlockSpec((1, N_FOX_HEADS, ts), lambda i, j: (i, 0, j))],
        out_shape=[rows_t, rows, key_tiles_t, rows_t, rows, key_tiles_t,
                   jax.ShapeDtypeStruct((b, s, FOX_WIDTH), F32),
                   jax.ShapeDtypeStruct((b, N_FOX_HEADS, s), F32)],
        scratch_shapes=[pltpu.VMEM((w_fz_t.shape[0], 1), F32)],
        compiler_params=_params("arbitrary", "arbitrary"),
        name="inproj",
    )(positions.reshape(b, 1, s), x, g, w_main, w_fz_t, b_fz_t)


ONES_ROWS = 16


def _block_diag_queries(qt):
    row = lax.broadcasted_iota(I32, qt.shape, 0)
    zero = jnp.zeros_like(qt)
    return jnp.concatenate([jnp.where(row < HEAD_DIM, qt, zero),
                            jnp.where(row >= HEAD_DIM, qt, zero)], axis=1)


class _Chain:
    def __init__(self, qbd, keys, values_t, scratch, key_bias=None, query_bias=None):
        self.qbd, self.keys, self.values_t = qbd, keys, values_t
        self.acc_ref, self.s_ref, self.p_ref, self.stat_ref = scratch
        self.key_bias, self.query_bias = key_bias, query_bias


_RUNNING_MAX, _TILE_MAX, _RESCALE = 0, 1, 2


def _causal_attention(chains, qi, first_step):
    tk, tq = chains[0].s_ref.shape[0], chains[0].s_ref.shape[1] // 2
    ones = jnp.ones((ONES_ROWS, tk), BF16)

    def scores(ch, i, diagonal):
        s = jnp.dot(ch.keys(i), ch.qbd, preferred_element_type=F32)
        if ch.key_bias is not None:
            s = s - ch.key_bias(i)
        if diagonal is not False:
            r = lax.broadcasted_iota(I32, s.shape, 0)
            c = lax.broadcasted_iota(I32, s.shape, 1) & (tq - 1)
            visible = r <= c if diagonal is True else jnp.logical_or(r <= c, jnp.logical_not(diagonal))
            s = jnp.where(visible, s, -jnp.inf)
        ch.s_ref[...] = s
        top = jnp.max(s, axis=0, keepdims=True)
        if ch.query_bias is not None:
            top = top + ch.query_bias
        ch.stat_ref[_TILE_MAX:_TILE_MAX + 1, :] = top

    def probs(ch):
        m = ch.stat_ref[_RUNNING_MAX:_RUNNING_MAX + 1, :]
        m_new = jnp.maximum(m, ch.stat_ref[_TILE_MAX:_TILE_MAX + 1, :])
        ch.stat_ref[_RUNNING_MAX:_RUNNING_MAX + 1, :] = m_new
        ch.stat_ref[_RESCALE:_RESCALE + 1, :] = jnp.exp2(m - m_new)
        ref = m_new if ch.query_bias is None else m_new - ch.query_bias
        ch.p_ref[...] = jnp.exp2(ch.s_ref[...] - ref).astype(BF16)

    def accumulate(ch, i):
        vts = jnp.concatenate([ch.values_t(i), ones], axis=0)
        ch.acc_ref[...] = (ch.stat_ref[_RESCALE:_RESCALE + 1, :] * ch.acc_ref[...]
                           + jnp.dot(vts, ch.p_ref[...], preferred_element_type=F32))

    def body(k, diagonal_next):
        for ch in chains:
            accumulate(ch, jnp.maximum(k - 1, 0))
        for ch in chains:
            probs(ch)
        for ch in chains:
            scores(ch, k + 1, diagonal_next)

    @pl.when(first_step)
    def _():
        for ch in chains:
            ch.acc_ref[...] = jnp.zeros_like(ch.acc_ref)
            ch.p_ref[...] = jnp.zeros_like(ch.p_ref)

    for ch in chains:
        ch.stat_ref[...] = jnp.concatenate(
            [jnp.full((1, 2 * tq), -jnp.inf, F32), jnp.zeros((ch.stat_ref.shape[0] - 1, 2 * tq), F32)], axis=0)
        scores(ch, 0, qi == 0)

    def loop_body(k, carry):
        body(k, False)
        return carry

    lax.fori_loop(0, qi - 1, loop_body, 0)

    @pl.when(qi >= 1)
    def _():
        body(qi - 1, True)

    for ch in chains:
        accumulate(ch, jnp.maximum(qi - 1, 0))
    for ch in chains:
        probs(ch)
    for ch in chains:
        accumulate(ch, qi)


def _attention_kernel(lq1_ref, lk1_ref, lq2_ref, lk2_ref, g_ref,
                      dqt_ref, dk_ref, dvt_ref, fqt_ref, fk_ref, fvt_ref, cq_ref, ck_ref,
                      do_ref, fo_ref, *scratch, lambda_init):
    tq = dqt_ref.shape[2]
    tk = dvt_ref.shape[3]
    units = dqt_ref.shape[1] // LANES
    lane = lax.broadcasted_iota(I32, (1, LANES), 1)
    scratch = [scratch[i:i + 4] for i in range(0, len(scratch), 4)]

    def unit_chains(u):
        lanes = slice(u * LANES, (u + 1) * LANES)

        def tile(ref):
            return lambda i: ref[0, pl.ds(pl.multiple_of(i * tk, tk), tk), lanes]

        def tile_t(ref):
            return lambda i: ref[0, i, lanes, :]

        cq = jnp.concatenate([cq_ref[0, u, 0:1, :], cq_ref[0, u, 1:2, :]], axis=1)

        def key_bias(i):
            ck = tile(ck_ref)(i)
            other = pltpu.roll(ck, HEAD_DIM, 1)
            first = jnp.where(lane < HEAD_DIM, ck, other)
            second = jnp.where(lane >= HEAD_DIM, ck, other)
            return jnp.concatenate([first] * (tq // LANES) + [second] * (tq // LANES), axis=1)

        return [_Chain(_block_diag_queries(dqt_ref[0, lanes, :]), tile(dk_ref), tile_t(dvt_ref),
                       scratch[2 * u]),
                _Chain(_block_diag_queries(fqt_ref[0, lanes, :]), tile(fk_ref), tile_t(fvt_ref),
                       scratch[2 * u + 1], key_bias, cq)]

    chains = [ch for u in range(units) for ch in unit_chains(u)]
    first_step = (pl.program_id(0) == 0) & (pl.program_id(1) == 0) & (pl.program_id(2) == 0)
    _causal_attention(chains, pl.program_id(2), first_step)

    lam = (jnp.exp(jnp.sum(lq1_ref[...] * lk1_ref[...], axis=-1, keepdims=True))
           - jnp.exp(jnp.sum(lq2_ref[...] * lk2_ref[...], axis=-1, keepdims=True)) + lambda_init)
    dv, h = DIFF_V_DIM, HEAD_DIM
    for u in range(units):
        lanes = slice(u * LANES, (u + 1) * LANES)
        acc = chains[2 * u].acc_ref
        o = (acc[0:dv, 0:tq] / acc[dv:dv + 1, 0:tq]
             - lam * (acc[0:dv, tq:] / acc[dv:dv + 1, tq:]))
        ms = jnp.mean(o * o, axis=0, keepdims=True)
        do_ref[0, :, lanes] = (o * lax.rsqrt(ms + NORM_EPS) * g_ref[...]
                               * (1.0 - lambda_init)).astype(BF16).T
        acc = chains[2 * u + 1].acc_ref
        fo_ref[0, :, lanes] = jnp.concatenate([acc[0:h, 0:tq] / acc[2 * h:2 * h + 1, 0:tq],
                                               acc[h:2 * h, tq:] / acc[2 * h:2 * h + 1, tq:]],
                                              axis=0).astype(BF16).T


def _attention(dqt, dk, dvt, fqt, fk, fvt, c_rows, c_cols, lq1, lk1, lq2, lk2, g_col, lambda_init):
    b, s, _ = dk.shape
    t = ATTN_TILE
    assert N_DIFF_HEADS == N_FOX_HEADS // 2 and N_DIFF_HEADS % ATTN_UNITS == 0
    width = ATTN_UNITS * LANES
    vec = lambda arr: pl.BlockSpec(arr.shape, lambda i, h, j: (0, 0))
    q_spec = pl.BlockSpec((1, width, t), lambda i, h, j: (i, h, j))
    k_spec = pl.BlockSpec((1, s, width), lambda i, h, j: (i, 0, h))
    v_spec = pl.BlockSpec((1, s // t, width, t), lambda i, h, j: (i, 0, h, 0))
    o_spec = pl.BlockSpec((1, t, width), lambda i, h, j: (i, j, h))
    out = jax.ShapeDtypeStruct((b, s, DIFF_WIDTH), BF16)
    chain_scratch = [pltpu.VMEM((LANES + ONES_ROWS, 2 * t), F32),
                     pltpu.VMEM((t, 2 * t), F32),
                     pltpu.VMEM((t, 2 * t), BF16),
                     pltpu.VMEM((8, 2 * t), F32)]
    return pl.pallas_call(
        functools.partial(_attention_kernel, lambda_init=lambda_init),
        grid=(b, N_DIFF_HEADS // ATTN_UNITS, s // t),
        in_specs=[vec(lq1), vec(lk1), vec(lq2), vec(lk2), vec(g_col),
                  q_spec, k_spec, v_spec, q_spec, k_spec, v_spec,
                  pl.BlockSpec((1, ATTN_UNITS, 2, t), lambda i, h, j: (i, h, 0, j)), k_spec],
        out_specs=[o_spec, o_spec],
        out_shape=[out, out],
        scratch_shapes=chain_scratch * (2 * ATTN_UNITS),
        compiler_params=_params("arbitrary", "arbitrary", "arbitrary"),
        name="attention",
    )(lq1, lk1, lq2, lk2, g_col, dqt, dk, dvt, fqt, fk, fvt, c_rows, c_cols)


def _router_kernel(d_ref, f_ref, x_ref, wo_ref, g_ref, wrt_ref, brt_ref,
                   h_ref, xn_ref, gate_ref, idx_ref, rank_ref, count_ref, carry_ref):
    tm = ROUTER_SUBTILE
    nt = (((1,), (1,)), ((), ()))

    def pieces(v, count):
        out = []
        for _ in range(count):
            part = v.astype(BF16)
            out.append(part)
            v = v - part.astype(F32)
        return out

    wh = wrt_ref[...].astype(BF16)
    row = lax.broadcasted_iota(I32, (tm, tm), 0)
    col = lax.broadcasted_iota(I32, (tm, tm), 1)
    earlier = jnp.where(row < col, 1.0, 0.0).astype(BF16)
    eye = jnp.where(row == col, 1.0, 0.0).astype(BF16)

    @pl.when(pl.program_id(0) == 0)
    def _():
        carry_ref[...] = jnp.zeros_like(carry_ref)

    subs = [slice(i * tm, (i + 1) * tm) for i in range(x_ref.shape[0] // tm)]

    def logits(rows):
        mixed = (jnp.dot(d_ref[rows, :], wo_ref[0:DIFF_WIDTH, :], preferred_element_type=F32)
                 + jnp.dot(f_ref[rows, :], wo_ref[DIFF_WIDTH:, :], preferred_element_type=F32))
        h = x_ref[rows, :] + mixed
        h_ref[rows, :] = h
        xn = _rms(h, g_ref[...])
        xn_ref[rows, :] = _pack_rows(xn)
        return lax.dot_general(wh, xn.astype(BF16), nt, preferred_element_type=F32) + brt_ref[...]

    def top_k(work):
        expert = lax.broadcasted_iota(I32, work.shape, 0)
        vals, ids, picks = [], [], []
        for _ in range(TOP_K):
            m = jnp.max(work, axis=0, keepdims=True)
            idx = jnp.min(jnp.where(work == m, expert, N_EXPERTS), axis=0, keepdims=True)
            pick = expert == idx
            vals.append(m)
            ids.append(idx)
            picks.append(pick)
            work = jnp.where(pick, -jnp.inf, work)
        exps = [jnp.exp(v - vals[0]) for v in vals]
        denom = exps[0] + exps[1] + exps[2] + exps[3]
        return ids, picks, [e / denom for e in exps]

    def ranks(picks):
        chosen = jnp.where(picks[0] | picks[1] | picks[2] | picks[3], 1.0, 0.0)
        rank = jnp.dot(chosen.astype(BF16), earlier, preferred_element_type=F32) + carry_ref[...]
        carry_ref[...] = carry_ref[...] + jnp.sum(chosen, axis=1, keepdims=True)
        return rank

    def emit(rows, ids, picks, gates, rank):
        slot = lax.broadcasted_iota(I32, (8, tm), 0)
        gate_t = jnp.zeros((8, tm), F32)
        idx_t = jnp.zeros((8, tm), I32)
        rank_t = jnp.zeros((8, tm), F32)
        for k in range(TOP_K):
            gate_t = jnp.where(slot == k, gates[k], gate_t)
            idx_t = jnp.where(slot == k, ids[k], idx_t)
            rank_t = jnp.where(slot == k, jnp.sum(jnp.where(picks[k], rank, 0.0), axis=0, keepdims=True),
                               rank_t)
        idx_ref[:, rows] = idx_t
        rank_ref[:, rows] = rank_t.astype(I32)
        gate_rows = jnp.concatenate([gate_t, jnp.zeros((LANES - 8, tm), F32)], axis=0)
        gate_ref[rows, :] = sum(lax.dot_general(eye, part, nt, preferred_element_type=F32)
                                for part in pieces(gate_rows, 3))

    work = [logits(rows) for rows in subs]
    chosen = [top_k(w) for w in work]
    rank = [ranks(picks) for _, picks, _ in chosen]
    for rows, (ids, picks, gates), r in zip(subs, chosen, rank):
        emit(rows, ids, picks, gates, r)
    count_ref[...] = jnp.broadcast_to(carry_ref[...], count_ref.shape)


def _router(d_out, f_out, x, w_out, g_moe, w_router_t, b_router_t):
    n, d = x.shape
    tm = ROUTER_TILE
    tile = lambda width: pl.BlockSpec((tm, width), lambda i: (i, 0))
    rows = pl.BlockSpec((8, tm), lambda i: (0, i))
    full = lambda arr: pl.BlockSpec(arr.shape, lambda i: (0,) * arr.ndim)
    return pl.pallas_call(
        _router_kernel,
        grid=(n // tm,),
        in_specs=[tile(DIFF_WIDTH), tile(FOX_WIDTH), tile(d), full(w_out), full(g_moe),
                  full(w_router_t), full(b_router_t)],
        out_specs=[tile(d), tile(d // 2), tile(LANES), rows, rows,
                   pl.BlockSpec((N_EXPERTS, LANES), lambda i: (0, 0))],
        out_shape=[jax.ShapeDtypeStruct((n, d), F32), jax.ShapeDtypeStruct((n, d // 2), U32),
                   jax.ShapeDtypeStruct((n, LANES), F32), jax.ShapeDtypeStruct((8, n), I32),
                   jax.ShapeDtypeStruct((8, n), I32), jax.ShapeDtypeStruct((N_EXPERTS, LANES), F32)],
        scratch_shapes=[pltpu.VMEM((N_EXPERTS, 1), F32)],
        compiler_params=_params("arbitrary"),
        name="router",
    )(d_out, f_out, x, w_out, g_moe, w_router_t, b_router_t)


def _slot_kernel(starts_ref, idx_ref, rank_ref, pos_ref):
    idx = idx_ref[...]
    pos = rank_ref[...]
    for e in range(N_EXPERTS):
        pos = jnp.where(idx == e, pos + starts_ref[e], pos)
    pos_ref[...] = pos


def _slots(starts, idx, rank):
    rows, n = idx.shape
    width = min(n, 2048)
    block = pl.BlockSpec((rows, width), lambda i, starts: (0, i))
    return pl.pallas_call(
        _slot_kernel,
        grid_spec=pltpu.PrefetchScalarGridSpec(num_scalar_prefetch=1, grid=(n // width,),
                                               in_specs=[block, block], out_specs=block),
        out_shape=jax.ShapeDtypeStruct((rows, n), I32),
        compiler_params=_params("arbitrary"),
        name="slots",
    )(starts, idx, rank)


SC_WINDOW = 128
SC_COLUMNS = 256


def _dispatch_sparsecore(pos, pad_rows, xn, n_rows):
    n, d = xn.shape
    mesh = plsc.VectorSubcoreMesh(core_axis_name="core", subcore_axis_name="subcore")
    spread = dict(core_axis_name=("core", "subcore"), dimension_semantics=(pltpu.PARALLEL,))

    @functools.partial(pl.kernel, out_type=jax.ShapeDtypeStruct((n_rows, d), xn.dtype), mesh=mesh,
                       scratch_types=[], name="dispatch_sc")
    def scatter(x_hbm, pos_hbm, zero_hbm, pad_hbm, xs_hbm):
        for c in range(d // SC_COLUMNS):
            columns = xs_hbm.at[:, pl.ds(c * SC_COLUMNS, SC_COLUMNS)]

            def body(x_vmem, pos_vmem, columns=columns):
                for k in range(TOP_K):
                    pltpu.sync_copy(x_vmem, columns.at[pos_vmem.at[k]])

            pltpu.emit_pipeline(
                body, grid=(n // SC_WINDOW,),
                in_specs=[pl.BlockSpec((SC_WINDOW, SC_COLUMNS), index_map=lambda i, c=c: (i, c)),
                          pl.BlockSpec((TOP_K, SC_WINDOW), index_map=lambda i: (0, i))],
                out_specs=[], **spread)(x_hbm, pos_hbm)

            def pad_body(zero_vmem, pad_vmem, columns=columns):
                pltpu.sync_copy(zero_vmem, columns.at[pad_vmem.at[0]])

            pltpu.emit_pipeline(
                pad_body, grid=(pad_rows.size // SC_WINDOW,),
                in_specs=[pl.BlockSpec((SC_WINDOW, SC_COLUMNS), index_map=lambda i: (0, 0)),
                          pl.BlockSpec((1, SC_WINDOW), index_map=lambda i: (0, i))],
                out_specs=[], **spread)(zero_hbm, pad_hbm)

    return scatter(xn, pos, jnp.zeros((SC_WINDOW, SC_COLUMNS), xn.dtype), pad_rows.reshape(1, -1))


def _expert_kernel(tile_expert_ref, n_used_ref, next_expert_ref, slot_ref,
                   x_ref, w1_ref, b1_ref, w2_ref, b2_ref, y_ref,
                   w1f_ref, w2f_ref, sem_ref, w1b_ref, w2p_ref, w2b_ref):
    t = pl.program_id(0)
    fresh = (t == 0) | (tile_expert_ref[t] != tile_expert_ref[jnp.maximum(t - 1, 0)])
    used = t < n_used_ref[0]

    def weight_copies(e, s):
        return (pltpu.make_async_copy(w1_ref.at[e], w1f_ref.at[s], sem_ref.at[0, s]),
                pltpu.make_async_copy(w2_ref.at[e], w2f_ref.at[s], sem_ref.at[1, s]))

    @pl.when(t == 0)
    def _():
        for copy in weight_copies(tile_expert_ref[0], slot_ref[0]):
            copy.start()

    @pl.when(fresh & used)
    def _():
        s = slot_ref[t]
        for copy in weight_copies(tile_expert_ref[t], s):
            copy.wait()

        @pl.when(next_expert_ref[t] >= 0)
        def _():
            for copy in weight_copies(next_expert_ref[t], 1 - s):
                copy.start()

        rows = LANES
        half = LANES // 2
        for r in range(0, w1b_ref.shape[0], rows):
            w1b_ref[r:r + rows, :] = w1f_ref[s, r:r + rows, :].astype(BF16)
        for c in range(w2p_ref.shape[0]):
            lanes = slice(c * LANES, (c + 1) * LANES)
            for r in range(0, w2b_ref.shape[0], rows):
                for b in range(2):
                    w2p_ref[c, pl.ds(r + b, half, stride=2), :] = (
                        w2f_ref[s, r + b * half:r + (b + 1) * half, lanes])
            w2b_ref[:, lanes] = w2p_ref[c].astype(BF16)

    @pl.when(used)
    def _():
        xb = _unpack_rows(x_ref[...]).astype(BF16)
        h = jnp.dot(xb, w1b_ref[...], preferred_element_type=F32) + b1_ref[0]
        even = lax.broadcasted_iota(I32, (1, LANES), 1) % 2 == 0
        parts = []
        for q in range(h.shape[1] // (2 * LANES)):
            a = h[:, (2 * q) * LANES:(2 * q + 1) * LANES]
            b = h[:, (2 * q + 1) * LANES:(2 * q + 2) * LANES]
            hg = jnp.where(even, a, pltpu.roll(b, 1, 1))
            hl = jnp.where(even, pltpu.roll(a, LANES - 1, 1), b)
            glu = jnp.minimum(hg, SWIGLU_LIMIT)
            lin = jnp.clip(hl, -SWIGLU_LIMIT, SWIGLU_LIMIT)
            parts.append((glu * jax.nn.sigmoid(SWIGLU_ALPHA * glu) * (lin + 1.0)).astype(BF16))
        act = jnp.concatenate(parts, axis=1)
        y_ref[...] = _pack_rows(jnp.dot(act, w2b_ref[...], preferred_element_type=F32) + b2_ref[0])

    @pl.when(jnp.logical_not(used))
    def _():
        y_ref[...] = jnp.zeros_like(y_ref)


def _experts(tile_expert, n_used, next_expert, slot, xs, w1, b1, w2, b2):
    n_rows, d = xs.shape
    tm = EXPERT_TILE
    by_expert = lambda arr: pl.BlockSpec((1,) + arr.shape[1:], lambda i, te, *_: (te[i], 0, 0))
    rows = pl.BlockSpec((tm, d), lambda i, *_: (i, 0))
    used_rows = pl.BlockSpec((tm, d), lambda i, te, nu, *_: (jnp.minimum(i, nu[0] - 1), 0))
    hbm = pl.BlockSpec(memory_space=pl.ANY)
    return pl.pallas_call(
        _expert_kernel,
        grid_spec=pltpu.PrefetchScalarGridSpec(
            num_scalar_prefetch=4,
            grid=(n_rows // tm,),
            in_specs=[used_rows, hbm, by_expert(b1), hbm, by_expert(b2)],
            out_specs=rows,
            scratch_shapes=[pltpu.VMEM((2,) + w1.shape[1:], F32), pltpu.VMEM((2,) + w2.shape[1:], F32),
                            pltpu.SemaphoreType.DMA((2, 2)),
                            pltpu.VMEM(w1.shape[1:], BF16),
                            pltpu.VMEM((w2.shape[2] // LANES, w2.shape[1], LANES), F32),
                            pltpu.VMEM(w2.shape[1:], BF16)]),
        out_shape=jax.ShapeDtypeStruct(xs.shape, xs.dtype),
        compiler_params=pltpu.CompilerParams(dimension_semantics=("arbitrary",),
                                             vmem_limit_bytes=EXPERT_VMEM_LIMIT),
        name="experts",
    )(tile_expert, n_used, next_expert, slot, xs, w1, b1, w2, b2)


def _gather_sparsecore(pos, y):
    d = y.shape[1]
    total = pos.size
    mesh = plsc.VectorSubcoreMesh(core_axis_name="core", subcore_axis_name="subcore")

    @functools.partial(pl.kernel, out_type=jax.ShapeDtypeStruct((total, d), y.dtype), mesh=mesh,
                       scratch_types=[], name="gather_sc")
    def gather(y_hbm, pos_hbm, out_hbm):
        for c in range(d // SC_COLUMNS):
            def body(pos_vmem, out_vmem, c=c):
                pltpu.sync_copy(y_hbm.at[:, pl.ds(c * SC_COLUMNS, SC_COLUMNS)].at[pos_vmem.at[0]], out_vmem)

            pltpu.emit_pipeline(
                body, grid=(total // SC_WINDOW,),
                in_specs=[pl.BlockSpec((1, SC_WINDOW), index_map=lambda i: (0, i))],
                out_specs=[pl.BlockSpec((SC_WINDOW, SC_COLUMNS), index_map=lambda i, c=c: (i, c))],
                core_axis_name=("core", "subcore"),
                dimension_semantics=(pltpu.PARALLEL,))(pos_hbm, out_hbm)

    return gather(y, pos.reshape(1, -1))


def _combine_kernel(h_ref, gate_ref, p_ref, wg_ref, wp_ref, gple_ref, gfin_ref, *refs, final_norm):
    y_refs, o_ref = refs[:TOP_K], refs[TOP_K]
    ts = COMBINE_SUBTILE
    subs = [slice(i * ts, (i + 1) * ts) for i in range(h_ref.shape[0] // ts)]

    def mix(rows):
        gates = gate_ref[rows, :]
        moe = gates[:, 0:1] * _unpack_rows(y_refs[0][rows, :])
        for k in range(1, TOP_K):
            moe = moe + gates[:, k:k + 1] * _unpack_rows(y_refs[k][rows, :])
        h = h_ref[rows, :] + moe
        return h, jnp.dot(_rms(h, gple_ref[...]).astype(BF16), wg_ref[...], preferred_element_type=F32)

    embs = [jnp.dot(p_ref[rows, :].astype(BF16), wp_ref[...], preferred_element_type=F32) for rows in subs]
    mixed = [mix(rows) for rows in subs]
    for rows, (h, logit), emb in zip(subs, mixed, embs):
        h = h + jax.nn.sigmoid(logit) * emb
        o_ref[rows, :] = _rms(h, gfin_ref[...]) if final_norm else h


def _combine(h, gates, p, w_gate, w_proj, g_ple, g_final, y_tok, final_norm):
    n, d = h.shape
    tm = TOKEN_TILE
    tile = lambda width: pl.BlockSpec((tm, width), lambda i: (i, 0))
    full = lambda arr: pl.BlockSpec(arr.shape, lambda i: (0,) * arr.ndim)
    choice = lambda k: pl.BlockSpec((tm, y_tok.shape[1]), lambda i: (k * (n // tm) + i, 0))
    return pl.pallas_call(
        functools.partial(_combine_kernel, final_norm=final_norm),
        grid=(n // tm,),
        in_specs=[tile(d), tile(LANES), tile(p.shape[1]), full(w_gate), full(w_proj),
                  full(g_ple), full(g_final)] + [choice(k) for k in range(TOP_K)],
        out_specs=tile(d),
        out_shape=jax.ShapeDtypeStruct((n, d), F32),
        compiler_params=_params("arbitrary"),
        name="combine",
    )(h, gates, p, w_gate, w_proj, g_ple, g_final, *([y_tok] * TOP_K))


def _layer(h, p, positions, g_attn, w_in, lq1, lk1, lq2, lk2, g_subln, b_forget, w_out, g_moe,
           w_router, b_router, w_e1, b_e1, w_e2, b_e2, g_ple, w_ple_gate, w_ple_proj, g_final,
           lambda_init, final_norm):
    b, s, d = h.shape
    n = b * s
    row = lambda v: v.reshape(1, -1)

    w_main = w_in[:, :MAIN_WIDTH].astype(BF16)
    pad_heads = ((0, BF16_TILE_ROWS - N_FOX_HEADS), (0, 0))
    w_fz_t = jnp.pad(w_in[:, MAIN_WIDTH:].T, pad_heads).astype(BF16)
    b_fz_t = jnp.pad(b_forget.reshape(-1, 1), pad_heads)
    b1 = b_e1[:, None, :]
    b2 = b_e2[:, None, :]

    dq_t, dk, dv_t, fq_t, fk, fv_t, c, c_t = _inproj(positions, h, row(g_attn), w_main, w_fz_t, b_fz_t)
    c_rows = c_t.reshape(b, N_FOX_HEADS // 2, 2, s)
    d_out, f_out = _attention(dq_t, dk, dv_t, fq_t, fk, fv_t, c_rows, c, row(lq1), row(lk1), row(lq2),
                              row(lk2), g_subln.reshape(-1, 1), lambda_init)

    h1, xn, gates, idx, rank, counts = _router(
        d_out.reshape(n, DIFF_WIDTH), f_out.reshape(n, FOX_WIDTH), h.reshape(n, d),
        w_out.astype(BF16), row(g_moe), w_router.T, b_router.reshape(-1, 1))

    te = EXPERT_TILE
    n_rows = n * TOP_K + N_EXPERTS * te
    counts = counts[:, 0].astype(I32)
    padded = (counts + te - 1) // te * te
    ends = jnp.cumsum(padded)
    starts = ends - padded
    pos = _slots(starts.astype(I32), idx, rank)[:TOP_K]
    tile_start = jnp.arange(n_rows // te, dtype=I32) * te
    tile_expert = jnp.minimum(jnp.sum((ends[None, :] <= tile_start[:, None]).astype(I32), axis=1),
                              N_EXPERTS - 1)
    n_used = (ends[-1:] // te).astype(I32)
    owns = padded > 0
    later = lax.cummin(jnp.where(owns, jnp.arange(N_EXPERTS, dtype=I32), N_EXPERTS), reverse=True)
    following = jnp.concatenate([later[1:], jnp.full((1,), N_EXPERTS, I32)])
    following = jnp.where(following < N_EXPERTS, following, -1)
    owner = tile_expert[:, None] == jnp.arange(N_EXPERTS, dtype=I32)[None, :]
    next_expert = jnp.sum(jnp.where(owner, following[None, :], 0), axis=1)
    slot = jnp.sum(jnp.where(owner, ((jnp.cumsum(owns.astype(I32)) - 1) % 2)[None, :], 0), axis=1)

    pad_count = padded - counts
    pad_total = jnp.sum(pad_count)
    pad_before = jnp.cumsum(pad_count) - pad_count
    entry = jnp.arange(N_EXPERTS * te, dtype=I32)
    which = entry % jnp.maximum(pad_total, 1)
    in_or_before = which[:, None] >= pad_before[None, :]
    expert_of = jnp.sum(in_or_before.astype(I32), axis=1) - 1
    mine = expert_of[:, None] == jnp.arange(N_EXPERTS, dtype=I32)[None, :]
    pad_rows = jnp.sum(jnp.where(mine, (starts + counts - pad_before)[None, :], 0), axis=1) + which
    pad_rows = jnp.where(pad_total > 0, pad_rows, n_rows - 1 - entry % N_EXPERTS)
    xs = _dispatch_sparsecore(pos.astype(I32), pad_rows.astype(I32), xn, n_rows)
    y = _experts(tile_expert, n_used, next_expert.astype(I32), slot.astype(I32), xs, w_e1, b1, w_e2, b2)
    y_tok = _gather_sparsecore(pos.astype(I32), y)
    out = _combine(h1, gates, p.reshape(n, PLE_DIM), w_ple_gate.astype(BF16),
                   w_ple_proj.astype(BF16), row(g_ple), row(g_final), y_tok, final_norm)
    return out.reshape(b, s, d)


def kernel(x, p, positions, g_attn, w_in, lambda_q1, lambda_k1, lambda_q2, lambda_k2, g_subln, b_forget, w_out, g_moe, w_router, b_router, w_e1, b_e1, w_e2, b_e2, g_ple, w_ple_gate, w_ple_proj, g_final):
    depth = g_attn.shape[0]
    h = x
    for i in range(depth):
        lambda_init = 0.8 - 0.6 * math.exp(-0.3 * i)
        h = _layer(h, p[i], positions, g_attn[i], w_in[i], lambda_q1[i], lambda_k1[i],
                   lambda_q2[i], lambda_k2[i], g_subln[i], b_forget[i], w_out[i], g_moe[i],
                   w_router[i], b_router[i], w_e1[i], b_e1[i], w_e2[i], b_e2[i], g_ple[i],
                   w_ple_gate[i], w_ple_proj[i], g_final, lambda_init, i == depth - 1)
    return h
```

```python
import functools
import math

import jax
import jax.numpy as jnp
from jax import lax
from jax.experimental import pallas as pl
from jax.experimental.pallas import tpu as pltpu
from jax.experimental.pallas import tpu_sc as plsc

F32 = jnp.float32
BF16 = jnp.bfloat16
I32 = jnp.int32
U32 = jnp.uint32

LANES = 128
BF16_TILE_ROWS = 16
D_MODEL = 1024
HEAD_DIM = 64
N_DIFF_HEADS = 4
DIFF_V_DIM = 2 * HEAD_DIM
N_FOX_HEADS = 8
DIFF_WIDTH = N_DIFF_HEADS * DIFF_V_DIM
FOX_WIDTH = N_FOX_HEADS * HEAD_DIM
ROT_DIM = HEAD_DIM // 4
ROPE_THETA = 500000.0
N_EXPERTS = 32
TOP_K = 4
D_EXPERT = D_MODEL
SWIGLU_ALPHA = 1.702
SWIGLU_LIMIT = 7.0
PLE_DIM = 256
NORM_EPS = 1e-5
LOG2_E = math.log2(math.e)
MAIN_WIDTH = 3 * DIFF_WIDTH + 3 * FOX_WIDTH

SEQ_TILE = 1024
SEQ_SUBTILE = 256
ATTN_TILE = 256
ATTN_UNITS = 2
TOKEN_TILE = 1024
COMBINE_SUBTILE = 256
ROUTER_TILE = 1024
ROUTER_SUBTILE = 256
EXPERT_TILE = 256
VMEM_LIMIT = 48 * 1024 * 1024
EXPERT_VMEM_LIMIT = 58 * 1024 * 1024


def _params(*semantics):
    return pltpu.CompilerParams(dimension_semantics=semantics, vmem_limit_bytes=VMEM_LIMIT)


def _rms(x, g):
    return x * lax.rsqrt(jnp.mean(x * x, axis=-1, keepdims=True) + NORM_EPS) * g


def _pack_rows(v):
    half = v.shape[1] // 2
    bits = pltpu.bitcast(v.astype(BF16).astype(F32), U32)
    return (bits[:, :half] >> 16) | bits[:, half:]


def _unpack_rows(words):
    low = pltpu.bitcast(words << 16, F32)
    high = pltpu.bitcast(words & jnp.uint32(0xFFFF0000), F32)
    return jnp.concatenate([low, high], axis=1)


def _inproj_kernel(pos_ref, x_ref, g_ref, w_ref, wfzt_ref, bfzt_ref,
                   dqt_ref, dk_ref, dvt_ref, fqt_ref, fk_ref, fvt_ref, c_ref, ct_ref, carry_t_ref):
    @pl.when(pl.program_id(1) == 0)
    def _():
        carry_t_ref[...] = jnp.zeros_like(carry_t_ref)

    ts = SEQ_SUBTILE
    subs = [slice(i * ts, (i + 1) * ts) for i in range(x_ref.shape[1] // ts)]
    xbs = [_rms(x_ref[0, rows, :], g_ref[...]).astype(BF16) for rows in subs]
    nt = (((1,), (1,)), ((), ()))
    tn = (((0,), (0,)), ((), ()))

    def pieces(v):
        hi = v.astype(BF16)
        r1 = v - hi.astype(F32)
        mid = r1.astype(BF16)
        return hi, mid, (r1 - mid.astype(F32)).astype(BF16)

    def rows_to_lanes(v, select):
        return sum(lax.dot_general(part, select, tn, preferred_element_type=F32) for part in pieces(v))

    half_rot = ROT_DIM // 2
    freq = lax.broadcasted_iota(I32, (half_rot, 1), 0)
    inv_freq = jnp.power(ROPE_THETA, -(2 * freq).astype(F32) / ROT_DIM)
    r = lax.broadcasted_iota(I32, (2 * half_rot, 2 * LANES), 0)
    lane2 = lax.broadcasted_iota(I32, (2 * half_rot, 2 * LANES), 1)
    d = lane2 % HEAD_DIM
    same_freq = jnp.where(d % half_rot == r % half_rot, 1.0, 0.0)
    cos_part = jnp.where((r < half_rot) & (lane2 < LANES) & (d < ROT_DIM), same_freq, 0.0)
    sin_part = jnp.where((r >= half_rot) & (lane2 >= LANES) & (d < ROT_DIM),
                         jnp.where(d < half_rot, -same_freq, same_freq), 0.0)
    expand = (cos_part + sin_part).astype(BF16)
    lane = lax.broadcasted_iota(I32, (1, LANES), 1) % HEAD_DIM
    first_half = lane < half_rot

    def rope_tables(rows):
        ang = inv_freq * pos_ref[0, :, rows].astype(F32)
        trig = jnp.concatenate([jnp.cos(ang), jnp.sin(ang)], axis=0)
        tables = rows_to_lanes(trig, expand)
        return jnp.where(lane < ROT_DIM, tables[:, :LANES], 1.0), tables[:, LANES:]

    def rope(t, cosf, sinf):
        outs = []
        for j in range(t.shape[1] // LANES):
            tj = t[:, j * LANES:(j + 1) * LANES]
            partner = jnp.where(first_half,
                                pltpu.roll(tj, LANES - ROT_DIM // 2, 1),
                                pltpu.roll(tj, ROT_DIM // 2, 1))
            outs.append(tj * cosf + partner * sinf)
        return jnp.concatenate(outs, axis=1)

    tables = [rope_tables(rows) for rows in subs]

    def log_sigmoid(z):
        return jnp.minimum(z, 0.0) - jnp.log1p(jnp.exp(-jnp.abs(z)))

    row = lax.broadcasted_iota(I32, (ts, ts), 0)
    col = lax.broadcasted_iota(I32, (ts, ts), 1)
    upto_col = jnp.where(row <= col, 1.0, 0.0).astype(BF16)
    heads = wfzt_ref.shape[0]
    head = lax.broadcasted_iota(I32, (heads, FOX_WIDTH), 0)
    group = lax.broadcasted_iota(I32, (heads, FOX_WIDTH), 1) // HEAD_DIM
    head_lanes = jnp.where(head == group, 1.0, 0.0).astype(BF16)

    for rows, xb in zip(subs, xbs):
        fzt = lax.dot_general(wfzt_ref[...], xb, nt, preferred_element_type=F32) + bfzt_ref[...]
        ct = carry_t_ref[...] + sum(jnp.dot(part, upto_col, preferred_element_type=F32)
                                    for part in pieces(log_sigmoid(fzt)))
        carry_t_ref[...] = ct[:, ts - 1:ts]
        ct = ct * LOG2_E
        ct_ref[0, :, rows] = ct[0:N_FOX_HEADS, :]
        c_ref[0, rows, :] = rows_to_lanes(ct, head_lanes)

    scale = HEAD_DIM ** -0.5 * LOG2_E
    w = DIFF_WIDTH
    projs = [jnp.dot(xb, w_ref[...], preferred_element_type=F32) for xb in xbs]
    for i, (rows, proj, (cosf, sinf)) in enumerate(zip(subs, projs, tables)):
        dqt_ref[0, :, rows] = (rope(proj[:, 0:w], cosf, sinf) * scale).astype(BF16).T
        dk_ref[0, rows, :] = rope(proj[:, w:2 * w], cosf, sinf).astype(BF16)
        dvt_ref[0, i] = proj[:, 2 * w:3 * w].astype(BF16).T
        fqt_ref[0, :, rows] = (proj[:, 3 * w:4 * w] * scale).astype(BF16).T
        fk_ref[0, rows, :] = proj[:, 4 * w:5 * w].astype(BF16)
        fvt_ref[0, i] = proj[:, 5 * w:6 * w].astype(BF16).T


def _inproj(positions, x, g, w_main, w_fz_t, b_fz_t):
    b, s, d = x.shape
    ts = SEQ_TILE
    sub = SEQ_SUBTILE
    assert sub == ATTN_TILE
    tile = lambda width: pl.BlockSpec((1, ts, width), lambda i, j: (i, j, 0))
    tile_t = pl.BlockSpec((1, DIFF_WIDTH, ts), lambda i, j: (i, 0, j))
    key_tile_t = pl.BlockSpec((1, ts // sub, DIFF_WIDTH, sub), lambda i, j: (i, j, 0, 0))
    full = lambda arr: pl.BlockSpec(arr.shape, lambda i, j: (0,) * arr.ndim)
    rows = jax.ShapeDtypeStruct((b, s, DIFF_WIDTH), BF16)
    rows_t = jax.ShapeDtypeStruct((b, DIFF_WIDTH, s), BF16)
    key_tiles_t = jax.ShapeDtypeStruct((b, s // sub, DIFF_WIDTH, sub), BF16)
    return pl.pallas_call(
        _inproj_kernel,
        grid=(b, s // ts),
        in_specs=[pl.BlockSpec((1, 1, ts), lambda i, j: (i, 0, j)), tile(d), full(g), full(w_main),
                  full(w_fz_t), full(b_fz_t)],
        out_specs=[tile_t, tile(DIFF_WIDTH), key_tile_t, tile_t, tile(FOX_WIDTH), key_tile_t,
                   tile(FOX_WIDTH), pl.BlockSpec((1, N_FOX_HEADS, ts), lambda i, j: (i, 0, j))],
        out_shape=[rows_t, rows, key_tiles_t, rows_t, rows, key_tiles_t,
                   jax.ShapeDtypeStruct((b, s, FOX_WIDTH), F32),
                   jax.ShapeDtypeStruct((b, N_FOX_HEADS, s), F32)],
        scratch_shapes=[pltpu.VMEM((w_fz_t.shape[0], 1), F32)],
        compiler_params=_params("arbitrary", "arbitrary"),
        name="inproj",
    )(positions.reshape(b, 1, s), x, g, w_main, w_fz_t, b_fz_t)


ONES_ROWS = 16


def _block_diag_queries(qt):
    row = lax.broadcasted_iota(I32, qt.shape, 0)
    zero = jnp.zeros_like(qt)
    return jnp.concatenate([jnp.where(row < HEAD_DIM, qt, zero),
                            jnp.where(row >= HEAD_DIM, qt, zero)], axis=1)


class _Chain:
    def __init__(self, qbd, keys, values_t, scratch, key_bias=None, query_bias=None):
        self.qbd, self.keys, self.values_t = qbd, keys, values_t
        self.acc_ref, self.s_ref, self.p_ref, self.stat_ref = scratch
        self.key_bias, self.query_bias = key_bias, query_bias


_RUNNING_MAX, _TILE_MAX, _RESCALE = 0, 1, 2


def _causal_attention(chains, qi):
    tk, tq = chains[0].s_ref.shape[0], chains[0].s_ref.shape[1] // 2
    ones = jnp.ones((ONES_ROWS, tk), BF16)

    def scores(ch, i, diagonal):
        s = jnp.dot(ch.keys(i), ch.qbd, preferred_element_type=F32)
        if ch.key_bias is not None:
            s = s - ch.key_bias(i)
        if diagonal is not False:
            r = lax.broadcasted_iota(I32, s.shape, 0)
            c = lax.broadcasted_iota(I32, s.shape, 1) & (tq - 1)
            visible = r <= c if diagonal is True else jnp.logical_or(r <= c, jnp.logical_not(diagonal))
            s = jnp.where(visible, s, -jnp.inf)
        ch.s_ref[...] = s
        top = jnp.max(s, axis=0, keepdims=True)
        if ch.query_bias is not None:
            top = top + ch.query_bias
        ch.stat_ref[_TILE_MAX:_TILE_MAX + 1, :] = top

    def probs(ch):
        m = ch.stat_ref[_RUNNING_MAX:_RUNNING_MAX + 1, :]
        m_new = jnp.maximum(m, ch.stat_ref[_TILE_MAX:_TILE_MAX + 1, :])
        ch.stat_ref[_RUNNING_MAX:_RUNNING_MAX + 1, :] = m_new
        ch.stat_ref[_RESCALE:_RESCALE + 1, :] = jnp.exp2(m - m_new)
        ref = m_new if ch.query_bias is None else m_new - ch.query_bias
        ch.p_ref[...] = jnp.exp2(ch.s_ref[...] - ref).astype(BF16)

    def accumulate(ch, i):
        vts = jnp.concatenate([ch.values_t(i), ones], axis=0)
        ch.acc_ref[...] = (ch.stat_ref[_RESCALE:_RESCALE + 1, :] * ch.acc_ref[...]
                           + jnp.dot(vts, ch.p_ref[...], preferred_element_type=F32))

    def body(k, diagonal_next):
        for ch in chains:
            accumulate(ch, jnp.maximum(k - 1, 0))
        for ch in chains:
            probs(ch)
        for ch in chains:
            scores(ch, k + 1, diagonal_next)

    for ch in chains:
        ch.acc_ref[...] = jnp.zeros_like(ch.acc_ref)
        ch.p_ref[...] = jnp.zeros_like(ch.p_ref)
        ch.stat_ref[...] = jnp.concatenate(
            [jnp.full((1, 2 * tq), -jnp.inf, F32), jnp.ones((ch.stat_ref.shape[0] - 1, 2 * tq), F32)], axis=0)
        scores(ch, 0, qi == 0)

    def loop_body(k, carry):
        body(k, False)
        return carry

    lax.fori_loop(0, qi - 1, loop_body, 0)

    @pl.when(qi >= 1)
    def _():
        body(qi - 1, True)

    for ch in chains:
        accumulate(ch, jnp.maximum(qi - 1, 0))
    for ch in chains:
        probs(ch)
    for ch in chains:
        accumulate(ch, qi)


def _attention_kernel(lq1_ref, lk1_ref, lq2_ref, lk2_ref, g_ref,
                      dqt_ref, dk_ref, dvt_ref, fqt_ref, fk_ref, fvt_ref, cq_ref, ck_ref,
                      do_ref, fo_ref, *scratch, lambda_init):
    tq = dqt_ref.shape[2]
    tk = dvt_ref.shape[3]
    units = dqt_ref.shape[1] // LANES
    lane = lax.broadcasted_iota(I32, (1, LANES), 1)
    scratch = [scratch[i:i + 4] for i in range(0, len(scratch), 4)]

    def unit_chains(u):
        lanes = slice(u * LANES, (u + 1) * LANES)

        def tile(ref):
            return lambda i: ref[0, pl.ds(pl.multiple_of(i * tk, tk), tk), lanes]

        def tile_t(ref):
            return lambda i: ref[0, i, lanes, :]

        cq = jnp.concatenate([cq_ref[0, u, 0:1, :], cq_ref[0, u, 1:2, :]], axis=1)

        def key_bias(i):
            ck = tile(ck_ref)(i)
            other = pltpu.roll(ck, HEAD_DIM, 1)
            first = jnp.where(lane < HEAD_DIM, ck, other)
            second = jnp.where(lane >= HEAD_DIM, ck, other)
            return jnp.concatenate([first] * (tq // LANES) + [second] * (tq // LANES), axis=1)

        return [_Chain(_block_diag_queries(dqt_ref[0, lanes, :]), tile(dk_ref), tile_t(dvt_ref),
                       scratch[2 * u]),
                _Chain(_block_diag_queries(fqt_ref[0, lanes, :]), tile(fk_ref), tile_t(fvt_ref),
                       scratch[2 * u + 1], key_bias, cq)]

    chains = [ch for u in range(units) for ch in unit_chains(u)]
    _causal_attention(chains, pl.program_id(2))

    lam = (jnp.exp(jnp.sum(lq1_ref[...] * lk1_ref[...], axis=-1, keepdims=True))
           - jnp.exp(jnp.sum(lq2_ref[...] * lk2_ref[...], axis=-1, keepdims=True)) + lambda_init)
    dv, h = DIFF_V_DIM, HEAD_DIM
    for u in range(units):
        lanes = slice(u * LANES, (u + 1) * LANES)
        acc = chains[2 * u].acc_ref
        o = (acc[0:dv, 0:tq] / acc[dv:dv + 1, 0:tq]
             - lam * (acc[0:dv, tq:] / acc[dv:dv + 1, tq:]))
        ms = jnp.mean(o * o, axis=0, keepdims=True)
        do_ref[0, :, lanes] = (o * lax.rsqrt(ms + NORM_EPS) * g_ref[...]
                               * (1.0 - lambda_init)).astype(BF16).T
        acc = chains[2 * u + 1].acc_ref
        fo_ref[0, :, lanes] = jnp.concatenate([acc[0:h, 0:tq] / acc[2 * h:2 * h + 1, 0:tq],
                                               acc[h:2 * h, tq:] / acc[2 * h:2 * h + 1, tq:]],
                                              axis=0).astype(BF16).T


def _attention(dqt, dk, dvt, fqt, fk, fvt, c_rows, c_cols, lq1, lk1, lq2, lk2, g_col, lambda_init):
    b, s, _ = dk.shape
    t = ATTN_TILE
    assert N_DIFF_HEADS == N_FOX_HEADS // 2 and N_DIFF_HEADS % ATTN_UNITS == 0
    width = ATTN_UNITS * LANES
    vec = lambda arr: pl.BlockSpec(arr.shape, lambda i, h, j: (0, 0))
    q_spec = pl.BlockSpec((1, width, t), lambda i, h, j: (i, h, j))
    k_spec = pl.BlockSpec((1, s, width), lambda i, h, j: (i, 0, h))
    v_spec = pl.BlockSpec((1, s // t, width, t), lambda i, h, j: (i, 0, h, 0))
    o_spec = pl.BlockSpec((1, t, width), lambda i, h, j: (i, j, h))
    out = jax.ShapeDtypeStruct((b, s, DIFF_WIDTH), BF16)
    chain_scratch = [pltpu.VMEM((LANES + ONES_ROWS, 2 * t), F32),
                     pltpu.VMEM((t, 2 * t), F32),
                     pltpu.VMEM((t, 2 * t), BF16),
                     pltpu.VMEM((8, 2 * t), F32)]
    return pl.pallas_call(
        functools.partial(_attention_kernel, lambda_init=lambda_init),
        grid=(b, N_DIFF_HEADS // ATTN_UNITS, s // t),
        in_specs=[vec(lq1), vec(lk1), vec(lq2), vec(lk2), vec(g_col),
                  q_spec, k_spec, v_spec, q_spec, k_spec, v_spec,
                  pl.BlockSpec((1, ATTN_UNITS, 2, t), lambda i, h, j: (i, h, 0, j)), k_spec],
        out_specs=[o_spec, o_spec],
        out_shape=[out, out],
        scratch_shapes=chain_scratch * (2 * ATTN_UNITS),
        compiler_params=_params("arbitrary", "arbitrary", "arbitrary"),
        name="attention",
    )(lq1, lk1, lq2, lk2, g_col, dqt, dk, dvt, fqt, fk, fvt, c_rows, c_cols)


def _router_kernel(d_ref, f_ref, x_ref, wo_ref, g_ref, wrt_ref, brt_ref,
                   h_ref, xn_ref, gate_ref, idx_ref, rank_ref, count_ref, carry_ref):
    tm = ROUTER_SUBTILE
    nt = (((1,), (1,)), ((), ()))

    def pieces(v, count):
        out = []
        for _ in range(count):
            part = v.astype(BF16)
            out.append(part)
            v = v - part.astype(F32)
        return out

    wh = wrt_ref[...].astype(BF16)
    row = lax.broadcasted_iota(I32, (tm, tm), 0)
    col = lax.broadcasted_iota(I32, (tm, tm), 1)
    earlier = jnp.where(row < col, 1.0, 0.0).astype(BF16)
    eye = jnp.where(row == col, 1.0, 0.0).astype(BF16)

    @pl.when(pl.program_id(0) == 0)
    def _():
        carry_ref[...] = jnp.zeros_like(carry_ref)

    subs = [slice(i * tm, (i + 1) * tm) for i in range(x_ref.shape[0] // tm)]

    def logits(rows):
        mixed = (jnp.dot(d_ref[rows, :], wo_ref[0:DIFF_WIDTH, :], preferred_element_type=F32)
                 + jnp.dot(f_ref[rows, :], wo_ref[DIFF_WIDTH:, :], preferred_element_type=F32))
        h = x_ref[rows, :] + mixed
        h_ref[rows, :] = h
        xn = _rms(h, g_ref[...])
        xn_ref[rows, :] = _pack_rows(xn)
        return lax.dot_general(wh, xn.astype(BF16), nt, preferred_element_type=F32) + brt_ref[...]

    def top_k(work):
        expert = lax.broadcasted_iota(I32, work.shape, 0)
        vals, ids, picks = [], [], []
        for _ in range(TOP_K):
            m = jnp.max(work, axis=0, keepdims=True)
            idx = jnp.min(jnp.where(work == m, expert, N_EXPERTS), axis=0, keepdims=True)
            pick = expert == idx
            vals.append(m)
            ids.append(idx)
            picks.append(pick)
            work = jnp.where(pick, -jnp.inf, work)
        exps = [jnp.exp(v - vals[0]) for v in vals]
        denom = exps[0] + exps[1] + exps[2] + exps[3]
        return ids, picks, [e / denom for e in exps]

    def ranks(picks):
        chosen = jnp.where(picks[0] | picks[1] | picks[2] | picks[3], 1.0, 0.0)
        rank = jnp.dot(chosen.astype(BF16), earlier, preferred_element_type=F32) + carry_ref[...]
        carry_ref[...] = carry_ref[...] + jnp.sum(chosen, axis=1, keepdims=True)
        return rank

    def emit(rows, ids, picks, gates, rank):
        slot = lax.broadcasted_iota(I32, (8, tm), 0)
        gate_t = jnp.zeros((8, tm), F32)
        idx_t = jnp.zeros((8, tm), I32)
        rank_t = jnp.zeros((8, tm), F32)
        for k in range(TOP_K):
            gate_t = jnp.where(slot == k, gates[k], gate_t)
            idx_t = jnp.where(slot == k, ids[k], idx_t)
            rank_t = jnp.where(slot == k, jnp.sum(jnp.where(picks[k], rank, 0.0), axis=0, keepdims=True),
                               rank_t)
        idx_ref[:, rows] = idx_t
        rank_ref[:, rows] = rank_t.astype(I32)
        gate_rows = jnp.concatenate([gate_t, jnp.zeros((LANES - 8, tm), F32)], axis=0)
        gate_ref[rows, :] = sum(lax.dot_general(eye, part, nt, preferred_element_type=F32)
                                for part in pieces(gate_rows, 3))

    work = [logits(rows) for rows in subs]
    chosen = [top_k(w) for w in work]
    rank = [ranks(picks) for _, picks, _ in chosen]
    for rows, (ids, picks, gates), r in zip(subs, chosen, rank):
        emit(rows, ids, picks, gates, r)
    count_ref[...] = jnp.broadcast_to(carry_ref[...], count_ref.shape)


def _router(d_out, f_out, x, w_out, g_moe, w_router_t, b_router_t):
    n, d = x.shape
    tm = ROUTER_TILE
    tile = lambda width: pl.BlockSpec((tm, width), lambda i: (i, 0))
    rows = pl.BlockSpec((8, tm), lambda i: (0, i))
    full = lambda arr: pl.BlockSpec(arr.shape, lambda i: (0,) * arr.ndim)
    return pl.pallas_call(
        _router_kernel,
        grid=(n // tm,),
        in_specs=[tile(DIFF_WIDTH), tile(FOX_WIDTH), tile(d), full(w_out), full(g_moe),
                  full(w_router_t), full(b_router_t)],
        out_specs=[tile(d), tile(d // 2), tile(LANES), rows, rows,
                   pl.BlockSpec((N_EXPERTS, LANES), lambda i: (0, 0))],
        out_shape=[jax.ShapeDtypeStruct((n, d), F32), jax.ShapeDtypeStruct((n, d // 2), U32),
                   jax.ShapeDtypeStruct((n, LANES), F32), jax.ShapeDtypeStruct((8, n), I32),
                   jax.ShapeDtypeStruct((8, n), I32), jax.ShapeDtypeStruct((N_EXPERTS, LANES), F32)],
        scratch_shapes=[pltpu.VMEM((N_EXPERTS, 1), F32)],
        compiler_params=_params("arbitrary"),
        name="router",
    )(d_out, f_out, x, w_out, g_moe, w_router_t, b_router_t)


def _slot_kernel(starts_ref, idx_ref, rank_ref, pos_ref):
    idx = idx_ref[...]
    pos = rank_ref[...]
    for e in range(N_EXPERTS):
        pos = jnp.where(idx == e, pos + starts_ref[e], pos)
    pos_ref[...] = pos


def _slots(starts, idx, rank):
    rows, n = idx.shape
    width = min(n, 2048)
    block = pl.BlockSpec((rows, width), lambda i, starts: (0, i))
    return pl.pallas_call(
        _slot_kernel,
        grid_spec=pltpu.PrefetchScalarGridSpec(num_scalar_prefetch=1, grid=(n // width,),
                                               in_specs=[block, block], out_specs=block),
        out_shape=jax.ShapeDtypeStruct((rows, n), I32),
        compiler_params=_params("arbitrary"),
        name="slots",
    )(starts, idx, rank)


SC_WINDOW = 128
SC_COLUMNS = 256


def _dispatch_sparsecore(pos, pad_rows, xn, n_rows):
    n, d = xn.shape
    mesh = plsc.VectorSubcoreMesh(core_axis_name="core", subcore_axis_name="subcore")
    spread = dict(core_axis_name=("core", "subcore"), dimension_semantics=(pltpu.PARALLEL,))

    @functools.partial(pl.kernel, out_type=jax.ShapeDtypeStruct((n_rows, d), xn.dtype), mesh=mesh,
                       scratch_types=[], name="dispatch_sc")
    def scatter(x_hbm, pos_hbm, zero_hbm, pad_hbm, xs_hbm):
        for c in range(d // SC_COLUMNS):
            columns = xs_hbm.at[:, pl.ds(c * SC_COLUMNS, SC_COLUMNS)]

            def body(x_vmem, pos_vmem, columns=columns):
                for k in range(TOP_K):
                    pltpu.sync_copy(x_vmem, columns.at[pos_vmem.at[k]])

            pltpu.emit_pipeline(
                body, grid=(n // SC_WINDOW,),
                in_specs=[pl.BlockSpec((SC_WINDOW, SC_COLUMNS), index_map=lambda i, c=c: (i, c)),
                          pl.BlockSpec((TOP_K, SC_WINDOW), index_map=lambda i: (0, i))],
                out_specs=[], **spread)(x_hbm, pos_hbm)

            def pad_body(zero_vmem, pad_vmem, columns=columns):
                pltpu.sync_copy(zero_vmem, columns.at[pad_vmem.at[0]])

            pltpu.emit_pipeline(
                pad_body, grid=(pad_rows.size // SC_WINDOW,),
                in_specs=[pl.BlockSpec((SC_WINDOW, SC_COLUMNS), index_map=lambda i: (0, 0)),
                          pl.BlockSpec((1, SC_WINDOW), index_map=lambda i: (0, i))],
                out_specs=[], **spread)(zero_hbm, pad_hbm)

    return scatter(xn, pos, jnp.zeros((SC_WINDOW, SC_COLUMNS), xn.dtype), pad_rows.reshape(1, -1))


def _expert_kernel(tile_expert_ref, n_used_ref, next_expert_ref, slot_ref,
                   x_ref, w1_ref, b1_ref, w2_ref, b2_ref, y_ref,
                   w1f_ref, w2f_ref, sem_ref, w1b_ref, w2p_ref, w2b_ref):
    t = pl.program_id(0)
    fresh = (t == 0) | (tile_expert_ref[t] != tile_expert_ref[jnp.maximum(t - 1, 0)])
    used = t < n_used_ref[0]

    def weight_copies(e, s):
        return (pltpu.make_async_copy(w1_ref.at[e], w1f_ref.at[s], sem_ref.at[0, s]),
                pltpu.make_async_copy(w2_ref.at[e], w2f_ref.at[s], sem_ref.at[1, s]))

    @pl.when(t == 0)
    def _():
        for copy in weight_copies(tile_expert_ref[0], slot_ref[0]):
            copy.start()

    @pl.when(fresh & used)
    def _():
        s = slot_ref[t]
        for copy in weight_copies(tile_expert_ref[t], s):
            copy.wait()

        @pl.when(next_expert_ref[t] >= 0)
        def _():
            for copy in weight_copies(next_expert_ref[t], 1 - s):
                copy.start()

        rows = LANES
        half = LANES // 2
        for r in range(0, w1b_ref.shape[0], rows):
            w1b_ref[r:r + rows, :] = w1f_ref[s, r:r + rows, :].astype(BF16)
        for c in range(w2p_ref.shape[0]):
            lanes = slice(c * LANES, (c + 1) * LANES)
            for r in range(0, w2b_ref.shape[0], rows):
                for b in range(2):
                    w2p_ref[c, pl.ds(r + b, half, stride=2), :] = (
                        w2f_ref[s, r + b * half:r + (b + 1) * half, lanes])
            w2b_ref[:, lanes] = w2p_ref[c].astype(BF16)

    @pl.when(used)
    def _():
        xb = _unpack_rows(x_ref[...]).astype(BF16)
        h = jnp.dot(xb, w1b_ref[...], preferred_element_type=F32) + b1_ref[0]
        even = lax.broadcasted_iota(I32, (1, LANES), 1) % 2 == 0
        parts = []
        for q in range(h.shape[1] // (2 * LANES)):
            a = h[:, (2 * q) * LANES:(2 * q + 1) * LANES]
            b = h[:, (2 * q + 1) * LANES:(2 * q + 2) * LANES]
            hg = jnp.where(even, a, pltpu.roll(b, 1, 1))
            hl = jnp.where(even, pltpu.roll(a, LANES - 1, 1), b)
            glu = jnp.minimum(hg, SWIGLU_LIMIT)
            lin = jnp.clip(hl, -SWIGLU_LIMIT, SWIGLU_LIMIT)
            parts.append((glu * jax.nn.sigmoid(SWIGLU_ALPHA * glu) * (lin + 1.0)).astype(BF16))
        act = jnp.concatenate(parts, axis=1)
        y_ref[...] = _pack_rows(jnp.dot(act, w2b_ref[...], preferred_element_type=F32) + b2_ref[0])

    @pl.when(jnp.logical_not(used))
    def _():
        y_ref[...] = jnp.zeros_like(y_ref)


def _experts(tile_expert, n_used, next_expert, slot, xs, w1, b1, w2, b2):
    n_rows, d = xs.shape
    tm = EXPERT_TILE
    by_expert = lambda arr: pl.BlockSpec((1,) + arr.shape[1:], lambda i, te, *_: (te[i], 0, 0))
    rows = pl.BlockSpec((tm, d), lambda i, *_: (i, 0))
    used_rows = pl.BlockSpec((tm, d), lambda i, te, nu, *_: (jnp.minimum(i, nu[0] - 1), 0))
    hbm = pl.BlockSpec(memory_space=pl.ANY)
    return pl.pallas_call(
        _expert_kernel,
        grid_spec=pltpu.PrefetchScalarGridSpec(
            num_scalar_prefetch=4,
            grid=(n_rows // tm,),
            in_specs=[used_rows, hbm, by_expert(b1), hbm, by_expert(b2)],
            out_specs=rows,
            scratch_shapes=[pltpu.VMEM((2,) + w1.shape[1:], F32), pltpu.VMEM((2,) + w2.shape[1:], F32),
                            pltpu.SemaphoreType.DMA((2, 2)),
                            pltpu.VMEM(w1.shape[1:], BF16),
                            pltpu.VMEM((w2.shape[2] // LANES, w2.shape[1], LANES), F32),
                            pltpu.VMEM(w2.shape[1:], BF16)]),
        out_shape=jax.ShapeDtypeStruct(xs.shape, xs.dtype),
        compiler_params=pltpu.CompilerParams(dimension_semantics=("arbitrary",),
                                             vmem_limit_bytes=EXPERT_VMEM_LIMIT),
        name="experts",
    )(tile_expert, n_used, next_expert, slot, xs, w1, b1, w2, b2)


def _gather_sparsecore(pos, y):
    d = y.shape[1]
    total = pos.size
    mesh = plsc.VectorSubcoreMesh(core_axis_name="core", subcore_axis_name="subcore")

    @functools.partial(pl.kernel, out_type=jax.ShapeDtypeStruct((total, d), y.dtype), mesh=mesh,
                       scratch_types=[], name="gather_sc")
    def gather(y_hbm, pos_hbm, out_hbm):
        for c in range(d // SC_COLUMNS):
            def body(pos_vmem, out_vmem, c=c):
                pltpu.sync_copy(y_hbm.at[:, pl.ds(c * SC_COLUMNS, SC_COLUMNS)].at[pos_vmem.at[0]], out_vmem)

            pltpu.emit_pipeline(
                body, grid=(total // SC_WINDOW,),
                in_specs=[pl.BlockSpec((1, SC_WINDOW), index_map=lambda i: (0, i))],
                out_specs=[pl.BlockSpec((SC_WINDOW, SC_COLUMNS), index_map=lambda i, c=c: (i, c))],
                core_axis_name=("core", "subcore"),
                dimension_semantics=(pltpu.PARALLEL,))(pos_hbm, out_hbm)

    return gather(y, pos.reshape(1, -1))


def _combine_kernel(h_ref, gate_ref, p_ref, wg_ref, wp_ref, gple_ref, gfin_ref, *refs, final_norm):
    y_refs, o_ref = refs[:TOP_K], refs[TOP_K]
    ts = COMBINE_SUBTILE
    subs = [slice(i * ts, (i + 1) * ts) for i in range(h_ref.shape[0] // ts)]

    def mix(rows):
        gates = gate_ref[rows, :]
        moe = gates[:, 0:1] * _unpack_rows(y_refs[0][rows, :])
        for k in range(1, TOP_K):
            moe = moe + gates[:, k:k + 1] * _unpack_rows(y_refs[k][rows, :])
        h = h_ref[rows, :] + moe
        return h, jnp.dot(_rms(h, gple_ref[...]).astype(BF16), wg_ref[...], preferred_element_type=F32)

    embs = [jnp.dot(p_ref[rows, :].astype(BF16), wp_ref[...], preferred_element_type=F32) for rows in subs]
    mixed = [mix(rows) for rows in subs]
    for rows, (h, logit), emb in zip(subs, mixed, embs):
        h = h + jax.nn.sigmoid(logit) * emb
        o_ref[rows, :] = _rms(h, gfin_ref[...]) if final_norm else h


def _combine(h, gates, p, w_gate, w_proj, g_ple, g_final, y_tok, final_norm):
    n, d = h.shape
    tm = TOKEN_TILE
    tile = lambda width: pl.BlockSpec((tm, width), lambda i: (i, 0))
    full = lambda arr: pl.BlockSpec(arr.shape, lambda i: (0,) * arr.ndim)
    choice = lambda k: pl.BlockSpec((tm, y_tok.shape[1]), lambda i: (k * (n // tm) + i, 0))
    return pl.pallas_call(
        functools.partial(_combine_kernel, final_norm=final_norm),
        grid=(n // tm,),
        in_specs=[tile(d), tile(LANES), tile(p.shape[1]), full(w_gate), full(w_proj),
                  full(g_ple), full(g_final)] + [choice(k) for k in range(TOP_K)],
        out_specs=tile(d),
        out_shape=jax.ShapeDtypeStruct((n, d), F32),
        compiler_params=_params("arbitrary"),
        name="combine",
    )(h, gates, p, w_gate, w_proj, g_ple, g_final, *([y_tok] * TOP_K))


def _layer(h, p, positions, g_attn, w_in, lq1, lk1, lq2, lk2, g_subln, b_forget, w_out, g_moe,
           w_router, b_router, w_e1, b_e1, w_e2, b_e2, g_ple, w_ple_gate, w_ple_proj, g_final,
           lambda_init, final_norm):
    b, s, d = h.shape
    n = b * s
    row = lambda v: v.reshape(1, -1)

    w_main = w_in[:, :MAIN_WIDTH].astype(BF16)
    pad_heads = ((0, BF16_TILE_ROWS - N_FOX_HEADS), (0, 0))
    w_fz_t = jnp.pad(w_in[:, MAIN_WIDTH:].T, pad_heads).astype(BF16)
    b_fz_t = jnp.pad(b_forget.reshape(-1, 1), pad_heads)
    b1 = b_e1[:, None, :]
    b2 = b_e2[:, None, :]

    dq_t, dk, dv_t, fq_t, fk, fv_t, c, c_t = _inproj(positions, h, row(g_attn), w_main, w_fz_t, b_fz_t)
    c_rows = c_t.reshape(b, N_FOX_HEADS // 2, 2, s)
    d_out, f_out = _attention(dq_t, dk, dv_t, fq_t, fk, fv_t, c_rows, c, row(lq1), row(lk1), row(lq2),
                              row(lk2), g_subln.reshape(-1, 1), lambda_init)

    h1, xn, gates, idx, rank, counts = _router(
        d_out.reshape(n, DIFF_WIDTH), f_out.reshape(n, FOX_WIDTH), h.reshape(n, d),
        w_out.astype(BF16), row(g_moe), w_router.T, b_router.reshape(-1, 1))

    te = EXPERT_TILE
    n_rows = n * TOP_K + N_EXPERTS * te
    counts = counts[:, 0].astype(I32)
    padded = (counts + te - 1) // te * te
    ends = jnp.cumsum(padded)
    starts = ends - padded
    pos = _slots(starts.astype(I32), idx, rank)[:TOP_K]
    tile_start = jnp.arange(n_rows // te, dtype=I32) * te
    tile_expert = jnp.minimum(jnp.sum((ends[None, :] <= tile_start[:, None]).astype(I32), axis=1),
                              N_EXPERTS - 1)
    n_used = (ends[-1:] // te).astype(I32)
    owns = padded > 0
    later = lax.cummin(jnp.where(owns, jnp.arange(N_EXPERTS, dtype=I32), N_EXPERTS), reverse=True)
    following = jnp.concatenate([later[1:], jnp.full((1,), N_EXPERTS, I32)])
    following = jnp.where(following < N_EXPERTS, following, -1)
    owner = tile_expert[:, None] == jnp.arange(N_EXPERTS, dtype=I32)[None, :]
    next_expert = jnp.sum(jnp.where(owner, following[None, :], 0), axis=1)
    slot = jnp.sum(jnp.where(owner, ((jnp.cumsum(owns.astype(I32)) - 1) % 2)[None, :], 0), axis=1)

    pad_count = padded - counts
    pad_total = jnp.sum(pad_count)
    pad_before = jnp.cumsum(pad_count) - pad_count
    entry = jnp.arange(N_EXPERTS * te, dtype=I32)
    in_or_before = entry[:, None] >= pad_before[None, :]
    expert_of = jnp.sum(in_or_before.astype(I32), axis=1) - 1
    mine = expert_of[:, None] == jnp.arange(N_EXPERTS, dtype=I32)[None, :]
    pad_rows = jnp.sum(jnp.where(mine, (starts + counts - pad_before)[None, :], 0), axis=1) + entry
    pad_rows = jnp.where(entry < pad_total, pad_rows, ends[-1] + entry - pad_total)
    xs = _dispatch_sparsecore(pos.astype(I32), pad_rows.astype(I32), xn, n_rows)
    y = _experts(tile_expert, n_used, next_expert.astype(I32), slot.astype(I32), xs, w_e1, b1, w_e2, b2)
    y_tok = _gather_sparsecore(pos.astype(I32), y)
    out = _combine(h1, gates, p.reshape(n, PLE_DIM), w_ple_gate.astype(BF16),
                   w_ple_proj.astype(BF16), row(g_ple), row(g_final), y_tok, final_norm)
    return out.reshape(b, s, d)


def kernel(x, p, positions, g_attn, w_in, lambda_q1, lambda_k1, lambda_q2, lambda_k2, g_subln, b_forget, w_out, g_moe, w_router, b_router, w_e1, b_e1, w_e2, b_e2, g_ple, w_ple_gate, w_ple_proj, g_final):
    depth = g_attn.shape[0]
    h = x
    for i in range(depth):
        lambda_init = 0.8 - 0.6 * math.exp(-0.3 * i)
        h = _layer(h, p[i], positions, g_attn[i], w_in[i], lambda_q1[i], lambda_k1[i],
                   lambda_q2[i], lambda_k2[i], g_subln[i], b_forget[i], w_out[i], g_moe[i],
                   w_router[i], b_router[i], w_e1[i], b_e1[i], w_e2[i], b_e2[i], g_ple[i],
                   w_ple_gate[i], w_ple_proj[i], g_final, lambda_init, i == depth - 1)
    return h
```

```python
import functools
import math

import jax
import jax.numpy as jnp
from jax import lax
from jax.experimental import pallas as pl
from jax.experimental.pallas import tpu as pltpu
from jax.experimental.pallas import tpu_sc as plsc

F32 = jnp.float32
BF16 = jnp.bfloat16
I32 = jnp.int32
U32 = jnp.uint32

LANES = 128
BF16_TILE_ROWS = 16
D_MODEL = 1024
HEAD_DIM = 64
N_DIFF_HEADS = 4
DIFF_V_DIM = 2 * HEAD_DIM
N_FOX_HEADS = 8
DIFF_WIDTH = N_DIFF_HEADS * DIFF_V_DIM
FOX_WIDTH = N_FOX_HEADS * HEAD_DIM
ROT_DIM = HEAD_DIM // 4
ROPE_THETA = 500000.0
N_EXPERTS = 32
TOP_K = 4
D_EXPERT = D_MODEL
SWIGLU_ALPHA = 1.702
SWIGLU_LIMIT = 7.0
PLE_DIM = 256
NORM_EPS = 1e-5
LOG2_E = math.log2(math.e)
MAIN_WIDTH = 3 * DIFF_WIDTH + 3 * FOX_WIDTH

SEQ_TILE = 1024
SEQ_SUBTILE = 256
ATTN_TILE = 256
ATTN_UNITS = 2
TOKEN_TILE = 1024
COMBINE_SUBTILE = 256
ROUTER_TILE = 1024
ROUTER_SUBTILE = 256
EXPERT_TILE = 256
VMEM_LIMIT = 48 * 1024 * 1024
EXPERT_VMEM_LIMIT = 58 * 1024 * 1024


def _params(*semantics):
    return pltpu.CompilerParams(dimension_semantics=semantics, vmem_limit_bytes=VMEM_LIMIT)


def _rms(x, g):
    return x * lax.rsqrt(jnp.mean(x * x, axis=-1, keepdims=True) + NORM_EPS) * g


def _pack_rows(v):
    half = v.shape[1] // 2
    bits = pltpu.bitcast(v.astype(BF16).astype(F32), U32)
    return (bits[:, :half] >> 16) | bits[:, half:]


def _unpack_rows(words):
    low = pltpu.bitcast(words << 16, F32)
    high = pltpu.bitcast(words & jnp.uint32(0xFFFF0000), F32)
    return jnp.concatenate([low, high], axis=1)


def _inproj_kernel(pos_ref, x_ref, g_ref, w_ref, wt_ref, wfzt_ref, bfzt_ref,
                   dqt_ref, dk_ref, dvt_ref, fqt_ref, fk_ref, fvt_ref, c_ref, ct_ref, carry_t_ref):
    @pl.when(pl.program_id(1) == 0)
    def _():
        carry_t_ref[...] = jnp.zeros_like(carry_t_ref)

    ts = SEQ_SUBTILE
    subs = [slice(i * ts, (i + 1) * ts) for i in range(x_ref.shape[1] // ts)]
    xbs = [_rms(x_ref[0, rows, :], g_ref[...]).astype(BF16) for rows in subs]
    nt = (((1,), (1,)), ((), ()))
    tn = (((0,), (0,)), ((), ()))

    def pieces(v):
        hi = v.astype(BF16)
        r1 = v - hi.astype(F32)
        mid = r1.astype(BF16)
        return hi, mid, (r1 - mid.astype(F32)).astype(BF16)

    def rows_to_lanes(v, select):
        return sum(lax.dot_general(part, select, tn, preferred_element_type=F32) for part in pieces(v))

    half_rot = ROT_DIM // 2
    freq = lax.broadcasted_iota(I32, (half_rot, 1), 0)
    inv_freq = jnp.power(ROPE_THETA, -(2 * freq).astype(F32) / ROT_DIM)
    r = lax.broadcasted_iota(I32, (2 * half_rot, 2 * LANES), 0)
    lane2 = lax.broadcasted_iota(I32, (2 * half_rot, 2 * LANES), 1)
    d = lane2 % HEAD_DIM
    same_freq = jnp.where(d % half_rot == r % half_rot, 1.0, 0.0)
    cos_part = jnp.where((r < half_rot) & (lane2 < LANES) & (d < ROT_DIM), same_freq, 0.0)
    sin_part = jnp.where((r >= half_rot) & (lane2 >= LANES) & (d < ROT_DIM),
                         jnp.where(d < half_rot, -same_freq, same_freq), 0.0)
    expand = (cos_part + sin_part).astype(BF16)
    lane = lax.broadcasted_iota(I32, (1, LANES), 1) % HEAD_DIM
    first_half = lane < half_rot

    def rope_tables(rows):
        ang = inv_freq * pos_ref[0, :, rows].astype(F32)
        trig = jnp.concatenate([jnp.cos(ang), jnp.sin(ang)], axis=0)
        tables = rows_to_lanes(trig, expand)
        return jnp.where(lane < ROT_DIM, tables[:, :LANES], 1.0), tables[:, LANES:]

    def rope(t, cosf, sinf):
        outs = []
        for j in range(t.shape[1] // LANES):
            tj = t[:, j * LANES:(j + 1) * LANES]
            partner = jnp.where(first_half,
                                pltpu.roll(tj, LANES - ROT_DIM // 2, 1),
                                pltpu.roll(tj, ROT_DIM // 2, 1))
            outs.append(tj * cosf + partner * sinf)
        return jnp.concatenate(outs, axis=1)

    tables = [rope_tables(rows) for rows in subs]

    def log_sigmoid(z):
        return jnp.minimum(z, 0.0) - jnp.log1p(jnp.exp(-jnp.abs(z)))

    row = lax.broadcasted_iota(I32, (ts, ts), 0)
    col = lax.broadcasted_iota(I32, (ts, ts), 1)
    upto_col = jnp.where(row <= col, 1.0, 0.0).astype(BF16)
    heads = wfzt_ref.shape[0]
    head = lax.broadcasted_iota(I32, (heads, FOX_WIDTH), 0)
    group = lax.broadcasted_iota(I32, (heads, FOX_WIDTH), 1) // HEAD_DIM
    head_lanes = jnp.where(head == group, 1.0, 0.0).astype(BF16)

    for rows, xb in zip(subs, xbs):
        fzt = lax.dot_general(wfzt_ref[...], xb, nt, preferred_element_type=F32) + bfzt_ref[...]
        ct = carry_t_ref[...] + sum(jnp.dot(part, upto_col, preferred_element_type=F32)
                                    for part in pieces(log_sigmoid(fzt)))
        carry_t_ref[...] = ct[:, ts - 1:ts]
        ct = ct * LOG2_E
        ct_ref[0, :, rows] = ct[0:N_FOX_HEADS, :]
        c_ref[0, rows, :] = rows_to_lanes(ct, head_lanes)

    scale = HEAD_DIM ** -0.5 * LOG2_E
    w = DIFF_WIDTH
    projs = [jnp.dot(xb, w_ref[...], preferred_element_type=F32) for xb in xbs]
    projs_t = [lax.dot_general(wt_ref[...], xb, nt, preferred_element_type=F32) for xb in xbs]
    for i, (rows, proj, proj_t, (cosf, sinf)) in enumerate(zip(subs, projs, projs_t, tables)):
        dqt_ref[0, :, rows] = (rope(proj[:, 0:w], cosf, sinf) * scale).astype(BF16).T
        dk_ref[0, rows, :] = rope(proj[:, w:2 * w], cosf, sinf).astype(BF16)
        fk_ref[0, rows, :] = proj[:, 2 * w:3 * w].astype(BF16)
        dvt_ref[0, i] = proj_t[0:w, :].astype(BF16)
        fqt_ref[0, :, rows] = (proj_t[w:2 * w, :] * scale).astype(BF16)
        fvt_ref[0, i] = proj_t[2 * w:3 * w, :].astype(BF16)


def _inproj(positions, x, g, w_main, w_main_t, w_fz_t, b_fz_t):
    b, s, d = x.shape
    ts = SEQ_TILE
    sub = SEQ_SUBTILE
    assert sub == ATTN_TILE
    tile = lambda width: pl.BlockSpec((1, ts, width), lambda i, j: (i, j, 0))
    tile_t = pl.BlockSpec((1, DIFF_WIDTH, ts), lambda i, j: (i, 0, j))
    key_tile_t = pl.BlockSpec((1, ts // sub, DIFF_WIDTH, sub), lambda i, j: (i, j, 0, 0))
    full = lambda arr: pl.BlockSpec(arr.shape, lambda i, j: (0,) * arr.ndim)
    rows = jax.ShapeDtypeStruct((b, s, DIFF_WIDTH), BF16)
    rows_t = jax.ShapeDtypeStruct((b, DIFF_WIDTH, s), BF16)
    key_tiles_t = jax.ShapeDtypeStruct((b, s // sub, DIFF_WIDTH, sub), BF16)
    return pl.pallas_call(
        _inproj_kernel,
        grid=(b, s // ts),
        in_specs=[pl.BlockSpec((1, 1, ts), lambda i, j: (i, 0, j)), tile(d), full(g), full(w_main),
                  full(w_main_t), full(w_fz_t), full(b_fz_t)],
        out_specs=[tile_t, tile(DIFF_WIDTH), key_tile_t, tile_t, tile(FOX_WIDTH), key_tile_t,
                   tile(FOX_WIDTH), pl.BlockSpec((1, N_FOX_HEADS, ts), lambda i, j: (i, 0, j))],
        out_shape=[rows_t, rows, key_tiles_t, rows_t, rows, key_tiles_t,
                   jax.ShapeDtypeStruct((b, s, FOX_WIDTH), F32),
                   jax.ShapeDtypeStruct((b, N_FOX_HEADS, s), F32)],
        scratch_shapes=[pltpu.VMEM((w_fz_t.shape[0], 1), F32)],
        compiler_params=_params("arbitrary", "arbitrary"),
        name="inproj",
    )(positions.reshape(b, 1, s), x, g, w_main, w_main_t, w_fz_t, b_fz_t)


ONES_ROWS = 16


def _block_diag_queries(qt):
    row = lax.broadcasted_iota(I32, qt.shape, 0)
    zero = jnp.zeros_like(qt)
    return jnp.concatenate([jnp.where(row < HEAD_DIM, qt, zero),
                            jnp.where(row >= HEAD_DIM, qt, zero)], axis=1)


class _Chain:
    def __init__(self, qbd, keys, values_t, scratch, key_bias=None, query_bias=None):
        self.qbd, self.keys, self.values_t = qbd, keys, values_t
        self.acc_ref, self.s_ref, self.p_ref, self.stat_ref = scratch
        self.key_bias, self.query_bias = key_bias, query_bias


_RUNNING_MAX, _TILE_MAX, _RESCALE = 0, 1, 2


def _causal_attention(chains, qi):
    tk, tq = chains[0].s_ref.shape[0], chains[0].s_ref.shape[1] // 2
    ones = jnp.ones((ONES_ROWS, tk), BF16)

    def scores(ch, i, diagonal):
        s = jnp.dot(ch.keys(i), ch.qbd, preferred_element_type=F32)
        if ch.key_bias is not None:
            s = s - ch.key_bias(i)
        if diagonal is not False:
            r = lax.broadcasted_iota(I32, s.shape, 0)
            c = lax.broadcasted_iota(I32, s.shape, 1) & (tq - 1)
            visible = r <= c if diagonal is True else jnp.logical_or(r <= c, jnp.logical_not(diagonal))
            s = jnp.where(visible, s, -jnp.inf)
        ch.s_ref[...] = s
        top = jnp.max(s, axis=0, keepdims=True)
        if ch.query_bias is not None:
            top = top + ch.query_bias
        ch.stat_ref[_TILE_MAX:_TILE_MAX + 1, :] = top

    def probs(ch):
        m = ch.stat_ref[_RUNNING_MAX:_RUNNING_MAX + 1, :]
        m_new = jnp.maximum(m, ch.stat_ref[_TILE_MAX:_TILE_MAX + 1, :])
        ch.stat_ref[_RUNNING_MAX:_RUNNING_MAX + 1, :] = m_new
        ch.stat_ref[_RESCALE:_RESCALE + 1, :] = jnp.exp2(m - m_new)
        ref = m_new if ch.query_bias is None else m_new - ch.query_bias
        ch.p_ref[...] = jnp.exp2(ch.s_ref[...] - ref).astype(BF16)

    def accumulate(ch, i):
        vts = jnp.concatenate([ch.values_t(i), ones], axis=0)
        ch.acc_ref[...] = (ch.stat_ref[_RESCALE:_RESCALE + 1, :] * ch.acc_ref[...]
                           + jnp.dot(vts, ch.p_ref[...], preferred_element_type=F32))

    def body(k, diagonal_next):
        for ch in chains:
            accumulate(ch, jnp.maximum(k - 1, 0))
        for ch in chains:
            probs(ch)
        for ch in chains:
            scores(ch, k + 1, diagonal_next)

    for ch in chains:
        ch.acc_ref[...] = jnp.zeros_like(ch.acc_ref)
        ch.p_ref[...] = jnp.zeros_like(ch.p_ref)
        ch.stat_ref[...] = jnp.concatenate(
            [jnp.full((1, 2 * tq), -jnp.inf, F32), jnp.ones((ch.stat_ref.shape[0] - 1, 2 * tq), F32)], axis=0)
        scores(ch, 0, qi == 0)

    def loop_body(k, carry):
        body(k, False)
        return carry

    lax.fori_loop(0, qi - 1, loop_body, 0)

    @pl.when(qi >= 1)
    def _():
        body(qi - 1, True)

    for ch in chains:
        accumulate(ch, jnp.maximum(qi - 1, 0))
    for ch in chains:
        probs(ch)
    for ch in chains:
        accumulate(ch, qi)


def _attention_kernel(lq1_ref, lk1_ref, lq2_ref, lk2_ref, g_ref,
                      dqt_ref, dk_ref, dvt_ref, fqt_ref, fk_ref, fvt_ref, cq_ref, ck_ref,
                      do_ref, fo_ref, *scratch, lambda_init):
    tq = dqt_ref.shape[2]
    tk = dvt_ref.shape[3]
    units = dqt_ref.shape[1] // LANES
    lane = lax.broadcasted_iota(I32, (1, LANES), 1)
    scratch = [scratch[i:i + 4] for i in range(0, len(scratch), 4)]

    def unit_chains(u):
        lanes = slice(u * LANES, (u + 1) * LANES)

        def tile(ref):
            return lambda i: ref[0, pl.ds(pl.multiple_of(i * tk, tk), tk), lanes]

        def tile_t(ref):
            return lambda i: ref[0, i, lanes, :]

        cq = jnp.concatenate([cq_ref[0, u, 0:1, :], cq_ref[0, u, 1:2, :]], axis=1)

        def key_bias(i):
            ck = tile(ck_ref)(i)
            other = pltpu.roll(ck, HEAD_DIM, 1)
            first = jnp.where(lane < HEAD_DIM, ck, other)
            second = jnp.where(lane >= HEAD_DIM, ck, other)
            return jnp.concatenate([first] * (tq // LANES) + [second] * (tq // LANES), axis=1)

        return [_Chain(_block_diag_queries(dqt_ref[0, lanes, :]), tile(dk_ref), tile_t(dvt_ref),
                       scratch[2 * u]),
                _Chain(_block_diag_queries(fqt_ref[0, lanes, :]), tile(fk_ref), tile_t(fvt_ref),
                       scratch[2 * u + 1], key_bias, cq)]

    chains = [ch for u in range(units) for ch in unit_chains(u)]
    _causal_attention(chains, pl.program_id(2))

    lam = (jnp.exp(jnp.sum(lq1_ref[...] * lk1_ref[...], axis=-1, keepdims=True))
           - jnp.exp(jnp.sum(lq2_ref[...] * lk2_ref[...], axis=-1, keepdims=True)) + lambda_init)
    dv, h = DIFF_V_DIM, HEAD_DIM
    for u in range(units):
        lanes = slice(u * LANES, (u + 1) * LANES)
        acc = chains[2 * u].acc_ref
        o = (acc[0:dv, 0:tq] / acc[dv:dv + 1, 0:tq]
             - lam * (acc[0:dv, tq:] / acc[dv:dv + 1, tq:]))
        ms = jnp.mean(o * o, axis=0, keepdims=True)
        do_ref[0, :, lanes] = (o * lax.rsqrt(ms + NORM_EPS) * g_ref[...]
                               * (1.0 - lambda_init)).astype(BF16).T
        acc = chains[2 * u + 1].acc_ref
        fo_ref[0, :, lanes] = jnp.concatenate([acc[0:h, 0:tq] / acc[2 * h:2 * h + 1, 0:tq],
                                               acc[h:2 * h, tq:] / acc[2 * h:2 * h + 1, tq:]],
                                              axis=0).astype(BF16).T


def _attention(dqt, dk, dvt, fqt, fk, fvt, c_rows, c_cols, lq1, lk1, lq2, lk2, g_col, lambda_init):
    b, s, _ = dk.shape
    t = ATTN_TILE
    assert N_DIFF_HEADS == N_FOX_HEADS // 2 and N_DIFF_HEADS % ATTN_UNITS == 0
    width = ATTN_UNITS * LANES
    vec = lambda arr: pl.BlockSpec(arr.shape, lambda i, h, j: (0, 0))
    q_spec = pl.BlockSpec((1, width, t), lambda i, h, j: (i, h, j))
    k_spec = pl.BlockSpec((1, s, width), lambda i, h, j: (i, 0, h))
    v_spec = pl.BlockSpec((1, s // t, width, t), lambda i, h, j: (i, 0, h, 0))
    o_spec = pl.BlockSpec((1, t, width), lambda i, h, j: (i, j, h))
    out = jax.ShapeDtypeStruct((b, s, DIFF_WIDTH), BF16)
    chain_scratch = [pltpu.VMEM((LANES + ONES_ROWS, 2 * t), F32),
                     pltpu.VMEM((t, 2 * t), F32),
                     pltpu.VMEM((t, 2 * t), BF16),
                     pltpu.VMEM((8, 2 * t), F32)]
    return pl.pallas_call(
        functools.partial(_attention_kernel, lambda_init=lambda_init),
        grid=(b, N_DIFF_HEADS // ATTN_UNITS, s // t),
        in_specs=[vec(lq1), vec(lk1), vec(lq2), vec(lk2), vec(g_col),
                  q_spec, k_spec, v_spec, q_spec, k_spec, v_spec,
                  pl.BlockSpec((1, ATTN_UNITS, 2, t), lambda i, h, j: (i, h, 0, j)), k_spec],
        out_specs=[o_spec, o_spec],
        out_shape=[out, out],
        scratch_shapes=chain_scratch * (2 * ATTN_UNITS),
        compiler_params=_params("arbitrary", "arbitrary", "arbitrary"),
        name="attention",
    )(lq1, lk1, lq2, lk2, g_col, dqt, dk, dvt, fqt, fk, fvt, c_rows, c_cols)


def _router_kernel(d_ref, f_ref, x_ref, wo_ref, g_ref, wrt_ref, brt_ref,
                   h_ref, xn_ref, gate_ref, idx_ref, rank_ref, count_ref, carry_ref):
    tm = ROUTER_SUBTILE
    nt = (((1,), (1,)), ((), ()))

    def pieces(v, count):
        out = []
        for _ in range(count):
            part = v.astype(BF16)
            out.append(part)
            v = v - part.astype(F32)
        return out

    wh = wrt_ref[...].astype(BF16)
    row = lax.broadcasted_iota(I32, (tm, tm), 0)
    col = lax.broadcasted_iota(I32, (tm, tm), 1)
    earlier = jnp.where(row < col, 1.0, 0.0).astype(BF16)
    eye = jnp.where(row == col, 1.0, 0.0).astype(BF16)

    @pl.when(pl.program_id(0) == 0)
    def _():
        carry_ref[...] = jnp.zeros_like(carry_ref)

    subs = [slice(i * tm, (i + 1) * tm) for i in range(x_ref.shape[0] // tm)]

    def logits(rows):
        mixed = (jnp.dot(d_ref[rows, :], wo_ref[0:DIFF_WIDTH, :], preferred_element_type=F32)
                 + jnp.dot(f_ref[rows, :], wo_ref[DIFF_WIDTH:, :], preferred_element_type=F32))
        h = x_ref[rows, :] + mixed
        h_ref[rows, :] = h
        xn = _rms(h, g_ref[...])
        xn_ref[rows, :] = _pack_rows(xn)
        return lax.dot_general(wh, xn.astype(BF16), nt, preferred_element_type=F32) + brt_ref[...]

    def top_k(work):
        expert = lax.broadcasted_iota(I32, work.shape, 0)
        vals, ids, picks = [], [], []
        for _ in range(TOP_K):
            m = jnp.max(work, axis=0, keepdims=True)
            idx = jnp.min(jnp.where(work == m, expert, N_EXPERTS), axis=0, keepdims=True)
            pick = expert == idx
            vals.append(m)
            ids.append(idx)
            picks.append(pick)
            work = jnp.where(pick, -jnp.inf, work)
        exps = [jnp.exp(v - vals[0]) for v in vals]
        denom = exps[0] + exps[1] + exps[2] + exps[3]
        return ids, picks, [e / denom for e in exps]

    def ranks(picks):
        chosen = jnp.where(picks[0] | picks[1] | picks[2] | picks[3], 1.0, 0.0)
        rank = jnp.dot(chosen.astype(BF16), earlier, preferred_element_type=F32) + carry_ref[...]
        carry_ref[...] = carry_ref[...] + jnp.sum(chosen, axis=1, keepdims=True)
        return rank

    def emit(rows, ids, picks, gates, rank):
        slot = lax.broadcasted_iota(I32, (8, tm), 0)
        gate_t = jnp.zeros((8, tm), F32)
        idx_t = jnp.zeros((8, tm), I32)
        rank_t = jnp.zeros((8, tm), F32)
        for k in range(TOP_K):
            gate_t = jnp.where(slot == k, gates[k], gate_t)
            idx_t = jnp.where(slot == k, ids[k], idx_t)
            rank_t = jnp.where(slot == k, jnp.sum(jnp.where(picks[k], rank, 0.0), axis=0, keepdims=True),
                               rank_t)
        idx_ref[:, rows] = idx_t
        rank_ref[:, rows] = rank_t.astype(I32)
        gate_rows = jnp.concatenate([gate_t, jnp.zeros((LANES - 8, tm), F32)], axis=0)
        gate_ref[rows, :] = sum(lax.dot_general(eye, part, nt, preferred_element_type=F32)
                                for part in pieces(gate_rows, 3))

    work = [logits(rows) for rows in subs]
    chosen = [top_k(w) for w in work]
    rank = [ranks(picks) for _, picks, _ in chosen]
    for rows, (ids, picks, gates), r in zip(subs, chosen, rank):
        emit(rows, ids, picks, gates, r)
    count_ref[...] = jnp.broadcast_to(carry_ref[...], count_ref.shape)


def _router(d_out, f_out, x, w_out, g_moe, w_router_t, b_router_t):
    n, d = x.shape
    tm = ROUTER_TILE
    tile = lambda width: pl.BlockSpec((tm, width), lambda i: (i, 0))
    rows = pl.BlockSpec((8, tm), lambda i: (0, i))
    full = lambda arr: pl.BlockSpec(arr.shape, lambda i: (0,) * arr.ndim)
    return pl.pallas_call(
        _router_kernel,
        grid=(n // tm,),
        in_specs=[tile(DIFF_WIDTH), tile(FOX_WIDTH), tile(d), full(w_out), full(g_moe),
                  full(w_router_t), full(b_router_t)],
        out_specs=[tile(d), tile(d // 2), tile(LANES), rows, rows,
                   pl.BlockSpec((N_EXPERTS, LANES), lambda i: (0, 0))],
        out_shape=[jax.ShapeDtypeStruct((n, d), F32), jax.ShapeDtypeStruct((n, d // 2), U32),
                   jax.ShapeDtypeStruct((n, LANES), F32), jax.ShapeDtypeStruct((8, n), I32),
                   jax.ShapeDtypeStruct((8, n), I32), jax.ShapeDtypeStruct((N_EXPERTS, LANES), F32)],
        scratch_shapes=[pltpu.VMEM((N_EXPERTS, 1), F32)],
        compiler_params=_params("arbitrary"),
        name="router",
    )(d_out, f_out, x, w_out, g_moe, w_router_t, b_router_t)


def _slot_kernel(starts_ref, idx_ref, rank_ref, pos_ref):
    idx = idx_ref[...]
    pos = rank_ref[...]
    for e in range(N_EXPERTS):
        pos = jnp.where(idx == e, pos + starts_ref[e], pos)
    pos_ref[...] = pos


def _slots(starts, idx, rank):
    rows, n = idx.shape
    width = min(n, 2048)
    block = pl.BlockSpec((rows, width), lambda i, starts: (0, i))
    return pl.pallas_call(
        _slot_kernel,
        grid_spec=pltpu.PrefetchScalarGridSpec(num_scalar_prefetch=1, grid=(n // width,),
                                               in_specs=[block, block], out_specs=block),
        out_shape=jax.ShapeDtypeStruct((rows, n), I32),
        compiler_params=_params("arbitrary"),
        name="slots",
    )(starts, idx, rank)


SC_WINDOW = 128
SC_COLUMNS = 256


def _dispatch_sparsecore(pos, pad_rows, xn, n_rows):
    n, d = xn.shape
    mesh = plsc.VectorSubcoreMesh(core_axis_name="core", subcore_axis_name="subcore")
    spread = dict(core_axis_name=("core", "subcore"), dimension_semantics=(pltpu.PARALLEL,))

    @functools.partial(pl.kernel, out_type=jax.ShapeDtypeStruct((n_rows, d), xn.dtype), mesh=mesh,
                       scratch_types=[], name="dispatch_sc")
    def scatter(x_hbm, pos_hbm, zero_hbm, pad_hbm, xs_hbm):
        for c in range(d // SC_COLUMNS):
            columns = xs_hbm.at[:, pl.ds(c * SC_COLUMNS, SC_COLUMNS)]

            def body(x_vmem, pos_vmem, columns=columns):
                for k in range(TOP_K):
                    pltpu.sync_copy(x_vmem, columns.at[pos_vmem.at[k]])

            pltpu.emit_pipeline(
                body, grid=(n // SC_WINDOW,),
                in_specs=[pl.BlockSpec((SC_WINDOW, SC_COLUMNS), index_map=lambda i, c=c: (i, c)),
                          pl.BlockSpec((TOP_K, SC_WINDOW), index_map=lambda i: (0, i))],
                out_specs=[], **spread)(x_hbm, pos_hbm)

            def pad_body(zero_vmem, pad_vmem, columns=columns):
                pltpu.sync_copy(zero_vmem, columns.at[pad_vmem.at[0]])

            pltpu.emit_pipeline(
                pad_body, grid=(pad_rows.size // SC_WINDOW,),
                in_specs=[pl.BlockSpec((SC_WINDOW, SC_COLUMNS), index_map=lambda i: (0, 0)),
                          pl.BlockSpec((1, SC_WINDOW), index_map=lambda i: (0, i))],
                out_specs=[], **spread)(zero_hbm, pad_hbm)

    return scatter(xn, pos, jnp.zeros((SC_WINDOW, SC_COLUMNS), xn.dtype), pad_rows.reshape(1, -1))


def _expert_kernel(tile_expert_ref, n_used_ref, next_expert_ref, slot_ref,
                   x_ref, w1_ref, b1_ref, w2_ref, b2_ref, y_ref,
                   w1f_ref, w2f_ref, sem_ref, w1b_ref, w2p_ref, w2b_ref):
    t = pl.program_id(0)
    fresh = (t == 0) | (tile_expert_ref[t] != tile_expert_ref[jnp.maximum(t - 1, 0)])
    used = t < n_used_ref[0]

    def weight_copies(e, s):
        return (pltpu.make_async_copy(w1_ref.at[e], w1f_ref.at[s], sem_ref.at[0, s]),
                pltpu.make_async_copy(w2_ref.at[e], w2f_ref.at[s], sem_ref.at[1, s]))

    @pl.when(t == 0)
    def _():
        for copy in weight_copies(tile_expert_ref[0], slot_ref[0]):
            copy.start()

    @pl.when(fresh & used)
    def _():
        s = slot_ref[t]
        for copy in weight_copies(tile_expert_ref[t], s):
            copy.wait()

        @pl.when(next_expert_ref[t] >= 0)
        def _():
            for copy in weight_copies(next_expert_ref[t], 1 - s):
                copy.start()

        rows = LANES
        half = LANES // 2
        for r in range(0, w1b_ref.shape[0], rows):
            w1b_ref[r:r + rows, :] = w1f_ref[s, r:r + rows, :].astype(BF16)
        for c in range(w2p_ref.shape[0]):
            lanes = slice(c * LANES, (c + 1) * LANES)
            for r in range(0, w2b_ref.shape[0], rows):
                for b in range(2):
                    w2p_ref[c, pl.ds(r + b, half, stride=2), :] = (
                        w2f_ref[s, r + b * half:r + (b + 1) * half, lanes])
            w2b_ref[:, lanes] = w2p_ref[c].astype(BF16)

    @pl.when(used)
    def _():
        xb = _unpack_rows(x_ref[...]).astype(BF16)
        h = jnp.dot(xb, w1b_ref[...], preferred_element_type=F32) + b1_ref[0]
        even = lax.broadcasted_iota(I32, (1, LANES), 1) % 2 == 0
        parts = []
        for q in range(h.shape[1] // (2 * LANES)):
            a = h[:, (2 * q) * LANES:(2 * q + 1) * LANES]
            b = h[:, (2 * q + 1) * LANES:(2 * q + 2) * LANES]
            hg = jnp.where(even, a, pltpu.roll(b, 1, 1))
            hl = jnp.where(even, pltpu.roll(a, LANES - 1, 1), b)
            glu = jnp.minimum(hg, SWIGLU_LIMIT)
            lin = jnp.clip(hl, -SWIGLU_LIMIT, SWIGLU_LIMIT)
            parts.append((glu * jax.nn.sigmoid(SWIGLU_ALPHA * glu) * (lin + 1.0)).astype(BF16))
        act = jnp.concatenate(parts, axis=1)
        y_ref[...] = _pack_rows(jnp.dot(act, w2b_ref[...], preferred_element_type=F32) + b2_ref[0])

    @pl.when(jnp.logical_not(used))
    def _():
        y_ref[...] = jnp.zeros_like(y_ref)


def _experts(tile_expert, n_used, next_expert, slot, xs, w1, b1, w2, b2):
    n_rows, d = xs.shape
    tm = EXPERT_TILE
    by_expert = lambda arr: pl.BlockSpec((1,) + arr.shape[1:], lambda i, te, *_: (te[i], 0, 0))
    rows = pl.BlockSpec((tm, d), lambda i, *_: (i, 0))
    used_rows = pl.BlockSpec((tm, d), lambda i, te, nu, *_: (jnp.minimum(i, nu[0] - 1), 0))
    hbm = pl.BlockSpec(memory_space=pl.ANY)
    return pl.pallas_call(
        _expert_kernel,
        grid_spec=pltpu.PrefetchScalarGridSpec(
            num_scalar_prefetch=4,
            grid=(n_rows // tm,),
            in_specs=[used_rows, hbm, by_expert(b1), hbm, by_expert(b2)],
            out_specs=rows,
            scratch_shapes=[pltpu.VMEM((2,) + w1.shape[1:], F32), pltpu.VMEM((2,) + w2.shape[1:], F32),
                            pltpu.SemaphoreType.DMA((2, 2)),
                            pltpu.VMEM(w1.shape[1:], BF16),
                            pltpu.VMEM((w2.shape[2] // LANES, w2.shape[1], LANES), F32),
                            pltpu.VMEM(w2.shape[1:], BF16)]),
        out_shape=jax.ShapeDtypeStruct(xs.shape, xs.dtype),
        compiler_params=pltpu.CompilerParams(dimension_semantics=("arbitrary",),
                                             vmem_limit_bytes=EXPERT_VMEM_LIMIT),
        name="experts",
    )(tile_expert, n_used, next_expert, slot, xs, w1, b1, w2, b2)


def _gather_sparsecore(pos, y):
    d = y.shape[1]
    total = pos.size
    mesh = plsc.VectorSubcoreMesh(core_axis_name="core", subcore_axis_name="subcore")

    @functools.partial(pl.kernel, out_type=jax.ShapeDtypeStruct((total, d), y.dtype), mesh=mesh,
                       scratch_types=[], name="gather_sc")
    def gather(y_hbm, pos_hbm, out_hbm):
        for c in range(d // SC_COLUMNS):
            def body(pos_vmem, out_vmem, c=c):
                pltpu.sync_copy(y_hbm.at[:, pl.ds(c * SC_COLUMNS, SC_COLUMNS)].at[pos_vmem.at[0]], out_vmem)

            pltpu.emit_pipeline(
                body, grid=(total // SC_WINDOW,),
                in_specs=[pl.BlockSpec((1, SC_WINDOW), index_map=lambda i: (0, i))],
                out_specs=[pl.BlockSpec((SC_WINDOW, SC_COLUMNS), index_map=lambda i, c=c: (i, c))],
                core_axis_name=("core", "subcore"),
                dimension_semantics=(pltpu.PARALLEL,))(pos_hbm, out_hbm)

    return gather(y, pos.reshape(1, -1))


def _combine_kernel(h_ref, gate_ref, p_ref, wg_ref, wp_ref, gple_ref, gfin_ref, *refs, final_norm):
    y_refs, o_ref = refs[:TOP_K], refs[TOP_K]
    ts = COMBINE_SUBTILE
    subs = [slice(i * ts, (i + 1) * ts) for i in range(h_ref.shape[0] // ts)]

    def mix(rows):
        gates = gate_ref[rows, :]
        moe = gates[:, 0:1] * _unpack_rows(y_refs[0][rows, :])
        for k in range(1, TOP_K):
            moe = moe + gates[:, k:k + 1] * _unpack_rows(y_refs[k][rows, :])
        h = h_ref[rows, :] + moe
        return h, jnp.dot(_rms(h, gple_ref[...]).astype(BF16), wg_ref[...], preferred_element_type=F32)

    embs = [jnp.dot(p_ref[rows, :].astype(BF16), wp_ref[...], preferred_element_type=F32) for rows in subs]
    mixed = [mix(rows) for rows in subs]
    for rows, (h, logit), emb in zip(subs, mixed, embs):
        h = h + jax.nn.sigmoid(logit) * emb
        o_ref[rows, :] = _rms(h, gfin_ref[...]) if final_norm else h


def _combine(h, gates, p, w_gate, w_proj, g_ple, g_final, y_tok, final_norm):
    n, d = h.shape
    tm = TOKEN_TILE
    tile = lambda width: pl.BlockSpec((tm, width), lambda i: (i, 0))
    full = lambda arr: pl.BlockSpec(arr.shape, lambda i: (0,) * arr.ndim)
    choice = lambda k: pl.BlockSpec((tm, y_tok.shape[1]), lambda i: (k * (n // tm) + i, 0))
    return pl.pallas_call(
        functools.partial(_combine_kernel, final_norm=final_norm),
        grid=(n // tm,),
        in_specs=[tile(d), tile(LANES), tile(p.shape[1]), full(w_gate), full(w_proj),
                  full(g_ple), full(g_final)] + [choice(k) for k in range(TOP_K)],
        out_specs=tile(d),
        out_shape=jax.ShapeDtypeStruct((n, d), F32),
        compiler_params=_params("arbitrary"),
        name="combine",
    )(h, gates, p, w_gate, w_proj, g_ple, g_final, *([y_tok] * TOP_K))


def _layer(h, p, positions, g_attn, w_in, lq1, lk1, lq2, lk2, g_subln, b_forget, w_out, g_moe,
           w_router, b_router, w_e1, b_e1, w_e2, b_e2, g_ple, w_ple_gate, w_ple_proj, g_final,
           lambda_init, final_norm):
    b, s, d = h.shape
    n = b * s
    row = lambda v: v.reshape(1, -1)

    group = lambda j: w_in[:, j * DIFF_WIDTH:(j + 1) * DIFF_WIDTH]
    w_main = jnp.concatenate([group(0), group(1), group(4)], axis=1).astype(BF16)
    w_main_t = jnp.concatenate([group(2), group(3), group(5)], axis=1).T.astype(BF16)
    pad_heads = ((0, BF16_TILE_ROWS - N_FOX_HEADS), (0, 0))
    w_fz_t = jnp.pad(w_in[:, MAIN_WIDTH:].T, pad_heads).astype(BF16)
    b_fz_t = jnp.pad(b_forget.reshape(-1, 1), pad_heads)
    b1 = b_e1[:, None, :]
    b2 = b_e2[:, None, :]

    dq_t, dk, dv_t, fq_t, fk, fv_t, c, c_t = _inproj(positions, h, row(g_attn), w_main, w_main_t, w_fz_t, b_fz_t)
    c_rows = c_t.reshape(b, N_FOX_HEADS // 2, 2, s)
    d_out, f_out = _attention(dq_t, dk, dv_t, fq_t, fk, fv_t, c_rows, c, row(lq1), row(lk1), row(lq2),
                              row(lk2), g_subln.reshape(-1, 1), lambda_init)

    h1, xn, gates, idx, rank, counts = _router(
        d_out.reshape(n, DIFF_WIDTH), f_out.reshape(n, FOX_WIDTH), h.reshape(n, d),
        w_out.astype(BF16), row(g_moe), w_router.T, b_router.reshape(-1, 1))

    te = EXPERT_TILE
    n_rows = n * TOP_K + N_EXPERTS * te
    counts = counts[:, 0].astype(I32)
    padded = (counts + te - 1) // te * te
    ends = jnp.cumsum(padded)
    starts = ends - padded
    pos = _slots(starts.astype(I32), idx, rank)[:TOP_K]
    tile_start = jnp.arange(n_rows // te, dtype=I32) * te
    tile_expert = jnp.minimum(jnp.sum((ends[None, :] <= tile_start[:, None]).astype(I32), axis=1),
                              N_EXPERTS - 1)
    n_used = (ends[-1:] // te).astype(I32)
    owns = padded > 0
    later = lax.cummin(jnp.where(owns, jnp.arange(N_EXPERTS, dtype=I32), N_EXPERTS), reverse=True)
    following = jnp.concatenate([later[1:], jnp.full((1,), N_EXPERTS, I32)])
    following = jnp.where(following < N_EXPERTS, following, -1)
    owner = tile_expert[:, None] == jnp.arange(N_EXPERTS, dtype=I32)[None, :]
    next_expert = jnp.sum(jnp.where(owner, following[None, :], 0), axis=1)
    slot = jnp.sum(jnp.where(owner, ((jnp.cumsum(owns.astype(I32)) - 1) % 2)[None, :], 0), axis=1)

    pad_count = padded - counts
    pad_total = jnp.sum(pad_count)
    pad_before = jnp.cumsum(pad_count) - pad_count
    entry = jnp.arange(N_EXPERTS * te, dtype=I32)
    in_or_before = entry[:, None] >= pad_before[None, :]
    expert_of = jnp.sum(in_or_before.astype(I32), axis=1) - 1
    mine = expert_of[:, None] == jnp.arange(N_EXPERTS, dtype=I32)[None, :]
    pad_rows = jnp.sum(jnp.where(mine, (starts + counts - pad_before)[None, :], 0), axis=1) + entry
    pad_rows = jnp.where(entry < pad_total, pad_rows, ends[-1] + entry - pad_total)
    xs = _dispatch_sparsecore(pos.astype(I32), pad_rows.astype(I32), xn, n_rows)
    y = _experts(tile_expert, n_used, next_expert.astype(I32), slot.astype(I32), xs, w_e1, b1, w_e2, b2)
    y_tok = _gather_sparsecore(pos.astype(I32), y)
    out = _combine(h1, gates, p.reshape(n, PLE_DIM), w_ple_gate.astype(BF16),
                   w_ple_proj.astype(BF16), row(g_ple), row(g_final), y_tok, final_norm)
    return out.reshape(b, s, d)


def kernel(x, p, positions, g_attn, w_in, lambda_q1, lambda_k1, lambda_q2, lambda_k2, g_subln, b_forget, w_out, g_moe, w_router, b_router, w_e1, b_e1, w_e2, b_e2, g_ple, w_ple_gate, w_ple_proj, g_final):
    depth = g_attn.shape[0]
    h = x
    for i in range(depth):
        lambda_init = 0.8 - 0.6 * math.exp(-0.3 * i)
        h = _layer(h, p[i], positions, g_attn[i], w_in[i], lambda_q1[i], lambda_k1[i],
                   lambda_q2[i], lambda_k2[i], g_subln[i], b_forget[i], w_out[i], g_moe[i],
                   w_router[i], b_router[i], w_e1[i], b_e1[i], w_e2[i], b_e2[i], g_ple[i],
                   w_ple_gate[i], w_ple_proj[i], g_final, lambda_init, i == depth - 1)
    return h
```

```python
import functools
import math

import jax
import jax.numpy as jnp
from jax import lax
from jax.experimental import pallas as pl
from jax.experimental.pallas import tpu as pltpu
from jax.experimental.pallas import tpu_sc as plsc

F32 = jnp.float32
BF16 = jnp.bfloat16
I32 = jnp.int32
U32 = jnp.uint32

LANES = 128
BF16_TILE_ROWS = 16
D_MODEL = 1024
HEAD_DIM = 64
N_DIFF_HEADS = 4
DIFF_V_DIM = 2 * HEAD_DIM
N_FOX_HEADS = 8
DIFF_WIDTH = N_DIFF_HEADS * DIFF_V_DIM
FOX_WIDTH = N_FOX_HEADS * HEAD_DIM
ROT_DIM = HEAD_DIM // 4
ROPE_THETA = 500000.0
N_EXPERTS = 32
TOP_K = 4
D_EXPERT = D_MODEL
SWIGLU_ALPHA = 1.702
SWIGLU_LIMIT = 7.0
PLE_DIM = 256
NORM_EPS = 1e-5
LOG2_E = math.log2(math.e)
MAIN_WIDTH = 3 * DIFF_WIDTH + 3 * FOX_WIDTH

SEQ_TILE = 1024
SEQ_SUBTILE = 256
ATTN_TILE = 256
ATTN_UNITS = 2
TOKEN_TILE = 1024
COMBINE_SUBTILE = 256
ROUTER_TILE = 1024
ROUTER_SUBTILE = 256
EXPERT_TILE = 256
VMEM_LIMIT = 48 * 1024 * 1024
EXPERT_VMEM_LIMIT = 58 * 1024 * 1024


def _params(*semantics):
    return pltpu.CompilerParams(dimension_semantics=semantics, vmem_limit_bytes=VMEM_LIMIT)


def _rms(x, g):
    return x * lax.rsqrt(jnp.mean(x * x, axis=-1, keepdims=True) + NORM_EPS) * g


def _pack_rows(v):
    half = v.shape[1] // 2
    bits = pltpu.bitcast(v.astype(BF16).astype(F32), U32)
    return (bits[:, :half] >> 16) | bits[:, half:]


def _unpack_rows(words):
    low = pltpu.bitcast(words << 16, F32)
    high = pltpu.bitcast(words & jnp.uint32(0xFFFF0000), F32)
    return jnp.concatenate([low, high], axis=1)


def _inproj_kernel(pos_ref, x_ref, g_ref, w_ref, wt_ref, wfzt_ref, bfzt_ref,
                   dqt_ref, dk_ref, dvt_ref, fqt_ref, fk_ref, fvt_ref, c_ref, ct_ref, carry_t_ref):
    @pl.when(pl.program_id(1) == 0)
    def _():
        carry_t_ref[...] = jnp.zeros_like(carry_t_ref)

    ts = SEQ_SUBTILE
    subs = [slice(i * ts, (i + 1) * ts) for i in range(x_ref.shape[1] // ts)]
    xbs = [_rms(x_ref[0, rows, :], g_ref[...]).astype(BF16) for rows in subs]
    nt = (((1,), (1,)), ((), ()))
    tn = (((0,), (0,)), ((), ()))

    def pieces(v):
        hi = v.astype(BF16)
        r1 = v - hi.astype(F32)
        mid = r1.astype(BF16)
        return hi, mid, (r1 - mid.astype(F32)).astype(BF16)

    def rows_to_lanes(v, select):
        return sum(lax.dot_general(part, select, tn, preferred_element_type=F32) for part in pieces(v))

    half_rot = ROT_DIM // 2
    freq = lax.broadcasted_iota(I32, (half_rot, 1), 0)
    inv_freq = jnp.power(ROPE_THETA, -(2 * freq).astype(F32) / ROT_DIM)
    r = lax.broadcasted_iota(I32, (2 * half_rot, 2 * LANES), 0)
    lane2 = lax.broadcasted_iota(I32, (2 * half_rot, 2 * LANES), 1)
    d = lane2 % HEAD_DIM
    same_freq = jnp.where(d % half_rot == r % half_rot, 1.0, 0.0)
    cos_part = jnp.where((r < half_rot) & (lane2 < LANES) & (d < ROT_DIM), same_freq, 0.0)
    sin_part = jnp.where((r >= half_rot) & (lane2 >= LANES) & (d < ROT_DIM),
                         jnp.where(d < half_rot, -same_freq, same_freq), 0.0)
    expand = (cos_part + sin_part).astype(BF16)
    lane = lax.broadcasted_iota(I32, (1, LANES), 1) % HEAD_DIM
    first_half = lane < half_rot

    def rope_tables(rows):
        ang = inv_freq * pos_ref[0, :, rows].astype(F32)
        trig = jnp.concatenate([jnp.cos(ang), jnp.sin(ang)], axis=0)
        tables = rows_to_lanes(trig, expand)
        return (jnp.where(lane < ROT_DIM, tables[:, :LANES], 1.0), tables[:, LANES:],
                trig[:half_rot], trig[half_rot:])

    def rope(t, cosf, sinf):
        outs = []
        for j in range(t.shape[1] // LANES):
            tj = t[:, j * LANES:(j + 1) * LANES]
            partner = jnp.where(first_half,
                                pltpu.roll(tj, LANES - ROT_DIM // 2, 1),
                                pltpu.roll(tj, ROT_DIM // 2, 1))
            outs.append(tj * cosf + partner * sinf)
        return jnp.concatenate(outs, axis=1)

    def rope_t(t, cos, sin):
        blocks = []
        for base in range(0, t.shape[0], HEAD_DIM):
            lo = t[base:base + half_rot]
            hi = t[base + half_rot:base + ROT_DIM]
            blocks += [lo * cos - hi * sin, hi * cos + lo * sin, t[base + ROT_DIM:base + HEAD_DIM]]
        return jnp.concatenate(blocks, axis=0)

    tables = [rope_tables(rows) for rows in subs]

    def log_sigmoid(z):
        return jnp.minimum(z, 0.0) - jnp.log1p(jnp.exp(-jnp.abs(z)))

    row = lax.broadcasted_iota(I32, (ts, ts), 0)
    col = lax.broadcasted_iota(I32, (ts, ts), 1)
    upto_col = jnp.where(row <= col, 1.0, 0.0).astype(BF16)
    heads = wfzt_ref.shape[0]
    head = lax.broadcasted_iota(I32, (heads, FOX_WIDTH), 0)
    group = lax.broadcasted_iota(I32, (heads, FOX_WIDTH), 1) // HEAD_DIM
    head_lanes = jnp.where(head == group, 1.0, 0.0).astype(BF16)

    for rows, xb in zip(subs, xbs):
        fzt = lax.dot_general(wfzt_ref[...], xb, nt, preferred_element_type=F32) + bfzt_ref[...]
        ct = carry_t_ref[...] + sum(jnp.dot(part, upto_col, preferred_element_type=F32)
                                    for part in pieces(log_sigmoid(fzt)))
        carry_t_ref[...] = ct[:, ts - 1:ts]
        ct = ct * LOG2_E
        ct_ref[0, :, rows] = ct[0:N_FOX_HEADS, :]
        c_ref[0, rows, :] = rows_to_lanes(ct, head_lanes)

    scale = HEAD_DIM ** -0.5 * LOG2_E
    w = DIFF_WIDTH
    projs = [jnp.dot(xb, w_ref[...], preferred_element_type=F32) for xb in xbs]
    projs_t = [lax.dot_general(wt_ref[...], xb, nt, preferred_element_type=F32) for xb in xbs]
    for i, (rows, proj, proj_t, (cosf, sinf, cos_t, sin_t)) in enumerate(zip(subs, projs, projs_t, tables)):
        dk_ref[0, rows, :] = rope(proj[:, 0:w], cosf, sinf).astype(BF16)
        fk_ref[0, rows, :] = proj[:, w:2 * w].astype(BF16)
        dqt_ref[0, :, rows] = (rope_t(proj_t[0:w, :], cos_t, sin_t) * scale).astype(BF16)
        dvt_ref[0, i] = proj_t[w:2 * w, :].astype(BF16)
        fqt_ref[0, :, rows] = (proj_t[2 * w:3 * w, :] * scale).astype(BF16)
        fvt_ref[0, i] = proj_t[3 * w:4 * w, :].astype(BF16)


def _inproj(positions, x, g, w_main, w_main_t, w_fz_t, b_fz_t):
    b, s, d = x.shape
    ts = SEQ_TILE
    sub = SEQ_SUBTILE
    assert sub == ATTN_TILE
    tile = lambda width: pl.BlockSpec((1, ts, width), lambda i, j: (i, j, 0))
    tile_t = pl.BlockSpec((1, DIFF_WIDTH, ts), lambda i, j: (i, 0, j))
    key_tile_t = pl.BlockSpec((1, ts // sub, DIFF_WIDTH, sub), lambda i, j: (i, j, 0, 0))
    full = lambda arr: pl.BlockSpec(arr.shape, lambda i, j: (0,) * arr.ndim)
    rows = jax.ShapeDtypeStruct((b, s, DIFF_WIDTH), BF16)
    rows_t = jax.ShapeDtypeStruct((b, DIFF_WIDTH, s), BF16)
    key_tiles_t = jax.ShapeDtypeStruct((b, s // sub, DIFF_WIDTH, sub), BF16)
    return pl.pallas_call(
        _inproj_kernel,
        grid=(b, s // ts),
        in_specs=[pl.BlockSpec((1, 1, ts), lambda i, j: (i, 0, j)), tile(d), full(g), full(w_main),
                  full(w_main_t), full(w_fz_t), full(b_fz_t)],
        out_specs=[tile_t, tile(DIFF_WIDTH), key_tile_t, tile_t, tile(FOX_WIDTH), key_tile_t,
                   tile(FOX_WIDTH), pl.BlockSpec((1, N_FOX_HEADS, ts), lambda i, j: (i, 0, j))],
        out_shape=[rows_t, rows, key_tiles_t, rows_t, rows, key_tiles_t,
                   jax.ShapeDtypeStruct((b, s, FOX_WIDTH), F32),
                   jax.ShapeDtypeStruct((b, N_FOX_HEADS, s), F32)],
        scratch_shapes=[pltpu.VMEM((w_fz_t.shape[0], 1), F32)],
        compiler_params=_params("arbitrary", "arbitrary"),
        name="inproj",
    )(positions.reshape(b, 1, s), x, g, w_main, w_main_t, w_fz_t, b_fz_t)


ONES_ROWS = 16


def _block_diag_queries(qt):
    row = lax.broadcasted_iota(I32, qt.shape, 0)
    zero = jnp.zeros_like(qt)
    return jnp.concatenate([jnp.where(row < HEAD_DIM, qt, zero),
                            jnp.where(row >= HEAD_DIM, qt, zero)], axis=1)


class _Chain:
    def __init__(self, qbd, keys, values_t, scratch, key_bias=None, query_bias=None):
        self.qbd, self.keys, self.values_t = qbd, keys, values_t
        self.acc_ref, self.s_ref, self.p_ref, self.stat_ref = scratch
        self.key_bias, self.query_bias = key_bias, query_bias


_RUNNING_MAX, _TILE_MAX, _RESCALE = 0, 1, 2


def _causal_attention(chains, qi):
    tk, tq = chains[0].s_ref.shape[0], chains[0].s_ref.shape[1] // 2
    ones = jnp.ones((ONES_ROWS, tk), BF16)

    def scores(ch, i, diagonal):
        s = jnp.dot(ch.keys(i), ch.qbd, preferred_element_type=F32)
        if ch.key_bias is not None:
            s = s - ch.key_bias(i)
        if diagonal is not False:
            r = lax.broadcasted_iota(I32, s.shape, 0)
            c = lax.broadcasted_iota(I32, s.shape, 1) & (tq - 1)
            visible = r <= c if diagonal is True else jnp.logical_or(r <= c, jnp.logical_not(diagonal))
            s = jnp.where(visible, s, -jnp.inf)
        ch.s_ref[...] = s
        top = jnp.max(s, axis=0, keepdims=True)
        if ch.query_bias is not None:
            top = top + ch.query_bias
        ch.stat_ref[_TILE_MAX:_TILE_MAX + 1, :] = top

    def probs(ch):
        m = ch.stat_ref[_RUNNING_MAX:_RUNNING_MAX + 1, :]
        m_new = jnp.maximum(m, ch.stat_ref[_TILE_MAX:_TILE_MAX + 1, :])
        ch.stat_ref[_RUNNING_MAX:_RUNNING_MAX + 1, :] = m_new
        ch.stat_ref[_RESCALE:_RESCALE + 1, :] = jnp.exp2(m - m_new)
        ref = m_new if ch.query_bias is None else m_new - ch.query_bias
        ch.p_ref[...] = jnp.exp2(ch.s_ref[...] - ref).astype(BF16)

    def accumulate(ch, i):
        vts = jnp.concatenate([ch.values_t(i), ones], axis=0)
        ch.acc_ref[...] = (ch.stat_ref[_RESCALE:_RESCALE + 1, :] * ch.acc_ref[...]
                           + jnp.dot(vts, ch.p_ref[...], preferred_element_type=F32))

    def body(k, diagonal_next):
        for ch in chains:
            accumulate(ch, jnp.maximum(k - 1, 0))
        for ch in chains:
            probs(ch)
        for ch in chains:
            scores(ch, k + 1, diagonal_next)

    for ch in chains:
        ch.acc_ref[...] = jnp.zeros_like(ch.acc_ref)
        ch.p_ref[...] = jnp.zeros_like(ch.p_ref)
        ch.stat_ref[...] = jnp.concatenate(
            [jnp.full((1, 2 * tq), -jnp.inf, F32), jnp.ones((ch.stat_ref.shape[0] - 1, 2 * tq), F32)], axis=0)
        scores(ch, 0, qi == 0)

    def loop_body(k, carry):
        body(k, False)
        return carry

    lax.fori_loop(0, qi - 1, loop_body, 0)

    @pl.when(qi >= 1)
    def _():
        body(qi - 1, True)

    for ch in chains:
        accumulate(ch, jnp.maximum(qi - 1, 0))
    for ch in chains:
        probs(ch)
    for ch in chains:
        accumulate(ch, qi)


def _attention_kernel(lq1_ref, lk1_ref, lq2_ref, lk2_ref, g_ref,
                      dqt_ref, dk_ref, dvt_ref, fqt_ref, fk_ref, fvt_ref, cq_ref, ck_ref,
                      do_ref, fo_ref, *scratch, lambda_init):
    tq = dqt_ref.shape[2]
    tk = dvt_ref.shape[3]
    units = dqt_ref.shape[1] // LANES
    lane = lax.broadcasted_iota(I32, (1, LANES), 1)
    scratch = [scratch[i:i + 4] for i in range(0, len(scratch), 4)]

    def unit_chains(u):
        lanes = slice(u * LANES, (u + 1) * LANES)

        def tile(ref):
            return lambda i: ref[0, pl.ds(pl.multiple_of(i * tk, tk), tk), lanes]

        def tile_t(ref):
            return lambda i: ref[0, i, lanes, :]

        cq = jnp.concatenate([cq_ref[0, u, 0:1, :], cq_ref[0, u, 1:2, :]], axis=1)

        def key_bias(i):
            ck = tile(ck_ref)(i)
            other = pltpu.roll(ck, HEAD_DIM, 1)
            first = jnp.where(lane < HEAD_DIM, ck, other)
            second = jnp.where(lane >= HEAD_DIM, ck, other)
            return jnp.concatenate([first] * (tq // LANES) + [second] * (tq // LANES), axis=1)

        return [_Chain(_block_diag_queries(dqt_ref[0, lanes, :]), tile(dk_ref), tile_t(dvt_ref),
                       scratch[2 * u]),
                _Chain(_block_diag_queries(fqt_ref[0, lanes, :]), tile(fk_ref), tile_t(fvt_ref),
                       scratch[2 * u + 1], key_bias, cq)]

    chains = [ch for u in range(units) for ch in unit_chains(u)]
    _causal_attention(chains, pl.program_id(2))

    lam = (jnp.exp(jnp.sum(lq1_ref[...] * lk1_ref[...], axis=-1, keepdims=True))
           - jnp.exp(jnp.sum(lq2_ref[...] * lk2_ref[...], axis=-1, keepdims=True)) + lambda_init)
    dv, h = DIFF_V_DIM, HEAD_DIM
    for u in range(units):
        lanes = slice(u * LANES, (u + 1) * LANES)
        acc = chains[2 * u].acc_ref
        o = (acc[0:dv, 0:tq] / acc[dv:dv + 1, 0:tq]
             - lam * (acc[0:dv, tq:] / acc[dv:dv + 1, tq:]))
        ms = jnp.mean(o * o, axis=0, keepdims=True)
        do_ref[0, :, lanes] = (o * lax.rsqrt(ms + NORM_EPS) * g_ref[...]
                               * (1.0 - lambda_init)).astype(BF16).T
        acc = chains[2 * u + 1].acc_ref
        fo_ref[0, :, lanes] = jnp.concatenate([acc[0:h, 0:tq] / acc[2 * h:2 * h + 1, 0:tq],
                                               acc[h:2 * h, tq:] / acc[2 * h:2 * h + 1, tq:]],
                                              axis=0).astype(BF16).T


def _attention(dqt, dk, dvt, fqt, fk, fvt, c_rows, c_cols, lq1, lk1, lq2, lk2, g_col, lambda_init):
    b, s, _ = dk.shape
    t = ATTN_TILE
    assert N_DIFF_HEADS == N_FOX_HEADS // 2 and N_DIFF_HEADS % ATTN_UNITS == 0
    width = ATTN_UNITS * LANES
    vec = lambda arr: pl.BlockSpec(arr.shape, lambda i, h, j: (0, 0))
    q_spec = pl.BlockSpec((1, width, t), lambda i, h, j: (i, h, j))
    k_spec = pl.BlockSpec((1, s, width), lambda i, h, j: (i, 0, h))
    v_spec = pl.BlockSpec((1, s // t, width, t), lambda i, h, j: (i, 0, h, 0))
    o_spec = pl.BlockSpec((1, t, width), lambda i, h, j: (i, j, h))
    out = jax.ShapeDtypeStruct((b, s, DIFF_WIDTH), BF16)
    chain_scratch = [pltpu.VMEM((LANES + ONES_ROWS, 2 * t), F32),
                     pltpu.VMEM((t, 2 * t), F32),
                     pltpu.VMEM((t, 2 * t), BF16),
                     pltpu.VMEM((8, 2 * t), F32)]
    return pl.pallas_call(
        functools.partial(_attention_kernel, lambda_init=lambda_init),
        grid=(b, N_DIFF_HEADS // ATTN_UNITS, s // t),
        in_specs=[vec(lq1), vec(lk1), vec(lq2), vec(lk2), vec(g_col),
                  q_spec, k_spec, v_spec, q_spec, k_spec, v_spec,
                  pl.BlockSpec((1, ATTN_UNITS, 2, t), lambda i, h, j: (i, h, 0, j)), k_spec],
        out_specs=[o_spec, o_spec],
        out_shape=[out, out],
        scratch_shapes=chain_scratch * (2 * ATTN_UNITS),
        compiler_params=_params("arbitrary", "arbitrary", "arbitrary"),
        name="attention",
    )(lq1, lk1, lq2, lk2, g_col, dqt, dk, dvt, fqt, fk, fvt, c_rows, c_cols)


def _router_kernel(d_ref, f_ref, x_ref, wo_ref, g_ref, wrt_ref, brt_ref,
                   h_ref, xn_ref, gate_ref, idx_ref, rank_ref, count_ref, carry_ref):
    tm = ROUTER_SUBTILE
    nt = (((1,), (1,)), ((), ()))

    def pieces(v, count):
        out = []
        for _ in range(count):
            part = v.astype(BF16)
            out.append(part)
            v = v - part.astype(F32)
        return out

    wh = wrt_ref[...].astype(BF16)
    row = lax.broadcasted_iota(I32, (tm, tm), 0)
    col = lax.broadcasted_iota(I32, (tm, tm), 1)
    earlier = jnp.where(row < col, 1.0, 0.0).astype(BF16)
    eye = jnp.where(row == col, 1.0, 0.0).astype(BF16)

    @pl.when(pl.program_id(0) == 0)
    def _():
        carry_ref[...] = jnp.zeros_like(carry_ref)

    subs = [slice(i * tm, (i + 1) * tm) for i in range(x_ref.shape[0] // tm)]

    def logits(rows):
        mixed = (jnp.dot(d_ref[rows, :], wo_ref[0:DIFF_WIDTH, :], preferred_element_type=F32)
                 + jnp.dot(f_ref[rows, :], wo_ref[DIFF_WIDTH:, :], preferred_element_type=F32))
        h = x_ref[rows, :] + mixed
        h_ref[rows, :] = h
        xn = _rms(h, g_ref[...])
        xn_ref[rows, :] = _pack_rows(xn)
        return lax.dot_general(wh, xn.astype(BF16), nt, preferred_element_type=F32) + brt_ref[...]

    def top_k(work):
        expert = lax.broadcasted_iota(I32, work.shape, 0)
        vals, ids, picks = [], [], []
        for _ in range(TOP_K):
            m = jnp.max(work, axis=0, keepdims=True)
            idx = jnp.min(jnp.where(work == m, expert, N_EXPERTS), axis=0, keepdims=True)
            pick = expert == idx
            vals.append(m)
            ids.append(idx)
            picks.append(pick)
            work = jnp.where(pick, -jnp.inf, work)
        exps = [jnp.exp(v - vals[0]) for v in vals]
        denom = exps[0] + exps[1] + exps[2] + exps[3]
        return ids, picks, [e / denom for e in exps]

    def ranks(picks):
        chosen = jnp.where(picks[0] | picks[1] | picks[2] | picks[3], 1.0, 0.0)
        rank = jnp.dot(chosen.astype(BF16), earlier, preferred_element_type=F32) + carry_ref[...]
        carry_ref[...] = carry_ref[...] + jnp.sum(chosen, axis=1, keepdims=True)
        return rank

    def emit(rows, ids, picks, gates, rank):
        slot = lax.broadcasted_iota(I32, (8, tm), 0)
        gate_t = jnp.zeros((8, tm), F32)
        idx_t = jnp.zeros((8, tm), I32)
        rank_t = jnp.zeros((8, tm), F32)
        for k in range(TOP_K):
            gate_t = jnp.where(slot == k, gates[k], gate_t)
            idx_t = jnp.where(slot == k, ids[k], idx_t)
            rank_t = jnp.where(slot == k, jnp.sum(jnp.where(picks[k], rank, 0.0), axis=0, keepdims=True),
                               rank_t)
        idx_ref[:, rows] = idx_t
        rank_ref[:, rows] = rank_t.astype(I32)
        gate_rows = jnp.concatenate([gate_t, jnp.zeros((LANES - 8, tm), F32)], axis=0)
        gate_ref[rows, :] = sum(lax.dot_general(eye, part, nt, preferred_element_type=F32)
                                for part in pieces(gate_rows, 3))

    work = [logits(rows) for rows in subs]
    chosen = [top_k(w) for w in work]
    rank = [ranks(picks) for _, picks, _ in chosen]
    for rows, (ids, picks, gates), r in zip(subs, chosen, rank):
        emit(rows, ids, picks, gates, r)
    count_ref[...] = jnp.broadcast_to(carry_ref[...], count_ref.shape)


def _router(d_out, f_out, x, w_out, g_moe, w_router_t, b_router_t):
    n, d = x.shape
    tm = ROUTER_TILE
    tile = lambda width: pl.BlockSpec((tm, width), lambda i: (i, 0))
    rows = pl.BlockSpec((8, tm), lambda i: (0, i))
    full = lambda arr: pl.BlockSpec(arr.shape, lambda i: (0,) * arr.ndim)
    return pl.pallas_call(
        _router_kernel,
        grid=(n // tm,),
        in_specs=[tile(DIFF_WIDTH), tile(FOX_WIDTH), tile(d), full(w_out), full(g_moe),
                  full(w_router_t), full(b_router_t)],
        out_specs=[tile(d), tile(d // 2), tile(LANES), rows, rows,
                   pl.BlockSpec((N_EXPERTS, LANES), lambda i: (0, 0))],
        out_shape=[jax.ShapeDtypeStruct((n, d), F32), jax.ShapeDtypeStruct((n, d // 2), U32),
                   jax.ShapeDtypeStruct((n, LANES), F32), jax.ShapeDtypeStruct((8, n), I32),
                   jax.ShapeDtypeStruct((8, n), I32), jax.ShapeDtypeStruct((N_EXPERTS, LANES), F32)],
        scratch_shapes=[pltpu.VMEM((N_EXPERTS, 1), F32)],
        compiler_params=_params("arbitrary"),
        name="router",
    )(d_out, f_out, x, w_out, g_moe, w_router_t, b_router_t)


def _slot_kernel(starts_ref, idx_ref, rank_ref, pos_ref):
    idx = idx_ref[...]
    pos = rank_ref[...]
    for e in range(N_EXPERTS):
        pos = jnp.where(idx == e, pos + starts_ref[e], pos)
    pos_ref[...] = pos


def _slots(starts, idx, rank):
    rows, n = idx.shape
    width = min(n, 2048)
    block = pl.BlockSpec((rows, width), lambda i, starts: (0, i))
    return pl.pallas_call(
        _slot_kernel,
        grid_spec=pltpu.PrefetchScalarGridSpec(num_scalar_prefetch=1, grid=(n // width,),
                                               in_specs=[block, block], out_specs=block),
        out_shape=jax.ShapeDtypeStruct((rows, n), I32),
        compiler_params=_params("arbitrary"),
        name="slots",
    )(starts, idx, rank)


SC_WINDOW = 128
SC_COLUMNS = 256


def _dispatch_sparsecore(pos, pad_rows, xn, n_rows):
    n, d = xn.shape
    mesh = plsc.VectorSubcoreMesh(core_axis_name="core", subcore_axis_name="subcore")
    spread = dict(core_axis_name=("core", "subcore"), dimension_semantics=(pltpu.PARALLEL,))

    @functools.partial(pl.kernel, out_type=jax.ShapeDtypeStruct((n_rows, d), xn.dtype), mesh=mesh,
                       scratch_types=[], name="dispatch_sc")
    def scatter(x_hbm, pos_hbm, zero_hbm, pad_hbm, xs_hbm):
        for c in range(d // SC_COLUMNS):
            columns = xs_hbm.at[:, pl.ds(c * SC_COLUMNS, SC_COLUMNS)]

            def body(x_vmem, pos_vmem, columns=columns):
                for k in range(TOP_K):
                    pltpu.sync_copy(x_vmem, columns.at[pos_vmem.at[k]])

            pltpu.emit_pipeline(
                body, grid=(n // SC_WINDOW,),
                in_specs=[pl.BlockSpec((SC_WINDOW, SC_COLUMNS), index_map=lambda i, c=c: (i, c)),
                          pl.BlockSpec((TOP_K, SC_WINDOW), index_map=lambda i: (0, i))],
                out_specs=[], **spread)(x_hbm, pos_hbm)

            def pad_body(zero_vmem, pad_vmem, columns=columns):
                pltpu.sync_copy(zero_vmem, columns.at[pad_vmem.at[0]])

            pltpu.emit_pipeline(
                pad_body, grid=(pad_rows.size // SC_WINDOW,),
                in_specs=[pl.BlockSpec((SC_WINDOW, SC_COLUMNS), index_map=lambda i: (0, 0)),
                          pl.BlockSpec((1, SC_WINDOW), index_map=lambda i: (0, i))],
                out_specs=[], **spread)(zero_hbm, pad_hbm)

    return scatter(xn, pos, jnp.zeros((SC_WINDOW, SC_COLUMNS), xn.dtype), pad_rows.reshape(1, -1))


def _expert_kernel(tile_expert_ref, n_used_ref, next_expert_ref, slot_ref,
                   x_ref, w1_ref, b1_ref, w2_ref, b2_ref, y_ref,
                   w1f_ref, w2f_ref, sem_ref, w1b_ref, w2p_ref, w2b_ref):
    t = pl.program_id(0)
    fresh = (t == 0) | (tile_expert_ref[t] != tile_expert_ref[jnp.maximum(t - 1, 0)])
    used = t < n_used_ref[0]

    def weight_copies(e, s):
        return (pltpu.make_async_copy(w1_ref.at[e], w1f_ref.at[s], sem_ref.at[0, s]),
                pltpu.make_async_copy(w2_ref.at[e], w2f_ref.at[s], sem_ref.at[1, s]))

    @pl.when(t == 0)
    def _():
        for copy in weight_copies(tile_expert_ref[0], slot_ref[0]):
            copy.start()

    @pl.when(fresh & used)
    def _():
        s = slot_ref[t]
        for copy in weight_copies(tile_expert_ref[t], s):
            copy.wait()

        @pl.when(next_expert_ref[t] >= 0)
        def _():
            for copy in weight_copies(next_expert_ref[t], 1 - s):
                copy.start()

        rows = LANES
        half = LANES // 2
        for r in range(0, w1b_ref.shape[0], rows):
            w1b_ref[r:r + rows, :] = w1f_ref[s, r:r + rows, :].astype(BF16)
        for c in range(w2p_ref.shape[0]):
            lanes = slice(c * LANES, (c + 1) * LANES)
            for r in range(0, w2b_ref.shape[0], rows):
                for b in range(2):
                    w2p_ref[c, pl.ds(r + b, half, stride=2), :] = (
                        w2f_ref[s, r + b * half:r + (b + 1) * half, lanes])
            w2b_ref[:, lanes] = w2p_ref[c].astype(BF16)

    @pl.when(used)
    def _():
        xb = _unpack_rows(x_ref[...]).astype(BF16)
        h = jnp.dot(xb, w1b_ref[...], preferred_element_type=F32) + b1_ref[0]
        even = lax.broadcasted_iota(I32, (1, LANES), 1) % 2 == 0
        parts = []
        for q in range(h.shape[1] // (2 * LANES)):
            a = h[:, (2 * q) * LANES:(2 * q + 1) * LANES]
            b = h[:, (2 * q + 1) * LANES:(2 * q + 2) * LANES]
            hg = jnp.where(even, a, pltpu.roll(b, 1, 1))
            hl = jnp.where(even, pltpu.roll(a, LANES - 1, 1), b)
            glu = jnp.minimum(hg, SWIGLU_LIMIT)
            lin = jnp.clip(hl, -SWIGLU_LIMIT, SWIGLU_LIMIT)
            parts.append((glu * jax.nn.sigmoid(SWIGLU_ALPHA * glu) * (lin + 1.0)).astype(BF16))
        act = jnp.concatenate(parts, axis=1)
        y_ref[...] = _pack_rows(jnp.dot(act, w2b_ref[...], preferred_element_type=F32) + b2_ref[0])

    @pl.when(jnp.logical_not(used))
    def _():
        y_ref[...] = jnp.zeros_like(y_ref)


def _experts(tile_expert, n_used, next_expert, slot, xs, w1, b1, w2, b2):
    n_rows, d = xs.shape
    tm = EXPERT_TILE
    by_expert = lambda arr: pl.BlockSpec((1,) + arr.shape[1:], lambda i, te, *_: (te[i], 0, 0))
    rows = pl.BlockSpec((tm, d), lambda i, *_: (i, 0))
    used_rows = pl.BlockSpec((tm, d), lambda i, te, nu, *_: (jnp.minimum(i, nu[0] - 1), 0))
    hbm = pl.BlockSpec(memory_space=pl.ANY)
    return pl.pallas_call(
        _expert_kernel,
        grid_spec=pltpu.PrefetchScalarGridSpec(
            num_scalar_prefetch=4,
            grid=(n_rows // tm,),
            in_specs=[used_rows, hbm, by_expert(b1), hbm, by_expert(b2)],
            out_specs=rows,
            scratch_shapes=[pltpu.VMEM((2,) + w1.shape[1:], F32), pltpu.VMEM((2,) + w2.shape[1:], F32),
                            pltpu.SemaphoreType.DMA((2, 2)),
                            pltpu.VMEM(w1.shape[1:], BF16),
                            pltpu.VMEM((w2.shape[2] // LANES, w2.shape[1], LANES), F32),
                            pltpu.VMEM(w2.shape[1:], BF16)]),
        out_shape=jax.ShapeDtypeStruct(xs.shape, xs.dtype),
        compiler_params=pltpu.CompilerParams(dimension_semantics=("arbitrary",),
                                             vmem_limit_bytes=EXPERT_VMEM_LIMIT),
        name="experts",
    )(tile_expert, n_used, next_expert, slot, xs, w1, b1, w2, b2)


def _gather_sparsecore(pos, y):
    d = y.shape[1]
    total = pos.size
    mesh = plsc.VectorSubcoreMesh(core_axis_name="core", subcore_axis_name="subcore")

    @functools.partial(pl.kernel, out_type=jax.ShapeDtypeStruct((total, d), y.dtype), mesh=mesh,
                       scratch_types=[], name="gather_sc")
    def gather(y_hbm, pos_hbm, out_hbm):
        for c in range(d // SC_COLUMNS):
            def body(pos_vmem, out_vmem, c=c):
                pltpu.sync_copy(y_hbm.at[:, pl.ds(c * SC_COLUMNS, SC_COLUMNS)].at[pos_vmem.at[0]], out_vmem)

            pltpu.emit_pipeline(
                body, grid=(total // SC_WINDOW,),
                in_specs=[pl.BlockSpec((1, SC_WINDOW), index_map=lambda i: (0, i))],
                out_specs=[pl.BlockSpec((SC_WINDOW, SC_COLUMNS), index_map=lambda i, c=c: (i, c))],
                core_axis_name=("core", "subcore"),
                dimension_semantics=(pltpu.PARALLEL,))(pos_hbm, out_hbm)

    return gather(y, pos.reshape(1, -1))


def _combine_kernel(h_ref, gate_ref, p_ref, wg_ref, wp_ref, gple_ref, gfin_ref, *refs, final_norm):
    y_refs, o_ref = refs[:TOP_K], refs[TOP_K]
    ts = COMBINE_SUBTILE
    subs = [slice(i * ts, (i + 1) * ts) for i in range(h_ref.shape[0] // ts)]

    def mix(rows):
        gates = gate_ref[rows, :]
        moe = gates[:, 0:1] * _unpack_rows(y_refs[0][rows, :])
        for k in range(1, TOP_K):
            moe = moe + gates[:, k:k + 1] * _unpack_rows(y_refs[k][rows, :])
        h = h_ref[rows, :] + moe
        return h, jnp.dot(_rms(h, gple_ref[...]).astype(BF16), wg_ref[...], preferred_element_type=F32)

    embs = [jnp.dot(p_ref[rows, :].astype(BF16), wp_ref[...], preferred_element_type=F32) for rows in subs]
    mixed = [mix(rows) for rows in subs]
    for rows, (h, logit), emb in zip(subs, mixed, embs):
        h = h + jax.nn.sigmoid(logit) * emb
        o_ref[rows, :] = _rms(h, gfin_ref[...]) if final_norm else h


def _combine(h, gates, p, w_gate, w_proj, g_ple, g_final, y_tok, final_norm):
    n, d = h.shape
    tm = TOKEN_TILE
    tile = lambda width: pl.BlockSpec((tm, width), lambda i: (i, 0))
    full = lambda arr: pl.BlockSpec(arr.shape, lambda i: (0,) * arr.ndim)
    choice = lambda k: pl.BlockSpec((tm, y_tok.shape[1]), lambda i: (k * (n // tm) + i, 0))
    return pl.pallas_call(
        functools.partial(_combine_kernel, final_norm=final_norm),
        grid=(n // tm,),
        in_specs=[tile(d), tile(LANES), tile(p.shape[1]), full(w_gate), full(w_proj),
                  full(g_ple), full(g_final)] + [choice(k) for k in range(TOP_K)],
        out_specs=tile(d),
        out_shape=jax.ShapeDtypeStruct((n, d), F32),
        compiler_params=_params("arbitrary"),
        name="combine",
    )(h, gates, p, w_gate, w_proj, g_ple, g_final, *([y_tok] * TOP_K))


def _layer(h, p, positions, g_attn, w_in, lq1, lk1, lq2, lk2, g_subln, b_forget, w_out, g_moe,
           w_router, b_router, w_e1, b_e1, w_e2, b_e2, g_ple, w_ple_gate, w_ple_proj, g_final,
           lambda_init, final_norm):
    b, s, d = h.shape
    n = b * s
    row = lambda v: v.reshape(1, -1)

    group = lambda j: w_in[:, j * DIFF_WIDTH:(j + 1) * DIFF_WIDTH]
    w_main = jnp.concatenate([group(1), group(4)], axis=1).astype(BF16)
    w_main_t = jnp.concatenate([group(0), group(2), group(3), group(5)], axis=1).T.astype(BF16)
    pad_heads = ((0, BF16_TILE_ROWS - N_FOX_HEADS), (0, 0))
    w_fz_t = jnp.pad(w_in[:, MAIN_WIDTH:].T, pad_heads).astype(BF16)
    b_fz_t = jnp.pad(b_forget.reshape(-1, 1), pad_heads)
    b1 = b_e1[:, None, :]
    b2 = b_e2[:, None, :]

    dq_t, dk, dv_t, fq_t, fk, fv_t, c, c_t = _inproj(positions, h, row(g_attn), w_main, w_main_t, w_fz_t, b_fz_t)
    c_rows = c_t.reshape(b, N_FOX_HEADS // 2, 2, s)
    d_out, f_out = _attention(dq_t, dk, dv_t, fq_t, fk, fv_t, c_rows, c, row(lq1), row(lk1), row(lq2),
                              row(lk2), g_subln.reshape(-1, 1), lambda_init)

    h1, xn, gates, idx, rank, counts = _router(
        d_out.reshape(n, DIFF_WIDTH), f_out.reshape(n, FOX_WIDTH), h.reshape(n, d),
        w_out.astype(BF16), row(g_moe), w_router.T, b_router.reshape(-1, 1))

    te = EXPERT_TILE
    n_rows = n * TOP_K + N_EXPERTS * te
    counts = counts[:, 0].astype(I32)
    padded = (counts + te - 1) // te * te
    ends = jnp.cumsum(padded)
    starts = ends - padded
    pos = _slots(starts.astype(I32), idx, rank)[:TOP_K]
    tile_start = jnp.arange(n_rows // te, dtype=I32) * te
    tile_expert = jnp.minimum(jnp.sum((ends[None, :] <= tile_start[:, None]).astype(I32), axis=1),
                              N_EXPERTS - 1)
    n_used = (ends[-1:] // te).astype(I32)
    owns = padded > 0
    later = lax.cummin(jnp.where(owns, jnp.arange(N_EXPERTS, dtype=I32), N_EXPERTS), reverse=True)
    following = jnp.concatenate([later[1:], jnp.full((1,), N_EXPERTS, I32)])
    following = jnp.where(following < N_EXPERTS, following, -1)
    owner = tile_expert[:, None] == jnp.arange(N_EXPERTS, dtype=I32)[None, :]
    next_expert = jnp.sum(jnp.where(owner, following[None, :], 0), axis=1)
    slot = jnp.sum(jnp.where(owner, ((jnp.cumsum(owns.astype(I32)) - 1) % 2)[None, :], 0), axis=1)

    pad_count = padded - counts
    pad_total = jnp.sum(pad_count)
    pad_before = jnp.cumsum(pad_count) - pad_count
    entry = jnp.arange(N_EXPERTS * te, dtype=I32)
    in_or_before = entry[:, None] >= pad_before[None, :]
    expert_of = jnp.sum(in_or_before.astype(I32), axis=1) - 1
    mine = expert_of[:, None] == jnp.arange(N_EXPERTS, dtype=I32)[None, :]
    pad_rows = jnp.sum(jnp.where(mine, (starts + counts - pad_before)[None, :], 0), axis=1) + entry
    pad_rows = jnp.where(entry < pad_total, pad_rows, ends[-1] + entry - pad_total)
    xs = _dispatch_sparsecore(pos.astype(I32), pad_rows.astype(I32), xn, n_rows)
    y = _experts(tile_expert, n_used, next_expert.astype(I32), slot.astype(I32), xs, w_e1, b1, w_e2, b2)
    y_tok = _gather_sparsecore(pos.astype(I32), y)
    out = _combine(h1, gates, p.reshape(n, PLE_DIM), w_ple_gate.astype(BF16),
                   w_ple_proj.astype(BF16), row(g_ple), row(g_final), y_tok, final_norm)
    return out.reshape(b, s, d)


def kernel(x, p, positions, g_attn, w_in, lambda_q1, lambda_k1, lambda_q2, lambda_k2, g_subln, b_forget, w_out, g_moe, w_router, b_router, w_e1, b_e1, w_e2, b_e2, g_ple, w_ple_gate, w_ple_proj, g_final):
    depth = g_attn.shape[0]
    h = x
    for i in range(depth):
        lambda_init = 0.8 - 0.6 * math.exp(-0.3 * i)
        h = _layer(h, p[i], positions, g_attn[i], w_in[i], lambda_q1[i], lambda_k1[i],
                   lambda_q2[i], lambda_k2[i], g_subln[i], b_forget[i], w_out[i], g_moe[i],
                   w_router[i], b_router[i], w_e1[i], b_e1[i], w_e2[i], b_e2[i], g_ple[i],
                   w_ple_gate[i], w_ple_proj[i], g_final, lambda_init, i == depth - 1)
    return h
```
